```python
import jax, jax.numpy as jnp
from jax import lax
import numpy as np

D_MODEL = 1024
BATCH = 8
SEQ = 4096
DEPTH = 4

DN_HEADS = 8
DN_HEAD_DIM = 128
DN_WIDTH = DN_HEADS * DN_HEAD_DIM
DN_CONV = 4
DN_CHUNK = 64
SWA_Q_HEADS = 16
SWA_KV_HEADS = 2
SWA_HEAD_DIM = 64
SWA_GROUP = SWA_Q_HEADS // SWA_KV_HEADS
SWA_WIDTH = SWA_Q_HEADS * SWA_HEAD_DIM
SWA_KV_WIDTH = SWA_KV_HEADS * SWA_HEAD_DIM
WINDOW = 128
SWA_BLOCK = 128
ROPE_THETA = 500000.0
ROPE_DIM = SWA_HEAD_DIM // 4
D_FF = 2816
FFN_CONV = 3
EPS = 1e-6
IN_SIZES = (3 * DN_WIDTH, DN_WIDTH, DN_HEADS, DN_HEADS, SWA_WIDTH, SWA_KV_WIDTH, SWA_KV_WIDTH, D_MODEL, D_MODEL)
IN_TOTAL = 4 * DN_WIDTH + 2 * DN_HEADS + SWA_WIDTH + 2 * SWA_KV_WIDTH + 2 * D_MODEL

kernel_name = "hybrid_gdn_swa_sink_convffn_adaln"


def _split_columns(t, sizes):
    idx, acc = [], 0
    for s in sizes[:-1]:
        acc += s
        idx.append(acc)
    return jnp.split(t, idx, axis=-1)


def rms_norm(x, w):
    xf = x.astype(jnp.float32)
    y = xf * lax.rsqrt(jnp.mean(xf * xf, axis=-1, keepdims=True) + EPS)
    return (y * w.astype(jnp.float32)).astype(x.dtype)


def l2_norm(x):
    xf = x.astype(jnp.float32)
    return (xf * lax.rsqrt(jnp.sum(xf * xf, axis=-1, keepdims=True) + EPS)).astype(x.dtype)


def causal_dwconv(x, w):
    K, C = w.shape
    return lax.conv_general_dilated(
        x, w[:, None, :].astype(x.dtype), window_strides=(1,), padding=[(K - 1, 0)],
        dimension_numbers=('NWC', 'WIO', 'NWC'), feature_group_count=C)


def partial_rope(x, pos):
    half = ROPE_DIM // 2
    inv = jnp.power(ROPE_THETA, -jnp.arange(half, dtype=jnp.float32) / half)
    ang = pos.astype(jnp.float32)[..., None] * inv
    cos, sin = jnp.cos(ang)[:, :, None, :], jnp.sin(ang)[:, :, None, :]
    xr = x[..., :ROPE_DIM].astype(jnp.float32)
    x1, x2 = xr[..., :half], xr[..., half:]
    rot = jnp.concatenate([x1 * cos - x2 * sin, x2 * cos + x1 * sin], axis=-1).astype(x.dtype)
    return jnp.concatenate([rot, x[..., ROPE_DIM:]], axis=-1)


def gated_delta_rule_chunked(q, k, v, g, beta):
    B, T, H, Dk = q.shape
    Dv = v.shape[-1]
    C = DN_CHUNK
    N = T // C
    f32 = jnp.float32

    def chunks(t):
        return t.astype(f32).reshape(B, N, C, H, -1).transpose(0, 3, 1, 2, 4)

    qc = chunks(q) * (Dk ** -0.5)
    kc, vc = chunks(k), chunks(v)
    gc = jnp.cumsum(g.astype(f32).reshape(B, N, C, H).transpose(0, 3, 1, 2), axis=-1)
    bc = beta.astype(f32).reshape(B, N, C, H).transpose(0, 3, 1, 2)
    tri_incl = jnp.tril(jnp.ones((C, C), dtype=bool))
    tri_strict = jnp.tril(jnp.ones((C, C), dtype=bool), -1)
    decay = jnp.exp(jnp.where(tri_incl, gc[..., :, None] - gc[..., None, :], -jnp.inf))
    k_beta = kc * bc[..., None]
    L = jnp.where(tri_strict, jnp.einsum('bhnid,bhnjd->bhnij', k_beta, kc) * decay, 0.0)
    A = L + jnp.eye(C, dtype=f32)
    rhs = jnp.concatenate([vc * bc[..., None], k_beta * jnp.exp(gc)[..., None]], axis=-1)
    sol = lax.linalg.triangular_solve(A, rhs, left_side=True, lower=True, unit_diagonal=True)
    u, w = sol[..., :Dv], sol[..., Dv:]
    qk = jnp.einsum('bhnid,bhnjd->bhnij', qc, kc) * decay
    q_dec = qc * jnp.exp(gc)[..., None]
    g_last = gc[..., -1]
    k_dec = kc * jnp.exp(g_last[..., None] - gc)[..., None]

    def step(S, xs):
        qd, kd, u_i, w_i, qk_i, gl = xs
        v_new = u_i - jnp.einsum('bhcd,bhde->bhce', w_i, S)
        o = jnp.einsum('bhcd,bhde->bhce', qd, S) + jnp.einsum('bhij,bhje->bhie', qk_i, v_new)
        S = S * jnp.exp(gl)[..., None, None] + jnp.einsum('bhcd,bhce->bhde', kd, v_new)
        return S, o

    xs = tuple(jnp.moveaxis(t, 2, 0) for t in (q_dec, k_dec, u, w, qk, g_last))
    S0 = jnp.zeros((B, H, Dk, Dv), f32)
    _, o = lax.scan(step, S0, xs)
    return o.transpose(1, 0, 3, 2, 4).reshape(B, T, H, Dv).astype(v.dtype)


def swa_sink_attention(q, k, v, sinks):
    B, T, Hq, D = q.shape
    nb = T // SWA_BLOCK
    f32 = jnp.float32
    qb = q.astype(f32).reshape(B, nb, SWA_BLOCK, SWA_KV_HEADS, SWA_GROUP, D)

    def band(t):
        tb = t.astype(f32).reshape(B, nb, SWA_BLOCK, SWA_KV_HEADS, D)
        prev = jnp.pad(tb, ((0, 0), (1, 0), (0, 0), (0, 0), (0, 0)))[:, :-1]
        return jnp.concatenate([prev, tb], axis=2)

    kb, vb = band(k), band(v)
    s = jnp.einsum('bnqhgd,bnkhd->bnhgqk', qb, kb) * (D ** -0.5)
    qi = jnp.arange(SWA_BLOCK)[:, None]
    kj = jnp.arange(2 * SWA_BLOCK)[None, :]
    rel = qi + SWA_BLOCK - kj
    blk = jnp.arange(nb)[:, None, None]
    mask = (rel >= 0) & (rel < WINDOW) & (blk * SWA_BLOCK + kj >= SWA_BLOCK)
    s = jnp.where(mask[None, :, None, None], s, -jnp.inf)
    sink = sinks.astype(f32).reshape(1, 1, SWA_KV_HEADS, SWA_GROUP, 1, 1)
    m = jnp.maximum(jnp.max(s, axis=-1, keepdims=True), sink)
    p = jnp.exp(s - m)
    denom = jnp.sum(p, axis=-1, keepdims=True) + jnp.exp(sink - m)
    o = jnp.einsum('bnhgqk,bnkhd->bnqhgd', p / denom, vb)
    return o.reshape(B, T, Hq * D).astype(q.dtype)


def _fwd_setup_inputs(seed: int = 0) -> dict:
    key = jax.random.key(seed)
    ks = jax.random.split(key, 24)
    nrm = jax.random.normal
    f32 = jnp.float32
    Lr = DEPTH
    x = nrm(ks[0], (BATCH, SEQ, D_MODEL), f32)
    c = nrm(ks[1], (BATCH, D_MODEL), f32)
    positions = (jax.random.randint(ks[2], (BATCH, 1), 0, 2048, dtype=jnp.int32)
                 + jnp.arange(SEQ, dtype=jnp.int32)[None, :])
    w_ada = nrm(ks[3], (Lr, D_MODEL, 6 * D_MODEL), f32) * D_MODEL ** -0.5
    b_ada = nrm(ks[4], (Lr, 6 * D_MODEL), f32) * 0.02
    norm_mix = 1.0 + 0.02 * nrm(ks[5], (Lr, D_MODEL), f32)
    w_in = nrm(ks[6], (Lr, D_MODEL, IN_TOTAL), f32) * D_MODEL ** -0.5
    dn_conv = nrm(ks[7], (Lr, DN_CONV, 3 * DN_WIDTH), f32) * DN_CONV ** -0.5
    dn_a_log = jnp.log(jax.random.uniform(ks[8], (Lr, DN_HEADS), f32, 1.0, 16.0))
    dt = jnp.exp(jax.random.uniform(ks[9], (Lr, DN_HEADS), f32, np.log(1e-3), np.log(1e-1)))
    dn_dt_bias = jnp.log(jnp.expm1(dt))
    dn_norm = 1.0 + 0.02 * nrm(ks[10], (Lr, DN_HEAD_DIM), f32)
    w_dn_out = nrm(ks[11], (Lr, DN_WIDTH, D_MODEL), f32) * DN_WIDTH ** -0.5
    swa_q_norm = 1.0 + 0.02 * nrm(ks[12], (Lr, SWA_HEAD_DIM), f32)
    swa_k_norm = 1.0 + 0.02 * nrm(ks[13], (Lr, SWA_HEAD_DIM), f32)
    swa_sinks = nrm(ks[14], (Lr, SWA_Q_HEADS), f32)
    w_swa_out = nrm(ks[15], (Lr, SWA_WIDTH, D_MODEL), f32) * SWA_WIDTH ** -0.5
    w_o = nrm(ks[16], (Lr, D_MODEL, D_MODEL), f32) * D_MODEL ** -0.5
    norm_ffn = 1.0 + 0.02 * nrm(ks[17], (Lr, D_MODEL), f32)
    w_up = nrm(ks[18], (Lr, D_MODEL, 2 * D_FF), f32) * D_MODEL ** -0.5
    ffn_conv = nrm(ks[19], (Lr, FFN_CONV, D_FF), f32) * FFN_CONV ** -0.5
    ffn_conv_b = nrm(ks[20], (Lr, D_FF), f32) * 0.02
    w_down = nrm(ks[21], (Lr, D_FF, D_MODEL), f32) * D_FF ** -0.5
    return {"x": x, "c": c, "positions": positions, "w_ada": w_ada, "b_ada": b_ada,
            "norm_mix": norm_mix, "w_in": w_in, "dn_conv": dn_conv, "dn_a_log": dn_a_log,
            "dn_dt_bias": dn_dt_bias, "dn_norm": dn_norm, "w_dn_out": w_dn_out,
            "swa_q_norm": swa_q_norm, "swa_k_norm": swa_k_norm, "swa_sinks": swa_sinks,
            "w_swa_out": w_swa_out, "w_o": w_o, "norm_ffn": norm_ffn, "w_up": w_up,
            "ffn_conv": ffn_conv, "ffn_conv_b": ffn_conv_b, "w_down": w_down}


def _fwd_reference(x, c, positions, w_ada, b_ada, norm_mix, w_in, dn_conv, dn_a_log, dn_dt_bias,
              dn_norm, w_dn_out, swa_q_norm, swa_k_norm, swa_sinks, w_swa_out, w_o,
              norm_ffn, w_up, ffn_conv, ffn_conv_b, w_down):
    B, T, _ = x.shape
    c_act = jax.nn.silu(c)
    for l in range(DEPTH):
        mod = c_act @ w_ada[l] + b_ada[l]
        sh1, sc1, gt1, sh2, sc2, gt2 = [m[:, None, :] for m in jnp.split(mod, 6, axis=-1)]

        h = rms_norm(x, norm_mix[l]) * (1.0 + sc1) + sh1
        proj = h @ w_in[l]
        dn_qkv, dn_z, dn_a, dn_b, sw_q, sw_k, sw_v, gate_a, gate_b = _split_columns(proj, IN_SIZES)

        dn_qkv = jax.nn.silu(causal_dwconv(dn_qkv, dn_conv[l]))
        dq, dk, dv = [t.reshape(B, T, DN_HEADS, DN_HEAD_DIM) for t in jnp.split(dn_qkv, 3, axis=-1)]
        dq, dk = l2_norm(dq), l2_norm(dk)
        g = -jnp.exp(dn_a_log[l].astype(jnp.float32)) * jax.nn.softplus(
            dn_a.astype(jnp.float32) + dn_dt_bias[l].astype(jnp.float32))
        beta = jax.nn.sigmoid(dn_b.astype(jnp.float32))
        o_dn = gated_delta_rule_chunked(dq, dk, dv, g, beta)
        o_dn = rms_norm(o_dn, dn_norm[l]) * jax.nn.silu(dn_z.reshape(B, T, DN_HEADS, DN_HEAD_DIM))
        y_a = o_dn.reshape(B, T, DN_WIDTH) @ w_dn_out[l]

        sq = partial_rope(rms_norm(sw_q.reshape(B, T, SWA_Q_HEADS, SWA_HEAD_DIM), swa_q_norm[l]), positions)
        sk = partial_rope(rms_norm(sw_k.reshape(B, T, SWA_KV_HEADS, SWA_HEAD_DIM), swa_k_norm[l]), positions)
        sv = sw_v.reshape(B, T, SWA_KV_HEADS, SWA_HEAD_DIM)
        y_b = swa_sink_attention(sq, sk, sv, swa_sinks[l]) @ w_swa_out[l]

        merged = jax.nn.sigmoid(gate_a) * y_a + jax.nn.sigmoid(gate_b) * y_b
        x = x + gt1 * (merged @ w_o[l])

        h = rms_norm(x, norm_ffn[l]) * (1.0 + sc2) + sh2
        up_act, up_lin = jnp.split(h @ w_up[l], 2, axis=-1)
        up_act = causal_dwconv(up_act, ffn_conv[l]) + ffn_conv_b[l]
        x = x + gt2 * ((jax.nn.silu(up_act) * up_lin) @ w_down[l])
    return x


import jax as _jax
import jax.numpy as _jnp

TWIN_FORMAT = 'train_step'
FWD_PARAMS = ['x', 'c', 'positions', 'w_ada', 'b_ada', 'norm_mix', 'w_in', 'dn_conv', 'dn_a_log', 'dn_dt_bias', 'dn_norm', 'w_dn_out', 'swa_q_norm', 'swa_k_norm', 'swa_sinks', 'w_swa_out', 'w_o', 'norm_ffn', 'w_up', 'ffn_conv', 'ffn_conv_b', 'w_down']
TWIN_WEIGHTS = ['w_ada', 'b_ada', 'norm_mix', 'w_in', 'dn_conv', 'dn_a_log', 'dn_dt_bias', 'dn_norm', 'w_dn_out', 'swa_q_norm', 'swa_k_norm', 'swa_sinks', 'w_swa_out', 'w_o', 'norm_ffn', 'w_up', 'ffn_conv', 'ffn_conv_b', 'w_down']
TWIN_DIFF_INPUT = 'x'
TWIN_INPUTS = ['x', 'c', 'positions', 'w_ada', 'b_ada', 'norm_mix', 'w_in', 'dn_conv', 'dn_a_log', 'dn_dt_bias', 'dn_norm', 'w_dn_out', 'swa_q_norm', 'swa_k_norm', 'swa_sinks', 'w_swa_out', 'w_o', 'norm_ffn', 'w_up', 'ffn_conv', 'ffn_conv_b', 'w_down', 'loss_target', 'm_w_ada', 'm_b_ada', 'm_norm_mix', 'm_w_in', 'm_dn_conv', 'm_dn_a_log', 'm_dn_dt_bias', 'm_dn_norm', 'm_w_dn_out', 'm_swa_q_norm', 'm_swa_k_norm', 'm_swa_sinks', 'm_w_swa_out', 'm_w_o', 'm_norm_ffn', 'm_w_up', 'm_ffn_conv', 'm_ffn_conv_b', 'm_w_down', 'v_w_ada', 'v_b_ada', 'v_norm_mix', 'v_w_in', 'v_dn_conv', 'v_dn_a_log', 'v_dn_dt_bias', 'v_dn_norm', 'v_w_dn_out', 'v_swa_q_norm', 'v_swa_k_norm', 'v_swa_sinks', 'v_w_swa_out', 'v_w_o', 'v_norm_ffn', 'v_w_up', 'v_ffn_conv', 'v_ffn_conv_b', 'v_w_down']
TWIN_OUTPUTS = ['loss', 'grad_x', 'grad_w_ada', 'grad_b_ada', 'grad_norm_mix', 'grad_w_in', 'grad_dn_conv', 'grad_dn_a_log', 'grad_dn_dt_bias', 'grad_dn_norm', 'grad_w_dn_out', 'grad_swa_q_norm', 'grad_swa_k_norm', 'grad_swa_sinks', 'grad_w_swa_out', 'grad_w_o', 'grad_norm_ffn', 'grad_w_up', 'grad_ffn_conv', 'grad_ffn_conv_b', 'grad_w_down', 'delta_w_ada', 'delta_b_ada', 'delta_norm_mix', 'delta_w_in', 'delta_dn_conv', 'delta_dn_a_log', 'delta_dn_dt_bias', 'delta_dn_norm', 'delta_w_dn_out', 'delta_swa_q_norm', 'delta_swa_k_norm', 'delta_swa_sinks', 'delta_w_swa_out', 'delta_w_o', 'delta_norm_ffn', 'delta_w_up', 'delta_ffn_conv', 'delta_ffn_conv_b', 'delta_w_down', 'new_m_w_ada', 'new_m_b_ada', 'new_m_norm_mix', 'new_m_w_in', 'new_m_dn_conv', 'new_m_dn_a_log', 'new_m_dn_dt_bias', 'new_m_dn_norm', 'new_m_w_dn_out', 'new_m_swa_q_norm', 'new_m_swa_k_norm', 'new_m_swa_sinks', 'new_m_w_swa_out', 'new_m_w_o', 'new_m_norm_ffn', 'new_m_w_up', 'new_m_ffn_conv', 'new_m_ffn_conv_b', 'new_m_w_down', 'new_v_w_ada', 'new_v_b_ada', 'new_v_norm_mix', 'new_v_w_in', 'new_v_dn_conv', 'new_v_dn_a_log', 'new_v_dn_dt_bias', 'new_v_dn_norm', 'new_v_w_dn_out', 'new_v_swa_q_norm', 'new_v_swa_k_norm', 'new_v_swa_sinks', 'new_v_w_swa_out', 'new_v_w_o', 'new_v_norm_ffn', 'new_v_w_up', 'new_v_ffn_conv', 'new_v_ffn_conv_b', 'new_v_w_down']
TWIN_LEAF_KINDS = {'loss': 'loss', 'grad_x': 'grad_x', 'grad_w_ada': 'grad_w', 'grad_b_ada': 'grad_w', 'grad_norm_mix': 'grad_w', 'grad_w_in': 'grad_w', 'grad_dn_conv': 'grad_w', 'grad_dn_a_log': 'grad_w', 'grad_dn_dt_bias': 'grad_w', 'grad_dn_norm': 'grad_w', 'grad_w_dn_out': 'grad_w', 'grad_swa_q_norm': 'grad_w', 'grad_swa_k_norm': 'grad_w', 'grad_swa_sinks': 'grad_w', 'grad_w_swa_out': 'grad_w', 'grad_w_o': 'grad_w', 'grad_norm_ffn': 'grad_w', 'grad_w_up': 'grad_w', 'grad_ffn_conv': 'grad_w', 'grad_ffn_conv_b': 'grad_w', 'grad_w_down': 'grad_w', 'delta_w_ada': 'delta_w', 'delta_b_ada': 'delta_w', 'delta_norm_mix': 'delta_w', 'delta_w_in': 'delta_w', 'delta_dn_conv': 'delta_w', 'delta_dn_a_log': 'delta_w', 'delta_dn_dt_bias': 'delta_w', 'delta_dn_norm': 'delta_w', 'delta_w_dn_out': 'delta_w', 'delta_swa_q_norm': 'delta_w', 'delta_swa_k_norm': 'delta_w', 'delta_swa_sinks': 'delta_w', 'delta_w_swa_out': 'delta_w', 'delta_w_o': 'delta_w', 'delta_norm_ffn': 'delta_w', 'delta_w_up': 'delta_w', 'delta_ffn_conv': 'delta_w', 'delta_ffn_conv_b': 'delta_w', 'delta_w_down': 'delta_w', 'new_m_w_ada': 'new_m', 'new_m_b_ada': 'new_m', 'new_m_norm_mix': 'new_m', 'new_m_w_in': 'new_m', 'new_m_dn_conv': 'new_m', 'new_m_dn_a_log': 'new_m', 'new_m_dn_dt_bias': 'new_m', 'new_m_dn_norm': 'new_m', 'new_m_w_dn_out': 'new_m', 'new_m_swa_q_norm': 'new_m', 'new_m_swa_k_norm': 'new_m', 'new_m_swa_sinks': 'new_m', 'new_m_w_swa_out': 'new_m', 'new_m_w_o': 'new_m', 'new_m_norm_ffn': 'new_m', 'new_m_w_up': 'new_m', 'new_m_ffn_conv': 'new_m', 'new_m_ffn_conv_b': 'new_m', 'new_m_w_down': 'new_m', 'new_v_w_ada': 'new_v', 'new_v_b_ada': 'new_v', 'new_v_norm_mix': 'new_v', 'new_v_w_in': 'new_v', 'new_v_dn_conv': 'new_v', 'new_v_dn_a_log': 'new_v', 'new_v_dn_dt_bias': 'new_v', 'new_v_dn_norm': 'new_v', 'new_v_w_dn_out': 'new_v', 'new_v_swa_q_norm': 'new_v', 'new_v_swa_k_norm': 'new_v', 'new_v_swa_sinks': 'new_v', 'new_v_w_swa_out': 'new_v', 'new_v_w_o': 'new_v', 'new_v_norm_ffn': 'new_v', 'new_v_w_up': 'new_v', 'new_v_ffn_conv': 'new_v', 'new_v_ffn_conv_b': 'new_v', 'new_v_w_down': 'new_v'}


def _forward(args):
    return _fwd_reference(*[args[k] for k in FWD_PARAMS])


def _output_shape():
    def fwd():
        inp = _fwd_setup_inputs(0)
        return _fwd_reference(*[inp[k] for k in FWD_PARAMS])
    out = _jax.eval_shape(fwd)
    return out.shape, out.dtype

N_MICROBATCH = 1
ADAM_LR = 0.001
ADAM_B1 = 0.9
ADAM_B2 = 0.999
ADAM_EPS = 1e-08
ADAM_WD = 0.01
ADAM_STEP = 10
PER_EXAMPLE_BATCH_AXIS = {'x': 0, 'c': 0, 'positions': 0, 'loss_target': 0}
SHARED_INPUTS = []
_WEIGHT_DTYPES = {'w_ada': _jnp.float32, 'b_ada': _jnp.float32, 'norm_mix': _jnp.float32, 'w_in': _jnp.float32, 'dn_conv': _jnp.float32, 'dn_a_log': _jnp.float32, 'dn_dt_bias': _jnp.float32, 'dn_norm': _jnp.float32, 'w_dn_out': _jnp.float32, 'swa_q_norm': _jnp.float32, 'swa_k_norm': _jnp.float32, 'swa_sinks': _jnp.float32, 'w_swa_out': _jnp.float32, 'w_o': _jnp.float32, 'norm_ffn': _jnp.float32, 'w_up': _jnp.float32, 'ffn_conv': _jnp.float32, 'ffn_conv_b': _jnp.float32, 'w_down': _jnp.float32}
MOMENT_SCALE = {'w_ada': 4.973945e+00, 'b_ada': 1.167651e+01, 'norm_mix': 3.906567e+00, 'w_in': 1.669423e+00, 'dn_conv': 1.059183e+00, 'dn_a_log': 4.213668e+00, 'dn_dt_bias': 3.940849e+00, 'dn_norm': 2.296675e+01, 'w_dn_out': 1.692214e+00, 'swa_q_norm': 8.559328e-01, 'swa_k_norm': 8.592008e-01, 'swa_sinks': 1.125086e+00, 'w_swa_out': 3.227513e+00, 'w_o': 3.522725e+00, 'norm_ffn': 2.740409e+01, 'w_up': 2.310594e+00, 'ffn_conv': 4.756749e+00, 'ffn_conv_b': 3.985742e+00, 'w_down': 2.776464e+00}


def _to_microbatches(a, axis):
    t = _jnp.moveaxis(a, axis, 0)
    t = t.reshape((N_MICROBATCH, t.shape[0] // N_MICROBATCH) + t.shape[1:])
    return _jnp.moveaxis(t, 1, axis + 1)


def setup_inputs(seed: int = 0) -> dict:
    inp = _fwd_setup_inputs(seed)
    key = _jax.random.fold_in(_jax.random.key(seed), 7919)
    shape, _ = _output_shape()
    out = dict(inp)
    out["loss_target"] = _jax.random.normal(_jax.random.fold_in(key, 0), shape, _jnp.float32)
    for i, name in enumerate(TWIN_WEIGHTS):
        w = inp[name].astype(_jnp.float32)
        if MOMENT_SCALE is None:
            s = _jnp.sqrt(_jnp.mean(_jnp.square(w)) + 1e-30)
        else:
            s = MOMENT_SCALE[name]
        km, kv = _jax.random.split(_jax.random.fold_in(key, i + 1))
        out[name] = w
        out["m_" + name] = s * _jax.random.normal(km, w.shape, _jnp.float32)
        out["v_" + name] = (s * s) * _jax.random.uniform(kv, w.shape, _jnp.float32, 0.5, 1.5)
    if N_MICROBATCH > 1:
        for name, axis in PER_EXAMPLE_BATCH_AXIS.items():
            out[name] = _to_microbatches(out[name], axis)
    return {'x': out['x'], 'c': out['c'], 'positions': out['positions'], 'w_ada': out['w_ada'], 'b_ada': out['b_ada'], 'norm_mix': out['norm_mix'], 'w_in': out['w_in'], 'dn_conv': out['dn_conv'], 'dn_a_log': out['dn_a_log'], 'dn_dt_bias': out['dn_dt_bias'], 'dn_norm': out['dn_norm'], 'w_dn_out': out['w_dn_out'], 'swa_q_norm': out['swa_q_norm'], 'swa_k_norm': out['swa_k_norm'], 'swa_sinks': out['swa_sinks'], 'w_swa_out': out['w_swa_out'], 'w_o': out['w_o'], 'norm_ffn': out['norm_ffn'], 'w_up': out['w_up'], 'ffn_conv': out['ffn_conv'], 'ffn_conv_b': out['ffn_conv_b'], 'w_down': out['w_down'], 'loss_target': out['loss_target'], 'm_w_ada': out['m_w_ada'], 'm_b_ada': out['m_b_ada'], 'm_norm_mix': out['m_norm_mix'], 'm_w_in': out['m_w_in'], 'm_dn_conv': out['m_dn_conv'], 'm_dn_a_log': out['m_dn_a_log'], 'm_dn_dt_bias': out['m_dn_dt_bias'], 'm_dn_norm': out['m_dn_norm'], 'm_w_dn_out': out['m_w_dn_out'], 'm_swa_q_norm': out['m_swa_q_norm'], 'm_swa_k_norm': out['m_swa_k_norm'], 'm_swa_sinks': out['m_swa_sinks'], 'm_w_swa_out': out['m_w_swa_out'], 'm_w_o': out['m_w_o'], 'm_norm_ffn': out['m_norm_ffn'], 'm_w_up': out['m_w_up'], 'm_ffn_conv': out['m_ffn_conv'], 'm_ffn_conv_b': out['m_ffn_conv_b'], 'm_w_down': out['m_w_down'], 'v_w_ada': out['v_w_ada'], 'v_b_ada': out['v_b_ada'], 'v_norm_mix': out['v_norm_mix'], 'v_w_in': out['v_w_in'], 'v_dn_conv': out['v_dn_conv'], 'v_dn_a_log': out['v_dn_a_log'], 'v_dn_dt_bias': out['v_dn_dt_bias'], 'v_dn_norm': out['v_dn_norm'], 'v_w_dn_out': out['v_w_dn_out'], 'v_swa_q_norm': out['v_swa_q_norm'], 'v_swa_k_norm': out['v_swa_k_norm'], 'v_swa_sinks': out['v_swa_sinks'], 'v_w_swa_out': out['v_w_swa_out'], 'v_w_o': out['v_w_o'], 'v_norm_ffn': out['v_norm_ffn'], 'v_w_up': out['v_w_up'], 'v_ffn_conv': out['v_ffn_conv'], 'v_ffn_conv_b': out['v_ffn_conv_b'], 'v_w_down': out['v_w_down']}


def _loss(weights, diff, rest, loss_target):
    with _jax.named_scope("forward"):
        args = {**rest, TWIN_DIFF_INPUT: diff, **{k: w.astype(_WEIGHT_DTYPES[k]) for k, w in weights.items()}}
        y = _forward(args)
    with _jax.named_scope("loss_head"):
        err = _jnp.square(y.astype(_jnp.float32) - loss_target)
        return 0.5 * _jnp.sum(_jnp.mean(err, axis=-1)) if err.ndim else 0.5 * err


def _adamw(w, g, m, v):
    m = ADAM_B1 * m + (1.0 - ADAM_B1) * g
    v = ADAM_B2 * v + (1.0 - ADAM_B2) * _jnp.square(g)
    m_hat = m / (1.0 - ADAM_B1 ** ADAM_STEP)
    v_hat = v / (1.0 - ADAM_B2 ** ADAM_STEP)
    delta = -ADAM_LR * (m_hat / (_jnp.sqrt(v_hat) + ADAM_EPS) + ADAM_WD * w)
    return delta, m, v


def reference(x, c, positions, w_ada, b_ada, norm_mix, w_in, dn_conv, dn_a_log, dn_dt_bias, dn_norm, w_dn_out, swa_q_norm, swa_k_norm, swa_sinks, w_swa_out, w_o, norm_ffn, w_up, ffn_conv, ffn_conv_b, w_down, loss_target, m_w_ada, m_b_ada, m_norm_mix, m_w_in, m_dn_conv, m_dn_a_log, m_dn_dt_bias, m_dn_norm, m_w_dn_out, m_swa_q_norm, m_swa_k_norm, m_swa_sinks, m_w_swa_out, m_w_o, m_norm_ffn, m_w_up, m_ffn_conv, m_ffn_conv_b, m_w_down, v_w_ada, v_b_ada, v_norm_mix, v_w_in, v_dn_conv, v_dn_a_log, v_dn_dt_bias, v_dn_norm, v_w_dn_out, v_swa_q_norm, v_swa_k_norm, v_swa_sinks, v_w_swa_out, v_w_o, v_norm_ffn, v_w_up, v_ffn_conv, v_ffn_conv_b, v_w_down):
    given = dict(x=x, c=c, positions=positions, w_ada=w_ada, b_ada=b_ada, norm_mix=norm_mix, w_in=w_in, dn_conv=dn_conv, dn_a_log=dn_a_log, dn_dt_bias=dn_dt_bias, dn_norm=dn_norm, w_dn_out=w_dn_out, swa_q_norm=swa_q_norm, swa_k_norm=swa_k_norm, swa_sinks=swa_sinks, w_swa_out=w_swa_out, w_o=w_o, norm_ffn=norm_ffn, w_up=w_up, ffn_conv=ffn_conv, ffn_conv_b=ffn_conv_b, w_down=w_down, loss_target=loss_target, m_w_ada=m_w_ada, m_b_ada=m_b_ada, m_norm_mix=m_norm_mix, m_w_in=m_w_in, m_dn_conv=m_dn_conv, m_dn_a_log=m_dn_a_log, m_dn_dt_bias=m_dn_dt_bias, m_dn_norm=m_dn_norm, m_w_dn_out=m_w_dn_out, m_swa_q_norm=m_swa_q_norm, m_swa_k_norm=m_swa_k_norm, m_swa_sinks=m_swa_sinks, m_w_swa_out=m_w_swa_out, m_w_o=m_w_o, m_norm_ffn=m_norm_ffn, m_w_up=m_w_up, m_ffn_conv=m_ffn_conv, m_ffn_conv_b=m_ffn_conv_b, m_w_down=m_w_down, v_w_ada=v_w_ada, v_b_ada=v_b_ada, v_norm_mix=v_norm_mix, v_w_in=v_w_in, v_dn_conv=v_dn_conv, v_dn_a_log=v_dn_a_log, v_dn_dt_bias=v_dn_dt_bias, v_dn_norm=v_dn_norm, v_w_dn_out=v_w_dn_out, v_swa_q_norm=v_swa_q_norm, v_swa_k_norm=v_swa_k_norm, v_swa_sinks=v_swa_sinks, v_w_swa_out=v_w_swa_out, v_w_o=v_w_o, v_norm_ffn=v_norm_ffn, v_w_up=v_w_up, v_ffn_conv=v_ffn_conv, v_ffn_conv_b=v_ffn_conv_b, v_w_down=v_w_down)
    weights = {n: given[n] for n in TWIN_WEIGHTS}
    shared = {n: given[n] for n in SHARED_INPUTS}
    per_example = {n: given[n] for n in ['x', 'c', 'positions']}
    grad_fn = _jax.value_and_grad(_loss, argnums=(0, 1))

    def one_microbatch(ex, loss_target):
        ex = dict(ex)
        diff = ex.pop(TWIN_DIFF_INPUT)
        return grad_fn(weights, diff, {**shared, **ex}, loss_target)

    if N_MICROBATCH == 1:
        loss, (grad_w, grad_x) = one_microbatch(per_example, given["loss_target"])
    else:
        def body(carry, xs):
            loss_sum, grad_sum = carry
            l_k, (gw_k, gx_k) = one_microbatch(xs[0], xs[1])
            with _jax.named_scope("update"):
                return (loss_sum + l_k, _jax.tree.map(_jnp.add, grad_sum, gw_k)), gx_k

        init = (_jnp.zeros((), _jnp.float32), _jax.tree.map(_jnp.zeros_like, weights))
        (loss, grad_w), grad_x = _jax.lax.scan(body, init, (per_example, given["loss_target"]))
    with _jax.named_scope("update"):
        delta_w, new_m, new_v = {}, {}, {}
        for n in TWIN_WEIGHTS:
            delta_w[n], new_m[n], new_v[n] = _adamw(weights[n], grad_w[n], given["m_" + n], given["v_" + n])
    return (loss, grad_x, *[grad_w[n] for n in TWIN_WEIGHTS], *[delta_w[n] for n in TWIN_WEIGHTS],
            *[new_m[n] for n in TWIN_WEIGHTS], *[new_v[n] for n in TWIN_WEIGHTS])
```

```python
import functools

import jax
import jax.numpy as jnp
import numpy as np
from jax import lax
from jax.experimental import pallas as pl
from jax.experimental.pallas import tpu as pltpu

f32 = jnp.float32
bf16 = jnp.bfloat16
SDS = jax.ShapeDtypeStruct
HI = lax.Precision.HIGHEST
MESH = pl.DeviceIdType.MESH

D = 1024
DEPTH = 4
EPS = 1e-6
DN_C = 64
SWA_B = 128
LANE = 128
ROPE_THETA = 500000.0
D_FF = 2816
IN_TOTAL = 7440
PROJ_W = 7680
CB_Q, CB_K, CB_V = 0, 8, 16
CB_AB, CB_SWK, CB_SWV = 56, 57, 58
WB_Z, WB_SWQ, WB_GA, WB_GB = 3, 4, 5, 6
TR = 256
COMM_W = 1024
COMM_ROWS = 4864
COMM_TR = 128
VMEM_BIG = 48 * 2 ** 20

ADAM_LR, ADAM_B1, ADAM_B2, ADAM_EPS, ADAM_WD, ADAM_STEP = 0.001, 0.9, 0.999, 1e-08, 0.01, 10


def _pcall(body, **kw):
    return pl.pallas_call(body, **kw)


def _cparams(vmem=None):
    return pltpu.CompilerParams(vmem_limit_bytes=vmem) if vmem else None


def _dot(a, b, ca, cb, precision=HI):
    return lax.dot_general(a, b, (((ca,), (cb,)), ((), ())), precision=precision, preferred_element_type=f32)


def _pick(n, cands):
    for c in cands:
        if n % c == 0:
            return c
    return n


def _matmul(name, a, b, mode, out_dtype, bias=None):
    if mode == "nn":
        (M, K), (_, N) = a.shape, b.shape
    elif mode == "nt":
        (M, K), (N, _) = a.shape, b.shape
    else:
        (K, M), (_, N) = a.shape, b.shape
    tm = _pick(M, (512, 256, 128, 16))
    tn = _pick(N, (512, 256, 128))
    tk = _pick(K, (512, 256, 128, 16))
    nk = K // tk
    ca, cb = {"nn": (1, 0), "nt": (1, 1), "tn": (0, 0)}[mode]

    def body(*refs):
        if bias is None:
            a_ref, b_ref, o_ref, acc = refs
        else:
            a_ref, b_ref, bias_ref, o_ref, acc = refs
        k = pl.program_id(2)

        @pl.when(k == 0)
        def _():
            acc[...] = jnp.zeros_like(acc)

        acc[...] += _dot(a_ref[...].astype(bf16), b_ref[...].astype(bf16), ca, cb, precision=None)

        @pl.when(k == nk - 1)
        def _():
            r = acc[...]
            if bias is not None:
                r = r + bias_ref[...]
            o_ref[...] = r.astype(o_ref.dtype)

    a_spec = pl.BlockSpec((tk, tm), lambda i, j, k: (k, i)) if mode == "tn" else pl.BlockSpec((tm, tk), lambda i, j, k: (i, k))
    b_spec = pl.BlockSpec((tn, tk), lambda i, j, k: (j, k)) if mode == "nt" else pl.BlockSpec((tk, tn), lambda i, j, k: (k, j))
    in_specs = [a_spec, b_spec]
    args = [a, b]
    if bias is not None:
        in_specs.append(pl.BlockSpec((1, tn), lambda i, j, k: (0, j)))
        args.append(bias)
    return _pcall(
        body, grid=(M // tm, N // tn, nk), in_specs=in_specs, out_specs=pl.BlockSpec((tm, tn), lambda i, j, k: (i, j)),
        out_shape=SDS((M, N), out_dtype), scratch_shapes=[pltpu.VMEM((tm, tn), f32)], name=name)(*args)


def _row_specs(rows, tr):
    return [pl.BlockSpec((tr, w), lambda i, j, off=off: (i, off + j)) for (_, off, w) in rows]


def _rowwise_fwd(name, fn, rows, vecs, out_widths, out_dtypes, nc=1, tr=TR):
    T = rows[0][0].shape[0]
    n_in = len(rows) + len(vecs)

    def body(*refs):
        vals = [r[...].astype(f32) for r in refs[:n_in]]
        res = fn(*vals)
        for o_ref, r in zip(refs[n_in:], res):
            o_ref[...] = r.astype(o_ref.dtype)

    in_specs = _row_specs(rows, tr) + [pl.BlockSpec(v.shape, lambda i, j: (0, 0)) for v in vecs]
    out_specs = [pl.BlockSpec((tr, w), lambda i, j: (i, j)) for w in out_widths]
    out_shape = [SDS((T, w * nc), dt) for w, dt in zip(out_widths, out_dtypes)]
    return _pcall(body, grid=(T // tr, nc), in_specs=in_specs, out_specs=out_specs, out_shape=out_shape, name=name)(
        *[r[0] for r in rows], *vecs)


def _rowwise_bwd(name, fn, rows, vecs, cts, drow_dtypes, nc=1, tr=TR, add_to_first=None):
    T = rows[0][0].shape[0]
    n_r, n_v, n_c = len(rows), len(vecs), len(cts)
    n_add = 0 if add_to_first is None else 1
    keep = [k for k, dt in enumerate(drow_dtypes) if dt is not None]

    def body(*refs):
        n_in = n_r + n_v + n_c + n_add
        vals = [r[...].astype(f32) for r in refs[:n_in]]
        outs = refs[n_in:]
        i, j = pl.program_id(0), pl.program_id(1)
        _, vjp = jax.vjp(fn, *vals[:n_r + n_v])
        grads = vjp(tuple(vals[n_r + n_v:n_r + n_v + n_c]))
        for pos, k in enumerate(keep):
            g = grads[k]
            if n_add and pos == 0:
                g = g + vals[-1]
            outs[pos][...] = g.astype(outs[pos].dtype)

        @pl.when((i == 0) & (j == 0))
        def _():
            for q in range(n_v):
                outs[len(keep) + q][...] = jnp.zeros_like(outs[len(keep) + q])

        for q in range(n_v):
            outs[len(keep) + q][...] += grads[n_r + q]

    extra = [] if add_to_first is None else [add_to_first]
    in_specs = (_row_specs(rows, tr) + [pl.BlockSpec(v.shape, lambda i, j: (0, 0)) for v in vecs]
                + _row_specs(cts, tr) + _row_specs(extra, tr))
    out_specs = [pl.BlockSpec((tr, rows[k][2]), lambda i, j: (i, j)) for k in keep]
    out_specs += [pl.BlockSpec(v.shape, lambda i, j: (0, 0)) for v in vecs]
    out_shape = [SDS((T, rows[k][2] * nc), drow_dtypes[k]) for k in keep] + [SDS(v.shape, f32) for v in vecs]
    return _pcall(body, grid=(T // tr, nc), in_specs=in_specs, out_specs=out_specs, out_shape=out_shape, name=name)(
        *[r[0] for r in rows], *vecs, *[c[0] for c in cts], *[e[0] for e in extra])


def _normmod_fn(x, w, sc, sh):
    y = x * lax.rsqrt(jnp.mean(x * x, axis=-1, keepdims=True) + EPS)
    return ((y * w) * (1.0 + sc) + sh,)


def _resid_fn(x, t, gt):
    return (x + gt * t,)


def _merge_fn(ga, gb, ya, yb):
    return (jax.nn.sigmoid(ga) * ya + jax.nn.sigmoid(gb) * yb,)


def _dngate_fn(o, z, w):
    y = o * lax.rsqrt(jnp.mean(o * o, axis=-1, keepdims=True) + EPS)
    return ((y * w) * (z * jax.nn.sigmoid(z)),)


def _shift_down(x, s):
    if s == 0:
        return x
    t = lax.broadcasted_iota(jnp.int32, x.shape, 0)
    return jnp.where(t >= s, pltpu.roll(x, s, 0), 0.0)


def _shift_up(x, s):
    if s == 0:
        return x
    n = x.shape[0]
    t = lax.broadcasted_iota(jnp.int32, x.shape, 0)
    return jnp.where(t < n - s, pltpu.roll(x, n - s, 0), 0.0)


def _conv_taps(x, w_ref, taps):
    acc = x * w_ref[taps - 1:taps, :]
    for s in range(1, taps):
        acc = acc + _shift_down(x, s) * w_ref[taps - 1 - s:taps - s, :]
    return acc


def _conv_taps_bwd(x, dy, w_ref, dw_ref, taps):
    dx = dy * w_ref[taps - 1:taps, :]
    dw_ref[taps - 1:taps, :] = jnp.sum(dy * x, axis=0, keepdims=True)
    for s in range(1, taps):
        dx = dx + _shift_up(dy, s) * w_ref[taps - 1 - s:taps - s, :]
        dw_ref[taps - 1 - s:taps - s, :] = jnp.sum(dy * _shift_down(x, s), axis=0, keepdims=True)
    return dx


def _dn_act(y, normalize):
    s = y * jax.nn.sigmoid(y)
    if normalize:
        s = s * lax.rsqrt(jnp.sum(s * s, axis=-1, keepdims=True) + EPS)
    return s


def _dnconv_fwd(name, proj, cb, w, normalize):
    T = proj.shape[0]

    def body(x_ref, w_ref, o_ref):
        o_ref[...] = _dn_act(_conv_taps(x_ref[...], w_ref, 4), normalize)

    return _pcall(
        body, grid=(8,), in_specs=[pl.BlockSpec((T, LANE), lambda j: (0, cb + j)), pl.BlockSpec((4, LANE), lambda j: (0, cb + j))],
        out_specs=pl.BlockSpec((T, LANE), lambda j: (0, j)), out_shape=SDS((T, 1024), f32), compiler_params=_cparams(VMEM_BIG), name=name)(proj, w)


def _dnconv_bwd(name, proj, cb, w, dout, normalize):
    T = proj.shape[0]

    def body(x_ref, w_ref, do_ref, dx_ref, dw_ref):
        x = x_ref[...]
        y = _conv_taps(x, w_ref, 4)
        _, vjp = jax.vjp(functools.partial(_dn_act, normalize=normalize), y)
        (dy,) = vjp(do_ref[...])
        dx_ref[...] = _conv_taps_bwd(x, dy, w_ref, dw_ref, 4).astype(dx_ref.dtype)

    return _pcall(
        body, grid=(8,),
        in_specs=[pl.BlockSpec((T, LANE), lambda j: (0, cb + j)), pl.BlockSpec((4, LANE), lambda j: (0, cb + j)),
                  pl.BlockSpec((T, LANE), lambda j: (0, j))],
        out_specs=[pl.BlockSpec((T, LANE), lambda j: (0, j)), pl.BlockSpec((4, LANE), lambda j: (0, j))],
        out_shape=[SDS((T, 1024), bf16), SDS((4, 1024), f32)], compiler_params=_cparams(VMEM_BIG), name=name)(proj, w, dout)


def _ffn_point(a, lin):
    return a * jax.nn.sigmoid(a) * lin


def _ffnact_fwd(name, up, w, b):
    T = up.shape[0]
    nblk = D_FF // LANE

    def body(a_ref, l_ref, w_ref, b_ref, o_ref):
        a = _conv_taps(a_ref[...], w_ref, 3) + b_ref[...]
        o_ref[...] = _ffn_point(a, l_ref[...]).astype(o_ref.dtype)

    return _pcall(
        body, grid=(nblk,),
        in_specs=[pl.BlockSpec((T, LANE), lambda j: (0, j)), pl.BlockSpec((T, LANE), lambda j: (0, nblk + j)),
                  pl.BlockSpec((3, LANE), lambda j: (0, j)), pl.BlockSpec((1, LANE), lambda j: (0, j))],
        out_specs=pl.BlockSpec((T, LANE), lambda j: (0, j)), out_shape=SDS((T, D_FF), bf16), compiler_params=_cparams(VMEM_BIG), name=name)(up, up, w, b)


def _ffnact_bwd(name, up, w, b, dmid):
    T = up.shape[0]
    nblk = D_FF // LANE

    def body(a_ref, l_ref, w_ref, b_ref, dm_ref, da_ref, dl_ref, dw_ref, db_ref):
        x = a_ref[...]
        a = _conv_taps(x, w_ref, 3) + b_ref[...]
        _, vjp = jax.vjp(_ffn_point, a, l_ref[...])
        da, dl = vjp(dm_ref[...].astype(f32))
        dl_ref[...] = dl.astype(dl_ref.dtype)
        db_ref[...] = jnp.sum(da, axis=0, keepdims=True)
        da_ref[...] = _conv_taps_bwd(x, da, w_ref, dw_ref, 3).astype(da_ref.dtype)

    col = lambda r: pl.BlockSpec((r, LANE), lambda j: (0, j))
    return _pcall(
        body, grid=(nblk,),
        in_specs=[col(T), pl.BlockSpec((T, LANE), lambda j: (0, nblk + j)), col(3), col(1), col(T)],
        out_specs=[col(T), col(T), col(3), col(1)],
        out_shape=[SDS((T, D_FF), bf16), SDS((T, D_FF), bf16), SDS((3, D_FF), f32), SDS((1, D_FF), f32)],
        compiler_params=_cparams(VMEM_BIG), name=name)(up, up, w, b, dmid)


def _gdn_chunk(q, k, v, ab, alog, dtb, S, h):
    C = DN_C
    lane = lax.broadcasted_iota(jnp.int32, (C, LANE), 1)
    a = jnp.sum(jnp.where(lane == h, ab, 0.0), axis=1, keepdims=True)
    b = jnp.sum(jnp.where(lane == h + 8, ab, 0.0), axis=1, keepdims=True)
    lane1 = lax.broadcasted_iota(jnp.int32, (1, LANE), 1)
    al = jnp.sum(jnp.where(lane1 == h, alog, 0.0), axis=1, keepdims=True)
    db = jnp.sum(jnp.where(lane1 == h, dtb, 0.0), axis=1, keepdims=True)
    g = -jnp.exp(al) * jax.nn.softplus(a + db)
    beta = jax.nn.sigmoid(b)
    ri = lax.broadcasted_iota(jnp.int32, (C, C), 0)
    ci = lax.broadcasted_iota(jnp.int32, (C, C), 1)
    tri = (ri >= ci).astype(f32)
    eye = (ri == ci).astype(f32)
    G = jnp.broadcast_to(g, (C, LANE))
    gc = _dot(tri, G, 1, 0)
    gi = _dot(tri, jnp.broadcast_to(g, (C, C)), 1, 0)
    decay = jnp.exp(jnp.where(ri >= ci, gi - gi.T, -jnp.inf))
    qs = q * (LANE ** -0.5)
    kb = k * beta
    L = jnp.where(ri > ci, _dot(kb, k, 1, 1) * decay, 0.0)
    X = eye
    for lg in range(6):
        same = (ri >> (lg + 1)) == (ci >> (lg + 1))
        lower_left = same & (((ri >> lg) & 1) == 1) & (((ci >> lg) & 1) == 0)
        X = X - _dot(_dot(X, jnp.where(lower_left, L, 0.0), 1, 0), X, 1, 0)
    egc = jnp.exp(gc)
    u = _dot(X, v * beta, 1, 0)
    w = _dot(X, kb * egc, 1, 0)
    qk = _dot(qs, k, 1, 1) * decay
    g_last = jnp.sum(G, axis=0, keepdims=True)
    k_dec = k * jnp.exp(g_last - gc)
    v_new = u - _dot(w, S, 1, 0)
    o = _dot(qs * egc, S, 1, 0) + _dot(qk, v_new, 1, 0)
    S_new = S * jnp.exp(g_last) + _dot(k_dec, v_new, 0, 0)
    return o, S_new


def _gdn_fwd(name, q, k, v, proj, alog, dtb):
    T = q.shape[0]
    N = T // DN_C

    def body(q_ref, k_ref, v_ref, ab_ref, al_ref, dt_ref, o_ref, sall_ref, s_scr):
        n, h = pl.program_id(0), pl.program_id(1)

        @pl.when(n == 0)
        def _():
            s_scr[h] = jnp.zeros((LANE, LANE), f32)

        S = s_scr[h]
        sall_ref[...] = S
        o, S_new = _gdn_chunk(q_ref[...], k_ref[...], v_ref[...], ab_ref[...], al_ref[...], dt_ref[...], S, h)
        o_ref[...] = o
        s_scr[h] = S_new

    blk = pl.BlockSpec((DN_C, LANE), lambda n, h: (n, h))
    vec = pl.BlockSpec((1, LANE), lambda n, h: (0, 0))
    state = pl.BlockSpec((None, None, LANE, LANE), lambda n, h: (n, h, 0, 0))
    return _pcall(
        body, grid=(N, 8), in_specs=[blk, blk, blk, pl.BlockSpec((DN_C, LANE), lambda n, h: (n, CB_AB)), vec, vec],
        out_specs=[blk, state], out_shape=[SDS((T, 1024), f32), SDS((N, 8, LANE, LANE), f32)],
        scratch_shapes=[pltpu.VMEM((8, LANE, LANE), f32)], name=name)(q, k, v, proj, alog, dtb)


def _gdn_bwd(name, q, k, v, proj, alog, dtb, sall, do):
    T = q.shape[0]
    N = T // DN_C

    def body(q_ref, k_ref, v_ref, ab_ref, al_ref, dt_ref, s_ref, do_ref, dq_ref, dk_ref, dv_ref, dab_ref, dal_ref, ddt_ref, ds_scr):
        n, h = pl.program_id(0), pl.program_id(1)

        @pl.when(n == 0)
        def _():
            ds_scr[h] = jnp.zeros((LANE, LANE), f32)

        @pl.when((n == 0) & (h == 0))
        def _():
            dal_ref[...] = jnp.zeros_like(dal_ref)
            ddt_ref[...] = jnp.zeros_like(ddt_ref)

        @pl.when(h == 0)
        def _():
            dab_ref[...] = jnp.zeros_like(dab_ref)

        _, vjp = jax.vjp(functools.partial(_gdn_chunk, h=h), q_ref[...], k_ref[...], v_ref[...], ab_ref[...], al_ref[...],
                         dt_ref[...], s_ref[...])
        dq, dk, dv, dab, dal, ddt, dS = vjp((do_ref[...], ds_scr[h]))
        dq_ref[...] = dq
        dk_ref[...] = dk
        dv_ref[...] = dv
        dab_ref[...] += dab
        dal_ref[...] += dal
        ddt_ref[...] += ddt
        ds_scr[h] = dS

    blk = pl.BlockSpec((DN_C, LANE), lambda n, h: (N - 1 - n, h))
    vec = pl.BlockSpec((1, LANE), lambda n, h: (0, 0))
    state = pl.BlockSpec((None, None, LANE, LANE), lambda n, h: (N - 1 - n, h, 0, 0))
    return _pcall(
        body, grid=(N, 8),
        in_specs=[blk, blk, blk, pl.BlockSpec((DN_C, LANE), lambda n, h: (N - 1 - n, CB_AB)), vec, vec, state, blk],
        out_specs=[blk, blk, blk, pl.BlockSpec((DN_C, LANE), lambda n, h: (N - 1 - n, 0)), vec, vec],
        out_shape=[SDS((T, 1024), f32)] * 3 + [SDS((T, LANE), f32), SDS((1, LANE), f32), SDS((1, LANE), f32)],
        scratch_shapes=[pltpu.VMEM((8, LANE, LANE), f32)], name=name)(q, k, v, proj, alog, dtb, sall, do)


def _qknorm_fn(x, w, bd):
    return x * lax.rsqrt(_dot(x * x, bd, 1, 0) + EPS) * w


def _rope_apply(xn, c, s1, s2):
    W = xn.shape[1]
    return xn * c + pltpu.roll(xn, W - 8, 1) * s1 + pltpu.roll(xn, 8, 1) * s2


def _rope_apply_t(d, c, s1, s2):
    W = d.shape[1]
    return d * c + pltpu.roll(d * s1, 8, 1) + pltpu.roll(d * s2, W - 8, 1)


def _qkprep_fwd(name, proj, wb, width, w, bd, tabs):
    T = proj.shape[0]
    tr = 128

    def body(x_ref, w_ref, bd_ref, c_ref, s1_ref, s2_ref, o_ref):
        xn = _qknorm_fn(x_ref[...], w_ref[...], bd_ref[...])
        o_ref[...] = _rope_apply(xn, c_ref[...], s1_ref[...], s2_ref[...])

    row0 = pl.BlockSpec((tr, width), lambda i: (i, 0))
    full = lambda a: pl.BlockSpec(a.shape, lambda i: (0, 0))
    return _pcall(
        body, grid=(T // tr,), in_specs=[pl.BlockSpec((tr, width), lambda i: (i, wb)), full(w), full(bd), row0, row0, row0],
        out_specs=row0, out_shape=SDS((T, width), f32), name=name)(proj, w, bd, *tabs)


def _qkprep_bwd(name, proj, wb, width, w, bd, tabs, dout):
    T = proj.shape[0]
    tr = 128

    def body(x_ref, w_ref, bd_ref, c_ref, s1_ref, s2_ref, do_ref, dx_ref, dw_ref):
        i = pl.program_id(0)
        dxn = _rope_apply_t(do_ref[...], c_ref[...], s1_ref[...], s2_ref[...])
        bd = bd_ref[...]
        _, vjp = jax.vjp(lambda x, w_: _qknorm_fn(x, w_, bd), x_ref[...], w_ref[...])
        dx, dw = vjp(dxn)
        dx_ref[...] = dx.astype(dx_ref.dtype)

        @pl.when(i == 0)
        def _():
            dw_ref[...] = jnp.zeros_like(dw_ref)

        dw_ref[...] += dw

    row0 = pl.BlockSpec((tr, width), lambda i: (i, 0))
    full = lambda a: pl.BlockSpec(a.shape, lambda i: (0, 0))
    return _pcall(
        body, grid=(T // tr,), in_specs=[pl.BlockSpec((tr, width), lambda i: (i, wb)), full(w), full(bd), row0, row0, row0, row0],
        out_specs=[row0, full(w)], out_shape=[SDS((T, width), bf16), SDS(w.shape, f32)], name=name)(proj, w, bd, *tabs, dout)


def _attn_head(q, kb, vb, sink, first):
    s = _dot(q, kb, 1, 1) * 0.125
    qi = lax.broadcasted_iota(jnp.int32, (SWA_B, 2 * SWA_B), 0)
    kj = lax.broadcasted_iota(jnp.int32, (SWA_B, 2 * SWA_B), 1)
    rel = qi + SWA_B - kj
    mask = (rel >= 0) & (rel < SWA_B) & ((kj >= SWA_B) | jnp.logical_not(first))
    s = jnp.where(mask, s, -jnp.inf)
    m = lax.stop_gradient(jnp.maximum(jnp.max(s, axis=1, keepdims=True), sink))
    p = jnp.exp(s - m)
    denom = jnp.sum(p, axis=1, keepdims=True) + jnp.exp(sink - m)
    return _dot(p / denom, vb, 1, 0)


def _attn_specs():
    qs = pl.BlockSpec((SWA_B, 1024), lambda i: (i, 0))
    cur = pl.BlockSpec((SWA_B, LANE), lambda i: (i, 0))
    prev = pl.BlockSpec((SWA_B, LANE), lambda i: (jnp.maximum(i - 1, 0), 0))
    vcur = pl.BlockSpec((SWA_B, LANE), lambda i: (i, CB_SWV))
    vprev = pl.BlockSpec((SWA_B, LANE), lambda i: (jnp.maximum(i - 1, 0), CB_SWV))
    vec = pl.BlockSpec((1, LANE), lambda i: (0, 0))
    return qs, cur, prev, vcur, vprev, vec


def _attn_fwd(name, sq, sk, proj, sinks):
    T = sq.shape[0]

    def body(q_ref, kp_ref, kc_ref, vp_ref, vc_ref, sk_ref, o_ref):
        first = pl.program_id(0) == 0
        lane1 = lax.broadcasted_iota(jnp.int32, (1, LANE), 1)
        sinks_v = sk_ref[...]
        for hk in range(2):
            ks = slice(64 * hk, 64 * hk + 64)
            kb = jnp.concatenate([kp_ref[:, ks], kc_ref[:, ks]], axis=0)
            vb = jnp.concatenate([vp_ref[:, ks], vc_ref[:, ks]], axis=0)
            for g in range(8):
                h = hk * 8 + g
                hs = slice(64 * h, 64 * h + 64)
                sink = jnp.sum(jnp.where(lane1 == h, sinks_v, 0.0), axis=1, keepdims=True)
                o_ref[:, hs] = _attn_head(q_ref[:, hs], kb, vb, sink, first).astype(o_ref.dtype)

    qs, cur, prev, vcur, vprev, vec = _attn_specs()
    return _pcall(body, grid=(T // SWA_B,), in_specs=[qs, prev, cur, vprev, vcur, vec], out_specs=qs,
                  out_shape=SDS((T, 1024), bf16), name=name)(sq, sk, sk, proj, proj, sinks)


def _attn_bwd(name, sq, sk, proj, sinks, do):
    T = sq.shape[0]

    def body(q_ref, kp_ref, kc_ref, vp_ref, vc_ref, sk_ref, do_ref, dq_ref, dkp_ref, dkc_ref, dvp_ref, dvc_ref, dsk_ref):
        first = pl.program_id(0) == 0

        @pl.when(first)
        def _():
            dsk_ref[...] = jnp.zeros_like(dsk_ref)

        lane1 = lax.broadcasted_iota(jnp.int32, (1, LANE), 1)
        sinks_v = sk_ref[...]
        dsk = jnp.zeros((1, LANE), f32)
        for hk in range(2):
            ks = slice(64 * hk, 64 * hk + 64)
            kb = jnp.concatenate([kp_ref[:, ks], kc_ref[:, ks]], axis=0)
            vb = jnp.concatenate([vp_ref[:, ks], vc_ref[:, ks]], axis=0)
            dkb = jnp.zeros((2 * SWA_B, 64), f32)
            dvb = jnp.zeros((2 * SWA_B, 64), f32)
            for g in range(8):
                h = hk * 8 + g
                hs = slice(64 * h, 64 * h + 64)
                sink = jnp.sum(jnp.where(lane1 == h, sinks_v, 0.0), axis=1, keepdims=True)
                _, vjp = jax.vjp(lambda q_, kb_, vb_, sink_: _attn_head(q_, kb_, vb_, sink_, first), q_ref[:, hs], kb, vb, sink)
                dq, dk_, dv_, ds_ = vjp(do_ref[:, hs])
                dq_ref[:, hs] = dq
                dkb = dkb + dk_
                dvb = dvb + dv_
                dsk = dsk + jnp.where(lane1 == h, ds_, 0.0)
            dkp_ref[:, ks] = dkb[:SWA_B]
            dkc_ref[:, ks] = dkb[SWA_B:]
            dvp_ref[:, ks] = dvb[:SWA_B]
            dvc_ref[:, ks] = dvb[SWA_B:]
        dsk_ref[...] += dsk

    qs, cur, prev, vcur, vprev, vec = _attn_specs()
    return _pcall(
        body, grid=(T // SWA_B,), in_specs=[qs, prev, cur, vprev, vcur, vec, qs], out_specs=[qs, cur, cur, cur, cur, vec],
        out_shape=[SDS((T, 1024), f32)] + [SDS((T, LANE), f32)] * 4 + [SDS((1, LANE), f32)], name=name)(sq, sk, sk, proj, proj, sinks, do)


def _shift_add(name, cur, prev, out_dtype):
    T = cur.shape[0]
    nb = T // SWA_B

    def body(c_ref, p_ref, o_ref):
        has_next = (pl.program_id(0) + 1 < nb).astype(f32)
        o_ref[...] = (c_ref[...] + has_next * p_ref[...]).astype(o_ref.dtype)

    blk = pl.BlockSpec((SWA_B, LANE), lambda i: (i, 0))
    nxt = pl.BlockSpec((SWA_B, LANE), lambda i: (jnp.minimum(i + 1, nb - 1), 0))
    return _pcall(body, grid=(nb,), in_specs=[blk, nxt], out_specs=blk, out_shape=SDS((T, LANE), out_dtype), name=name)(cur, prev)


def _loss(name, y, tgt):
    T = y.shape[0]

    def body(y_ref, t_ref, l_ref, dy_ref):
        @pl.when(pl.program_id(0) == 0)
        def _():
            l_ref[...] = jnp.zeros_like(l_ref)

        d = y_ref[...] - t_ref[...]
        l_ref[...] += jnp.sum(d * d) * (0.5 / D)
        dy_ref[...] = d * (1.0 / D)

    row = pl.BlockSpec((TR, D), lambda i: (i, 0))
    return _pcall(body, grid=(T // TR,), in_specs=[row, row], out_specs=[pl.BlockSpec((8, LANE), lambda i: (0, 0)), row],
                  out_shape=[SDS((8, LANE), f32), SDS((T, D), f32)], name=name)(y, tgt)


def _adamw(name, w, g, m, v):
    shape = w.shape
    C = shape[-1]
    R = int(np.prod(shape[:-1]))
    tr = _pick(R, (128, 64, 16, 8))
    bc1 = np.float32(1.0 - ADAM_B1 ** ADAM_STEP)
    bc2 = np.float32(1.0 - ADAM_B2 ** ADAM_STEP)

    def body(w_ref, g_ref, m_ref, v_ref, d_ref, mo_ref, vo_ref):
        g_ = g_ref[...]
        m_ = ADAM_B1 * m_ref[...] + (1.0 - ADAM_B1) * g_
        v_ = ADAM_B2 * v_ref[...] + (1.0 - ADAM_B2) * (g_ * g_)
        d_ref[...] = -ADAM_LR * ((m_ / bc1) / (jnp.sqrt(v_ / bc2) + ADAM_EPS) + ADAM_WD * w_ref[...])
        mo_ref[...] = m_
        vo_ref[...] = v_

    blk = pl.BlockSpec((tr, C), lambda i: (i, 0))
    outs = _pcall(body, grid=(R // tr,), in_specs=[blk] * 4, out_specs=[blk] * 3, out_shape=[SDS((R, C), f32)] * 3,
                  compiler_params=_cparams(VMEM_BIG), name=name)(*[t.reshape(R, C) for t in (w, g, m, v)])
    return [o.reshape(shape) for o in outs]


def _silu_rows(name, x):
    def body(x_ref, o_ref):
        t = x_ref[...]
        o_ref[...] = (t * jax.nn.sigmoid(t)).astype(o_ref.dtype)

    return _pcall(body, out_shape=SDS(x.shape, bf16), name=name)(x)


def _sum_leading(name, x):
    n = x.shape[0]

    def body(x_ref, o_ref):
        acc = x_ref[0]
        for k in range(1, n):
            acc = acc + x_ref[k]
        o_ref[...] = acc

    tr = _pick(x.shape[1], (COMM_TR, 8))
    return _pcall(body, grid=(x.shape[1] // tr,), in_specs=[pl.BlockSpec((n, tr, x.shape[2]), lambda i: (0, i, 0))],
                  out_specs=pl.BlockSpec((tr, x.shape[2]), lambda i: (i, 0)), out_shape=SDS(x.shape[1:], x.dtype), name=name)(x)


def _add_my_half(name, g4, b1, c):
    _, R, W = g4.shape
    nblk = (R // 2) // COMM_TR

    def body(c_ref, g_ref, b_ref, o_ref):
        o_ref[...] = g_ref[...] + b_ref[...]

    grid_spec = pltpu.PrefetchScalarGridSpec(
        num_scalar_prefetch=1, grid=(4, nblk),
        in_specs=[pl.BlockSpec((None, COMM_TR, W), lambda s, i, c_ref: (s, c_ref[0] * nblk + i, 0)),
                  pl.BlockSpec((None, COMM_TR, W), lambda s, i, c_ref: (s, i, 0))],
        out_specs=pl.BlockSpec((None, COMM_TR, W), lambda s, i, c_ref: (s, i, 0)))
    return _pcall(body, grid_spec=grid_spec, out_shape=SDS((4, R // 2, W), f32), name=name)(c.reshape(1), g4, b1)


HBM_SPEC = pl.BlockSpec(memory_space=pltpu.HBM)


def _position():
    x, y, c = lax.axis_index("x"), lax.axis_index("y"), lax.axis_index("c")
    return x, y, c, [(1 - x, y), (x, 1 - y), (1 - x, 1 - y)]


def _remote(src, dst, send_sems, recv_sems, k, to):
    return pltpu.make_async_remote_copy(src_ref=src, dst_ref=dst, send_sem=send_sems.at[k], recv_sem=recv_sems.at[k],
                                        device_id=to, device_id_type=MESH)


def _allgather_chips(name, buf):
    R, W = buf.shape
    Rh = R // 2

    def body(in_ref, out_ref, send_sems, recv_sems, local_sem):
        x, y, c, chips = _position()
        me = 2 * x + y
        sib = (x, y, 1 - c)
        half = pl.ds(pl.multiple_of(c * Rh, 32), Rh)
        ohalf = pl.ds(pl.multiple_of((1 - c) * Rh, 32), Rh)
        mine = pltpu.make_async_copy(in_ref, out_ref.at[me], local_sem)
        mine.start()
        first = [_remote(in_ref.at[half], out_ref.at[me, half], send_sems, recv_sems, j, (cx, cy, c)) for j, (cx, cy) in enumerate(chips)]
        for cp in first:
            cp.start()
        passed = []
        for j, (cx, cy) in enumerate(chips):
            rows = out_ref.at[2 * cx + cy, half]
            _remote(rows, rows, send_sems, recv_sems, j, (cx, cy, c)).wait_recv()
            cp = _remote(rows, rows, send_sems, recv_sems, 3 + j, sib)
            cp.start()
            passed.append(cp)
        for j, (cx, cy) in enumerate(chips):
            rows = out_ref.at[2 * cx + cy, ohalf]
            _remote(rows, rows, send_sems, recv_sems, 3 + j, sib).wait_recv()
        for cp in first + passed:
            cp.wait_send()
        mine.wait()

    return _pcall(body, in_specs=[HBM_SPEC], out_specs=HBM_SPEC, out_shape=SDS((4, R, W), buf.dtype),
                  scratch_shapes=[pltpu.SemaphoreType.DMA((6,)), pltpu.SemaphoreType.DMA((6,)), pltpu.SemaphoreType.DMA], name=name)(buf)


def _swap_halves(name, g4):
    _, R, W = g4.shape
    Rh = R // 2

    def body(in_ref, out_ref, send_sems, recv_sems):
        x, y, c, _ = _position()
        ohalf = pl.ds(pl.multiple_of((1 - c) * Rh, 32), Rh)
        cp = _remote(in_ref.at[:, ohalf, :], out_ref, send_sems, recv_sems, 0, (x, y, 1 - c))
        cp.start()
        cp.wait()

    return _pcall(body, in_specs=[HBM_SPEC], out_specs=HBM_SPEC, out_shape=SDS((4, Rh, W), g4.dtype),
                  scratch_shapes=[pltpu.SemaphoreType.DMA((1,)), pltpu.SemaphoreType.DMA((1,))], name=name)(g4)


def _scatter_chips(name, p4):
    _, Rh, W = p4.shape

    def body(in_ref, out_ref, send_sems, recv_sems, local_sem):
        x, y, c, chips = _position()
        me = 2 * x + y
        mine = pltpu.make_async_copy(in_ref.at[me], out_ref.at[me], local_sem)
        mine.start()
        sends = [_remote(in_ref.at[2 * cx + cy], out_ref.at[me], send_sems, recv_sems, j, (cx, cy, c)) for j, (cx, cy) in enumerate(chips)]
        for cp in sends:
            cp.start()
        for j, (cx, cy) in enumerate(chips):
            slot = out_ref.at[2 * cx + cy]
            _remote(slot, slot, send_sems, recv_sems, j, (cx, cy, c)).wait_recv()
        for cp in sends:
            cp.wait_send()
        mine.wait()

    return _pcall(body, in_specs=[HBM_SPEC], out_specs=HBM_SPEC, out_shape=SDS((4, Rh, W), p4.dtype),
                  scratch_shapes=[pltpu.SemaphoreType.DMA((3,)), pltpu.SemaphoreType.DMA((3,)), pltpu.SemaphoreType.DMA], name=name)(p4)


def _join_halves(name, r):
    Rh, W = r.shape

    def body(in_ref, out_ref, send_sems, recv_sems, local_sem):
        x, y, c, _ = _position()
        mine = pltpu.make_async_copy(in_ref, out_ref.at[c], local_sem)
        mine.start()
        cp = _remote(in_ref, out_ref.at[c], send_sems, recv_sems, 0, (x, y, 1 - c))
        cp.start()
        _remote(in_ref, out_ref.at[1 - c], send_sems, recv_sems, 0, (x, y, 1 - c)).wait_recv()
        cp.wait_send()
        mine.wait()

    return _pcall(body, in_specs=[HBM_SPEC], out_specs=HBM_SPEC, out_shape=SDS((2, Rh, W), r.dtype),
                  scratch_shapes=[pltpu.SemaphoreType.DMA((1,)), pltpu.SemaphoreType.DMA((1,)), pltpu.SemaphoreType.DMA], name=name)(r)


def _allgather_all(name, buf):
    r, W = buf.shape

    def body(in_ref, out_ref, send_sems, recv_sems, local_sem):
        x, y, c, _ = _position()
        me = 4 * x + 2 * y + c
        mine = pltpu.make_async_copy(in_ref, out_ref.at[me], local_sem)
        mine.start()
        peers = []
        for mk in range(1, 8):
            mx, my, mc = (mk >> 2) & 1, (mk >> 1) & 1, mk & 1
            px = 1 - x if mx else x
            py = 1 - y if my else y
            pc = 1 - c if mc else c
            peers.append((px, py, pc))
        sends = [_remote(in_ref, out_ref.at[me], send_sems, recv_sems, k, p) for k, p in enumerate(peers)]
        for cp in sends:
            cp.start()
        for k, (px, py, pc) in enumerate(peers):
            slot = out_ref.at[4 * px + 2 * py + pc]
            _remote(slot, slot, send_sems, recv_sems, k, (px, py, pc)).wait_recv()
        for cp in sends:
            cp.wait_send()
        mine.wait()

    return _pcall(body, in_specs=[HBM_SPEC], out_specs=HBM_SPEC, out_shape=SDS((8, r, W), buf.dtype),
                  scratch_shapes=[pltpu.SemaphoreType.DMA((7,)), pltpu.SemaphoreType.DMA((7,)), pltpu.SemaphoreType.DMA], name=name)(buf)


def _reduce_scatter(g4, c, tag):
    b1 = _swap_halves("rs_swap_" + tag, g4)
    p4 = _add_my_half("rs_pair_" + tag, g4, b1, c)
    b2 = _scatter_chips("rs_scatter_" + tag, p4)
    r = _sum_leading("rs_sum_" + tag, b2)
    full = _join_halves("rs_join_" + tag, r)
    return full.reshape(g4.shape[1], g4.shape[2])


_SHARDED = (
    ("w_in", (D, 1860), 1, False),
    ("dn_conv", (4, 768), 1, True),
    ("w_dn_out", (256, D), 0, False),
    ("w_swa_out", (256, D), 0, False),
    ("w_o", (256, D), 0, False),
    ("w_up", (D, 1408), 1, False),
    ("ffn_conv", (3, 704), 1, True),
    ("w_down", (704, D), 0, False),
)


def _pack_weights(shards):
    parts = []
    for nm, shp, _, as_bits in _SHARDED:
        a = shards[nm]
        parts.append(lax.bitcast_convert_type(a, bf16).reshape(-1) if as_bits else a.astype(bf16).reshape(-1))
    flat = jnp.concatenate(parts)
    return jnp.pad(flat, (0, COMM_ROWS * COMM_W - flat.shape[0])).reshape(COMM_ROWS, COMM_W)


def _unpack_weights(g):
    flat = g.reshape(4, -1)
    out, off = {}, 0
    for nm, shp, ax, as_bits in _SHARDED:
        n = int(np.prod(shp)) * (2 if as_bits else 1)
        piece = flat[:, off:off + n]
        off += n
        if as_bits:
            piece = lax.bitcast_convert_type(piece.reshape((4,) + shp + (2,)), f32)
        else:
            piece = piece.reshape((4,) + shp)
        out[nm] = jnp.concatenate([piece[s] for s in range(4)], axis=ax)
    return out


def _pack_grads(grads):
    slots = []
    for s in range(4):
        parts = []
        for nm, shp, ax, _ in _SHARDED:
            n = shp[ax]
            parts.append(lax.slice_in_dim(grads[nm], s * n, (s + 1) * n, axis=ax).reshape(-1))
        flat = jnp.concatenate(parts)
        slots.append(jnp.pad(flat, (0, COMM_ROWS * COMM_W - flat.shape[0])))
    return jnp.stack(slots).reshape(4, COMM_ROWS, COMM_W)


def _unpack_grads(r):
    flat = r.reshape(-1)
    out, off = {}, 0
    for nm, shp, _, _ in _SHARDED:
        n = int(np.prod(shp))
        out[nm] = flat[off:off + n].reshape(shp)
        off += n
    return out


def _regroup_w_in(w):
    z = lambda n: jnp.zeros((w.shape[0], n), w.dtype)
    return jnp.concatenate([w[:, 0:3072], w[:, 3072:4096], w[:, 4112:5136], w[:, 5392:6416], w[:, 6416:7440],
                            w[:, 4096:4112], z(112), w[:, 5136:5264], w[:, 5264:5392], z(128)], axis=1)


def _ungroup_w_in(g):
    return jnp.concatenate([g[:, 0:3072], g[:, 3072:4096], g[:, 7168:7184], g[:, 4096:5120], g[:, 7296:7424], g[:, 7424:7552],
                            g[:, 5120:6144], g[:, 6144:7168]], axis=1)


def _pad_lanes(v, n=LANE):
    return jnp.pad(v, (0, n - v.shape[0])).reshape(1, n)


def _layer_consts(P):
    K = {}
    K["norm_mix"] = P["norm_mix"].reshape(1, D)
    K["norm_ffn"] = P["norm_ffn"].reshape(1, D)
    K["alog"] = _pad_lanes(P["dn_a_log"])
    K["dtb"] = _pad_lanes(P["dn_dt_bias"])
    K["dn_norm"] = P["dn_norm"].reshape(1, LANE)
    K["qn"] = jnp.tile(P["swa_q_norm"], 16).reshape(1, D)
    K["kn"] = jnp.tile(P["swa_k_norm"], 2).reshape(1, LANE)
    K["sinks"] = _pad_lanes(P["swa_sinks"])
    K["ffn_b"] = P["ffn_conv_b"].reshape(1, D_FF)
    return K


def _layer_fwd(x, mod, W, K, tabs, bd):
    sh1, sc1, gt1, sh2, sc2, gt2 = mod
    S = {"x": x}
    (h1,) = _rowwise_fwd("normmod1_fwd", _normmod_fn, [(x, 0, D)], [K["norm_mix"], sc1, sh1], [D], [bf16])
    proj = _matmul("proj_fwd", h1, W["w_in"], "nn", f32)
    qn = _dnconv_fwd("dnconv_q_fwd", proj, CB_Q, W["dn_conv"], True)
    kn = _dnconv_fwd("dnconv_k_fwd", proj, CB_K, W["dn_conv"], True)
    vc = _dnconv_fwd("dnconv_v_fwd", proj, CB_V, W["dn_conv"], False)
    o, sall = _gdn_fwd("gdn_fwd", qn, kn, vc, proj, K["alog"], K["dtb"])
    (on,) = _rowwise_fwd("dngate_fwd", _dngate_fn, [(o, 0, LANE), (proj, 8 * WB_Z, LANE)], [K["dn_norm"]], [LANE], [bf16], nc=8)
    ya = _matmul("dnout_fwd", on, W["w_dn_out"], "nn", f32)
    sq = _qkprep_fwd("qprep_fwd", proj, WB_SWQ, D, K["qn"], bd[0], tabs[0])
    sk = _qkprep_fwd("kprep_fwd", proj, CB_SWK, LANE, K["kn"], bd[1], tabs[1])
    attn = _attn_fwd("attn_fwd", sq, sk, proj, K["sinks"])
    yb = _matmul("swaout_fwd", attn, W["w_swa_out"], "nn", f32)
    (merged,) = _rowwise_fwd("merge_fwd", _merge_fn, [(proj, WB_GA, D), (proj, WB_GB, D), (ya, 0, D), (yb, 0, D)], [], [D], [bf16])
    t1 = _matmul("wo_fwd", merged, W["w_o"], "nn", f32)
    (x1,) = _rowwise_fwd("resid1_fwd", _resid_fn, [(x, 0, D), (t1, 0, D)], [gt1], [D], [f32])
    (h2,) = _rowwise_fwd("normmod2_fwd", _normmod_fn, [(x1, 0, D)], [K["norm_ffn"], sc2, sh2], [D], [bf16])
    up = _matmul("up_fwd", h2, W["w_up"], "nn", f32)
    mid = _ffnact_fwd("ffnact_fwd", up, W["ffn_conv"], K["ffn_b"])
    t2 = _matmul("down_fwd", mid, W["w_down"], "nn", f32)
    (x2,) = _rowwise_fwd("resid2_fwd", _resid_fn, [(x1, 0, D), (t2, 0, D)], [gt2], [D], [f32])
    S.update(h1=h1, proj=proj, qn=qn, kn=kn, vc=vc, o=o, sall=sall, on=on, ya=ya, sq=sq, sk=sk, attn=attn, yb=yb,
             merged=merged, t1=t1, x1=x1, h2=h2, up=up, mid=mid, t2=t2)
    return x2, S


def _layer_bwd(dx2, S, mod, W, K, tabs, bd):
    sh1, sc1, gt1, sh2, sc2, gt2 = mod
    x, x1, proj, up = S["x"], S["x1"], S["proj"], S["up"]
    T = x.shape[0]
    gw, gs = {}, {}
    dt2, dgt2 = _rowwise_bwd("resid2_bwd", _resid_fn, [(x1, 0, D), (S["t2"], 0, D)], [gt2], [(dx2, 0, D)], [None, bf16])
    dmid = _matmul("down_bwd_x", dt2, W["w_down"], "nt", bf16)
    gw["w_down"] = _matmul("down_bwd_w", S["mid"], dt2, "tn", f32)
    dact, dlin, gw["ffn_conv"], dffn_b = _ffnact_bwd("ffnact_bwd", up, W["ffn_conv"], K["ffn_b"], dmid)
    dup = jnp.concatenate([dact, dlin], axis=1)
    dh2 = _matmul("up_bwd_x", dup, W["w_up"], "nt", f32)
    gw["w_up"] = _matmul("up_bwd_w", S["h2"], dup, "tn", f32)
    dx1, dnorm_ffn, dsc2, dsh2 = _rowwise_bwd("normmod2_bwd", _normmod_fn, [(x1, 0, D)], [K["norm_ffn"], sc2, sh2], [(dh2, 0, D)], [f32],
                                              add_to_first=(dx2, 0, D))
    dt1, dgt1 = _rowwise_bwd("resid1_bwd", _resid_fn, [(x, 0, D), (S["t1"], 0, D)], [gt1], [(dx1, 0, D)], [None, bf16])
    dmerged = _matmul("wo_bwd_x", dt1, W["w_o"], "nt", f32)
    gw["w_o"] = _matmul("wo_bwd_w", S["merged"], dt1, "tn", f32)
    dga, dgb, dya, dyb = _rowwise_bwd("merge_bwd", _merge_fn, [(proj, WB_GA, D), (proj, WB_GB, D), (S["ya"], 0, D), (S["yb"], 0, D)], [],
                                      [(dmerged, 0, D)], [bf16, bf16, bf16, bf16])
    don = _matmul("dnout_bwd_x", dya, W["w_dn_out"], "nt", f32)
    gw["w_dn_out"] = _matmul("dnout_bwd_w", S["on"], dya, "tn", f32)
    do, dz, ddn_norm = _rowwise_bwd("dngate_bwd", _dngate_fn, [(S["o"], 0, LANE), (proj, 8 * WB_Z, LANE)], [K["dn_norm"]], [(don, 0, LANE)],
                                    [f32, bf16], nc=8)
    dqn, dkn, dvc, dab, dalog, ddtb = _gdn_bwd("gdn_bwd", S["qn"], S["kn"], S["vc"], proj, K["alog"], K["dtb"], S["sall"], do)
    dpq, dwq = _dnconv_bwd("dnconv_q_bwd", proj, CB_Q, W["dn_conv"], dqn, True)
    dpk, dwk = _dnconv_bwd("dnconv_k_bwd", proj, CB_K, W["dn_conv"], dkn, True)
    dpv, dwv = _dnconv_bwd("dnconv_v_bwd", proj, CB_V, W["dn_conv"], dvc, False)
    gw["dn_conv"] = jnp.concatenate([dwq, dwk, dwv], axis=1)
    dattn = _matmul("swaout_bwd_x", dyb, W["w_swa_out"], "nt", f32)
    gw["w_swa_out"] = _matmul("swaout_bwd_w", S["attn"], dyb, "tn", f32)
    dsq, dkp, dkc, dvp, dvc_, dsinks = _attn_bwd("attn_bwd", S["sq"], S["sk"], proj, K["sinks"], dattn)
    dsk = _shift_add("attn_dk_join", dkc, dkp, f32)
    dswv = _shift_add("attn_dv_join", dvc_, dvp, bf16)
    dswq, dqn_w = _qkprep_bwd("qprep_bwd", proj, WB_SWQ, D, K["qn"], bd[0], tabs[0], dsq)
    dswk, dkn_w = _qkprep_bwd("kprep_bwd", proj, CB_SWK, LANE, K["kn"], bd[1], tabs[1], dsk)
    dproj = jnp.concatenate([dpq, dpk, dpv, dz, dswq, dga, dgb, dab.astype(bf16), dswk, dswv, jnp.zeros((T, LANE), bf16)], axis=1)
    dh1 = _matmul("proj_bwd_x", dproj, W["w_in"], "nt", f32)
    gw["w_in"] = _matmul("proj_bwd_w", S["h1"], dproj, "tn", f32)
    dx, dnorm_mix, dsc1, dsh1 = _rowwise_bwd("normmod1_bwd", _normmod_fn, [(x, 0, D)], [K["norm_mix"], sc1, sh1], [(dh1, 0, D)], [f32],
                                             add_to_first=(dx1, 0, D))
    gs = {"norm_mix": dnorm_mix[0], "dn_a_log": dalog[0, :8], "dn_dt_bias": ddtb[0, :8], "dn_norm": ddn_norm[0],
          "swa_q_norm": dqn_w.reshape(16, 64).sum(0), "swa_k_norm": dkn_w.reshape(2, 64).sum(0), "swa_sinks": dsinks[0, :16],
          "norm_ffn": dnorm_ffn[0], "ffn_conv_b": dffn_b[0]}
    dmod = jnp.concatenate([dsh1, dsc1, dgt1, dsh2, dsc2, dgt2], axis=1)
    return dx, gw, gs, dmod


def _rope_tables(pos):
    T = pos.shape[0]
    half = 8
    inv = jnp.power(ROPE_THETA, -jnp.arange(half, dtype=f32) / half)
    ang = pos.astype(f32)[:, None] * inv
    cos, sin = jnp.cos(ang), jnp.sin(ang)
    z8, z48, o48 = jnp.zeros((T, 8), f32), jnp.zeros((T, 48), f32), jnp.ones((T, 48), f32)
    c64 = jnp.concatenate([cos, cos, o48], axis=1)
    s1 = jnp.concatenate([-sin, z8, z48], axis=1)
    s2 = jnp.concatenate([z8, sin, z48], axis=1)
    return tuple(jnp.tile(t, (1, 16)) for t in (c64, s1, s2))


_SMALL = (("norm_mix", D), ("dn_a_log", 8), ("dn_dt_bias", 8), ("dn_norm", 128), ("swa_q_norm", 64), ("swa_k_norm", 64),
          ("swa_sinks", 16), ("norm_ffn", D), ("ffn_conv_b", D_FF), ("b_ada", 6 * D))
_SMALL_ROWS = 360


def _pack_small(vals):
    flat = jnp.concatenate([vals[nm].reshape(-1) for nm, _ in _SMALL])
    return jnp.pad(flat, (0, _SMALL_ROWS * LANE - flat.shape[0])).reshape(_SMALL_ROWS, LANE)


def _unpack_small(buf):
    flat = buf.reshape(-1)
    out, off = {}, 0
    for nm, n in _SMALL:
        out[nm] = flat[off:off + DEPTH * n].reshape(DEPTH, n)
        off += DEPTH * n
    return out


def kernel(x, c, positions, w_ada, b_ada, norm_mix, w_in, dn_conv, dn_a_log, dn_dt_bias, dn_norm, w_dn_out, swa_q_norm, swa_k_norm, swa_sinks, w_swa_out, w_o, norm_ffn, w_up, ffn_conv, ffn_conv_b, w_down, loss_target, m_w_ada, m_b_ada, m_norm_mix, m_w_in, m_dn_conv, m_dn_a_log, m_dn_dt_bias, m_dn_norm, m_w_dn_out, m_swa_q_norm, m_swa_k_norm, m_swa_sinks, m_w_swa_out, m_w_o, m_norm_ffn, m_w_up, m_ffn_conv, m_ffn_conv_b, m_w_down, v_w_ada, v_b_ada, v_norm_mix, v_w_in, v_dn_conv, v_dn_a_log, v_dn_dt_bias, v_dn_norm, v_w_dn_out, v_swa_q_norm, v_swa_k_norm, v_swa_sinks, v_w_swa_out, v_w_o, v_norm_ffn, v_w_up, v_ffn_conv, v_ffn_conv_b, v_w_down):
    weights = dict(w_ada=w_ada, b_ada=b_ada, norm_mix=norm_mix, w_in=w_in, dn_conv=dn_conv, dn_a_log=dn_a_log, dn_dt_bias=dn_dt_bias,
                   dn_norm=dn_norm, w_dn_out=w_dn_out, swa_q_norm=swa_q_norm, swa_k_norm=swa_k_norm, swa_sinks=swa_sinks,
                   w_swa_out=w_swa_out, w_o=w_o, norm_ffn=norm_ffn, w_up=w_up, ffn_conv=ffn_conv, ffn_conv_b=ffn_conv_b, w_down=w_down)
    mom_m = dict(w_ada=m_w_ada, b_ada=m_b_ada, norm_mix=m_norm_mix, w_in=m_w_in, dn_conv=m_dn_conv, dn_a_log=m_dn_a_log,
                 dn_dt_bias=m_dn_dt_bias, dn_norm=m_dn_norm, w_dn_out=m_w_dn_out, swa_q_norm=m_swa_q_norm, swa_k_norm=m_swa_k_norm,
                 swa_sinks=m_swa_sinks, w_swa_out=m_w_swa_out, w_o=m_w_o, norm_ffn=m_norm_ffn, w_up=m_w_up, ffn_conv=m_ffn_conv,
                 ffn_conv_b=m_ffn_conv_b, w_down=m_w_down)
    mom_v = dict(w_ada=v_w_ada, b_ada=v_b_ada, norm_mix=v_norm_mix, w_in=v_w_in, dn_conv=v_dn_conv, dn_a_log=v_dn_a_log,
                 dn_dt_bias=v_dn_dt_bias, dn_norm=v_dn_norm, w_dn_out=v_w_dn_out, swa_q_norm=v_swa_q_norm, swa_k_norm=v_swa_k_norm,
                 swa_sinks=v_swa_sinks, w_swa_out=v_w_swa_out, w_o=v_w_o, norm_ffn=v_norm_ffn, w_up=v_w_up, ffn_conv=v_ffn_conv,
                 ffn_conv_b=v_ffn_conv_b, w_down=v_w_down)
    order = ["w_ada", "b_ada", "norm_mix", "w_in", "dn_conv", "dn_a_log", "dn_dt_bias", "dn_norm", "w_dn_out", "swa_q_norm",
             "swa_k_norm", "swa_sinks", "w_swa_out", "w_o", "norm_ffn", "w_up", "ffn_conv", "ffn_conv_b", "w_down"]
    ax, ay, ac = lax.axis_index("x"), lax.axis_index("y"), lax.axis_index("c")
    chip = 2 * ax + ay
    dev = 4 * ax + 2 * ay + ac
    T = x.shape[1]
    xs = x[0]

    c_all = _allgather_all("gather_c", jnp.pad(c, ((0, 7), (0, 0)))).reshape(8, 8, D)[:, 0]
    c_act = _silu_rows("silu_c", jnp.pad(c_all, ((0, 8), (0, 0))))
    mod_sh = jnp.stack([
        _matmul("mod_fwd", c_act, w_ada[l].astype(bf16), "nn", f32,
                bias=lax.dynamic_slice(b_ada[l], (chip * 1536,), (1536,)).reshape(1, 1536)) for l in range(DEPTH)])
    mod_all = _allgather_all("gather_mod", mod_sh.reshape(DEPTH * 16 * 12, LANE)).reshape(8, DEPTH, 16, 1536)
    mod_me = jnp.concatenate([lax.dynamic_index_in_dim(mod_all[2 * s], dev, axis=1, keepdims=False) for s in range(4)], axis=1)

    tabs_q = _rope_tables(positions[0])
    tabs = (tabs_q, tuple(t[:, :LANE] for t in tabs_q))
    head = jnp.arange(D) // 64
    bd_q = (head[:, None] == head[None, :]).astype(f32) / 64.0
    bd = (bd_q, bd_q[:LANE, :LANE])

    saved, Ws, Ks, mods = [], [], [], []
    h = xs
    for l in range(DEPTH):
        g = _allgather_chips("gather_w", _pack_weights({nm: weights[nm][l] for nm, _, _, _ in _SHARDED}))
        W = _unpack_weights(g)
        W["w_in"] = _regroup_w_in(W["w_in"])
        K = _layer_consts({nm: weights[nm][l] for nm in ("norm_mix", "norm_ffn", "dn_a_log", "dn_dt_bias", "dn_norm", "swa_q_norm",
                                                          "swa_k_norm", "swa_sinks", "ffn_conv_b")})
        mod = tuple(mod_me[l, k * D:(k + 1) * D].reshape(1, D) for k in range(6))
        h, S = _layer_fwd(h, mod, W, K, tabs, bd)
        saved.append(S), Ws.append(W), Ks.append(K), mods.append(mod)

    loss_blk, dh = _loss("loss", h, loss_target[0])
    loss = lax.psum(loss_blk[0, 0], ("x", "y", "c"))

    grad_sh = [None] * DEPTH
    small = [None] * DEPTH
    dmods = [None] * DEPTH
    for l in reversed(range(DEPTH)):
        dh, gw, gs, dmod = _layer_bwd(dh, saved[l], mods[l], Ws[l], Ks[l], tabs, bd)
        gw["w_in"] = _ungroup_w_in(gw["w_in"])
        grad_sh[l] = _unpack_grads(_reduce_scatter(_pack_grads(gw), ac, "w"))
        small[l], dmods[l] = gs, dmod[0]

    vals = {nm: jnp.stack([small[l][nm] for l in range(DEPTH)]) for nm, _ in _SMALL[:-1]}
    vals["b_ada"] = jnp.stack(dmods)
    small_all = _allgather_all("gather_small", _pack_small(vals))
    g_small = _unpack_small(_sum_leading("sum_small", small_all))
    dmod_all = jnp.stack([_unpack_small(small_all[d])["b_ada"] for d in range(8)])
    dmod_sh = lax.dynamic_slice(dmod_all, (0, 0, chip * 1536), (8, DEPTH, 1536))
    dmod_sh = jnp.pad(dmod_sh, ((0, 8), (0, 0), (0, 0))).astype(bf16)
    g_w_ada = jnp.stack([_matmul("mod_bwd_w", c_act, dmod_sh[:, l], "tn", f32) for l in range(DEPTH)])

    grads = dict(g_small)
    grads["w_ada"] = g_w_ada
    for nm, _, _, _ in _SHARDED:
        grads[nm] = jnp.stack([grad_sh[l][nm] for l in range(DEPTH)])

    delta, new_m, new_v = {}, {}, {}
    for nm in ["w_ada"] + [s[0] for s in _SHARDED]:
        delta[nm], new_m[nm], new_v[nm] = _adamw("adamw_" + nm, weights[nm], grads[nm], mom_m[nm], mom_v[nm])
    sm = [_pack_small({nm: t[nm] for nm, _ in _SMALL}) for t in (weights, grads, mom_m, mom_v)]
    for tgt, buf in zip((delta, new_m, new_v), _adamw("adamw_small", *sm)):
        tgt.update(_unpack_small(buf))

    return (loss, dh[None], *[grads[n] for n in order], *[delta[n] for n in order], *[new_m[n] for n in order], *[new_v[n] for n in order])
```

```python
import functools

import jax
import jax.numpy as jnp
import numpy as np
from jax import lax
from jax.experimental import pallas as pl
from jax.experimental.pallas import tpu as pltpu

f32 = jnp.float32
bf16 = jnp.bfloat16
SDS = jax.ShapeDtypeStruct
HI = lax.Precision.HIGHEST
MESH = pl.DeviceIdType.MESH

D = 1024
DEPTH = 4
EPS = 1e-6
DN_C = 64
SWA_B = 128
LANE = 128
ROPE_THETA = 500000.0
D_FF = 2816
IN_TOTAL = 7440
PROJ_W = 7680
CB_Q, CB_K, CB_V = 0, 8, 16
CB_AB, CB_SWK, CB_SWV = 56, 57, 58
WB_Z, WB_SWQ, WB_GA, WB_GB = 3, 4, 5, 6
TR = 256
COMM_W = 1024
COMM_ROWS = 4864
COMM_TR = 128
VMEM_BIG = 48 * 2 ** 20

ADAM_LR, ADAM_B1, ADAM_B2, ADAM_EPS, ADAM_WD, ADAM_STEP = 0.001, 0.9, 0.999, 1e-08, 0.01, 10


def _pcall(body, **kw):
    return pl.pallas_call(body, **kw)


def _cparams(vmem=None):
    return pltpu.CompilerParams(vmem_limit_bytes=vmem) if vmem else None


def _dot(a, b, ca, cb, precision=HI):
    return lax.dot_general(a, b, (((ca,), (cb,)), ((), ())), precision=precision, preferred_element_type=f32)


def _pick(n, cands):
    for c in cands:
        if n % c == 0:
            return c
    return n


def _tile(n, cap):
    if n <= cap:
        return n
    best = None
    for t in range(LANE, cap + 1, LANE):
        if n % t == 0:
            best = t
    assert best is not None, (n, cap)
    return best


def _matmul(name, a, b, mode, out_dtype, bias=None, out_slots=None):
    if mode == "nn":
        (M, K), (_, N) = a.shape, b.shape
    elif mode == "nt":
        (M, K), (N, _) = a.shape, b.shape
    else:
        (K, M), (_, N) = a.shape, b.shape
    tm = _tile(M, 1536 if mode == "tn" else 1024)
    tn = N // out_slots if out_slots else _tile(N, 1536)
    tk = _tile(K, 512 if mode == "tn" else (1024 if K <= 1024 else 1536))
    nk = K // tk
    ca, cb = {"nn": (1, 0), "nt": (1, 1), "tn": (0, 0)}[mode]

    def body(*refs):
        a_ref, b_ref = refs[:2]
        bias_ref = refs[2] if bias is not None else None
        o_ref = refs[3 if bias is not None else 2]

        def finish(r):
            if bias is not None:
                r = r + bias_ref[...]
            o_ref[...] = r.astype(o_ref.dtype)

        part = _dot(a_ref[...].astype(bf16), b_ref[...].astype(bf16), ca, cb, precision=None)
        if nk == 1:
            finish(part)
        else:
            acc = refs[-1]
            k = pl.program_id(2)

            @pl.when(k == 0)
            def _():
                acc[...] = part

            @pl.when(k > 0)
            def _():
                acc[...] += part

            @pl.when(k == nk - 1)
            def _():
                finish(acc[...])

    a_spec = pl.BlockSpec((tk, tm), lambda i, j, k: (k, i)) if mode == "tn" else pl.BlockSpec((tm, tk), lambda i, j, k: (i, k))
    b_spec = pl.BlockSpec((tn, tk), lambda i, j, k: (j, k)) if mode == "nt" else pl.BlockSpec((tk, tn), lambda i, j, k: (k, j))
    in_specs = [a_spec, b_spec]
    args = [a, b]
    if bias is not None:
        in_specs.append(pl.BlockSpec((1, tn), lambda i, j, k: (0, j)))
        args.append(bias)
    if out_slots:
        out_spec = pl.BlockSpec((None, tm, tn), lambda i, j, k: (j, i, 0))
        out_shape = SDS((out_slots, M, tn), out_dtype)
    else:
        out_spec = pl.BlockSpec((tm, tn), lambda i, j, k: (i, j))
        out_shape = SDS((M, N), out_dtype)
    return _pcall(
        body, grid=(M // tm, N // tn, nk), in_specs=in_specs, out_specs=out_spec, out_shape=out_shape,
        scratch_shapes=[pltpu.VMEM((tm, tn), f32)] if nk > 1 else [], compiler_params=_cparams(VMEM_BIG), name=name)(*args)


def _row_specs(rows, tr):
    return [pl.BlockSpec((tr, w), lambda i, j, off=off: (i, off + j)) for (_, off, w) in rows]


def _rowwise_fwd(name, fn, rows, vecs, out_widths, out_dtypes, nc=1, tr=TR):
    T = rows[0][0].shape[0]
    n_in = len(rows) + len(vecs)

    def body(*refs):
        vals = [r[...].astype(f32) for r in refs[:n_in]]
        res = fn(*vals)
        for o_ref, r in zip(refs[n_in:], res):
            o_ref[...] = r.astype(o_ref.dtype)

    in_specs = _row_specs(rows, tr) + [pl.BlockSpec(v.shape, lambda i, j: (0, 0)) for v in vecs]
    out_specs = [pl.BlockSpec((tr, w), lambda i, j: (i, j)) for w in out_widths]
    out_shape = [SDS((T, w * nc), dt) for w, dt in zip(out_widths, out_dtypes)]
    return _pcall(body, grid=(T // tr, nc), in_specs=in_specs, out_specs=out_specs, out_shape=out_shape, name=name)(
        *[r[0] for r in rows], *vecs)


def _rowwise_bwd(name, fn, rows, vecs, cts, drow_dtypes, nc=1, tr=TR, add_to_first=None):
    T = rows[0][0].shape[0]
    n_r, n_v, n_c = len(rows), len(vecs), len(cts)
    n_add = 0 if add_to_first is None else 1
    keep = [k for k, dt in enumerate(drow_dtypes) if dt is not None]

    def body(*refs):
        n_in = n_r + n_v + n_c + n_add
        vals = [r[...].astype(f32) for r in refs[:n_in]]
        outs = refs[n_in:]
        i, j = pl.program_id(0), pl.program_id(1)
        _, vjp = jax.vjp(fn, *vals[:n_r + n_v])
        grads = vjp(tuple(vals[n_r + n_v:n_r + n_v + n_c]))
        for pos, k in enumerate(keep):
            g = grads[k]
            if n_add and pos == 0:
                g = g + vals[-1]
            outs[pos][...] = g.astype(outs[pos].dtype)

        @pl.when((i == 0) & (j == 0))
        def _():
            for q in range(n_v):
                outs[len(keep) + q][...] = jnp.zeros_like(outs[len(keep) + q])

        for q in range(n_v):
            outs[len(keep) + q][...] += grads[n_r + q]

    extra = [] if add_to_first is None else [add_to_first]
    in_specs = (_row_specs(rows, tr) + [pl.BlockSpec(v.shape, lambda i, j: (0, 0)) for v in vecs]
                + _row_specs(cts, tr) + _row_specs(extra, tr))
    out_specs = [pl.BlockSpec((tr, rows[k][2]), lambda i, j: (i, j)) for k in keep]
    out_specs += [pl.BlockSpec(v.shape, lambda i, j: (0, 0)) for v in vecs]
    out_shape = [SDS((T, rows[k][2] * nc), drow_dtypes[k]) for k in keep] + [SDS(v.shape, f32) for v in vecs]
    return _pcall(body, grid=(T // tr, nc), in_specs=in_specs, out_specs=out_specs, out_shape=out_shape, name=name)(
        *[r[0] for r in rows], *vecs, *[c[0] for c in cts], *[e[0] for e in extra])


def _normmod_fn(x, w, sc, sh):
    y = x * lax.rsqrt(jnp.mean(x * x, axis=-1, keepdims=True) + EPS)
    return ((y * w) * (1.0 + sc) + sh,)


def _resid_fn(x, t, gt):
    return (x + gt * t,)


def _merge_fn(ga, gb, ya, yb):
    return (jax.nn.sigmoid(ga) * ya + jax.nn.sigmoid(gb) * yb,)


def _dngate_fn(o, z, w):
    y = o * lax.rsqrt(jnp.mean(o * o, axis=-1, keepdims=True) + EPS)
    return ((y * w) * (z * jax.nn.sigmoid(z)),)


def _shift_down(x, s):
    if s == 0:
        return x
    t = lax.broadcasted_iota(jnp.int32, x.shape, 0)
    return jnp.where(t >= s, pltpu.roll(x, s, 0), 0.0)


def _shift_up(x, s):
    if s == 0:
        return x
    n = x.shape[0]
    t = lax.broadcasted_iota(jnp.int32, x.shape, 0)
    return jnp.where(t < n - s, pltpu.roll(x, n - s, 0), 0.0)


def _conv_taps(x, w_ref, taps):
    acc = x * w_ref[taps - 1:taps, :]
    for s in range(1, taps):
        acc = acc + _shift_down(x, s) * w_ref[taps - 1 - s:taps - s, :]
    return acc


def _conv_taps_bwd(x, dy, w_ref, dw_ref, taps):
    dx = dy * w_ref[taps - 1:taps, :]
    dw_ref[taps - 1:taps, :] = jnp.sum(dy * x, axis=0, keepdims=True)
    for s in range(1, taps):
        dx = dx + _shift_up(dy, s) * w_ref[taps - 1 - s:taps - s, :]
        dw_ref[taps - 1 - s:taps - s, :] = jnp.sum(dy * _shift_down(x, s), axis=0, keepdims=True)
    return dx


def _dn_act(y, normalize):
    s = y * jax.nn.sigmoid(y)
    if normalize:
        s = s * lax.rsqrt(jnp.sum(s * s, axis=-1, keepdims=True) + EPS)
    return s


def _dnconv_fwd(name, proj, cb, w, normalize):
    T = proj.shape[0]

    def body(x_ref, w_ref, o_ref):
        o_ref[...] = _dn_act(_conv_taps(x_ref[...], w_ref, 4), normalize)

    return _pcall(
        body, grid=(8,), in_specs=[pl.BlockSpec((T, LANE), lambda j: (0, cb + j)), pl.BlockSpec((4, LANE), lambda j: (0, cb + j))],
        out_specs=pl.BlockSpec((T, LANE), lambda j: (0, j)), out_shape=SDS((T, 1024), f32), compiler_params=_cparams(VMEM_BIG), name=name)(proj, w)


def _dnconv_bwd(name, proj, cb, w, dout, normalize):
    T = proj.shape[0]

    def body(x_ref, w_ref, do_ref, dx_ref, dw_ref):
        x = x_ref[...]
        y = _conv_taps(x, w_ref, 4)
        _, vjp = jax.vjp(functools.partial(_dn_act, normalize=normalize), y)
        (dy,) = vjp(do_ref[...])
        dx_ref[...] = _conv_taps_bwd(x, dy, w_ref, dw_ref, 4).astype(dx_ref.dtype)

    return _pcall(
        body, grid=(8,),
        in_specs=[pl.BlockSpec((T, LANE), lambda j: (0, cb + j)), pl.BlockSpec((4, LANE), lambda j: (0, cb + j)),
                  pl.BlockSpec((T, LANE), lambda j: (0, j))],
        out_specs=[pl.BlockSpec((T, LANE), lambda j: (0, j)), pl.BlockSpec((4, LANE), lambda j: (0, j))],
        out_shape=[SDS((T, 1024), bf16), SDS((4, 1024), f32)], compiler_params=_cparams(VMEM_BIG), name=name)(proj, w, dout)


def _ffn_point(a, lin):
    return a * jax.nn.sigmoid(a) * lin


def _ffnact_fwd(name, up, w, b):
    T = up.shape[0]
    nblk = D_FF // LANE

    def body(a_ref, l_ref, w_ref, b_ref, o_ref):
        a = _conv_taps(a_ref[...], w_ref, 3) + b_ref[...]
        o_ref[...] = _ffn_point(a, l_ref[...]).astype(o_ref.dtype)

    return _pcall(
        body, grid=(nblk,),
        in_specs=[pl.BlockSpec((T, LANE), lambda j: (0, j)), pl.BlockSpec((T, LANE), lambda j: (0, nblk + j)),
                  pl.BlockSpec((3, LANE), lambda j: (0, j)), pl.BlockSpec((1, LANE), lambda j: (0, j))],
        out_specs=pl.BlockSpec((T, LANE), lambda j: (0, j)), out_shape=SDS((T, D_FF), bf16), compiler_params=_cparams(VMEM_BIG), name=name)(up, up, w, b)


def _ffnact_bwd(name, up, w, b, dmid):
    T = up.shape[0]
    nblk = D_FF // LANE

    def body(a_ref, l_ref, w_ref, b_ref, dm_ref, da_ref, dl_ref, dw_ref, db_ref):
        x = a_ref[...]
        a = _conv_taps(x, w_ref, 3) + b_ref[...]
        _, vjp = jax.vjp(_ffn_point, a, l_ref[...])
        da, dl = vjp(dm_ref[...].astype(f32))
        dl_ref[...] = dl.astype(dl_ref.dtype)
        db_ref[...] = jnp.sum(da, axis=0, keepdims=True)
        da_ref[...] = _conv_taps_bwd(x, da, w_ref, dw_ref, 3).astype(da_ref.dtype)

    col = lambda r: pl.BlockSpec((r, LANE), lambda j: (0, j))
    return _pcall(
        body, grid=(nblk,),
        in_specs=[col(T), pl.BlockSpec((T, LANE), lambda j: (0, nblk + j)), col(3), col(1), col(T)],
        out_specs=[col(T), col(T), col(3), col(1)],
        out_shape=[SDS((T, D_FF), bf16), SDS((T, D_FF), bf16), SDS((3, D_FF), f32), SDS((1, D_FF), f32)],
        compiler_params=_cparams(VMEM_BIG), name=name)(up, up, w, b, dmid)


def _bmm(a, b, ca, cb, precision=HI):
    return lax.dot_general(a, b, (((ca,), (cb,)), ((0,), (0,))), precision=precision, preferred_element_type=f32)


def _make_bdot(ca, cb):
    def raw(x, y, cx, cy):
        return _bmm(x.astype(bf16), y.astype(bf16), cx, cy, precision=None)

    @jax.custom_vjp
    def f(a, b):
        return raw(a, b, ca, cb)

    def fwd(a, b):
        return raw(a, b, ca, cb), (a, b)

    def bwd(res, dy):
        a, b = res
        if (ca, cb) == (2, 1):
            return raw(dy, b, 2, 2), raw(a, dy, 1, 1)
        if (ca, cb) == (2, 2):
            return raw(dy, b, 2, 1), raw(dy, a, 1, 1)
        return raw(b, dy, 2, 2), raw(a, dy, 2, 1)

    f.defvjp(fwd, bwd)
    return f


_bdot_nn, _bdot_nt, _bdot_tn = _make_bdot(2, 1), _make_bdot(2, 2), _make_bdot(1, 1)


def _tri_inverse_raw(L):
    H, C, _ = L.shape
    ri = lax.broadcasted_iota(jnp.int32, (C, C), 0)
    ci = lax.broadcasted_iota(jnp.int32, (C, C), 1)
    X = jnp.broadcast_to((ri == ci).astype(f32)[None], (H, C, C))
    for lg in range(C.bit_length() - 1):
        same = (ri >> (lg + 1)) == (ci >> (lg + 1))
        lower_left = same & (((ri >> lg) & 1) == 1) & (((ci >> lg) & 1) == 0)
        X = X - _bmm(_bmm(X, jnp.where(lower_left[None], L, 0.0), 2, 1), X, 2, 1)
    return X


@jax.custom_vjp
def _tri_inverse(L):
    return _tri_inverse_raw(L)


def _tri_inverse_fwd(L):
    X = _tri_inverse_raw(L)
    return X, X


def _tri_inverse_bwd(X, dX):
    return (-_bmm(_bmm(X, dX, 1, 1), X, 2, 2),)


_tri_inverse.defvjp(_tri_inverse_fwd, _tri_inverse_bwd)


def _gdn_chunk(q, k, v, ab, alog, dtb, S):
    H, C, _ = q.shape
    lane = lax.broadcasted_iota(jnp.int32, (H, C, LANE), 2)
    head = lax.broadcasted_iota(jnp.int32, (H, C, LANE), 0)
    abb = jnp.broadcast_to(ab[None], (H, C, LANE))
    a = jnp.sum(jnp.where(lane == head, abb, 0.0), axis=2, keepdims=True)
    b = jnp.sum(jnp.where(lane == head + 8, abb, 0.0), axis=2, keepdims=True)
    pick = lax.broadcasted_iota(jnp.int32, (H, 1, LANE), 2) == lax.broadcasted_iota(jnp.int32, (H, 1, LANE), 0)
    al = jnp.sum(jnp.where(pick, alog[None], 0.0), axis=2, keepdims=True)
    db = jnp.sum(jnp.where(pick, dtb[None], 0.0), axis=2, keepdims=True)
    g = -jnp.exp(al) * jax.nn.softplus(a + db)
    beta = jax.nn.sigmoid(b)
    ri = lax.broadcasted_iota(jnp.int32, (C, C), 0)
    ci = lax.broadcasted_iota(jnp.int32, (C, C), 1)
    tri = jnp.broadcast_to((ri >= ci).astype(f32)[None], (H, C, C))
    G = jnp.broadcast_to(g, (H, C, LANE))
    gc = _bmm(tri, G, 2, 1)
    gi = _bmm(tri, jnp.broadcast_to(g, (H, C, C)), 2, 1)
    decay = jnp.exp(jnp.where((ri >= ci)[None], gi - jnp.swapaxes(gi, 1, 2), -jnp.inf))
    qs = q * (LANE ** -0.5)
    kb = k * beta
    X = _tri_inverse(jnp.where((ri > ci)[None], _bdot_nt(kb, k) * decay, 0.0))
    egc = jnp.exp(gc)
    u = _bmm(X, v * beta, 2, 1)
    w = _bmm(X, kb * egc, 2, 1)
    qk = _bdot_nt(qs, k) * decay
    g_last = jnp.sum(G, axis=1, keepdims=True)
    k_dec = k * jnp.exp(g_last - gc)
    v_new = u - _bdot_nn(w, S)
    o = _bdot_nn(qs * egc, S) + _bdot_nn(qk, v_new)
    S_new = S * jnp.exp(g_last) + _bdot_tn(k_dec, v_new)
    return o, S_new


def _heads(ref):
    return jnp.stack([ref[:, LANE * h:LANE * (h + 1)] for h in range(8)], axis=0)


def _put_heads(ref, val):
    for h in range(8):
        ref[:, LANE * h:LANE * (h + 1)] = val[h]


def _gdn_fwd(name, q, k, v, proj, alog, dtb):
    T = q.shape[0]
    N = T // DN_C

    def body(q_ref, k_ref, v_ref, ab_ref, al_ref, dt_ref, o_ref, sall_ref, s_scr):
        @pl.when(pl.program_id(0) == 0)
        def _():
            s_scr[...] = jnp.zeros_like(s_scr)

        S = s_scr[...]
        sall_ref[...] = S
        o, S_new = _gdn_chunk(_heads(q_ref), _heads(k_ref), _heads(v_ref), ab_ref[...], al_ref[...], dt_ref[...], S)
        _put_heads(o_ref, o)
        s_scr[...] = S_new

    blk = pl.BlockSpec((DN_C, 8 * LANE), lambda n: (n, 0))
    vec = pl.BlockSpec((1, LANE), lambda n: (0, 0))
    state = pl.BlockSpec((None, 8, LANE, LANE), lambda n: (n, 0, 0, 0))
    return _pcall(
        body, grid=(N,), in_specs=[blk, blk, blk, pl.BlockSpec((DN_C, LANE), lambda n: (n, CB_AB)), vec, vec],
        out_specs=[blk, state], out_shape=[SDS((T, 1024), f32), SDS((N, 8, LANE, LANE), f32)],
        scratch_shapes=[pltpu.VMEM((8, LANE, LANE), f32)], name=name)(q, k, v, proj, alog, dtb)


def _gdn_bwd(name, q, k, v, proj, alog, dtb, sall, do):
    T = q.shape[0]
    N = T // DN_C

    def body(q_ref, k_ref, v_ref, ab_ref, al_ref, dt_ref, s_ref, do_ref, dq_ref, dk_ref, dv_ref, dab_ref, dal_ref, ddt_ref, ds_scr):
        @pl.when(pl.program_id(0) == 0)
        def _():
            ds_scr[...] = jnp.zeros_like(ds_scr)
            dal_ref[...] = jnp.zeros_like(dal_ref)
            ddt_ref[...] = jnp.zeros_like(ddt_ref)

        _, vjp = jax.vjp(_gdn_chunk, _heads(q_ref), _heads(k_ref), _heads(v_ref), ab_ref[...], al_ref[...], dt_ref[...], s_ref[...])
        dq, dk, dv, dab, dal, ddt, dS = vjp((_heads(do_ref), ds_scr[...]))
        _put_heads(dq_ref, dq)
        _put_heads(dk_ref, dk)
        _put_heads(dv_ref, dv)
        ds_scr[...] = dS
        dab_ref[...] = dab
        dal_ref[...] += dal
        ddt_ref[...] += ddt

    blk = pl.BlockSpec((DN_C, 8 * LANE), lambda n: (N - 1 - n, 0))
    vec = pl.BlockSpec((1, LANE), lambda n: (0, 0))
    state = pl.BlockSpec((None, 8, LANE, LANE), lambda n: (N - 1 - n, 0, 0, 0))
    return _pcall(
        body, grid=(N,),
        in_specs=[blk, blk, blk, pl.BlockSpec((DN_C, LANE), lambda n: (N - 1 - n, CB_AB)), vec, vec, state, blk],
        out_specs=[blk, blk, blk, pl.BlockSpec((DN_C, LANE), lambda n: (N - 1 - n, 0)), vec, vec],
        out_shape=[SDS((T, 1024), f32)] * 3 + [SDS((T, LANE), f32), SDS((1, LANE), f32), SDS((1, LANE), f32)],
        scratch_shapes=[pltpu.VMEM((8, LANE, LANE), f32)], name=name)(q, k, v, proj, alog, dtb, sall, do)


def _qknorm_fn(x, w, bd):
    return x * lax.rsqrt(_dot(x * x, bd, 1, 0) + EPS) * w


def _rope_apply(xn, c, s1, s2):
    W = xn.shape[1]
    return xn * c + pltpu.roll(xn, W - 8, 1) * s1 + pltpu.roll(xn, 8, 1) * s2


def _rope_apply_t(d, c, s1, s2):
    W = d.shape[1]
    return d * c + pltpu.roll(d * s1, 8, 1) + pltpu.roll(d * s2, W - 8, 1)


def _qkprep_fwd(name, proj, wb, width, w, bd, tabs):
    T = proj.shape[0]
    tr = 128

    def body(x_ref, w_ref, bd_ref, c_ref, s1_ref, s2_ref, o_ref):
        xn = _qknorm_fn(x_ref[...], w_ref[...], bd_ref[...])
        o_ref[...] = _rope_apply(xn, c_ref[...], s1_ref[...], s2_ref[...])

    row0 = pl.BlockSpec((tr, width), lambda i: (i, 0))
    full = lambda a: pl.BlockSpec(a.shape, lambda i: (0, 0))
    return _pcall(
        body, grid=(T // tr,), in_specs=[pl.BlockSpec((tr, width), lambda i: (i, wb)), full(w), full(bd), row0, row0, row0],
        out_specs=row0, out_shape=SDS((T, width), f32), name=name)(proj, w, bd, *tabs)


def _qkprep_bwd(name, proj, wb, width, w, bd, tabs, dout):
    T = proj.shape[0]
    tr = 128

    def body(x_ref, w_ref, bd_ref, c_ref, s1_ref, s2_ref, do_ref, dx_ref, dw_ref):
        i = pl.program_id(0)
        dxn = _rope_apply_t(do_ref[...], c_ref[...], s1_ref[...], s2_ref[...])
        bd = bd_ref[...]
        _, vjp = jax.vjp(lambda x, w_: _qknorm_fn(x, w_, bd), x_ref[...], w_ref[...])
        dx, dw = vjp(dxn)
        dx_ref[...] = dx.astype(dx_ref.dtype)

        @pl.when(i == 0)
        def _():
            dw_ref[...] = jnp.zeros_like(dw_ref)

        dw_ref[...] += dw

    row0 = pl.BlockSpec((tr, width), lambda i: (i, 0))
    full = lambda a: pl.BlockSpec(a.shape, lambda i: (0, 0))
    return _pcall(
        body, grid=(T // tr,), in_specs=[pl.BlockSpec((tr, width), lambda i: (i, wb)), full(w), full(bd), row0, row0, row0, row0],
        out_specs=[row0, full(w)], out_shape=[SDS((T, width), bf16), SDS(w.shape, f32)], name=name)(proj, w, bd, *tabs, dout)


def _make_dot16(ca, cb):
    def raw(x, y, cx, cy):
        return _dot(x.astype(bf16), y.astype(bf16), cx, cy, precision=None)

    @jax.custom_vjp
    def f(a, b):
        return raw(a, b, ca, cb)

    def fwd(a, b):
        return raw(a, b, ca, cb), (a, b)

    def bwd(res, dy):
        a, b = res
        if (ca, cb) == (1, 0):
            return raw(dy, b, 1, 1), raw(a, dy, 0, 0)
        return raw(dy, b, 1, 0), raw(dy, a, 0, 0)

    f.defvjp(fwd, bwd)
    return f


_dot16_nn, _dot16_nt = _make_dot16(1, 0), _make_dot16(1, 1)


def _attn_group(qg, kb, vb, sinks, first, hk):
    R = qg.shape[0]
    s = _dot16_nt(qg, kb) * 0.125
    qi = lax.broadcasted_iota(jnp.int32, (R, 2 * SWA_B), 0) & (SWA_B - 1)
    kj = lax.broadcasted_iota(jnp.int32, (R, 2 * SWA_B), 1)
    rel = qi + SWA_B - kj
    mask = (rel >= 0) & (rel < SWA_B) & ((kj >= SWA_B) | jnp.logical_not(first))
    s = jnp.where(mask, s, -jnp.inf)
    head = (lax.broadcasted_iota(jnp.int32, (R, LANE), 0) >> 7) + 8 * hk
    lane = lax.broadcasted_iota(jnp.int32, (R, LANE), 1)
    sink = jnp.sum(jnp.where(lane == head, jnp.broadcast_to(sinks, (R, LANE)), 0.0), axis=1, keepdims=True)
    m = lax.stop_gradient(jnp.maximum(jnp.max(s, axis=1, keepdims=True), sink))
    p = jnp.exp(s - m)
    denom = jnp.sum(p, axis=1, keepdims=True) + jnp.exp(sink - m)
    return _dot16_nn(p / denom, vb)


def _group_rows(ref, hk):
    return jnp.concatenate([ref[:, 64 * (8 * hk + g):64 * (8 * hk + g + 1)] for g in range(8)], axis=0)


def _put_group(ref, hk, val):
    for g in range(8):
        ref[:, 64 * (8 * hk + g):64 * (8 * hk + g + 1)] = val[SWA_B * g:SWA_B * (g + 1)].astype(ref.dtype)


def _attn_specs():
    qs = pl.BlockSpec((SWA_B, 1024), lambda i: (i, 0))
    cur = pl.BlockSpec((SWA_B, LANE), lambda i: (i, 0))
    prev = pl.BlockSpec((SWA_B, LANE), lambda i: (jnp.maximum(i - 1, 0), 0))
    vcur = pl.BlockSpec((SWA_B, LANE), lambda i: (i, CB_SWV))
    vprev = pl.BlockSpec((SWA_B, LANE), lambda i: (jnp.maximum(i - 1, 0), CB_SWV))
    vec = pl.BlockSpec((1, LANE), lambda i: (0, 0))
    return qs, cur, prev, vcur, vprev, vec


def _attn_fwd(name, sq, sk, proj, sinks):
    T = sq.shape[0]

    def body(q_ref, kp_ref, kc_ref, vp_ref, vc_ref, sk_ref, o_ref):
        first = pl.program_id(0) == 0
        sinks_v = sk_ref[...]
        for hk in range(2):
            ks = slice(64 * hk, 64 * hk + 64)
            kb = jnp.concatenate([kp_ref[:, ks], kc_ref[:, ks]], axis=0)
            vb = jnp.concatenate([vp_ref[:, ks], vc_ref[:, ks]], axis=0)
            _put_group(o_ref, hk, _attn_group(_group_rows(q_ref, hk), kb, vb, sinks_v, first, hk))

    qs, cur, prev, vcur, vprev, vec = _attn_specs()
    return _pcall(body, grid=(T // SWA_B,), in_specs=[qs, prev, cur, vprev, vcur, vec], out_specs=qs,
                  out_shape=SDS((T, 1024), bf16), name=name)(sq, sk, sk, proj, proj, sinks)


def _attn_bwd(name, sq, sk, proj, sinks, do):
    T = sq.shape[0]

    def body(q_ref, kp_ref, kc_ref, vp_ref, vc_ref, sk_ref, do_ref, dq_ref, dkp_ref, dkc_ref, dvp_ref, dvc_ref, dsk_ref):
        first = pl.program_id(0) == 0

        @pl.when(first)
        def _():
            dsk_ref[...] = jnp.zeros_like(dsk_ref)

        sinks_v = sk_ref[...]
        dsk = jnp.zeros((1, LANE), f32)
        for hk in range(2):
            ks = slice(64 * hk, 64 * hk + 64)
            kb = jnp.concatenate([kp_ref[:, ks], kc_ref[:, ks]], axis=0)
            vb = jnp.concatenate([vp_ref[:, ks], vc_ref[:, ks]], axis=0)
            _, vjp = jax.vjp(functools.partial(_attn_group, first=first, hk=hk), _group_rows(q_ref, hk), kb, vb, sinks_v)
            dq, dkb, dvb, ds_ = vjp(_group_rows(do_ref, hk))
            _put_group(dq_ref, hk, dq)
            dsk = dsk + ds_
            dkp_ref[:, ks] = dkb[:SWA_B]
            dkc_ref[:, ks] = dkb[SWA_B:]
            dvp_ref[:, ks] = dvb[:SWA_B]
            dvc_ref[:, ks] = dvb[SWA_B:]
        dsk_ref[...] += dsk

    qs, cur, prev, vcur, vprev, vec = _attn_specs()
    return _pcall(
        body, grid=(T // SWA_B,), in_specs=[qs, prev, cur, vprev, vcur, vec, qs], out_specs=[qs, cur, cur, cur, cur, vec],
        out_shape=[SDS((T, 1024), f32)] + [SDS((T, LANE), f32)] * 4 + [SDS((1, LANE), f32)], name=name)(sq, sk, sk, proj, proj, sinks, do)


def _shift_add(name, cur, prev, out_dtype):
    T = cur.shape[0]
    nb = T // SWA_B

    def body(c_ref, p_ref, o_ref):
        has_next = (pl.program_id(0) + 1 < nb).astype(f32)
        o_ref[...] = (c_ref[...] + has_next * p_ref[...]).astype(o_ref.dtype)

    blk = pl.BlockSpec((SWA_B, LANE), lambda i: (i, 0))
    nxt = pl.BlockSpec((SWA_B, LANE), lambda i: (jnp.minimum(i + 1, nb - 1), 0))
    return _pcall(body, grid=(nb,), in_specs=[blk, nxt], out_specs=blk, out_shape=SDS((T, LANE), out_dtype), name=name)(cur, prev)


def _loss(name, y, tgt):
    T = y.shape[0]

    def body(y_ref, t_ref, l_ref, dy_ref):
        @pl.when(pl.program_id(0) == 0)
        def _():
            l_ref[...] = jnp.zeros_like(l_ref)

        d = y_ref[...] - t_ref[...]
        l_ref[...] += jnp.sum(d * d) * (0.5 / D)
        dy_ref[...] = d * (1.0 / D)

    row = pl.BlockSpec((TR, D), lambda i: (i, 0))
    return _pcall(body, grid=(T // TR,), in_specs=[row, row], out_specs=[pl.BlockSpec((8, LANE), lambda i: (0, 0)), row],
                  out_shape=[SDS((8, LANE), f32), SDS((T, D), f32)], name=name)(y, tgt)


def _adamw(name, w, g, m, v):
    shape = w.shape
    C = shape[-1]
    R = int(np.prod(shape[:-1]))
    tr = _pick(R, (128, 64, 16, 8))
    bc1 = np.float32(1.0 - ADAM_B1 ** ADAM_STEP)
    bc2 = np.float32(1.0 - ADAM_B2 ** ADAM_STEP)

    def body(w_ref, g_ref, m_ref, v_ref, d_ref, mo_ref, vo_ref):
        g_ = g_ref[...]
        m_ = ADAM_B1 * m_ref[...] + (1.0 - ADAM_B1) * g_
        v_ = ADAM_B2 * v_ref[...] + (1.0 - ADAM_B2) * (g_ * g_)
        d_ref[...] = -ADAM_LR * ((m_ / bc1) / (jnp.sqrt(v_ / bc2) + ADAM_EPS) + ADAM_WD * w_ref[...])
        mo_ref[...] = m_
        vo_ref[...] = v_

    blk = pl.BlockSpec((tr, C), lambda i: (i, 0))
    outs = _pcall(body, grid=(R // tr,), in_specs=[blk] * 4, out_specs=[blk] * 3, out_shape=[SDS((R, C), f32)] * 3,
                  compiler_params=_cparams(VMEM_BIG), name=name)(*[t.reshape(R, C) for t in (w, g, m, v)])
    return [o.reshape(shape) for o in outs]


def _silu_rows(name, x):
    def body(x_ref, o_ref):
        t = x_ref[...]
        o_ref[...] = (t * jax.nn.sigmoid(t)).astype(o_ref.dtype)

    return _pcall(body, out_shape=SDS(x.shape, bf16), name=name)(x)


def _sum_leading(name, x):
    n = x.shape[0]

    def body(x_ref, o_ref):
        acc = x_ref[0]
        for k in range(1, n):
            acc = acc + x_ref[k]
        o_ref[...] = acc

    tr = _pick(x.shape[1], (COMM_TR, 8))
    return _pcall(body, grid=(x.shape[1] // tr,), in_specs=[pl.BlockSpec((n, tr, x.shape[2]), lambda i: (0, i, 0))],
                  out_specs=pl.BlockSpec((tr, x.shape[2]), lambda i: (i, 0)), out_shape=SDS(x.shape[1:], x.dtype), name=name)(x)


def _add_my_half(name, g4, b1, c):
    _, R, W = g4.shape
    nblk = (R // 2) // COMM_TR

    def body(c_ref, g_ref, b_ref, o_ref):
        o_ref[...] = g_ref[...] + b_ref[...]

    grid_spec = pltpu.PrefetchScalarGridSpec(
        num_scalar_prefetch=1, grid=(4, nblk),
        in_specs=[pl.BlockSpec((None, COMM_TR, W), lambda s, i, c_ref: (s, c_ref[0] * nblk + i, 0)),
                  pl.BlockSpec((None, COMM_TR, W), lambda s, i, c_ref: (s, i, 0))],
        out_specs=pl.BlockSpec((None, COMM_TR, W), lambda s, i, c_ref: (s, i, 0)))
    return _pcall(body, grid_spec=grid_spec, out_shape=SDS((4, R // 2, W), f32), name=name)(c.reshape(1), g4, b1)


HBM_SPEC = pl.BlockSpec(memory_space=pltpu.HBM)


def _position():
    x, y, c = lax.axis_index("x"), lax.axis_index("y"), lax.axis_index("c")
    return x, y, c, [(1 - x, y), (x, 1 - y), (1 - x, 1 - y)]


def _remote(src, dst, send_sems, recv_sems, k, to):
    return pltpu.make_async_remote_copy(src_ref=src, dst_ref=dst, send_sem=send_sems.at[k], recv_sem=recv_sems.at[k],
                                        device_id=to, device_id_type=MESH)


def _allgather_chips(name, buf):
    R, W = buf.shape
    Rh = R // 2

    def body(in_ref, out_ref, send_sems, recv_sems, local_sem):
        x, y, c, chips = _position()
        me = 2 * x + y
        sib = (x, y, 1 - c)
        half = pl.ds(pl.multiple_of(c * Rh, 32), Rh)
        ohalf = pl.ds(pl.multiple_of((1 - c) * Rh, 32), Rh)
        mine = pltpu.make_async_copy(in_ref, out_ref.at[me], local_sem)
        mine.start()
        first = [_remote(in_ref.at[half], out_ref.at[me, half], send_sems, recv_sems, j, (cx, cy, c)) for j, (cx, cy) in enumerate(chips)]
        for cp in first:
            cp.start()
        passed = []
        for j, (cx, cy) in enumerate(chips):
            rows = out_ref.at[2 * cx + cy, half]
            _remote(rows, rows, send_sems, recv_sems, j, (cx, cy, c)).wait_recv()
            cp = _remote(rows, rows, send_sems, recv_sems, 3 + j, sib)
            cp.start()
            passed.append(cp)
        for j, (cx, cy) in enumerate(chips):
            rows = out_ref.at[2 * cx + cy, ohalf]
            _remote(rows, rows, send_sems, recv_sems, 3 + j, sib).wait_recv()
        for cp in first + passed:
            cp.wait_send()
        mine.wait()

    return _pcall(body, in_specs=[HBM_SPEC], out_specs=HBM_SPEC, out_shape=SDS((4, R, W), buf.dtype),
                  scratch_shapes=[pltpu.SemaphoreType.DMA((6,)), pltpu.SemaphoreType.DMA((6,)), pltpu.SemaphoreType.DMA], name=name)(buf)


def _swap_halves(name, g4):
    _, R, W = g4.shape
    Rh = R // 2

    def body(in_ref, out_ref, send_sems, recv_sems):
        x, y, c, _ = _position()
        ohalf = pl.ds(pl.multiple_of((1 - c) * Rh, 32), Rh)
        cp = _remote(in_ref.at[:, ohalf, :], out_ref, send_sems, recv_sems, 0, (x, y, 1 - c))
        cp.start()
        cp.wait()

    return _pcall(body, in_specs=[HBM_SPEC], out_specs=HBM_SPEC, out_shape=SDS((4, Rh, W), g4.dtype),
                  scratch_shapes=[pltpu.SemaphoreType.DMA((1,)), pltpu.SemaphoreType.DMA((1,))], name=name)(g4)


def _scatter_chips(name, p4):
    _, Rh, W = p4.shape

    def body(in_ref, out_ref, send_sems, recv_sems, local_sem):
        x, y, c, chips = _position()
        me = 2 * x + y
        mine = pltpu.make_async_copy(in_ref.at[me], out_ref.at[me], local_sem)
        mine.start()
        sends = [_remote(in_ref.at[2 * cx + cy], out_ref.at[me], send_sems, recv_sems, j, (cx, cy, c)) for j, (cx, cy) in enumerate(chips)]
        for cp in sends:
            cp.start()
        for j, (cx, cy) in enumerate(chips):
            slot = out_ref.at[2 * cx + cy]
            _remote(slot, slot, send_sems, recv_sems, j, (cx, cy, c)).wait_recv()
        for cp in sends:
            cp.wait_send()
        mine.wait()

    return _pcall(body, in_specs=[HBM_SPEC], out_specs=HBM_SPEC, out_shape=SDS((4, Rh, W), p4.dtype),
                  scratch_shapes=[pltpu.SemaphoreType.DMA((3,)), pltpu.SemaphoreType.DMA((3,)), pltpu.SemaphoreType.DMA], name=name)(p4)


def _join_halves(name, r):
    Rh, W = r.shape

    def body(in_ref, out_ref, send_sems, recv_sems, local_sem):
        x, y, c, _ = _position()
        mine = pltpu.make_async_copy(in_ref, out_ref.at[c], local_sem)
        mine.start()
        cp = _remote(in_ref, out_ref.at[c], send_sems, recv_sems, 0, (x, y, 1 - c))
        cp.start()
        _remote(in_ref, out_ref.at[1 - c], send_sems, recv_sems, 0, (x, y, 1 - c)).wait_recv()
        cp.wait_send()
        mine.wait()

    return _pcall(body, in_specs=[HBM_SPEC], out_specs=HBM_SPEC, out_shape=SDS((2, Rh, W), r.dtype),
                  scratch_shapes=[pltpu.SemaphoreType.DMA((1,)), pltpu.SemaphoreType.DMA((1,)), pltpu.SemaphoreType.DMA], name=name)(r)


def _allgather_all(name, buf):
    r, W = buf.shape

    def body(in_ref, out_ref, send_sems, recv_sems, local_sem):
        x, y, c, _ = _position()
        me = 4 * x + 2 * y + c
        mine = pltpu.make_async_copy(in_ref, out_ref.at[me], local_sem)
        mine.start()
        peers = []
        for mk in range(1, 8):
            mx, my, mc = (mk >> 2) & 1, (mk >> 1) & 1, mk & 1
            px = 1 - x if mx else x
            py = 1 - y if my else y
            pc = 1 - c if mc else c
            peers.append((px, py, pc))
        sends = [_remote(in_ref, out_ref.at[me], send_sems, recv_sems, k, p) for k, p in enumerate(peers)]
        for cp in sends:
            cp.start()
        for k, (px, py, pc) in enumerate(peers):
            slot = out_ref.at[4 * px + 2 * py + pc]
            _remote(slot, slot, send_sems, recv_sems, k, (px, py, pc)).wait_recv()
        for cp in sends:
            cp.wait_send()
        mine.wait()

    return _pcall(body, in_specs=[HBM_SPEC], out_specs=HBM_SPEC, out_shape=SDS((8, r, W), buf.dtype),
                  scratch_shapes=[pltpu.SemaphoreType.DMA((7,)), pltpu.SemaphoreType.DMA((7,)), pltpu.SemaphoreType.DMA], name=name)(buf)


def _reduce_scatter(g4, c, tag):
    b1 = _swap_halves("rs_swap_" + tag, g4)
    p4 = _add_my_half("rs_pair_" + tag, g4, b1, c)
    b2 = _scatter_chips("rs_scatter_" + tag, p4)
    r = _sum_leading("rs_sum_" + tag, b2)
    full = _join_halves("rs_join_" + tag, r)
    return full.reshape(g4.shape[1], g4.shape[2])


def _dma_sems(n):
    return [pltpu.SemaphoreType.DMA((n,)), pltpu.SemaphoreType.DMA((n,))]


def _gather_chips(name, shards):
    n = len(shards)

    def body(*refs):
        ins, outs = refs[:n], refs[n:2 * n]
        send_sems, recv_sems = refs[2 * n:]
        x, y, c, chips = _position()
        me = 2 * x + y
        sib = (x, y, 1 - c)
        sends, halves = [], []
        for i in range(n):
            rh = ins[i].shape[0] // 2
            halves.append((pl.ds(pl.multiple_of(c * rh, 16), rh), pl.ds(pl.multiple_of((1 - c) * rh, 16), rh)))
        for i in range(n):
            for j, (cx, cy) in enumerate(chips):
                cp = _remote(ins[i].at[halves[i][0]], outs[i].at[me, halves[i][0]], send_sems, recv_sems, 6 * i + j, (cx, cy, c))
                cp.start()
                sends.append(cp)
        for j, (cx, cy) in enumerate(chips):
            for i in range(n):
                rows = outs[i].at[2 * cx + cy, halves[i][0]]
                _remote(rows, rows, send_sems, recv_sems, 6 * i + j, (cx, cy, c)).wait_recv()
                cp = _remote(rows, rows, send_sems, recv_sems, 6 * i + 3 + j, sib)
                cp.start()
                sends.append(cp)
        for j, (cx, cy) in enumerate(chips):
            for i in range(n):
                rows = outs[i].at[2 * cx + cy, halves[i][1]]
                _remote(rows, rows, send_sems, recv_sems, 6 * i + 3 + j, sib).wait_recv()
        for cp in sends:
            cp.wait_send()

    return _pcall(body, in_specs=[HBM_SPEC] * n, out_specs=[HBM_SPEC] * n, out_shape=[SDS((4,) + s.shape, s.dtype) for s in shards],
                  scratch_shapes=_dma_sems(6 * n), name=name)(*shards)


def _swap_halves_multi(name, slots):
    n = len(slots)

    def body(*refs):
        ins, outs = refs[:n], refs[n:2 * n]
        send_sems, recv_sems = refs[2 * n:]
        x, y, c, _ = _position()
        cps = []
        for i in range(n):
            rh = ins[i].shape[1] // 2
            ohalf = pl.ds(pl.multiple_of((1 - c) * rh, 8), rh)
            cp = _remote(ins[i].at[:, ohalf, :], outs[i], send_sems, recv_sems, i, (x, y, 1 - c))
            cp.start()
            cps.append(cp)
        for cp in cps:
            cp.wait()

    return _pcall(body, in_specs=[HBM_SPEC] * n, out_specs=[HBM_SPEC] * n,
                  out_shape=[SDS((4, s.shape[1] // 2, s.shape[2]), s.dtype) for s in slots], scratch_shapes=_dma_sems(n), name=name)(*slots)


def _pair_add(name, g4, b1, c):
    _, R, W = g4.shape
    tr = _pick(R // 2, (256, 128, 32, 16))
    nblk = (R // 2) // tr

    def body(c_ref, g_ref, b_ref, o_ref):
        o_ref[...] = (g_ref[...] + b_ref[...]).astype(o_ref.dtype)

    grid_spec = pltpu.PrefetchScalarGridSpec(
        num_scalar_prefetch=1, grid=(4, nblk),
        in_specs=[pl.BlockSpec((None, tr, W), lambda s, i, c_ref: (s, c_ref[0] * nblk + i, 0)),
                  pl.BlockSpec((None, tr, W), lambda s, i, c_ref: (s, i, 0))],
        out_specs=pl.BlockSpec((None, tr, W), lambda s, i, c_ref: (s, i, 0)))
    return _pcall(body, grid_spec=grid_spec, out_shape=SDS((4, R // 2, W), bf16), name=name)(c.reshape(1), g4, b1)


def _scatter_chips_multi(name, ps):
    n = len(ps)

    def body(*refs):
        ins, outs = refs[:n], refs[n:2 * n]
        send_sems, recv_sems = refs[2 * n:]
        x, y, c, chips = _position()
        me = 2 * x + y
        sends = []
        for i in range(n):
            for j, (cx, cy) in enumerate(chips):
                cp = _remote(ins[i].at[2 * cx + cy], outs[i].at[me], send_sems, recv_sems, 3 * i + j, (cx, cy, c))
                cp.start()
                sends.append(cp)
        for i in range(n):
            for j, (cx, cy) in enumerate(chips):
                slot = outs[i].at[2 * cx + cy]
                _remote(slot, slot, send_sems, recv_sems, 3 * i + j, (cx, cy, c)).wait_recv()
        for cp in sends:
            cp.wait_send()

    return _pcall(body, in_specs=[HBM_SPEC] * n, out_specs=[HBM_SPEC] * n, out_shape=[SDS(p.shape, p.dtype) for p in ps],
                  scratch_shapes=_dma_sems(3 * n), name=name)(*ps)


def _sum_chips(name, p4, b2, chip):
    _, Rh, W = p4.shape
    tr = _pick(Rh, (256, 128, 32, 16))

    def body(m_ref, own_ref, r1_ref, r2_ref, r3_ref, o_ref):
        o_ref[...] = ((own_ref[...].astype(f32) + r1_ref[...].astype(f32)) + r2_ref[...].astype(f32)) + r3_ref[...].astype(f32)

    other = lambda k: pl.BlockSpec((None, tr, W), lambda i, m_ref: (m_ref[0] ^ k, i, 0))
    grid_spec = pltpu.PrefetchScalarGridSpec(
        num_scalar_prefetch=1, grid=(Rh // tr,),
        in_specs=[pl.BlockSpec((None, tr, W), lambda i, m_ref: (m_ref[0], i, 0)), other(1), other(2), other(3)],
        out_specs=pl.BlockSpec((tr, W), lambda i, m_ref: (i, 0)))
    return _pcall(body, grid_spec=grid_spec, out_shape=SDS((Rh, W), f32), name=name)(chip.reshape(1), p4, b2, b2, b2)


def _swap_multi(name, rs):
    n = len(rs)

    def body(*refs):
        ins, outs = refs[:n], refs[n:2 * n]
        send_sems, recv_sems = refs[2 * n:]
        x, y, c, _ = _position()
        cps = [_remote(ins[i], outs[i], send_sems, recv_sems, i, (x, y, 1 - c)) for i in range(n)]
        for cp in cps:
            cp.start()
        for cp in cps:
            cp.wait()

    return _pcall(body, in_specs=[HBM_SPEC] * n, out_specs=[HBM_SPEC] * n, out_shape=[SDS(r.shape, r.dtype) for r in rs],
                  scratch_shapes=_dma_sems(n), name=name)(*rs)


def _reduce_scatter_multi(slots, c, chip):
    b1 = _swap_halves_multi("rs_swap", slots)
    ps = [_pair_add("rs_pair_%d" % i, g, b, c) for i, (g, b) in enumerate(zip(slots, b1))]
    b2 = _scatter_chips_multi("rs_scatter", ps)
    rs = [_sum_chips("rs_sum_%d" % i, p, b, chip) for i, (p, b) in enumerate(zip(ps, b2))]
    others = _swap_multi("rs_join", rs)
    return [jnp.where(c == 0, jnp.concatenate([r, o], axis=0), jnp.concatenate([o, r], axis=0)) for r, o in zip(rs, others)]


_BIG = ("w_in", "w_dn_out", "w_swa_out", "w_o", "w_up", "w_down")


def _assemble_weights(gs):
    cat1 = lambda g: jnp.concatenate([g[s] for s in range(4)], axis=1)
    rows = lambda g: g.reshape(4 * g.shape[1], g.shape[2])
    return {"w_in": _regroup_w_in(cat1(gs[0])), "w_dn_out": rows(gs[1]), "w_swa_out": rows(gs[2]), "w_o": rows(gs[3]),
            "w_up": cat1(gs[4]), "w_down": rows(gs[5])}


def _grad_slots(gw):
    cols = lambda g: g.reshape(g.shape[0], 4, g.shape[1] // 4).transpose(1, 0, 2)
    rows = lambda g: g.reshape(4, g.shape[0] // 4, g.shape[1])
    return [cols(_ungroup_w_in(gw["w_in"])), rows(gw["w_dn_out"]), rows(gw["w_swa_out"]), rows(gw["w_o"]), gw["w_up"], rows(gw["w_down"])]


_SHARDED = (
    ("w_in", (D, 1860), 1, False),
    ("dn_conv", (4, 768), 1, True),
    ("w_dn_out", (256, D), 0, False),
    ("w_swa_out", (256, D), 0, False),
    ("w_o", (256, D), 0, False),
    ("w_up", (D, 1408), 1, False),
    ("ffn_conv", (3, 704), 1, True),
    ("w_down", (704, D), 0, False),
)


def _pack_weights(shards):
    parts = []
    for nm, shp, _, as_bits in _SHARDED:
        a = shards[nm]
        parts.append(lax.bitcast_convert_type(a, bf16).reshape(-1) if as_bits else a.astype(bf16).reshape(-1))
    flat = jnp.concatenate(parts)
    return jnp.pad(flat, (0, COMM_ROWS * COMM_W - flat.shape[0])).reshape(COMM_ROWS, COMM_W)


def _unpack_weights(g):
    flat = g.reshape(4, -1)
    out, off = {}, 0
    for nm, shp, ax, as_bits in _SHARDED:
        n = int(np.prod(shp)) * (2 if as_bits else 1)
        piece = flat[:, off:off + n]
        off += n
        if as_bits:
            piece = lax.bitcast_convert_type(piece.reshape((4,) + shp + (2,)), f32)
        else:
            piece = piece.reshape((4,) + shp)
        out[nm] = jnp.concatenate([piece[s] for s in range(4)], axis=ax)
    return out


def _pack_grads(grads):
    slots = []
    for s in range(4):
        parts = []
        for nm, shp, ax, _ in _SHARDED:
            n = shp[ax]
            parts.append(lax.slice_in_dim(grads[nm], s * n, (s + 1) * n, axis=ax).reshape(-1))
        flat = jnp.concatenate(parts)
        slots.append(jnp.pad(flat, (0, COMM_ROWS * COMM_W - flat.shape[0])))
    return jnp.stack(slots).reshape(4, COMM_ROWS, COMM_W)


def _unpack_grads(r):
    flat = r.reshape(-1)
    out, off = {}, 0
    for nm, shp, _, _ in _SHARDED:
        n = int(np.prod(shp))
        out[nm] = flat[off:off + n].reshape(shp)
        off += n
    return out


def _regroup_w_in(w):
    z = lambda n: jnp.zeros((w.shape[0], n), w.dtype)
    return jnp.concatenate([w[:, 0:3072], w[:, 3072:4096], w[:, 4112:5136], w[:, 5392:6416], w[:, 6416:7440],
                            w[:, 4096:4112], z(112), w[:, 5136:5264], w[:, 5264:5392], z(128)], axis=1)


def _ungroup_w_in(g):
    return jnp.concatenate([g[:, 0:3072], g[:, 3072:4096], g[:, 7168:7184], g[:, 4096:5120], g[:, 7296:7424], g[:, 7424:7552],
                            g[:, 5120:6144], g[:, 6144:7168]], axis=1)


def _pad_lanes(v, n=LANE):
    return jnp.pad(v, (0, n - v.shape[0])).reshape(1, n)


def _layer_consts(P):
    K = {}
    K["norm_mix"] = P["norm_mix"].reshape(1, D)
    K["norm_ffn"] = P["norm_ffn"].reshape(1, D)
    K["alog"] = _pad_lanes(P["dn_a_log"])
    K["dtb"] = _pad_lanes(P["dn_dt_bias"])
    K["dn_norm"] = P["dn_norm"].reshape(1, LANE)
    K["qn"] = jnp.tile(P["swa_q_norm"], 16).reshape(1, D)
    K["kn"] = jnp.tile(P["swa_k_norm"], 2).reshape(1, LANE)
    K["sinks"] = _pad_lanes(P["swa_sinks"])
    K["ffn_b"] = P["ffn_conv_b"].reshape(1, D_FF)
    return K


def _layer_fwd(x, mod, W, K, tabs, bd):
    sh1, sc1, gt1, sh2, sc2, gt2 = mod
    S = {"x": x}
    (h1,) = _rowwise_fwd("normmod1_fwd", _normmod_fn, [(x, 0, D)], [K["norm_mix"], sc1, sh1], [D], [bf16])
    proj = _matmul("proj_fwd", h1, W["w_in"], "nn", f32)
    qn = _dnconv_fwd("dnconv_q_fwd", proj, CB_Q, W["dn_conv"], True)
    kn = _dnconv_fwd("dnconv_k_fwd", proj, CB_K, W["dn_conv"], True)
    vc = _dnconv_fwd("dnconv_v_fwd", proj, CB_V, W["dn_conv"], False)
    o, sall = _gdn_fwd("gdn_fwd", qn, kn, vc, proj, K["alog"], K["dtb"])
    (on,) = _rowwise_fwd("dngate_fwd", _dngate_fn, [(o, 0, LANE), (proj, 8 * WB_Z, LANE)], [K["dn_norm"]], [LANE], [bf16], nc=8)
    ya = _matmul("dnout_fwd", on, W["w_dn_out"], "nn", f32)
    sq = _qkprep_fwd("qprep_fwd", proj, WB_SWQ, D, K["qn"], bd[0], tabs[0])
    sk = _qkprep_fwd("kprep_fwd", proj, CB_SWK, LANE, K["kn"], bd[1], tabs[1])
    attn = _attn_fwd("attn_fwd", sq, sk, proj, K["sinks"])
    yb = _matmul("swaout_fwd", attn, W["w_swa_out"], "nn", f32)
    (merged,) = _rowwise_fwd("merge_fwd", _merge_fn, [(proj, WB_GA, D), (proj, WB_GB, D), (ya, 0, D), (yb, 0, D)], [], [D], [bf16])
    t1 = _matmul("wo_fwd", merged, W["w_o"], "nn", f32)
    (x1,) = _rowwise_fwd("resid1_fwd", _resid_fn, [(x, 0, D), (t1, 0, D)], [gt1], [D], [f32])
    (h2,) = _rowwise_fwd("normmod2_fwd", _normmod_fn, [(x1, 0, D)], [K["norm_ffn"], sc2, sh2], [D], [bf16])
    up = _matmul("up_fwd", h2, W["w_up"], "nn", f32)
    mid = _ffnact_fwd("ffnact_fwd", up, W["ffn_conv"], K["ffn_b"])
    t2 = _matmul("down_fwd", mid, W["w_down"], "nn", f32)
    (x2,) = _rowwise_fwd("resid2_fwd", _resid_fn, [(x1, 0, D), (t2, 0, D)], [gt2], [D], [f32])
    S.update(h1=h1, proj=proj, qn=qn, kn=kn, vc=vc, o=o, sall=sall, on=on, ya=ya, sq=sq, sk=sk, attn=attn, yb=yb,
             merged=merged, t1=t1, x1=x1, h2=h2, up=up, mid=mid, t2=t2)
    return x2, S


def _layer_bwd(dx2, S, mod, W, K, tabs, bd):
    sh1, sc1, gt1, sh2, sc2, gt2 = mod
    x, x1, proj, up = S["x"], S["x1"], S["proj"], S["up"]
    T = x.shape[0]
    gw, gs = {}, {}
    dt2, dgt2 = _rowwise_bwd("resid2_bwd", _resid_fn, [(x1, 0, D), (S["t2"], 0, D)], [gt2], [(dx2, 0, D)], [None, bf16])
    dmid = _matmul("down_bwd_x", dt2, W["w_down"], "nt", bf16)
    gw["w_down"] = _matmul("down_bwd_w", S["mid"], dt2, "tn", f32)
    dact, dlin, gw["ffn_conv"], dffn_b = _ffnact_bwd("ffnact_bwd", up, W["ffn_conv"], K["ffn_b"], dmid)
    dup = jnp.concatenate([dact, dlin], axis=1)
    dh2 = _matmul("up_bwd_x", dup, W["w_up"], "nt", f32)
    gw["w_up"] = _matmul("up_bwd_w", S["h2"], dup, "tn", f32, out_slots=4)
    dx1, dnorm_ffn, dsc2, dsh2 = _rowwise_bwd("normmod2_bwd", _normmod_fn, [(x1, 0, D)], [K["norm_ffn"], sc2, sh2], [(dh2, 0, D)], [f32],
                                              add_to_first=(dx2, 0, D))
    dt1, dgt1 = _rowwise_bwd("resid1_bwd", _resid_fn, [(x, 0, D), (S["t1"], 0, D)], [gt1], [(dx1, 0, D)], [None, bf16])
    dmerged = _matmul("wo_bwd_x", dt1, W["w_o"], "nt", f32)
    gw["w_o"] = _matmul("wo_bwd_w", S["merged"], dt1, "tn", f32)
    dga, dgb, dya, dyb = _rowwise_bwd("merge_bwd", _merge_fn, [(proj, WB_GA, D), (proj, WB_GB, D), (S["ya"], 0, D), (S["yb"], 0, D)], [],
                                      [(dmerged, 0, D)], [bf16, bf16, bf16, bf16])
    don = _matmul("dnout_bwd_x", dya, W["w_dn_out"], "nt", f32)
    gw["w_dn_out"] = _matmul("dnout_bwd_w", S["on"], dya, "tn", f32)
    do, dz, ddn_norm = _rowwise_bwd("dngate_bwd", _dngate_fn, [(S["o"], 0, LANE), (proj, 8 * WB_Z, LANE)], [K["dn_norm"]], [(don, 0, LANE)],
                                    [f32, bf16], nc=8)
    dqn, dkn, dvc, dab, dalog, ddtb = _gdn_bwd("gdn_bwd", S["qn"], S["kn"], S["vc"], proj, K["alog"], K["dtb"], S["sall"], do)
    dpq, dwq = _dnconv_bwd("dnconv_q_bwd", proj, CB_Q, W["dn_conv"], dqn, True)
    dpk, dwk = _dnconv_bwd("dnconv_k_bwd", proj, CB_K, W["dn_conv"], dkn, True)
    dpv, dwv = _dnconv_bwd("dnconv_v_bwd", proj, CB_V, W["dn_conv"], dvc, False)
    gw["dn_conv"] = jnp.concatenate([dwq, dwk, dwv], axis=1)
    dattn = _matmul("swaout_bwd_x", dyb, W["w_swa_out"], "nt", f32)
    gw["w_swa_out"] = _matmul("swaout_bwd_w", S["attn"], dyb, "tn", f32)
    dsq, dkp, dkc, dvp, dvc_, dsinks = _attn_bwd("attn_bwd", S["sq"], S["sk"], proj, K["sinks"], dattn)
    dsk = _shift_add("attn_dk_join", dkc, dkp, f32)
    dswv = _shift_add("attn_dv_join", dvc_, dvp, bf16)
    dswq, dqn_w = _qkprep_bwd("qprep_bwd", proj, WB_SWQ, D, K["qn"], bd[0], tabs[0], dsq)
    dswk, dkn_w = _qkprep_bwd("kprep_bwd", proj, CB_SWK, LANE, K["kn"], bd[1], tabs[1], dsk)
    dproj = jnp.concatenate([dpq, dpk, dpv, dz, dswq, dga, dgb, dab.astype(bf16), dswk, dswv, jnp.zeros((T, LANE), bf16)], axis=1)
    dh1 = _matmul("proj_bwd_x", dproj, W["w_in"], "nt", f32)
    gw["w_in"] = _matmul("proj_bwd_w", S["h1"], dproj, "tn", f32)
    dx, dnorm_mix, dsc1, dsh1 = _rowwise_bwd("normmod1_bwd", _normmod_fn, [(x, 0, D)], [K["norm_mix"], sc1, sh1], [(dh1, 0, D)], [f32],
                                             add_to_first=(dx1, 0, D))
    gs = {"norm_mix": dnorm_mix[0], "dn_a_log": dalog[0, :8], "dn_dt_bias": ddtb[0, :8], "dn_norm": ddn_norm[0],
          "swa_q_norm": dqn_w.reshape(16, 64).sum(0), "swa_k_norm": dkn_w.reshape(2, 64).sum(0), "swa_sinks": dsinks[0, :16],
          "norm_ffn": dnorm_ffn[0], "ffn_conv_b": dffn_b[0]}
    dmod = jnp.concatenate([dsh1, dsc1, dgt1, dsh2, dsc2, dgt2], axis=1)
    return dx, gw, gs, dmod


def _rope_tables(pos):
    T = pos.shape[0]
    half = 8
    inv = jnp.power(ROPE_THETA, -jnp.arange(half, dtype=f32) / half)
    ang = pos.astype(f32)[:, None] * inv
    cos, sin = jnp.cos(ang), jnp.sin(ang)
    z8, z48, o48 = jnp.zeros((T, 8), f32), jnp.zeros((T, 48), f32), jnp.ones((T, 48), f32)
    c64 = jnp.concatenate([cos, cos, o48], axis=1)
    s1 = jnp.concatenate([-sin, z8, z48], axis=1)
    s2 = jnp.concatenate([z8, sin, z48], axis=1)
    return tuple(jnp.tile(t, (1, 16)) for t in (c64, s1, s2))


_SMALL = (("norm_mix", D), ("dn_a_log", 8), ("dn_dt_bias", 8), ("dn_norm", 128), ("swa_q_norm", 64), ("swa_k_norm", 64),
          ("swa_sinks", 16), ("norm_ffn", D), ("ffn_conv_b", D_FF), ("b_ada", 6 * D))
_CONV = (("dn_conv", 4 * 3072), ("ffn_conv", 3 * D_FF))
_CONV_SHARD = (("dn_conv", 4 * 768), ("ffn_conv", 3 * 704))


def _pack_small(vals, spec):
    flat = jnp.concatenate([vals[nm].reshape(-1) for nm, _ in spec])
    rows = -(-flat.shape[0] // (8 * LANE)) * 8
    return jnp.pad(flat, (0, rows * LANE - flat.shape[0])).reshape(rows, LANE)


def _unpack_small(buf, spec):
    flat = buf.reshape(-1)
    out, off = {}, 0
    for nm, n in spec:
        out[nm] = flat[off:off + DEPTH * n].reshape(DEPTH, n)
        off += DEPTH * n
    return out


def kernel(x, c, positions, w_ada, b_ada, norm_mix, w_in, dn_conv, dn_a_log, dn_dt_bias, dn_norm, w_dn_out, swa_q_norm, swa_k_norm, swa_sinks, w_swa_out, w_o, norm_ffn, w_up, ffn_conv, ffn_conv_b, w_down, loss_target, m_w_ada, m_b_ada, m_norm_mix, m_w_in, m_dn_conv, m_dn_a_log, m_dn_dt_bias, m_dn_norm, m_w_dn_out, m_swa_q_norm, m_swa_k_norm, m_swa_sinks, m_w_swa_out, m_w_o, m_norm_ffn, m_w_up, m_ffn_conv, m_ffn_conv_b, m_w_down, v_w_ada, v_b_ada, v_norm_mix, v_w_in, v_dn_conv, v_dn_a_log, v_dn_dt_bias, v_dn_norm, v_w_dn_out, v_swa_q_norm, v_swa_k_norm, v_swa_sinks, v_w_swa_out, v_w_o, v_norm_ffn, v_w_up, v_ffn_conv, v_ffn_conv_b, v_w_down):
    weights = dict(w_ada=w_ada, b_ada=b_ada, norm_mix=norm_mix, w_in=w_in, dn_conv=dn_conv, dn_a_log=dn_a_log, dn_dt_bias=dn_dt_bias,
                   dn_norm=dn_norm, w_dn_out=w_dn_out, swa_q_norm=swa_q_norm, swa_k_norm=swa_k_norm, swa_sinks=swa_sinks,
                   w_swa_out=w_swa_out, w_o=w_o, norm_ffn=norm_ffn, w_up=w_up, ffn_conv=ffn_conv, ffn_conv_b=ffn_conv_b, w_down=w_down)
    mom_m = dict(w_ada=m_w_ada, b_ada=m_b_ada, norm_mix=m_norm_mix, w_in=m_w_in, dn_conv=m_dn_conv, dn_a_log=m_dn_a_log,
                 dn_dt_bias=m_dn_dt_bias, dn_norm=m_dn_norm, w_dn_out=m_w_dn_out, swa_q_norm=m_swa_q_norm, swa_k_norm=m_swa_k_norm,
                 swa_sinks=m_swa_sinks, w_swa_out=m_w_swa_out, w_o=m_w_o, norm_ffn=m_norm_ffn, w_up=m_w_up, ffn_conv=m_ffn_conv,
                 ffn_conv_b=m_ffn_conv_b, w_down=m_w_down)
    mom_v = dict(w_ada=v_w_ada, b_ada=v_b_ada, norm_mix=v_norm_mix, w_in=v_w_in, dn_conv=v_dn_conv, dn_a_log=v_dn_a_log,
                 dn_dt_bias=v_dn_dt_bias, dn_norm=v_dn_norm, w_dn_out=v_w_dn_out, swa_q_norm=v_swa_q_norm, swa_k_norm=v_swa_k_norm,
                 swa_sinks=v_swa_sinks, w_swa_out=v_w_swa_out, w_o=v_w_o, norm_ffn=v_norm_ffn, w_up=v_w_up, ffn_conv=v_ffn_conv,
                 ffn_conv_b=v_ffn_conv_b, w_down=v_w_down)
    order = ["w_ada", "b_ada", "norm_mix", "w_in", "dn_conv", "dn_a_log", "dn_dt_bias", "dn_norm", "w_dn_out", "swa_q_norm",
             "swa_k_norm", "swa_sinks", "w_swa_out", "w_o", "norm_ffn", "w_up", "ffn_conv", "ffn_conv_b", "w_down"]
    ax, ay, ac = lax.axis_index("x"), lax.axis_index("y"), lax.axis_index("c")
    chip = 2 * ax + ay
    dev = 4 * ax + 2 * ay + ac
    T = x.shape[1]
    xs = x[0]

    c_all = _allgather_all("gather_c", jnp.pad(c, ((0, 7), (0, 0)))).reshape(8, 8, D)[:, 0]
    c_act = _silu_rows("silu_c", jnp.pad(c_all, ((0, 8), (0, 0))))
    mod_sh = jnp.stack([
        _matmul("mod_fwd", c_act, w_ada[l].astype(bf16), "nn", f32,
                bias=lax.dynamic_slice(b_ada[l], (chip * 1536,), (1536,)).reshape(1, 1536)) for l in range(DEPTH)])
    mod_all = _allgather_all("gather_mod", mod_sh.reshape(DEPTH * 16 * 12, LANE)).reshape(8, DEPTH, 16, 1536)
    mod_me = jnp.concatenate([lax.dynamic_index_in_dim(mod_all[2 * s], dev, axis=1, keepdims=False) for s in range(4)], axis=1)

    tabs_q = _rope_tables(positions[0])
    tabs = (tabs_q, tuple(t[:, :LANE] for t in tabs_q))
    head = jnp.arange(D) // 64
    bd_q = (head[:, None] == head[None, :]).astype(f32) / 64.0
    bd = (bd_q, bd_q[:LANE, :LANE])

    conv_all = _allgather_all("gather_conv", _pack_small({"dn_conv": dn_conv, "ffn_conv": ffn_conv}, _CONV_SHARD))
    conv_parts = [_unpack_small(conv_all[2 * s], _CONV_SHARD) for s in range(4)]
    dn_conv_full = jnp.concatenate([p["dn_conv"].reshape(DEPTH, 4, 768) for p in conv_parts], axis=2)
    ffn_conv_full = jnp.concatenate([p["ffn_conv"].reshape(DEPTH, 3, 704) for p in conv_parts], axis=2)

    saved, Ws, Ks, mods = [], [], [], []
    h = xs
    for l in range(DEPTH):
        shards = [weights[nm][l].astype(bf16) for nm in _BIG]
        gathered = _gather_chips("gather_w", shards)
        gathered = [lax.dynamic_update_index_in_dim(g, s, chip, 0) for g, s in zip(gathered, shards)]
        W = _assemble_weights(gathered)
        W["dn_conv"], W["ffn_conv"] = dn_conv_full[l], ffn_conv_full[l]
        K = _layer_consts({nm: weights[nm][l] for nm in ("norm_mix", "norm_ffn", "dn_a_log", "dn_dt_bias", "dn_norm", "swa_q_norm",
                                                          "swa_k_norm", "swa_sinks", "ffn_conv_b")})
        mod = tuple(mod_me[l, k * D:(k + 1) * D].reshape(1, D) for k in range(6))
        h, S = _layer_fwd(h, mod, W, K, tabs, bd)
        saved.append(S), Ws.append(W), Ks.append(K), mods.append(mod)

    loss_blk, dh = _loss("loss", h, loss_target[0])
    loss = lax.psum(loss_blk[0, 0], ("x", "y", "c"))

    grad_sh = [None] * DEPTH
    small = [None] * DEPTH
    dmods = [None] * DEPTH
    for l in reversed(range(DEPTH)):
        dh, gw, gs, dmod = _layer_bwd(dh, saved[l], mods[l], Ws[l], Ks[l], tabs, bd)
        grad_sh[l] = dict(zip(_BIG, _reduce_scatter_multi(_grad_slots(gw), ac, chip)))
        small[l], dmods[l] = dict(gs, dn_conv=gw["dn_conv"], ffn_conv=gw["ffn_conv"]), dmod[0]

    spec_g = _SMALL + _CONV
    vals = {nm: jnp.stack([small[l][nm] for l in range(DEPTH)]) for nm, _ in spec_g if nm != "b_ada"}
    vals["b_ada"] = jnp.stack(dmods)
    small_all = _allgather_all("gather_small", _pack_small(vals, spec_g))
    g_small = _unpack_small(_sum_leading("sum_small", small_all), spec_g)
    dmod_all = jnp.stack([_unpack_small(small_all[d], spec_g)["b_ada"] for d in range(8)])
    dmod_sh = lax.dynamic_slice(dmod_all, (0, 0, chip * 1536), (8, DEPTH, 1536))
    dmod_sh = jnp.pad(dmod_sh, ((0, 8), (0, 0), (0, 0))).astype(bf16)
    g_w_ada = jnp.stack([_matmul("mod_bwd_w", c_act, dmod_sh[:, l], "tn", f32) for l in range(DEPTH)])

    grads = {nm: g_small[nm] for nm, _ in _SMALL}
    grads["dn_conv"] = lax.dynamic_slice(g_small["dn_conv"].reshape(DEPTH, 4, 3072), (0, 0, chip * 768), (DEPTH, 4, 768))
    grads["ffn_conv"] = lax.dynamic_slice(g_small["ffn_conv"].reshape(DEPTH, 3, D_FF), (0, 0, chip * 704), (DEPTH, 3, 704))
    grads["w_ada"] = g_w_ada
    for nm in _BIG:
        grads[nm] = jnp.stack([grad_sh[l][nm] for l in range(DEPTH)])

    delta, new_m, new_v = {}, {}, {}
    for nm in ("w_ada", "dn_conv", "ffn_conv") + _BIG:
        delta[nm], new_m[nm], new_v[nm] = _adamw("adamw_" + nm, weights[nm], grads[nm], mom_m[nm], mom_v[nm])
    sm = [_pack_small({nm: t[nm] for nm, _ in _SMALL}, _SMALL) for t in (weights, grads, mom_m, mom_v)]
    for tgt, buf in zip((delta, new_m, new_v), _adamw("adamw_small", *sm)):
        tgt.update(_unpack_small(buf, _SMALL))

    return (loss, dh[None], *[grads[n] for n in order], *[delta[n] for n in order], *[new_m[n] for n in order], *[new_v[n] for n in order])
```

```python
import functools

import jax
import jax.numpy as jnp
import numpy as np
from jax import lax
from jax.experimental import pallas as pl
from jax.experimental.pallas import tpu as pltpu

f32 = jnp.float32
bf16 = jnp.bfloat16
SDS = jax.ShapeDtypeStruct
HI = lax.Precision.HIGHEST
MESH = pl.DeviceIdType.MESH

D = 1024
DEPTH = 4
EPS = 1e-6
DN_C = 64
SWA_B = 128
LANE = 128
ROPE_THETA = 500000.0
D_FF = 2816
IN_TOTAL = 7440
PROJ_W = 7680
CB_Q, CB_K, CB_V = 0, 8, 16
CB_AB, CB_SWK, CB_SWV = 56, 57, 58
WB_Z, WB_SWQ, WB_GA, WB_GB = 3, 4, 5, 6
TR = 256
COMM_W = 1024
COMM_ROWS = 4864
COMM_TR = 128
VMEM_BIG = 48 * 2 ** 20

ADAM_LR, ADAM_B1, ADAM_B2, ADAM_EPS, ADAM_WD, ADAM_STEP = 0.001, 0.9, 0.999, 1e-08, 0.01, 10


def _pcall(body, **kw):
    return pl.pallas_call(body, **kw)


def _cparams(vmem=None):
    return pltpu.CompilerParams(vmem_limit_bytes=vmem) if vmem else None


def _dot(a, b, ca, cb, precision=HI):
    return lax.dot_general(a, b, (((ca,), (cb,)), ((), ())), precision=precision, preferred_element_type=f32)


def _pick(n, cands):
    for c in cands:
        if n % c == 0:
            return c
    return n


def _tile(n, cap):
    if n <= cap:
        return n
    best = None
    for t in range(LANE, cap + 1, LANE):
        if n % t == 0:
            best = t
    assert best is not None, (n, cap)
    return best


def _matmul(name, a, b, mode, out_dtype, bias=None, out_slots=None):
    if mode == "nn":
        (M, K), (_, N) = a.shape, b.shape
    elif mode == "nt":
        (M, K), (N, _) = a.shape, b.shape
    else:
        (K, M), (_, N) = a.shape, b.shape
    tm = _tile(M, 1536 if mode == "tn" else 1024)
    tn = N // out_slots if out_slots else _tile(N, 1536)
    tk = _tile(K, 512 if mode == "tn" else (1024 if K <= 1024 else 1536))
    nk = K // tk
    ca, cb = {"nn": (1, 0), "nt": (1, 1), "tn": (0, 0)}[mode]

    def body(*refs):
        a_ref, b_ref = refs[:2]
        bias_ref = refs[2] if bias is not None else None
        o_ref = refs[3 if bias is not None else 2]

        def finish(r):
            if bias is not None:
                r = r + bias_ref[...]
            o_ref[...] = r.astype(o_ref.dtype)

        part = _dot(a_ref[...].astype(bf16), b_ref[...].astype(bf16), ca, cb, precision=None)
        if nk == 1:
            finish(part)
        else:
            acc = refs[-1]
            k = pl.program_id(2)

            @pl.when(k == 0)
            def _():
                acc[...] = part

            @pl.when(k > 0)
            def _():
                acc[...] += part

            @pl.when(k == nk - 1)
            def _():
                finish(acc[...])

    a_spec = pl.BlockSpec((tk, tm), lambda i, j, k: (k, i)) if mode == "tn" else pl.BlockSpec((tm, tk), lambda i, j, k: (i, k))
    b_spec = pl.BlockSpec((tn, tk), lambda i, j, k: (j, k)) if mode == "nt" else pl.BlockSpec((tk, tn), lambda i, j, k: (k, j))
    in_specs = [a_spec, b_spec]
    args = [a, b]
    if bias is not None:
        in_specs.append(pl.BlockSpec((1, tn), lambda i, j, k: (0, j)))
        args.append(bias)
    if out_slots:
        out_spec = pl.BlockSpec((None, tm, tn), lambda i, j, k: (j, i, 0))
        out_shape = SDS((out_slots, M, tn), out_dtype)
    else:
        out_spec = pl.BlockSpec((tm, tn), lambda i, j, k: (i, j))
        out_shape = SDS((M, N), out_dtype)
    return _pcall(
        body, grid=(M // tm, N // tn, nk), in_specs=in_specs, out_specs=out_spec, out_shape=out_shape,
        scratch_shapes=[pltpu.VMEM((tm, tn), f32)] if nk > 1 else [], compiler_params=_cparams(VMEM_BIG), name=name)(*args)


def _row_specs(rows, tr):
    return [pl.BlockSpec((tr, w), lambda i, j, off=off: (i, off + j)) for (_, off, w) in rows]


def _rowwise_fwd(name, fn, rows, vecs, out_widths, out_dtypes, nc=1, tr=TR):
    T = rows[0][0].shape[0]
    n_in = len(rows) + len(vecs)

    def body(*refs):
        vals = [r[...].astype(f32) for r in refs[:n_in]]
        res = fn(*vals)
        for o_ref, r in zip(refs[n_in:], res):
            o_ref[...] = r.astype(o_ref.dtype)

    in_specs = _row_specs(rows, tr) + [pl.BlockSpec(v.shape, lambda i, j: (0, 0)) for v in vecs]
    out_specs = [pl.BlockSpec((tr, w), lambda i, j: (i, j)) for w in out_widths]
    out_shape = [SDS((T, w * nc), dt) for w, dt in zip(out_widths, out_dtypes)]
    return _pcall(body, grid=(T // tr, nc), in_specs=in_specs, out_specs=out_specs, out_shape=out_shape, name=name)(
        *[r[0] for r in rows], *vecs)


def _rowwise_bwd(name, fn, rows, vecs, cts, drow_dtypes, nc=1, tr=TR, add_to_first=None):
    T = rows[0][0].shape[0]
    n_r, n_v, n_c = len(rows), len(vecs), len(cts)
    n_add = 0 if add_to_first is None else 1
    keep = [k for k, dt in enumerate(drow_dtypes) if dt is not None]

    def body(*refs):
        n_in = n_r + n_v + n_c + n_add
        vals = [r[...].astype(f32) for r in refs[:n_in]]
        outs = refs[n_in:]
        i, j = pl.program_id(0), pl.program_id(1)
        _, vjp = jax.vjp(fn, *vals[:n_r + n_v])
        grads = vjp(tuple(vals[n_r + n_v:n_r + n_v + n_c]))
        for pos, k in enumerate(keep):
            g = grads[k]
            if n_add and pos == 0:
                g = g + vals[-1]
            outs[pos][...] = g.astype(outs[pos].dtype)

        @pl.when((i == 0) & (j == 0))
        def _():
            for q in range(n_v):
                outs[len(keep) + q][...] = jnp.zeros_like(outs[len(keep) + q])

        for q in range(n_v):
            outs[len(keep) + q][...] += grads[n_r + q]

    extra = [] if add_to_first is None else [add_to_first]
    in_specs = (_row_specs(rows, tr) + [pl.BlockSpec(v.shape, lambda i, j: (0, 0)) for v in vecs]
                + _row_specs(cts, tr) + _row_specs(extra, tr))
    out_specs = [pl.BlockSpec((tr, rows[k][2]), lambda i, j: (i, j)) for k in keep]
    out_specs += [pl.BlockSpec(v.shape, lambda i, j: (0, 0)) for v in vecs]
    out_shape = [SDS((T, rows[k][2] * nc), drow_dtypes[k]) for k in keep] + [SDS(v.shape, f32) for v in vecs]
    return _pcall(body, grid=(T // tr, nc), in_specs=in_specs, out_specs=out_specs, out_shape=out_shape, name=name)(
        *[r[0] for r in rows], *vecs, *[c[0] for c in cts], *[e[0] for e in extra])


def _normmod_fn(x, w, sc, sh):
    y = x * lax.rsqrt(jnp.mean(x * x, axis=-1, keepdims=True) + EPS)
    return ((y * w) * (1.0 + sc) + sh,)


def _resid_fn(x, t, gt):
    return (x + gt * t,)


def _merge_fn(ga, gb, ya, yb):
    return (jax.nn.sigmoid(ga) * ya + jax.nn.sigmoid(gb) * yb,)


def _dngate_fn(o, z, w):
    y = o * lax.rsqrt(jnp.mean(o * o, axis=-1, keepdims=True) + EPS)
    return ((y * w) * (z * jax.nn.sigmoid(z)),)


def _shift_down(x, s):
    if s == 0:
        return x
    t = lax.broadcasted_iota(jnp.int32, x.shape, 0)
    return jnp.where(t >= s, pltpu.roll(x, s, 0), 0.0)


def _shift_up(x, s):
    if s == 0:
        return x
    n = x.shape[0]
    t = lax.broadcasted_iota(jnp.int32, x.shape, 0)
    return jnp.where(t < n - s, pltpu.roll(x, n - s, 0), 0.0)


def _conv_taps(x, w_ref, taps):
    acc = x * w_ref[taps - 1:taps, :]
    for s in range(1, taps):
        acc = acc + _shift_down(x, s) * w_ref[taps - 1 - s:taps - s, :]
    return acc


def _conv_taps_bwd(x, dy, w_ref, dw_ref, taps):
    dx = dy * w_ref[taps - 1:taps, :]
    dw_ref[taps - 1:taps, :] = jnp.sum(dy * x, axis=0, keepdims=True)
    for s in range(1, taps):
        dx = dx + _shift_up(dy, s) * w_ref[taps - 1 - s:taps - s, :]
        dw_ref[taps - 1 - s:taps - s, :] = jnp.sum(dy * _shift_down(x, s), axis=0, keepdims=True)
    return dx


def _dn_act(y, normalize):
    s = y * jax.nn.sigmoid(y)
    if normalize:
        s = s * lax.rsqrt(jnp.sum(s * s, axis=-1, keepdims=True) + EPS)
    return s


def _dnconv_fwd(name, proj, cb, w, normalize):
    T = proj.shape[0]

    def body(x_ref, w_ref, o_ref):
        o_ref[...] = _dn_act(_conv_taps(x_ref[...], w_ref, 4), normalize)

    return _pcall(
        body, grid=(8,), in_specs=[pl.BlockSpec((T, LANE), lambda j: (0, cb + j)), pl.BlockSpec((4, LANE), lambda j: (0, cb + j))],
        out_specs=pl.BlockSpec((T, LANE), lambda j: (0, j)), out_shape=SDS((T, 1024), f32), compiler_params=_cparams(VMEM_BIG), name=name)(proj, w)


def _dnconv_bwd(name, proj, cb, w, dout, normalize):
    T = proj.shape[0]

    def body(x_ref, w_ref, do_ref, dx_ref, dw_ref):
        x = x_ref[...]
        y = _conv_taps(x, w_ref, 4)
        _, vjp = jax.vjp(functools.partial(_dn_act, normalize=normalize), y)
        (dy,) = vjp(do_ref[...])
        dx_ref[...] = _conv_taps_bwd(x, dy, w_ref, dw_ref, 4).astype(dx_ref.dtype)

    return _pcall(
        body, grid=(8,),
        in_specs=[pl.BlockSpec((T, LANE), lambda j: (0, cb + j)), pl.BlockSpec((4, LANE), lambda j: (0, cb + j)),
                  pl.BlockSpec((T, LANE), lambda j: (0, j))],
        out_specs=[pl.BlockSpec((T, LANE), lambda j: (0, j)), pl.BlockSpec((4, LANE), lambda j: (0, j))],
        out_shape=[SDS((T, 1024), bf16), SDS((4, 1024), f32)], compiler_params=_cparams(VMEM_BIG), name=name)(proj, w, dout)


def _ffn_point(a, lin):
    return a * jax.nn.sigmoid(a) * lin


def _ffnact_fwd(name, up, w, b):
    T = up.shape[0]
    nblk = D_FF // LANE

    def body(a_ref, l_ref, w_ref, b_ref, o_ref):
        a = _conv_taps(a_ref[...], w_ref, 3) + b_ref[...]
        o_ref[...] = _ffn_point(a, l_ref[...]).astype(o_ref.dtype)

    return _pcall(
        body, grid=(nblk,),
        in_specs=[pl.BlockSpec((T, LANE), lambda j: (0, j)), pl.BlockSpec((T, LANE), lambda j: (0, nblk + j)),
                  pl.BlockSpec((3, LANE), lambda j: (0, j)), pl.BlockSpec((1, LANE), lambda j: (0, j))],
        out_specs=pl.BlockSpec((T, LANE), lambda j: (0, j)), out_shape=SDS((T, D_FF), bf16), compiler_params=_cparams(VMEM_BIG), name=name)(up, up, w, b)


def _ffnact_bwd(name, up, w, b, dmid):
    T = up.shape[0]
    nblk = D_FF // LANE

    def body(a_ref, l_ref, w_ref, b_ref, dm_ref, da_ref, dl_ref, dw_ref, db_ref):
        x = a_ref[...]
        a = _conv_taps(x, w_ref, 3) + b_ref[...]
        _, vjp = jax.vjp(_ffn_point, a, l_ref[...])
        da, dl = vjp(dm_ref[...].astype(f32))
        dl_ref[...] = dl.astype(dl_ref.dtype)
        db_ref[...] = jnp.sum(da, axis=0, keepdims=True)
        da_ref[...] = _conv_taps_bwd(x, da, w_ref, dw_ref, 3).astype(da_ref.dtype)

    col = lambda r: pl.BlockSpec((r, LANE), lambda j: (0, j))
    return _pcall(
        body, grid=(nblk,),
        in_specs=[col(T), pl.BlockSpec((T, LANE), lambda j: (0, nblk + j)), col(3), col(1), col(T)],
        out_specs=[col(T), col(T), col(3), col(1)],
        out_shape=[SDS((T, D_FF), bf16), SDS((T, D_FF), bf16), SDS((3, D_FF), f32), SDS((1, D_FF), f32)],
        compiler_params=_cparams(VMEM_BIG), name=name)(up, up, w, b, dmid)


def _bmm(a, b, ca, cb, precision=HI):
    return lax.dot_general(a, b, (((ca,), (cb,)), ((0,), (0,))), precision=precision, preferred_element_type=f32)


def _make_bdot(ca, cb):
    def raw(x, y, cx, cy):
        return _bmm(x.astype(bf16), y.astype(bf16), cx, cy, precision=None)

    @jax.custom_vjp
    def f(a, b):
        return raw(a, b, ca, cb)

    def fwd(a, b):
        return raw(a, b, ca, cb), (a, b)

    def bwd(res, dy):
        a, b = res
        if (ca, cb) == (2, 1):
            return raw(dy, b, 2, 2), raw(a, dy, 1, 1)
        if (ca, cb) == (2, 2):
            return raw(dy, b, 2, 1), raw(dy, a, 1, 1)
        return raw(b, dy, 2, 2), raw(a, dy, 2, 1)

    f.defvjp(fwd, bwd)
    return f


_bdot_nn, _bdot_nt, _bdot_tn = _make_bdot(2, 1), _make_bdot(2, 2), _make_bdot(1, 1)


def _pieces(a, n):
    out, r = [], a
    for _ in range(n):
        p = r.astype(bf16)
        out.append(p)
        r = r - p.astype(f32)
    return out


def _bmm_split(x, y, cx, cy, nx=2, ny=2, order=1):
    xs, ys = _pieces(x, nx), _pieces(y, ny)
    acc = None
    for i in reversed(range(nx)):
        for j in reversed(range(ny)):
            if i + j <= order:
                t = _bmm(xs[i], ys[j], cx, cy, precision=None)
                acc = t if acc is None else acc + t
    return acc


@jax.custom_vjp
def _solve_apply(X, r):
    return _bmm_split(X, r, 2, 1)


def _solve_apply_fwd(X, r):
    return _bmm_split(X, r, 2, 1), (X, r)


def _solve_apply_bwd(res, dy):
    X, r = res
    return _bmm_split(dy, r, 2, 2), _bmm_split(X, dy, 1, 1)


_solve_apply.defvjp(_solve_apply_fwd, _solve_apply_bwd)


def _lower_ones(H, C):
    ri = lax.broadcasted_iota(jnp.int32, (H, C, C), 1)
    ci = lax.broadcasted_iota(jnp.int32, (H, C, C), 2)
    return (ri >= ci).astype(f32)


def _cumsum_rows_raw(G):
    return _bmm_split(_lower_ones(G.shape[0], G.shape[1]), G, 2, 1, nx=1, ny=3, order=2)


@jax.custom_vjp
def _cumsum_rows(G):
    return _cumsum_rows_raw(G)


def _cumsum_rows_fwd(G):
    return _cumsum_rows_raw(G), None


def _cumsum_rows_bwd(_, dy):
    return (_bmm_split(_lower_ones(dy.shape[0], dy.shape[1]), dy, 1, 1, nx=1, ny=3, order=2),)


_cumsum_rows.defvjp(_cumsum_rows_fwd, _cumsum_rows_bwd)


def _tri_inverse_raw(L):
    H, C, _ = L.shape
    ri = lax.broadcasted_iota(jnp.int32, (C, C), 0)
    ci = lax.broadcasted_iota(jnp.int32, (C, C), 1)
    X = jnp.broadcast_to((ri == ci).astype(f32)[None], (H, C, C))
    for lg in range(C.bit_length() - 1):
        same = (ri >> (lg + 1)) == (ci >> (lg + 1))
        lower_left = same & (((ri >> lg) & 1) == 1) & (((ci >> lg) & 1) == 0)
        X = X - _bmm_split(_bmm_split(X, jnp.where(lower_left[None], L, 0.0), 2, 1), X, 2, 1)
    return X


@jax.custom_vjp
def _tri_inverse(L):
    return _tri_inverse_raw(L)


def _tri_inverse_fwd(L):
    X = _tri_inverse_raw(L)
    return X, X


def _tri_inverse_bwd(X, dX):
    return (-_bmm_split(_bmm_split(X, dX, 1, 1), X, 2, 2),)


_tri_inverse.defvjp(_tri_inverse_fwd, _tri_inverse_bwd)


def _gdn_chunk(q, k, v, ab, alog, dtb, S):
    H, C, _ = q.shape
    lane = lax.broadcasted_iota(jnp.int32, (H, C, LANE), 2)
    head = lax.broadcasted_iota(jnp.int32, (H, C, LANE), 0)
    abb = jnp.broadcast_to(ab[None], (H, C, LANE))
    a = jnp.sum(jnp.where(lane == head, abb, 0.0), axis=2, keepdims=True)
    b = jnp.sum(jnp.where(lane == head + 8, abb, 0.0), axis=2, keepdims=True)
    pick = lax.broadcasted_iota(jnp.int32, (H, 1, LANE), 2) == lax.broadcasted_iota(jnp.int32, (H, 1, LANE), 0)
    al = jnp.sum(jnp.where(pick, alog[None], 0.0), axis=2, keepdims=True)
    db = jnp.sum(jnp.where(pick, dtb[None], 0.0), axis=2, keepdims=True)
    g = -jnp.exp(al) * jax.nn.softplus(a + db)
    beta = jax.nn.sigmoid(b)
    ri = lax.broadcasted_iota(jnp.int32, (C, C), 0)
    ci = lax.broadcasted_iota(jnp.int32, (C, C), 1)
    G = jnp.broadcast_to(g, (H, C, LANE))
    gc = _cumsum_rows(G)
    gi = _cumsum_rows(jnp.broadcast_to(g, (H, C, C)))
    decay = jnp.exp(jnp.where((ri >= ci)[None], gi - jnp.swapaxes(gi, 1, 2), -jnp.inf))
    qs = q * (LANE ** -0.5)
    kb = k * beta
    X = _tri_inverse(jnp.where((ri > ci)[None], _bdot_nt(kb, k) * decay, 0.0))
    egc = jnp.exp(gc)
    u = _solve_apply(X, v * beta)
    w = _solve_apply(X, kb * egc)
    qk = _bdot_nt(qs, k) * decay
    g_last = jnp.sum(G, axis=1, keepdims=True)
    k_dec = k * jnp.exp(g_last - gc)
    v_new = u - _bdot_nn(w, S)
    o = _bdot_nn(qs * egc, S) + _bdot_nn(qk, v_new)
    S_new = S * jnp.exp(g_last) + _bdot_tn(k_dec, v_new)
    return o, S_new


def _heads(ref):
    return jnp.stack([ref[:, LANE * h:LANE * (h + 1)] for h in range(8)], axis=0)


def _put_heads(ref, val):
    for h in range(8):
        ref[:, LANE * h:LANE * (h + 1)] = val[h]


def _hosted_parts(hosted):
    if hosted is None:
        return [], [], [], []
    n = len(hosted["arrays"])
    return list(hosted["arrays"]), [HBM_SPEC] * n, list(hosted["out_shape"]), _dma_sems(hosted["n_sems"])


def _gdn_fwd(name, q, k, v, proj, alog, dtb, hosted=None):
    T = q.shape[0]
    N = T // DN_C
    h_args, h_specs, h_shapes, h_sems = _hosted_parts(hosted)
    nh = len(h_args)

    def body(q_ref, k_ref, v_ref, ab_ref, al_ref, dt_ref, *rest):
        h_ins, (o_ref, sall_ref), h_outs, s_scr, sems = rest[:nh], rest[nh:nh + 2], rest[nh + 2:2 * nh + 2], rest[2 * nh + 2], rest[2 * nh + 3:]
        step = pl.program_id(0)

        @pl.when(step == 0)
        def _():
            s_scr[...] = jnp.zeros_like(s_scr)
            if hosted is not None:
                hosted["start"](h_ins, h_outs, *sems)

        S = s_scr[...]
        sall_ref[...] = S
        o, S_new = _gdn_chunk(_heads(q_ref), _heads(k_ref), _heads(v_ref), ab_ref[...], al_ref[...], dt_ref[...], S)
        _put_heads(o_ref, o)
        s_scr[...] = S_new

        if hosted is not None:
            @pl.when(step == N - 1)
            def _():
                hosted["finish"](h_ins, h_outs, *sems)

    blk = pl.BlockSpec((DN_C, 8 * LANE), lambda n: (n, 0))
    vec = pl.BlockSpec((1, LANE), lambda n: (0, 0))
    state = pl.BlockSpec((None, 8, LANE, LANE), lambda n: (n, 0, 0, 0))
    outs = _pcall(
        body, grid=(N,), in_specs=[blk, blk, blk, pl.BlockSpec((DN_C, LANE), lambda n: (n, CB_AB)), vec, vec] + h_specs,
        out_specs=[blk, state] + h_specs, out_shape=[SDS((T, 1024), f32), SDS((N, 8, LANE, LANE), f32)] + h_shapes,
        scratch_shapes=[pltpu.VMEM((8, LANE, LANE), f32)] + h_sems, name=name)(q, k, v, proj, alog, dtb, *h_args)
    return outs[0], outs[1], list(outs[2:])


def _gdn_bwd(name, q, k, v, proj, alog, dtb, sall, do, hosted=None):
    T = q.shape[0]
    N = T // DN_C
    h_args, h_specs, h_shapes, h_sems = _hosted_parts(hosted)
    nh = len(h_args)

    def body(q_ref, k_ref, v_ref, ab_ref, al_ref, dt_ref, s_ref, do_ref, *rest):
        h_ins, h_outs, ds_scr, sems = rest[:nh], rest[nh + 6:2 * nh + 6], rest[2 * nh + 6], rest[2 * nh + 7:]
        dq_ref, dk_ref, dv_ref, dab_ref, dal_ref, ddt_ref = rest[nh:nh + 6]
        step = pl.program_id(0)

        @pl.when(step == 0)
        def _():
            ds_scr[...] = jnp.zeros_like(ds_scr)
            dal_ref[...] = jnp.zeros_like(dal_ref)
            ddt_ref[...] = jnp.zeros_like(ddt_ref)
            if hosted is not None:
                hosted["start"](h_ins, h_outs, *sems)

        _, vjp = jax.vjp(_gdn_chunk, _heads(q_ref), _heads(k_ref), _heads(v_ref), ab_ref[...], al_ref[...], dt_ref[...], s_ref[...])
        dq, dk, dv, dab, dal, ddt, dS = vjp((_heads(do_ref), ds_scr[...]))
        _put_heads(dq_ref, dq)
        _put_heads(dk_ref, dk)
        _put_heads(dv_ref, dv)
        ds_scr[...] = dS
        dab_ref[...] = dab
        dal_ref[...] += dal
        ddt_ref[...] += ddt

        if hosted is not None:
            @pl.when(step == N - 1)
            def _():
                hosted["finish"](h_ins, h_outs, *sems)

    blk = pl.BlockSpec((DN_C, 8 * LANE), lambda n: (N - 1 - n, 0))
    vec = pl.BlockSpec((1, LANE), lambda n: (0, 0))
    state = pl.BlockSpec((None, 8, LANE, LANE), lambda n: (N - 1 - n, 0, 0, 0))
    outs = _pcall(
        body, grid=(N,),
        in_specs=[blk, blk, blk, pl.BlockSpec((DN_C, LANE), lambda n: (N - 1 - n, CB_AB)), vec, vec, state, blk] + h_specs,
        out_specs=[blk, blk, blk, pl.BlockSpec((DN_C, LANE), lambda n: (N - 1 - n, 0)), vec, vec] + h_specs,
        out_shape=[SDS((T, 1024), f32)] * 3 + [SDS((T, LANE), f32), SDS((1, LANE), f32), SDS((1, LANE), f32)] + h_shapes,
        scratch_shapes=[pltpu.VMEM((8, LANE, LANE), f32)] + h_sems, name=name)(q, k, v, proj, alog, dtb, sall, do, *h_args)
    return tuple(outs[:6]), list(outs[6:])


def _segmean_raw(x2, bd):
    return jnp.concatenate([_dot(x2[:, LANE * j:LANE * (j + 1)], bd, 1, 0) for j in range(x2.shape[1] // LANE)], axis=1)


@jax.custom_vjp
def _segmean(x2, bd):
    return _segmean_raw(x2, bd)


def _segmean_fwd(x2, bd):
    return _segmean_raw(x2, bd), bd


def _segmean_bwd(bd, dy):
    return _segmean_raw(dy, bd), jnp.zeros_like(bd)


_segmean.defvjp(_segmean_fwd, _segmean_bwd)


def _qknorm_fn(x, w, bd):
    return x * lax.rsqrt(_segmean(x * x, bd) + EPS) * w


def _rope_apply(xn, c, s1, s2):
    W = xn.shape[1]
    return xn * c + pltpu.roll(xn, W - 8, 1) * s1 + pltpu.roll(xn, 8, 1) * s2


def _rope_apply_t(d, c, s1, s2):
    W = d.shape[1]
    return d * c + pltpu.roll(d * s1, 8, 1) + pltpu.roll(d * s2, W - 8, 1)


def _qkprep_fwd(name, proj, wb, width, w, bd, tabs):
    T = proj.shape[0]
    tr = 128

    def body(x_ref, w_ref, bd_ref, c_ref, s1_ref, s2_ref, o_ref):
        xn = _qknorm_fn(x_ref[...], w_ref[...], bd_ref[...])
        o_ref[...] = _rope_apply(xn, c_ref[...], s1_ref[...], s2_ref[...])

    row0 = pl.BlockSpec((tr, width), lambda i: (i, 0))
    full = lambda a: pl.BlockSpec(a.shape, lambda i: (0, 0))
    return _pcall(
        body, grid=(T // tr,), in_specs=[pl.BlockSpec((tr, width), lambda i: (i, wb)), full(w), full(bd), row0, row0, row0],
        out_specs=row0, out_shape=SDS((T, width), f32), name=name)(proj, w, bd, *tabs)


def _qkprep_bwd(name, proj, wb, width, w, bd, tabs, dout):
    T = proj.shape[0]
    tr = 128

    def body(x_ref, w_ref, bd_ref, c_ref, s1_ref, s2_ref, do_ref, dx_ref, dw_ref):
        i = pl.program_id(0)
        dxn = _rope_apply_t(do_ref[...], c_ref[...], s1_ref[...], s2_ref[...])
        bd = bd_ref[...]
        _, vjp = jax.vjp(lambda x, w_: _qknorm_fn(x, w_, bd), x_ref[...], w_ref[...])
        dx, dw = vjp(dxn)
        dx_ref[...] = dx.astype(dx_ref.dtype)

        @pl.when(i == 0)
        def _():
            dw_ref[...] = jnp.zeros_like(dw_ref)

        dw_ref[...] += dw

    row0 = pl.BlockSpec((tr, width), lambda i: (i, 0))
    full = lambda a: pl.BlockSpec(a.shape, lambda i: (0, 0))
    return _pcall(
        body, grid=(T // tr,), in_specs=[pl.BlockSpec((tr, width), lambda i: (i, wb)), full(w), full(bd), row0, row0, row0, row0],
        out_specs=[row0, full(w)], out_shape=[SDS((T, width), bf16), SDS(w.shape, f32)], name=name)(proj, w, bd, *tabs, dout)


def _make_dot16(ca, cb):
    def raw(x, y, cx, cy):
        return _dot(x.astype(bf16), y.astype(bf16), cx, cy, precision=None)

    @jax.custom_vjp
    def f(a, b):
        return raw(a, b, ca, cb)

    def fwd(a, b):
        return raw(a, b, ca, cb), (a, b)

    def bwd(res, dy):
        a, b = res
        if (ca, cb) == (1, 0):
            return raw(dy, b, 1, 1), raw(a, dy, 0, 0)
        return raw(dy, b, 1, 0), raw(dy, a, 0, 0)

    f.defvjp(fwd, bwd)
    return f


_dot16_nn, _dot16_nt = _make_dot16(1, 0), _make_dot16(1, 1)


def _attn_group(qg, kb, vb, sinks, first, hk):
    R = qg.shape[0]
    s = _dot16_nt(qg, kb) * 0.125
    qi = lax.broadcasted_iota(jnp.int32, (R, 2 * SWA_B), 0) & (SWA_B - 1)
    kj = lax.broadcasted_iota(jnp.int32, (R, 2 * SWA_B), 1)
    rel = qi + SWA_B - kj
    mask = (rel >= 0) & (rel < SWA_B) & ((kj >= SWA_B) | jnp.logical_not(first))
    s = jnp.where(mask, s, -jnp.inf)
    head = (lax.broadcasted_iota(jnp.int32, (R, LANE), 0) >> 7) + 8 * hk
    lane = lax.broadcasted_iota(jnp.int32, (R, LANE), 1)
    sink = jnp.sum(jnp.where(lane == head, jnp.broadcast_to(sinks, (R, LANE)), 0.0), axis=1, keepdims=True)
    m = lax.stop_gradient(jnp.maximum(jnp.max(s, axis=1, keepdims=True), sink))
    p = jnp.exp(s - m)
    denom = jnp.sum(p, axis=1, keepdims=True) + jnp.exp(sink - m)
    return _dot16_nn(p / denom, vb)


def _group_rows(ref, hk):
    return jnp.concatenate([ref[:, 64 * (8 * hk + g):64 * (8 * hk + g + 1)] for g in range(8)], axis=0)


def _put_group(ref, hk, val):
    for g in range(8):
        ref[:, 64 * (8 * hk + g):64 * (8 * hk + g + 1)] = val[SWA_B * g:SWA_B * (g + 1)].astype(ref.dtype)


def _attn_specs():
    qs = pl.BlockSpec((SWA_B, 1024), lambda i: (i, 0))
    cur = pl.BlockSpec((SWA_B, LANE), lambda i: (i, 0))
    prev = pl.BlockSpec((SWA_B, LANE), lambda i: (jnp.maximum(i - 1, 0), 0))
    vcur = pl.BlockSpec((SWA_B, LANE), lambda i: (i, CB_SWV))
    vprev = pl.BlockSpec((SWA_B, LANE), lambda i: (jnp.maximum(i - 1, 0), CB_SWV))
    vec = pl.BlockSpec((1, LANE), lambda i: (0, 0))
    return qs, cur, prev, vcur, vprev, vec


def _attn_fwd(name, sq, sk, proj, sinks):
    T = sq.shape[0]

    def body(q_ref, kp_ref, kc_ref, vp_ref, vc_ref, sk_ref, o_ref):
        first = pl.program_id(0) == 0
        sinks_v = sk_ref[...]
        for hk in range(2):
            ks = slice(64 * hk, 64 * hk + 64)
            kb = jnp.concatenate([kp_ref[:, ks], kc_ref[:, ks]], axis=0)
            vb = jnp.concatenate([vp_ref[:, ks], vc_ref[:, ks]], axis=0)
            _put_group(o_ref, hk, _attn_group(_group_rows(q_ref, hk), kb, vb, sinks_v, first, hk))

    qs, cur, prev, vcur, vprev, vec = _attn_specs()
    return _pcall(body, grid=(T // SWA_B,), in_specs=[qs, prev, cur, vprev, vcur, vec], out_specs=qs,
                  out_shape=SDS((T, 1024), bf16), name=name)(sq, sk, sk, proj, proj, sinks)


def _attn_bwd(name, sq, sk, proj, sinks, do):
    T = sq.shape[0]

    def body(q_ref, kp_ref, kc_ref, vp_ref, vc_ref, sk_ref, do_ref, dq_ref, dkp_ref, dkc_ref, dvp_ref, dvc_ref, dsk_ref):
        first = pl.program_id(0) == 0

        @pl.when(first)
        def _():
            dsk_ref[...] = jnp.zeros_like(dsk_ref)

        sinks_v = sk_ref[...]
        dsk = jnp.zeros((1, LANE), f32)
        for hk in range(2):
            ks = slice(64 * hk, 64 * hk + 64)
            kb = jnp.concatenate([kp_ref[:, ks], kc_ref[:, ks]], axis=0)
            vb = jnp.concatenate([vp_ref[:, ks], vc_ref[:, ks]], axis=0)
            _, vjp = jax.vjp(functools.partial(_attn_group, first=first, hk=hk), _group_rows(q_ref, hk), kb, vb, sinks_v)
            dq, dkb, dvb, ds_ = vjp(_group_rows(do_ref, hk))
            _put_group(dq_ref, hk, dq)
            dsk = dsk + ds_
            dkp_ref[:, ks] = dkb[:SWA_B]
            dkc_ref[:, ks] = dkb[SWA_B:]
            dvp_ref[:, ks] = dvb[:SWA_B]
            dvc_ref[:, ks] = dvb[SWA_B:]
        dsk_ref[...] += dsk

    qs, cur, prev, vcur, vprev, vec = _attn_specs()
    return _pcall(
        body, grid=(T // SWA_B,), in_specs=[qs, prev, cur, vprev, vcur, vec, qs], out_specs=[qs, cur, cur, cur, cur, vec],
        out_shape=[SDS((T, 1024), f32)] + [SDS((T, LANE), f32)] * 4 + [SDS((1, LANE), f32)], name=name)(sq, sk, sk, proj, proj, sinks, do)


def _shift_add(name, cur, prev, out_dtype):
    T = cur.shape[0]
    nb = T // SWA_B

    def body(c_ref, p_ref, o_ref):
        has_next = (pl.program_id(0) + 1 < nb).astype(f32)
        o_ref[...] = (c_ref[...] + has_next * p_ref[...]).astype(o_ref.dtype)

    blk = pl.BlockSpec((SWA_B, LANE), lambda i: (i, 0))
    nxt = pl.BlockSpec((SWA_B, LANE), lambda i: (jnp.minimum(i + 1, nb - 1), 0))
    return _pcall(body, grid=(nb,), in_specs=[blk, nxt], out_specs=blk, out_shape=SDS((T, LANE), out_dtype), name=name)(cur, prev)


def _loss(name, y, tgt):
    T = y.shape[0]

    def body(y_ref, t_ref, l_ref, dy_ref):
        @pl.when(pl.program_id(0) == 0)
        def _():
            l_ref[...] = jnp.zeros_like(l_ref)

        d = y_ref[...] - t_ref[...]
        l_ref[...] += jnp.sum(d * d) * (0.5 / D)
        dy_ref[...] = d * (1.0 / D)

    row = pl.BlockSpec((TR, D), lambda i: (i, 0))
    return _pcall(body, grid=(T // TR,), in_specs=[row, row], out_specs=[pl.BlockSpec((8, LANE), lambda i: (0, 0)), row],
                  out_shape=[SDS((8, LANE), f32), SDS((T, D), f32)], name=name)(y, tgt)


def _adamw(name, w, g, m, v):
    shape = w.shape
    C = shape[-1]
    R = int(np.prod(shape[:-1]))
    tr = _pick(R, (128, 64, 16, 8))
    bc1 = np.float32(1.0 - ADAM_B1 ** ADAM_STEP)
    bc2 = np.float32(1.0 - ADAM_B2 ** ADAM_STEP)

    def body(w_ref, g_ref, m_ref, v_ref, d_ref, mo_ref, vo_ref):
        g_ = g_ref[...]
        m_ = ADAM_B1 * m_ref[...] + (1.0 - ADAM_B1) * g_
        v_ = ADAM_B2 * v_ref[...] + (1.0 - ADAM_B2) * (g_ * g_)
        d_ref[...] = -ADAM_LR * ((m_ / bc1) / (jnp.sqrt(v_ / bc2) + ADAM_EPS) + ADAM_WD * w_ref[...])
        mo_ref[...] = m_
        vo_ref[...] = v_

    blk = pl.BlockSpec((tr, C), lambda i: (i, 0))
    outs = _pcall(body, grid=(R // tr,), in_specs=[blk] * 4, out_specs=[blk] * 3, out_shape=[SDS((R, C), f32)] * 3,
                  compiler_params=_cparams(VMEM_BIG), name=name)(*[t.reshape(R, C) for t in (w, g, m, v)])
    return [o.reshape(shape) for o in outs]


def _silu_rows(name, x):
    def body(x_ref, o_ref):
        t = x_ref[...]
        o_ref[...] = (t * jax.nn.sigmoid(t)).astype(o_ref.dtype)

    return _pcall(body, out_shape=SDS(x.shape, bf16), name=name)(x)


def _sum_leading(name, x):
    n = x.shape[0]

    def body(x_ref, o_ref):
        acc = x_ref[0]
        for k in range(1, n):
            acc = acc + x_ref[k]
        o_ref[...] = acc

    tr = _pick(x.shape[1], (COMM_TR, 8))
    return _pcall(body, grid=(x.shape[1] // tr,), in_specs=[pl.BlockSpec((n, tr, x.shape[2]), lambda i: (0, i, 0))],
                  out_specs=pl.BlockSpec((tr, x.shape[2]), lambda i: (i, 0)), out_shape=SDS(x.shape[1:], x.dtype), name=name)(x)


def _add_my_half(name, g4, b1, c):
    _, R, W = g4.shape
    nblk = (R // 2) // COMM_TR

    def body(c_ref, g_ref, b_ref, o_ref):
        o_ref[...] = g_ref[...] + b_ref[...]

    grid_spec = pltpu.PrefetchScalarGridSpec(
        num_scalar_prefetch=1, grid=(4, nblk),
        in_specs=[pl.BlockSpec((None, COMM_TR, W), lambda s, i, c_ref: (s, c_ref[0] * nblk + i, 0)),
                  pl.BlockSpec((None, COMM_TR, W), lambda s, i, c_ref: (s, i, 0))],
        out_specs=pl.BlockSpec((None, COMM_TR, W), lambda s, i, c_ref: (s, i, 0)))
    return _pcall(body, grid_spec=grid_spec, out_shape=SDS((4, R // 2, W), f32), name=name)(c.reshape(1), g4, b1)


HBM_SPEC = pl.BlockSpec(memory_space=pltpu.HBM)


def _position():
    x, y, c = lax.axis_index("x"), lax.axis_index("y"), lax.axis_index("c")
    return x, y, c, [(1 - x, y), (x, 1 - y), (1 - x, 1 - y)]


def _remote(src, dst, send_sems, recv_sems, k, to):
    return pltpu.make_async_remote_copy(src_ref=src, dst_ref=dst, send_sem=send_sems.at[k], recv_sem=recv_sems.at[k],
                                        device_id=to, device_id_type=MESH)


def _allgather_chips(name, buf):
    R, W = buf.shape
    Rh = R // 2

    def body(in_ref, out_ref, send_sems, recv_sems, local_sem):
        x, y, c, chips = _position()
        me = 2 * x + y
        sib = (x, y, 1 - c)
        half = pl.ds(pl.multiple_of(c * Rh, 32), Rh)
        ohalf = pl.ds(pl.multiple_of((1 - c) * Rh, 32), Rh)
        mine = pltpu.make_async_copy(in_ref, out_ref.at[me], local_sem)
        mine.start()
        first = [_remote(in_ref.at[half], out_ref.at[me, half], send_sems, recv_sems, j, (cx, cy, c)) for j, (cx, cy) in enumerate(chips)]
        for cp in first:
            cp.start()
        passed = []
        for j, (cx, cy) in enumerate(chips):
            rows = out_ref.at[2 * cx + cy, half]
            _remote(rows, rows, send_sems, recv_sems, j, (cx, cy, c)).wait_recv()
            cp = _remote(rows, rows, send_sems, recv_sems, 3 + j, sib)
            cp.start()
            passed.append(cp)
        for j, (cx, cy) in enumerate(chips):
            rows = out_ref.at[2 * cx + cy, ohalf]
            _remote(rows, rows, send_sems, recv_sems, 3 + j, sib).wait_recv()
        for cp in first + passed:
            cp.wait_send()
        mine.wait()

    return _pcall(body, in_specs=[HBM_SPEC], out_specs=HBM_SPEC, out_shape=SDS((4, R, W), buf.dtype),
                  scratch_shapes=[pltpu.SemaphoreType.DMA((6,)), pltpu.SemaphoreType.DMA((6,)), pltpu.SemaphoreType.DMA], name=name)(buf)


def _swap_halves(name, g4):
    _, R, W = g4.shape
    Rh = R // 2

    def body(in_ref, out_ref, send_sems, recv_sems):
        x, y, c, _ = _position()
        ohalf = pl.ds(pl.multiple_of((1 - c) * Rh, 32), Rh)
        cp = _remote(in_ref.at[:, ohalf, :], out_ref, send_sems, recv_sems, 0, (x, y, 1 - c))
        cp.start()
        cp.wait()

    return _pcall(body, in_specs=[HBM_SPEC], out_specs=HBM_SPEC, out_shape=SDS((4, Rh, W), g4.dtype),
                  scratch_shapes=[pltpu.SemaphoreType.DMA((1,)), pltpu.SemaphoreType.DMA((1,))], name=name)(g4)


def _scatter_chips(name, p4):
    _, Rh, W = p4.shape

    def body(in_ref, out_ref, send_sems, recv_sems, local_sem):
        x, y, c, chips = _position()
        me = 2 * x + y
        mine = pltpu.make_async_copy(in_ref.at[me], out_ref.at[me], local_sem)
        mine.start()
        sends = [_remote(in_ref.at[2 * cx + cy], out_ref.at[me], send_sems, recv_sems, j, (cx, cy, c)) for j, (cx, cy) in enumerate(chips)]
        for cp in sends:
            cp.start()
        for j, (cx, cy) in enumerate(chips):
            slot = out_ref.at[2 * cx + cy]
            _remote(slot, slot, send_sems, recv_sems, j, (cx, cy, c)).wait_recv()
        for cp in sends:
            cp.wait_send()
        mine.wait()

    return _pcall(body, in_specs=[HBM_SPEC], out_specs=HBM_SPEC, out_shape=SDS((4, Rh, W), p4.dtype),
                  scratch_shapes=[pltpu.SemaphoreType.DMA((3,)), pltpu.SemaphoreType.DMA((3,)), pltpu.SemaphoreType.DMA], name=name)(p4)


def _join_halves(name, r):
    Rh, W = r.shape

    def body(in_ref, out_ref, send_sems, recv_sems, local_sem):
        x, y, c, _ = _position()
        mine = pltpu.make_async_copy(in_ref, out_ref.at[c], local_sem)
        mine.start()
        cp = _remote(in_ref, out_ref.at[c], send_sems, recv_sems, 0, (x, y, 1 - c))
        cp.start()
        _remote(in_ref, out_ref.at[1 - c], send_sems, recv_sems, 0, (x, y, 1 - c)).wait_recv()
        cp.wait_send()
        mine.wait()

    return _pcall(body, in_specs=[HBM_SPEC], out_specs=HBM_SPEC, out_shape=SDS((2, Rh, W), r.dtype),
                  scratch_shapes=[pltpu.SemaphoreType.DMA((1,)), pltpu.SemaphoreType.DMA((1,)), pltpu.SemaphoreType.DMA], name=name)(r)


def _allgather_all(name, buf):
    r, W = buf.shape

    def body(in_ref, out_ref, send_sems, recv_sems, local_sem):
        x, y, c, _ = _position()
        me = 4 * x + 2 * y + c
        mine = pltpu.make_async_copy(in_ref, out_ref.at[me], local_sem)
        mine.start()
        peers = []
        for mk in range(1, 8):
            mx, my, mc = (mk >> 2) & 1, (mk >> 1) & 1, mk & 1
            px = 1 - x if mx else x
            py = 1 - y if my else y
            pc = 1 - c if mc else c
            peers.append((px, py, pc))
        sends = [_remote(in_ref, out_ref.at[me], send_sems, recv_sems, k, p) for k, p in enumerate(peers)]
        for cp in sends:
            cp.start()
        for k, (px, py, pc) in enumerate(peers):
            slot = out_ref.at[4 * px + 2 * py + pc]
            _remote(slot, slot, send_sems, recv_sems, k, (px, py, pc)).wait_recv()
        for cp in sends:
            cp.wait_send()
        mine.wait()

    return _pcall(body, in_specs=[HBM_SPEC], out_specs=HBM_SPEC, out_shape=SDS((8, r, W), buf.dtype),
                  scratch_shapes=[pltpu.SemaphoreType.DMA((7,)), pltpu.SemaphoreType.DMA((7,)), pltpu.SemaphoreType.DMA], name=name)(buf)


def _reduce_scatter(g4, c, tag):
    b1 = _swap_halves("rs_swap_" + tag, g4)
    p4 = _add_my_half("rs_pair_" + tag, g4, b1, c)
    b2 = _scatter_chips("rs_scatter_" + tag, p4)
    r = _sum_leading("rs_sum_" + tag, b2)
    full = _join_halves("rs_join_" + tag, r)
    return full.reshape(g4.shape[1], g4.shape[2])


def _dma_sems(n):
    return [pltpu.SemaphoreType.DMA((n,)), pltpu.SemaphoreType.DMA((n,))]


def _gather_chips(name, shards):
    n = len(shards)

    def body(*refs):
        ins, outs = refs[:n], refs[n:2 * n]
        send_sems, recv_sems = refs[2 * n:]
        x, y, c, chips = _position()
        me = 2 * x + y
        sib = (x, y, 1 - c)
        sends, halves = [], []
        for i in range(n):
            rh = ins[i].shape[0] // 2
            halves.append((pl.ds(pl.multiple_of(c * rh, 16), rh), pl.ds(pl.multiple_of((1 - c) * rh, 16), rh)))
        for i in range(n):
            for j, (cx, cy) in enumerate(chips):
                cp = _remote(ins[i].at[halves[i][0]], outs[i].at[me, halves[i][0]], send_sems, recv_sems, 6 * i + j, (cx, cy, c))
                cp.start()
                sends.append(cp)
        for j, (cx, cy) in enumerate(chips):
            for i in range(n):
                rows = outs[i].at[2 * cx + cy, halves[i][0]]
                _remote(rows, rows, send_sems, recv_sems, 6 * i + j, (cx, cy, c)).wait_recv()
                cp = _remote(rows, rows, send_sems, recv_sems, 6 * i + 3 + j, sib)
                cp.start()
                sends.append(cp)
        for j, (cx, cy) in enumerate(chips):
            for i in range(n):
                rows = outs[i].at[2 * cx + cy, halves[i][1]]
                _remote(rows, rows, send_sems, recv_sems, 6 * i + 3 + j, sib).wait_recv()
        for cp in sends:
            cp.wait_send()

    return _pcall(body, in_specs=[HBM_SPEC] * n, out_specs=[HBM_SPEC] * n, out_shape=[SDS((4,) + s.shape, s.dtype) for s in shards],
                  scratch_shapes=_dma_sems(6 * n), name=name)(*shards)


def _swap_halves_multi(name, slots):
    n = len(slots)

    def body(*refs):
        ins, outs = refs[:n], refs[n:2 * n]
        send_sems, recv_sems = refs[2 * n:]
        x, y, c, _ = _position()
        cps = []
        for i in range(n):
            rh = ins[i].shape[1] // 2
            ohalf = pl.ds(pl.multiple_of((1 - c) * rh, 8), rh)
            cp = _remote(ins[i].at[:, ohalf, :], outs[i], send_sems, recv_sems, i, (x, y, 1 - c))
            cp.start()
            cps.append(cp)
        for cp in cps:
            cp.wait()

    return _pcall(body, in_specs=[HBM_SPEC] * n, out_specs=[HBM_SPEC] * n,
                  out_shape=[SDS((4, s.shape[1] // 2, s.shape[2]), s.dtype) for s in slots], scratch_shapes=_dma_sems(n), name=name)(*slots)


def _pair_add(name, g4, b1, c):
    _, R, W = g4.shape
    tr = _pick(R // 2, (256, 128, 32, 16))
    nblk = (R // 2) // tr

    def body(c_ref, g_ref, b_ref, o_ref):
        o_ref[...] = (g_ref[...] + b_ref[...]).astype(o_ref.dtype)

    grid_spec = pltpu.PrefetchScalarGridSpec(
        num_scalar_prefetch=1, grid=(4, nblk),
        in_specs=[pl.BlockSpec((None, tr, W), lambda s, i, c_ref: (s, c_ref[0] * nblk + i, 0)),
                  pl.BlockSpec((None, tr, W), lambda s, i, c_ref: (s, i, 0))],
        out_specs=pl.BlockSpec((None, tr, W), lambda s, i, c_ref: (s, i, 0)))
    return _pcall(body, grid_spec=grid_spec, out_shape=SDS((4, R // 2, W), bf16), name=name)(c.reshape(1), g4, b1)


def _scatter_chips_multi(name, ps):
    n = len(ps)

    def body(*refs):
        ins, outs = refs[:n], refs[n:2 * n]
        send_sems, recv_sems = refs[2 * n:]
        x, y, c, chips = _position()
        me = 2 * x + y
        sends = []
        for i in range(n):
            for j, (cx, cy) in enumerate(chips):
                cp = _remote(ins[i].at[2 * cx + cy], outs[i].at[me], send_sems, recv_sems, 3 * i + j, (cx, cy, c))
                cp.start()
                sends.append(cp)
        for i in range(n):
            for j, (cx, cy) in enumerate(chips):
                slot = outs[i].at[2 * cx + cy]
                _remote(slot, slot, send_sems, recv_sems, 3 * i + j, (cx, cy, c)).wait_recv()
        for cp in sends:
            cp.wait_send()

    return _pcall(body, in_specs=[HBM_SPEC] * n, out_specs=[HBM_SPEC] * n, out_shape=[SDS(p.shape, p.dtype) for p in ps],
                  scratch_shapes=_dma_sems(3 * n), name=name)(*ps)


def _sum_chips(name, p4, b2, chip):
    _, Rh, W = p4.shape
    tr = _pick(Rh, (256, 128, 32, 16))

    def body(m_ref, own_ref, r1_ref, r2_ref, r3_ref, o_ref):
        o_ref[...] = ((own_ref[...].astype(f32) + r1_ref[...].astype(f32)) + r2_ref[...].astype(f32)) + r3_ref[...].astype(f32)

    other = lambda k: pl.BlockSpec((None, tr, W), lambda i, m_ref: (m_ref[0] ^ k, i, 0))
    grid_spec = pltpu.PrefetchScalarGridSpec(
        num_scalar_prefetch=1, grid=(Rh // tr,),
        in_specs=[pl.BlockSpec((None, tr, W), lambda i, m_ref: (m_ref[0], i, 0)), other(1), other(2), other(3)],
        out_specs=pl.BlockSpec((tr, W), lambda i, m_ref: (i, 0)))
    return _pcall(body, grid_spec=grid_spec, out_shape=SDS((Rh, W), f32), name=name)(chip.reshape(1), p4, b2, b2, b2)


def _swap_multi(name, rs):
    n = len(rs)

    def body(*refs):
        ins, outs = refs[:n], refs[n:2 * n]
        send_sems, recv_sems = refs[2 * n:]
        x, y, c, _ = _position()
        cps = [_remote(ins[i], outs[i], send_sems, recv_sems, i, (x, y, 1 - c)) for i in range(n)]
        for cp in cps:
            cp.start()
        for cp in cps:
            cp.wait()

    return _pcall(body, in_specs=[HBM_SPEC] * n, out_specs=[HBM_SPEC] * n, out_shape=[SDS(r.shape, r.dtype) for r in rs],
                  scratch_shapes=_dma_sems(n), name=name)(*rs)


def _hosted_gather(shards):
    n = len(shards)

    def half(ref_rows, c):
        rh = ref_rows // 2
        return pl.ds(pl.multiple_of(c * rh, 16), rh)

    def start(ins, outs, send_sems, recv_sems):
        x, y, c, chips = _position()
        me = 2 * x + y
        for i in range(n):
            rows = half(ins[i].shape[0], c)
            for j, (cx, cy) in enumerate(chips):
                _remote(ins[i].at[rows], outs[i].at[me, rows], send_sems, recv_sems, 3 * i + j, (cx, cy, c)).start()

    def finish(ins, outs, send_sems, recv_sems):
        x, y, c, chips = _position()
        me = 2 * x + y
        for i in range(n):
            rows = half(ins[i].shape[0], c)
            for j, (cx, cy) in enumerate(chips):
                _remote(ins[i].at[rows], outs[i].at[2 * cx + cy, rows], send_sems, recv_sems, 3 * i + j, (cx, cy, c)).wait_recv()
        for i in range(n):
            rows = half(ins[i].shape[0], c)
            for j, (cx, cy) in enumerate(chips):
                _remote(ins[i].at[rows], outs[i].at[me, rows], send_sems, recv_sems, 3 * i + j, (cx, cy, c)).wait_send()

    return {"arrays": shards, "out_shape": [SDS((4,) + s.shape, s.dtype) for s in shards], "n_sems": 3 * n, "start": start, "finish": finish}


def _gather_forward(name, gathered):
    n = len(gathered)

    def body(*refs):
        outs = refs[n:2 * n]
        send_sems, recv_sems = refs[2 * n:]
        x, y, c, chips = _position()
        sib = (x, y, 1 - c)
        sends = []
        for i in range(n):
            rh = outs[i].shape[1] // 2
            mine = pl.ds(pl.multiple_of(c * rh, 16), rh)
            for j, (cx, cy) in enumerate(chips):
                rows = outs[i].at[2 * cx + cy, mine]
                cp = _remote(rows, rows, send_sems, recv_sems, 3 * i + j, sib)
                cp.start()
                sends.append(cp)
        for i in range(n):
            rh = outs[i].shape[1] // 2
            theirs = pl.ds(pl.multiple_of((1 - c) * rh, 16), rh)
            for j, (cx, cy) in enumerate(chips):
                rows = outs[i].at[2 * cx + cy, theirs]
                _remote(rows, rows, send_sems, recv_sems, 3 * i + j, sib).wait_recv()
        for cp in sends:
            cp.wait_send()

    return _pcall(body, in_specs=[HBM_SPEC] * n, out_specs=[HBM_SPEC] * n, out_shape=[SDS(g.shape, g.dtype) for g in gathered],
                  input_output_aliases={i: i for i in range(n)}, scratch_shapes=_dma_sems(3 * n), name=name)(*gathered)


def _hosted_scatter(ps):
    n = len(ps)

    def start(ins, outs, send_sems, recv_sems):
        x, y, c, chips = _position()
        me = 2 * x + y
        for i in range(n):
            for j, (cx, cy) in enumerate(chips):
                _remote(ins[i].at[2 * cx + cy], outs[i].at[me], send_sems, recv_sems, 3 * i + j, (cx, cy, c)).start()

    def finish(ins, outs, send_sems, recv_sems):
        x, y, c, chips = _position()
        me = 2 * x + y
        for i in range(n):
            for j, (cx, cy) in enumerate(chips):
                slot = outs[i].at[2 * cx + cy]
                _remote(slot, slot, send_sems, recv_sems, 3 * i + j, (cx, cy, c)).wait_recv()
        for i in range(n):
            for j, (cx, cy) in enumerate(chips):
                _remote(ins[i].at[2 * cx + cy], outs[i].at[me], send_sems, recv_sems, 3 * i + j, (cx, cy, c)).wait_send()

    return {"arrays": ps, "out_shape": [SDS(p.shape, p.dtype) for p in ps], "n_sems": 3 * n, "start": start, "finish": finish}


def _rs_begin(slots, c):
    b1 = _swap_halves_multi("rs_swap", slots)
    return [_pair_add("rs_pair_%d" % i, g, b, c) for i, (g, b) in enumerate(zip(slots, b1))]


def _rs_end(ps, b2, c, chip):
    rs = [_sum_chips("rs_sum_%d" % i, p, b, chip) for i, (p, b) in enumerate(zip(ps, b2))]
    others = _swap_multi("rs_join", rs)
    return [jnp.where(c == 0, jnp.concatenate([r, o], axis=0), jnp.concatenate([o, r], axis=0)) for r, o in zip(rs, others)]


def _reduce_scatter_multi(slots, c, chip):
    ps = _rs_begin(slots, c)
    return _rs_end(ps, _scatter_chips_multi("rs_scatter", ps), c, chip)


_BIG = ("w_in", "w_dn_out", "w_swa_out", "w_o", "w_up", "w_down")


def _assemble_weights(gs):
    cat1 = lambda g: jnp.concatenate([g[s] for s in range(4)], axis=1)
    rows = lambda g: g.reshape(4 * g.shape[1], g.shape[2])
    return {"w_in": _regroup_w_in(cat1(gs[0])), "w_dn_out": rows(gs[1]), "w_swa_out": rows(gs[2]), "w_o": rows(gs[3]),
            "w_up": cat1(gs[4]), "w_down": rows(gs[5])}


def _grad_slots(gw):
    cols = lambda g: g.reshape(g.shape[0], 4, g.shape[1] // 4).transpose(1, 0, 2)
    rows = lambda g: g.reshape(4, g.shape[0] // 4, g.shape[1])
    return [cols(_ungroup_w_in(gw["w_in"])), rows(gw["w_dn_out"]), rows(gw["w_swa_out"]), rows(gw["w_o"]), gw["w_up"], rows(gw["w_down"])]


_SHARDED = (
    ("w_in", (D, 1860), 1, False),
    ("dn_conv", (4, 768), 1, True),
    ("w_dn_out", (256, D), 0, False),
    ("w_swa_out", (256, D), 0, False),
    ("w_o", (256, D), 0, False),
    ("w_up", (D, 1408), 1, False),
    ("ffn_conv", (3, 704), 1, True),
    ("w_down", (704, D), 0, False),
)


def _pack_weights(shards):
    parts = []
    for nm, shp, _, as_bits in _SHARDED:
        a = shards[nm]
        parts.append(lax.bitcast_convert_type(a, bf16).reshape(-1) if as_bits else a.astype(bf16).reshape(-1))
    flat = jnp.concatenate(parts)
    return jnp.pad(flat, (0, COMM_ROWS * COMM_W - flat.shape[0])).reshape(COMM_ROWS, COMM_W)


def _unpack_weights(g):
    flat = g.reshape(4, -1)
    out, off = {}, 0
    for nm, shp, ax, as_bits in _SHARDED:
        n = int(np.prod(shp)) * (2 if as_bits else 1)
        piece = flat[:, off:off + n]
        off += n
        if as_bits:
            piece = lax.bitcast_convert_type(piece.reshape((4,) + shp + (2,)), f32)
        else:
            piece = piece.reshape((4,) + shp)
        out[nm] = jnp.concatenate([piece[s] for s in range(4)], axis=ax)
    return out


def _pack_grads(grads):
    slots = []
    for s in range(4):
        parts = []
        for nm, shp, ax, _ in _SHARDED:
            n = shp[ax]
            parts.append(lax.slice_in_dim(grads[nm], s * n, (s + 1) * n, axis=ax).reshape(-1))
        flat = jnp.concatenate(parts)
        slots.append(jnp.pad(flat, (0, COMM_ROWS * COMM_W - flat.shape[0])))
    return jnp.stack(slots).reshape(4, COMM_ROWS, COMM_W)


def _unpack_grads(r):
    flat = r.reshape(-1)
    out, off = {}, 0
    for nm, shp, _, _ in _SHARDED:
        n = int(np.prod(shp))
        out[nm] = flat[off:off + n].reshape(shp)
        off += n
    return out


def _regroup_w_in(w):
    z = lambda n: jnp.zeros((w.shape[0], n), w.dtype)
    return jnp.concatenate([w[:, 0:3072], w[:, 3072:4096], w[:, 4112:5136], w[:, 5392:6416], w[:, 6416:7440],
                            w[:, 4096:4112], z(112), w[:, 5136:5264], w[:, 5264:5392], z(128)], axis=1)


def _ungroup_w_in(g):
    return jnp.concatenate([g[:, 0:3072], g[:, 3072:4096], g[:, 7168:7184], g[:, 4096:5120], g[:, 7296:7424], g[:, 7424:7552],
                            g[:, 5120:6144], g[:, 6144:7168]], axis=1)


def _pad_lanes(v, n=LANE):
    return jnp.pad(v, (0, n - v.shape[0])).reshape(1, n)


def _layer_consts(P):
    K = {}
    K["norm_mix"] = P["norm_mix"].reshape(1, D)
    K["norm_ffn"] = P["norm_ffn"].reshape(1, D)
    K["alog"] = _pad_lanes(P["dn_a_log"])
    K["dtb"] = _pad_lanes(P["dn_dt_bias"])
    K["dn_norm"] = P["dn_norm"].reshape(1, LANE)
    K["qn"] = jnp.tile(P["swa_q_norm"], 16).reshape(1, D)
    K["kn"] = jnp.tile(P["swa_k_norm"], 2).reshape(1, LANE)
    K["sinks"] = _pad_lanes(P["swa_sinks"])
    K["ffn_b"] = P["ffn_conv_b"].reshape(1, D_FF)
    return K


def _layer_fwd(x, mod, W, K, tabs, bd, hosted=None):
    sh1, sc1, gt1, sh2, sc2, gt2 = mod
    S = {"x": x}
    (h1,) = _rowwise_fwd("normmod1_fwd", _normmod_fn, [(x, 0, D)], [K["norm_mix"], sc1, sh1], [D], [bf16])
    proj = _matmul("proj_fwd", h1, W["w_in"], "nn", f32)
    qn = _dnconv_fwd("dnconv_q_fwd", proj, CB_Q, W["dn_conv"], True)
    kn = _dnconv_fwd("dnconv_k_fwd", proj, CB_K, W["dn_conv"], True)
    vc = _dnconv_fwd("dnconv_v_fwd", proj, CB_V, W["dn_conv"], False)
    o, sall, hosted_out = _gdn_fwd("gdn_fwd", qn, kn, vc, proj, K["alog"], K["dtb"], hosted=hosted)
    (on,) = _rowwise_fwd("dngate_fwd", _dngate_fn, [(o, 0, LANE), (proj, 8 * WB_Z, LANE)], [K["dn_norm"]], [LANE], [bf16], nc=8,
                         tr=_pick(x.shape[0], (1024,)))
    ya = _matmul("dnout_fwd", on, W["w_dn_out"], "nn", f32)
    sq = _qkprep_fwd("qprep_fwd", proj, WB_SWQ, D, K["qn"], bd[0], tabs[0])
    sk = _qkprep_fwd("kprep_fwd", proj, CB_SWK, LANE, K["kn"], bd[1], tabs[1])
    attn = _attn_fwd("attn_fwd", sq, sk, proj, K["sinks"])
    yb = _matmul("swaout_fwd", attn, W["w_swa_out"], "nn", f32)
    (merged,) = _rowwise_fwd("merge_fwd", _merge_fn, [(proj, WB_GA, D), (proj, WB_GB, D), (ya, 0, D), (yb, 0, D)], [], [D], [bf16])
    t1 = _matmul("wo_fwd", merged, W["w_o"], "nn", f32)
    (x1,) = _rowwise_fwd("resid1_fwd", _resid_fn, [(x, 0, D), (t1, 0, D)], [gt1], [D], [f32])
    (h2,) = _rowwise_fwd("normmod2_fwd", _normmod_fn, [(x1, 0, D)], [K["norm_ffn"], sc2, sh2], [D], [bf16])
    up = _matmul("up_fwd", h2, W["w_up"], "nn", f32)
    mid = _ffnact_fwd("ffnact_fwd", up, W["ffn_conv"], K["ffn_b"])
    t2 = _matmul("down_fwd", mid, W["w_down"], "nn", f32)
    (x2,) = _rowwise_fwd("resid2_fwd", _resid_fn, [(x1, 0, D), (t2, 0, D)], [gt2], [D], [f32])
    S.update(h1=h1, proj=proj, qn=qn, kn=kn, vc=vc, o=o, sall=sall, on=on, ya=ya, sq=sq, sk=sk, attn=attn, yb=yb,
             merged=merged, t1=t1, x1=x1, h2=h2, up=up, mid=mid, t2=t2)
    return x2, S, hosted_out


def _layer_bwd(dx2, S, mod, W, K, tabs, bd, hosted=None):
    sh1, sc1, gt1, sh2, sc2, gt2 = mod
    x, x1, proj, up = S["x"], S["x1"], S["proj"], S["up"]
    T = x.shape[0]
    gw, gs = {}, {}
    dt2, dgt2 = _rowwise_bwd("resid2_bwd", _resid_fn, [(x1, 0, D), (S["t2"], 0, D)], [gt2], [(dx2, 0, D)], [None, bf16])
    dmid = _matmul("down_bwd_x", dt2, W["w_down"], "nt", bf16)
    gw["w_down"] = _matmul("down_bwd_w", S["mid"], dt2, "tn", f32)
    dact, dlin, gw["ffn_conv"], dffn_b = _ffnact_bwd("ffnact_bwd", up, W["ffn_conv"], K["ffn_b"], dmid)
    dup = jnp.concatenate([dact, dlin], axis=1)
    dh2 = _matmul("up_bwd_x", dup, W["w_up"], "nt", f32)
    gw["w_up"] = _matmul("up_bwd_w", S["h2"], dup, "tn", f32, out_slots=4)
    dx1, dnorm_ffn, dsc2, dsh2 = _rowwise_bwd("normmod2_bwd", _normmod_fn, [(x1, 0, D)], [K["norm_ffn"], sc2, sh2], [(dh2, 0, D)], [f32],
                                              add_to_first=(dx2, 0, D))
    dt1, dgt1 = _rowwise_bwd("resid1_bwd", _resid_fn, [(x, 0, D), (S["t1"], 0, D)], [gt1], [(dx1, 0, D)], [None, bf16])
    dmerged = _matmul("wo_bwd_x", dt1, W["w_o"], "nt", f32)
    gw["w_o"] = _matmul("wo_bwd_w", S["merged"], dt1, "tn", f32)
    dga, dgb, dya, dyb = _rowwise_bwd("merge_bwd", _merge_fn, [(proj, WB_GA, D), (proj, WB_GB, D), (S["ya"], 0, D), (S["yb"], 0, D)], [],
                                      [(dmerged, 0, D)], [bf16, bf16, bf16, bf16])
    don = _matmul("dnout_bwd_x", dya, W["w_dn_out"], "nt", f32)
    gw["w_dn_out"] = _matmul("dnout_bwd_w", S["on"], dya, "tn", f32)
    do, dz, ddn_norm = _rowwise_bwd("dngate_bwd", _dngate_fn, [(S["o"], 0, LANE), (proj, 8 * WB_Z, LANE)], [K["dn_norm"]], [(don, 0, LANE)],
                                    [f32, bf16], nc=8, tr=_pick(T, (1024,)))
    (dqn, dkn, dvc, dab, dalog, ddtb), hosted_out = _gdn_bwd("gdn_bwd", S["qn"], S["kn"], S["vc"], proj, K["alog"], K["dtb"], S["sall"], do,
                                                            hosted=hosted)
    dpq, dwq = _dnconv_bwd("dnconv_q_bwd", proj, CB_Q, W["dn_conv"], dqn, True)
    dpk, dwk = _dnconv_bwd("dnconv_k_bwd", proj, CB_K, W["dn_conv"], dkn, True)
    dpv, dwv = _dnconv_bwd("dnconv_v_bwd", proj, CB_V, W["dn_conv"], dvc, False)
    gw["dn_conv"] = jnp.concatenate([dwq, dwk, dwv], axis=1)
    dattn = _matmul("swaout_bwd_x", dyb, W["w_swa_out"], "nt", f32)
    gw["w_swa_out"] = _matmul("swaout_bwd_w", S["attn"], dyb, "tn", f32)
    dsq, dkp, dkc, dvp, dvc_, dsinks = _attn_bwd("attn_bwd", S["sq"], S["sk"], proj, K["sinks"], dattn)
    dsk = _shift_add("attn_dk_join", dkc, dkp, f32)
    dswv = _shift_add("attn_dv_join", dvc_, dvp, bf16)
    dswq, dqn_w = _qkprep_bwd("qprep_bwd", proj, WB_SWQ, D, K["qn"], bd[0], tabs[0], dsq)
    dswk, dkn_w = _qkprep_bwd("kprep_bwd", proj, CB_SWK, LANE, K["kn"], bd[1], tabs[1], dsk)
    dproj = jnp.concatenate([dpq, dpk, dpv, dz, dswq, dga, dgb, dab.astype(bf16), dswk, dswv, jnp.zeros((T, LANE), bf16)], axis=1)
    dh1 = _matmul("proj_bwd_x", dproj, W["w_in"], "nt", f32)
    gw["w_in"] = _matmul("proj_bwd_w", S["h1"], dproj, "tn", f32)
    dx, dnorm_mix, dsc1, dsh1 = _rowwise_bwd("normmod1_bwd", _normmod_fn, [(x, 0, D)], [K["norm_mix"], sc1, sh1], [(dh1, 0, D)], [f32],
                                             add_to_first=(dx1, 0, D))
    gs = {"norm_mix": dnorm_mix[0], "dn_a_log": dalog[0, :8], "dn_dt_bias": ddtb[0, :8], "dn_norm": ddn_norm[0],
          "swa_q_norm": dqn_w.reshape(16, 64).sum(0), "swa_k_norm": dkn_w.reshape(2, 64).sum(0), "swa_sinks": dsinks[0, :16],
          "norm_ffn": dnorm_ffn[0], "ffn_conv_b": dffn_b[0]}
    dmod = jnp.concatenate([dsh1, dsc1, dgt1, dsh2, dsc2, dgt2], axis=1)
    return dx, gw, gs, dmod, hosted_out


def _rope_tables(pos):
    T = pos.shape[0]
    half = 8
    inv = jnp.power(ROPE_THETA, -jnp.arange(half, dtype=f32) / half)
    ang = pos.astype(f32)[:, None] * inv
    cos, sin = jnp.cos(ang), jnp.sin(ang)
    z8, z48, o48 = jnp.zeros((T, 8), f32), jnp.zeros((T, 48), f32), jnp.ones((T, 48), f32)
    c64 = jnp.concatenate([cos, cos, o48], axis=1)
    s1 = jnp.concatenate([-sin, z8, z48], axis=1)
    s2 = jnp.concatenate([z8, sin, z48], axis=1)
    return tuple(jnp.tile(t, (1, 16)) for t in (c64, s1, s2))


_SMALL = (("norm_mix", D), ("dn_a_log", 8), ("dn_dt_bias", 8), ("dn_norm", 128), ("swa_q_norm", 64), ("swa_k_norm", 64),
          ("swa_sinks", 16), ("norm_ffn", D), ("ffn_conv_b", D_FF), ("b_ada", 6 * D))
_CONV = (("dn_conv", 4 * 3072), ("ffn_conv", 3 * D_FF))
_CONV_SHARD = (("dn_conv", 4 * 768), ("ffn_conv", 3 * 704))


def _pack_small(vals, spec):
    flat = jnp.concatenate([vals[nm].reshape(-1) for nm, _ in spec])
    rows = -(-flat.shape[0] // (8 * LANE)) * 8
    return jnp.pad(flat, (0, rows * LANE - flat.shape[0])).reshape(rows, LANE)


def _unpack_small(buf, spec):
    flat = buf.reshape(-1)
    out, off = {}, 0
    for nm, n in spec:
        out[nm] = flat[off:off + DEPTH * n].reshape(DEPTH, n)
        off += DEPTH * n
    return out


def kernel(x, c, positions, w_ada, b_ada, norm_mix, w_in, dn_conv, dn_a_log, dn_dt_bias, dn_norm, w_dn_out, swa_q_norm, swa_k_norm, swa_sinks, w_swa_out, w_o, norm_ffn, w_up, ffn_conv, ffn_conv_b, w_down, loss_target, m_w_ada, m_b_ada, m_norm_mix, m_w_in, m_dn_conv, m_dn_a_log, m_dn_dt_bias, m_dn_norm, m_w_dn_out, m_swa_q_norm, m_swa_k_norm, m_swa_sinks, m_w_swa_out, m_w_o, m_norm_ffn, m_w_up, m_ffn_conv, m_ffn_conv_b, m_w_down, v_w_ada, v_b_ada, v_norm_mix, v_w_in, v_dn_conv, v_dn_a_log, v_dn_dt_bias, v_dn_norm, v_w_dn_out, v_swa_q_norm, v_swa_k_norm, v_swa_sinks, v_w_swa_out, v_w_o, v_norm_ffn, v_w_up, v_ffn_conv, v_ffn_conv_b, v_w_down):
    weights = dict(w_ada=w_ada, b_ada=b_ada, norm_mix=norm_mix, w_in=w_in, dn_conv=dn_conv, dn_a_log=dn_a_log, dn_dt_bias=dn_dt_bias,
                   dn_norm=dn_norm, w_dn_out=w_dn_out, swa_q_norm=swa_q_norm, swa_k_norm=swa_k_norm, swa_sinks=swa_sinks,
                   w_swa_out=w_swa_out, w_o=w_o, norm_ffn=norm_ffn, w_up=w_up, ffn_conv=ffn_conv, ffn_conv_b=ffn_conv_b, w_down=w_down)
    mom_m = dict(w_ada=m_w_ada, b_ada=m_b_ada, norm_mix=m_norm_mix, w_in=m_w_in, dn_conv=m_dn_conv, dn_a_log=m_dn_a_log,
                 dn_dt_bias=m_dn_dt_bias, dn_norm=m_dn_norm, w_dn_out=m_w_dn_out, swa_q_norm=m_swa_q_norm, swa_k_norm=m_swa_k_norm,
                 swa_sinks=m_swa_sinks, w_swa_out=m_w_swa_out, w_o=m_w_o, norm_ffn=m_norm_ffn, w_up=m_w_up, ffn_conv=m_ffn_conv,
                 ffn_conv_b=m_ffn_conv_b, w_down=m_w_down)
    mom_v = dict(w_ada=v_w_ada, b_ada=v_b_ada, norm_mix=v_norm_mix, w_in=v_w_in, dn_conv=v_dn_conv, dn_a_log=v_dn_a_log,
                 dn_dt_bias=v_dn_dt_bias, dn_norm=v_dn_norm, w_dn_out=v_w_dn_out, swa_q_norm=v_swa_q_norm, swa_k_norm=v_swa_k_norm,
                 swa_sinks=v_swa_sinks, w_swa_out=v_w_swa_out, w_o=v_w_o, norm_ffn=v_norm_ffn, w_up=v_w_up, ffn_conv=v_ffn_conv,
                 ffn_conv_b=v_ffn_conv_b, w_down=v_w_down)
    order = ["w_ada", "b_ada", "norm_mix", "w_in", "dn_conv", "dn_a_log", "dn_dt_bias", "dn_norm", "w_dn_out", "swa_q_norm",
             "swa_k_norm", "swa_sinks", "w_swa_out", "w_o", "norm_ffn", "w_up", "ffn_conv", "ffn_conv_b", "w_down"]
    ax, ay, ac = lax.axis_index("x"), lax.axis_index("y"), lax.axis_index("c")
    chip = 2 * ax + ay
    dev = 4 * ax + 2 * ay + ac
    T = x.shape[1]
    xs = x[0]

    c_all = _allgather_all("gather_c", jnp.pad(c, ((0, 7), (0, 0)))).reshape(8, 8, D)[:, 0]
    c_act = _silu_rows("silu_c", jnp.pad(c_all, ((0, 8), (0, 0))))
    mod_sh = jnp.stack([
        _matmul("mod_fwd", c_act, w_ada[l].astype(bf16), "nn", f32,
                bias=lax.dynamic_slice(b_ada[l], (chip * 1536,), (1536,)).reshape(1, 1536)) for l in range(DEPTH)])
    mod_all = _allgather_all("gather_mod", mod_sh.reshape(DEPTH * 16 * 12, LANE)).reshape(8, DEPTH, 16, 1536)
    mod_me = jnp.concatenate([lax.dynamic_index_in_dim(mod_all[2 * s], dev, axis=1, keepdims=False) for s in range(4)], axis=1)

    tabs_q = _rope_tables(positions[0])
    tabs = (tabs_q, tuple(t[:, :LANE] for t in tabs_q))
    head = jnp.arange(LANE) // 64
    bd128 = (head[:, None] == head[None, :]).astype(f32) / 64.0
    bd = (bd128, bd128)

    conv_all = _allgather_all("gather_conv", _pack_small({"dn_conv": dn_conv, "ffn_conv": ffn_conv}, _CONV_SHARD))
    conv_parts = [_unpack_small(conv_all[2 * s], _CONV_SHARD) for s in range(4)]
    dn_conv_full = jnp.concatenate([p["dn_conv"].reshape(DEPTH, 4, 768) for p in conv_parts], axis=2)
    ffn_conv_full = jnp.concatenate([p["ffn_conv"].reshape(DEPTH, 3, 704) for p in conv_parts], axis=2)

    saved, Ws, Ks, mods = [], [], [], []
    h = xs
    shards = [[weights[nm][l].astype(bf16) for nm in _BIG] for l in range(DEPTH)]
    gathered = _gather_chips("gather_w", shards[0])
    for l in range(DEPTH):
        gathered = [lax.dynamic_update_index_in_dim(g, s, chip, 0) for g, s in zip(gathered, shards[l])]
        W = _assemble_weights(gathered)
        W["dn_conv"], W["ffn_conv"] = dn_conv_full[l], ffn_conv_full[l]
        K = _layer_consts({nm: weights[nm][l] for nm in ("norm_mix", "norm_ffn", "dn_a_log", "dn_dt_bias", "dn_norm", "swa_q_norm",
                                                          "swa_k_norm", "swa_sinks", "ffn_conv_b")})
        mod = tuple(mod_me[l, k * D:(k + 1) * D].reshape(1, D) for k in range(6))
        nxt = _hosted_gather(shards[l + 1]) if l + 1 < DEPTH else None
        h, S, arrived = _layer_fwd(h, mod, W, K, tabs, bd, hosted=nxt)
        if nxt is not None:
            gathered = _gather_forward("gather_w_pass", arrived)
        saved.append(S), Ws.append(W), Ks.append(K), mods.append(mod)

    loss_blk, dh = _loss("loss", h, loss_target[0])
    loss = lax.psum(loss_blk[0, 0], ("x", "y", "c"))

    grad_sh = [None] * DEPTH
    small = [None] * DEPTH
    dmods = [None] * DEPTH
    pending = None
    for l in reversed(range(DEPTH)):
        dh, gw, gs, dmod, b2 = _layer_bwd(dh, saved[l], mods[l], Ws[l], Ks[l], tabs, bd,
                                          hosted=None if pending is None else _hosted_scatter(pending))
        if pending is not None:
            grad_sh[l + 1] = dict(zip(_BIG, _rs_end(pending, b2, ac, chip)))
        pending = _rs_begin(_grad_slots(gw), ac)
        small[l], dmods[l] = dict(gs, dn_conv=gw["dn_conv"], ffn_conv=gw["ffn_conv"]), dmod[0]
    grad_sh[0] = dict(zip(_BIG, _rs_end(pending, _scatter_chips_multi("rs_scatter", pending), ac, chip)))

    spec_g = _SMALL + _CONV
    vals = {nm: jnp.stack([small[l][nm] for l in range(DEPTH)]) for nm, _ in spec_g if nm != "b_ada"}
    vals["b_ada"] = jnp.stack(dmods)
    small_all = _allgather_all("gather_small", _pack_small(vals, spec_g))
    g_small = _unpack_small(_sum_leading("sum_small", small_all), spec_g)
    dmod_all = jnp.stack([_unpack_small(small_all[d], spec_g)["b_ada"] for d in range(8)])
    dmod_sh = lax.dynamic_slice(dmod_all, (0, 0, chip * 1536), (8, DEPTH, 1536))
    dmod_sh = jnp.pad(dmod_sh, ((0, 8), (0, 0), (0, 0))).astype(bf16)
    g_w_ada = jnp.stack([_matmul("mod_bwd_w", c_act, dmod_sh[:, l], "tn", f32) for l in range(DEPTH)])

    grads = {nm: g_small[nm] for nm, _ in _SMALL}
    grads["dn_conv"] = lax.dynamic_slice(g_small["dn_conv"].reshape(DEPTH, 4, 3072), (0, 0, chip * 768), (DEPTH, 4, 768))
    grads["ffn_conv"] = lax.dynamic_slice(g_small["ffn_conv"].reshape(DEPTH, 3, D_FF), (0, 0, chip * 704), (DEPTH, 3, 704))
    grads["w_ada"] = g_w_ada
    for nm in _BIG:
        grads[nm] = jnp.stack([grad_sh[l][nm] for l in range(DEPTH)])

    delta, new_m, new_v = {}, {}, {}
    for nm in ("w_ada", "dn_conv", "ffn_conv") + _BIG:
        delta[nm], new_m[nm], new_v[nm] = _adamw("adamw_" + nm, weights[nm], grads[nm], mom_m[nm], mom_v[nm])
    sm = [_pack_small({nm: t[nm] for nm, _ in _SMALL}, _SMALL) for t in (weights, grads, mom_m, mom_v)]
    for tgt, buf in zip((delta, new_m, new_v), _adamw("adamw_small", *sm)):
        tgt.update(_unpack_small(buf, _SMALL))

    return (loss, dh[None], *[grads[n] for n in order], *[delta[n] for n in order], *[new_m[n] for n in order], *[new_v[n] for n in order])
```

```python
import functools

import jax
import jax.numpy as jnp
import numpy as np
from jax import lax
from jax.experimental import pallas as pl
from jax.experimental.pallas import tpu as pltpu

f32 = jnp.float32
bf16 = jnp.bfloat16
SDS = jax.ShapeDtypeStruct
HI = lax.Precision.HIGHEST
MESH = pl.DeviceIdType.MESH

D = 1024
DEPTH = 4
EPS = 1e-6
DN_C = 64
SWA_B = 128
LANE = 128
ROPE_THETA = 500000.0
D_FF = 2816
IN_TOTAL = 7440
PROJ_W = 7680
CB_Q, CB_K, CB_V = 0, 8, 16
CB_AB, CB_SWK, CB_SWV = 56, 57, 58
WB_Z, WB_SWQ, WB_GA, WB_GB = 3, 4, 5, 6
TR = 256
COMM_W = 1024
COMM_ROWS = 4864
COMM_TR = 128
VMEM_BIG = 48 * 2 ** 20

ADAM_LR, ADAM_B1, ADAM_B2, ADAM_EPS, ADAM_WD, ADAM_STEP = 0.001, 0.9, 0.999, 1e-08, 0.01, 10


def _pcall(body, **kw):
    return pl.pallas_call(body, **kw)


def _cparams(vmem=None):
    return pltpu.CompilerParams(vmem_limit_bytes=vmem) if vmem else None


def _dot(a, b, ca, cb, precision=HI):
    return lax.dot_general(a, b, (((ca,), (cb,)), ((), ())), precision=precision, preferred_element_type=f32)


def _pick(n, cands):
    for c in cands:
        if n % c == 0:
            return c
    return n


def _tile(n, cap):
    if n <= cap:
        return n
    best = None
    for t in range(LANE, cap + 1, LANE):
        if n % t == 0:
            best = t
    assert best is not None, (n, cap)
    return best


def _matmul(name, a, b, mode, out_dtype, bias=None, out_slots=None, hosted=None):
    h_args, h_specs, h_shapes, h_sems = _hosted_parts(hosted)
    nh = len(h_args)
    if mode == "nn":
        (M, K), (_, N) = a.shape, b.shape
    elif mode == "nt":
        (M, K), (N, _) = a.shape, b.shape
    else:
        (K, M), (_, N) = a.shape, b.shape
    tm = _tile(M, 1536 if mode == "tn" else 1024)
    tn = N // out_slots if out_slots else _tile(N, 1536)
    tk = _tile(K, 512 if mode == "tn" else (1024 if K <= 1024 else 1536))
    nk = K // tk
    ca, cb = {"nn": (1, 0), "nt": (1, 1), "tn": (0, 0)}[mode]

    grid = (M // tm, N // tn, nk)

    def body(*refs):
        a_ref, b_ref = refs[:2]
        nb = 1 if bias is not None else 0
        bias_ref = refs[2] if nb else None
        h_ins, o_ref, h_outs = refs[2 + nb:2 + nb + nh], refs[2 + nb + nh], refs[3 + nb + nh:3 + nb + 2 * nh]
        rest = refs[3 + nb + 2 * nh:]
        if hosted is not None:
            step = (pl.program_id(0) * grid[1] + pl.program_id(1)) * grid[2] + pl.program_id(2)
            sems = rest[1:] if nk > 1 else rest

            @pl.when(step == 0)
            def _():
                hosted["start"](h_ins, h_outs, *sems)

        def finish(r):
            if bias is not None:
                r = r + bias_ref[...]
            o_ref[...] = r.astype(o_ref.dtype)

        part = _dot(a_ref[...].astype(bf16), b_ref[...].astype(bf16), ca, cb, precision=None)
        if nk == 1:
            finish(part)
        else:
            acc = rest[0]
            k = pl.program_id(2)

            @pl.when(k == 0)
            def _():
                acc[...] = part

            @pl.when(k > 0)
            def _():
                acc[...] += part

            @pl.when(k == nk - 1)
            def _():
                finish(acc[...])

        if hosted is not None:
            @pl.when(step == grid[0] * grid[1] * grid[2] - 1)
            def _():
                hosted["finish"](h_ins, h_outs, *sems)

    a_spec =pl.BlockSpec((tk, tm), lambda i, j, k: (k, i)) if mode == "tn" else pl.BlockSpec((tm, tk), lambda i, j, k: (i, k))
    b_spec = pl.BlockSpec((tn, tk), lambda i, j, k: (j, k)) if mode == "nt" else pl.BlockSpec((tk, tn), lambda i, j, k: (k, j))
    in_specs = [a_spec, b_spec]
    args = [a, b]
    if bias is not None:
        in_specs.append(pl.BlockSpec((1, tn), lambda i, j, k: (0, j)))
        args.append(bias)
    if out_slots:
        out_spec = pl.BlockSpec((None, tm, tn), lambda i, j, k: (j, i, 0))
        out_shape = SDS((out_slots, M, tn), out_dtype)
    else:
        out_spec = pl.BlockSpec((tm, tn), lambda i, j, k: (i, j))
        out_shape = SDS((M, N), out_dtype)
    if hosted is None:
        return _pcall(
            body, grid=grid, in_specs=in_specs, out_specs=out_spec, out_shape=out_shape,
            scratch_shapes=[pltpu.VMEM((tm, tn), f32)] if nk > 1 else [], compiler_params=_cparams(VMEM_BIG), name=name)(*args)
    outs = _pcall(
        body, grid=grid, in_specs=in_specs + h_specs, out_specs=[out_spec] + h_specs, out_shape=[out_shape] + h_shapes,
        scratch_shapes=([pltpu.VMEM((tm, tn), f32)] if nk > 1 else []) + h_sems, compiler_params=_cparams(VMEM_BIG), name=name)(*args, *h_args)
    return outs[0], list(outs[1:])


def _row_specs(rows, tr):
    return [pl.BlockSpec((tr, w), lambda i, j, off=off: (i, off + j)) for (_, off, w) in rows]


def _rowwise_fwd(name, fn, rows, vecs, out_widths, out_dtypes, nc=1, tr=TR):
    T = rows[0][0].shape[0]
    n_in = len(rows) + len(vecs)

    def body(*refs):
        vals = [r[...].astype(f32) for r in refs[:n_in]]
        res = fn(*vals)
        for o_ref, r in zip(refs[n_in:], res):
            o_ref[...] = r.astype(o_ref.dtype)

    in_specs = _row_specs(rows, tr) + [pl.BlockSpec(v.shape, lambda i, j: (0, 0)) for v in vecs]
    out_specs = [pl.BlockSpec((tr, w), lambda i, j: (i, j)) for w in out_widths]
    out_shape = [SDS((T, w * nc), dt) for w, dt in zip(out_widths, out_dtypes)]
    return _pcall(body, grid=(T // tr, nc), in_specs=in_specs, out_specs=out_specs, out_shape=out_shape, name=name)(
        *[r[0] for r in rows], *vecs)


def _rowwise_bwd(name, fn, rows, vecs, cts, drow_dtypes, nc=1, tr=TR, add_to_first=None):
    T = rows[0][0].shape[0]
    n_r, n_v, n_c = len(rows), len(vecs), len(cts)
    n_add = 0 if add_to_first is None else 1
    keep = [k for k, dt in enumerate(drow_dtypes) if dt is not None]

    def body(*refs):
        n_in = n_r + n_v + n_c + n_add
        vals = [r[...].astype(f32) for r in refs[:n_in]]
        outs = refs[n_in:]
        i, j = pl.program_id(0), pl.program_id(1)
        _, vjp = jax.vjp(fn, *vals[:n_r + n_v])
        grads = vjp(tuple(vals[n_r + n_v:n_r + n_v + n_c]))
        for pos, k in enumerate(keep):
            g = grads[k]
            if n_add and pos == 0:
                g = g + vals[-1]
            outs[pos][...] = g.astype(outs[pos].dtype)

        @pl.when((i == 0) & (j == 0))
        def _():
            for q in range(n_v):
                outs[len(keep) + q][...] = jnp.zeros_like(outs[len(keep) + q])

        for q in range(n_v):
            outs[len(keep) + q][...] += grads[n_r + q]

    extra = [] if add_to_first is None else [add_to_first]
    in_specs = (_row_specs(rows, tr) + [pl.BlockSpec(v.shape, lambda i, j: (0, 0)) for v in vecs]
                + _row_specs(cts, tr) + _row_specs(extra, tr))
    out_specs = [pl.BlockSpec((tr, rows[k][2]), lambda i, j: (i, j)) for k in keep]
    out_specs += [pl.BlockSpec(v.shape, lambda i, j: (0, 0)) for v in vecs]
    out_shape = [SDS((T, rows[k][2] * nc), drow_dtypes[k]) for k in keep] + [SDS(v.shape, f32) for v in vecs]
    return _pcall(body, grid=(T // tr, nc), in_specs=in_specs, out_specs=out_specs, out_shape=out_shape, name=name)(
        *[r[0] for r in rows], *vecs, *[c[0] for c in cts], *[e[0] for e in extra])


def _normmod_fn(x, w, sc, sh):
    y = x * lax.rsqrt(jnp.mean(x * x, axis=-1, keepdims=True) + EPS)
    return ((y * w) * (1.0 + sc) + sh,)


def _resid_fn(x, t, gt):
    return (x + gt * t,)


def _merge_fn(ga, gb, ya, yb):
    return (jax.nn.sigmoid(ga) * ya + jax.nn.sigmoid(gb) * yb,)


def _dngate_fn(o, z, w):
    y = o * lax.rsqrt(jnp.mean(o * o, axis=-1, keepdims=True) + EPS)
    return ((y * w) * (z * jax.nn.sigmoid(z)),)


def _conv_taps(x, w_ref, taps, buf):
    w = lambda s: w_ref[taps - 1 - s:taps - s, :]
    row = lax.broadcasted_iota(jnp.int32, (8, x.shape[1]), 0)
    x8 = x[0:8]
    acc, acc8 = x * w(0), x8 * w(0)
    for s in range(1, taps):
        acc = acc + pltpu.roll(x, s, 0) * w(s)
        acc8 = acc8 + jnp.where(row >= s, pltpu.roll(x8, s, 0), 0.0) * w(s)
    buf[...] = acc
    buf[0:8, :] = acc8
    return buf[...]


def _conv_taps_bwd(x, dy, w_ref, dw_ref, taps, buf):
    T = x.shape[0]
    w = lambda s: w_ref[taps - 1 - s:taps - s, :]
    row = lax.broadcasted_iota(jnp.int32, (8, x.shape[1]), 0)
    dy_first, dy_last = dy[0:8], dy[T - 8:T]
    dx, dx_last = dy * w(0), dy_last * w(0)
    dw_ref[taps - 1:taps, :] = jnp.sum(dy * x, axis=0, keepdims=True)
    for s in range(1, taps):
        dx = dx + pltpu.roll(dy, T - s, 0) * w(s)
        dx_last = dx_last + jnp.where(row < 8 - s, pltpu.roll(dy_last, 8 - s, 0), 0.0) * w(s)
        xr = pltpu.roll(x, s, 0)
        wrapped = jnp.sum(jnp.where(row < s, dy_first * xr[0:8], 0.0), axis=0, keepdims=True)
        dw_ref[taps - 1 - s:taps - s, :] = jnp.sum(dy * xr, axis=0, keepdims=True) - wrapped
    buf[...] = dx
    buf[T - 8:T, :] = dx_last
    return buf[...]


def _dn_act(y, normalize):
    s = y * jax.nn.sigmoid(y)
    if normalize:
        s = s * lax.rsqrt(jnp.sum(s * s, axis=-1, keepdims=True) + EPS)
    return s


def _dnconv_fwd(name, proj, cb, w, normalize):
    T = proj.shape[0]

    def body(x_ref, w_ref, o_ref, buf):
        o_ref[...] = _dn_act(_conv_taps(x_ref[...], w_ref, 4, buf), normalize)

    return _pcall(
        body, grid=(8,), in_specs=[pl.BlockSpec((T, LANE), lambda j: (0, cb + j)), pl.BlockSpec((4, LANE), lambda j: (0, cb + j))],
        out_specs=pl.BlockSpec((T, LANE), lambda j: (0, j)), out_shape=SDS((T, 1024), f32), scratch_shapes=[pltpu.VMEM((T, LANE), f32)],
        compiler_params=_cparams(VMEM_BIG), name=name)(proj, w)


def _dnconv_bwd(name, proj, cb, w, dout, normalize):
    T = proj.shape[0]

    def body(x_ref, w_ref, do_ref, dx_ref, dw_ref, buf, buf2):
        x = x_ref[...]
        y = _conv_taps(x, w_ref, 4, buf)
        _, vjp = jax.vjp(functools.partial(_dn_act, normalize=normalize), y)
        (dy,) = vjp(do_ref[...])
        dx_ref[...] = _conv_taps_bwd(x, dy, w_ref, dw_ref, 4, buf2).astype(dx_ref.dtype)

    return _pcall(
        body, grid=(8,),
        in_specs=[pl.BlockSpec((T, LANE), lambda j: (0, cb + j)), pl.BlockSpec((4, LANE), lambda j: (0, cb + j)),
                  pl.BlockSpec((T, LANE), lambda j: (0, j))],
        out_specs=[pl.BlockSpec((T, LANE), lambda j: (0, j)), pl.BlockSpec((4, LANE), lambda j: (0, j))],
        out_shape=[SDS((T, 1024), bf16), SDS((4, 1024), f32)], scratch_shapes=[pltpu.VMEM((T, LANE), f32)] * 2,
        compiler_params=_cparams(VMEM_BIG), name=name)(proj, w, dout)


def _ffn_point(a, lin):
    return a * jax.nn.sigmoid(a) * lin


def _ffnact_fwd(name, up, w, b):
    T = up.shape[0]
    nblk = D_FF // LANE

    def body(a_ref, l_ref, w_ref, b_ref, o_ref, buf):
        a = _conv_taps(a_ref[...], w_ref, 3, buf) + b_ref[...]
        o_ref[...] = _ffn_point(a, l_ref[...]).astype(o_ref.dtype)

    return _pcall(
        body, grid=(nblk,),
        in_specs=[pl.BlockSpec((T, LANE), lambda j: (0, j)), pl.BlockSpec((T, LANE), lambda j: (0, nblk + j)),
                  pl.BlockSpec((3, LANE), lambda j: (0, j)), pl.BlockSpec((1, LANE), lambda j: (0, j))],
        out_specs=pl.BlockSpec((T, LANE), lambda j: (0, j)), out_shape=SDS((T, D_FF), bf16), scratch_shapes=[pltpu.VMEM((T, LANE), f32)],
        compiler_params=_cparams(VMEM_BIG), name=name)(up, up, w, b)


def _ffnact_bwd(name, up, w, b, dmid):
    T = up.shape[0]
    nblk = D_FF // LANE

    def body(a_ref, l_ref, w_ref, b_ref, dm_ref, da_ref, dl_ref, dw_ref, db_ref, buf, buf2):
        x = a_ref[...]
        a = _conv_taps(x, w_ref, 3, buf) + b_ref[...]
        _, vjp = jax.vjp(_ffn_point, a, l_ref[...])
        da, dl = vjp(dm_ref[...].astype(f32))
        dl_ref[...] = dl.astype(dl_ref.dtype)
        db_ref[...] = jnp.sum(da, axis=0, keepdims=True)
        da_ref[...] = _conv_taps_bwd(x, da, w_ref, dw_ref, 3, buf2).astype(da_ref.dtype)

    col = lambda r: pl.BlockSpec((r, LANE), lambda j: (0, j))
    return _pcall(
        body, grid=(nblk,),
        in_specs=[col(T), pl.BlockSpec((T, LANE), lambda j: (0, nblk + j)), col(3), col(1), col(T)],
        out_specs=[col(T), col(T), col(3), col(1)],
        out_shape=[SDS((T, D_FF), bf16), SDS((T, D_FF), bf16), SDS((3, D_FF), f32), SDS((1, D_FF), f32)],
        scratch_shapes=[pltpu.VMEM((T, LANE), f32)] * 2, compiler_params=_cparams(VMEM_BIG), name=name)(up, up, w, b, dmid)


def _bmm(a, b, ca, cb, precision=HI):
    return lax.dot_general(a, b, (((ca,), (cb,)), ((0,), (0,))), precision=precision, preferred_element_type=f32)


def _make_bdot(ca, cb):
    def raw(x, y, cx, cy):
        return _bmm(x.astype(bf16), y.astype(bf16), cx, cy, precision=None)

    @jax.custom_vjp
    def f(a, b):
        return raw(a, b, ca, cb)

    def fwd(a, b):
        return raw(a, b, ca, cb), (a, b)

    def bwd(res, dy):
        a, b = res
        if (ca, cb) == (2, 1):
            return raw(dy, b, 2, 2), raw(a, dy, 1, 1)
        if (ca, cb) == (2, 2):
            return raw(dy, b, 2, 1), raw(dy, a, 1, 1)
        return raw(b, dy, 2, 2), raw(a, dy, 2, 1)

    f.defvjp(fwd, bwd)
    return f


_bdot_nn, _bdot_nt, _bdot_tn = _make_bdot(2, 1), _make_bdot(2, 2), _make_bdot(1, 1)


def _pieces(a, n):
    out, r = [], a
    for _ in range(n):
        p = r.astype(bf16)
        out.append(p)
        r = r - p.astype(f32)
    return out


def _bmm_split(x, y, cx, cy, nx=2, ny=2, order=1):
    xs, ys = _pieces(x, nx), _pieces(y, ny)
    acc = None
    for i in reversed(range(nx)):
        for j in reversed(range(ny)):
            if i + j <= order:
                t = _bmm(xs[i], ys[j], cx, cy, precision=None)
                acc = t if acc is None else acc + t
    return acc


@jax.custom_vjp
def _solve_apply(X, r):
    return _bmm_split(X, r, 2, 1)


def _solve_apply_fwd(X, r):
    return _bmm_split(X, r, 2, 1), (X, r)


def _solve_apply_bwd(res, dy):
    X, r = res
    return _bmm_split(dy, r, 2, 2), _bmm_split(X, dy, 1, 1)


_solve_apply.defvjp(_solve_apply_fwd, _solve_apply_bwd)


def _lower_ones(H, C):
    ri = lax.broadcasted_iota(jnp.int32, (H, C, C), 1)
    ci = lax.broadcasted_iota(jnp.int32, (H, C, C), 2)
    return (ri >= ci).astype(f32)


def _cumsum_rows_raw(G):
    return _bmm_split(_lower_ones(G.shape[0], G.shape[1]), G, 2, 1, nx=1, ny=3, order=2)


@jax.custom_vjp
def _cumsum_rows(G):
    return _cumsum_rows_raw(G)


def _cumsum_rows_fwd(G):
    return _cumsum_rows_raw(G), None


def _cumsum_rows_bwd(_, dy):
    return (_bmm_split(_lower_ones(dy.shape[0], dy.shape[1]), dy, 1, 1, nx=1, ny=3, order=2),)


_cumsum_rows.defvjp(_cumsum_rows_fwd, _cumsum_rows_bwd)


def _tri_inverse_raw(L):
    H, C, _ = L.shape
    ri = lax.broadcasted_iota(jnp.int32, (C, C), 0)
    ci = lax.broadcasted_iota(jnp.int32, (C, C), 1)
    eye = jnp.broadcast_to((ri == ci).astype(f32)[None], (H, C, C))
    Dg = jnp.where(((ri >> 3) == (ci >> 3))[None], L, 0.0)
    D2 = _bmm_split(Dg, Dg, 2, 1)
    X = _bmm_split(_bmm_split(eye - Dg, eye + D2, 2, 1), eye + _bmm_split(D2, D2, 2, 1), 2, 1)
    for lg in range(3, C.bit_length() - 1):
        same = (ri >> (lg + 1)) == (ci >> (lg + 1))
        lower_left = same & (((ri >> lg) & 1) == 1) & (((ci >> lg) & 1) == 0)
        X = X - _bmm_split(_bmm_split(X, jnp.where(lower_left[None], L, 0.0), 2, 1), X, 2, 1)
    return X


@jax.custom_vjp
def _tri_inverse(L):
    return _tri_inverse_raw(L)


def _tri_inverse_fwd(L):
    X = _tri_inverse_raw(L)
    return X, X


def _tri_inverse_bwd(X, dX):
    return (-_bmm_split(_bmm_split(X, dX, 1, 1), X, 2, 2),)


_tri_inverse.defvjp(_tri_inverse_fwd, _tri_inverse_bwd)


def _gdn_chunk(q, k, v, ab, alog, dtb, S):
    H, C, _ = q.shape
    lane = lax.broadcasted_iota(jnp.int32, (H, C, LANE), 2)
    head = lax.broadcasted_iota(jnp.int32, (H, C, LANE), 0)
    abb = jnp.broadcast_to(ab[None], (H, C, LANE))
    a = jnp.sum(jnp.where(lane == head, abb, 0.0), axis=2, keepdims=True)
    b = jnp.sum(jnp.where(lane == head + 8, abb, 0.0), axis=2, keepdims=True)
    pick = lax.broadcasted_iota(jnp.int32, (H, 1, LANE), 2) == lax.broadcasted_iota(jnp.int32, (H, 1, LANE), 0)
    al = jnp.sum(jnp.where(pick, alog[None], 0.0), axis=2, keepdims=True)
    db = jnp.sum(jnp.where(pick, dtb[None], 0.0), axis=2, keepdims=True)
    g = -jnp.exp(al) * jax.nn.softplus(a + db)
    beta = jax.nn.sigmoid(b)
    ri = lax.broadcasted_iota(jnp.int32, (C, C), 0)
    ci = lax.broadcasted_iota(jnp.int32, (C, C), 1)
    G = jnp.broadcast_to(g, (H, C, LANE))
    gc = _cumsum_rows(G)
    gi = _cumsum_rows(jnp.broadcast_to(g, (H, C, C)))
    decay = jnp.exp(jnp.where((ri >= ci)[None], gi - jnp.swapaxes(gi, 1, 2), -jnp.inf))
    qs = q * (LANE ** -0.5)
    kb = k * beta
    X = _tri_inverse(jnp.where((ri > ci)[None], _bdot_nt(kb, k) * decay, 0.0))
    egc = jnp.exp(gc)
    u = _solve_apply(X, v * beta)
    w = _solve_apply(X, kb * egc)
    qk = _bdot_nt(qs, k) * decay
    g_last = jnp.sum(G, axis=1, keepdims=True)
    k_dec = k * jnp.exp(g_last - gc)
    v_new = u - _bdot_nn(w, S)
    o = _bdot_nn(qs * egc, S) + _bdot_nn(qk, v_new)
    S_new = S * jnp.exp(g_last) + _bdot_tn(k_dec, v_new)
    return o, S_new


def _heads(ref):
    return jnp.stack([ref[:, LANE * h:LANE * (h + 1)] for h in range(8)], axis=0)


def _put_heads(ref, val):
    for h in range(8):
        ref[:, LANE * h:LANE * (h + 1)] = val[h]


def _hosted_parts(hosted):
    if hosted is None:
        return [], [], [], []
    n = len(hosted["arrays"])
    return list(hosted["arrays"]), [HBM_SPEC] * n, list(hosted["out_shape"]), _dma_sems(hosted["n_sems"])


def _gdn_fwd(name, q, k, v, proj, alog, dtb, hosted=None):
    T = q.shape[0]
    N = T // DN_C
    h_args, h_specs, h_shapes, h_sems = _hosted_parts(hosted)
    nh = len(h_args)

    def body(q_ref, k_ref, v_ref, ab_ref, al_ref, dt_ref, *rest):
        h_ins, (o_ref, sall_ref), h_outs, s_scr, sems = rest[:nh], rest[nh:nh + 2], rest[nh + 2:2 * nh + 2], rest[2 * nh + 2], rest[2 * nh + 3:]
        step = pl.program_id(0)

        @pl.when(step == 0)
        def _():
            s_scr[...] = jnp.zeros_like(s_scr)
            if hosted is not None:
                hosted["start"](h_ins, h_outs, *sems)

        S = s_scr[...]
        sall_ref[...] = S
        o, S_new = _gdn_chunk(_heads(q_ref), _heads(k_ref), _heads(v_ref), ab_ref[...], al_ref[...], dt_ref[...], S)
        _put_heads(o_ref, o)
        s_scr[...] = S_new

        if hosted is not None:
            @pl.when(step == N - 1)
            def _():
                hosted["finish"](h_ins, h_outs, *sems)

    blk = pl.BlockSpec((DN_C, 8 * LANE), lambda n: (n, 0))
    vec = pl.BlockSpec((1, LANE), lambda n: (0, 0))
    state = pl.BlockSpec((None, 8, LANE, LANE), lambda n: (n, 0, 0, 0))
    outs = _pcall(
        body, grid=(N,), in_specs=[blk, blk, blk, pl.BlockSpec((DN_C, LANE), lambda n: (n, CB_AB)), vec, vec] + h_specs,
        out_specs=[blk, state] + h_specs, out_shape=[SDS((T, 1024), f32), SDS((N, 8, LANE, LANE), f32)] + h_shapes,
        scratch_shapes=[pltpu.VMEM((8, LANE, LANE), f32)] + h_sems, name=name)(q, k, v, proj, alog, dtb, *h_args)
    return outs[0], outs[1], list(outs[2:])


def _gdn_bwd(name, q, k, v, proj, alog, dtb, sall, do, hosted=None):
    T = q.shape[0]
    N = T // DN_C
    h_args, h_specs, h_shapes, h_sems = _hosted_parts(hosted)
    nh = len(h_args)

    def body(q_ref, k_ref, v_ref, ab_ref, al_ref, dt_ref, s_ref, do_ref, *rest):
        h_ins, h_outs, ds_scr, sems = rest[:nh], rest[nh + 6:2 * nh + 6], rest[2 * nh + 6], rest[2 * nh + 7:]
        dq_ref, dk_ref, dv_ref, dab_ref, dal_ref, ddt_ref = rest[nh:nh + 6]
        step = pl.program_id(0)

        @pl.when(step == 0)
        def _():
            ds_scr[...] = jnp.zeros_like(ds_scr)
            dal_ref[...] = jnp.zeros_like(dal_ref)
            ddt_ref[...] = jnp.zeros_like(ddt_ref)
            if hosted is not None:
                hosted["start"](h_ins, h_outs, *sems)

        _, vjp = jax.vjp(_gdn_chunk, _heads(q_ref), _heads(k_ref), _heads(v_ref), ab_ref[...], al_ref[...], dt_ref[...], s_ref[...])
        dq, dk, dv, dab, dal, ddt, dS = vjp((_heads(do_ref), ds_scr[...]))
        _put_heads(dq_ref, dq)
        _put_heads(dk_ref, dk)
        _put_heads(dv_ref, dv)
        ds_scr[...] = dS
        dab_ref[...] = dab
        dal_ref[...] += dal
        ddt_ref[...] += ddt

        if hosted is not None:
            @pl.when(step == N - 1)
            def _():
                hosted["finish"](h_ins, h_outs, *sems)

    blk = pl.BlockSpec((DN_C, 8 * LANE), lambda n: (N - 1 - n, 0))
    vec = pl.BlockSpec((1, LANE), lambda n: (0, 0))
    state = pl.BlockSpec((None, 8, LANE, LANE), lambda n: (N - 1 - n, 0, 0, 0))
    outs = _pcall(
        body, grid=(N,),
        in_specs=[blk, blk, blk, pl.BlockSpec((DN_C, LANE), lambda n: (N - 1 - n, CB_AB)), vec, vec, state, blk] + h_specs,
        out_specs=[blk, blk, blk, pl.BlockSpec((DN_C, LANE), lambda n: (N - 1 - n, 0)), vec, vec] + h_specs,
        out_shape=[SDS((T, 1024), f32)] * 3 + [SDS((T, LANE), f32), SDS((1, LANE), f32), SDS((1, LANE), f32)] + h_shapes,
        scratch_shapes=[pltpu.VMEM((8, LANE, LANE), f32)] + h_sems, name=name)(q, k, v, proj, alog, dtb, sall, do, *h_args)
    return tuple(outs[:6]), list(outs[6:])


def _segmean_raw(x2, bd):
    return jnp.concatenate([_dot(x2[:, LANE * j:LANE * (j + 1)], bd, 1, 0) for j in range(x2.shape[1] // LANE)], axis=1)


@jax.custom_vjp
def _segmean(x2, bd):
    return _segmean_raw(x2, bd)


def _segmean_fwd(x2, bd):
    return _segmean_raw(x2, bd), bd


def _segmean_bwd(bd, dy):
    return _segmean_raw(dy, bd), jnp.zeros_like(bd)


_segmean.defvjp(_segmean_fwd, _segmean_bwd)


def _qknorm_fn(x, w, bd):
    return x * lax.rsqrt(_segmean(x * x, bd) + EPS) * w


def _rope_apply(xn, c, s1, s2):
    W = xn.shape[1]
    return xn * c + pltpu.roll(xn, W - 8, 1) * s1 + pltpu.roll(xn, 8, 1) * s2


def _rope_apply_t(d, c, s1, s2):
    W = d.shape[1]
    return d * c + pltpu.roll(d * s1, 8, 1) + pltpu.roll(d * s2, W - 8, 1)


def _qkprep_fwd(name, proj, wb, width, w, bd, tabs):
    T = proj.shape[0]
    tr = 128

    def body(x_ref, w_ref, bd_ref, c_ref, s1_ref, s2_ref, o_ref):
        xn = _qknorm_fn(x_ref[...], w_ref[...], bd_ref[...])
        o_ref[...] = _rope_apply(xn, c_ref[...], s1_ref[...], s2_ref[...])

    row0 = pl.BlockSpec((tr, width), lambda i: (i, 0))
    full = lambda a: pl.BlockSpec(a.shape, lambda i: (0, 0))
    return _pcall(
        body, grid=(T // tr,), in_specs=[pl.BlockSpec((tr, width), lambda i: (i, wb)), full(w), full(bd), row0, row0, row0],
        out_specs=row0, out_shape=SDS((T, width), f32), name=name)(proj, w, bd, *tabs)


def _qkprep_bwd(name, proj, wb, width, w, bd, tabs, dout):
    T = proj.shape[0]
    tr = 128

    def body(x_ref, w_ref, bd_ref, c_ref, s1_ref, s2_ref, do_ref, dx_ref, dw_ref):
        i = pl.program_id(0)
        dxn = _rope_apply_t(do_ref[...], c_ref[...], s1_ref[...], s2_ref[...])
        bd = bd_ref[...]
        _, vjp = jax.vjp(lambda x, w_: _qknorm_fn(x, w_, bd), x_ref[...], w_ref[...])
        dx, dw = vjp(dxn)
        dx_ref[...] = dx.astype(dx_ref.dtype)

        @pl.when(i == 0)
        def _():
            dw_ref[...] = jnp.zeros_like(dw_ref)

        dw_ref[...] += dw

    row0 = pl.BlockSpec((tr, width), lambda i: (i, 0))
    full = lambda a: pl.BlockSpec(a.shape, lambda i: (0, 0))
    return _pcall(
        body, grid=(T // tr,), in_specs=[pl.BlockSpec((tr, width), lambda i: (i, wb)), full(w), full(bd), row0, row0, row0, row0],
        out_specs=[row0, full(w)], out_shape=[SDS((T, width), bf16), SDS(w.shape, f32)], name=name)(proj, w, bd, *tabs, dout)


def _make_dot16(ca, cb):
    def raw(x, y, cx, cy):
        return _dot(x.astype(bf16), y.astype(bf16), cx, cy, precision=None)

    @jax.custom_vjp
    def f(a, b):
        return raw(a, b, ca, cb)

    def fwd(a, b):
        return raw(a, b, ca, cb), (a, b)

    def bwd(res, dy):
        a, b = res
        if (ca, cb) == (1, 0):
            return raw(dy, b, 1, 1), raw(a, dy, 0, 0)
        return raw(dy, b, 1, 0), raw(dy, a, 0, 0)

    f.defvjp(fwd, bwd)
    return f


_dot16_nn, _dot16_nt = _make_dot16(1, 0), _make_dot16(1, 1)


def _attn_group(qg, kb, vb, sinks, first, hk):
    R = qg.shape[0]
    s = _dot16_nt(qg, kb) * 0.125
    qi = lax.broadcasted_iota(jnp.int32, (R, 2 * SWA_B), 0) & (SWA_B - 1)
    kj = lax.broadcasted_iota(jnp.int32, (R, 2 * SWA_B), 1)
    rel = qi + SWA_B - kj
    mask = (rel >= 0) & (rel < SWA_B) & ((kj >= SWA_B) | jnp.logical_not(first))
    s = jnp.where(mask, s, -jnp.inf)
    head = (lax.broadcasted_iota(jnp.int32, (R, LANE), 0) >> 7) + 8 * hk
    lane = lax.broadcasted_iota(jnp.int32, (R, LANE), 1)
    sink = jnp.sum(jnp.where(lane == head, jnp.broadcast_to(sinks, (R, LANE)), 0.0), axis=1, keepdims=True)
    m = lax.stop_gradient(jnp.maximum(jnp.max(s, axis=1, keepdims=True), sink))
    p = jnp.exp(s - m)
    denom = jnp.sum(p, axis=1, keepdims=True) + jnp.exp(sink - m)
    return _dot16_nn(p / denom, vb)


def _group_rows(ref, hk):
    return jnp.concatenate([ref[:, 64 * (8 * hk + g):64 * (8 * hk + g + 1)] for g in range(8)], axis=0)


def _put_group(ref, hk, val):
    for g in range(8):
        ref[:, 64 * (8 * hk + g):64 * (8 * hk + g + 1)] = val[SWA_B * g:SWA_B * (g + 1)].astype(ref.dtype)


def _attn_specs():
    qs = pl.BlockSpec((SWA_B, 1024), lambda i: (i, 0))
    cur = pl.BlockSpec((SWA_B, LANE), lambda i: (i, 0))
    prev = pl.BlockSpec((SWA_B, LANE), lambda i: (jnp.maximum(i - 1, 0), 0))
    vcur = pl.BlockSpec((SWA_B, LANE), lambda i: (i, CB_SWV))
    vprev = pl.BlockSpec((SWA_B, LANE), lambda i: (jnp.maximum(i - 1, 0), CB_SWV))
    vec = pl.BlockSpec((1, LANE), lambda i: (0, 0))
    return qs, cur, prev, vcur, vprev, vec


def _attn_fwd(name, sq, sk, proj, sinks):
    T = sq.shape[0]

    def body(q_ref, kp_ref, kc_ref, vp_ref, vc_ref, sk_ref, o_ref):
        first = pl.program_id(0) == 0
        sinks_v = sk_ref[...]
        for hk in range(2):
            ks = slice(64 * hk, 64 * hk + 64)
            kb = jnp.concatenate([kp_ref[:, ks], kc_ref[:, ks]], axis=0)
            vb = jnp.concatenate([vp_ref[:, ks], vc_ref[:, ks]], axis=0)
            _put_group(o_ref, hk, _attn_group(_group_rows(q_ref, hk), kb, vb, sinks_v, first, hk))

    qs, cur, prev, vcur, vprev, vec = _attn_specs()
    return _pcall(body, grid=(T // SWA_B,), in_specs=[qs, prev, cur, vprev, vcur, vec], out_specs=qs,
                  out_shape=SDS((T, 1024), bf16), name=name)(sq, sk, sk, proj, proj, sinks)


def _attn_bwd(name, sq, sk, proj, sinks, do):
    T = sq.shape[0]

    def body(q_ref, kp_ref, kc_ref, vp_ref, vc_ref, sk_ref, do_ref, dq_ref, dkp_ref, dkc_ref, dvp_ref, dvc_ref, dsk_ref):
        first = pl.program_id(0) == 0

        @pl.when(first)
        def _():
            dsk_ref[...] = jnp.zeros_like(dsk_ref)

        sinks_v = sk_ref[...]
        dsk = jnp.zeros((1, LANE), f32)
        for hk in range(2):
            ks = slice(64 * hk, 64 * hk + 64)
            kb = jnp.concatenate([kp_ref[:, ks], kc_ref[:, ks]], axis=0)
            vb = jnp.concatenate([vp_ref[:, ks], vc_ref[:, ks]], axis=0)
            _, vjp = jax.vjp(functools.partial(_attn_group, first=first, hk=hk), _group_rows(q_ref, hk), kb, vb, sinks_v)
            dq, dkb, dvb, ds_ = vjp(_group_rows(do_ref, hk))
            _put_group(dq_ref, hk, dq)
            dsk = dsk + ds_
            dkp_ref[:, ks] = dkb[:SWA_B]
            dkc_ref[:, ks] = dkb[SWA_B:]
            dvp_ref[:, ks] = dvb[:SWA_B]
            dvc_ref[:, ks] = dvb[SWA_B:]
        dsk_ref[...] += dsk

    qs, cur, prev, vcur, vprev, vec = _attn_specs()
    return _pcall(
        body, grid=(T // SWA_B,), in_specs=[qs, prev, cur, vprev, vcur, vec, qs], out_specs=[qs, cur, cur, cur, cur, vec],
        out_shape=[SDS((T, 1024), f32)] + [SDS((T, LANE), f32)] * 4 + [SDS((1, LANE), f32)], name=name)(sq, sk, sk, proj, proj, sinks, do)


def _shift_add(name, cur, prev, out_dtype):
    T = cur.shape[0]
    nb = T // SWA_B

    def body(c_ref, p_ref, o_ref):
        has_next = (pl.program_id(0) + 1 < nb).astype(f32)
        o_ref[...] = (c_ref[...] + has_next * p_ref[...]).astype(o_ref.dtype)

    blk = pl.BlockSpec((SWA_B, LANE), lambda i: (i, 0))
    nxt = pl.BlockSpec((SWA_B, LANE), lambda i: (jnp.minimum(i + 1, nb - 1), 0))
    return _pcall(body, grid=(nb,), in_specs=[blk, nxt], out_specs=blk, out_shape=SDS((T, LANE), out_dtype), name=name)(cur, prev)


def _loss(name, y, tgt):
    T = y.shape[0]

    def body(y_ref, t_ref, l_ref, dy_ref):
        @pl.when(pl.program_id(0) == 0)
        def _():
            l_ref[...] = jnp.zeros_like(l_ref)

        d = y_ref[...] - t_ref[...]
        l_ref[...] += jnp.sum(d * d) * (0.5 / D)
        dy_ref[...] = d * (1.0 / D)

    row = pl.BlockSpec((TR, D), lambda i: (i, 0))
    return _pcall(body, grid=(T // TR,), in_specs=[row, row], out_specs=[pl.BlockSpec((8, LANE), lambda i: (0, 0)), row],
                  out_shape=[SDS((8, LANE), f32), SDS((T, D), f32)], name=name)(y, tgt)


def _adamw(name, w, g, m, v):
    shape = w.shape
    C = shape[-1]
    R = int(np.prod(shape[:-1]))
    tr = _pick(R, (128, 64, 16, 8))
    bc1 = np.float32(1.0 - ADAM_B1 ** ADAM_STEP)
    bc2 = np.float32(1.0 - ADAM_B2 ** ADAM_STEP)

    def body(w_ref, g_ref, m_ref, v_ref, d_ref, mo_ref, vo_ref):
        g_ = g_ref[...]
        m_ = ADAM_B1 * m_ref[...] + (1.0 - ADAM_B1) * g_
        v_ = ADAM_B2 * v_ref[...] + (1.0 - ADAM_B2) * (g_ * g_)
        d_ref[...] = -ADAM_LR * ((m_ / bc1) / (jnp.sqrt(v_ / bc2) + ADAM_EPS) + ADAM_WD * w_ref[...])
        mo_ref[...] = m_
        vo_ref[...] = v_

    blk = pl.BlockSpec((tr, C), lambda i: (i, 0))
    outs = _pcall(body, grid=(R // tr,), in_specs=[blk] * 4, out_specs=[blk] * 3, out_shape=[SDS((R, C), f32)] * 3,
                  compiler_params=_cparams(VMEM_BIG), name=name)(*[t.reshape(R, C) for t in (w, g, m, v)])
    return [o.reshape(shape) for o in outs]


def _silu_rows(name, x):
    def body(x_ref, o_ref):
        t = x_ref[...]
        o_ref[...] = (t * jax.nn.sigmoid(t)).astype(o_ref.dtype)

    return _pcall(body, out_shape=SDS(x.shape, bf16), name=name)(x)


def _sum_leading(name, x):
    n = x.shape[0]

    def body(x_ref, o_ref):
        acc = x_ref[0]
        for k in range(1, n):
            acc = acc + x_ref[k]
        o_ref[...] = acc

    tr = _pick(x.shape[1], (COMM_TR, 8))
    return _pcall(body, grid=(x.shape[1] // tr,), in_specs=[pl.BlockSpec((n, tr, x.shape[2]), lambda i: (0, i, 0))],
                  out_specs=pl.BlockSpec((tr, x.shape[2]), lambda i: (i, 0)), out_shape=SDS(x.shape[1:], x.dtype), name=name)(x)


def _add_my_half(name, g4, b1, c):
    _, R, W = g4.shape
    nblk = (R // 2) // COMM_TR

    def body(c_ref, g_ref, b_ref, o_ref):
        o_ref[...] = g_ref[...] + b_ref[...]

    grid_spec = pltpu.PrefetchScalarGridSpec(
        num_scalar_prefetch=1, grid=(4, nblk),
        in_specs=[pl.BlockSpec((None, COMM_TR, W), lambda s, i, c_ref: (s, c_ref[0] * nblk + i, 0)),
                  pl.BlockSpec((None, COMM_TR, W), lambda s, i, c_ref: (s, i, 0))],
        out_specs=pl.BlockSpec((None, COMM_TR, W), lambda s, i, c_ref: (s, i, 0)))
    return _pcall(body, grid_spec=grid_spec, out_shape=SDS((4, R // 2, W), f32), name=name)(c.reshape(1), g4, b1)


HBM_SPEC = pl.BlockSpec(memory_space=pltpu.HBM)


def _position():
    x, y, c = lax.axis_index("x"), lax.axis_index("y"), lax.axis_index("c")
    return x, y, c, [(1 - x, y), (x, 1 - y), (1 - x, 1 - y)]


def _remote(src, dst, send_sems, recv_sems, k, to):
    return pltpu.make_async_remote_copy(src_ref=src, dst_ref=dst, send_sem=send_sems.at[k], recv_sem=recv_sems.at[k],
                                        device_id=to, device_id_type=MESH)


def _allgather_chips(name, buf):
    R, W = buf.shape
    Rh = R // 2

    def body(in_ref, out_ref, send_sems, recv_sems, local_sem):
        x, y, c, chips = _position()
        me = 2 * x + y
        sib = (x, y, 1 - c)
        half = pl.ds(pl.multiple_of(c * Rh, 32), Rh)
        ohalf = pl.ds(pl.multiple_of((1 - c) * Rh, 32), Rh)
        mine = pltpu.make_async_copy(in_ref, out_ref.at[me], local_sem)
        mine.start()
        first = [_remote(in_ref.at[half], out_ref.at[me, half], send_sems, recv_sems, j, (cx, cy, c)) for j, (cx, cy) in enumerate(chips)]
        for cp in first:
            cp.start()
        passed = []
        for j, (cx, cy) in enumerate(chips):
            rows = out_ref.at[2 * cx + cy, half]
            _remote(rows, rows, send_sems, recv_sems, j, (cx, cy, c)).wait_recv()
            cp = _remote(rows, rows, send_sems, recv_sems, 3 + j, sib)
            cp.start()
            passed.append(cp)
        for j, (cx, cy) in enumerate(chips):
            rows = out_ref.at[2 * cx + cy, ohalf]
            _remote(rows, rows, send_sems, recv_sems, 3 + j, sib).wait_recv()
        for cp in first + passed:
            cp.wait_send()
        mine.wait()

    return _pcall(body, in_specs=[HBM_SPEC], out_specs=HBM_SPEC, out_shape=SDS((4, R, W), buf.dtype),
                  scratch_shapes=[pltpu.SemaphoreType.DMA((6,)), pltpu.SemaphoreType.DMA((6,)), pltpu.SemaphoreType.DMA], name=name)(buf)


def _swap_halves(name, g4):
    _, R, W = g4.shape
    Rh = R // 2

    def body(in_ref, out_ref, send_sems, recv_sems):
        x, y, c, _ = _position()
        ohalf = pl.ds(pl.multiple_of((1 - c) * Rh, 32), Rh)
        cp = _remote(in_ref.at[:, ohalf, :], out_ref, send_sems, recv_sems, 0, (x, y, 1 - c))
        cp.start()
        cp.wait()

    return _pcall(body, in_specs=[HBM_SPEC], out_specs=HBM_SPEC, out_shape=SDS((4, Rh, W), g4.dtype),
                  scratch_shapes=[pltpu.SemaphoreType.DMA((1,)), pltpu.SemaphoreType.DMA((1,))], name=name)(g4)


def _scatter_chips(name, p4):
    _, Rh, W = p4.shape

    def body(in_ref, out_ref, send_sems, recv_sems, local_sem):
        x, y, c, chips = _position()
        me = 2 * x + y
        mine = pltpu.make_async_copy(in_ref.at[me], out_ref.at[me], local_sem)
        mine.start()
        sends = [_remote(in_ref.at[2 * cx + cy], out_ref.at[me], send_sems, recv_sems, j, (cx, cy, c)) for j, (cx, cy) in enumerate(chips)]
        for cp in sends:
            cp.start()
        for j, (cx, cy) in enumerate(chips):
            slot = out_ref.at[2 * cx + cy]
            _remote(slot, slot, send_sems, recv_sems, j, (cx, cy, c)).wait_recv()
        for cp in sends:
            cp.wait_send()
        mine.wait()

    return _pcall(body, in_specs=[HBM_SPEC], out_specs=HBM_SPEC, out_shape=SDS((4, Rh, W), p4.dtype),
                  scratch_shapes=[pltpu.SemaphoreType.DMA((3,)), pltpu.SemaphoreType.DMA((3,)), pltpu.SemaphoreType.DMA], name=name)(p4)


def _join_halves(name, r):
    Rh, W = r.shape

    def body(in_ref, out_ref, send_sems, recv_sems, local_sem):
        x, y, c, _ = _position()
        mine = pltpu.make_async_copy(in_ref, out_ref.at[c], local_sem)
        mine.start()
        cp = _remote(in_ref, out_ref.at[c], send_sems, recv_sems, 0, (x, y, 1 - c))
        cp.start()
        _remote(in_ref, out_ref.at[1 - c], send_sems, recv_sems, 0, (x, y, 1 - c)).wait_recv()
        cp.wait_send()
        mine.wait()

    return _pcall(body, in_specs=[HBM_SPEC], out_specs=HBM_SPEC, out_shape=SDS((2, Rh, W), r.dtype),
                  scratch_shapes=[pltpu.SemaphoreType.DMA((1,)), pltpu.SemaphoreType.DMA((1,)), pltpu.SemaphoreType.DMA], name=name)(r)


def _allgather_all(name, buf):
    r, W = buf.shape

    def body(in_ref, out_ref, send_sems, recv_sems, local_sem):
        x, y, c, _ = _position()
        me = 4 * x + 2 * y + c
        mine = pltpu.make_async_copy(in_ref, out_ref.at[me], local_sem)
        mine.start()
        peers = []
        for mk in range(1, 8):
            mx, my, mc = (mk >> 2) & 1, (mk >> 1) & 1, mk & 1
            px = 1 - x if mx else x
            py = 1 - y if my else y
            pc = 1 - c if mc else c
            peers.append((px, py, pc))
        sends = [_remote(in_ref, out_ref.at[me], send_sems, recv_sems, k, p) for k, p in enumerate(peers)]
        for cp in sends:
            cp.start()
        for k, (px, py, pc) in enumerate(peers):
            slot = out_ref.at[4 * px + 2 * py + pc]
            _remote(slot, slot, send_sems, recv_sems, k, (px, py, pc)).wait_recv()
        for cp in sends:
            cp.wait_send()
        mine.wait()

    return _pcall(body, in_specs=[HBM_SPEC], out_specs=HBM_SPEC, out_shape=SDS((8, r, W), buf.dtype),
                  scratch_shapes=[pltpu.SemaphoreType.DMA((7,)), pltpu.SemaphoreType.DMA((7,)), pltpu.SemaphoreType.DMA], name=name)(buf)


def _reduce_scatter(g4, c, tag):
    b1 = _swap_halves("rs_swap_" + tag, g4)
    p4 = _add_my_half("rs_pair_" + tag, g4, b1, c)
    b2 = _scatter_chips("rs_scatter_" + tag, p4)
    r = _sum_leading("rs_sum_" + tag, b2)
    full = _join_halves("rs_join_" + tag, r)
    return full.reshape(g4.shape[1], g4.shape[2])


def _dma_sems(n):
    return [pltpu.SemaphoreType.DMA((n,)), pltpu.SemaphoreType.DMA((n,))]


def _gather_chips(name, shards):
    n = len(shards)

    def body(*refs):
        ins, outs = refs[:n], refs[n:2 * n]
        send_sems, recv_sems = refs[2 * n:]
        x, y, c, chips = _position()
        me = 2 * x + y
        sib = (x, y, 1 - c)
        sends, halves = [], []
        for i in range(n):
            rh = ins[i].shape[0] // 2
            halves.append((pl.ds(pl.multiple_of(c * rh, 16), rh), pl.ds(pl.multiple_of((1 - c) * rh, 16), rh)))
        for i in range(n):
            for j, (cx, cy) in enumerate(chips):
                cp = _remote(ins[i].at[halves[i][0]], outs[i].at[me, halves[i][0]], send_sems, recv_sems, 6 * i + j, (cx, cy, c))
                cp.start()
                sends.append(cp)
        for j, (cx, cy) in enumerate(chips):
            for i in range(n):
                rows = outs[i].at[2 * cx + cy, halves[i][0]]
                _remote(rows, rows, send_sems, recv_sems, 6 * i + j, (cx, cy, c)).wait_recv()
                cp = _remote(rows, rows, send_sems, recv_sems, 6 * i + 3 + j, sib)
                cp.start()
                sends.append(cp)
        for j, (cx, cy) in enumerate(chips):
            for i in range(n):
                rows = outs[i].at[2 * cx + cy, halves[i][1]]
                _remote(rows, rows, send_sems, recv_sems, 6 * i + 3 + j, sib).wait_recv()
        for cp in sends:
            cp.wait_send()

    return _pcall(body, in_specs=[HBM_SPEC] * n, out_specs=[HBM_SPEC] * n, out_shape=[SDS((4,) + s.shape, s.dtype) for s in shards],
                  scratch_shapes=_dma_sems(6 * n), name=name)(*shards)


def _swap_halves_multi(name, slots):
    n = len(slots)

    def body(*refs):
        ins, outs = refs[:n], refs[n:2 * n]
        send_sems, recv_sems = refs[2 * n:]
        x, y, c, _ = _position()
        cps = []
        for i in range(n):
            rh = ins[i].shape[1] // 2
            ohalf = pl.ds(pl.multiple_of((1 - c) * rh, 8), rh)
            cp = _remote(ins[i].at[:, ohalf, :], outs[i], send_sems, recv_sems, i, (x, y, 1 - c))
            cp.start()
            cps.append(cp)
        for cp in cps:
            cp.wait()

    return _pcall(body, in_specs=[HBM_SPEC] * n, out_specs=[HBM_SPEC] * n,
                  out_shape=[SDS((4, s.shape[1] // 2, s.shape[2]), s.dtype) for s in slots], scratch_shapes=_dma_sems(n), name=name)(*slots)


def _pair_add(name, g4, b1, c):
    _, R, W = g4.shape
    tr = _pick(R // 2, (256, 128, 32, 16))
    nblk = (R // 2) // tr

    def body(c_ref, g_ref, b_ref, o_ref):
        o_ref[...] = (g_ref[...] + b_ref[...]).astype(o_ref.dtype)

    grid_spec = pltpu.PrefetchScalarGridSpec(
        num_scalar_prefetch=1, grid=(4, nblk),
        in_specs=[pl.BlockSpec((None, tr, W), lambda s, i, c_ref: (s, c_ref[0] * nblk + i, 0)),
                  pl.BlockSpec((None, tr, W), lambda s, i, c_ref: (s, i, 0))],
        out_specs=pl.BlockSpec((None, tr, W), lambda s, i, c_ref: (s, i, 0)))
    return _pcall(body, grid_spec=grid_spec, out_shape=SDS((4, R // 2, W), bf16), name=name)(c.reshape(1), g4, b1)


def _scatter_chips_multi(name, ps):
    n = len(ps)

    def body(*refs):
        ins, outs = refs[:n], refs[n:2 * n]
        send_sems, recv_sems = refs[2 * n:]
        x, y, c, chips = _position()
        me = 2 * x + y
        sends = []
        for i in range(n):
            for j, (cx, cy) in enumerate(chips):
                cp = _remote(ins[i].at[2 * cx + cy], outs[i].at[me], send_sems, recv_sems, 3 * i + j, (cx, cy, c))
                cp.start()
                sends.append(cp)
        for i in range(n):
            for j, (cx, cy) in enumerate(chips):
                slot = outs[i].at[2 * cx + cy]
                _remote(slot, slot, send_sems, recv_sems, 3 * i + j, (cx, cy, c)).wait_recv()
        for cp in sends:
            cp.wait_send()

    return _pcall(body, in_specs=[HBM_SPEC] * n, out_specs=[HBM_SPEC] * n, out_shape=[SDS(p.shape, p.dtype) for p in ps],
                  scratch_shapes=_dma_sems(3 * n), name=name)(*ps)


def _sum_chips(name, p4, b2, chip, c):
    _, Rh, W = p4.shape
    tr = _pick(Rh, (256, 128, 32, 16))
    nblk = Rh // tr

    def body(m_ref, c_ref, own_ref, r1_ref, r2_ref, r3_ref, o_ref):
        o_ref[...] = ((own_ref[...].astype(f32) + r1_ref[...].astype(f32)) + r2_ref[...].astype(f32)) + r3_ref[...].astype(f32)

    other = lambda k: pl.BlockSpec((None, tr, W), lambda i, m_ref, c_ref: (m_ref[0] ^ k, i, 0))
    grid_spec = pltpu.PrefetchScalarGridSpec(
        num_scalar_prefetch=2, grid=(nblk,),
        in_specs=[pl.BlockSpec((None, tr, W), lambda i, m_ref, c_ref: (m_ref[0], i, 0)), other(1), other(2), other(3)],
        out_specs=pl.BlockSpec((tr, W), lambda i, m_ref, c_ref: (c_ref[0] * nblk + i, 0)))
    return _pcall(body, grid_spec=grid_spec, out_shape=SDS((2 * Rh, W), f32), name=name)(chip.reshape(1), c.reshape(1), p4, b2, b2, b2)


def _join_halves(name, fulls):
    n = len(fulls)

    def body(*refs):
        outs = refs[n:2 * n]
        send_sems, recv_sems = refs[2 * n:]
        x, y, c, _ = _position()
        cps = []
        for i in range(n):
            rh = outs[i].shape[0] // 2
            mine = outs[i].at[pl.ds(pl.multiple_of(c * rh, 8), rh)]
            theirs = outs[i].at[pl.ds(pl.multiple_of((1 - c) * rh, 8), rh)]
            cp = _remote(mine, mine, send_sems, recv_sems, i, (x, y, 1 - c))
            cp.start()
            cps.append((cp, _remote(theirs, theirs, send_sems, recv_sems, i, (x, y, 1 - c))))
        for cp, back in cps:
            back.wait_recv()
            cp.wait_send()

    return _pcall(body, in_specs=[HBM_SPEC] * n, out_specs=[HBM_SPEC] * n, out_shape=[SDS(r.shape, r.dtype) for r in fulls],
                  input_output_aliases={i: i for i in range(n)}, scratch_shapes=_dma_sems(n), name=name)(*fulls)


def _hosted_gather(shards):
    n = len(shards)

    def half(ref_rows, c):
        rh = ref_rows // 2
        return pl.ds(pl.multiple_of(c * rh, 16), rh)

    def start(ins, outs, send_sems, recv_sems):
        x, y, c, chips = _position()
        me = 2 * x + y
        for i in range(n):
            rows = half(ins[i].shape[0], c)
            for j, (cx, cy) in enumerate(chips):
                _remote(ins[i].at[rows], outs[i].at[me, rows], send_sems, recv_sems, 3 * i + j, (cx, cy, c)).start()

    def finish(ins, outs, send_sems, recv_sems):
        x, y, c, chips = _position()
        me = 2 * x + y
        for i in range(n):
            rows = half(ins[i].shape[0], c)
            for j, (cx, cy) in enumerate(chips):
                _remote(ins[i].at[rows], outs[i].at[2 * cx + cy, rows], send_sems, recv_sems, 3 * i + j, (cx, cy, c)).wait_recv()
        for i in range(n):
            rows = half(ins[i].shape[0], c)
            for j, (cx, cy) in enumerate(chips):
                _remote(ins[i].at[rows], outs[i].at[me, rows], send_sems, recv_sems, 3 * i + j, (cx, cy, c)).wait_send()

    return {"arrays": shards, "out_shape": [SDS((4,) + s.shape, s.dtype) for s in shards], "n_sems": 3 * n, "start": start, "finish": finish}


def _gather_forward(name, gathered):
    n = len(gathered)

    def body(*refs):
        outs = refs[n:2 * n]
        send_sems, recv_sems = refs[2 * n:]
        x, y, c, chips = _position()
        sib = (x, y, 1 - c)
        sends = []
        for i in range(n):
            rh = outs[i].shape[1] // 2
            mine = pl.ds(pl.multiple_of(c * rh, 16), rh)
            for j, (cx, cy) in enumerate(chips):
                rows = outs[i].at[2 * cx + cy, mine]
                cp = _remote(rows, rows, send_sems, recv_sems, 3 * i + j, sib)
                cp.start()
                sends.append(cp)
        for i in range(n):
            rh = outs[i].shape[1] // 2
            theirs = pl.ds(pl.multiple_of((1 - c) * rh, 16), rh)
            for j, (cx, cy) in enumerate(chips):
                rows = outs[i].at[2 * cx + cy, theirs]
                _remote(rows, rows, send_sems, recv_sems, 3 * i + j, sib).wait_recv()
        for cp in sends:
            cp.wait_send()

    return _pcall(body, in_specs=[HBM_SPEC] * n, out_specs=[HBM_SPEC] * n, out_shape=[SDS(g.shape, g.dtype) for g in gathered],
                  input_output_aliases={i: i for i in range(n)}, scratch_shapes=_dma_sems(3 * n), name=name)(*gathered)


def _hosted_scatter(ps):
    n = len(ps)

    def start(ins, outs, send_sems, recv_sems):
        x, y, c, chips = _position()
        me = 2 * x + y
        for i in range(n):
            for j, (cx, cy) in enumerate(chips):
                _remote(ins[i].at[2 * cx + cy], outs[i].at[me], send_sems, recv_sems, 3 * i + j, (cx, cy, c)).start()

    def finish(ins, outs, send_sems, recv_sems):
        x, y, c, chips = _position()
        me = 2 * x + y
        for i in range(n):
            for j, (cx, cy) in enumerate(chips):
                slot = outs[i].at[2 * cx + cy]
                _remote(slot, slot, send_sems, recv_sems, 3 * i + j, (cx, cy, c)).wait_recv()
        for i in range(n):
            for j, (cx, cy) in enumerate(chips):
                _remote(ins[i].at[2 * cx + cy], outs[i].at[me], send_sems, recv_sems, 3 * i + j, (cx, cy, c)).wait_send()

    return {"arrays": ps, "out_shape": [SDS(p.shape, p.dtype) for p in ps], "n_sems": 3 * n, "start": start, "finish": finish}


def _hosted_swap(slots):
    n = len(slots)

    def copies(ins, outs, send_sems, recv_sems):
        x, y, c, _ = _position()
        cps = []
        for i in range(n):
            rh = ins[i].shape[1] // 2
            ohalf = pl.ds(pl.multiple_of((1 - c) * rh, 8), rh)
            cps.append(_remote(ins[i].at[:, ohalf, :], outs[i], send_sems, recv_sems, i, (x, y, 1 - c)))
        return cps

    def start(ins, outs, send_sems, recv_sems):
        for cp in copies(ins, outs, send_sems, recv_sems):
            cp.start()

    def finish(ins, outs, send_sems, recv_sems):
        for cp in copies(ins, outs, send_sems, recv_sems):
            cp.wait()

    return {"arrays": slots, "out_shape": [SDS((4, s.shape[1] // 2, s.shape[2]), s.dtype) for s in slots], "n_sems": n,
            "start": start, "finish": finish}


def _rs_begin(slots, c):
    b1 = _swap_halves_multi("rs_swap", slots)
    return [_pair_add("rs_pair_%d" % i, g, b, c) for i, (g, b) in enumerate(zip(slots, b1))]


def _rs_end(ps, b2, c, chip):
    return _join_halves("rs_join", [_sum_chips("rs_sum_%d" % i, p, b, chip, c) for i, (p, b) in enumerate(zip(ps, b2))])


def _reduce_scatter_multi(slots, c, chip):
    ps = _rs_begin(slots, c)
    return _rs_end(ps, _scatter_chips_multi("rs_scatter", ps), c, chip)


_BIG = ("w_in", "w_dn_out", "w_swa_out", "w_o", "w_up", "w_down")


def _assemble_weights(gs):
    cat1 = lambda g: jnp.concatenate([g[s] for s in range(4)], axis=1)
    rows = lambda g: g.reshape(4 * g.shape[1], g.shape[2])
    return {"w_in": _regroup_w_in(cat1(gs[0])), "w_dn_out": rows(gs[1]), "w_swa_out": rows(gs[2]), "w_o": rows(gs[3]),
            "w_up": cat1(gs[4]), "w_down": rows(gs[5])}


def _grad_slots(gw):
    cols = lambda g: g.reshape(g.shape[0], 4, g.shape[1] // 4).transpose(1, 0, 2)
    rows = lambda g: g.reshape(4, g.shape[0] // 4, g.shape[1])
    return [cols(_ungroup_w_in(gw["w_in"])), rows(gw["w_dn_out"]), rows(gw["w_swa_out"]), rows(gw["w_o"]), gw["w_up"], rows(gw["w_down"])]


_SHARDED = (
    ("w_in", (D, 1860), 1, False),
    ("dn_conv", (4, 768), 1, True),
    ("w_dn_out", (256, D), 0, False),
    ("w_swa_out", (256, D), 0, False),
    ("w_o", (256, D), 0, False),
    ("w_up", (D, 1408), 1, False),
    ("ffn_conv", (3, 704), 1, True),
    ("w_down", (704, D), 0, False),
)


def _pack_weights(shards):
    parts = []
    for nm, shp, _, as_bits in _SHARDED:
        a = shards[nm]
        parts.append(lax.bitcast_convert_type(a, bf16).reshape(-1) if as_bits else a.astype(bf16).reshape(-1))
    flat = jnp.concatenate(parts)
    return jnp.pad(flat, (0, COMM_ROWS * COMM_W - flat.shape[0])).reshape(COMM_ROWS, COMM_W)


def _unpack_weights(g):
    flat = g.reshape(4, -1)
    out, off = {}, 0
    for nm, shp, ax, as_bits in _SHARDED:
        n = int(np.prod(shp)) * (2 if as_bits else 1)
        piece = flat[:, off:off + n]
        off += n
        if as_bits:
            piece = lax.bitcast_convert_type(piece.reshape((4,) + shp + (2,)), f32)
        else:
            piece = piece.reshape((4,) + shp)
        out[nm] = jnp.concatenate([piece[s] for s in range(4)], axis=ax)
    return out


def _pack_grads(grads):
    slots = []
    for s in range(4):
        parts = []
        for nm, shp, ax, _ in _SHARDED:
            n = shp[ax]
            parts.append(lax.slice_in_dim(grads[nm], s * n, (s + 1) * n, axis=ax).reshape(-1))
        flat = jnp.concatenate(parts)
        slots.append(jnp.pad(flat, (0, COMM_ROWS * COMM_W - flat.shape[0])))
    return jnp.stack(slots).reshape(4, COMM_ROWS, COMM_W)


def _unpack_grads(r):
    flat = r.reshape(-1)
    out, off = {}, 0
    for nm, shp, _, _ in _SHARDED:
        n = int(np.prod(shp))
        out[nm] = flat[off:off + n].reshape(shp)
        off += n
    return out


def _regroup_w_in(w):
    z = lambda n: jnp.zeros((w.shape[0], n), w.dtype)
    return jnp.concatenate([w[:, 0:3072], w[:, 3072:4096], w[:, 4112:5136], w[:, 5392:6416], w[:, 6416:7440],
                            w[:, 4096:4112], z(112), w[:, 5136:5264], w[:, 5264:5392], z(128)], axis=1)


def _ungroup_w_in(g):
    return jnp.concatenate([g[:, 0:3072], g[:, 3072:4096], g[:, 7168:7184], g[:, 4096:5120], g[:, 7296:7424], g[:, 7424:7552],
                            g[:, 5120:6144], g[:, 6144:7168]], axis=1)


def _pad_lanes(v, n=LANE):
    return jnp.pad(v, (0, n - v.shape[0])).reshape(1, n)


def _layer_consts(P):
    K = {}
    K["norm_mix"] = P["norm_mix"].reshape(1, D)
    K["norm_ffn"] = P["norm_ffn"].reshape(1, D)
    K["alog"] = _pad_lanes(P["dn_a_log"])
    K["dtb"] = _pad_lanes(P["dn_dt_bias"])
    K["dn_norm"] = P["dn_norm"].reshape(1, LANE)
    K["qn"] = jnp.tile(P["swa_q_norm"], 16).reshape(1, D)
    K["kn"] = jnp.tile(P["swa_k_norm"], 2).reshape(1, LANE)
    K["sinks"] = _pad_lanes(P["swa_sinks"])
    K["ffn_b"] = P["ffn_conv_b"].reshape(1, D_FF)
    return K


def _layer_fwd(x, mod, W, K, tabs, bd, hosted=None):
    sh1, sc1, gt1, sh2, sc2, gt2 = mod
    S = {"x": x}
    (h1,) = _rowwise_fwd("normmod1_fwd", _normmod_fn, [(x, 0, D)], [K["norm_mix"], sc1, sh1], [D], [bf16])
    proj = _matmul("proj_fwd", h1, W["w_in"], "nn", f32)
    qn = _dnconv_fwd("dnconv_q_fwd", proj, CB_Q, W["dn_conv"], True)
    kn = _dnconv_fwd("dnconv_k_fwd", proj, CB_K, W["dn_conv"], True)
    vc = _dnconv_fwd("dnconv_v_fwd", proj, CB_V, W["dn_conv"], False)
    o, sall, hosted_out = _gdn_fwd("gdn_fwd", qn, kn, vc, proj, K["alog"], K["dtb"], hosted=hosted)
    (on,) = _rowwise_fwd("dngate_fwd", _dngate_fn, [(o, 0, LANE), (proj, 8 * WB_Z, LANE)], [K["dn_norm"]], [LANE], [bf16], nc=8,
                         tr=_pick(x.shape[0], (1024,)))
    ya = _matmul("dnout_fwd", on, W["w_dn_out"], "nn", f32)
    sq = _qkprep_fwd("qprep_fwd", proj, WB_SWQ, D, K["qn"], bd[0], tabs[0])
    sk = _qkprep_fwd("kprep_fwd", proj, CB_SWK, LANE, K["kn"], bd[1], tabs[1])
    attn = _attn_fwd("attn_fwd", sq, sk, proj, K["sinks"])
    yb = _matmul("swaout_fwd", attn, W["w_swa_out"], "nn", f32)
    (merged,) = _rowwise_fwd("merge_fwd", _merge_fn, [(proj, WB_GA, D), (proj, WB_GB, D), (ya, 0, D), (yb, 0, D)], [], [D], [bf16])
    t1 = _matmul("wo_fwd", merged, W["w_o"], "nn", f32)
    (x1,) = _rowwise_fwd("resid1_fwd", _resid_fn, [(x, 0, D), (t1, 0, D)], [gt1], [D], [f32])
    (h2,) = _rowwise_fwd("normmod2_fwd", _normmod_fn, [(x1, 0, D)], [K["norm_ffn"], sc2, sh2], [D], [bf16])
    up = _matmul("up_fwd", h2, W["w_up"], "nn", f32)
    mid = _ffnact_fwd("ffnact_fwd", up, W["ffn_conv"], K["ffn_b"])
    t2 = _matmul("down_fwd", mid, W["w_down"], "nn", f32)
    (x2,) = _rowwise_fwd("resid2_fwd", _resid_fn, [(x1, 0, D), (t2, 0, D)], [gt2], [D], [f32])
    S.update(h1=h1, proj=proj, qn=qn, kn=kn, vc=vc, o=o, sall=sall, on=on, ya=ya, sq=sq, sk=sk, attn=attn, yb=yb,
             merged=merged, t1=t1, x1=x1, h2=h2, up=up, mid=mid, t2=t2)
    return x2, S, hosted_out


def _layer_bwd(dx2, S, mod, W, K, tabs, bd, carry=None):
    sh1, sc1, gt1, sh2, sc2, gt2 = mod
    x, x1, proj, up = S["x"], S["x1"], S["proj"], S["up"]
    T = x.shape[0]
    gw, gs = {}, {}
    dt2, dgt2 = _rowwise_bwd("resid2_bwd", _resid_fn, [(x1, 0, D), (S["t2"], 0, D)], [gt2], [(dx2, 0, D)], [None, bf16])
    dmid = _matmul("down_bwd_x", dt2, W["w_down"], "nt", bf16)
    gw["w_down"] = _matmul("down_bwd_w", S["mid"], dt2, "tn", f32)
    dact, dlin, gw["ffn_conv"], dffn_b = _ffnact_bwd("ffnact_bwd", up, W["ffn_conv"], K["ffn_b"], dmid)
    dup = jnp.concatenate([dact, dlin], axis=1)
    dh2 = _matmul("up_bwd_x", dup, W["w_up"], "nt", f32)
    if carry is None:
        gw["w_up"] = _matmul("up_bwd_w", S["h2"], dup, "tn", f32, out_slots=4)
        pair_sums = hosted = None
    else:
        gw["w_up"], b1 = _matmul("up_bwd_w", S["h2"], dup, "tn", f32, out_slots=4, hosted=_hosted_swap(carry[0]))
        pair_sums = [_pair_add("rs_pair_%d" % i, g, b, carry[1]) for i, (g, b) in enumerate(zip(carry[0], b1))]
        hosted = _hosted_scatter(pair_sums)
    dx1, dnorm_ffn, dsc2, dsh2 = _rowwise_bwd("normmod2_bwd", _normmod_fn, [(x1, 0, D)], [K["norm_ffn"], sc2, sh2], [(dh2, 0, D)], [f32],
                                              add_to_first=(dx2, 0, D))
    dt1, dgt1 = _rowwise_bwd("resid1_bwd", _resid_fn, [(x, 0, D), (S["t1"], 0, D)], [gt1], [(dx1, 0, D)], [None, bf16])
    dmerged = _matmul("wo_bwd_x", dt1, W["w_o"], "nt", f32)
    gw["w_o"] = _matmul("wo_bwd_w", S["merged"], dt1, "tn", f32)
    dga, dgb, dya, dyb = _rowwise_bwd("merge_bwd", _merge_fn, [(proj, WB_GA, D), (proj, WB_GB, D), (S["ya"], 0, D), (S["yb"], 0, D)], [],
                                      [(dmerged, 0, D)], [bf16, bf16, bf16, bf16])
    don = _matmul("dnout_bwd_x", dya, W["w_dn_out"], "nt", f32)
    gw["w_dn_out"] = _matmul("dnout_bwd_w", S["on"], dya, "tn", f32)
    do, dz, ddn_norm = _rowwise_bwd("dngate_bwd", _dngate_fn, [(S["o"], 0, LANE), (proj, 8 * WB_Z, LANE)], [K["dn_norm"]], [(don, 0, LANE)],
                                    [f32, bf16], nc=8, tr=_pick(T, (1024,)))
    (dqn, dkn, dvc, dab, dalog, ddtb), hosted_out = _gdn_bwd("gdn_bwd", S["qn"], S["kn"], S["vc"], proj, K["alog"], K["dtb"], S["sall"], do,
                                                            hosted=hosted)
    dpq, dwq = _dnconv_bwd("dnconv_q_bwd", proj, CB_Q, W["dn_conv"], dqn, True)
    dpk, dwk = _dnconv_bwd("dnconv_k_bwd", proj, CB_K, W["dn_conv"], dkn, True)
    dpv, dwv = _dnconv_bwd("dnconv_v_bwd", proj, CB_V, W["dn_conv"], dvc, False)
    gw["dn_conv"] = jnp.concatenate([dwq, dwk, dwv], axis=1)
    dattn = _matmul("swaout_bwd_x", dyb, W["w_swa_out"], "nt", f32)
    gw["w_swa_out"] = _matmul("swaout_bwd_w", S["attn"], dyb, "tn", f32)
    dsq, dkp, dkc, dvp, dvc_, dsinks = _attn_bwd("attn_bwd", S["sq"], S["sk"], proj, K["sinks"], dattn)
    dsk = _shift_add("attn_dk_join", dkc, dkp, f32)
    dswv = _shift_add("attn_dv_join", dvc_, dvp, bf16)
    dswq, dqn_w = _qkprep_bwd("qprep_bwd", proj, WB_SWQ, D, K["qn"], bd[0], tabs[0], dsq)
    dswk, dkn_w = _qkprep_bwd("kprep_bwd", proj, CB_SWK, LANE, K["kn"], bd[1], tabs[1], dsk)
    dproj = jnp.concatenate([dpq, dpk, dpv, dz, dswq, dga, dgb, dab.astype(bf16), dswk, dswv, jnp.zeros((T, LANE), bf16)], axis=1)
    dh1 = _matmul("proj_bwd_x", dproj, W["w_in"], "nt", f32)
    gw["w_in"] = _matmul("proj_bwd_w", S["h1"], dproj, "tn", f32)
    dx, dnorm_mix, dsc1, dsh1 = _rowwise_bwd("normmod1_bwd", _normmod_fn, [(x, 0, D)], [K["norm_mix"], sc1, sh1], [(dh1, 0, D)], [f32],
                                             add_to_first=(dx1, 0, D))
    gs = {"norm_mix": dnorm_mix[0], "dn_a_log": dalog[0, :8], "dn_dt_bias": ddtb[0, :8], "dn_norm": ddn_norm[0],
          "swa_q_norm": dqn_w.reshape(16, 64).sum(0), "swa_k_norm": dkn_w.reshape(2, 64).sum(0), "swa_sinks": dsinks[0, :16],
          "norm_ffn": dnorm_ffn[0], "ffn_conv_b": dffn_b[0]}
    dmod = jnp.concatenate([dsh1, dsc1, dgt1, dsh2, dsc2, dgt2], axis=1)
    return dx, gw, gs, dmod, (pair_sums, hosted_out)


def _rope_tables(pos):
    T = pos.shape[0]
    half = 8
    inv = jnp.power(ROPE_THETA, -jnp.arange(half, dtype=f32) / half)
    ang = pos.astype(f32)[:, None] * inv
    cos, sin = jnp.cos(ang), jnp.sin(ang)
    z8, z48, o48 = jnp.zeros((T, 8), f32), jnp.zeros((T, 48), f32), jnp.ones((T, 48), f32)
    c64 = jnp.concatenate([cos, cos, o48], axis=1)
    s1 = jnp.concatenate([-sin, z8, z48], axis=1)
    s2 = jnp.concatenate([z8, sin, z48], axis=1)
    return tuple(jnp.tile(t, (1, 16)) for t in (c64, s1, s2))


_SMALL = (("norm_mix", D), ("dn_a_log", 8), ("dn_dt_bias", 8), ("dn_norm", 128), ("swa_q_norm", 64), ("swa_k_norm", 64),
          ("swa_sinks", 16), ("norm_ffn", D), ("ffn_conv_b", D_FF), ("b_ada", 6 * D))
_CONV = (("dn_conv", 4 * 3072), ("ffn_conv", 3 * D_FF))
_CONV_SHARD = (("dn_conv", 4 * 768), ("ffn_conv", 3 * 704))


def _pack_small(vals, spec):
    flat = jnp.concatenate([vals[nm].reshape(-1) for nm, _ in spec])
    rows = -(-flat.shape[0] // (8 * LANE)) * 8
    return jnp.pad(flat, (0, rows * LANE - flat.shape[0])).reshape(rows, LANE)


def _unpack_small(buf, spec):
    flat = buf.reshape(-1)
    out, off = {}, 0
    for nm, n in spec:
        out[nm] = flat[off:off + DEPTH * n].reshape(DEPTH, n)
        off += DEPTH * n
    return out


def kernel(x, c, positions, w_ada, b_ada, norm_mix, w_in, dn_conv, dn_a_log, dn_dt_bias, dn_norm, w_dn_out, swa_q_norm, swa_k_norm, swa_sinks, w_swa_out, w_o, norm_ffn, w_up, ffn_conv, ffn_conv_b, w_down, loss_target, m_w_ada, m_b_ada, m_norm_mix, m_w_in, m_dn_conv, m_dn_a_log, m_dn_dt_bias, m_dn_norm, m_w_dn_out, m_swa_q_norm, m_swa_k_norm, m_swa_sinks, m_w_swa_out, m_w_o, m_norm_ffn, m_w_up, m_ffn_conv, m_ffn_conv_b, m_w_down, v_w_ada, v_b_ada, v_norm_mix, v_w_in, v_dn_conv, v_dn_a_log, v_dn_dt_bias, v_dn_norm, v_w_dn_out, v_swa_q_norm, v_swa_k_norm, v_swa_sinks, v_w_swa_out, v_w_o, v_norm_ffn, v_w_up, v_ffn_conv, v_ffn_conv_b, v_w_down):
    weights = dict(w_ada=w_ada, b_ada=b_ada, norm_mix=norm_mix, w_in=w_in, dn_conv=dn_conv, dn_a_log=dn_a_log, dn_dt_bias=dn_dt_bias,
                   dn_norm=dn_norm, w_dn_out=w_dn_out, swa_q_norm=swa_q_norm, swa_k_norm=swa_k_norm, swa_sinks=swa_sinks,
                   w_swa_out=w_swa_out, w_o=w_o, norm_ffn=norm_ffn, w_up=w_up, ffn_conv=ffn_conv, ffn_conv_b=ffn_conv_b, w_down=w_down)
    mom_m = dict(w_ada=m_w_ada, b_ada=m_b_ada, norm_mix=m_norm_mix, w_in=m_w_in, dn_conv=m_dn_conv, dn_a_log=m_dn_a_log,
                 dn_dt_bias=m_dn_dt_bias, dn_norm=m_dn_norm, w_dn_out=m_w_dn_out, swa_q_norm=m_swa_q_norm, swa_k_norm=m_swa_k_norm,
                 swa_sinks=m_swa_sinks, w_swa_out=m_w_swa_out, w_o=m_w_o, norm_ffn=m_norm_ffn, w_up=m_w_up, ffn_conv=m_ffn_conv,
                 ffn_conv_b=m_ffn_conv_b, w_down=m_w_down)
    mom_v = dict(w_ada=v_w_ada, b_ada=v_b_ada, norm_mix=v_norm_mix, w_in=v_w_in, dn_conv=v_dn_conv, dn_a_log=v_dn_a_log,
                 dn_dt_bias=v_dn_dt_bias, dn_norm=v_dn_norm, w_dn_out=v_w_dn_out, swa_q_norm=v_swa_q_norm, swa_k_norm=v_swa_k_norm,
                 swa_sinks=v_swa_sinks, w_swa_out=v_w_swa_out, w_o=v_w_o, norm_ffn=v_norm_ffn, w_up=v_w_up, ffn_conv=v_ffn_conv,
                 ffn_conv_b=v_ffn_conv_b, w_down=v_w_down)
    order = ["w_ada", "b_ada", "norm_mix", "w_in", "dn_conv", "dn_a_log", "dn_dt_bias", "dn_norm", "w_dn_out", "swa_q_norm",
             "swa_k_norm", "swa_sinks", "w_swa_out", "w_o", "norm_ffn", "w_up", "ffn_conv", "ffn_conv_b", "w_down"]
    ax, ay, ac = lax.axis_index("x"), lax.axis_index("y"), lax.axis_index("c")
    chip = 2 * ax + ay
    dev = 4 * ax + 2 * ay + ac
    T = x.shape[1]
    xs = x[0]

    c_all = _allgather_all("gather_c", jnp.pad(c, ((0, 7), (0, 0)))).reshape(8, 8, D)[:, 0]
    c_act = _silu_rows("silu_c", jnp.pad(c_all, ((0, 8), (0, 0))))
    mod_sh = jnp.stack([
        _matmul("mod_fwd", c_act, w_ada[l].astype(bf16), "nn", f32,
                bias=lax.dynamic_slice(b_ada[l], (chip * 1536,), (1536,)).reshape(1, 1536)) for l in range(DEPTH)])
    mod_all = _allgather_all("gather_mod", mod_sh.reshape(DEPTH * 16 * 12, LANE)).reshape(8, DEPTH, 16, 1536)
    mod_me = jnp.concatenate([lax.dynamic_index_in_dim(mod_all[2 * s], dev, axis=1, keepdims=False) for s in range(4)], axis=1)

    tabs_q = _rope_tables(positions[0])
    tabs = (tabs_q, tuple(t[:, :LANE] for t in tabs_q))
    head = jnp.arange(LANE) // 64
    bd128 = (head[:, None] == head[None, :]).astype(f32) / 64.0
    bd = (bd128, bd128)

    conv_all = _allgather_all("gather_conv", _pack_small({"dn_conv": dn_conv, "ffn_conv": ffn_conv}, _CONV_SHARD))
    conv_parts = [_unpack_small(conv_all[2 * s], _CONV_SHARD) for s in range(4)]
    dn_conv_full = jnp.concatenate([p["dn_conv"].reshape(DEPTH, 4, 768) for p in conv_parts], axis=2)
    ffn_conv_full = jnp.concatenate([p["ffn_conv"].reshape(DEPTH, 3, 704) for p in conv_parts], axis=2)

    saved, Ws, Ks, mods = [], [], [], []
    h = xs
    shards = [[weights[nm][l].astype(bf16) for nm in _BIG] for l in range(DEPTH)]
    gathered = _gather_chips("gather_w", shards[0])
    for l in range(DEPTH):
        gathered = [lax.dynamic_update_index_in_dim(g, s, chip, 0) for g, s in zip(gathered, shards[l])]
        W = _assemble_weights(gathered)
        W["dn_conv"], W["ffn_conv"] = dn_conv_full[l], ffn_conv_full[l]
        K = _layer_consts({nm: weights[nm][l] for nm in ("norm_mix", "norm_ffn", "dn_a_log", "dn_dt_bias", "dn_norm", "swa_q_norm",
                                                          "swa_k_norm", "swa_sinks", "ffn_conv_b")})
        mod = tuple(mod_me[l, k * D:(k + 1) * D].reshape(1, D) for k in range(6))
        nxt = _hosted_gather(shards[l + 1]) if l + 1 < DEPTH else None
        h, S, arrived = _layer_fwd(h, mod, W, K, tabs, bd, hosted=nxt)
        if nxt is not None:
            gathered = _gather_forward("gather_w_pass", arrived)
        saved.append(S), Ws.append(W), Ks.append(K), mods.append(mod)

    loss_blk, dh = _loss("loss", h, loss_target[0])
    loss = lax.psum(loss_blk[0, 0], ("x", "y", "c"))

    grad_sh = [None] * DEPTH
    small = [None] * DEPTH
    dmods = [None] * DEPTH
    slots = None
    for l in reversed(range(DEPTH)):
        dh, gw, gs, dmod, (ps, b2) = _layer_bwd(dh, saved[l], mods[l], Ws[l], Ks[l], tabs, bd, carry=None if slots is None else (slots, ac))
        if slots is not None:
            grad_sh[l + 1] = dict(zip(_BIG, _rs_end(ps, b2, ac, chip)))
        slots = _grad_slots(gw)
        small[l], dmods[l] = dict(gs, dn_conv=gw["dn_conv"], ffn_conv=gw["ffn_conv"]), dmod[0]
    grad_sh[0] = dict(zip(_BIG, _reduce_scatter_multi(slots, ac, chip)))

    spec_g = _SMALL + _CONV
    vals = {nm: jnp.stack([small[l][nm] for l in range(DEPTH)]) for nm, _ in spec_g if nm != "b_ada"}
    vals["b_ada"] = jnp.stack(dmods)
    small_all = _allgather_all("gather_small", _pack_small(vals, spec_g))
    g_small = _unpack_small(_sum_leading("sum_small", small_all), spec_g)
    dmod_all = jnp.stack([_unpack_small(small_all[d], spec_g)["b_ada"] for d in range(8)])
    dmod_sh = lax.dynamic_slice(dmod_all, (0, 0, chip * 1536), (8, DEPTH, 1536))
    dmod_sh = jnp.pad(dmod_sh, ((0, 8), (0, 0), (0, 0))).astype(bf16)
    g_w_ada = jnp.stack([_matmul("mod_bwd_w", c_act, dmod_sh[:, l], "tn", f32) for l in range(DEPTH)])

    grads = {nm: g_small[nm] for nm, _ in _SMALL}
    grads["dn_conv"] = lax.dynamic_slice(g_small["dn_conv"].reshape(DEPTH, 4, 3072), (0, 0, chip * 768), (DEPTH, 4, 768))
    grads["ffn_conv"] = lax.dynamic_slice(g_small["ffn_conv"].reshape(DEPTH, 3, D_FF), (0, 0, chip * 704), (DEPTH, 3, 704))
    grads["w_ada"] = g_w_ada
    for nm in _BIG:
        grads[nm] = jnp.stack([grad_sh[l][nm] for l in range(DEPTH)])

    delta, new_m, new_v = {}, {}, {}
    for nm in ("w_ada", "dn_conv", "ffn_conv") + _BIG:
        delta[nm], new_m[nm], new_v[nm] = _adamw("adamw_" + nm, weights[nm], grads[nm], mom_m[nm], mom_v[nm])
    sm = [_pack_small({nm: t[nm] for nm, _ in _SMALL}, _SMALL) for t in (weights, grads, mom_m, mom_v)]
    for tgt, buf in zip((delta, new_m, new_v), _adamw("adamw_small", *sm)):
        tgt.update(_unpack_small(buf, _SMALL))

    return (loss, dh[None], *[grads[n] for n in order], *[delta[n] for n in order], *[new_m[n] for n in order], *[new_v[n] for n in order])
```

```python
import functools

import jax
import jax.numpy as jnp
import numpy as np
from jax import lax
from jax.experimental import pallas as pl
from jax.experimental.pallas import tpu as pltpu

f32 = jnp.float32
bf16 = jnp.bfloat16
SDS = jax.ShapeDtypeStruct
HI = lax.Precision.HIGHEST
MESH = pl.DeviceIdType.MESH

D = 1024
DEPTH = 4
EPS = 1e-6
DN_C = 64
SWA_B = 128
LANE = 128
ROPE_THETA = 500000.0
D_FF = 2816
IN_TOTAL = 7440
PROJ_W = 7680
CB_Q, CB_K, CB_V = 0, 8, 16
CB_AB, CB_SWK, CB_SWV = 56, 57, 58
WB_Z, WB_SWQ, WB_GA, WB_GB = 3, 4, 5, 6
TR = 256
COMM_W = 1024
COMM_ROWS = 4864
COMM_TR = 128
VMEM_BIG = 48 * 2 ** 20

ADAM_LR, ADAM_B1, ADAM_B2, ADAM_EPS, ADAM_WD, ADAM_STEP = 0.001, 0.9, 0.999, 1e-08, 0.01, 10


def _pcall(body, **kw):
    return pl.pallas_call(body, **kw)


def _cparams(vmem=None):
    return pltpu.CompilerParams(vmem_limit_bytes=vmem) if vmem else None


def _dot(a, b, ca, cb, precision=HI):
    return lax.dot_general(a, b, (((ca,), (cb,)), ((), ())), precision=precision, preferred_element_type=f32)


def _pick(n, cands):
    for c in cands:
        if n % c == 0:
            return c
    return n


def _tile(n, cap):
    if n <= cap:
        return n
    best = None
    for t in range(LANE, cap + 1, LANE):
        if n % t == 0:
            best = t
    assert best is not None, (n, cap)
    return best


def _matmul(name, a, b, mode, out_dtype, bias=None, out_slots=None, hosted=None):
    h_args, h_specs, h_shapes, h_sems = _hosted_parts(hosted)
    nh = len(h_args)
    if mode == "nn":
        (M, K), (_, N) = a.shape, b.shape
    elif mode == "nt":
        (M, K), (N, _) = a.shape, b.shape
    else:
        (K, M), (_, N) = a.shape, b.shape
    tm = _tile(M, 1536 if mode == "tn" else 1024)
    tn = N // out_slots if out_slots else _tile(N, 1536)
    tk = _tile(K, 512 if mode == "tn" else (1024 if K <= 1024 else 1536))
    nk = K // tk
    ca, cb = {"nn": (1, 0), "nt": (1, 1), "tn": (0, 0)}[mode]

    grid = (M // tm, N // tn, nk)

    def body(*refs):
        a_ref, b_ref = refs[:2]
        nb = 1 if bias is not None else 0
        bias_ref = refs[2] if nb else None
        h_ins, o_ref, h_outs = refs[2 + nb:2 + nb + nh], refs[2 + nb + nh], refs[3 + nb + nh:3 + nb + 2 * nh]
        rest = refs[3 + nb + 2 * nh:]
        if hosted is not None:
            step = (pl.program_id(0) * grid[1] + pl.program_id(1)) * grid[2] + pl.program_id(2)
            sems = rest[1:] if nk > 1 else rest

            @pl.when(step == 0)
            def _():
                hosted["start"](h_ins, h_outs, *sems)

        def finish(r):
            if bias is not None:
                r = r + bias_ref[...]
            o_ref[...] = r.astype(o_ref.dtype)

        part = _dot(a_ref[...].astype(bf16), b_ref[...].astype(bf16), ca, cb, precision=None)
        if nk == 1:
            finish(part)
        else:
            acc = rest[0]
            k = pl.program_id(2)

            @pl.when(k == 0)
            def _():
                acc[...] = part

            @pl.when(k > 0)
            def _():
                acc[...] += part

            @pl.when(k == nk - 1)
            def _():
                finish(acc[...])

        if hosted is not None:
            @pl.when(step == grid[0] * grid[1] * grid[2] - 1)
            def _():
                hosted["finish"](h_ins, h_outs, *sems)

    a_spec =pl.BlockSpec((tk, tm), lambda i, j, k: (k, i)) if mode == "tn" else pl.BlockSpec((tm, tk), lambda i, j, k: (i, k))
    b_spec = pl.BlockSpec((tn, tk), lambda i, j, k: (j, k)) if mode == "nt" else pl.BlockSpec((tk, tn), lambda i, j, k: (k, j))
    in_specs = [a_spec, b_spec]
    args = [a, b]
    if bias is not None:
        in_specs.append(pl.BlockSpec((1, tn), lambda i, j, k: (0, j)))
        args.append(bias)
    if out_slots:
        out_spec = pl.BlockSpec((None, tm, tn), lambda i, j, k: (j, i, 0))
        out_shape = SDS((out_slots, M, tn), out_dtype)
    else:
        out_spec = pl.BlockSpec((tm, tn), lambda i, j, k: (i, j))
        out_shape = SDS((M, N), out_dtype)
    if hosted is None:
        return _pcall(
            body, grid=grid, in_specs=in_specs, out_specs=out_spec, out_shape=out_shape,
            scratch_shapes=[pltpu.VMEM((tm, tn), f32)] if nk > 1 else [], compiler_params=_cparams(VMEM_BIG), name=name)(*args)
    outs = _pcall(
        body, grid=grid, in_specs=in_specs + h_specs, out_specs=[out_spec] + h_specs, out_shape=[out_shape] + h_shapes,
        scratch_shapes=([pltpu.VMEM((tm, tn), f32)] if nk > 1 else []) + h_sems, compiler_params=_cparams(VMEM_BIG), name=name)(*args, *h_args)
    return outs[0], list(outs[1:])


def _row_specs(rows, tr):
    return [pl.BlockSpec((tr, w), lambda i, j, off=off: (i, off + j)) for (_, off, w) in rows]


def _rowwise_fwd(name, fn, rows, vecs, out_widths, out_dtypes, nc=1, tr=TR):
    T = rows[0][0].shape[0]
    n_in = len(rows) + len(vecs)

    def body(*refs):
        vals = [r[...].astype(f32) for r in refs[:n_in]]
        res = fn(*vals)
        for o_ref, r in zip(refs[n_in:], res):
            o_ref[...] = r.astype(o_ref.dtype)

    in_specs = _row_specs(rows, tr) + [pl.BlockSpec(v.shape, lambda i, j: (0, 0)) for v in vecs]
    out_specs = [pl.BlockSpec((tr, w), lambda i, j: (i, j)) for w in out_widths]
    out_shape = [SDS((T, w * nc), dt) for w, dt in zip(out_widths, out_dtypes)]
    return _pcall(body, grid=(T // tr, nc), in_specs=in_specs, out_specs=out_specs, out_shape=out_shape, name=name)(
        *[r[0] for r in rows], *vecs)


def _rowwise_bwd(name, fn, rows, vecs, cts, drow_dtypes, nc=1, tr=TR, add_to_first=None):
    T = rows[0][0].shape[0]
    n_r, n_v, n_c = len(rows), len(vecs), len(cts)
    n_add = 0 if add_to_first is None else 1
    keep = [k for k, dt in enumerate(drow_dtypes) if dt is not None]

    def body(*refs):
        n_in = n_r + n_v + n_c + n_add
        vals = [r[...].astype(f32) for r in refs[:n_in]]
        outs = refs[n_in:]
        i, j = pl.program_id(0), pl.program_id(1)
        _, vjp = jax.vjp(fn, *vals[:n_r + n_v])
        grads = vjp(tuple(vals[n_r + n_v:n_r + n_v + n_c]))
        for pos, k in enumerate(keep):
            g = grads[k]
            if n_add and pos == 0:
                g = g + vals[-1]
            outs[pos][...] = g.astype(outs[pos].dtype)

        @pl.when((i == 0) & (j == 0))
        def _():
            for q in range(n_v):
                outs[len(keep) + q][...] = jnp.zeros_like(outs[len(keep) + q])

        for q in range(n_v):
            outs[len(keep) + q][...] += grads[n_r + q]

    extra = [] if add_to_first is None else [add_to_first]
    in_specs = (_row_specs(rows, tr) + [pl.BlockSpec(v.shape, lambda i, j: (0, 0)) for v in vecs]
                + _row_specs(cts, tr) + _row_specs(extra, tr))
    out_specs = [pl.BlockSpec((tr, rows[k][2]), lambda i, j: (i, j)) for k in keep]
    out_specs += [pl.BlockSpec(v.shape, lambda i, j: (0, 0)) for v in vecs]
    out_shape = [SDS((T, rows[k][2] * nc), drow_dtypes[k]) for k in keep] + [SDS(v.shape, f32) for v in vecs]
    return _pcall(body, grid=(T // tr, nc), in_specs=in_specs, out_specs=out_specs, out_shape=out_shape, name=name)(
        *[r[0] for r in rows], *vecs, *[c[0] for c in cts], *[e[0] for e in extra])


def _normmod_fn(x, w, sc, sh):
    y = x * lax.rsqrt(jnp.mean(x * x, axis=-1, keepdims=True) + EPS)
    return ((y * w) * (1.0 + sc) + sh,)


def _resid_fn(x, t, gt):
    return (x + gt * t,)


def _merge_fn(ga, gb, ya, yb):
    return (jax.nn.sigmoid(ga) * ya + jax.nn.sigmoid(gb) * yb,)


def _dngate_fn(o, z, w):
    y = o * lax.rsqrt(jnp.mean(o * o, axis=-1, keepdims=True) + EPS)
    return ((y * w) * (z * jax.nn.sigmoid(z)),)


def _conv_taps(x, w_ref, taps, buf):
    w = lambda s: w_ref[taps - 1 - s:taps - s, :]
    row = lax.broadcasted_iota(jnp.int32, (8, x.shape[1]), 0)
    x8 = x[0:8]
    acc, acc8 = x * w(0), x8 * w(0)
    for s in range(1, taps):
        acc = acc + pltpu.roll(x, s, 0) * w(s)
        acc8 = acc8 + jnp.where(row >= s, pltpu.roll(x8, s, 0), 0.0) * w(s)
    buf[...] = acc
    buf[0:8, :] = acc8
    return buf[...]


def _conv_taps_bwd(x, dy, w_ref, dw_ref, taps, buf):
    T = x.shape[0]
    w = lambda s: w_ref[taps - 1 - s:taps - s, :]
    row = lax.broadcasted_iota(jnp.int32, (8, x.shape[1]), 0)
    dy_first, dy_last = dy[0:8], dy[T - 8:T]
    dx, dx_last = dy * w(0), dy_last * w(0)
    dw_ref[taps - 1:taps, :] = jnp.sum(dy * x, axis=0, keepdims=True)
    for s in range(1, taps):
        dx = dx + pltpu.roll(dy, T - s, 0) * w(s)
        dx_last = dx_last + jnp.where(row < 8 - s, pltpu.roll(dy_last, 8 - s, 0), 0.0) * w(s)
        xr = pltpu.roll(x, s, 0)
        wrapped = jnp.sum(jnp.where(row < s, dy_first * xr[0:8], 0.0), axis=0, keepdims=True)
        dw_ref[taps - 1 - s:taps - s, :] = jnp.sum(dy * xr, axis=0, keepdims=True) - wrapped
    buf[...] = dx
    buf[T - 8:T, :] = dx_last
    return buf[...]


def _dn_act(y, normalize):
    s = y * jax.nn.sigmoid(y)
    if normalize:
        s = s * lax.rsqrt(jnp.sum(s * s, axis=-1, keepdims=True) + EPS)
    return s


def _dnconv_fwd(name, proj, cb, w, normalize):
    T = proj.shape[0]

    def body(x_ref, w_ref, o_ref, buf):
        o_ref[...] = _dn_act(_conv_taps(x_ref[...], w_ref, 4, buf), normalize)

    return _pcall(
        body, grid=(8,), in_specs=[pl.BlockSpec((T, LANE), lambda j: (0, cb + j)), pl.BlockSpec((4, LANE), lambda j: (0, cb + j))],
        out_specs=pl.BlockSpec((T, LANE), lambda j: (0, j)), out_shape=SDS((T, 1024), f32), scratch_shapes=[pltpu.VMEM((T, LANE), f32)],
        compiler_params=_cparams(VMEM_BIG), name=name)(proj, w)


def _dnconv_bwd(name, proj, cb, w, dout, normalize):
    T = proj.shape[0]

    def body(x_ref, w_ref, do_ref, dx_ref, dw_ref, buf, buf2):
        x = x_ref[...]
        y = _conv_taps(x, w_ref, 4, buf)
        _, vjp = jax.vjp(functools.partial(_dn_act, normalize=normalize), y)
        (dy,) = vjp(do_ref[...])
        dx_ref[...] = _conv_taps_bwd(x, dy, w_ref, dw_ref, 4, buf2).astype(dx_ref.dtype)

    return _pcall(
        body, grid=(8,),
        in_specs=[pl.BlockSpec((T, LANE), lambda j: (0, cb + j)), pl.BlockSpec((4, LANE), lambda j: (0, cb + j)),
                  pl.BlockSpec((T, LANE), lambda j: (0, j))],
        out_specs=[pl.BlockSpec((T, LANE), lambda j: (0, j)), pl.BlockSpec((4, LANE), lambda j: (0, j))],
        out_shape=[SDS((T, 1024), bf16), SDS((4, 1024), f32)], scratch_shapes=[pltpu.VMEM((T, LANE), f32)] * 2,
        compiler_params=_cparams(VMEM_BIG), name=name)(proj, w, dout)


def _ffn_point(a, lin):
    return a * jax.nn.sigmoid(a) * lin


def _ffnact_fwd(name, up, w, b):
    T = up.shape[0]
    nblk = D_FF // LANE

    def body(a_ref, l_ref, w_ref, b_ref, o_ref, buf):
        a = _conv_taps(a_ref[...], w_ref, 3, buf) + b_ref[...]
        o_ref[...] = _ffn_point(a, l_ref[...]).astype(o_ref.dtype)

    return _pcall(
        body, grid=(nblk,),
        in_specs=[pl.BlockSpec((T, LANE), lambda j: (0, j)), pl.BlockSpec((T, LANE), lambda j: (0, nblk + j)),
                  pl.BlockSpec((3, LANE), lambda j: (0, j)), pl.BlockSpec((1, LANE), lambda j: (0, j))],
        out_specs=pl.BlockSpec((T, LANE), lambda j: (0, j)), out_shape=SDS((T, D_FF), bf16), scratch_shapes=[pltpu.VMEM((T, LANE), f32)],
        compiler_params=_cparams(VMEM_BIG), name=name)(up, up, w, b)


def _ffnact_bwd(name, up, w, b, dmid):
    T = up.shape[0]
    nblk = D_FF // LANE

    def body(a_ref, l_ref, w_ref, b_ref, dm_ref, da_ref, dl_ref, dw_ref, db_ref, buf, buf2):
        x = a_ref[...]
        a = _conv_taps(x, w_ref, 3, buf) + b_ref[...]
        _, vjp = jax.vjp(_ffn_point, a, l_ref[...])
        da, dl = vjp(dm_ref[...].astype(f32))
        dl_ref[...] = dl.astype(dl_ref.dtype)
        db_ref[...] = jnp.sum(da, axis=0, keepdims=True)
        da_ref[...] = _conv_taps_bwd(x, da, w_ref, dw_ref, 3, buf2).astype(da_ref.dtype)

    col = lambda r: pl.BlockSpec((r, LANE), lambda j: (0, j))
    return _pcall(
        body, grid=(nblk,),
        in_specs=[col(T), pl.BlockSpec((T, LANE), lambda j: (0, nblk + j)), col(3), col(1), col(T)],
        out_specs=[col(T), col(T), col(3), col(1)],
        out_shape=[SDS((T, D_FF), bf16), SDS((T, D_FF), bf16), SDS((3, D_FF), f32), SDS((1, D_FF), f32)],
        scratch_shapes=[pltpu.VMEM((T, LANE), f32)] * 2, compiler_params=_cparams(VMEM_BIG), name=name)(up, up, w, b, dmid)


def _bmm(a, b, ca, cb, precision=HI):
    return lax.dot_general(a, b, (((ca,), (cb,)), ((0,), (0,))), precision=precision, preferred_element_type=f32)


def _make_bdot(ca, cb):
    def raw(x, y, cx, cy):
        return _bmm(x.astype(bf16), y.astype(bf16), cx, cy, precision=None)

    @jax.custom_vjp
    def f(a, b):
        return raw(a, b, ca, cb)

    def fwd(a, b):
        return raw(a, b, ca, cb), (a, b)

    def bwd(res, dy):
        a, b = res
        if (ca, cb) == (2, 1):
            return raw(dy, b, 2, 2), raw(a, dy, 1, 1)
        if (ca, cb) == (2, 2):
            return raw(dy, b, 2, 1), raw(dy, a, 1, 1)
        return raw(b, dy, 2, 2), raw(a, dy, 2, 1)

    f.defvjp(fwd, bwd)
    return f


_bdot_nn, _bdot_nt, _bdot_tn = _make_bdot(2, 1), _make_bdot(2, 2), _make_bdot(1, 1)


def _pieces(a, n):
    out, r = [], a
    for _ in range(n):
        p = r.astype(bf16)
        out.append(p)
        r = r - p.astype(f32)
    return out


def _bmm_split(x, y, cx, cy, nx=2, ny=2, order=1):
    xs, ys = _pieces(x, nx), _pieces(y, ny)
    acc = None
    for i in reversed(range(nx)):
        for j in reversed(range(ny)):
            if i + j <= order:
                t = _bmm(xs[i], ys[j], cx, cy, precision=None)
                acc = t if acc is None else acc + t
    return acc


@jax.custom_vjp
def _solve_apply(X, r):
    return _bmm_split(X, r, 2, 1)


def _solve_apply_fwd(X, r):
    return _bmm_split(X, r, 2, 1), (X, r)


def _solve_apply_bwd(res, dy):
    X, r = res
    return _bmm_split(dy, r, 2, 2), _bmm_split(X, dy, 1, 1)


_solve_apply.defvjp(_solve_apply_fwd, _solve_apply_bwd)


def _lower_ones(H, C):
    ri = lax.broadcasted_iota(jnp.int32, (H, C, C), 1)
    ci = lax.broadcasted_iota(jnp.int32, (H, C, C), 2)
    return (ri >= ci).astype(f32)


def _cumsum_rows_raw(G):
    return _bmm_split(_lower_ones(G.shape[0], G.shape[1]), G, 2, 1, nx=1, ny=3, order=2)


@jax.custom_vjp
def _cumsum_rows(G):
    return _cumsum_rows_raw(G)


def _cumsum_rows_fwd(G):
    return _cumsum_rows_raw(G), None


def _cumsum_rows_bwd(_, dy):
    return (_bmm_split(_lower_ones(dy.shape[0], dy.shape[1]), dy, 1, 1, nx=1, ny=3, order=2),)


_cumsum_rows.defvjp(_cumsum_rows_fwd, _cumsum_rows_bwd)


def _tri_inverse_raw(L):
    H, C, _ = L.shape
    ri = lax.broadcasted_iota(jnp.int32, (C, C), 0)
    ci = lax.broadcasted_iota(jnp.int32, (C, C), 1)
    eye = jnp.broadcast_to((ri == ci).astype(f32)[None], (H, C, C))
    Dg = jnp.where(((ri >> 3) == (ci >> 3))[None], L, 0.0)
    D2 = _bmm_split(Dg, Dg, 2, 1)
    X = _bmm_split(_bmm_split(eye - Dg, eye + D2, 2, 1), eye + _bmm_split(D2, D2, 2, 1), 2, 1)
    for lg in range(3, C.bit_length() - 1):
        same = (ri >> (lg + 1)) == (ci >> (lg + 1))
        lower_left = same & (((ri >> lg) & 1) == 1) & (((ci >> lg) & 1) == 0)
        X = X - _bmm_split(_bmm_split(X, jnp.where(lower_left[None], L, 0.0), 2, 1), X, 2, 1)
    return X


@jax.custom_vjp
def _tri_inverse(L):
    return _tri_inverse_raw(L)


def _tri_inverse_fwd(L):
    X = _tri_inverse_raw(L)
    return X, X


def _tri_inverse_bwd(X, dX):
    return (-_bmm_split(_bmm_split(X, dX, 1, 1), X, 2, 2),)


_tri_inverse.defvjp(_tri_inverse_fwd, _tri_inverse_bwd)


@jax.custom_vjp
def _tri_inverse_known(L, X):
    return X


def _tri_inverse_known_fwd(L, X):
    return X, X


def _tri_inverse_known_bwd(X, dX):
    return _tri_inverse_bwd(X, dX)[0], jnp.zeros_like(X)


_tri_inverse_known.defvjp(_tri_inverse_known_fwd, _tri_inverse_known_bwd)


DN_NC = 2


def _gdn_chunk(q, k, v, ab, alog, dtb, S, X_known=None, keep_X=False):
    H, C, _ = q.shape
    NC, NH = ab.shape[0], H // ab.shape[0]
    lane = lax.broadcasted_iota(jnp.int32, (H, C, LANE), 2)
    head = lax.broadcasted_iota(jnp.int32, (H, C, LANE), 0) & (NH - 1)
    abb = jnp.concatenate([jnp.broadcast_to(ab[c][None], (NH, C, LANE)) for c in range(NC)], axis=0)
    a = jnp.sum(jnp.where(lane == head, abb, 0.0), axis=2, keepdims=True)
    b = jnp.sum(jnp.where(lane == head + 8, abb, 0.0), axis=2, keepdims=True)
    pick = lax.broadcasted_iota(jnp.int32, (H, 1, LANE), 2) == (lax.broadcasted_iota(jnp.int32, (H, 1, LANE), 0) & (NH - 1))
    al = jnp.sum(jnp.where(pick, alog[None], 0.0), axis=2, keepdims=True)
    db = jnp.sum(jnp.where(pick, dtb[None], 0.0), axis=2, keepdims=True)
    g = -jnp.exp(al) * jax.nn.softplus(a + db)
    beta = jax.nn.sigmoid(b)
    ri = lax.broadcasted_iota(jnp.int32, (C, C), 0)
    ci = lax.broadcasted_iota(jnp.int32, (C, C), 1)
    G = jnp.broadcast_to(g, (H, C, LANE))
    gc = _cumsum_rows(G)
    gi = _cumsum_rows(jnp.broadcast_to(g, (H, C, C)))
    decay = jnp.exp(jnp.where((ri >= ci)[None], gi - jnp.swapaxes(gi, 1, 2), -jnp.inf))
    qs = q * (LANE ** -0.5)
    kb = k * beta
    L = jnp.where((ri > ci)[None], _bdot_nt(kb, k) * decay, 0.0)
    X = _tri_inverse(L) if X_known is None else _tri_inverse_known(L, X_known)
    egc = jnp.exp(gc)
    u = _solve_apply(X, v * beta)
    w = _solve_apply(X, kb * egc)
    qk = _bdot_nt(qs, k) * decay
    g_last = jnp.sum(G, axis=1, keepdims=True)
    k_dec = k * jnp.exp(g_last - gc)
    q_dec = qs * egc
    e_last = jnp.exp(g_last)
    outs = []
    for c in range(NC):
        sl = slice(c * NH, (c + 1) * NH)
        v_new = u[sl] - _bdot_nn(w[sl], S)
        outs.append(_bdot_nn(q_dec[sl], S) + _bdot_nn(qk[sl], v_new))
        S = S * e_last[sl] + _bdot_tn(k_dec[sl], v_new)
    o = jnp.concatenate(outs, axis=0)
    return (o, S, X) if keep_X else (o, S)


def _heads(ref):
    return jnp.stack([ref[DN_C * c:DN_C * (c + 1), LANE * h:LANE * (h + 1)] for c in range(DN_NC) for h in range(8)], axis=0)


def _put_heads(ref, val):
    for c in range(DN_NC):
        for h in range(8):
            ref[DN_C * c:DN_C * (c + 1), LANE * h:LANE * (h + 1)] = val[8 * c + h]


def _chunk_rows(ref):
    return jnp.stack([ref[DN_C * c:DN_C * (c + 1), :] for c in range(DN_NC)], axis=0)


def _hosted_parts(hosted):
    if hosted is None:
        return [], [], [], []
    n = len(hosted["arrays"])
    return list(hosted["arrays"]), [HBM_SPEC] * n, list(hosted["out_shape"]), _dma_sems(hosted["n_sems"])


def _gdn_fwd(name, q, k, v, proj, alog, dtb, hosted=None):
    T = q.shape[0]
    R = DN_C * DN_NC
    N = T // R
    h_args, h_specs, h_shapes, h_sems = _hosted_parts(hosted)
    nh = len(h_args)

    def body(q_ref, k_ref, v_ref, ab_ref, al_ref, dt_ref, *rest):
        h_ins, (o_ref, sall_ref, xinv_ref), h_outs, s_scr, sems = rest[:nh], rest[nh:nh + 3], rest[nh + 3:2 * nh + 3], rest[2 * nh + 3], rest[2 * nh + 4:]
        step = pl.program_id(0)

        @pl.when(step == 0)
        def _():
            s_scr[...] = jnp.zeros_like(s_scr)
            if hosted is not None:
                hosted["start"](h_ins, h_outs, *sems)

        S = s_scr[...]
        sall_ref[...] = S
        o, S_new, X = _gdn_chunk(_heads(q_ref), _heads(k_ref), _heads(v_ref), _chunk_rows(ab_ref), al_ref[...], dt_ref[...], S, keep_X=True)
        _put_heads(o_ref, o)
        s_scr[...] = S_new
        xinv_ref[...] = X

        if hosted is not None:
            @pl.when(step == N - 1)
            def _():
                hosted["finish"](h_ins, h_outs, *sems)

    blk = pl.BlockSpec((R, 8 * LANE), lambda n: (n, 0))
    vec = pl.BlockSpec((1, LANE), lambda n: (0, 0))
    state = pl.BlockSpec((None, 8, LANE, LANE), lambda n: (n, 0, 0, 0))
    xinv = pl.BlockSpec((None, 8 * DN_NC, DN_C, DN_C), lambda n: (n, 0, 0, 0))
    outs = _pcall(
        body, grid=(N,), in_specs=[blk, blk, blk, pl.BlockSpec((R, LANE), lambda n: (n, CB_AB)), vec, vec] + h_specs,
        out_specs=[blk, state, xinv] + h_specs,
        out_shape=[SDS((T, 1024), f32), SDS((N, 8, LANE, LANE), f32), SDS((N, 8 * DN_NC, DN_C, DN_C), f32)] + h_shapes,
        scratch_shapes=[pltpu.VMEM((8, LANE, LANE), f32)] + h_sems, name=name)(q, k, v, proj, alog, dtb, *h_args)
    return outs[0], outs[1], outs[2], list(outs[3:])


def _gdn_bwd(name, q, k, v, proj, alog, dtb, sall, xinv, do, hosted=None):
    T = q.shape[0]
    R = DN_C * DN_NC
    N = T // R
    h_args, h_specs, h_shapes, h_sems = _hosted_parts(hosted)
    nh = len(h_args)

    def body(q_ref, k_ref, v_ref, ab_ref, al_ref, dt_ref, s_ref, x_ref, do_ref, *rest):
        h_ins, h_outs, ds_scr, sems = rest[:nh], rest[nh + 6:2 * nh + 6], rest[2 * nh + 6], rest[2 * nh + 7:]
        dq_ref, dk_ref, dv_ref, dab_ref, dal_ref, ddt_ref = rest[nh:nh + 6]
        step = pl.program_id(0)

        @pl.when(step == 0)
        def _():
            ds_scr[...] = jnp.zeros_like(ds_scr)
            dal_ref[...] = jnp.zeros_like(dal_ref)
            ddt_ref[...] = jnp.zeros_like(ddt_ref)
            if hosted is not None:
                hosted["start"](h_ins, h_outs, *sems)

        _, vjp = jax.vjp(functools.partial(_gdn_chunk, X_known=x_ref[...]), _heads(q_ref), _heads(k_ref), _heads(v_ref), _chunk_rows(ab_ref),
                         al_ref[...], dt_ref[...], s_ref[...])
        dq, dk, dv, dab, dal, ddt, dS = vjp((_heads(do_ref), ds_scr[...]))
        _put_heads(dq_ref, dq)
        _put_heads(dk_ref, dk)
        _put_heads(dv_ref, dv)
        ds_scr[...] = dS
        for c in range(DN_NC):
            dab_ref[DN_C * c:DN_C * (c + 1), :] = dab[c]
        dal_ref[...] += dal
        ddt_ref[...] += ddt

        if hosted is not None:
            @pl.when(step == N - 1)
            def _():
                hosted["finish"](h_ins, h_outs, *sems)

    blk = pl.BlockSpec((R, 8 * LANE), lambda n: (N - 1 - n, 0))
    vec = pl.BlockSpec((1, LANE), lambda n: (0, 0))
    state = pl.BlockSpec((None, 8, LANE, LANE), lambda n: (N - 1 - n, 0, 0, 0))
    outs = _pcall(
        body, grid=(N,),
        in_specs=[blk, blk, blk, pl.BlockSpec((R, LANE), lambda n: (N - 1 - n, CB_AB)), vec, vec, state,
                  pl.BlockSpec((None, 8 * DN_NC, DN_C, DN_C), lambda n: (N - 1 - n, 0, 0, 0)), blk] + h_specs,
        out_specs=[blk, blk, blk, pl.BlockSpec((R, LANE), lambda n: (N - 1 - n, 0)), vec, vec] + h_specs,
        out_shape=[SDS((T, 1024), f32)] * 3 + [SDS((T, LANE), f32), SDS((1, LANE), f32), SDS((1, LANE), f32)] + h_shapes,
        scratch_shapes=[pltpu.VMEM((8, LANE, LANE), f32)] + h_sems, name=name)(q, k, v, proj, alog, dtb, sall, xinv, do, *h_args)
    return tuple(outs[:6]), list(outs[6:])


def _segmean_raw(x2, bd):
    return jnp.concatenate([_dot(x2[:, LANE * j:LANE * (j + 1)], bd, 1, 0) for j in range(x2.shape[1] // LANE)], axis=1)


@jax.custom_vjp
def _segmean(x2, bd):
    return _segmean_raw(x2, bd)


def _segmean_fwd(x2, bd):
    return _segmean_raw(x2, bd), bd


def _segmean_bwd(bd, dy):
    return _segmean_raw(dy, bd), jnp.zeros_like(bd)


_segmean.defvjp(_segmean_fwd, _segmean_bwd)


def _qknorm_fn(x, w, bd):
    return x * lax.rsqrt(_segmean(x * x, bd) + EPS) * w


def _rope_apply(xn, c, s1, s2):
    W = xn.shape[1]
    return xn * c + pltpu.roll(xn, W - 8, 1) * s1 + pltpu.roll(xn, 8, 1) * s2


def _rope_apply_t(d, c, s1, s2):
    W = d.shape[1]
    return d * c + pltpu.roll(d * s1, 8, 1) + pltpu.roll(d * s2, W - 8, 1)


def _qkprep_fwd(name, proj, wb, width, w, bd, tabs):
    T = proj.shape[0]
    tr = 128

    def body(x_ref, w_ref, bd_ref, c_ref, s1_ref, s2_ref, o_ref):
        xn = _qknorm_fn(x_ref[...], w_ref[...], bd_ref[...])
        o_ref[...] = _rope_apply(xn, c_ref[...], s1_ref[...], s2_ref[...])

    row0 = pl.BlockSpec((tr, width), lambda i: (i, 0))
    full = lambda a: pl.BlockSpec(a.shape, lambda i: (0, 0))
    return _pcall(
        body, grid=(T // tr,), in_specs=[pl.BlockSpec((tr, width), lambda i: (i, wb)), full(w), full(bd), row0, row0, row0],
        out_specs=row0, out_shape=SDS((T, width), f32), name=name)(proj, w, bd, *tabs)


def _qkprep_bwd(name, proj, wb, width, w, bd, tabs, dout):
    T = proj.shape[0]
    tr = 128

    def body(x_ref, w_ref, bd_ref, c_ref, s1_ref, s2_ref, do_ref, dx_ref, dw_ref):
        i = pl.program_id(0)
        dxn = _rope_apply_t(do_ref[...], c_ref[...], s1_ref[...], s2_ref[...])
        bd = bd_ref[...]
        _, vjp = jax.vjp(lambda x, w_: _qknorm_fn(x, w_, bd), x_ref[...], w_ref[...])
        dx, dw = vjp(dxn)
        dx_ref[...] = dx.astype(dx_ref.dtype)

        @pl.when(i == 0)
        def _():
            dw_ref[...] = jnp.zeros_like(dw_ref)

        dw_ref[...] += dw

    row0 = pl.BlockSpec((tr, width), lambda i: (i, 0))
    full = lambda a: pl.BlockSpec(a.shape, lambda i: (0, 0))
    return _pcall(
        body, grid=(T // tr,), in_specs=[pl.BlockSpec((tr, width), lambda i: (i, wb)), full(w), full(bd), row0, row0, row0, row0],
        out_specs=[row0, full(w)], out_shape=[SDS((T, width), bf16), SDS(w.shape, f32)], name=name)(proj, w, bd, *tabs, dout)


def _make_dot16(ca, cb):
    def raw(x, y, cx, cy):
        return _dot(x.astype(bf16), y.astype(bf16), cx, cy, precision=None)

    @jax.custom_vjp
    def f(a, b):
        return raw(a, b, ca, cb)

    def fwd(a, b):
        return raw(a, b, ca, cb), (a, b)

    def bwd(res, dy):
        a, b = res
        if (ca, cb) == (1, 0):
            return raw(dy, b, 1, 1), raw(a, dy, 0, 0)
        return raw(dy, b, 1, 0), raw(dy, a, 0, 0)

    f.defvjp(fwd, bwd)
    return f


_dot16_nn, _dot16_nt = _make_dot16(1, 0), _make_dot16(1, 1)


def _attn_group(qg, kb, vb, sinks, first, hk):
    R = qg.shape[0]
    s = _dot16_nt(qg, kb) * 0.125
    qi = lax.broadcasted_iota(jnp.int32, (R, 2 * SWA_B), 0) & (SWA_B - 1)
    kj = lax.broadcasted_iota(jnp.int32, (R, 2 * SWA_B), 1)
    rel = qi + SWA_B - kj
    mask = (rel >= 0) & (rel < SWA_B) & ((kj >= SWA_B) | jnp.logical_not(first))
    s = jnp.where(mask, s, -jnp.inf)
    head = (lax.broadcasted_iota(jnp.int32, (R, LANE), 0) >> 7) + 8 * hk
    lane = lax.broadcasted_iota(jnp.int32, (R, LANE), 1)
    sink = jnp.sum(jnp.where(lane == head, jnp.broadcast_to(sinks, (R, LANE)), 0.0), axis=1, keepdims=True)
    m = lax.stop_gradient(jnp.maximum(jnp.max(s, axis=1, keepdims=True), sink))
    p = jnp.exp(s - m)
    denom = jnp.sum(p, axis=1, keepdims=True) + jnp.exp(sink - m)
    return _dot16_nn(p / denom, vb)


def _group_rows(ref, hk):
    return jnp.concatenate([ref[:, 64 * (8 * hk + g):64 * (8 * hk + g + 1)] for g in range(8)], axis=0)


def _put_group(ref, hk, val):
    for g in range(8):
        ref[:, 64 * (8 * hk + g):64 * (8 * hk + g + 1)] = val[SWA_B * g:SWA_B * (g + 1)].astype(ref.dtype)


def _attn_specs():
    qs = pl.BlockSpec((SWA_B, 1024), lambda i: (i, 0))
    cur = pl.BlockSpec((SWA_B, LANE), lambda i: (i, 0))
    prev = pl.BlockSpec((SWA_B, LANE), lambda i: (jnp.maximum(i - 1, 0), 0))
    vcur = pl.BlockSpec((SWA_B, LANE), lambda i: (i, CB_SWV))
    vprev = pl.BlockSpec((SWA_B, LANE), lambda i: (jnp.maximum(i - 1, 0), CB_SWV))
    vec = pl.BlockSpec((1, LANE), lambda i: (0, 0))
    return qs, cur, prev, vcur, vprev, vec


def _attn_fwd(name, sq, sk, proj, sinks):
    T = sq.shape[0]

    def body(q_ref, kp_ref, kc_ref, vp_ref, vc_ref, sk_ref, o_ref):
        first = pl.program_id(0) == 0
        sinks_v = sk_ref[...]
        for hk in range(2):
            ks = slice(64 * hk, 64 * hk + 64)
            kb = jnp.concatenate([kp_ref[:, ks], kc_ref[:, ks]], axis=0)
            vb = jnp.concatenate([vp_ref[:, ks], vc_ref[:, ks]], axis=0)
            _put_group(o_ref, hk, _attn_group(_group_rows(q_ref, hk), kb, vb, sinks_v, first, hk))

    qs, cur, prev, vcur, vprev, vec = _attn_specs()
    return _pcall(body, grid=(T // SWA_B,), in_specs=[qs, prev, cur, vprev, vcur, vec], out_specs=qs,
                  out_shape=SDS((T, 1024), bf16), name=name)(sq, sk, sk, proj, proj, sinks)


def _attn_bwd(name, sq, sk, proj, sinks, do):
    T = sq.shape[0]

    def body(q_ref, kp_ref, kc_ref, vp_ref, vc_ref, sk_ref, do_ref, dq_ref, dkp_ref, dkc_ref, dvp_ref, dvc_ref, dsk_ref):
        first = pl.program_id(0) == 0

        @pl.when(first)
        def _():
            dsk_ref[...] = jnp.zeros_like(dsk_ref)

        sinks_v = sk_ref[...]
        dsk = jnp.zeros((1, LANE), f32)
        for hk in range(2):
            ks = slice(64 * hk, 64 * hk + 64)
            kb = jnp.concatenate([kp_ref[:, ks], kc_ref[:, ks]], axis=0)
            vb = jnp.concatenate([vp_ref[:, ks], vc_ref[:, ks]], axis=0)
            _, vjp = jax.vjp(functools.partial(_attn_group, first=first, hk=hk), _group_rows(q_ref, hk), kb, vb, sinks_v)
            dq, dkb, dvb, ds_ = vjp(_group_rows(do_ref, hk))
            _put_group(dq_ref, hk, dq)
            dsk = dsk + ds_
            dkp_ref[:, ks] = dkb[:SWA_B]
            dkc_ref[:, ks] = dkb[SWA_B:]
            dvp_ref[:, ks] = dvb[:SWA_B]
            dvc_ref[:, ks] = dvb[SWA_B:]
        dsk_ref[...] += dsk

    qs, cur, prev, vcur, vprev, vec = _attn_specs()
    return _pcall(
        body, grid=(T // SWA_B,), in_specs=[qs, prev, cur, vprev, vcur, vec, qs], out_specs=[qs, cur, cur, cur, cur, vec],
        out_shape=[SDS((T, 1024), f32)] + [SDS((T, LANE), f32)] * 4 + [SDS((1, LANE), f32)], name=name)(sq, sk, sk, proj, proj, sinks, do)


def _shift_add(name, cur, prev, out_dtype):
    T = cur.shape[0]
    nb = T // SWA_B

    def body(c_ref, p_ref, o_ref):
        has_next = (pl.program_id(0) + 1 < nb).astype(f32)
        o_ref[...] = (c_ref[...] + has_next * p_ref[...]).astype(o_ref.dtype)

    blk = pl.BlockSpec((SWA_B, LANE), lambda i: (i, 0))
    nxt = pl.BlockSpec((SWA_B, LANE), lambda i: (jnp.minimum(i + 1, nb - 1), 0))
    return _pcall(body, grid=(nb,), in_specs=[blk, nxt], out_specs=blk, out_shape=SDS((T, LANE), out_dtype), name=name)(cur, prev)


def _loss(name, y, tgt):
    T = y.shape[0]

    def body(y_ref, t_ref, l_ref, dy_ref):
        @pl.when(pl.program_id(0) == 0)
        def _():
            l_ref[...] = jnp.zeros_like(l_ref)

        d = y_ref[...] - t_ref[...]
        l_ref[...] += jnp.sum(d * d) * (0.5 / D)
        dy_ref[...] = d * (1.0 / D)

    row = pl.BlockSpec((TR, D), lambda i: (i, 0))
    return _pcall(body, grid=(T // TR,), in_specs=[row, row], out_specs=[pl.BlockSpec((8, LANE), lambda i: (0, 0)), row],
                  out_shape=[SDS((8, LANE), f32), SDS((T, D), f32)], name=name)(y, tgt)


def _adamw(name, w, g, m, v):
    shape = w.shape
    C = shape[-1]
    R = int(np.prod(shape[:-1]))
    tr = _pick(R, (128, 64, 16, 8))
    bc1 = np.float32(1.0 - ADAM_B1 ** ADAM_STEP)
    bc2 = np.float32(1.0 - ADAM_B2 ** ADAM_STEP)

    def body(w_ref, g_ref, m_ref, v_ref, d_ref, mo_ref, vo_ref):
        g_ = g_ref[...]
        m_ = ADAM_B1 * m_ref[...] + (1.0 - ADAM_B1) * g_
        v_ = ADAM_B2 * v_ref[...] + (1.0 - ADAM_B2) * (g_ * g_)
        d_ref[...] = -ADAM_LR * ((m_ / bc1) / (jnp.sqrt(v_ / bc2) + ADAM_EPS) + ADAM_WD * w_ref[...])
        mo_ref[...] = m_
        vo_ref[...] = v_

    blk = pl.BlockSpec((tr, C), lambda i: (i, 0))
    outs = _pcall(body, grid=(R // tr,), in_specs=[blk] * 4, out_specs=[blk] * 3, out_shape=[SDS((R, C), f32)] * 3,
                  compiler_params=_cparams(VMEM_BIG), name=name)(*[t.reshape(R, C) for t in (w, g, m, v)])
    return [o.reshape(shape) for o in outs]


def _silu_rows(name, x):
    def body(x_ref, o_ref):
        t = x_ref[...]
        o_ref[...] = (t * jax.nn.sigmoid(t)).astype(o_ref.dtype)

    return _pcall(body, out_shape=SDS(x.shape, bf16), name=name)(x)


def _sum_leading(name, x):
    n = x.shape[0]

    def body(x_ref, o_ref):
        acc = x_ref[0]
        for k in range(1, n):
            acc = acc + x_ref[k]
        o_ref[...] = acc

    tr = _pick(x.shape[1], (COMM_TR, 8))
    return _pcall(body, grid=(x.shape[1] // tr,), in_specs=[pl.BlockSpec((n, tr, x.shape[2]), lambda i: (0, i, 0))],
                  out_specs=pl.BlockSpec((tr, x.shape[2]), lambda i: (i, 0)), out_shape=SDS(x.shape[1:], x.dtype), name=name)(x)


def _add_my_half(name, g4, b1, c):
    _, R, W = g4.shape
    nblk = (R // 2) // COMM_TR

    def body(c_ref, g_ref, b_ref, o_ref):
        o_ref[...] = g_ref[...] + b_ref[...]

    grid_spec = pltpu.PrefetchScalarGridSpec(
        num_scalar_prefetch=1, grid=(4, nblk),
        in_specs=[pl.BlockSpec((None, COMM_TR, W), lambda s, i, c_ref: (s, c_ref[0] * nblk + i, 0)),
                  pl.BlockSpec((None, COMM_TR, W), lambda s, i, c_ref: (s, i, 0))],
        out_specs=pl.BlockSpec((None, COMM_TR, W), lambda s, i, c_ref: (s, i, 0)))
    return _pcall(body, grid_spec=grid_spec, out_shape=SDS((4, R // 2, W), f32), name=name)(c.reshape(1), g4, b1)


HBM_SPEC = pl.BlockSpec(memory_space=pltpu.HBM)


def _position():
    x, y, c = lax.axis_index("x"), lax.axis_index("y"), lax.axis_index("c")
    return x, y, c, [(1 - x, y), (x, 1 - y), (1 - x, 1 - y)]


def _remote(src, dst, send_sems, recv_sems, k, to):
    return pltpu.make_async_remote_copy(src_ref=src, dst_ref=dst, send_sem=send_sems.at[k], recv_sem=recv_sems.at[k],
                                        device_id=to, device_id_type=MESH)


def _allgather_chips(name, buf):
    R, W = buf.shape
    Rh = R // 2

    def body(in_ref, out_ref, send_sems, recv_sems, local_sem):
        x, y, c, chips = _position()
        me = 2 * x + y
        sib = (x, y, 1 - c)
        half = pl.ds(pl.multiple_of(c * Rh, 32), Rh)
        ohalf = pl.ds(pl.multiple_of((1 - c) * Rh, 32), Rh)
        mine = pltpu.make_async_copy(in_ref, out_ref.at[me], local_sem)
        mine.start()
        first = [_remote(in_ref.at[half], out_ref.at[me, half], send_sems, recv_sems, j, (cx, cy, c)) for j, (cx, cy) in enumerate(chips)]
        for cp in first:
            cp.start()
        passed = []
        for j, (cx, cy) in enumerate(chips):
            rows = out_ref.at[2 * cx + cy, half]
            _remote(rows, rows, send_sems, recv_sems, j, (cx, cy, c)).wait_recv()
            cp = _remote(rows, rows, send_sems, recv_sems, 3 + j, sib)
            cp.start()
            passed.append(cp)
        for j, (cx, cy) in enumerate(chips):
            rows = out_ref.at[2 * cx + cy, ohalf]
            _remote(rows, rows, send_sems, recv_sems, 3 + j, sib).wait_recv()
        for cp in first + passed:
            cp.wait_send()
        mine.wait()

    return _pcall(body, in_specs=[HBM_SPEC], out_specs=HBM_SPEC, out_shape=SDS((4, R, W), buf.dtype),
                  scratch_shapes=[pltpu.SemaphoreType.DMA((6,)), pltpu.SemaphoreType.DMA((6,)), pltpu.SemaphoreType.DMA], name=name)(buf)


def _swap_halves(name, g4):
    _, R, W = g4.shape
    Rh = R // 2

    def body(in_ref, out_ref, send_sems, recv_sems):
        x, y, c, _ = _position()
        ohalf = pl.ds(pl.multiple_of((1 - c) * Rh, 32), Rh)
        cp = _remote(in_ref.at[:, ohalf, :], out_ref, send_sems, recv_sems, 0, (x, y, 1 - c))
        cp.start()
        cp.wait()

    return _pcall(body, in_specs=[HBM_SPEC], out_specs=HBM_SPEC, out_shape=SDS((4, Rh, W), g4.dtype),
                  scratch_shapes=[pltpu.SemaphoreType.DMA((1,)), pltpu.SemaphoreType.DMA((1,))], name=name)(g4)


def _scatter_chips(name, p4):
    _, Rh, W = p4.shape

    def body(in_ref, out_ref, send_sems, recv_sems, local_sem):
        x, y, c, chips = _position()
        me = 2 * x + y
        mine = pltpu.make_async_copy(in_ref.at[me], out_ref.at[me], local_sem)
        mine.start()
        sends = [_remote(in_ref.at[2 * cx + cy], out_ref.at[me], send_sems, recv_sems, j, (cx, cy, c)) for j, (cx, cy) in enumerate(chips)]
        for cp in sends:
            cp.start()
        for j, (cx, cy) in enumerate(chips):
            slot = out_ref.at[2 * cx + cy]
            _remote(slot, slot, send_sems, recv_sems, j, (cx, cy, c)).wait_recv()
        for cp in sends:
            cp.wait_send()
        mine.wait()

    return _pcall(body, in_specs=[HBM_SPEC], out_specs=HBM_SPEC, out_shape=SDS((4, Rh, W), p4.dtype),
                  scratch_shapes=[pltpu.SemaphoreType.DMA((3,)), pltpu.SemaphoreType.DMA((3,)), pltpu.SemaphoreType.DMA], name=name)(p4)


def _join_halves(name, r):
    Rh, W = r.shape

    def body(in_ref, out_ref, send_sems, recv_sems, local_sem):
        x, y, c, _ = _position()
        mine = pltpu.make_async_copy(in_ref, out_ref.at[c], local_sem)
        mine.start()
        cp = _remote(in_ref, out_ref.at[c], send_sems, recv_sems, 0, (x, y, 1 - c))
        cp.start()
        _remote(in_ref, out_ref.at[1 - c], send_sems, recv_sems, 0, (x, y, 1 - c)).wait_recv()
        cp.wait_send()
        mine.wait()

    return _pcall(body, in_specs=[HBM_SPEC], out_specs=HBM_SPEC, out_shape=SDS((2, Rh, W), r.dtype),
                  scratch_shapes=[pltpu.SemaphoreType.DMA((1,)), pltpu.SemaphoreType.DMA((1,)), pltpu.SemaphoreType.DMA], name=name)(r)


def _allgather_all(name, buf):
    r, W = buf.shape

    def body(in_ref, out_ref, send_sems, recv_sems, local_sem):
        x, y, c, _ = _position()
        me = 4 * x + 2 * y + c
        mine = pltpu.make_async_copy(in_ref, out_ref.at[me], local_sem)
        mine.start()
        peers = []
        for mk in range(1, 8):
            mx, my, mc = (mk >> 2) & 1, (mk >> 1) & 1, mk & 1
            px = 1 - x if mx else x
            py = 1 - y if my else y
            pc = 1 - c if mc else c
            peers.append((px, py, pc))
        sends = [_remote(in_ref, out_ref.at[me], send_sems, recv_sems, k, p) for k, p in enumerate(peers)]
        for cp in sends:
            cp.start()
        for k, (px, py, pc) in enumerate(peers):
            slot = out_ref.at[4 * px + 2 * py + pc]
            _remote(slot, slot, send_sems, recv_sems, k, (px, py, pc)).wait_recv()
        for cp in sends:
            cp.wait_send()
        mine.wait()

    return _pcall(body, in_specs=[HBM_SPEC], out_specs=HBM_SPEC, out_shape=SDS((8, r, W), buf.dtype),
                  scratch_shapes=[pltpu.SemaphoreType.DMA((7,)), pltpu.SemaphoreType.DMA((7,)), pltpu.SemaphoreType.DMA], name=name)(buf)


def _reduce_scatter(g4, c, tag):
    b1 = _swap_halves("rs_swap_" + tag, g4)
    p4 = _add_my_half("rs_pair_" + tag, g4, b1, c)
    b2 = _scatter_chips("rs_scatter_" + tag, p4)
    r = _sum_leading("rs_sum_" + tag, b2)
    full = _join_halves("rs_join_" + tag, r)
    return full.reshape(g4.shape[1], g4.shape[2])


def _dma_sems(n):
    return [pltpu.SemaphoreType.DMA((n,)), pltpu.SemaphoreType.DMA((n,))]


def _gather_chips(name, shards):
    n = len(shards)

    def body(*refs):
        ins, outs = refs[:n], refs[n:2 * n]
        send_sems, recv_sems = refs[2 * n:]
        x, y, c, chips = _position()
        me = 2 * x + y
        sib = (x, y, 1 - c)
        sends, halves = [], []
        for i in range(n):
            rh = ins[i].shape[0] // 2
            halves.append((pl.ds(pl.multiple_of(c * rh, 16), rh), pl.ds(pl.multiple_of((1 - c) * rh, 16), rh)))
        for i in range(n):
            for j, (cx, cy) in enumerate(chips):
                cp = _remote(ins[i].at[halves[i][0]], outs[i].at[me, halves[i][0]], send_sems, recv_sems, 6 * i + j, (cx, cy, c))
                cp.start()
                sends.append(cp)
        for j, (cx, cy) in enumerate(chips):
            for i in range(n):
                rows = outs[i].at[2 * cx + cy, halves[i][0]]
                _remote(rows, rows, send_sems, recv_sems, 6 * i + j, (cx, cy, c)).wait_recv()
                cp = _remote(rows, rows, send_sems, recv_sems, 6 * i + 3 + j, sib)
                cp.start()
                sends.append(cp)
        for j, (cx, cy) in enumerate(chips):
            for i in range(n):
                rows = outs[i].at[2 * cx + cy, halves[i][1]]
                _remote(rows, rows, send_sems, recv_sems, 6 * i + 3 + j, sib).wait_recv()
        for cp in sends:
            cp.wait_send()

    return _pcall(body, in_specs=[HBM_SPEC] * n, out_specs=[HBM_SPEC] * n, out_shape=[SDS((4,) + s.shape, s.dtype) for s in shards],
                  scratch_shapes=_dma_sems(6 * n), name=name)(*shards)


def _swap_halves_multi(name, slots):
    n = len(slots)

    def body(*refs):
        ins, outs = refs[:n], refs[n:2 * n]
        send_sems, recv_sems = refs[2 * n:]
        x, y, c, _ = _position()
        cps = []
        for i in range(n):
            rh = ins[i].shape[1] // 2
            ohalf = pl.ds(pl.multiple_of((1 - c) * rh, 8), rh)
            cp = _remote(ins[i].at[:, ohalf, :], outs[i], send_sems, recv_sems, i, (x, y, 1 - c))
            cp.start()
            cps.append(cp)
        for cp in cps:
            cp.wait()

    return _pcall(body, in_specs=[HBM_SPEC] * n, out_specs=[HBM_SPEC] * n,
                  out_shape=[SDS((4, s.shape[1] // 2, s.shape[2]), s.dtype) for s in slots], scratch_shapes=_dma_sems(n), name=name)(*slots)


def _pair_add(name, g4, b1, c):
    _, R, W = g4.shape
    tr = _pick(R // 2, (256, 128, 32, 16))
    nblk = (R // 2) // tr

    def body(c_ref, g_ref, b_ref, o_ref):
        o_ref[...] = (g_ref[...] + b_ref[...]).astype(o_ref.dtype)

    grid_spec = pltpu.PrefetchScalarGridSpec(
        num_scalar_prefetch=1, grid=(4, nblk),
        in_specs=[pl.BlockSpec((None, tr, W), lambda s, i, c_ref: (s, c_ref[0] * nblk + i, 0)),
                  pl.BlockSpec((None, tr, W), lambda s, i, c_ref: (s, i, 0))],
        out_specs=pl.BlockSpec((None, tr, W), lambda s, i, c_ref: (s, i, 0)))
    return _pcall(body, grid_spec=grid_spec, out_shape=SDS((4, R // 2, W), bf16), name=name)(c.reshape(1), g4, b1)


def _scatter_chips_multi(name, ps):
    n = len(ps)

    def body(*refs):
        ins, outs = refs[:n], refs[n:2 * n]
        send_sems, recv_sems = refs[2 * n:]
        x, y, c, chips = _position()
        me = 2 * x + y
        sends = []
        for i in range(n):
            for j, (cx, cy) in enumerate(chips):
                cp = _remote(ins[i].at[2 * cx + cy], outs[i].at[me], send_sems, recv_sems, 3 * i + j, (cx, cy, c))
                cp.start()
                sends.append(cp)
        for i in range(n):
            for j, (cx, cy) in enumerate(chips):
                slot = outs[i].at[2 * cx + cy]
                _remote(slot, slot, send_sems, recv_sems, 3 * i + j, (cx, cy, c)).wait_recv()
        for cp in sends:
            cp.wait_send()

    return _pcall(body, in_specs=[HBM_SPEC] * n, out_specs=[HBM_SPEC] * n, out_shape=[SDS(p.shape, p.dtype) for p in ps],
                  scratch_shapes=_dma_sems(3 * n), name=name)(*ps)


def _sum_chips(name, p4, b2, chip, c):
    _, Rh, W = p4.shape
    tr = _pick(Rh, (256, 128, 32, 16))
    nblk = Rh // tr

    def body(m_ref, c_ref, own_ref, r1_ref, r2_ref, r3_ref, o_ref):
        o_ref[...] = ((own_ref[...].astype(f32) + r1_ref[...].astype(f32)) + r2_ref[...].astype(f32)) + r3_ref[...].astype(f32)

    other = lambda k: pl.BlockSpec((None, tr, W), lambda i, m_ref, c_ref: (m_ref[0] ^ k, i, 0))
    grid_spec = pltpu.PrefetchScalarGridSpec(
        num_scalar_prefetch=2, grid=(nblk,),
        in_specs=[pl.BlockSpec((None, tr, W), lambda i, m_ref, c_ref: (m_ref[0], i, 0)), other(1), other(2), other(3)],
        out_specs=pl.BlockSpec((tr, W), lambda i, m_ref, c_ref: (c_ref[0] * nblk + i, 0)))
    return _pcall(body, grid_spec=grid_spec, out_shape=SDS((2 * Rh, W), f32), name=name)(chip.reshape(1), c.reshape(1), p4, b2, b2, b2)


def _join_halves(name, fulls):
    n = len(fulls)

    def body(*refs):
        outs = refs[n:2 * n]
        send_sems, recv_sems = refs[2 * n:]
        x, y, c, _ = _position()
        cps = []
        for i in range(n):
            rh = outs[i].shape[0] // 2
            mine = outs[i].at[pl.ds(pl.multiple_of(c * rh, 8), rh)]
            theirs = outs[i].at[pl.ds(pl.multiple_of((1 - c) * rh, 8), rh)]
            cp = _remote(mine, mine, send_sems, recv_sems, i, (x, y, 1 - c))
            cp.start()
            cps.append((cp, _remote(theirs, theirs, send_sems, recv_sems, i, (x, y, 1 - c))))
        for cp, back in cps:
            back.wait_recv()
            cp.wait_send()

    return _pcall(body, in_specs=[HBM_SPEC] * n, out_specs=[HBM_SPEC] * n, out_shape=[SDS(r.shape, r.dtype) for r in fulls],
                  input_output_aliases={i: i for i in range(n)}, scratch_shapes=_dma_sems(n), name=name)(*fulls)


def _hosted_gather(shards):
    n = len(shards)

    def half(ref_rows, c):
        rh = ref_rows // 2
        return pl.ds(pl.multiple_of(c * rh, 16), rh)

    def start(ins, outs, send_sems, recv_sems):
        x, y, c, chips = _position()
        me = 2 * x + y
        for i in range(n):
            rows = half(ins[i].shape[0], c)
            for j, (cx, cy) in enumerate(chips):
                _remote(ins[i].at[rows], outs[i].at[me, rows], send_sems, recv_sems, 3 * i + j, (cx, cy, c)).start()

    def finish(ins, outs, send_sems, recv_sems):
        x, y, c, chips = _position()
        me = 2 * x + y
        for i in range(n):
            rows = half(ins[i].shape[0], c)
            for j, (cx, cy) in enumerate(chips):
                _remote(ins[i].at[rows], outs[i].at[2 * cx + cy, rows], send_sems, recv_sems, 3 * i + j, (cx, cy, c)).wait_recv()
        for i in range(n):
            rows = half(ins[i].shape[0], c)
            for j, (cx, cy) in enumerate(chips):
                _remote(ins[i].at[rows], outs[i].at[me, rows], send_sems, recv_sems, 3 * i + j, (cx, cy, c)).wait_send()

    return {"arrays": shards, "out_shape": [SDS((4,) + s.shape, s.dtype) for s in shards], "n_sems": 3 * n, "start": start, "finish": finish}


def _gather_forward(name, gathered):
    n = len(gathered)

    def body(*refs):
        outs = refs[n:2 * n]
        send_sems, recv_sems = refs[2 * n:]
        x, y, c, chips = _position()
        sib = (x, y, 1 - c)
        sends = []
        for i in range(n):
            rh = outs[i].shape[1] // 2
            mine = pl.ds(pl.multiple_of(c * rh, 16), rh)
            for j, (cx, cy) in enumerate(chips):
                rows = outs[i].at[2 * cx + cy, mine]
                cp = _remote(rows, rows, send_sems, recv_sems, 3 * i + j, sib)
                cp.start()
                sends.append(cp)
        for i in range(n):
            rh = outs[i].shape[1] // 2
            theirs = pl.ds(pl.multiple_of((1 - c) * rh, 16), rh)
            for j, (cx, cy) in enumerate(chips):
                rows = outs[i].at[2 * cx + cy, theirs]
                _remote(rows, rows, send_sems, recv_sems, 3 * i + j, sib).wait_recv()
        for cp in sends:
            cp.wait_send()

    return _pcall(body, in_specs=[HBM_SPEC] * n, out_specs=[HBM_SPEC] * n, out_shape=[SDS(g.shape, g.dtype) for g in gathered],
                  input_output_aliases={i: i for i in range(n)}, scratch_shapes=_dma_sems(3 * n), name=name)(*gathered)


def _hosted_scatter(ps):
    n = len(ps)

    def start(ins, outs, send_sems, recv_sems):
        x, y, c, chips = _position()
        me = 2 * x + y
        for i in range(n):
            for j, (cx, cy) in enumerate(chips):
                _remote(ins[i].at[2 * cx + cy], outs[i].at[me], send_sems, recv_sems, 3 * i + j, (cx, cy, c)).start()

    def finish(ins, outs, send_sems, recv_sems):
        x, y, c, chips = _position()
        me = 2 * x + y
        for i in range(n):
            for j, (cx, cy) in enumerate(chips):
                slot = outs[i].at[2 * cx + cy]
                _remote(slot, slot, send_sems, recv_sems, 3 * i + j, (cx, cy, c)).wait_recv()
        for i in range(n):
            for j, (cx, cy) in enumerate(chips):
                _remote(ins[i].at[2 * cx + cy], outs[i].at[me], send_sems, recv_sems, 3 * i + j, (cx, cy, c)).wait_send()

    return {"arrays": ps, "out_shape": [SDS(p.shape, p.dtype) for p in ps], "n_sems": 3 * n, "start": start, "finish": finish}


def _hosted_swap(slots):
    n = len(slots)

    def copies(ins, outs, send_sems, recv_sems):
        x, y, c, _ = _position()
        cps = []
        for i in range(n):
            rh = ins[i].shape[1] // 2
            ohalf = pl.ds(pl.multiple_of((1 - c) * rh, 8), rh)
            cps.append(_remote(ins[i].at[:, ohalf, :], outs[i], send_sems, recv_sems, i, (x, y, 1 - c)))
        return cps

    def start(ins, outs, send_sems, recv_sems):
        for cp in copies(ins, outs, send_sems, recv_sems):
            cp.start()

    def finish(ins, outs, send_sems, recv_sems):
        for cp in copies(ins, outs, send_sems, recv_sems):
            cp.wait()

    return {"arrays": slots, "out_shape": [SDS((4, s.shape[1] // 2, s.shape[2]), s.dtype) for s in slots], "n_sems": n,
            "start": start, "finish": finish}


def _rs_begin(slots, c):
    b1 = _swap_halves_multi("rs_swap", slots)
    return [_pair_add("rs_pair_%d" % i, g, b, c) for i, (g, b) in enumerate(zip(slots, b1))]


def _rs_end(ps, b2, c, chip):
    return _join_halves("rs_join", [_sum_chips("rs_sum_%d" % i, p, b, chip, c) for i, (p, b) in enumerate(zip(ps, b2))])


def _reduce_scatter_multi(slots, c, chip):
    ps = _rs_begin(slots, c)
    return _rs_end(ps, _scatter_chips_multi("rs_scatter", ps), c, chip)


_BIG = ("w_in", "w_dn_out", "w_swa_out", "w_o", "w_up", "w_down")


def _assemble_weights(gs):
    cat1 = lambda g: jnp.concatenate([g[s] for s in range(4)], axis=1)
    rows = lambda g: g.reshape(4 * g.shape[1], g.shape[2])
    return {"w_in": _regroup_w_in(cat1(gs[0])), "w_dn_out": rows(gs[1]), "w_swa_out": rows(gs[2]), "w_o": rows(gs[3]),
            "w_up": cat1(gs[4]), "w_down": rows(gs[5])}


def _grad_slots(gw):
    cols = lambda g: g.reshape(g.shape[0], 4, g.shape[1] // 4).transpose(1, 0, 2)
    rows = lambda g: g.reshape(4, g.shape[0] // 4, g.shape[1])
    return [cols(_ungroup_w_in(gw["w_in"])), rows(gw["w_dn_out"]), rows(gw["w_swa_out"]), rows(gw["w_o"]), gw["w_up"], rows(gw["w_down"])]


_SHARDED = (
    ("w_in", (D, 1860), 1, False),
    ("dn_conv", (4, 768), 1, True),
    ("w_dn_out", (256, D), 0, False),
    ("w_swa_out", (256, D), 0, False),
    ("w_o", (256, D), 0, False),
    ("w_up", (D, 1408), 1, False),
    ("ffn_conv", (3, 704), 1, True),
    ("w_down", (704, D), 0, False),
)


def _pack_weights(shards):
    parts = []
    for nm, shp, _, as_bits in _SHARDED:
        a = shards[nm]
        parts.append(lax.bitcast_convert_type(a, bf16).reshape(-1) if as_bits else a.astype(bf16).reshape(-1))
    flat = jnp.concatenate(parts)
    return jnp.pad(flat, (0, COMM_ROWS * COMM_W - flat.shape[0])).reshape(COMM_ROWS, COMM_W)


def _unpack_weights(g):
    flat = g.reshape(4, -1)
    out, off = {}, 0
    for nm, shp, ax, as_bits in _SHARDED:
        n = int(np.prod(shp)) * (2 if as_bits else 1)
        piece = flat[:, off:off + n]
        off += n
        if as_bits:
            piece = lax.bitcast_convert_type(piece.reshape((4,) + shp + (2,)), f32)
        else:
            piece = piece.reshape((4,) + shp)
        out[nm] = jnp.concatenate([piece[s] for s in range(4)], axis=ax)
    return out


def _pack_grads(grads):
    slots = []
    for s in range(4):
        parts = []
        for nm, shp, ax, _ in _SHARDED:
            n = shp[ax]
            parts.append(lax.slice_in_dim(grads[nm], s * n, (s + 1) * n, axis=ax).reshape(-1))
        flat = jnp.concatenate(parts)
        slots.append(jnp.pad(flat, (0, COMM_ROWS * COMM_W - flat.shape[0])))
    return jnp.stack(slots).reshape(4, COMM_ROWS, COMM_W)


def _unpack_grads(r):
    flat = r.reshape(-1)
    out, off = {}, 0
    for nm, shp, _, _ in _SHARDED:
        n = int(np.prod(shp))
        out[nm] = flat[off:off + n].reshape(shp)
        off += n
    return out


def _regroup_w_in(w):
    z = lambda n: jnp.zeros((w.shape[0], n), w.dtype)
    return jnp.concatenate([w[:, 0:3072], w[:, 3072:4096], w[:, 4112:5136], w[:, 5392:6416], w[:, 6416:7440],
                            w[:, 4096:4112], z(112), w[:, 5136:5264], w[:, 5264:5392], z(128)], axis=1)


def _ungroup_w_in(g):
    return jnp.concatenate([g[:, 0:3072], g[:, 3072:4096], g[:, 7168:7184], g[:, 4096:5120], g[:, 7296:7424], g[:, 7424:7552],
                            g[:, 5120:6144], g[:, 6144:7168]], axis=1)


def _pad_lanes(v, n=LANE):
    return jnp.pad(v, (0, n - v.shape[0])).reshape(1, n)


def _layer_consts(P):
    K = {}
    K["norm_mix"] = P["norm_mix"].reshape(1, D)
    K["norm_ffn"] = P["norm_ffn"].reshape(1, D)
    K["alog"] = _pad_lanes(P["dn_a_log"])
    K["dtb"] = _pad_lanes(P["dn_dt_bias"])
    K["dn_norm"] = P["dn_norm"].reshape(1, LANE)
    K["qn"] = jnp.tile(P["swa_q_norm"], 16).reshape(1, D)
    K["kn"] = jnp.tile(P["swa_k_norm"], 2).reshape(1, LANE)
    K["sinks"] = _pad_lanes(P["swa_sinks"])
    K["ffn_b"] = P["ffn_conv_b"].reshape(1, D_FF)
    return K


def _layer_fwd(x, mod, W, K, tabs, bd, hosted=None):
    sh1, sc1, gt1, sh2, sc2, gt2 = mod
    S = {"x": x}
    (h1,) = _rowwise_fwd("normmod1_fwd", _normmod_fn, [(x, 0, D)], [K["norm_mix"], sc1, sh1], [D], [bf16])
    proj = _matmul("proj_fwd", h1, W["w_in"], "nn", f32)
    qn = _dnconv_fwd("dnconv_q_fwd", proj, CB_Q, W["dn_conv"], True)
    kn = _dnconv_fwd("dnconv_k_fwd", proj, CB_K, W["dn_conv"], True)
    vc = _dnconv_fwd("dnconv_v_fwd", proj, CB_V, W["dn_conv"], False)
    o, sall, xinv, hosted_out = _gdn_fwd("gdn_fwd", qn, kn, vc, proj, K["alog"], K["dtb"], hosted=hosted)
    (on,) = _rowwise_fwd("dngate_fwd", _dngate_fn, [(o, 0, LANE), (proj, 8 * WB_Z, LANE)], [K["dn_norm"]], [LANE], [bf16], nc=8,
                         tr=_pick(x.shape[0], (1024,)))
    ya = _matmul("dnout_fwd", on, W["w_dn_out"], "nn", f32)
    sq = _qkprep_fwd("qprep_fwd", proj, WB_SWQ, D, K["qn"], bd[0], tabs[0])
    sk = _qkprep_fwd("kprep_fwd", proj, CB_SWK, LANE, K["kn"], bd[1], tabs[1])
    attn = _attn_fwd("attn_fwd", sq, sk, proj, K["sinks"])
    yb = _matmul("swaout_fwd", attn, W["w_swa_out"], "nn", f32)
    (merged,) = _rowwise_fwd("merge_fwd", _merge_fn, [(proj, WB_GA, D), (proj, WB_GB, D), (ya, 0, D), (yb, 0, D)], [], [D], [bf16])
    t1 = _matmul("wo_fwd", merged, W["w_o"], "nn", f32)
    (x1,) = _rowwise_fwd("resid1_fwd", _resid_fn, [(x, 0, D), (t1, 0, D)], [gt1], [D], [f32])
    (h2,) = _rowwise_fwd("normmod2_fwd", _normmod_fn, [(x1, 0, D)], [K["norm_ffn"], sc2, sh2], [D], [bf16])
    up = _matmul("up_fwd", h2, W["w_up"], "nn", f32)
    mid = _ffnact_fwd("ffnact_fwd", up, W["ffn_conv"], K["ffn_b"])
    t2 = _matmul("down_fwd", mid, W["w_down"], "nn", f32)
    (x2,) = _rowwise_fwd("resid2_fwd", _resid_fn, [(x1, 0, D), (t2, 0, D)], [gt2], [D], [f32])
    S.update(h1=h1, proj=proj, qn=qn, kn=kn, vc=vc, o=o, sall=sall, xinv=xinv, on=on, ya=ya, sq=sq, sk=sk, attn=attn, yb=yb,
             merged=merged, t1=t1, x1=x1, h2=h2, up=up, mid=mid, t2=t2)
    return x2, S, hosted_out


def _layer_bwd(dx2, S, mod, W, K, tabs, bd, carry=None):
    sh1, sc1, gt1, sh2, sc2, gt2 = mod
    x, x1, proj, up = S["x"], S["x1"], S["proj"], S["up"]
    T = x.shape[0]
    gw, gs = {}, {}
    dt2, dgt2 = _rowwise_bwd("resid2_bwd", _resid_fn, [(x1, 0, D), (S["t2"], 0, D)], [gt2], [(dx2, 0, D)], [None, bf16])
    dmid = _matmul("down_bwd_x", dt2, W["w_down"], "nt", bf16)
    gw["w_down"] = _matmul("down_bwd_w", S["mid"], dt2, "tn", f32)
    dact, dlin, gw["ffn_conv"], dffn_b = _ffnact_bwd("ffnact_bwd", up, W["ffn_conv"], K["ffn_b"], dmid)
    dup = jnp.concatenate([dact, dlin], axis=1)
    dh2 = _matmul("up_bwd_x", dup, W["w_up"], "nt", f32)
    if carry is None:
        gw["w_up"] = _matmul("up_bwd_w", S["h2"], dup, "tn", f32, out_slots=4)
        pair_sums = hosted = None
    else:
        gw["w_up"], b1 = _matmul("up_bwd_w", S["h2"], dup, "tn", f32, out_slots=4, hosted=_hosted_swap(carry[0]))
        pair_sums = [_pair_add("rs_pair_%d" % i, g, b, carry[1]) for i, (g, b) in enumerate(zip(carry[0], b1))]
        hosted = _hosted_scatter(pair_sums)
    dx1, dnorm_ffn, dsc2, dsh2 = _rowwise_bwd("normmod2_bwd", _normmod_fn, [(x1, 0, D)], [K["norm_ffn"], sc2, sh2], [(dh2, 0, D)], [f32],
                                              add_to_first=(dx2, 0, D))
    dt1, dgt1 = _rowwise_bwd("resid1_bwd", _resid_fn, [(x, 0, D), (S["t1"], 0, D)], [gt1], [(dx1, 0, D)], [None, bf16])
    dmerged = _matmul("wo_bwd_x", dt1, W["w_o"], "nt", f32)
    gw["w_o"] = _matmul("wo_bwd_w", S["merged"], dt1, "tn", f32)
    dga, dgb, dya, dyb = _rowwise_bwd("merge_bwd", _merge_fn, [(proj, WB_GA, D), (proj, WB_GB, D), (S["ya"], 0, D), (S["yb"], 0, D)], [],
                                      [(dmerged, 0, D)], [bf16, bf16, bf16, bf16])
    don = _matmul("dnout_bwd_x", dya, W["w_dn_out"], "nt", f32)
    gw["w_dn_out"] = _matmul("dnout_bwd_w", S["on"], dya, "tn", f32)
    do, dz, ddn_norm = _rowwise_bwd("dngate_bwd", _dngate_fn, [(S["o"], 0, LANE), (proj, 8 * WB_Z, LANE)], [K["dn_norm"]], [(don, 0, LANE)],
                                    [f32, bf16], nc=8, tr=_pick(T, (1024,)))
    (dqn, dkn, dvc, dab, dalog, ddtb), hosted_out = _gdn_bwd("gdn_bwd", S["qn"], S["kn"], S["vc"], proj, K["alog"], K["dtb"], S["sall"], S["xinv"], do,
                                                            hosted=hosted)
    dpq, dwq = _dnconv_bwd("dnconv_q_bwd", proj, CB_Q, W["dn_conv"], dqn, True)
    dpk, dwk = _dnconv_bwd("dnconv_k_bwd", proj, CB_K, W["dn_conv"], dkn, True)
    dpv, dwv = _dnconv_bwd("dnconv_v_bwd", proj, CB_V, W["dn_conv"], dvc, False)
    gw["dn_conv"] = jnp.concatenate([dwq, dwk, dwv], axis=1)
    dattn = _matmul("swaout_bwd_x", dyb, W["w_swa_out"], "nt", f32)
    gw["w_swa_out"] = _matmul("swaout_bwd_w", S["attn"], dyb, "tn", f32)
    dsq, dkp, dkc, dvp, dvc_, dsinks = _attn_bwd("attn_bwd", S["sq"], S["sk"], proj, K["sinks"], dattn)
    dsk = _shift_add("attn_dk_join", dkc, dkp, f32)
    dswv = _shift_add("attn_dv_join", dvc_, dvp, bf16)
    dswq, dqn_w = _qkprep_bwd("qprep_bwd", proj, WB_SWQ, D, K["qn"], bd[0], tabs[0], dsq)
    dswk, dkn_w = _qkprep_bwd("kprep_bwd", proj, CB_SWK, LANE, K["kn"], bd[1], tabs[1], dsk)
    dproj = jnp.concatenate([dpq, dpk, dpv, dz, dswq, dga, dgb, dab.astype(bf16), dswk, dswv, jnp.zeros((T, LANE), bf16)], axis=1)
    dh1 = _matmul("proj_bwd_x", dproj, W["w_in"], "nt", f32)
    gw["w_in"] = _matmul("proj_bwd_w", S["h1"], dproj, "tn", f32)
    dx, dnorm_mix, dsc1, dsh1 = _rowwise_bwd("normmod1_bwd", _normmod_fn, [(x, 0, D)], [K["norm_mix"], sc1, sh1], [(dh1, 0, D)], [f32],
                                             add_to_first=(dx1, 0, D))
    gs = {"norm_mix": dnorm_mix[0], "dn_a_log": dalog[0, :8], "dn_dt_bias": ddtb[0, :8], "dn_norm": ddn_norm[0],
          "swa_q_norm": dqn_w.reshape(16, 64).sum(0), "swa_k_norm": dkn_w.reshape(2, 64).sum(0), "swa_sinks": dsinks[0, :16],
          "norm_ffn": dnorm_ffn[0], "ffn_conv_b": dffn_b[0]}
    dmod = jnp.concatenate([dsh1, dsc1, dgt1, dsh2, dsc2, dgt2], axis=1)
    return dx, gw, gs, dmod, (pair_sums, hosted_out)


def _rope_tables(pos):
    T = pos.shape[0]
    half = 8
    inv = jnp.power(ROPE_THETA, -jnp.arange(half, dtype=f32) / half)
    ang = pos.astype(f32)[:, None] * inv
    cos, sin = jnp.cos(ang), jnp.sin(ang)
    z8, z48, o48 = jnp.zeros((T, 8), f32), jnp.zeros((T, 48), f32), jnp.ones((T, 48), f32)
    c64 = jnp.concatenate([cos, cos, o48], axis=1)
    s1 = jnp.concatenate([-sin, z8, z48], axis=1)
    s2 = jnp.concatenate([z8, sin, z48], axis=1)
    return tuple(jnp.tile(t, (1, 16)) for t in (c64, s1, s2))


_SMALL = (("norm_mix", D), ("dn_a_log", 8), ("dn_dt_bias", 8), ("dn_norm", 128), ("swa_q_norm", 64), ("swa_k_norm", 64),
          ("swa_sinks", 16), ("norm_ffn", D), ("ffn_conv_b", D_FF), ("b_ada", 6 * D))
_CONV = (("dn_conv", 4 * 3072), ("ffn_conv", 3 * D_FF))
_CONV_SHARD = (("dn_conv", 4 * 768), ("ffn_conv", 3 * 704))


def _pack_small(vals, spec):
    flat = jnp.concatenate([vals[nm].reshape(-1) for nm, _ in spec])
    rows = -(-flat.shape[0] // (8 * LANE)) * 8
    return jnp.pad(flat, (0, rows * LANE - flat.shape[0])).reshape(rows, LANE)


def _unpack_small(buf, spec):
    flat = buf.reshape(-1)
    out, off = {}, 0
    for nm, n in spec:
        out[nm] = flat[off:off + DEPTH * n].reshape(DEPTH, n)
        off += DEPTH * n
    return out


def kernel(x, c, positions, w_ada, b_ada, norm_mix, w_in, dn_conv, dn_a_log, dn_dt_bias, dn_norm, w_dn_out, swa_q_norm, swa_k_norm, swa_sinks, w_swa_out, w_o, norm_ffn, w_up, ffn_conv, ffn_conv_b, w_down, loss_target, m_w_ada, m_b_ada, m_norm_mix, m_w_in, m_dn_conv, m_dn_a_log, m_dn_dt_bias, m_dn_norm, m_w_dn_out, m_swa_q_norm, m_swa_k_norm, m_swa_sinks, m_w_swa_out, m_w_o, m_norm_ffn, m_w_up, m_ffn_conv, m_ffn_conv_b, m_w_down, v_w_ada, v_b_ada, v_norm_mix, v_w_in, v_dn_conv, v_dn_a_log, v_dn_dt_bias, v_dn_norm, v_w_dn_out, v_swa_q_norm, v_swa_k_norm, v_swa_sinks, v_w_swa_out, v_w_o, v_norm_ffn, v_w_up, v_ffn_conv, v_ffn_conv_b, v_w_down):
    weights = dict(w_ada=w_ada, b_ada=b_ada, norm_mix=norm_mix, w_in=w_in, dn_conv=dn_conv, dn_a_log=dn_a_log, dn_dt_bias=dn_dt_bias,
                   dn_norm=dn_norm, w_dn_out=w_dn_out, swa_q_norm=swa_q_norm, swa_k_norm=swa_k_norm, swa_sinks=swa_sinks,
                   w_swa_out=w_swa_out, w_o=w_o, norm_ffn=norm_ffn, w_up=w_up, ffn_conv=ffn_conv, ffn_conv_b=ffn_conv_b, w_down=w_down)
    mom_m = dict(w_ada=m_w_ada, b_ada=m_b_ada, norm_mix=m_norm_mix, w_in=m_w_in, dn_conv=m_dn_conv, dn_a_log=m_dn_a_log,
                 dn_dt_bias=m_dn_dt_bias, dn_norm=m_dn_norm, w_dn_out=m_w_dn_out, swa_q_norm=m_swa_q_norm, swa_k_norm=m_swa_k_norm,
                 swa_sinks=m_swa_sinks, w_swa_out=m_w_swa_out, w_o=m_w_o, norm_ffn=m_norm_ffn, w_up=m_w_up, ffn_conv=m_ffn_conv,
                 ffn_conv_b=m_ffn_conv_b, w_down=m_w_down)
    mom_v = dict(w_ada=v_w_ada, b_ada=v_b_ada, norm_mix=v_norm_mix, w_in=v_w_in, dn_conv=v_dn_conv, dn_a_log=v_dn_a_log,
                 dn_dt_bias=v_dn_dt_bias, dn_norm=v_dn_norm, w_dn_out=v_w_dn_out, swa_q_norm=v_swa_q_norm, swa_k_norm=v_swa_k_norm,
                 swa_sinks=v_swa_sinks, w_swa_out=v_w_swa_out, w_o=v_w_o, norm_ffn=v_norm_ffn, w_up=v_w_up, ffn_conv=v_ffn_conv,
                 ffn_conv_b=v_ffn_conv_b, w_down=v_w_down)
    order = ["w_ada", "b_ada", "norm_mix", "w_in", "dn_conv", "dn_a_log", "dn_dt_bias", "dn_norm", "w_dn_out", "swa_q_norm",
             "swa_k_norm", "swa_sinks", "w_swa_out", "w_o", "norm_ffn", "w_up", "ffn_conv", "ffn_conv_b", "w_down"]
    ax, ay, ac = lax.axis_index("x"), lax.axis_index("y"), lax.axis_index("c")
    chip = 2 * ax + ay
    dev = 4 * ax + 2 * ay + ac
    T = x.shape[1]
    xs = x[0]

    c_all = _allgather_all("gather_c", jnp.pad(c, ((0, 7), (0, 0)))).reshape(8, 8, D)[:, 0]
    c_act = _silu_rows("silu_c", jnp.pad(c_all, ((0, 8), (0, 0))))
    mod_sh = jnp.stack([
        _matmul("mod_fwd", c_act, w_ada[l].astype(bf16), "nn", f32,
                bias=lax.dynamic_slice(b_ada[l], (chip * 1536,), (1536,)).reshape(1, 1536)) for l in range(DEPTH)])
    mod_all = _allgather_all("gather_mod", mod_sh.reshape(DEPTH * 16 * 12, LANE)).reshape(8, DEPTH, 16, 1536)
    mod_me = jnp.concatenate([lax.dynamic_index_in_dim(mod_all[2 * s], dev, axis=1, keepdims=False) for s in range(4)], axis=1)

    tabs_q = _rope_tables(positions[0])
    tabs = (tabs_q, tuple(t[:, :LANE] for t in tabs_q))
    head = jnp.arange(LANE) // 64
    bd128 = (head[:, None] == head[None, :]).astype(f32) / 64.0
    bd = (bd128, bd128)

    conv_all = _allgather_all("gather_conv", _pack_small({"dn_conv": dn_conv, "ffn_conv": ffn_conv}, _CONV_SHARD))
    conv_parts = [_unpack_small(conv_all[2 * s], _CONV_SHARD) for s in range(4)]
    dn_conv_full = jnp.concatenate([p["dn_conv"].reshape(DEPTH, 4, 768) for p in conv_parts], axis=2)
    ffn_conv_full = jnp.concatenate([p["ffn_conv"].reshape(DEPTH, 3, 704) for p in conv_parts], axis=2)

    saved, Ws, Ks, mods = [], [], [], []
    h = xs
    shards = [[weights[nm][l].astype(bf16) for nm in _BIG] for l in range(DEPTH)]
    gathered = _gather_chips("gather_w", shards[0])
    for l in range(DEPTH):
        gathered = [lax.dynamic_update_index_in_dim(g, s, chip, 0) for g, s in zip(gathered, shards[l])]
        W = _assemble_weights(gathered)
        W["dn_conv"], W["ffn_conv"] = dn_conv_full[l], ffn_conv_full[l]
        K = _layer_consts({nm: weights[nm][l] for nm in ("norm_mix", "norm_ffn", "dn_a_log", "dn_dt_bias", "dn_norm", "swa_q_norm",
                                                          "swa_k_norm", "swa_sinks", "ffn_conv_b")})
        mod = tuple(mod_me[l, k * D:(k + 1) * D].reshape(1, D) for k in range(6))
        nxt = _hosted_gather(shards[l + 1]) if l + 1 < DEPTH else None
        h, S, arrived = _layer_fwd(h, mod, W, K, tabs, bd, hosted=nxt)
        if nxt is not None:
            gathered = _gather_forward("gather_w_pass", arrived)
        saved.append(S), Ws.append(W), Ks.append(K), mods.append(mod)

    loss_blk, dh = _loss("loss", h, loss_target[0])
    loss = lax.psum(loss_blk[0, 0], ("x", "y", "c"))

    grad_sh = [None] * DEPTH
    small = [None] * DEPTH
    dmods = [None] * DEPTH
    slots = None
    for l in reversed(range(DEPTH)):
        dh, gw, gs, dmod, (ps, b2) = _layer_bwd(dh, saved[l], mods[l], Ws[l], Ks[l], tabs, bd, carry=None if slots is None else (slots, ac))
        if slots is not None:
            grad_sh[l + 1] = dict(zip(_BIG, _rs_end(ps, b2, ac, chip)))
        slots = _grad_slots(gw)
        small[l], dmods[l] = dict(gs, dn_conv=gw["dn_conv"], ffn_conv=gw["ffn_conv"]), dmod[0]
    grad_sh[0] = dict(zip(_BIG, _reduce_scatter_multi(slots, ac, chip)))

    spec_g = _SMALL + _CONV
    vals = {nm: jnp.stack([small[l][nm] for l in range(DEPTH)]) for nm, _ in spec_g if nm != "b_ada"}
    vals["b_ada"] = jnp.stack(dmods)
    small_all = _allgather_all("gather_small", _pack_small(vals, spec_g))
    g_small = _unpack_small(_sum_leading("sum_small", small_all), spec_g)
    dmod_all = jnp.stack([_unpack_small(small_all[d], spec_g)["b_ada"] for d in range(8)])
    dmod_sh = lax.dynamic_slice(dmod_all, (0, 0, chip * 1536), (8, DEPTH, 1536))
    dmod_sh = jnp.pad(dmod_sh, ((0, 8), (0, 0), (0, 0))).astype(bf16)
    g_w_ada = jnp.stack([_matmul("mod_bwd_w", c_act, dmod_sh[:, l], "tn", f32) for l in range(DEPTH)])

    grads = {nm: g_small[nm] for nm, _ in _SMALL}
    grads["dn_conv"] = lax.dynamic_slice(g_small["dn_conv"].reshape(DEPTH, 4, 3072), (0, 0, chip * 768), (DEPTH, 4, 768))
    grads["ffn_conv"] = lax.dynamic_slice(g_small["ffn_conv"].reshape(DEPTH, 3, D_FF), (0, 0, chip * 704), (DEPTH, 3, 704))
    grads["w_ada"] = g_w_ada
    for nm in _BIG:
        grads[nm] = jnp.stack([grad_sh[l][nm] for l in range(DEPTH)])

    delta, new_m, new_v = {}, {}, {}
    for nm in ("w_ada", "dn_conv", "ffn_conv") + _BIG:
        delta[nm], new_m[nm], new_v[nm] = _adamw("adamw_" + nm, weights[nm], grads[nm], mom_m[nm], mom_v[nm])
    sm = [_pack_small({nm: t[nm] for nm, _ in _SMALL}, _SMALL) for t in (weights, grads, mom_m, mom_v)]
    for tgt, buf in zip((delta, new_m, new_v), _adamw("adamw_small", *sm)):
        tgt.update(_unpack_small(buf, _SMALL))

    return (loss, dh[None], *[grads[n] for n in order], *[delta[n] for n in order], *[new_m[n] for n in order], *[new_v[n] for n in order])
```

```python
import functools

import jax
import jax.numpy as jnp
import numpy as np
from jax import lax
from jax.experimental import pallas as pl
from jax.experimental.pallas import tpu as pltpu

f32 = jnp.float32
bf16 = jnp.bfloat16
SDS = jax.ShapeDtypeStruct
HI = lax.Precision.HIGHEST
MESH = pl.DeviceIdType.MESH

D = 1024
DEPTH = 4
EPS = 1e-6
DN_C = 64
SWA_B = 128
LANE = 128
ROPE_THETA = 500000.0
D_FF = 2816
IN_TOTAL = 7440
PROJ_W = 7680
CB_Q, CB_K, CB_V = 0, 8, 16
CB_AB, CB_SWK, CB_SWV = 56, 57, 58
WB_Z, WB_GA, WB_GB, WB_SWQ = 3, 4, 5, 6
TR = 256
COMM_W = 1024
COMM_ROWS = 4864
COMM_TR = 128
VMEM_BIG = 48 * 2 ** 20

ADAM_LR, ADAM_B1, ADAM_B2, ADAM_EPS, ADAM_WD, ADAM_STEP = 0.001, 0.9, 0.999, 1e-08, 0.01, 10


def _pcall(body, **kw):
    return pl.pallas_call(body, **kw)


def _cparams(vmem=None):
    return pltpu.CompilerParams(vmem_limit_bytes=vmem) if vmem else None


def _dot(a, b, ca, cb, precision=HI):
    return lax.dot_general(a, b, (((ca,), (cb,)), ((), ())), precision=precision, preferred_element_type=f32)


def _pick(n, cands):
    for c in cands:
        if n % c == 0:
            return c
    return n


def _tile(n, cap):
    if n <= cap:
        return n
    best = None
    for t in range(LANE, cap + 1, LANE):
        if n % t == 0:
            best = t
    assert best is not None, (n, cap)
    return best


def _matmul(name, a, b, mode, out_dtype, bias=None, out_slots=None, hosted=None):
    h_args, h_specs, h_shapes, h_sems = _hosted_parts(hosted)
    nh = len(h_args)
    if mode == "nn":
        (M, K), (_, N) = a.shape, b.shape
    elif mode == "nt":
        (M, K), (N, _) = a.shape, b.shape
    else:
        (K, M), (_, N) = a.shape, b.shape
    tm = _tile(M, 1536 if mode == "tn" else 1024)
    tn = N // out_slots if out_slots else _tile(N, 1536)
    tk = _tile(K, 512 if mode == "tn" else (1024 if K <= 1024 else 1536))
    nk = K // tk
    ca, cb = {"nn": (1, 0), "nt": (1, 1), "tn": (0, 0)}[mode]

    grid = (M // tm, N // tn, nk)

    def body(*refs):
        a_ref, b_ref = refs[:2]
        nb = 1 if bias is not None else 0
        bias_ref = refs[2] if nb else None
        h_ins, o_ref, h_outs = refs[2 + nb:2 + nb + nh], refs[2 + nb + nh], refs[3 + nb + nh:3 + nb + 2 * nh]
        rest = refs[3 + nb + 2 * nh:]
        if hosted is not None:
            step = (pl.program_id(0) * grid[1] + pl.program_id(1)) * grid[2] + pl.program_id(2)
            sems = rest[1:] if nk > 1 else rest

            @pl.when(step == 0)
            def _():
                hosted["start"](h_ins, h_outs, *sems)

        def finish(r):
            if bias is not None:
                r = r + bias_ref[...]
            o_ref[...] = r.astype(o_ref.dtype)

        part = _dot(a_ref[...].astype(bf16), b_ref[...].astype(bf16), ca, cb, precision=None)
        if nk == 1:
            finish(part)
        else:
            acc = rest[0]
            k = pl.program_id(2)

            @pl.when(k == 0)
            def _():
                acc[...] = part

            @pl.when(k > 0)
            def _():
                acc[...] += part

            @pl.when(k == nk - 1)
            def _():
                finish(acc[...])

        if hosted is not None:
            @pl.when(step == grid[0] * grid[1] * grid[2] - 1)
            def _():
                hosted["finish"](h_ins, h_outs, *sems)

    a_spec =pl.BlockSpec((tk, tm), lambda i, j, k: (k, i)) if mode == "tn" else pl.BlockSpec((tm, tk), lambda i, j, k: (i, k))
    b_spec = pl.BlockSpec((tn, tk), lambda i, j, k: (j, k)) if mode == "nt" else pl.BlockSpec((tk, tn), lambda i, j, k: (k, j))
    in_specs = [a_spec, b_spec]
    args = [a, b]
    if bias is not None:
        in_specs.append(pl.BlockSpec((1, tn), lambda i, j, k: (0, j)))
        args.append(bias)
    if out_slots:
        out_spec = pl.BlockSpec((None, tm, tn), lambda i, j, k: (j, i, 0))
        out_shape = SDS((out_slots, M, tn), out_dtype)
    else:
        out_spec = pl.BlockSpec((tm, tn), lambda i, j, k: (i, j))
        out_shape = SDS((M, N), out_dtype)
    if hosted is None:
        return _pcall(
            body, grid=grid, in_specs=in_specs, out_specs=out_spec, out_shape=out_shape,
            scratch_shapes=[pltpu.VMEM((tm, tn), f32)] if nk > 1 else [], compiler_params=_cparams(VMEM_BIG), name=name)(*args)
    outs = _pcall(
        body, grid=grid, in_specs=in_specs + h_specs, out_specs=[out_spec] + h_specs, out_shape=[out_shape] + h_shapes,
        scratch_shapes=([pltpu.VMEM((tm, tn), f32)] if nk > 1 else []) + h_sems, compiler_params=_cparams(VMEM_BIG), name=name)(*args, *h_args)
    return outs[0], list(outs[1:])


def _row_specs(rows, tr):
    return [pl.BlockSpec((tr, w), lambda i, j, off=off: (i, off + j)) for (_, off, w) in rows]


def _rowwise_fwd(name, fn, rows, vecs, out_widths, out_dtypes, nc=1, tr=TR, also_transposed=False):
    T = rows[0][0].shape[0]
    n_in = len(rows) + len(vecs)
    n_out = len(out_widths)

    def body(*refs):
        vals = [r[...].astype(f32) for r in refs[:n_in]]
        res = fn(*vals)
        for o_ref, r in zip(refs[n_in:n_in + n_out], res):
            o_ref[...] = r.astype(o_ref.dtype)
        if also_transposed:
            refs[n_in + n_out][...] = res[0].T.astype(refs[n_in + n_out].dtype)

    in_specs = _row_specs(rows, tr) + [pl.BlockSpec(v.shape, lambda i, j: (0, 0)) for v in vecs]
    out_specs = [pl.BlockSpec((tr, w), lambda i, j: (i, j)) for w in out_widths]
    out_shape = [SDS((T, w * nc), dt) for w, dt in zip(out_widths, out_dtypes)]
    if also_transposed:
        out_specs.append(pl.BlockSpec((out_widths[0], tr), lambda i, j: (j, i)))
        out_shape.append(SDS((out_widths[0] * nc, T), out_dtypes[0]))
    return _pcall(body, grid=(T // tr, nc), in_specs=in_specs, out_specs=out_specs, out_shape=out_shape, name=name)(
        *[r[0] for r in rows], *vecs)


def _into_buffer(dest_buf, n_inputs, out_index):
    if dest_buf is None:
        return [], [], {}
    return [dest_buf], [pl.BlockSpec(memory_space=pl.ANY)], {n_inputs: out_index}


def _rowwise_bwd(name, fn, rows, vecs, cts, drow_dtypes, nc=1, tr=TR, add_to_first=None, dest=None):
    T = rows[0][0].shape[0]
    n_r, n_v, n_c = len(rows), len(vecs), len(cts)
    n_add = 0 if add_to_first is None else 1
    keep = [k for k, dt in enumerate(drow_dtypes) if dt is not None]
    members = [] if dest is None else list(dest[1])
    plain = [pos for pos in range(len(keep)) if pos not in members]
    n_dest = 1 if dest is not None else 0
    n_in = n_r + n_v + n_c + n_add

    def body(*refs):
        vals = [r[...].astype(f32) for r in refs[:n_in]]
        outs = refs[n_in + (1 if dest is not None and dest[0] is not None else 0):]
        i, j = pl.program_id(0), pl.program_id(1)
        _, vjp = jax.vjp(fn, *vals[:n_r + n_v])
        grads = vjp(tuple(vals[n_r + n_v:n_r + n_v + n_c]))
        got = []
        for pos, k in enumerate(keep):
            g = grads[k]
            if n_add and pos == 0:
                g = g + vals[n_in - 1]
            got.append(g)
        if dest is not None:
            outs[0][...] = jnp.concatenate([got[pos] for pos in members], axis=1).astype(outs[0].dtype)
        for q, pos in enumerate(plain):
            outs[n_dest + q][...] = got[pos].astype(outs[n_dest + q].dtype)
        vec_outs = outs[n_dest + len(plain):]

        @pl.when((i == 0) & (j == 0))
        def _():
            for q in range(n_v):
                vec_outs[q][...] = jnp.zeros_like(vec_outs[q])

        for q in range(n_v):
            vec_outs[q][...] += grads[n_r + q]

    extra = [] if add_to_first is None else [add_to_first]
    in_specs = (_row_specs(rows, tr) + [pl.BlockSpec(v.shape, lambda i, j: (0, 0)) for v in vecs]
                + _row_specs(cts, tr) + _row_specs(extra, tr))
    args = [r[0] for r in rows] + list(vecs) + [c[0] for c in cts] + [e[0] for e in extra]
    out_specs, out_shape, aliases = [], [], {}
    if dest is not None:
        width = sum(rows[keep[pos]][2] for pos in members)
        col = dest[2]
        b_args, b_specs, aliases = _into_buffer(dest[0], n_in, 0)
        args, in_specs = args + b_args, in_specs + b_specs
        out_specs.append(pl.BlockSpec((tr, width), lambda i, j: (i, col + j)))
        out_shape.append(SDS((T, PROJ_W), bf16))
    out_specs += [pl.BlockSpec((tr, rows[keep[pos]][2]), lambda i, j: (i, j)) for pos in plain]
    out_shape += [SDS((T, rows[keep[pos]][2] * nc), drow_dtypes[keep[pos]]) for pos in plain]
    out_specs += [pl.BlockSpec(v.shape, lambda i, j: (0, 0)) for v in vecs]
    out_shape += [SDS(v.shape, f32) for v in vecs]
    return _pcall(body, grid=(T // tr, nc), in_specs=in_specs, out_specs=out_specs, out_shape=out_shape,
                  input_output_aliases=aliases, name=name)(*args)


def _normmod_fn(x, w, sc, sh):
    y = x * lax.rsqrt(jnp.mean(x * x, axis=-1, keepdims=True) + EPS)
    return ((y * w) * (1.0 + sc) + sh,)


def _resid_fn(x, t, gt):
    return (x + gt * t,)


def _merge_fn(ga, gb, ya, yb):
    return (jax.nn.sigmoid(ga) * ya + jax.nn.sigmoid(gb) * yb,)


def _dngate_fn(o, z, w):
    y = o * lax.rsqrt(jnp.mean(o * o, axis=-1, keepdims=True) + EPS)
    return ((y * w) * (z * jax.nn.sigmoid(z)),)


def _conv_taps(x, w_ref, taps, buf):
    w = lambda s: w_ref[taps - 1 - s:taps - s, :]
    row = lax.broadcasted_iota(jnp.int32, (8, x.shape[1]), 0)
    x8 = x[0:8]
    acc, acc8 = x * w(0), x8 * w(0)
    for s in range(1, taps):
        acc = acc + pltpu.roll(x, s, 0) * w(s)
        acc8 = acc8 + jnp.where(row >= s, pltpu.roll(x8, s, 0), 0.0) * w(s)
    buf[...] = acc
    buf[0:8, :] = acc8
    return buf[...]


def _conv_taps_bwd(x, dy, w_ref, dw_ref, taps, buf):
    T = x.shape[0]
    w = lambda s: w_ref[taps - 1 - s:taps - s, :]
    row = lax.broadcasted_iota(jnp.int32, (8, x.shape[1]), 0)
    dy_first, dy_last = dy[0:8], dy[T - 8:T]
    dx, dx_last = dy * w(0), dy_last * w(0)
    dw_ref[taps - 1:taps, :] = jnp.sum(dy * x, axis=0, keepdims=True)
    for s in range(1, taps):
        dx = dx + pltpu.roll(dy, T - s, 0) * w(s)
        dx_last = dx_last + jnp.where(row < 8 - s, pltpu.roll(dy_last, 8 - s, 0), 0.0) * w(s)
        xr = pltpu.roll(x, s, 0)
        wrapped = jnp.sum(jnp.where(row < s, dy_first * xr[0:8], 0.0), axis=0, keepdims=True)
        dw_ref[taps - 1 - s:taps - s, :] = jnp.sum(dy * xr, axis=0, keepdims=True) - wrapped
    buf[...] = dx
    buf[T - 8:T, :] = dx_last
    return buf[...]


def _dn_act(y, normalize):
    s = y * jax.nn.sigmoid(y)
    if normalize:
        s = s * lax.rsqrt(jnp.sum(s * s, axis=-1, keepdims=True) + EPS)
    return s


def _dnconv_fwd(name, proj, cb, w, normalize):
    T = proj.shape[0]

    def body(x_ref, w_ref, o_ref, buf):
        o_ref[...] = _dn_act(_conv_taps(x_ref[...], w_ref, 4, buf), normalize)

    return _pcall(
        body, grid=(8,), in_specs=[pl.BlockSpec((T, LANE), lambda j: (0, cb + j)), pl.BlockSpec((4, LANE), lambda j: (0, cb + j))],
        out_specs=pl.BlockSpec((T, LANE), lambda j: (0, j)), out_shape=SDS((T, 1024), f32), scratch_shapes=[pltpu.VMEM((T, LANE), f32)],
        compiler_params=_cparams(VMEM_BIG), name=name)(proj, w)


def _dnconv_bwd(name, proj, cb, w, dout, normalize, dest_buf):
    T = proj.shape[0]
    b_args, b_specs, aliases = _into_buffer(dest_buf, 3, 0)

    def body(x_ref, w_ref, do_ref, *rest):
        dx_ref, dw_ref, buf, buf2 = rest[len(b_args):]
        x = x_ref[...]
        y = _conv_taps(x, w_ref, 4, buf)
        _, vjp = jax.vjp(functools.partial(_dn_act, normalize=normalize), y)
        (dy,) = vjp(do_ref[...])
        dx_ref[...] = _conv_taps_bwd(x, dy, w_ref, dw_ref, 4, buf2).astype(dx_ref.dtype)

    return _pcall(
        body, grid=(8,),
        in_specs=[pl.BlockSpec((T, LANE), lambda j: (0, cb + j)), pl.BlockSpec((4, LANE), lambda j: (0, cb + j)),
                  pl.BlockSpec((T, LANE), lambda j: (0, j))] + b_specs,
        out_specs=[pl.BlockSpec((T, LANE), lambda j: (0, cb + j)), pl.BlockSpec((4, LANE), lambda j: (0, j))],
        out_shape=[SDS((T, PROJ_W), bf16), SDS((4, 1024), f32)], scratch_shapes=[pltpu.VMEM((T, LANE), f32)] * 2,
        input_output_aliases=aliases, compiler_params=_cparams(VMEM_BIG), name=name)(proj, w, dout, *b_args)


def _ffn_point(a, lin):
    return a * jax.nn.sigmoid(a) * lin


def _ffnact_fwd(name, up, w, b):
    T = up.shape[0]
    nblk = D_FF // LANE

    def body(a_ref, l_ref, w_ref, b_ref, o_ref, buf):
        a = _conv_taps(a_ref[...], w_ref, 3, buf) + b_ref[...]
        o_ref[...] = _ffn_point(a, l_ref[...]).astype(o_ref.dtype)

    return _pcall(
        body, grid=(nblk,),
        in_specs=[pl.BlockSpec((T, LANE), lambda j: (0, j)), pl.BlockSpec((T, LANE), lambda j: (0, nblk + j)),
                  pl.BlockSpec((3, LANE), lambda j: (0, j)), pl.BlockSpec((1, LANE), lambda j: (0, j))],
        out_specs=pl.BlockSpec((T, LANE), lambda j: (0, j)), out_shape=SDS((T, D_FF), bf16), scratch_shapes=[pltpu.VMEM((T, LANE), f32)],
        compiler_params=_cparams(VMEM_BIG), name=name)(up, up, w, b)


def _ffnact_bwd(name, up, w, b, dmid):
    T = up.shape[0]
    nblk = D_FF // LANE

    def body(a_ref, l_ref, w_ref, b_ref, dm_ref, da_ref, dl_ref, dw_ref, db_ref, buf, buf2):
        x = a_ref[...]
        a = _conv_taps(x, w_ref, 3, buf) + b_ref[...]
        _, vjp = jax.vjp(_ffn_point, a, l_ref[...])
        da, dl = vjp(dm_ref[...].astype(f32))
        dl_ref[...] = dl.astype(dl_ref.dtype)
        db_ref[...] = jnp.sum(da, axis=0, keepdims=True)
        da_ref[...] = _conv_taps_bwd(x, da, w_ref, dw_ref, 3, buf2).astype(da_ref.dtype)

    col = lambda r: pl.BlockSpec((r, LANE), lambda j: (0, j))
    return _pcall(
        body, grid=(nblk,),
        in_specs=[col(T), pl.BlockSpec((T, LANE), lambda j: (0, nblk + j)), col(3), col(1), col(T)],
        out_specs=[col(T), col(T), col(3), col(1)],
        out_shape=[SDS((T, D_FF), bf16), SDS((T, D_FF), bf16), SDS((3, D_FF), f32), SDS((1, D_FF), f32)],
        scratch_shapes=[pltpu.VMEM((T, LANE), f32)] * 2, compiler_params=_cparams(VMEM_BIG), name=name)(up, up, w, b, dmid)


def _bmm(a, b, ca, cb, precision=HI):
    return lax.dot_general(a, b, (((ca,), (cb,)), ((0,), (0,))), precision=precision, preferred_element_type=f32)


def _make_bdot(ca, cb):
    def raw(x, y, cx, cy):
        return _bmm(x.astype(bf16), y.astype(bf16), cx, cy, precision=None)

    @jax.custom_vjp
    def f(a, b):
        return raw(a, b, ca, cb)

    def fwd(a, b):
        return raw(a, b, ca, cb), (a, b)

    def bwd(res, dy):
        a, b = res
        if (ca, cb) == (2, 1):
            return raw(dy, b, 2, 2), raw(a, dy, 1, 1)
        if (ca, cb) == (2, 2):
            return raw(dy, b, 2, 1), raw(dy, a, 1, 1)
        return raw(b, dy, 2, 2), raw(a, dy, 2, 1)

    f.defvjp(fwd, bwd)
    return f


_bdot_nn, _bdot_nt, _bdot_tn = _make_bdot(2, 1), _make_bdot(2, 2), _make_bdot(1, 1)


def _pieces(a, n):
    out, r = [], a
    for _ in range(n):
        p = r.astype(bf16)
        out.append(p)
        r = r - p.astype(f32)
    return out


def _bmm_split(x, y, cx, cy, nx=2, ny=2, order=1):
    xs, ys = _pieces(x, nx), _pieces(y, ny)
    acc = None
    for i in reversed(range(nx)):
        for j in reversed(range(ny)):
            if i + j <= order:
                t = _bmm(xs[i], ys[j], cx, cy, precision=None)
                acc = t if acc is None else acc + t
    return acc


@jax.custom_vjp
def _solve_apply(X, r):
    return _bmm_split(X, r, 2, 1)


def _solve_apply_fwd(X, r):
    return _bmm_split(X, r, 2, 1), (X, r)


def _solve_apply_bwd(res, dy):
    X, r = res
    return _bmm_split(dy, r, 2, 2), _bmm_split(X, dy, 1, 1)


_solve_apply.defvjp(_solve_apply_fwd, _solve_apply_bwd)


def _lower_ones(H, C):
    ri = lax.broadcasted_iota(jnp.int32, (H, C, C), 1)
    ci = lax.broadcasted_iota(jnp.int32, (H, C, C), 2)
    return (ri >= ci).astype(f32)


def _cumsum_rows_raw(G):
    return _bmm_split(_lower_ones(G.shape[0], G.shape[1]), G, 2, 1, nx=1, ny=3, order=2)


@jax.custom_vjp
def _cumsum_rows(G):
    return _cumsum_rows_raw(G)


def _cumsum_rows_fwd(G):
    return _cumsum_rows_raw(G), None


def _cumsum_rows_bwd(_, dy):
    return (_bmm_split(_lower_ones(dy.shape[0], dy.shape[1]), dy, 1, 1, nx=1, ny=3, order=2),)


_cumsum_rows.defvjp(_cumsum_rows_fwd, _cumsum_rows_bwd)


def _tri_inverse_raw(L):
    H, C, _ = L.shape
    ri = lax.broadcasted_iota(jnp.int32, (C, C), 0)
    ci = lax.broadcasted_iota(jnp.int32, (C, C), 1)
    eye = jnp.broadcast_to((ri == ci).astype(f32)[None], (H, C, C))
    Dg = jnp.where(((ri >> 3) == (ci >> 3))[None], L, 0.0)
    D2 = _bmm_split(Dg, Dg, 2, 1)
    X = _bmm_split(_bmm_split(eye - Dg, eye + D2, 2, 1), eye + _bmm_split(D2, D2, 2, 1), 2, 1)
    for lg in range(3, C.bit_length() - 1):
        same = (ri >> (lg + 1)) == (ci >> (lg + 1))
        lower_left = same & (((ri >> lg) & 1) == 1) & (((ci >> lg) & 1) == 0)
        X = X - _bmm_split(_bmm_split(X, jnp.where(lower_left[None], L, 0.0), 2, 1), X, 2, 1)
    return X


@jax.custom_vjp
def _tri_inverse(L):
    return _tri_inverse_raw(L)


def _tri_inverse_fwd(L):
    X = _tri_inverse_raw(L)
    return X, X


def _tri_inverse_bwd(X, dX):
    return (-_bmm_split(_bmm_split(X, dX, 1, 1), X, 2, 2),)


_tri_inverse.defvjp(_tri_inverse_fwd, _tri_inverse_bwd)


@jax.custom_vjp
def _tri_inverse_known(L, X):
    return X


def _tri_inverse_known_fwd(L, X):
    return X, X


def _tri_inverse_known_bwd(X, dX):
    return _tri_inverse_bwd(X, dX)[0], jnp.zeros_like(X)


_tri_inverse_known.defvjp(_tri_inverse_known_fwd, _tri_inverse_known_bwd)


DN_NC = 2


def _gdn_chunk(q, k, v, ab, alog, dtb, S, X_known=None, keep_X=False):
    H, C, _ = q.shape
    NC, NH = ab.shape[0], H // ab.shape[0]
    lane = lax.broadcasted_iota(jnp.int32, (H, C, LANE), 2)
    head = lax.broadcasted_iota(jnp.int32, (H, C, LANE), 0) & (NH - 1)
    abb = jnp.concatenate([jnp.broadcast_to(ab[c][None], (NH, C, LANE)) for c in range(NC)], axis=0)
    a = jnp.sum(jnp.where(lane == head, abb, 0.0), axis=2, keepdims=True)
    b = jnp.sum(jnp.where(lane == head + 8, abb, 0.0), axis=2, keepdims=True)
    pick = lax.broadcasted_iota(jnp.int32, (H, 1, LANE), 2) == (lax.broadcasted_iota(jnp.int32, (H, 1, LANE), 0) & (NH - 1))
    al = jnp.sum(jnp.where(pick, alog[None], 0.0), axis=2, keepdims=True)
    db = jnp.sum(jnp.where(pick, dtb[None], 0.0), axis=2, keepdims=True)
    g = -jnp.exp(al) * jax.nn.softplus(a + db)
    beta = jax.nn.sigmoid(b)
    ri = lax.broadcasted_iota(jnp.int32, (C, C), 0)
    ci = lax.broadcasted_iota(jnp.int32, (C, C), 1)
    G = jnp.broadcast_to(g, (H, C, LANE))
    gc = _cumsum_rows(G)
    gi = _cumsum_rows(jnp.broadcast_to(g, (H, C, C)))
    decay = jnp.exp(jnp.where((ri >= ci)[None], gi - jnp.swapaxes(gi, 1, 2), -jnp.inf))
    qs = q * (LANE ** -0.5)
    kb = k * beta
    L = jnp.where((ri > ci)[None], _bdot_nt(kb, k) * decay, 0.0)
    X = _tri_inverse(L) if X_known is None else _tri_inverse_known(L, X_known)
    egc = jnp.exp(gc)
    u = _solve_apply(X, v * beta)
    w = _solve_apply(X, kb * egc)
    qk = _bdot_nt(qs, k) * decay
    g_last = jnp.sum(G, axis=1, keepdims=True)
    k_dec = k * jnp.exp(g_last - gc)
    q_dec = qs * egc
    e_last = jnp.exp(g_last)
    outs = []
    for c in range(NC):
        sl = slice(c * NH, (c + 1) * NH)
        v_new = u[sl] - _bdot_nn(w[sl], S)
        outs.append(_bdot_nn(q_dec[sl], S) + _bdot_nn(qk[sl], v_new))
        S = S * e_last[sl] + _bdot_tn(k_dec[sl], v_new)
    o = jnp.concatenate(outs, axis=0)
    return (o, S, X) if keep_X else (o, S)


def _heads(ref):
    return jnp.stack([ref[DN_C * c:DN_C * (c + 1), LANE * h:LANE * (h + 1)] for c in range(DN_NC) for h in range(8)], axis=0)


def _put_heads(ref, val):
    for c in range(DN_NC):
        for h in range(8):
            ref[DN_C * c:DN_C * (c + 1), LANE * h:LANE * (h + 1)] = val[8 * c + h]


def _chunk_rows(ref):
    return jnp.stack([ref[DN_C * c:DN_C * (c + 1), :] for c in range(DN_NC)], axis=0)


def _hosted_parts(hosted):
    if hosted is None:
        return [], [], [], []
    n = len(hosted["arrays"])
    return list(hosted["arrays"]), [HBM_SPEC] * n, list(hosted["out_shape"]), _dma_sems(hosted["n_sems"])


def _gdn_fwd(name, q, k, v, proj, alog, dtb, hosted=None):
    T = q.shape[0]
    R = DN_C * DN_NC
    N = T // R
    h_args, h_specs, h_shapes, h_sems = _hosted_parts(hosted)
    nh = len(h_args)

    def body(q_ref, k_ref, v_ref, ab_ref, al_ref, dt_ref, *rest):
        h_ins, (o_ref, sall_ref, xinv_ref), h_outs, s_scr, sems = rest[:nh], rest[nh:nh + 3], rest[nh + 3:2 * nh + 3], rest[2 * nh + 3], rest[2 * nh + 4:]
        step = pl.program_id(0)

        @pl.when(step == 0)
        def _():
            s_scr[...] = jnp.zeros_like(s_scr)
            if hosted is not None:
                hosted["start"](h_ins, h_outs, *sems)

        S = s_scr[...]
        sall_ref[...] = S
        o, S_new, X = _gdn_chunk(_heads(q_ref), _heads(k_ref), _heads(v_ref), _chunk_rows(ab_ref), al_ref[...], dt_ref[...], S, keep_X=True)
        _put_heads(o_ref, o)
        s_scr[...] = S_new
        xinv_ref[...] = X

        if hosted is not None:
            @pl.when(step == N - 1)
            def _():
                hosted["finish"](h_ins, h_outs, *sems)

    blk = pl.BlockSpec((R, 8 * LANE), lambda n: (n, 0))
    vec = pl.BlockSpec((1, LANE), lambda n: (0, 0))
    state = pl.BlockSpec((None, 8, LANE, LANE), lambda n: (n, 0, 0, 0))
    xinv = pl.BlockSpec((None, 8 * DN_NC, DN_C, DN_C), lambda n: (n, 0, 0, 0))
    outs = _pcall(
        body, grid=(N,), in_specs=[blk, blk, blk, pl.BlockSpec((R, LANE), lambda n: (n, CB_AB)), vec, vec] + h_specs,
        out_specs=[blk, state, xinv] + h_specs,
        out_shape=[SDS((T, 1024), f32), SDS((N, 8, LANE, LANE), f32), SDS((N, 8 * DN_NC, DN_C, DN_C), f32)] + h_shapes,
        scratch_shapes=[pltpu.VMEM((8, LANE, LANE), f32)] + h_sems, name=name)(q, k, v, proj, alog, dtb, *h_args)
    return outs[0], outs[1], outs[2], list(outs[3:])


def _gdn_bwd(name, q, k, v, proj, alog, dtb, sall, xinv, do, hosted=None):
    T = q.shape[0]
    R = DN_C * DN_NC
    N = T // R
    h_args, h_specs, h_shapes, h_sems = _hosted_parts(hosted)
    nh = len(h_args)

    def body(q_ref, k_ref, v_ref, ab_ref, al_ref, dt_ref, s_ref, x_ref, do_ref, *rest):
        h_ins, h_outs, ds_scr, sems = rest[:nh], rest[nh + 6:2 * nh + 6], rest[2 * nh + 6], rest[2 * nh + 7:]
        dq_ref, dk_ref, dv_ref, dab_ref, dal_ref, ddt_ref = rest[nh:nh + 6]
        step = pl.program_id(0)

        @pl.when(step == 0)
        def _():
            ds_scr[...] = jnp.zeros_like(ds_scr)
            dal_ref[...] = jnp.zeros_like(dal_ref)
            ddt_ref[...] = jnp.zeros_like(ddt_ref)
            if hosted is not None:
                hosted["start"](h_ins, h_outs, *sems)

        _, vjp = jax.vjp(functools.partial(_gdn_chunk, X_known=x_ref[...]), _heads(q_ref), _heads(k_ref), _heads(v_ref), _chunk_rows(ab_ref),
                         al_ref[...], dt_ref[...], s_ref[...])
        dq, dk, dv, dab, dal, ddt, dS = vjp((_heads(do_ref), ds_scr[...]))
        _put_heads(dq_ref, dq)
        _put_heads(dk_ref, dk)
        _put_heads(dv_ref, dv)
        ds_scr[...] = dS
        for c in range(DN_NC):
            dab_ref[DN_C * c:DN_C * (c + 1), :] = dab[c]
        dal_ref[...] += dal
        ddt_ref[...] += ddt

        if hosted is not None:
            @pl.when(step == N - 1)
            def _():
                hosted["finish"](h_ins, h_outs, *sems)

    blk = pl.BlockSpec((R, 8 * LANE), lambda n: (N - 1 - n, 0))
    vec = pl.BlockSpec((1, LANE), lambda n: (0, 0))
    state = pl.BlockSpec((None, 8, LANE, LANE), lambda n: (N - 1 - n, 0, 0, 0))
    outs = _pcall(
        body, grid=(N,),
        in_specs=[blk, blk, blk, pl.BlockSpec((R, LANE), lambda n: (N - 1 - n, CB_AB)), vec, vec, state,
                  pl.BlockSpec((None, 8 * DN_NC, DN_C, DN_C), lambda n: (N - 1 - n, 0, 0, 0)), blk] + h_specs,
        out_specs=[blk, blk, blk, pl.BlockSpec((R, LANE), lambda n: (N - 1 - n, 0)), vec, vec] + h_specs,
        out_shape=[SDS((T, 1024), f32)] * 3 + [SDS((T, LANE), f32), SDS((1, LANE), f32), SDS((1, LANE), f32)] + h_shapes,
        scratch_shapes=[pltpu.VMEM((8, LANE, LANE), f32)] + h_sems, name=name)(q, k, v, proj, alog, dtb, sall, xinv, do, *h_args)
    return tuple(outs[:6]), list(outs[6:])


def _segmean_raw(x2, bd):
    return jnp.concatenate([_dot(x2[:, LANE * j:LANE * (j + 1)], bd, 1, 0) for j in range(x2.shape[1] // LANE)], axis=1)


@jax.custom_vjp
def _segmean(x2, bd):
    return _segmean_raw(x2, bd)


def _segmean_fwd(x2, bd):
    return _segmean_raw(x2, bd), bd


def _segmean_bwd(bd, dy):
    return _segmean_raw(dy, bd), jnp.zeros_like(bd)


_segmean.defvjp(_segmean_fwd, _segmean_bwd)


def _qknorm_fn(x, w, bd):
    return x * lax.rsqrt(_segmean(x * x, bd) + EPS) * w


def _rope_apply(xn, c, s1, s2):
    W = xn.shape[1]
    return xn * c + pltpu.roll(xn, W - 8, 1) * s1 + pltpu.roll(xn, 8, 1) * s2


def _rope_apply_t(d, c, s1, s2):
    W = d.shape[1]
    return d * c + pltpu.roll(d * s1, 8, 1) + pltpu.roll(d * s2, W - 8, 1)


def _rope_tiles(refs, width):
    return [jnp.tile(r[...], (1, width // LANE)) for r in refs]


def _qkprep_fwd(name, proj, wb, width, w, bd, tabs):
    T = proj.shape[0]
    tr = _pick(T, (256, 128))

    def body(x_ref, w_ref, bd_ref, c_ref, s1_ref, s2_ref, o_ref):
        xn = _qknorm_fn(x_ref[...], w_ref[...], bd_ref[...])
        o_ref[...] = _rope_apply(xn, *_rope_tiles((c_ref, s1_ref, s2_ref), width))

    row0 = pl.BlockSpec((tr, width), lambda i: (i, 0))
    tab = pl.BlockSpec((tr, LANE), lambda i: (i, 0))
    full = lambda a: pl.BlockSpec(a.shape, lambda i: (0, 0))
    return _pcall(
        body, grid=(T // tr,), in_specs=[pl.BlockSpec((tr, width), lambda i: (i, wb)), full(w), full(bd), tab, tab, tab],
        out_specs=row0, out_shape=SDS((T, width), f32), name=name)(proj, w, bd, *tabs)


def _qkprep_bwd(name, proj, wb, width, w, bd, tabs, dout, dest_buf="none"):
    T = proj.shape[0]
    tr = _pick(T, (256, 128))
    into = not isinstance(dest_buf, str)
    b_args, b_specs, aliases = _into_buffer(dest_buf, 7, 0) if into else ([], [], {})

    def body(x_ref, w_ref, bd_ref, c_ref, s1_ref, s2_ref, do_ref, *rest):
        dx_ref, dw_ref = rest[len(b_args):]
        i = pl.program_id(0)
        dxn = _rope_apply_t(do_ref[...], *_rope_tiles((c_ref, s1_ref, s2_ref), width))
        bd = bd_ref[...]
        _, vjp = jax.vjp(lambda x, w_: _qknorm_fn(x, w_, bd), x_ref[...], w_ref[...])
        dx, dw = vjp(dxn)
        dx_ref[...] = dx.astype(dx_ref.dtype)

        @pl.when(i == 0)
        def _():
            dw_ref[...] = jnp.zeros_like(dw_ref)

        dw_ref[...] += dw

    row0 = pl.BlockSpec((tr, width), lambda i: (i, 0))
    tab = pl.BlockSpec((tr, LANE), lambda i: (i, 0))
    full = lambda a: pl.BlockSpec(a.shape, lambda i: (0, 0))
    dx_spec = pl.BlockSpec((tr, width), lambda i: (i, wb)) if into else row0
    dx_shape = SDS((T, PROJ_W), bf16) if into else SDS((T, width), bf16)
    return _pcall(
        body, grid=(T // tr,), in_specs=[pl.BlockSpec((tr, width), lambda i: (i, wb)), full(w), full(bd), tab, tab, tab, row0] + b_specs,
        out_specs=[dx_spec, full(w)], out_shape=[dx_shape, SDS(w.shape, f32)], input_output_aliases=aliases, name=name)(
            proj, w, bd, *tabs, dout, *b_args)


def _make_dot16(ca, cb):
    def raw(x, y, cx, cy):
        return _dot(x.astype(bf16), y.astype(bf16), cx, cy, precision=None)

    @jax.custom_vjp
    def f(a, b):
        return raw(a, b, ca, cb)

    def fwd(a, b):
        return raw(a, b, ca, cb), (a, b)

    def bwd(res, dy):
        a, b = res
        if (ca, cb) == (1, 0):
            return raw(dy, b, 1, 1), raw(a, dy, 0, 0)
        return raw(dy, b, 1, 0), raw(dy, a, 0, 0)

    f.defvjp(fwd, bwd)
    return f


_dot16_nn, _dot16_nt = _make_dot16(1, 0), _make_dot16(1, 1)


def _attn_group(qg, kb, vb, sinks, first, hk):
    R = qg.shape[0]
    s = _dot16_nt(qg, kb) * 0.125
    qi = lax.broadcasted_iota(jnp.int32, (R, 2 * SWA_B), 0) & (SWA_B - 1)
    kj = lax.broadcasted_iota(jnp.int32, (R, 2 * SWA_B), 1)
    rel = qi + SWA_B - kj
    mask = (rel >= 0) & (rel < SWA_B) & ((kj >= SWA_B) | jnp.logical_not(first))
    s = jnp.where(mask, s, -jnp.inf)
    head = (lax.broadcasted_iota(jnp.int32, (R, LANE), 0) >> 7) + 8 * hk
    lane = lax.broadcasted_iota(jnp.int32, (R, LANE), 1)
    sink = jnp.sum(jnp.where(lane == head, jnp.broadcast_to(sinks, (R, LANE)), 0.0), axis=1, keepdims=True)
    m = lax.stop_gradient(jnp.maximum(jnp.max(s, axis=1, keepdims=True), sink))
    p = jnp.exp(s - m)
    denom = jnp.sum(p, axis=1, keepdims=True) + jnp.exp(sink - m)
    return _dot16_nn(p / denom, vb)


def _group_rows(ref, hk):
    return jnp.concatenate([ref[:, 64 * (8 * hk + g):64 * (8 * hk + g + 1)] for g in range(8)], axis=0)


def _put_group(ref, hk, val):
    for g in range(8):
        ref[:, 64 * (8 * hk + g):64 * (8 * hk + g + 1)] = val[SWA_B * g:SWA_B * (g + 1)].astype(ref.dtype)


def _attn_specs():
    qs = pl.BlockSpec((SWA_B, 1024), lambda i: (i, 0))
    cur = pl.BlockSpec((SWA_B, LANE), lambda i: (i, 0))
    prev = pl.BlockSpec((SWA_B, LANE), lambda i: (jnp.maximum(i - 1, 0), 0))
    vcur = pl.BlockSpec((SWA_B, LANE), lambda i: (i, CB_SWV))
    vprev = pl.BlockSpec((SWA_B, LANE), lambda i: (jnp.maximum(i - 1, 0), CB_SWV))
    vec = pl.BlockSpec((1, LANE), lambda i: (0, 0))
    return qs, cur, prev, vcur, vprev, vec


def _attn_fwd(name, sq, sk, proj, sinks):
    T = sq.shape[0]

    def body(q_ref, kp_ref, kc_ref, vp_ref, vc_ref, sk_ref, o_ref):
        first = pl.program_id(0) == 0
        sinks_v = sk_ref[...]
        for hk in range(2):
            ks = slice(64 * hk, 64 * hk + 64)
            kb = jnp.concatenate([kp_ref[:, ks], kc_ref[:, ks]], axis=0)
            vb = jnp.concatenate([vp_ref[:, ks], vc_ref[:, ks]], axis=0)
            _put_group(o_ref, hk, _attn_group(_group_rows(q_ref, hk), kb, vb, sinks_v, first, hk))

    qs, cur, prev, vcur, vprev, vec = _attn_specs()
    return _pcall(body, grid=(T // SWA_B,), in_specs=[qs, prev, cur, vprev, vcur, vec], out_specs=qs,
                  out_shape=SDS((T, 1024), bf16), name=name)(sq, sk, sk, proj, proj, sinks)


def _attn_bwd(name, sq, sk, proj, sinks, do):
    T = sq.shape[0]

    def body(q_ref, kp_ref, kc_ref, vp_ref, vc_ref, sk_ref, do_ref, dq_ref, dkp_ref, dkc_ref, dvp_ref, dvc_ref, dsk_ref):
        first = pl.program_id(0) == 0

        @pl.when(first)
        def _():
            dsk_ref[...] = jnp.zeros_like(dsk_ref)

        sinks_v = sk_ref[...]
        dsk = jnp.zeros((1, LANE), f32)
        for hk in range(2):
            ks = slice(64 * hk, 64 * hk + 64)
            kb = jnp.concatenate([kp_ref[:, ks], kc_ref[:, ks]], axis=0)
            vb = jnp.concatenate([vp_ref[:, ks], vc_ref[:, ks]], axis=0)
            _, vjp = jax.vjp(functools.partial(_attn_group, first=first, hk=hk), _group_rows(q_ref, hk), kb, vb, sinks_v)
            dq, dkb, dvb, ds_ = vjp(_group_rows(do_ref, hk))
            _put_group(dq_ref, hk, dq)
            dsk = dsk + ds_
            dkp_ref[:, ks] = dkb[:SWA_B]
            dkc_ref[:, ks] = dkb[SWA_B:]
            dvp_ref[:, ks] = dvb[:SWA_B]
            dvc_ref[:, ks] = dvb[SWA_B:]
        dsk_ref[...] += dsk

    qs, cur, prev, vcur, vprev, vec = _attn_specs()
    return _pcall(
        body, grid=(T // SWA_B,), in_specs=[qs, prev, cur, vprev, vcur, vec, qs], out_specs=[qs, cur, cur, cur, cur, vec],
        out_shape=[SDS((T, 1024), f32)] + [SDS((T, LANE), f32)] * 4 + [SDS((1, LANE), f32)], name=name)(sq, sk, sk, proj, proj, sinks, do)


def _shift_add(name, cur, prev, out_dtype):
    T = cur.shape[0]
    nb = T // SWA_B

    def body(c_ref, p_ref, o_ref):
        has_next = (pl.program_id(0) + 1 < nb).astype(f32)
        o_ref[...] = (c_ref[...] + has_next * p_ref[...]).astype(o_ref.dtype)

    blk = pl.BlockSpec((SWA_B, LANE), lambda i: (i, 0))
    nxt = pl.BlockSpec((SWA_B, LANE), lambda i: (jnp.minimum(i + 1, nb - 1), 0))
    return _pcall(body, grid=(nb,), in_specs=[blk, nxt], out_specs=blk, out_shape=SDS((T, LANE), out_dtype), name=name)(cur, prev)


def _loss(name, y, tgt):
    T = y.shape[0]

    def body(y_ref, t_ref, l_ref, dy_ref):
        @pl.when(pl.program_id(0) == 0)
        def _():
            l_ref[...] = jnp.zeros_like(l_ref)

        d = y_ref[...] - t_ref[...]
        l_ref[...] += jnp.sum(d * d) * (0.5 / D)
        dy_ref[...] = d * (1.0 / D)

    row = pl.BlockSpec((TR, D), lambda i: (i, 0))
    return _pcall(body, grid=(T // TR,), in_specs=[row, row], out_specs=[pl.BlockSpec((8, LANE), lambda i: (0, 0)), row],
                  out_shape=[SDS((8, LANE), f32), SDS((T, D), f32)], name=name)(y, tgt)


def _adamw(name, w, g, m, v):
    shape = w.shape
    C = shape[-1]
    R = int(np.prod(shape[:-1]))
    tr = _pick(R, (128, 64, 16, 8))
    bc1 = np.float32(1.0 - ADAM_B1 ** ADAM_STEP)
    bc2 = np.float32(1.0 - ADAM_B2 ** ADAM_STEP)

    def body(w_ref, g_ref, m_ref, v_ref, d_ref, mo_ref, vo_ref):
        g_ = g_ref[...]
        m_ = ADAM_B1 * m_ref[...] + (1.0 - ADAM_B1) * g_
        v_ = ADAM_B2 * v_ref[...] + (1.0 - ADAM_B2) * (g_ * g_)
        d_ref[...] = -ADAM_LR * ((m_ / bc1) / (jnp.sqrt(v_ / bc2) + ADAM_EPS) + ADAM_WD * w_ref[...])
        mo_ref[...] = m_
        vo_ref[...] = v_

    blk = pl.BlockSpec((tr, C), lambda i: (i, 0))
    outs = _pcall(body, grid=(R // tr,), in_specs=[blk] * 4, out_specs=[blk] * 3, out_shape=[SDS((R, C), f32)] * 3,
                  compiler_params=_cparams(VMEM_BIG), name=name)(*[t.reshape(R, C) for t in (w, g, m, v)])
    return [o.reshape(shape) for o in outs]


def _silu_rows(name, x):
    def body(x_ref, o_ref):
        t = x_ref[...]
        o_ref[...] = (t * jax.nn.sigmoid(t)).astype(o_ref.dtype)

    return _pcall(body, out_shape=SDS(x.shape, bf16), name=name)(x)


def _sum_leading(name, x):
    n = x.shape[0]

    def body(x_ref, o_ref):
        acc = x_ref[0]
        for k in range(1, n):
            acc = acc + x_ref[k]
        o_ref[...] = acc

    tr = x.shape[1] if x.size * 4 <= 8 * 2 ** 20 else _pick(x.shape[1], (COMM_TR, 8))
    return _pcall(body, grid=(x.shape[1] // tr,), in_specs=[pl.BlockSpec((n, tr, x.shape[2]), lambda i: (0, i, 0))],
                  out_specs=pl.BlockSpec((tr, x.shape[2]), lambda i: (i, 0)), out_shape=SDS(x.shape[1:], x.dtype), name=name)(x)


def _add_my_half(name, g4, b1, c):
    _, R, W = g4.shape
    nblk = (R // 2) // COMM_TR

    def body(c_ref, g_ref, b_ref, o_ref):
        o_ref[...] = g_ref[...] + b_ref[...]

    grid_spec = pltpu.PrefetchScalarGridSpec(
        num_scalar_prefetch=1, grid=(4, nblk),
        in_specs=[pl.BlockSpec((None, COMM_TR, W), lambda s, i, c_ref: (s, c_ref[0] * nblk + i, 0)),
                  pl.BlockSpec((None, COMM_TR, W), lambda s, i, c_ref: (s, i, 0))],
        out_specs=pl.BlockSpec((None, COMM_TR, W), lambda s, i, c_ref: (s, i, 0)))
    return _pcall(body, grid_spec=grid_spec, out_shape=SDS((4, R // 2, W), f32), name=name)(c.reshape(1), g4, b1)


HBM_SPEC = pl.BlockSpec(memory_space=pltpu.HBM)


def _position():
    x, y, c = lax.axis_index("x"), lax.axis_index("y"), lax.axis_index("c")
    return x, y, c, [(1 - x, y), (x, 1 - y), (1 - x, 1 - y)]


def _remote(src, dst, send_sems, recv_sems, k, to):
    return pltpu.make_async_remote_copy(src_ref=src, dst_ref=dst, send_sem=send_sems.at[k], recv_sem=recv_sems.at[k],
                                        device_id=to, device_id_type=MESH)


def _allgather_chips(name, buf):
    R, W = buf.shape
    Rh = R // 2

    def body(in_ref, out_ref, send_sems, recv_sems, local_sem):
        x, y, c, chips = _position()
        me = 2 * x + y
        sib = (x, y, 1 - c)
        half = pl.ds(pl.multiple_of(c * Rh, 32), Rh)
        ohalf = pl.ds(pl.multiple_of((1 - c) * Rh, 32), Rh)
        mine = pltpu.make_async_copy(in_ref, out_ref.at[me], local_sem)
        mine.start()
        first = [_remote(in_ref.at[half], out_ref.at[me, half], send_sems, recv_sems, j, (cx, cy, c)) for j, (cx, cy) in enumerate(chips)]
        for cp in first:
            cp.start()
        passed = []
        for j, (cx, cy) in enumerate(chips):
            rows = out_ref.at[2 * cx + cy, half]
            _remote(rows, rows, send_sems, recv_sems, j, (cx, cy, c)).wait_recv()
            cp = _remote(rows, rows, send_sems, recv_sems, 3 + j, sib)
            cp.start()
            passed.append(cp)
        for j, (cx, cy) in enumerate(chips):
            rows = out_ref.at[2 * cx + cy, ohalf]
            _remote(rows, rows, send_sems, recv_sems, 3 + j, sib).wait_recv()
        for cp in first + passed:
            cp.wait_send()
        mine.wait()

    return _pcall(body, in_specs=[HBM_SPEC], out_specs=HBM_SPEC, out_shape=SDS((4, R, W), buf.dtype),
                  scratch_shapes=[pltpu.SemaphoreType.DMA((6,)), pltpu.SemaphoreType.DMA((6,)), pltpu.SemaphoreType.DMA], name=name)(buf)


def _swap_halves(name, g4):
    _, R, W = g4.shape
    Rh = R // 2

    def body(in_ref, out_ref, send_sems, recv_sems):
        x, y, c, _ = _position()
        ohalf = pl.ds(pl.multiple_of((1 - c) * Rh, 32), Rh)
        cp = _remote(in_ref.at[:, ohalf, :], out_ref, send_sems, recv_sems, 0, (x, y, 1 - c))
        cp.start()
        cp.wait()

    return _pcall(body, in_specs=[HBM_SPEC], out_specs=HBM_SPEC, out_shape=SDS((4, Rh, W), g4.dtype),
                  scratch_shapes=[pltpu.SemaphoreType.DMA((1,)), pltpu.SemaphoreType.DMA((1,))], name=name)(g4)


def _scatter_chips(name, p4):
    _, Rh, W = p4.shape

    def body(in_ref, out_ref, send_sems, recv_sems, local_sem):
        x, y, c, chips = _position()
        me = 2 * x + y
        mine = pltpu.make_async_copy(in_ref.at[me], out_ref.at[me], local_sem)
        mine.start()
        sends = [_remote(in_ref.at[2 * cx + cy], out_ref.at[me], send_sems, recv_sems, j, (cx, cy, c)) for j, (cx, cy) in enumerate(chips)]
        for cp in sends:
            cp.start()
        for j, (cx, cy) in enumerate(chips):
            slot = out_ref.at[2 * cx + cy]
            _remote(slot, slot, send_sems, recv_sems, j, (cx, cy, c)).wait_recv()
        for cp in sends:
            cp.wait_send()
        mine.wait()

    return _pcall(body, in_specs=[HBM_SPEC], out_specs=HBM_SPEC, out_shape=SDS((4, Rh, W), p4.dtype),
                  scratch_shapes=[pltpu.SemaphoreType.DMA((3,)), pltpu.SemaphoreType.DMA((3,)), pltpu.SemaphoreType.DMA], name=name)(p4)


def _join_halves(name, r):
    Rh, W = r.shape

    def body(in_ref, out_ref, send_sems, recv_sems, local_sem):
        x, y, c, _ = _position()
        mine = pltpu.make_async_copy(in_ref, out_ref.at[c], local_sem)
        mine.start()
        cp = _remote(in_ref, out_ref.at[c], send_sems, recv_sems, 0, (x, y, 1 - c))
        cp.start()
        _remote(in_ref, out_ref.at[1 - c], send_sems, recv_sems, 0, (x, y, 1 - c)).wait_recv()
        cp.wait_send()
        mine.wait()

    return _pcall(body, in_specs=[HBM_SPEC], out_specs=HBM_SPEC, out_shape=SDS((2, Rh, W), r.dtype),
                  scratch_shapes=[pltpu.SemaphoreType.DMA((1,)), pltpu.SemaphoreType.DMA((1,)), pltpu.SemaphoreType.DMA], name=name)(r)


def _allgather_all(name, buf):
    r, W = buf.shape

    def body(in_ref, out_ref, send_sems, recv_sems, local_sem):
        x, y, c, _ = _position()
        me = 4 * x + 2 * y + c
        mine = pltpu.make_async_copy(in_ref, out_ref.at[me], local_sem)
        mine.start()
        peers = []
        for mk in range(1, 8):
            mx, my, mc = (mk >> 2) & 1, (mk >> 1) & 1, mk & 1
            px = 1 - x if mx else x
            py = 1 - y if my else y
            pc = 1 - c if mc else c
            peers.append((px, py, pc))
        sends = [_remote(in_ref, out_ref.at[me], send_sems, recv_sems, k, p) for k, p in enumerate(peers)]
        for cp in sends:
            cp.start()
        for k, (px, py, pc) in enumerate(peers):
            slot = out_ref.at[4 * px + 2 * py + pc]
            _remote(slot, slot, send_sems, recv_sems, k, (px, py, pc)).wait_recv()
        for cp in sends:
            cp.wait_send()
        mine.wait()

    return _pcall(body, in_specs=[HBM_SPEC], out_specs=HBM_SPEC, out_shape=SDS((8, r, W), buf.dtype),
                  scratch_shapes=[pltpu.SemaphoreType.DMA((7,)), pltpu.SemaphoreType.DMA((7,)), pltpu.SemaphoreType.DMA], name=name)(buf)


def _reduce_scatter(g4, c, tag):
    b1 = _swap_halves("rs_swap_" + tag, g4)
    p4 = _add_my_half("rs_pair_" + tag, g4, b1, c)
    b2 = _scatter_chips("rs_scatter_" + tag, p4)
    r = _sum_leading("rs_sum_" + tag, b2)
    full = _join_halves("rs_join_" + tag, r)
    return full.reshape(g4.shape[1], g4.shape[2])


def _dma_sems(n):
    return [pltpu.SemaphoreType.DMA((n,)), pltpu.SemaphoreType.DMA((n,))]


def _gather_chips(name, shards):
    n = len(shards)

    def body(*refs):
        ins, outs = refs[:n], refs[n:2 * n]
        send_sems, recv_sems = refs[2 * n:]
        x, y, c, chips = _position()
        me = 2 * x + y
        sib = (x, y, 1 - c)
        sends, halves = [], []
        for i in range(n):
            rh = ins[i].shape[0] // 2
            halves.append((pl.ds(pl.multiple_of(c * rh, 16), rh), pl.ds(pl.multiple_of((1 - c) * rh, 16), rh)))
        for i in range(n):
            for j, (cx, cy) in enumerate(chips):
                cp = _remote(ins[i].at[halves[i][0]], outs[i].at[me, halves[i][0]], send_sems, recv_sems, 6 * i + j, (cx, cy, c))
                cp.start()
                sends.append(cp)
        for j, (cx, cy) in enumerate(chips):
            for i in range(n):
                rows = outs[i].at[2 * cx + cy, halves[i][0]]
                _remote(rows, rows, send_sems, recv_sems, 6 * i + j, (cx, cy, c)).wait_recv()
                cp = _remote(rows, rows, send_sems, recv_sems, 6 * i + 3 + j, sib)
                cp.start()
                sends.append(cp)
        for j, (cx, cy) in enumerate(chips):
            for i in range(n):
                rows = outs[i].at[2 * cx + cy, halves[i][1]]
                _remote(rows, rows, send_sems, recv_sems, 6 * i + 3 + j, sib).wait_recv()
        for cp in sends:
            cp.wait_send()

    return _pcall(body, in_specs=[HBM_SPEC] * n, out_specs=[HBM_SPEC] * n, out_shape=[SDS((4,) + s.shape, s.dtype) for s in shards],
                  scratch_shapes=_dma_sems(6 * n), name=name)(*shards)


def _swap_halves_multi(name, slots):
    n = len(slots)

    def body(*refs):
        ins, outs = refs[:n], refs[n:2 * n]
        send_sems, recv_sems = refs[2 * n:]
        x, y, c, _ = _position()
        cps = []
        for i in range(n):
            rh = ins[i].shape[1] // 2
            ohalf = pl.ds(pl.multiple_of((1 - c) * rh, 8), rh)
            cp = _remote(ins[i].at[:, ohalf, :], outs[i], send_sems, recv_sems, i, (x, y, 1 - c))
            cp.start()
            cps.append(cp)
        for cp in cps:
            cp.wait()

    return _pcall(body, in_specs=[HBM_SPEC] * n, out_specs=[HBM_SPEC] * n,
                  out_shape=[SDS((4, s.shape[1] // 2, s.shape[2]), s.dtype) for s in slots], scratch_shapes=_dma_sems(n), name=name)(*slots)


def _pair_add(name, g4, b1, c):
    _, R, W = g4.shape
    tr = _pick(R // 2, (256, 128, 32, 16))
    nblk = (R // 2) // tr

    def body(c_ref, g_ref, b_ref, o_ref):
        o_ref[...] = (g_ref[...] + b_ref[...]).astype(o_ref.dtype)

    grid_spec = pltpu.PrefetchScalarGridSpec(
        num_scalar_prefetch=1, grid=(4, nblk),
        in_specs=[pl.BlockSpec((None, tr, W), lambda s, i, c_ref: (s, c_ref[0] * nblk + i, 0)),
                  pl.BlockSpec((None, tr, W), lambda s, i, c_ref: (s, i, 0))],
        out_specs=pl.BlockSpec((None, tr, W), lambda s, i, c_ref: (s, i, 0)))
    return _pcall(body, grid_spec=grid_spec, out_shape=SDS((4, R // 2, W), bf16), name=name)(c.reshape(1), g4, b1)


def _scatter_chips_multi(name, ps):
    n = len(ps)

    def body(*refs):
        ins, outs = refs[:n], refs[n:2 * n]
        send_sems, recv_sems = refs[2 * n:]
        x, y, c, chips = _position()
        me = 2 * x + y
        sends = []
        for i in range(n):
            for j, (cx, cy) in enumerate(chips):
                cp = _remote(ins[i].at[2 * cx + cy], outs[i].at[me], send_sems, recv_sems, 3 * i + j, (cx, cy, c))
                cp.start()
                sends.append(cp)
        for i in range(n):
            for j, (cx, cy) in enumerate(chips):
                slot = outs[i].at[2 * cx + cy]
                _remote(slot, slot, send_sems, recv_sems, 3 * i + j, (cx, cy, c)).wait_recv()
        for cp in sends:
            cp.wait_send()

    return _pcall(body, in_specs=[HBM_SPEC] * n, out_specs=[HBM_SPEC] * n, out_shape=[SDS(p.shape, p.dtype) for p in ps],
                  scratch_shapes=_dma_sems(3 * n), name=name)(*ps)


def _sum_chips(name, p4, b2, chip, c):
    _, Rh, W = p4.shape
    tr = _pick(Rh, (256, 128, 32, 16))
    nblk = Rh // tr

    def body(m_ref, c_ref, own_ref, r1_ref, r2_ref, r3_ref, o_ref):
        o_ref[...] = ((own_ref[...].astype(f32) + r1_ref[...].astype(f32)) + r2_ref[...].astype(f32)) + r3_ref[...].astype(f32)

    other = lambda k: pl.BlockSpec((None, tr, W), lambda i, m_ref, c_ref: (m_ref[0] ^ k, i, 0))
    grid_spec = pltpu.PrefetchScalarGridSpec(
        num_scalar_prefetch=2, grid=(nblk,),
        in_specs=[pl.BlockSpec((None, tr, W), lambda i, m_ref, c_ref: (m_ref[0], i, 0)), other(1), other(2), other(3)],
        out_specs=pl.BlockSpec((tr, W), lambda i, m_ref, c_ref: (c_ref[0] * nblk + i, 0)))
    return _pcall(body, grid_spec=grid_spec, out_shape=SDS((2 * Rh, W), f32), name=name)(chip.reshape(1), c.reshape(1), p4, b2, b2, b2)


def _join_halves(name, fulls):
    n = len(fulls)

    def body(*refs):
        outs = refs[n:2 * n]
        send_sems, recv_sems = refs[2 * n:]
        x, y, c, _ = _position()
        cps = []
        for i in range(n):
            rh = outs[i].shape[0] // 2
            mine = outs[i].at[pl.ds(pl.multiple_of(c * rh, 8), rh)]
            theirs = outs[i].at[pl.ds(pl.multiple_of((1 - c) * rh, 8), rh)]
            cp = _remote(mine, mine, send_sems, recv_sems, i, (x, y, 1 - c))
            cp.start()
            cps.append((cp, _remote(theirs, theirs, send_sems, recv_sems, i, (x, y, 1 - c))))
        for cp, back in cps:
            back.wait_recv()
            cp.wait_send()

    return _pcall(body, in_specs=[HBM_SPEC] * n, out_specs=[HBM_SPEC] * n, out_shape=[SDS(r.shape, r.dtype) for r in fulls],
                  input_output_aliases={i: i for i in range(n)}, scratch_shapes=_dma_sems(n), name=name)(*fulls)


def _hosted_gather(shards):
    n = len(shards)

    def half(ref_rows, c):
        rh = ref_rows // 2
        return pl.ds(pl.multiple_of(c * rh, 16), rh)

    def start(ins, outs, send_sems, recv_sems):
        x, y, c, chips = _position()
        me = 2 * x + y
        for i in range(n):
            rows = half(ins[i].shape[0], c)
            for j, (cx, cy) in enumerate(chips):
                _remote(ins[i].at[rows], outs[i].at[me, rows], send_sems, recv_sems, 3 * i + j, (cx, cy, c)).start()

    def finish(ins, outs, send_sems, recv_sems):
        x, y, c, chips = _position()
        me = 2 * x + y
        for i in range(n):
            rows = half(ins[i].shape[0], c)
            for j, (cx, cy) in enumerate(chips):
                _remote(ins[i].at[rows], outs[i].at[2 * cx + cy, rows], send_sems, recv_sems, 3 * i + j, (cx, cy, c)).wait_recv()
        for i in range(n):
            rows = half(ins[i].shape[0], c)
            for j, (cx, cy) in enumerate(chips):
                _remote(ins[i].at[rows], outs[i].at[me, rows], send_sems, recv_sems, 3 * i + j, (cx, cy, c)).wait_send()

    return {"arrays": shards, "out_shape": [SDS((4,) + s.shape, s.dtype) for s in shards], "n_sems": 3 * n, "start": start, "finish": finish}


def _gather_forward(name, gathered):
    n = len(gathered)

    def body(*refs):
        outs = refs[n:2 * n]
        send_sems, recv_sems = refs[2 * n:]
        x, y, c, chips = _position()
        sib = (x, y, 1 - c)
        sends = []
        for i in range(n):
            rh = outs[i].shape[1] // 2
            mine = pl.ds(pl.multiple_of(c * rh, 16), rh)
            for j, (cx, cy) in enumerate(chips):
                rows = outs[i].at[2 * cx + cy, mine]
                cp = _remote(rows, rows, send_sems, recv_sems, 3 * i + j, sib)
                cp.start()
                sends.append(cp)
        for i in range(n):
            rh = outs[i].shape[1] // 2
            theirs = pl.ds(pl.multiple_of((1 - c) * rh, 16), rh)
            for j, (cx, cy) in enumerate(chips):
                rows = outs[i].at[2 * cx + cy, theirs]
                _remote(rows, rows, send_sems, recv_sems, 3 * i + j, sib).wait_recv()
        for cp in sends:
            cp.wait_send()

    return _pcall(body, in_specs=[HBM_SPEC] * n, out_specs=[HBM_SPEC] * n, out_shape=[SDS(g.shape, g.dtype) for g in gathered],
                  input_output_aliases={i: i for i in range(n)}, scratch_shapes=_dma_sems(3 * n), name=name)(*gathered)


def _hosted_scatter(ps):
    n = len(ps)

    def start(ins, outs, send_sems, recv_sems):
        x, y, c, chips = _position()
        me = 2 * x + y
        for i in range(n):
            for j, (cx, cy) in enumerate(chips):
                _remote(ins[i].at[2 * cx + cy], outs[i].at[me], send_sems, recv_sems, 3 * i + j, (cx, cy, c)).start()

    def finish(ins, outs, send_sems, recv_sems):
        x, y, c, chips = _position()
        me = 2 * x + y
        for i in range(n):
            for j, (cx, cy) in enumerate(chips):
                slot = outs[i].at[2 * cx + cy]
                _remote(slot, slot, send_sems, recv_sems, 3 * i + j, (cx, cy, c)).wait_recv()
        for i in range(n):
            for j, (cx, cy) in enumerate(chips):
                _remote(ins[i].at[2 * cx + cy], outs[i].at[me], send_sems, recv_sems, 3 * i + j, (cx, cy, c)).wait_send()

    return {"arrays": ps, "out_shape": [SDS(p.shape, p.dtype) for p in ps], "n_sems": 3 * n, "start": start, "finish": finish}


def _hosted_swap(slots):
    n = len(slots)

    def copies(ins, outs, send_sems, recv_sems):
        x, y, c, _ = _position()
        cps = []
        for i in range(n):
            rh = ins[i].shape[1] // 2
            ohalf = pl.ds(pl.multiple_of((1 - c) * rh, 8), rh)
            cps.append(_remote(ins[i].at[:, ohalf, :], outs[i], send_sems, recv_sems, i, (x, y, 1 - c)))
        return cps

    def start(ins, outs, send_sems, recv_sems):
        for cp in copies(ins, outs, send_sems, recv_sems):
            cp.start()

    def finish(ins, outs, send_sems, recv_sems):
        for cp in copies(ins, outs, send_sems, recv_sems):
            cp.wait()

    return {"arrays": slots, "out_shape": [SDS((4, s.shape[1] // 2, s.shape[2]), s.dtype) for s in slots], "n_sems": n,
            "start": start, "finish": finish}


def _rs_begin(slots, c):
    b1 = _swap_halves_multi("rs_swap", slots)
    return [_pair_add("rs_pair_%d" % i, g, b, c) for i, (g, b) in enumerate(zip(slots, b1))]


def _rs_end(ps, b2, c, chip):
    return _join_halves("rs_join", [_sum_chips("rs_sum_%d" % i, p, b, chip, c) for i, (p, b) in enumerate(zip(ps, b2))])


def _reduce_scatter_multi(slots, c, chip):
    ps = _rs_begin(slots, c)
    return _rs_end(ps, _scatter_chips_multi("rs_scatter", ps), c, chip)


_BIG = ("w_in", "w_dn_out", "w_swa_out", "w_o", "w_up", "w_down")


def _assemble_weights(gs):
    cat1 = lambda g: jnp.concatenate([g[s] for s in range(4)], axis=1)
    rows = lambda g: g.reshape(4 * g.shape[1], g.shape[2])
    return {"w_in": _regroup_w_in(cat1(gs[0])), "w_dn_out": rows(gs[1]), "w_swa_out": rows(gs[2]), "w_o": rows(gs[3]),
            "w_up": cat1(gs[4]), "w_down": rows(gs[5])}


def _grad_slots(gw):
    cols = lambda g: g.reshape(g.shape[0], 4, g.shape[1] // 4).transpose(1, 0, 2)
    rows = lambda g: g.reshape(4, g.shape[0] // 4, g.shape[1])
    return [cols(_ungroup_w_in(gw["w_in"])), rows(gw["w_dn_out"]), rows(gw["w_swa_out"]), rows(gw["w_o"]), gw["w_up"], rows(gw["w_down"])]


_SHARDED = (
    ("w_in", (D, 1860), 1, False),
    ("dn_conv", (4, 768), 1, True),
    ("w_dn_out", (256, D), 0, False),
    ("w_swa_out", (256, D), 0, False),
    ("w_o", (256, D), 0, False),
    ("w_up", (D, 1408), 1, False),
    ("ffn_conv", (3, 704), 1, True),
    ("w_down", (704, D), 0, False),
)


def _pack_weights(shards):
    parts = []
    for nm, shp, _, as_bits in _SHARDED:
        a = shards[nm]
        parts.append(lax.bitcast_convert_type(a, bf16).reshape(-1) if as_bits else a.astype(bf16).reshape(-1))
    flat = jnp.concatenate(parts)
    return jnp.pad(flat, (0, COMM_ROWS * COMM_W - flat.shape[0])).reshape(COMM_ROWS, COMM_W)


def _unpack_weights(g):
    flat = g.reshape(4, -1)
    out, off = {}, 0
    for nm, shp, ax, as_bits in _SHARDED:
        n = int(np.prod(shp)) * (2 if as_bits else 1)
        piece = flat[:, off:off + n]
        off += n
        if as_bits:
            piece = lax.bitcast_convert_type(piece.reshape((4,) + shp + (2,)), f32)
        else:
            piece = piece.reshape((4,) + shp)
        out[nm] = jnp.concatenate([piece[s] for s in range(4)], axis=ax)
    return out


def _pack_grads(grads):
    slots = []
    for s in range(4):
        parts = []
        for nm, shp, ax, _ in _SHARDED:
            n = shp[ax]
            parts.append(lax.slice_in_dim(grads[nm], s * n, (s + 1) * n, axis=ax).reshape(-1))
        flat = jnp.concatenate(parts)
        slots.append(jnp.pad(flat, (0, COMM_ROWS * COMM_W - flat.shape[0])))
    return jnp.stack(slots).reshape(4, COMM_ROWS, COMM_W)


def _unpack_grads(r):
    flat = r.reshape(-1)
    out, off = {}, 0
    for nm, shp, _, _ in _SHARDED:
        n = int(np.prod(shp))
        out[nm] = flat[off:off + n].reshape(shp)
        off += n
    return out


def _regroup_w_in(w):
    z = lambda n: jnp.zeros((w.shape[0], n), w.dtype)
    return jnp.concatenate([w[:, 0:3072], w[:, 3072:4096], w[:, 5392:6416], w[:, 6416:7440], w[:, 4112:5136],
                            w[:, 4096:4112], z(112), w[:, 5136:5264], w[:, 5264:5392], z(128)], axis=1)


def _ungroup_w_in(g):
    return jnp.concatenate([g[:, 0:3072], g[:, 3072:4096], g[:, 7168:7184], g[:, 6144:7168], g[:, 7296:7424], g[:, 7424:7552],
                            g[:, 4096:5120], g[:, 5120:6144]], axis=1)


def _pad_lanes(v, n=LANE):
    return jnp.pad(v, (0, n - v.shape[0])).reshape(1, n)


def _layer_consts(P):
    K = {}
    K["norm_mix"] = P["norm_mix"].reshape(1, D)
    K["norm_ffn"] = P["norm_ffn"].reshape(1, D)
    K["alog"] = _pad_lanes(P["dn_a_log"])
    K["dtb"] = _pad_lanes(P["dn_dt_bias"])
    K["dn_norm"] = P["dn_norm"].reshape(1, LANE)
    K["qn"] = jnp.tile(P["swa_q_norm"], 16).reshape(1, D)
    K["kn"] = jnp.tile(P["swa_k_norm"], 2).reshape(1, LANE)
    K["sinks"] = _pad_lanes(P["swa_sinks"])
    K["ffn_b"] = P["ffn_conv_b"].reshape(1, D_FF)
    return K


def _layer_fwd(x, mod, W, K, tabs, bd, hosted=None):
    sh1, sc1, gt1, sh2, sc2, gt2 = mod
    S = {"x": x}
    h1, h1t = _rowwise_fwd("normmod1_fwd", _normmod_fn, [(x, 0, D)], [K["norm_mix"], sc1, sh1], [D], [bf16], also_transposed=True)
    proj = _matmul("proj_fwd", h1, W["w_in"], "nn", f32)
    qn = _dnconv_fwd("dnconv_q_fwd", proj, CB_Q, W["dn_conv"], True)
    kn = _dnconv_fwd("dnconv_k_fwd", proj, CB_K, W["dn_conv"], True)
    vc = _dnconv_fwd("dnconv_v_fwd", proj, CB_V, W["dn_conv"], False)
    o, sall, xinv, hosted_out = _gdn_fwd("gdn_fwd", qn, kn, vc, proj, K["alog"], K["dtb"], hosted=hosted)
    (on,) = _rowwise_fwd("dngate_fwd", _dngate_fn, [(o, 0, LANE), (proj, 8 * WB_Z, LANE)], [K["dn_norm"]], [LANE], [bf16], nc=8,
                         tr=_pick(x.shape[0], (1024,)))
    ya = _matmul("dnout_fwd", on, W["w_dn_out"], "nn", f32)
    sq = _qkprep_fwd("qprep_fwd", proj, WB_SWQ, D, K["qn"], bd[0], tabs[0])
    sk = _qkprep_fwd("kprep_fwd", proj, CB_SWK, LANE, K["kn"], bd[1], tabs[1])
    attn = _attn_fwd("attn_fwd", sq, sk, proj, K["sinks"])
    yb = _matmul("swaout_fwd", attn, W["w_swa_out"], "nn", f32)
    (merged,) = _rowwise_fwd("merge_fwd", _merge_fn, [(proj, WB_GA, D), (proj, WB_GB, D), (ya, 0, D), (yb, 0, D)], [], [D], [bf16])
    t1 = _matmul("wo_fwd", merged, W["w_o"], "nn", f32)
    (x1,) = _rowwise_fwd("resid1_fwd", _resid_fn, [(x, 0, D), (t1, 0, D)], [gt1], [D], [f32])
    h2, h2t = _rowwise_fwd("normmod2_fwd", _normmod_fn, [(x1, 0, D)], [K["norm_ffn"], sc2, sh2], [D], [bf16], also_transposed=True)
    up = _matmul("up_fwd", h2, W["w_up"], "nn", f32)
    mid = _ffnact_fwd("ffnact_fwd", up, W["ffn_conv"], K["ffn_b"])
    t2 = _matmul("down_fwd", mid, W["w_down"], "nn", f32)
    (x2,) = _rowwise_fwd("resid2_fwd", _resid_fn, [(x1, 0, D), (t2, 0, D)], [gt2], [D], [f32])
    S.update(h1t=h1t, h2t=h2t, proj=proj, qn=qn, kn=kn, vc=vc, o=o, sall=sall, xinv=xinv, on=on, ya=ya, sq=sq, sk=sk, attn=attn, yb=yb,
             merged=merged, t1=t1, x1=x1, h2=h2, up=up, mid=mid, t2=t2)
    return x2, S, hosted_out


def _layer_bwd(dx2, S, mod, W, K, tabs, bd, carry=None):
    sh1, sc1, gt1, sh2, sc2, gt2 = mod
    x, x1, proj, up = S["x"], S["x1"], S["proj"], S["up"]
    T = x.shape[0]
    gw, gs = {}, {}
    dt2, dgt2 = _rowwise_bwd("resid2_bwd", _resid_fn, [(x1, 0, D), (S["t2"], 0, D)], [gt2], [(dx2, 0, D)], [None, bf16])
    dmid = _matmul("down_bwd_x", dt2, W["w_down"], "nt", bf16)
    gw["w_down"] = _matmul("down_bwd_w", S["mid"], dt2, "tn", f32)
    dact, dlin, gw["ffn_conv"], dffn_b = _ffnact_bwd("ffnact_bwd", up, W["ffn_conv"], K["ffn_b"], dmid)
    dup = jnp.concatenate([dact, dlin], axis=1)
    dh2 = _matmul("up_bwd_x", dup, W["w_up"], "nt", f32)
    if carry is None:
        gw["w_up"] = _matmul("up_bwd_w", S["h2t"], dup, "nn", f32, out_slots=4)
        pair_sums = hosted = None
    else:
        gw["w_up"], b1 = _matmul("up_bwd_w", S["h2t"], dup, "nn", f32, out_slots=4, hosted=_hosted_swap(carry[0]))
        pair_sums = [_pair_add("rs_pair_%d" % i, g, b, carry[1]) for i, (g, b) in enumerate(zip(carry[0], b1))]
        hosted = _hosted_scatter(pair_sums)
    dx1, dnorm_ffn, dsc2, dsh2 = _rowwise_bwd("normmod2_bwd", _normmod_fn, [(x1, 0, D)], [K["norm_ffn"], sc2, sh2], [(dh2, 0, D)], [f32],
                                              add_to_first=(dx2, 0, D))
    dt1, dgt1 = _rowwise_bwd("resid1_bwd", _resid_fn, [(x, 0, D), (S["t1"], 0, D)], [gt1], [(dx1, 0, D)], [None, bf16])
    dmerged = _matmul("wo_bwd_x", dt1, W["w_o"], "nt", f32)
    gw["w_o"] = _matmul("wo_bwd_w", S["merged"], dt1, "tn", f32)
    dproj, dya, dyb = _rowwise_bwd("merge_bwd", _merge_fn, [(proj, WB_GA, D), (proj, WB_GB, D), (S["ya"], 0, D), (S["yb"], 0, D)], [],
                                   [(dmerged, 0, D)], [bf16, bf16, bf16, bf16], dest=(None, (0, 1), WB_GA // 2))
    don = _matmul("dnout_bwd_x", dya, W["w_dn_out"], "nt", f32)
    gw["w_dn_out"] = _matmul("dnout_bwd_w", S["on"], dya, "tn", f32)
    dproj, do, ddn_norm = _rowwise_bwd("dngate_bwd", _dngate_fn, [(S["o"], 0, LANE), (proj, 8 * WB_Z, LANE)], [K["dn_norm"]], [(don, 0, LANE)],
                                       [f32, bf16], nc=8, tr=_pick(T, (1024,)), dest=(dproj, (1,), 8 * WB_Z))
    (dqn, dkn, dvc, dab, dalog, ddtb), hosted_out = _gdn_bwd("gdn_bwd", S["qn"], S["kn"], S["vc"], proj, K["alog"], K["dtb"], S["sall"], S["xinv"], do,
                                                            hosted=hosted)
    dproj, dwq = _dnconv_bwd("dnconv_q_bwd", proj, CB_Q, W["dn_conv"], dqn, True, dproj)
    dproj, dwk = _dnconv_bwd("dnconv_k_bwd", proj, CB_K, W["dn_conv"], dkn, True, dproj)
    dproj, dwv = _dnconv_bwd("dnconv_v_bwd", proj, CB_V, W["dn_conv"], dvc, False, dproj)
    gw["dn_conv"] = jnp.concatenate([dwq, dwk, dwv], axis=1)
    dattn = _matmul("swaout_bwd_x", dyb, W["w_swa_out"], "nt", f32)
    gw["w_swa_out"] = _matmul("swaout_bwd_w", S["attn"], dyb, "tn", f32)
    dsq, dkp, dkc, dvp, dvc_, dsinks = _attn_bwd("attn_bwd", S["sq"], S["sk"], proj, K["sinks"], dattn)
    dsk = _shift_add("attn_dk_join", dkc, dkp, f32)
    dswv = _shift_add("attn_dv_join", dvc_, dvp, bf16)
    dproj, dqn_w = _qkprep_bwd("qprep_bwd", proj, WB_SWQ, D, K["qn"], bd[0], tabs[0], dsq, dest_buf=dproj)
    dswk, dkn_w = _qkprep_bwd("kprep_bwd", proj, CB_SWK, LANE, K["kn"], bd[1], tabs[1], dsk)
    tail = jnp.concatenate([dab.astype(bf16), dswk, dswv, jnp.zeros((T, LANE), bf16)], axis=1)
    dproj = lax.dynamic_update_slice(dproj, tail, (0, CB_AB * LANE))
    dh1 = _matmul("proj_bwd_x", dproj, W["w_in"], "nt", f32)
    gw["w_in"] = _matmul("proj_bwd_w", S["h1t"], dproj, "nn", f32)
    dx, dnorm_mix, dsc1, dsh1 = _rowwise_bwd("normmod1_bwd", _normmod_fn, [(x, 0, D)], [K["norm_mix"], sc1, sh1], [(dh1, 0, D)], [f32],
                                             add_to_first=(dx1, 0, D))
    gs = {"norm_mix": dnorm_mix[0], "dn_a_log": dalog[0, :8], "dn_dt_bias": ddtb[0, :8], "dn_norm": ddn_norm[0],
          "swa_q_norm": dqn_w.reshape(16, 64).sum(0), "swa_k_norm": dkn_w.reshape(2, 64).sum(0), "swa_sinks": dsinks[0, :16],
          "norm_ffn": dnorm_ffn[0], "ffn_conv_b": dffn_b[0]}
    dmod = jnp.concatenate([dsh1, dsc1, dgt1, dsh2, dsc2, dgt2], axis=1)
    return dx, gw, gs, dmod, (pair_sums, hosted_out)


def _rope_tables(pos):
    T = pos.shape[0]
    half = 8
    inv = jnp.power(ROPE_THETA, -jnp.arange(half, dtype=f32) / half)
    ang = pos.astype(f32)[:, None] * inv
    cos, sin = jnp.cos(ang), jnp.sin(ang)
    z8, z48, o48 = jnp.zeros((T, 8), f32), jnp.zeros((T, 48), f32), jnp.ones((T, 48), f32)
    c64 = jnp.concatenate([cos, cos, o48], axis=1)
    s1 = jnp.concatenate([-sin, z8, z48], axis=1)
    s2 = jnp.concatenate([z8, sin, z48], axis=1)
    return tuple(jnp.tile(t, (1, 2)) for t in (c64, s1, s2))


_SMALL = (("norm_mix", D), ("dn_a_log", 8), ("dn_dt_bias", 8), ("dn_norm", 128), ("swa_q_norm", 64), ("swa_k_norm", 64),
          ("swa_sinks", 16), ("norm_ffn", D), ("ffn_conv_b", D_FF), ("b_ada", 6 * D))
_CONV = (("dn_conv", 4 * 3072), ("ffn_conv", 3 * D_FF))
_CONV_SHARD = (("dn_conv", 4 * 768), ("ffn_conv", 3 * 704))


def _pack_small(vals, spec):
    flat = jnp.concatenate([vals[nm].reshape(-1) for nm, _ in spec])
    rows = -(-flat.shape[0] // (8 * LANE)) * 8
    return jnp.pad(flat, (0, rows * LANE - flat.shape[0])).reshape(rows, LANE)


def _unpack_small(buf, spec):
    flat = buf.reshape(-1)
    out, off = {}, 0
    for nm, n in spec:
        out[nm] = flat[off:off + DEPTH * n].reshape(DEPTH, n)
        off += DEPTH * n
    return out


def kernel(x, c, positions, w_ada, b_ada, norm_mix, w_in, dn_conv, dn_a_log, dn_dt_bias, dn_norm, w_dn_out, swa_q_norm, swa_k_norm, swa_sinks, w_swa_out, w_o, norm_ffn, w_up, ffn_conv, ffn_conv_b, w_down, loss_target, m_w_ada, m_b_ada, m_norm_mix, m_w_in, m_dn_conv, m_dn_a_log, m_dn_dt_bias, m_dn_norm, m_w_dn_out, m_swa_q_norm, m_swa_k_norm, m_swa_sinks, m_w_swa_out, m_w_o, m_norm_ffn, m_w_up, m_ffn_conv, m_ffn_conv_b, m_w_down, v_w_ada, v_b_ada, v_norm_mix, v_w_in, v_dn_conv, v_dn_a_log, v_dn_dt_bias, v_dn_norm, v_w_dn_out, v_swa_q_norm, v_swa_k_norm, v_swa_sinks, v_w_swa_out, v_w_o, v_norm_ffn, v_w_up, v_ffn_conv, v_ffn_conv_b, v_w_down):
    weights = dict(w_ada=w_ada, b_ada=b_ada, norm_mix=norm_mix, w_in=w_in, dn_conv=dn_conv, dn_a_log=dn_a_log, dn_dt_bias=dn_dt_bias,
                   dn_norm=dn_norm, w_dn_out=w_dn_out, swa_q_norm=swa_q_norm, swa_k_norm=swa_k_norm, swa_sinks=swa_sinks,
                   w_swa_out=w_swa_out, w_o=w_o, norm_ffn=norm_ffn, w_up=w_up, ffn_conv=ffn_conv, ffn_conv_b=ffn_conv_b, w_down=w_down)
    mom_m = dict(w_ada=m_w_ada, b_ada=m_b_ada, norm_mix=m_norm_mix, w_in=m_w_in, dn_conv=m_dn_conv, dn_a_log=m_dn_a_log,
                 dn_dt_bias=m_dn_dt_bias, dn_norm=m_dn_norm, w_dn_out=m_w_dn_out, swa_q_norm=m_swa_q_norm, swa_k_norm=m_swa_k_norm,
                 swa_sinks=m_swa_sinks, w_swa_out=m_w_swa_out, w_o=m_w_o, norm_ffn=m_norm_ffn, w_up=m_w_up, ffn_conv=m_ffn_conv,
                 ffn_conv_b=m_ffn_conv_b, w_down=m_w_down)
    mom_v = dict(w_ada=v_w_ada, b_ada=v_b_ada, norm_mix=v_norm_mix, w_in=v_w_in, dn_conv=v_dn_conv, dn_a_log=v_dn_a_log,
                 dn_dt_bias=v_dn_dt_bias, dn_norm=v_dn_norm, w_dn_out=v_w_dn_out, swa_q_norm=v_swa_q_norm, swa_k_norm=v_swa_k_norm,
                 swa_sinks=v_swa_sinks, w_swa_out=v_w_swa_out, w_o=v_w_o, norm_ffn=v_norm_ffn, w_up=v_w_up, ffn_conv=v_ffn_conv,
                 ffn_conv_b=v_ffn_conv_b, w_down=v_w_down)
    order = ["w_ada", "b_ada", "norm_mix", "w_in", "dn_conv", "dn_a_log", "dn_dt_bias", "dn_norm", "w_dn_out", "swa_q_norm",
             "swa_k_norm", "swa_sinks", "w_swa_out", "w_o", "norm_ffn", "w_up", "ffn_conv", "ffn_conv_b", "w_down"]
    ax, ay, ac = lax.axis_index("x"), lax.axis_index("y"), lax.axis_index("c")
    chip = 2 * ax + ay
    dev = 4 * ax + 2 * ay + ac
    T = x.shape[1]
    xs = x[0]

    c_all = _allgather_all("gather_c", jnp.pad(c, ((0, 7), (0, 0)))).reshape(8, 8, D)[:, 0]
    c_act = _silu_rows("silu_c", jnp.pad(c_all, ((0, 8), (0, 0))))
    mod_sh = jnp.stack([
        _matmul("mod_fwd", c_act, w_ada[l].astype(bf16), "nn", f32,
                bias=lax.dynamic_slice(b_ada[l], (chip * 1536,), (1536,)).reshape(1, 1536)) for l in range(DEPTH)])
    mod_all = _allgather_all("gather_mod", mod_sh.reshape(DEPTH * 16 * 12, LANE)).reshape(8, DEPTH, 16, 1536)
    mod_me = jnp.concatenate([lax.dynamic_index_in_dim(mod_all[2 * s], dev, axis=1, keepdims=False) for s in range(4)], axis=1)

    tabs_q = _rope_tables(positions[0])
    tabs = (tabs_q, tabs_q)
    head = jnp.arange(LANE) // 64
    bd128 = (head[:, None] == head[None, :]).astype(f32) / 64.0
    bd = (bd128, bd128)

    conv_all = _allgather_all("gather_conv", _pack_small({"dn_conv": dn_conv, "ffn_conv": ffn_conv}, _CONV_SHARD))
    conv_parts = [_unpack_small(conv_all[2 * s], _CONV_SHARD) for s in range(4)]
    dn_conv_full = jnp.concatenate([p["dn_conv"].reshape(DEPTH, 4, 768) for p in conv_parts], axis=2)
    ffn_conv_full = jnp.concatenate([p["ffn_conv"].reshape(DEPTH, 3, 704) for p in conv_parts], axis=2)

    saved, Ws, Ks, mods = [], [], [], []
    h = xs
    shards = [[weights[nm][l].astype(bf16) for nm in _BIG] for l in range(DEPTH)]
    gathered = _gather_chips("gather_w", shards[0])
    for l in range(DEPTH):
        gathered = [lax.dynamic_update_index_in_dim(g, s, chip, 0) for g, s in zip(gathered, shards[l])]
        W = _assemble_weights(gathered)
        W["dn_conv"], W["ffn_conv"] = dn_conv_full[l], ffn_conv_full[l]
        K = _layer_consts({nm: weights[nm][l] for nm in ("norm_mix", "norm_ffn", "dn_a_log", "dn_dt_bias", "dn_norm", "swa_q_norm",
                                                          "swa_k_norm", "swa_sinks", "ffn_conv_b")})
        mod = tuple(mod_me[l, k * D:(k + 1) * D].reshape(1, D) for k in range(6))
        nxt = _hosted_gather(shards[l + 1]) if l + 1 < DEPTH else None
        h, S, arrived = _layer_fwd(h, mod, W, K, tabs, bd, hosted=nxt)
        if nxt is not None:
            gathered = _gather_forward("gather_w_pass", arrived)
        saved.append(S), Ws.append(W), Ks.append(K), mods.append(mod)

    loss_blk, dh = _loss("loss", h, loss_target[0])
    loss = lax.psum(loss_blk[0, 0], ("x", "y", "c"))

    grad_sh = [None] * DEPTH
    small = [None] * DEPTH
    dmods = [None] * DEPTH
    slots = None
    for l in reversed(range(DEPTH)):
        dh, gw, gs, dmod, (ps, b2) = _layer_bwd(dh, saved[l], mods[l], Ws[l], Ks[l], tabs, bd, carry=None if slots is None else (slots, ac))
        if slots is not None:
            grad_sh[l + 1] = dict(zip(_BIG, _rs_end(ps, b2, ac, chip)))
        slots = _grad_slots(gw)
        small[l], dmods[l] = dict(gs, dn_conv=gw["dn_conv"], ffn_conv=gw["ffn_conv"]), dmod[0]
    grad_sh[0] = dict(zip(_BIG, _reduce_scatter_multi(slots, ac, chip)))

    spec_g = _SMALL + _CONV
    vals = {nm: jnp.stack([small[l][nm] for l in range(DEPTH)]) for nm, _ in spec_g if nm != "b_ada"}
    vals["b_ada"] = jnp.stack(dmods)
    small_all = _allgather_all("gather_small", _pack_small(vals, spec_g))
    g_small = _unpack_small(_sum_leading("sum_small", small_all), spec_g)
    dmod_all = jnp.stack([_unpack_small(small_all[d], spec_g)["b_ada"] for d in range(8)])
    dmod_sh = lax.dynamic_slice(dmod_all, (0, 0, chip * 1536), (8, DEPTH, 1536))
    dmod_sh = jnp.pad(dmod_sh, ((0, 8), (0, 0), (0, 0))).astype(bf16)
    g_w_ada = jnp.stack([_matmul("mod_bwd_w", c_act, dmod_sh[:, l], "tn", f32) for l in range(DEPTH)])

    grads = {nm: g_small[nm] for nm, _ in _SMALL}
    grads["dn_conv"] = lax.dynamic_slice(g_small["dn_conv"].reshape(DEPTH, 4, 3072), (0, 0, chip * 768), (DEPTH, 4, 768))
    grads["ffn_conv"] = lax.dynamic_slice(g_small["ffn_conv"].reshape(DEPTH, 3, D_FF), (0, 0, chip * 704), (DEPTH, 3, 704))
    grads["w_ada"] = g_w_ada
    for nm in _BIG:
        grads[nm] = jnp.stack([grad_sh[l][nm] for l in range(DEPTH)])

    delta, new_m, new_v = {}, {}, {}
    for nm in ("w_ada", "dn_conv", "ffn_conv") + _BIG:
        delta[nm], new_m[nm], new_v[nm] = _adamw("adamw_" + nm, weights[nm], grads[nm], mom_m[nm], mom_v[nm])
    sm = [_pack_small({nm: t[nm] for nm, _ in _SMALL}, _SMALL) for t in (weights, grads, mom_m, mom_v)]
    for tgt, buf in zip((delta, new_m, new_v), _adamw("adamw_small", *sm)):
        tgt.update(_unpack_small(buf, _SMALL))

    return (loss, dh[None], *[grads[n] for n in order], *[delta[n] for n in order], *[new_m[n] for n in order], *[new_v[n] for n in order])
```

```python
import functools

import jax
import jax.numpy as jnp
import numpy as np
from jax import lax
from jax.experimental import pallas as pl
from jax.experimental.pallas import tpu as pltpu

f32 = jnp.float32
bf16 = jnp.bfloat16
SDS = jax.ShapeDtypeStruct
HI = lax.Precision.HIGHEST
MESH = pl.DeviceIdType.MESH

D = 1024
DEPTH = 4
EPS = 1e-6
DN_C = 64
SWA_B = 128
LANE = 128
ROPE_THETA = 500000.0
D_FF = 2816
IN_TOTAL = 7440
PROJ_W = 7680
CB_Q, CB_K, CB_V = 0, 8, 16
CB_AB, CB_SWK, CB_SWV = 56, 57, 58
WB_Z, WB_GA, WB_GB, WB_SWQ = 3, 4, 5, 6
TR = 256
COMM_W = 1024
COMM_ROWS = 4864
COMM_TR = 128
VMEM_BIG = 48 * 2 ** 20

ADAM_LR, ADAM_B1, ADAM_B2, ADAM_EPS, ADAM_WD, ADAM_STEP = 0.001, 0.9, 0.999, 1e-08, 0.01, 10


def _pcall(body, **kw):
    return pl.pallas_call(body, **kw)


def _cparams(vmem=None):
    return pltpu.CompilerParams(vmem_limit_bytes=vmem) if vmem else None


def _dot(a, b, ca, cb, precision=HI):
    return lax.dot_general(a, b, (((ca,), (cb,)), ((), ())), precision=precision, preferred_element_type=f32)


def _pick(n, cands):
    for c in cands:
        if n % c == 0:
            return c
    return n


def _tile(n, cap):
    if n <= cap:
        return n
    best = None
    for t in range(LANE, cap + 1, LANE):
        if n % t == 0:
            best = t
    assert best is not None, (n, cap)
    return best


def _matmul(name, a, b, mode, out_dtype, bias=None, out_slots=None, hosted=None):
    h_args, h_specs, h_shapes, h_sems = _hosted_parts(hosted)
    nh = len(h_args)
    if mode == "nn":
        (M, K), (_, N) = a.shape, b.shape
    elif mode == "nt":
        (M, K), (N, _) = a.shape, b.shape
    else:
        (K, M), (_, N) = a.shape, b.shape
    tm = _tile(M, 1536 if mode == "tn" else 1024)
    tn = N // out_slots if out_slots else _tile(N, 1536)
    tk = _tile(K, 512 if mode == "tn" else (1024 if K <= 1024 else 1536))
    nk = K // tk
    ca, cb = {"nn": (1, 0), "nt": (1, 1), "tn": (0, 0)}[mode]

    grid = (M // tm, N // tn, nk)

    def body(*refs):
        a_ref, b_ref = refs[:2]
        nb = 1 if bias is not None else 0
        bias_ref = refs[2] if nb else None
        h_ins, o_ref, h_outs = refs[2 + nb:2 + nb + nh], refs[2 + nb + nh], refs[3 + nb + nh:3 + nb + 2 * nh]
        rest = refs[3 + nb + 2 * nh:]
        if hosted is not None:
            step = (pl.program_id(0) * grid[1] + pl.program_id(1)) * grid[2] + pl.program_id(2)
            sems = rest[1:] if nk > 1 else rest

            @pl.when(step == 0)
            def _():
                hosted["start"](h_ins, h_outs, *sems)

        def finish(r):
            if bias is not None:
                r = r + bias_ref[...]
            o_ref[...] = r.astype(o_ref.dtype)

        part = _dot(a_ref[...].astype(bf16), b_ref[...].astype(bf16), ca, cb, precision=None)
        if nk == 1:
            finish(part)
        else:
            acc = rest[0]
            k = pl.program_id(2)

            @pl.when(k == 0)
            def _():
                acc[...] = part

            @pl.when(k > 0)
            def _():
                acc[...] += part

            @pl.when(k == nk - 1)
            def _():
                finish(acc[...])

        if hosted is not None:
            @pl.when(step == grid[0] * grid[1] * grid[2] - 1)
            def _():
                hosted["finish"](h_ins, h_outs, *sems)

    a_spec =pl.BlockSpec((tk, tm), lambda i, j, k: (k, i)) if mode == "tn" else pl.BlockSpec((tm, tk), lambda i, j, k: (i, k))
    b_spec = pl.BlockSpec((tn, tk), lambda i, j, k: (j, k)) if mode == "nt" else pl.BlockSpec((tk, tn), lambda i, j, k: (k, j))
    in_specs = [a_spec, b_spec]
    args = [a, b]
    if bias is not None:
        in_specs.append(pl.BlockSpec((1, tn), lambda i, j, k: (0, j)))
        args.append(bias)
    if out_slots:
        out_spec = pl.BlockSpec((None, tm, tn), lambda i, j, k: (j, i, 0))
        out_shape = SDS((out_slots, M, tn), out_dtype)
    else:
        out_spec = pl.BlockSpec((tm, tn), lambda i, j, k: (i, j))
        out_shape = SDS((M, N), out_dtype)
    if hosted is None:
        return _pcall(
            body, grid=grid, in_specs=in_specs, out_specs=out_spec, out_shape=out_shape,
            scratch_shapes=[pltpu.VMEM((tm, tn), f32)] if nk > 1 else [], compiler_params=_cparams(VMEM_BIG), name=name)(*args)
    outs = _pcall(
        body, grid=grid, in_specs=in_specs + h_specs, out_specs=[out_spec] + h_specs, out_shape=[out_shape] + h_shapes,
        scratch_shapes=([pltpu.VMEM((tm, tn), f32)] if nk > 1 else []) + h_sems, compiler_params=_cparams(VMEM_BIG), name=name)(*args, *h_args)
    return outs[0], list(outs[1:])


def _row_specs(rows, tr):
    return [pl.BlockSpec((tr, w), lambda i, j, off=off: (i, off + j)) for (_, off, w) in rows]


def _rowwise_fwd(name, fn, rows, vecs, out_widths, out_dtypes, nc=1, tr=TR, also_transposed=False):
    T = rows[0][0].shape[0]
    n_in = len(rows) + len(vecs)
    n_out = len(out_widths)

    def body(*refs):
        vals = [r[...].astype(f32) for r in refs[:n_in]]
        res = fn(*vals)
        for o_ref, r in zip(refs[n_in:n_in + n_out], res):
            o_ref[...] = r.astype(o_ref.dtype)
        if also_transposed:
            refs[n_in + n_out][...] = res[0].T.astype(refs[n_in + n_out].dtype)

    in_specs = _row_specs(rows, tr) + [pl.BlockSpec(v.shape, lambda i, j: (0, 0)) for v in vecs]
    out_specs = [pl.BlockSpec((tr, w), lambda i, j: (i, j)) for w in out_widths]
    out_shape = [SDS((T, w * nc), dt) for w, dt in zip(out_widths, out_dtypes)]
    if also_transposed:
        out_specs.append(pl.BlockSpec((out_widths[0], tr), lambda i, j: (j, i)))
        out_shape.append(SDS((out_widths[0] * nc, T), out_dtypes[0]))
    return _pcall(body, grid=(T // tr, nc), in_specs=in_specs, out_specs=out_specs, out_shape=out_shape, name=name)(
        *[r[0] for r in rows], *vecs)


def _into_buffer(dest_buf, n_inputs, out_index):
    if dest_buf is None:
        return [], [], {}
    return [dest_buf], [pl.BlockSpec(memory_space=pl.ANY)], {n_inputs: out_index}


def _rowwise_bwd(name, fn, rows, vecs, cts, drow_dtypes, nc=1, tr=TR, add_to_first=None, dest=None):
    T = rows[0][0].shape[0]
    n_r, n_v, n_c = len(rows), len(vecs), len(cts)
    n_add = 0 if add_to_first is None else 1
    keep = [k for k, dt in enumerate(drow_dtypes) if dt is not None]
    members = [] if dest is None else list(dest[1])
    plain = [pos for pos in range(len(keep)) if pos not in members]
    n_dest = 1 if dest is not None else 0
    n_in = n_r + n_v + n_c + n_add

    def body(*refs):
        vals = [r[...].astype(f32) for r in refs[:n_in]]
        outs = refs[n_in + (1 if dest is not None and dest[0] is not None else 0):]
        i, j = pl.program_id(0), pl.program_id(1)
        _, vjp = jax.vjp(fn, *vals[:n_r + n_v])
        grads = vjp(tuple(vals[n_r + n_v:n_r + n_v + n_c]))
        got = []
        for pos, k in enumerate(keep):
            g = grads[k]
            if n_add and pos == 0:
                g = g + vals[n_in - 1]
            got.append(g)
        if dest is not None:
            outs[0][...] = jnp.concatenate([got[pos] for pos in members], axis=1).astype(outs[0].dtype)
        for q, pos in enumerate(plain):
            outs[n_dest + q][...] = got[pos].astype(outs[n_dest + q].dtype)
        vec_outs = outs[n_dest + len(plain):]

        @pl.when((i == 0) & (j == 0))
        def _():
            for q in range(n_v):
                vec_outs[q][...] = jnp.zeros_like(vec_outs[q])

        for q in range(n_v):
            vec_outs[q][...] += grads[n_r + q]

    extra = [] if add_to_first is None else [add_to_first]
    in_specs = (_row_specs(rows, tr) + [pl.BlockSpec(v.shape, lambda i, j: (0, 0)) for v in vecs]
                + _row_specs(cts, tr) + _row_specs(extra, tr))
    args = [r[0] for r in rows] + list(vecs) + [c[0] for c in cts] + [e[0] for e in extra]
    out_specs, out_shape, aliases = [], [], {}
    if dest is not None:
        width = sum(rows[keep[pos]][2] for pos in members)
        col = dest[2]
        b_args, b_specs, aliases = _into_buffer(dest[0], n_in, 0)
        args, in_specs = args + b_args, in_specs + b_specs
        out_specs.append(pl.BlockSpec((tr, width), lambda i, j: (i, col + j)))
        out_shape.append(SDS((T, PROJ_W), bf16))
    out_specs += [pl.BlockSpec((tr, rows[keep[pos]][2]), lambda i, j: (i, j)) for pos in plain]
    out_shape += [SDS((T, rows[keep[pos]][2] * nc), drow_dtypes[keep[pos]]) for pos in plain]
    out_specs += [pl.BlockSpec(v.shape, lambda i, j: (0, 0)) for v in vecs]
    out_shape += [SDS(v.shape, f32) for v in vecs]
    return _pcall(body, grid=(T // tr, nc), in_specs=in_specs, out_specs=out_specs, out_shape=out_shape,
                  input_output_aliases=aliases, name=name)(*args)


def _normmod_fn(x, w, sc, sh):
    y = x * lax.rsqrt(jnp.mean(x * x, axis=-1, keepdims=True) + EPS)
    return ((y * w) * (1.0 + sc) + sh,)


def _resid_fn(x, t, gt):
    return (x + gt * t,)


def _merge_fn(ga, gb, ya, yb):
    return (jax.nn.sigmoid(ga) * ya + jax.nn.sigmoid(gb) * yb,)


def _dngate_fn(o, z, w):
    y = o * lax.rsqrt(jnp.mean(o * o, axis=-1, keepdims=True) + EPS)
    return ((y * w) * (z * jax.nn.sigmoid(z)),)


def _conv_taps(x, w_ref, taps, buf):
    w = lambda s: w_ref[taps - 1 - s:taps - s, :]
    row = lax.broadcasted_iota(jnp.int32, (8, x.shape[1]), 0)
    x8 = x[0:8]
    acc, acc8 = x * w(0), x8 * w(0)
    for s in range(1, taps):
        acc = acc + pltpu.roll(x, s, 0) * w(s)
        acc8 = acc8 + jnp.where(row >= s, pltpu.roll(x8, s, 0), 0.0) * w(s)
    buf[...] = acc
    buf[0:8, :] = acc8
    return buf[...]


def _conv_taps_bwd(x, dy, w_ref, dw_ref, taps, buf):
    T = x.shape[0]
    w = lambda s: w_ref[taps - 1 - s:taps - s, :]
    row = lax.broadcasted_iota(jnp.int32, (8, x.shape[1]), 0)
    dy_first, dy_last = dy[0:8], dy[T - 8:T]
    dx, dx_last = dy * w(0), dy_last * w(0)
    dw_ref[taps - 1:taps, :] = jnp.sum(dy * x, axis=0, keepdims=True)
    for s in range(1, taps):
        dx = dx + pltpu.roll(dy, T - s, 0) * w(s)
        dx_last = dx_last + jnp.where(row < 8 - s, pltpu.roll(dy_last, 8 - s, 0), 0.0) * w(s)
        xr = pltpu.roll(x, s, 0)
        wrapped = jnp.sum(jnp.where(row < s, dy_first * xr[0:8], 0.0), axis=0, keepdims=True)
        dw_ref[taps - 1 - s:taps - s, :] = jnp.sum(dy * xr, axis=0, keepdims=True) - wrapped
    buf[...] = dx
    buf[T - 8:T, :] = dx_last
    return buf[...]


def _dn_act(y, normalize):
    s = y * jax.nn.sigmoid(y)
    if normalize:
        s = s * lax.rsqrt(jnp.sum(s * s, axis=-1, keepdims=True) + EPS)
    return s


def _dnconv_fwd(name, proj, cb, w, normalize):
    T = proj.shape[0]

    def body(x_ref, w_ref, o_ref, buf):
        o_ref[...] = _dn_act(_conv_taps(x_ref[...], w_ref, 4, buf), normalize)

    return _pcall(
        body, grid=(8,), in_specs=[pl.BlockSpec((T, LANE), lambda j: (0, cb + j)), pl.BlockSpec((4, LANE), lambda j: (0, cb + j))],
        out_specs=pl.BlockSpec((T, LANE), lambda j: (0, j)), out_shape=SDS((T, 1024), f32), scratch_shapes=[pltpu.VMEM((T, LANE), f32)],
        compiler_params=_cparams(VMEM_BIG), name=name)(proj, w)


def _dnconv_bwd(name, proj, cb, w, dout, normalize, dest_buf):
    T = proj.shape[0]
    b_args, b_specs, aliases = _into_buffer(dest_buf, 3, 0)

    def body(x_ref, w_ref, do_ref, *rest):
        dx_ref, dw_ref, buf, buf2 = rest[len(b_args):]
        x = x_ref[...]
        y = _conv_taps(x, w_ref, 4, buf)
        _, vjp = jax.vjp(functools.partial(_dn_act, normalize=normalize), y)
        (dy,) = vjp(do_ref[...])
        dx_ref[...] = _conv_taps_bwd(x, dy, w_ref, dw_ref, 4, buf2).astype(dx_ref.dtype)

    return _pcall(
        body, grid=(8,),
        in_specs=[pl.BlockSpec((T, LANE), lambda j: (0, cb + j)), pl.BlockSpec((4, LANE), lambda j: (0, cb + j)),
                  pl.BlockSpec((T, LANE), lambda j: (0, j))] + b_specs,
        out_specs=[pl.BlockSpec((T, LANE), lambda j: (0, cb + j)), pl.BlockSpec((4, LANE), lambda j: (0, j))],
        out_shape=[SDS((T, PROJ_W), bf16), SDS((4, 1024), f32)], scratch_shapes=[pltpu.VMEM((T, LANE), f32)] * 2,
        input_output_aliases=aliases, compiler_params=_cparams(VMEM_BIG), name=name)(proj, w, dout, *b_args)


def _ffn_point(a, lin):
    return a * jax.nn.sigmoid(a) * lin


def _ffnact_fwd(name, up, w, b):
    T = up.shape[0]
    nblk = D_FF // LANE

    def body(a_ref, l_ref, w_ref, b_ref, o_ref, buf):
        a = _conv_taps(a_ref[...], w_ref, 3, buf) + b_ref[...]
        o_ref[...] = _ffn_point(a, l_ref[...]).astype(o_ref.dtype)

    return _pcall(
        body, grid=(nblk,),
        in_specs=[pl.BlockSpec((T, LANE), lambda j: (0, j)), pl.BlockSpec((T, LANE), lambda j: (0, nblk + j)),
                  pl.BlockSpec((3, LANE), lambda j: (0, j)), pl.BlockSpec((1, LANE), lambda j: (0, j))],
        out_specs=pl.BlockSpec((T, LANE), lambda j: (0, j)), out_shape=SDS((T, D_FF), bf16), scratch_shapes=[pltpu.VMEM((T, LANE), f32)],
        compiler_params=_cparams(VMEM_BIG), name=name)(up, up, w, b)


def _ffnact_bwd(name, up, w, b, dmid):
    T = up.shape[0]
    nblk = D_FF // LANE

    def body(a_ref, l_ref, w_ref, b_ref, dm_ref, da_ref, dl_ref, dw_ref, db_ref, buf, buf2):
        x = a_ref[...]
        a = _conv_taps(x, w_ref, 3, buf) + b_ref[...]
        _, vjp = jax.vjp(_ffn_point, a, l_ref[...])
        da, dl = vjp(dm_ref[...].astype(f32))
        dl_ref[...] = dl.astype(dl_ref.dtype)
        db_ref[...] = jnp.sum(da, axis=0, keepdims=True)
        da_ref[...] = _conv_taps_bwd(x, da, w_ref, dw_ref, 3, buf2).astype(da_ref.dtype)

    col = lambda r: pl.BlockSpec((r, LANE), lambda j: (0, j))
    return _pcall(
        body, grid=(nblk,),
        in_specs=[col(T), pl.BlockSpec((T, LANE), lambda j: (0, nblk + j)), col(3), col(1), col(T)],
        out_specs=[col(T), col(T), col(3), col(1)],
        out_shape=[SDS((T, D_FF), bf16), SDS((T, D_FF), bf16), SDS((3, D_FF), f32), SDS((1, D_FF), f32)],
        scratch_shapes=[pltpu.VMEM((T, LANE), f32)] * 2, compiler_params=_cparams(VMEM_BIG), name=name)(up, up, w, b, dmid)


def _bmm(a, b, ca, cb, precision=HI):
    return lax.dot_general(a, b, (((ca,), (cb,)), ((0,), (0,))), precision=precision, preferred_element_type=f32)


def _make_bdot(ca, cb):
    def raw(x, y, cx, cy):
        return _bmm(x.astype(bf16), y.astype(bf16), cx, cy, precision=None)

    @jax.custom_vjp
    def f(a, b):
        return raw(a, b, ca, cb)

    def fwd(a, b):
        return raw(a, b, ca, cb), (a, b)

    def bwd(res, dy):
        a, b = res
        if (ca, cb) == (2, 1):
            return raw(dy, b, 2, 2), raw(a, dy, 1, 1)
        if (ca, cb) == (2, 2):
            return raw(dy, b, 2, 1), raw(dy, a, 1, 1)
        return raw(b, dy, 2, 2), raw(a, dy, 2, 1)

    f.defvjp(fwd, bwd)
    return f


_bdot_nn, _bdot_nt, _bdot_tn = _make_bdot(2, 1), _make_bdot(2, 2), _make_bdot(1, 1)


def _pieces(a, n):
    out, r = [], a
    for _ in range(n):
        p = r.astype(bf16)
        out.append(p)
        r = r - p.astype(f32)
    return out


def _bmm_split(x, y, cx, cy, nx=2, ny=2, order=1):
    xs, ys = _pieces(x, nx), _pieces(y, ny)
    acc = None
    for i in reversed(range(nx)):
        for j in reversed(range(ny)):
            if i + j <= order:
                t = _bmm(xs[i], ys[j], cx, cy, precision=None)
                acc = t if acc is None else acc + t
    return acc


@jax.custom_vjp
def _solve_apply(X, r):
    return _bmm_split(X, r, 2, 1)


def _solve_apply_fwd(X, r):
    return _bmm_split(X, r, 2, 1), (X, r)


def _solve_apply_bwd(res, dy):
    X, r = res
    return _bmm_split(dy, r, 2, 2), _bmm_split(X, dy, 1, 1)


_solve_apply.defvjp(_solve_apply_fwd, _solve_apply_bwd)


def _lower_ones(H, C):
    ri = lax.broadcasted_iota(jnp.int32, (H, C, C), 1)
    ci = lax.broadcasted_iota(jnp.int32, (H, C, C), 2)
    return (ri >= ci).astype(f32)


def _cumsum_rows_raw(G):
    return _bmm_split(_lower_ones(G.shape[0], G.shape[1]), G, 2, 1, nx=1, ny=3, order=2)


@jax.custom_vjp
def _cumsum_rows(G):
    return _cumsum_rows_raw(G)


def _cumsum_rows_fwd(G):
    return _cumsum_rows_raw(G), None


def _cumsum_rows_bwd(_, dy):
    return (_bmm_split(_lower_ones(dy.shape[0], dy.shape[1]), dy, 1, 1, nx=1, ny=3, order=2),)


_cumsum_rows.defvjp(_cumsum_rows_fwd, _cumsum_rows_bwd)


def _tri_inverse_raw(L):
    H, C, _ = L.shape
    ri = lax.broadcasted_iota(jnp.int32, (C, C), 0)
    ci = lax.broadcasted_iota(jnp.int32, (C, C), 1)
    eye = jnp.broadcast_to((ri == ci).astype(f32)[None], (H, C, C))
    Dg = jnp.where(((ri >> 3) == (ci >> 3))[None], L, 0.0)
    D2 = _bmm_split(Dg, Dg, 2, 1)
    X = _bmm_split(_bmm_split(eye - Dg, eye + D2, 2, 1), eye + _bmm_split(D2, D2, 2, 1), 2, 1)
    for lg in range(3, C.bit_length() - 1):
        same = (ri >> (lg + 1)) == (ci >> (lg + 1))
        lower_left = same & (((ri >> lg) & 1) == 1) & (((ci >> lg) & 1) == 0)
        X = X - _bmm_split(_bmm_split(X, jnp.where(lower_left[None], L, 0.0), 2, 1), X, 2, 1)
    return X


@jax.custom_vjp
def _tri_inverse(L):
    return _tri_inverse_raw(L)


def _tri_inverse_fwd(L):
    X = _tri_inverse_raw(L)
    return X, X


def _tri_inverse_bwd(X, dX):
    return (-_bmm_split(_bmm_split(X, dX, 1, 1), X, 2, 2),)


_tri_inverse.defvjp(_tri_inverse_fwd, _tri_inverse_bwd)


@jax.custom_vjp
def _tri_inverse_known(L, X):
    return X


def _tri_inverse_known_fwd(L, X):
    return X, X


def _tri_inverse_known_bwd(X, dX):
    return _tri_inverse_bwd(X, dX)[0], jnp.zeros_like(X)


_tri_inverse_known.defvjp(_tri_inverse_known_fwd, _tri_inverse_known_bwd)


DN_NC = 2


def _gdn_chunk(q, k, v, ab, alog, dtb, S, X_known=None, keep_X=False):
    H, C, _ = q.shape
    NC, NH = ab.shape[0], H // ab.shape[0]
    lane = lax.broadcasted_iota(jnp.int32, (H, C, LANE), 2)
    head = lax.broadcasted_iota(jnp.int32, (H, C, LANE), 0) & (NH - 1)
    abb = jnp.concatenate([jnp.broadcast_to(ab[c][None], (NH, C, LANE)) for c in range(NC)], axis=0)
    a = jnp.sum(jnp.where(lane == head, abb, 0.0), axis=2, keepdims=True)
    b = jnp.sum(jnp.where(lane == head + 8, abb, 0.0), axis=2, keepdims=True)
    pick = lax.broadcasted_iota(jnp.int32, (H, 1, LANE), 2) == (lax.broadcasted_iota(jnp.int32, (H, 1, LANE), 0) & (NH - 1))
    al = jnp.sum(jnp.where(pick, alog[None], 0.0), axis=2, keepdims=True)
    db = jnp.sum(jnp.where(pick, dtb[None], 0.0), axis=2, keepdims=True)
    g = -jnp.exp(al) * jax.nn.softplus(a + db)
    beta = jax.nn.sigmoid(b)
    ri = lax.broadcasted_iota(jnp.int32, (C, C), 0)
    ci = lax.broadcasted_iota(jnp.int32, (C, C), 1)
    G = jnp.broadcast_to(g, (H, C, LANE))
    gc = _cumsum_rows(G)
    gi = _cumsum_rows(jnp.broadcast_to(g, (H, C, C)))
    decay = jnp.exp(jnp.where((ri >= ci)[None], gi - jnp.swapaxes(gi, 1, 2), -jnp.inf))
    qs = q * (LANE ** -0.5)
    kb = k * beta
    L = jnp.where((ri > ci)[None], _bdot_nt(kb, k) * decay, 0.0)
    X = _tri_inverse(L) if X_known is None else _tri_inverse_known(L, X_known)
    egc = jnp.exp(gc)
    u = _solve_apply(X, v * beta)
    w = _solve_apply(X, kb * egc)
    qk = _bdot_nt(qs, k) * decay
    g_last = jnp.sum(G, axis=1, keepdims=True)
    k_dec = k * jnp.exp(g_last - gc)
    q_dec = qs * egc
    e_last = jnp.exp(g_last)
    outs = []
    for c in range(NC):
        sl = slice(c * NH, (c + 1) * NH)
        v_new = u[sl] - _bdot_nn(w[sl], S)
        outs.append(_bdot_nn(q_dec[sl], S) + _bdot_nn(qk[sl], v_new))
        S = S * e_last[sl] + _bdot_tn(k_dec[sl], v_new)
    o = jnp.concatenate(outs, axis=0)
    return (o, S, X) if keep_X else (o, S)


def _heads(ref):
    return jnp.stack([ref[DN_C * c:DN_C * (c + 1), LANE * h:LANE * (h + 1)] for c in range(DN_NC) for h in range(8)], axis=0)


def _put_heads(ref, val):
    for c in range(DN_NC):
        for h in range(8):
            ref[DN_C * c:DN_C * (c + 1), LANE * h:LANE * (h + 1)] = val[8 * c + h]


def _chunk_rows(ref):
    return jnp.stack([ref[DN_C * c:DN_C * (c + 1), :] for c in range(DN_NC)], axis=0)


def _hosted_parts(hosted):
    if hosted is None:
        return [], [], [], []
    n = len(hosted["arrays"])
    return list(hosted["arrays"]), [HBM_SPEC] * n, list(hosted["out_shape"]), _dma_sems(hosted["n_sems"])


def _gdn_fwd(name, q, k, v, proj, alog, dtb, hosted=None):
    T = q.shape[0]
    R = DN_C * DN_NC
    N = T // R
    h_args, h_specs, h_shapes, h_sems = _hosted_parts(hosted)
    nh = len(h_args)

    def body(q_ref, k_ref, v_ref, ab_ref, al_ref, dt_ref, *rest):
        h_ins, (o_ref, sall_ref, xinv_ref), h_outs, s_scr, sems = rest[:nh], rest[nh:nh + 3], rest[nh + 3:2 * nh + 3], rest[2 * nh + 3], rest[2 * nh + 4:]
        step = pl.program_id(0)

        @pl.when(step == 0)
        def _():
            s_scr[...] = jnp.zeros_like(s_scr)
            if hosted is not None:
                hosted["start"](h_ins, h_outs, *sems)

        S = s_scr[...]
        sall_ref[...] = S
        o, S_new, X = _gdn_chunk(_heads(q_ref), _heads(k_ref), _heads(v_ref), _chunk_rows(ab_ref), al_ref[...], dt_ref[...], S, keep_X=True)
        _put_heads(o_ref, o)
        s_scr[...] = S_new
        xinv_ref[...] = X

        if hosted is not None:
            @pl.when(step == N - 1)
            def _():
                hosted["finish"](h_ins, h_outs, *sems)

    blk = pl.BlockSpec((R, 8 * LANE), lambda n: (n, 0))
    vec = pl.BlockSpec((1, LANE), lambda n: (0, 0))
    state = pl.BlockSpec((None, 8, LANE, LANE), lambda n: (n, 0, 0, 0))
    xinv = pl.BlockSpec((None, 8 * DN_NC, DN_C, DN_C), lambda n: (n, 0, 0, 0))
    outs = _pcall(
        body, grid=(N,), in_specs=[blk, blk, blk, pl.BlockSpec((R, LANE), lambda n: (n, CB_AB)), vec, vec] + h_specs,
        out_specs=[blk, state, xinv] + h_specs,
        out_shape=[SDS((T, 1024), f32), SDS((N, 8, LANE, LANE), f32), SDS((N, 8 * DN_NC, DN_C, DN_C), f32)] + h_shapes,
        scratch_shapes=[pltpu.VMEM((8, LANE, LANE), f32)] + h_sems, name=name)(q, k, v, proj, alog, dtb, *h_args)
    return outs[0], outs[1], outs[2], list(outs[3:])


def _gdn_bwd(name, q, k, v, proj, alog, dtb, sall, xinv, do, hosted=None):
    T = q.shape[0]
    R = DN_C * DN_NC
    N = T // R
    h_args, h_specs, h_shapes, h_sems = _hosted_parts(hosted)
    nh = len(h_args)

    def body(q_ref, k_ref, v_ref, ab_ref, al_ref, dt_ref, s_ref, x_ref, do_ref, *rest):
        h_ins, h_outs, ds_scr, sems = rest[:nh], rest[nh + 6:2 * nh + 6], rest[2 * nh + 6], rest[2 * nh + 7:]
        dq_ref, dk_ref, dv_ref, dab_ref, dal_ref, ddt_ref = rest[nh:nh + 6]
        step = pl.program_id(0)

        @pl.when(step == 0)
        def _():
            ds_scr[...] = jnp.zeros_like(ds_scr)
            dal_ref[...] = jnp.zeros_like(dal_ref)
            ddt_ref[...] = jnp.zeros_like(ddt_ref)
            if hosted is not None:
                hosted["start"](h_ins, h_outs, *sems)

        _, vjp = jax.vjp(functools.partial(_gdn_chunk, X_known=x_ref[...]), _heads(q_ref), _heads(k_ref), _heads(v_ref), _chunk_rows(ab_ref),
                         al_ref[...], dt_ref[...], s_ref[...])
        dq, dk, dv, dab, dal, ddt, dS = vjp((_heads(do_ref), ds_scr[...]))
        _put_heads(dq_ref, dq)
        _put_heads(dk_ref, dk)
        _put_heads(dv_ref, dv)
        ds_scr[...] = dS
        for c in range(DN_NC):
            dab_ref[DN_C * c:DN_C * (c + 1), :] = dab[c]
        dal_ref[...] += dal
        ddt_ref[...] += ddt

        if hosted is not None:
            @pl.when(step == N - 1)
            def _():
                hosted["finish"](h_ins, h_outs, *sems)

    blk = pl.BlockSpec((R, 8 * LANE), lambda n: (N - 1 - n, 0))
    vec = pl.BlockSpec((1, LANE), lambda n: (0, 0))
    state = pl.BlockSpec((None, 8, LANE, LANE), lambda n: (N - 1 - n, 0, 0, 0))
    outs = _pcall(
        body, grid=(N,),
        in_specs=[blk, blk, blk, pl.BlockSpec((R, LANE), lambda n: (N - 1 - n, CB_AB)), vec, vec, state,
                  pl.BlockSpec((None, 8 * DN_NC, DN_C, DN_C), lambda n: (N - 1 - n, 0, 0, 0)), blk] + h_specs,
        out_specs=[blk, blk, blk, pl.BlockSpec((R, LANE), lambda n: (N - 1 - n, 0)), vec, vec] + h_specs,
        out_shape=[SDS((T, 1024), f32)] * 3 + [SDS((T, LANE), f32), SDS((1, LANE), f32), SDS((1, LANE), f32)] + h_shapes,
        scratch_shapes=[pltpu.VMEM((8, LANE, LANE), f32)] + h_sems, name=name)(q, k, v, proj, alog, dtb, sall, xinv, do, *h_args)
    return tuple(outs[:6]), list(outs[6:])


def _segmean_raw(x2, bd):
    return jnp.concatenate([_dot(x2[:, LANE * j:LANE * (j + 1)], bd, 1, 0) for j in range(x2.shape[1] // LANE)], axis=1)


@jax.custom_vjp
def _segmean(x2, bd):
    return _segmean_raw(x2, bd)


def _segmean_fwd(x2, bd):
    return _segmean_raw(x2, bd), bd


def _segmean_bwd(bd, dy):
    return _segmean_raw(dy, bd), jnp.zeros_like(bd)


_segmean.defvjp(_segmean_fwd, _segmean_bwd)


def _qknorm_fn(x, w, bd):
    return x * lax.rsqrt(_segmean(x * x, bd) + EPS) * w


def _rope_apply(xn, c, s1, s2):
    W = xn.shape[1]
    return xn * c + pltpu.roll(xn, W - 8, 1) * s1 + pltpu.roll(xn, 8, 1) * s2


def _rope_apply_t(d, c, s1, s2):
    W = d.shape[1]
    return d * c + pltpu.roll(d * s1, 8, 1) + pltpu.roll(d * s2, W - 8, 1)


def _rope_tiles(refs, width):
    return [jnp.tile(r[...], (1, width // LANE)) for r in refs]


def _qkprep_fwd(name, proj, wb, width, w, bd, tabs):
    T = proj.shape[0]
    tr = _pick(T, (256, 128))

    def body(x_ref, w_ref, bd_ref, c_ref, s1_ref, s2_ref, o_ref):
        xn = _qknorm_fn(x_ref[...], w_ref[...], bd_ref[...])
        o_ref[...] = _rope_apply(xn, *_rope_tiles((c_ref, s1_ref, s2_ref), width))

    row0 = pl.BlockSpec((tr, width), lambda i: (i, 0))
    tab = pl.BlockSpec((tr, LANE), lambda i: (i, 0))
    full = lambda a: pl.BlockSpec(a.shape, lambda i: (0, 0))
    return _pcall(
        body, grid=(T // tr,), in_specs=[pl.BlockSpec((tr, width), lambda i: (i, wb)), full(w), full(bd), tab, tab, tab],
        out_specs=row0, out_shape=SDS((T, width), f32), name=name)(proj, w, bd, *tabs)


def _qkprep_bwd(name, proj, wb, width, w, bd, tabs, dout, dest_buf="none"):
    T = proj.shape[0]
    tr = _pick(T, (256, 128))
    into = not isinstance(dest_buf, str)
    b_args, b_specs, aliases = _into_buffer(dest_buf, 7, 0) if into else ([], [], {})

    def body(x_ref, w_ref, bd_ref, c_ref, s1_ref, s2_ref, do_ref, *rest):
        dx_ref, dw_ref = rest[len(b_args):]
        i = pl.program_id(0)
        dxn = _rope_apply_t(do_ref[...], *_rope_tiles((c_ref, s1_ref, s2_ref), width))
        bd = bd_ref[...]
        _, vjp = jax.vjp(lambda x, w_: _qknorm_fn(x, w_, bd), x_ref[...], w_ref[...])
        dx, dw = vjp(dxn)
        dx_ref[...] = dx.astype(dx_ref.dtype)

        @pl.when(i == 0)
        def _():
            dw_ref[...] = jnp.zeros_like(dw_ref)

        dw_ref[...] += dw

    row0 = pl.BlockSpec((tr, width), lambda i: (i, 0))
    tab = pl.BlockSpec((tr, LANE), lambda i: (i, 0))
    full = lambda a: pl.BlockSpec(a.shape, lambda i: (0, 0))
    dx_spec = pl.BlockSpec((tr, width), lambda i: (i, wb)) if into else row0
    dx_shape = SDS((T, PROJ_W), bf16) if into else SDS((T, width), bf16)
    return _pcall(
        body, grid=(T // tr,), in_specs=[pl.BlockSpec((tr, width), lambda i: (i, wb)), full(w), full(bd), tab, tab, tab, row0] + b_specs,
        out_specs=[dx_spec, full(w)], out_shape=[dx_shape, SDS(w.shape, f32)], input_output_aliases=aliases, name=name)(
            proj, w, bd, *tabs, dout, *b_args)


def _make_dot16(ca, cb):
    def raw(x, y, cx, cy):
        return _dot(x.astype(bf16), y.astype(bf16), cx, cy, precision=None)

    @jax.custom_vjp
    def f(a, b):
        return raw(a, b, ca, cb)

    def fwd(a, b):
        return raw(a, b, ca, cb), (a, b)

    def bwd(res, dy):
        a, b = res
        if (ca, cb) == (1, 0):
            return raw(dy, b, 1, 1), raw(a, dy, 0, 0)
        return raw(dy, b, 1, 0), raw(dy, a, 0, 0)

    f.defvjp(fwd, bwd)
    return f


_dot16_nn, _dot16_nt = _make_dot16(1, 0), _make_dot16(1, 1)


def _attn_bias():
    qi = jnp.arange(8 * SWA_B) % SWA_B
    kj = jnp.arange(2 * SWA_B)
    rel = qi[:, None] + SWA_B - kj[None, :]
    valid = (rel >= 0) & (rel < SWA_B)
    neg = jnp.float32(-jnp.inf)
    return jnp.stack([jnp.where(valid & (kj[None, :] >= SWA_B), 0.0, neg), jnp.where(valid, 0.0, neg)]).astype(f32)


def _attn_group(qg, kb, vb, sinks, bias, hk):
    R = qg.shape[0]
    s = _dot16_nt(qg, kb) * 0.125 + bias
    head = (lax.broadcasted_iota(jnp.int32, (R, LANE), 0) >> 7) + 8 * hk
    lane = lax.broadcasted_iota(jnp.int32, (R, LANE), 1)
    sink = jnp.sum(jnp.where(lane == head, jnp.broadcast_to(sinks, (R, LANE)), 0.0), axis=1, keepdims=True)
    m = lax.stop_gradient(jnp.maximum(jnp.max(s, axis=1, keepdims=True), sink))
    p = jnp.exp(s - m)
    denom = jnp.sum(p, axis=1, keepdims=True) + jnp.exp(sink - m)
    return _dot16_nn(p / denom, vb)


def _group_rows(ref, hk):
    return jnp.concatenate([ref[:, 64 * (8 * hk + g):64 * (8 * hk + g + 1)] for g in range(8)], axis=0)


def _put_group(ref, hk, val):
    for g in range(8):
        ref[:, 64 * (8 * hk + g):64 * (8 * hk + g + 1)] = val[SWA_B * g:SWA_B * (g + 1)].astype(ref.dtype)


def _attn_specs():
    qs = pl.BlockSpec((SWA_B, 1024), lambda i: (i, 0))
    cur = pl.BlockSpec((SWA_B, LANE), lambda i: (i, 0))
    prev = pl.BlockSpec((SWA_B, LANE), lambda i: (jnp.maximum(i - 1, 0), 0))
    vcur = pl.BlockSpec((SWA_B, LANE), lambda i: (i, CB_SWV))
    vprev = pl.BlockSpec((SWA_B, LANE), lambda i: (jnp.maximum(i - 1, 0), CB_SWV))
    vec = pl.BlockSpec((1, LANE), lambda i: (0, 0))
    bias = pl.BlockSpec((None, 8 * SWA_B, 2 * SWA_B), lambda i: (jnp.minimum(i, 1), 0, 0))
    return qs, cur, prev, vcur, vprev, vec, bias


def _attn_fwd(name, sq, sk, proj, sinks, bias):
    T = sq.shape[0]

    def body(q_ref, kp_ref, kc_ref, vp_ref, vc_ref, sk_ref, b_ref, o_ref):
        sinks_v, bias_v = sk_ref[...], b_ref[...]
        for hk in range(2):
            ks = slice(64 * hk, 64 * hk + 64)
            kb = jnp.concatenate([kp_ref[:, ks], kc_ref[:, ks]], axis=0)
            vb = jnp.concatenate([vp_ref[:, ks], vc_ref[:, ks]], axis=0)
            _put_group(o_ref, hk, _attn_group(_group_rows(q_ref, hk), kb, vb, sinks_v, bias_v, hk))

    qs, cur, prev, vcur, vprev, vec, bspec = _attn_specs()
    return _pcall(body, grid=(T // SWA_B,), in_specs=[qs, prev, cur, vprev, vcur, vec, bspec], out_specs=qs,
                  out_shape=SDS((T, 1024), bf16), name=name)(sq, sk, sk, proj, proj, sinks, bias)


def _attn_bwd(name, sq, sk, proj, sinks, bias, do):
    T = sq.shape[0]

    def body(q_ref, kp_ref, kc_ref, vp_ref, vc_ref, sk_ref, b_ref, do_ref, dq_ref, dkp_ref, dkc_ref, dvp_ref, dvc_ref, dsk_ref):
        @pl.when(pl.program_id(0) == 0)
        def _():
            dsk_ref[...] = jnp.zeros_like(dsk_ref)

        sinks_v, bias_v = sk_ref[...], b_ref[...]
        dsk = jnp.zeros((1, LANE), f32)
        for hk in range(2):
            ks = slice(64 * hk, 64 * hk + 64)
            kb = jnp.concatenate([kp_ref[:, ks], kc_ref[:, ks]], axis=0)
            vb = jnp.concatenate([vp_ref[:, ks], vc_ref[:, ks]], axis=0)
            _, vjp = jax.vjp(functools.partial(_attn_group, bias=bias_v, hk=hk), _group_rows(q_ref, hk), kb, vb, sinks_v)
            dq, dkb, dvb, ds_ = vjp(_group_rows(do_ref, hk))
            _put_group(dq_ref, hk, dq)
            dsk = dsk + ds_
            dkp_ref[:, ks] = dkb[:SWA_B]
            dkc_ref[:, ks] = dkb[SWA_B:]
            dvp_ref[:, ks] = dvb[:SWA_B]
            dvc_ref[:, ks] = dvb[SWA_B:]
        dsk_ref[...] += dsk

    qs, cur, prev, vcur, vprev, vec, bspec = _attn_specs()
    return _pcall(
        body, grid=(T // SWA_B,), in_specs=[qs, prev, cur, vprev, vcur, vec, bspec, qs], out_specs=[qs, cur, cur, cur, cur, vec],
        out_shape=[SDS((T, 1024), f32)] + [SDS((T, LANE), f32)] * 4 + [SDS((1, LANE), f32)], name=name)(sq, sk, sk, proj, proj, sinks, bias, do)


def _shift_add(name, cur, prev, out_dtype):
    T = cur.shape[0]

    def body(c_ref, p_ref, o_ref):
        o_ref[0:T - SWA_B, :] = (c_ref[0:T - SWA_B, :] + p_ref[SWA_B:T, :]).astype(o_ref.dtype)
        o_ref[T - SWA_B:T, :] = c_ref[T - SWA_B:T, :].astype(o_ref.dtype)

    return _pcall(body, out_shape=SDS((T, LANE), out_dtype), name=name)(cur, prev)


def _loss(name, y, tgt):
    T = y.shape[0]

    def body(y_ref, t_ref, l_ref, dy_ref):
        @pl.when(pl.program_id(0) == 0)
        def _():
            l_ref[...] = jnp.zeros_like(l_ref)

        d = y_ref[...] - t_ref[...]
        l_ref[...] += jnp.sum(d * d) * (0.5 / D)
        dy_ref[...] = d * (1.0 / D)

    row = pl.BlockSpec((TR, D), lambda i: (i, 0))
    return _pcall(body, grid=(T // TR,), in_specs=[row, row], out_specs=[pl.BlockSpec((8, LANE), lambda i: (0, 0)), row],
                  out_shape=[SDS((8, LANE), f32), SDS((T, D), f32)], name=name)(y, tgt)


def _adamw(name, w, g, m, v):
    shape = w.shape
    C = shape[-1]
    R = int(np.prod(shape[:-1]))
    tr = _pick(R, (128, 64, 16, 8))
    bc1 = np.float32(1.0 - ADAM_B1 ** ADAM_STEP)
    bc2 = np.float32(1.0 - ADAM_B2 ** ADAM_STEP)

    def body(w_ref, g_ref, m_ref, v_ref, d_ref, mo_ref, vo_ref):
        g_ = g_ref[...]
        m_ = ADAM_B1 * m_ref[...] + (1.0 - ADAM_B1) * g_
        v_ = ADAM_B2 * v_ref[...] + (1.0 - ADAM_B2) * (g_ * g_)
        d_ref[...] = -ADAM_LR * ((m_ / bc1) / (jnp.sqrt(v_ / bc2) + ADAM_EPS) + ADAM_WD * w_ref[...])
        mo_ref[...] = m_
        vo_ref[...] = v_

    blk = pl.BlockSpec((tr, C), lambda i: (i, 0))
    outs = _pcall(body, grid=(R // tr,), in_specs=[blk] * 4, out_specs=[blk] * 3, out_shape=[SDS((R, C), f32)] * 3,
                  compiler_params=_cparams(VMEM_BIG), name=name)(*[t.reshape(R, C) for t in (w, g, m, v)])
    return [o.reshape(shape) for o in outs]


def _silu_rows(name, x):
    def body(x_ref, o_ref):
        t = x_ref[...]
        o_ref[...] = (t * jax.nn.sigmoid(t)).astype(o_ref.dtype)

    return _pcall(body, out_shape=SDS(x.shape, bf16), name=name)(x)


def _sum_leading(name, x):
    n = x.shape[0]

    def body(x_ref, o_ref):
        acc = x_ref[0]
        for k in range(1, n):
            acc = acc + x_ref[k]
        o_ref[...] = acc

    tr = x.shape[1] if x.size * 4 <= 8 * 2 ** 20 else _pick(x.shape[1], (COMM_TR, 8))
    return _pcall(body, grid=(x.shape[1] // tr,), in_specs=[pl.BlockSpec((n, tr, x.shape[2]), lambda i: (0, i, 0))],
                  out_specs=pl.BlockSpec((tr, x.shape[2]), lambda i: (i, 0)), out_shape=SDS(x.shape[1:], x.dtype), name=name)(x)


def _add_my_half(name, g4, b1, c):
    _, R, W = g4.shape
    nblk = (R // 2) // COMM_TR

    def body(c_ref, g_ref, b_ref, o_ref):
        o_ref[...] = g_ref[...] + b_ref[...]

    grid_spec = pltpu.PrefetchScalarGridSpec(
        num_scalar_prefetch=1, grid=(4, nblk),
        in_specs=[pl.BlockSpec((None, COMM_TR, W), lambda s, i, c_ref: (s, c_ref[0] * nblk + i, 0)),
                  pl.BlockSpec((None, COMM_TR, W), lambda s, i, c_ref: (s, i, 0))],
        out_specs=pl.BlockSpec((None, COMM_TR, W), lambda s, i, c_ref: (s, i, 0)))
    return _pcall(body, grid_spec=grid_spec, out_shape=SDS((4, R // 2, W), f32), name=name)(c.reshape(1), g4, b1)


HBM_SPEC = pl.BlockSpec(memory_space=pltpu.HBM)


def _position():
    x, y, c = lax.axis_index("x"), lax.axis_index("y"), lax.axis_index("c")
    return x, y, c, [(1 - x, y), (x, 1 - y), (1 - x, 1 - y)]


def _remote(src, dst, send_sems, recv_sems, k, to):
    return pltpu.make_async_remote_copy(src_ref=src, dst_ref=dst, send_sem=send_sems.at[k], recv_sem=recv_sems.at[k],
                                        device_id=to, device_id_type=MESH)


def _allgather_chips(name, buf):
    R, W = buf.shape
    Rh = R // 2

    def body(in_ref, out_ref, send_sems, recv_sems, local_sem):
        x, y, c, chips = _position()
        me = 2 * x + y
        sib = (x, y, 1 - c)
        half = pl.ds(pl.multiple_of(c * Rh, 32), Rh)
        ohalf = pl.ds(pl.multiple_of((1 - c) * Rh, 32), Rh)
        mine = pltpu.make_async_copy(in_ref, out_ref.at[me], local_sem)
        mine.start()
        first = [_remote(in_ref.at[half], out_ref.at[me, half], send_sems, recv_sems, j, (cx, cy, c)) for j, (cx, cy) in enumerate(chips)]
        for cp in first:
            cp.start()
        passed = []
        for j, (cx, cy) in enumerate(chips):
            rows = out_ref.at[2 * cx + cy, half]
            _remote(rows, rows, send_sems, recv_sems, j, (cx, cy, c)).wait_recv()
            cp = _remote(rows, rows, send_sems, recv_sems, 3 + j, sib)
            cp.start()
            passed.append(cp)
        for j, (cx, cy) in enumerate(chips):
            rows = out_ref.at[2 * cx + cy, ohalf]
            _remote(rows, rows, send_sems, recv_sems, 3 + j, sib).wait_recv()
        for cp in first + passed:
            cp.wait_send()
        mine.wait()

    return _pcall(body, in_specs=[HBM_SPEC], out_specs=HBM_SPEC, out_shape=SDS((4, R, W), buf.dtype),
                  scratch_shapes=[pltpu.SemaphoreType.DMA((6,)), pltpu.SemaphoreType.DMA((6,)), pltpu.SemaphoreType.DMA], name=name)(buf)


def _swap_halves(name, g4):
    _, R, W = g4.shape
    Rh = R // 2

    def body(in_ref, out_ref, send_sems, recv_sems):
        x, y, c, _ = _position()
        ohalf = pl.ds(pl.multiple_of((1 - c) * Rh, 32), Rh)
        cp = _remote(in_ref.at[:, ohalf, :], out_ref, send_sems, recv_sems, 0, (x, y, 1 - c))
        cp.start()
        cp.wait()

    return _pcall(body, in_specs=[HBM_SPEC], out_specs=HBM_SPEC, out_shape=SDS((4, Rh, W), g4.dtype),
                  scratch_shapes=[pltpu.SemaphoreType.DMA((1,)), pltpu.SemaphoreType.DMA((1,))], name=name)(g4)


def _scatter_chips(name, p4):
    _, Rh, W = p4.shape

    def body(in_ref, out_ref, send_sems, recv_sems, local_sem):
        x, y, c, chips = _position()
        me = 2 * x + y
        mine = pltpu.make_async_copy(in_ref.at[me], out_ref.at[me], local_sem)
        mine.start()
        sends = [_remote(in_ref.at[2 * cx + cy], out_ref.at[me], send_sems, recv_sems, j, (cx, cy, c)) for j, (cx, cy) in enumerate(chips)]
        for cp in sends:
            cp.start()
        for j, (cx, cy) in enumerate(chips):
            slot = out_ref.at[2 * cx + cy]
            _remote(slot, slot, send_sems, recv_sems, j, (cx, cy, c)).wait_recv()
        for cp in sends:
            cp.wait_send()
        mine.wait()

    return _pcall(body, in_specs=[HBM_SPEC], out_specs=HBM_SPEC, out_shape=SDS((4, Rh, W), p4.dtype),
                  scratch_shapes=[pltpu.SemaphoreType.DMA((3,)), pltpu.SemaphoreType.DMA((3,)), pltpu.SemaphoreType.DMA], name=name)(p4)


def _join_halves(name, r):
    Rh, W = r.shape

    def body(in_ref, out_ref, send_sems, recv_sems, local_sem):
        x, y, c, _ = _position()
        mine = pltpu.make_async_copy(in_ref, out_ref.at[c], local_sem)
        mine.start()
        cp = _remote(in_ref, out_ref.at[c], send_sems, recv_sems, 0, (x, y, 1 - c))
        cp.start()
        _remote(in_ref, out_ref.at[1 - c], send_sems, recv_sems, 0, (x, y, 1 - c)).wait_recv()
        cp.wait_send()
        mine.wait()

    return _pcall(body, in_specs=[HBM_SPEC], out_specs=HBM_SPEC, out_shape=SDS((2, Rh, W), r.dtype),
                  scratch_shapes=[pltpu.SemaphoreType.DMA((1,)), pltpu.SemaphoreType.DMA((1,)), pltpu.SemaphoreType.DMA], name=name)(r)


def _allgather_all(name, buf):
    r, W = buf.shape

    def body(in_ref, out_ref, send_sems, recv_sems, local_sem):
        x, y, c, _ = _position()
        me = 4 * x + 2 * y + c
        mine = pltpu.make_async_copy(in_ref, out_ref.at[me], local_sem)
        mine.start()
        peers = []
        for mk in range(1, 8):
            mx, my, mc = (mk >> 2) & 1, (mk >> 1) & 1, mk & 1
            px = 1 - x if mx else x
            py = 1 - y if my else y
            pc = 1 - c if mc else c
            peers.append((px, py, pc))
        sends = [_remote(in_ref, out_ref.at[me], send_sems, recv_sems, k, p) for k, p in enumerate(peers)]
        for cp in sends:
            cp.start()
        for k, (px, py, pc) in enumerate(peers):
            slot = out_ref.at[4 * px + 2 * py + pc]
            _remote(slot, slot, send_sems, recv_sems, k, (px, py, pc)).wait_recv()
        for cp in sends:
            cp.wait_send()
        mine.wait()

    return _pcall(body, in_specs=[HBM_SPEC], out_specs=HBM_SPEC, out_shape=SDS((8, r, W), buf.dtype),
                  scratch_shapes=[pltpu.SemaphoreType.DMA((7,)), pltpu.SemaphoreType.DMA((7,)), pltpu.SemaphoreType.DMA], name=name)(buf)


def _reduce_scatter(g4, c, tag):
    b1 = _swap_halves("rs_swap_" + tag, g4)
    p4 = _add_my_half("rs_pair_" + tag, g4, b1, c)
    b2 = _scatter_chips("rs_scatter_" + tag, p4)
    r = _sum_leading("rs_sum_" + tag, b2)
    full = _join_halves("rs_join_" + tag, r)
    return full.reshape(g4.shape[1], g4.shape[2])


def _dma_sems(n):
    return [pltpu.SemaphoreType.DMA((n,)), pltpu.SemaphoreType.DMA((n,))]


def _gather_chips(name, shards):
    n = len(shards)

    def body(*refs):
        ins, outs = refs[:n], refs[n:2 * n]
        send_sems, recv_sems = refs[2 * n:]
        x, y, c, chips = _position()
        me = 2 * x + y
        sib = (x, y, 1 - c)
        sends, halves = [], []
        for i in range(n):
            rh = ins[i].shape[0] // 2
            halves.append((pl.ds(pl.multiple_of(c * rh, 16), rh), pl.ds(pl.multiple_of((1 - c) * rh, 16), rh)))
        for i in range(n):
            for j, (cx, cy) in enumerate(chips):
                cp = _remote(ins[i].at[halves[i][0]], outs[i].at[me, halves[i][0]], send_sems, recv_sems, 6 * i + j, (cx, cy, c))
                cp.start()
                sends.append(cp)
        for j, (cx, cy) in enumerate(chips):
            for i in range(n):
                rows = outs[i].at[2 * cx + cy, halves[i][0]]
                _remote(rows, rows, send_sems, recv_sems, 6 * i + j, (cx, cy, c)).wait_recv()
                cp = _remote(rows, rows, send_sems, recv_sems, 6 * i + 3 + j, sib)
                cp.start()
                sends.append(cp)
        for j, (cx, cy) in enumerate(chips):
            for i in range(n):
                rows = outs[i].at[2 * cx + cy, halves[i][1]]
                _remote(rows, rows, send_sems, recv_sems, 6 * i + 3 + j, sib).wait_recv()
        for cp in sends:
            cp.wait_send()

    return _pcall(body, in_specs=[HBM_SPEC] * n, out_specs=[HBM_SPEC] * n, out_shape=[SDS((4,) + s.shape, s.dtype) for s in shards],
                  scratch_shapes=_dma_sems(6 * n), name=name)(*shards)


def _swap_halves_multi(name, slots):
    n = len(slots)

    def body(*refs):
        ins, outs = refs[:n], refs[n:2 * n]
        send_sems, recv_sems = refs[2 * n:]
        x, y, c, _ = _position()
        cps = []
        for i in range(n):
            rh = ins[i].shape[1] // 2
            ohalf = pl.ds(pl.multiple_of((1 - c) * rh, 8), rh)
            cp = _remote(ins[i].at[:, ohalf, :], outs[i], send_sems, recv_sems, i, (x, y, 1 - c))
            cp.start()
            cps.append(cp)
        for cp in cps:
            cp.wait()

    return _pcall(body, in_specs=[HBM_SPEC] * n, out_specs=[HBM_SPEC] * n,
                  out_shape=[SDS((4, s.shape[1] // 2, s.shape[2]), s.dtype) for s in slots], scratch_shapes=_dma_sems(n), name=name)(*slots)


def _pair_add(name, g4, b1, c):
    _, R, W = g4.shape
    tr = _pick(R // 2, (256, 128, 32, 16))
    nblk = (R // 2) // tr

    def body(c_ref, g_ref, b_ref, o_ref):
        o_ref[...] = (g_ref[...] + b_ref[...]).astype(o_ref.dtype)

    grid_spec = pltpu.PrefetchScalarGridSpec(
        num_scalar_prefetch=1, grid=(4, nblk),
        in_specs=[pl.BlockSpec((None, tr, W), lambda s, i, c_ref: (s, c_ref[0] * nblk + i, 0)),
                  pl.BlockSpec((None, tr, W), lambda s, i, c_ref: (s, i, 0))],
        out_specs=pl.BlockSpec((None, tr, W), lambda s, i, c_ref: (s, i, 0)))
    return _pcall(body, grid_spec=grid_spec, out_shape=SDS((4, R // 2, W), bf16), name=name)(c.reshape(1), g4, b1)


def _scatter_chips_multi(name, ps):
    n = len(ps)

    def body(*refs):
        ins, outs = refs[:n], refs[n:2 * n]
        send_sems, recv_sems = refs[2 * n:]
        x, y, c, chips = _position()
        me = 2 * x + y
        sends = []
        for i in range(n):
            for j, (cx, cy) in enumerate(chips):
                cp = _remote(ins[i].at[2 * cx + cy], outs[i].at[me], send_sems, recv_sems, 3 * i + j, (cx, cy, c))
                cp.start()
                sends.append(cp)
        for i in range(n):
            for j, (cx, cy) in enumerate(chips):
                slot = outs[i].at[2 * cx + cy]
                _remote(slot, slot, send_sems, recv_sems, 3 * i + j, (cx, cy, c)).wait_recv()
        for cp in sends:
            cp.wait_send()

    return _pcall(body, in_specs=[HBM_SPEC] * n, out_specs=[HBM_SPEC] * n, out_shape=[SDS(p.shape, p.dtype) for p in ps],
                  scratch_shapes=_dma_sems(3 * n), name=name)(*ps)


def _sum_chips(name, p4, b2, chip, c):
    _, Rh, W = p4.shape
    tr = _pick(Rh, (256, 128, 32, 16))
    nblk = Rh // tr

    def body(m_ref, c_ref, own_ref, r1_ref, r2_ref, r3_ref, o_ref):
        o_ref[...] = ((own_ref[...].astype(f32) + r1_ref[...].astype(f32)) + r2_ref[...].astype(f32)) + r3_ref[...].astype(f32)

    other = lambda k: pl.BlockSpec((None, tr, W), lambda i, m_ref, c_ref: (m_ref[0] ^ k, i, 0))
    grid_spec = pltpu.PrefetchScalarGridSpec(
        num_scalar_prefetch=2, grid=(nblk,),
        in_specs=[pl.BlockSpec((None, tr, W), lambda i, m_ref, c_ref: (m_ref[0], i, 0)), other(1), other(2), other(3)],
        out_specs=pl.BlockSpec((tr, W), lambda i, m_ref, c_ref: (c_ref[0] * nblk + i, 0)))
    return _pcall(body, grid_spec=grid_spec, out_shape=SDS((2 * Rh, W), f32), name=name)(chip.reshape(1), c.reshape(1), p4, b2, b2, b2)


def _join_halves(name, fulls):
    n = len(fulls)

    def body(*refs):
        outs = refs[n:2 * n]
        send_sems, recv_sems = refs[2 * n:]
        x, y, c, _ = _position()
        cps = []
        for i in range(n):
            rh = outs[i].shape[0] // 2
            mine = outs[i].at[pl.ds(pl.multiple_of(c * rh, 8), rh)]
            theirs = outs[i].at[pl.ds(pl.multiple_of((1 - c) * rh, 8), rh)]
            cp = _remote(mine, mine, send_sems, recv_sems, i, (x, y, 1 - c))
            cp.start()
            cps.append((cp, _remote(theirs, theirs, send_sems, recv_sems, i, (x, y, 1 - c))))
        for cp, back in cps:
            back.wait_recv()
            cp.wait_send()

    return _pcall(body, in_specs=[HBM_SPEC] * n, out_specs=[HBM_SPEC] * n, out_shape=[SDS(r.shape, r.dtype) for r in fulls],
                  input_output_aliases={i: i for i in range(n)}, scratch_shapes=_dma_sems(n), name=name)(*fulls)


def _hosted_gather(shards):
    n = len(shards)

    def half(ref_rows, c):
        rh = ref_rows // 2
        return pl.ds(pl.multiple_of(c * rh, 16), rh)

    def start(ins, outs, send_sems, recv_sems):
        x, y, c, chips = _position()
        me = 2 * x + y
        for i in range(n):
            rows = half(ins[i].shape[0], c)
            for j, (cx, cy) in enumerate(chips):
                _remote(ins[i].at[rows], outs[i].at[me, rows], send_sems, recv_sems, 3 * i + j, (cx, cy, c)).start()

    def finish(ins, outs, send_sems, recv_sems):
        x, y, c, chips = _position()
        me = 2 * x + y
        for i in range(n):
            rows = half(ins[i].shape[0], c)
            for j, (cx, cy) in enumerate(chips):
                _remote(ins[i].at[rows], outs[i].at[2 * cx + cy, rows], send_sems, recv_sems, 3 * i + j, (cx, cy, c)).wait_recv()
        for i in range(n):
            rows = half(ins[i].shape[0], c)
            for j, (cx, cy) in enumerate(chips):
                _remote(ins[i].at[rows], outs[i].at[me, rows], send_sems, recv_sems, 3 * i + j, (cx, cy, c)).wait_send()

    return {"arrays": shards, "out_shape": [SDS((4,) + s.shape, s.dtype) for s in shards], "n_sems": 3 * n, "start": start, "finish": finish}


def _gather_forward(name, gathered):
    n = len(gathered)

    def body(*refs):
        outs = refs[n:2 * n]
        send_sems, recv_sems = refs[2 * n:]
        x, y, c, chips = _position()
        sib = (x, y, 1 - c)
        sends = []
        for i in range(n):
            rh = outs[i].shape[1] // 2
            mine = pl.ds(pl.multiple_of(c * rh, 16), rh)
            for j, (cx, cy) in enumerate(chips):
                rows = outs[i].at[2 * cx + cy, mine]
                cp = _remote(rows, rows, send_sems, recv_sems, 3 * i + j, sib)
                cp.start()
                sends.append(cp)
        for i in range(n):
            rh = outs[i].shape[1] // 2
            theirs = pl.ds(pl.multiple_of((1 - c) * rh, 16), rh)
            for j, (cx, cy) in enumerate(chips):
                rows = outs[i].at[2 * cx + cy, theirs]
                _remote(rows, rows, send_sems, recv_sems, 3 * i + j, sib).wait_recv()
        for cp in sends:
            cp.wait_send()

    return _pcall(body, in_specs=[HBM_SPEC] * n, out_specs=[HBM_SPEC] * n, out_shape=[SDS(g.shape, g.dtype) for g in gathered],
                  input_output_aliases={i: i for i in range(n)}, scratch_shapes=_dma_sems(3 * n), name=name)(*gathered)


def _hosted_scatter(ps):
    n = len(ps)

    def start(ins, outs, send_sems, recv_sems):
        x, y, c, chips = _position()
        me = 2 * x + y
        for i in range(n):
            for j, (cx, cy) in enumerate(chips):
                _remote(ins[i].at[2 * cx + cy], outs[i].at[me], send_sems, recv_sems, 3 * i + j, (cx, cy, c)).start()

    def finish(ins, outs, send_sems, recv_sems):
        x, y, c, chips = _position()
        me = 2 * x + y
        for i in range(n):
            for j, (cx, cy) in enumerate(chips):
                slot = outs[i].at[2 * cx + cy]
                _remote(slot, slot, send_sems, recv_sems, 3 * i + j, (cx, cy, c)).wait_recv()
        for i in range(n):
            for j, (cx, cy) in enumerate(chips):
                _remote(ins[i].at[2 * cx + cy], outs[i].at[me], send_sems, recv_sems, 3 * i + j, (cx, cy, c)).wait_send()

    return {"arrays": ps, "out_shape": [SDS(p.shape, p.dtype) for p in ps], "n_sems": 3 * n, "start": start, "finish": finish}


def _hosted_swap(slots):
    n = len(slots)

    def copies(ins, outs, send_sems, recv_sems):
        x, y, c, _ = _position()
        cps = []
        for i in range(n):
            rh = ins[i].shape[1] // 2
            ohalf = pl.ds(pl.multiple_of((1 - c) * rh, 8), rh)
            cps.append(_remote(ins[i].at[:, ohalf, :], outs[i], send_sems, recv_sems, i, (x, y, 1 - c)))
        return cps

    def start(ins, outs, send_sems, recv_sems):
        for cp in copies(ins, outs, send_sems, recv_sems):
            cp.start()

    def finish(ins, outs, send_sems, recv_sems):
        for cp in copies(ins, outs, send_sems, recv_sems):
            cp.wait()

    return {"arrays": slots, "out_shape": [SDS((4, s.shape[1] // 2, s.shape[2]), s.dtype) for s in slots], "n_sems": n,
            "start": start, "finish": finish}


def _rs_begin(slots, c):
    b1 = _swap_halves_multi("rs_swap", slots)
    return [_pair_add("rs_pair_%d" % i, g, b, c) for i, (g, b) in enumerate(zip(slots, b1))]


def _rs_end(ps, b2, c, chip):
    return _join_halves("rs_join", [_sum_chips("rs_sum_%d" % i, p, b, chip, c) for i, (p, b) in enumerate(zip(ps, b2))])


def _reduce_scatter_multi(slots, c, chip):
    ps = _rs_begin(slots, c)
    return _rs_end(ps, _scatter_chips_multi("rs_scatter", ps), c, chip)


_BIG = ("w_in", "w_dn_out", "w_swa_out", "w_o", "w_up", "w_down")


_W_IN_PIECES = ((0, 3072, 0), (3072, 4096, 3072), (5392, 6416, 4096), (6416, 7440, 5120), (4112, 5136, 6144), (4096, 4112, 7168),
                (5136, 5264, 7296), (5264, 5392, 7424))
_W_IN_SHARD = IN_TOTAL // 4


def _w_in_from_slots(g):
    parts, at = [], 0
    for lo, hi, dst in _W_IN_PIECES:
        if dst > at:
            parts.append(jnp.zeros((g.shape[1], dst - at), g.dtype))
        for s in range(4):
            a, b = max(lo, s * _W_IN_SHARD), min(hi, (s + 1) * _W_IN_SHARD)
            if a < b:
                parts.append(g[s][:, a - s * _W_IN_SHARD:b - s * _W_IN_SHARD])
        at = dst + hi - lo
    parts.append(jnp.zeros((g.shape[1], PROJ_W - at), g.dtype))
    return jnp.concatenate(parts, axis=1)


def _w_in_to_slots(gw):
    slots = []
    for s in range(4):
        parts = []
        for lo, hi, dst in sorted(_W_IN_PIECES):
            a, b = max(lo, s * _W_IN_SHARD), min(hi, (s + 1) * _W_IN_SHARD)
            if a < b:
                parts.append(gw[:, dst + a - lo:dst + b - lo])
        slots.append(jnp.concatenate(parts, axis=1))
    return jnp.stack(slots)


def _assemble_weights(gs):
    cat1 = lambda g: jnp.concatenate([g[s] for s in range(4)], axis=1)
    rows = lambda g: g.reshape(4 * g.shape[1], g.shape[2])
    return {"w_in": _w_in_from_slots(gs[0]), "w_dn_out": rows(gs[1]), "w_swa_out": rows(gs[2]), "w_o": rows(gs[3]),
            "w_up": cat1(gs[4]), "w_down": rows(gs[5])}


def _grad_slots(gw):
    rows = lambda g: g.reshape(4, g.shape[0] // 4, g.shape[1])
    return [_w_in_to_slots(gw["w_in"]), rows(gw["w_dn_out"]), rows(gw["w_swa_out"]), rows(gw["w_o"]), gw["w_up"], rows(gw["w_down"])]


_SHARDED = (
    ("w_in", (D, 1860), 1, False),
    ("dn_conv", (4, 768), 1, True),
    ("w_dn_out", (256, D), 0, False),
    ("w_swa_out", (256, D), 0, False),
    ("w_o", (256, D), 0, False),
    ("w_up", (D, 1408), 1, False),
    ("ffn_conv", (3, 704), 1, True),
    ("w_down", (704, D), 0, False),
)


def _pack_weights(shards):
    parts = []
    for nm, shp, _, as_bits in _SHARDED:
        a = shards[nm]
        parts.append(lax.bitcast_convert_type(a, bf16).reshape(-1) if as_bits else a.astype(bf16).reshape(-1))
    flat = jnp.concatenate(parts)
    return jnp.pad(flat, (0, COMM_ROWS * COMM_W - flat.shape[0])).reshape(COMM_ROWS, COMM_W)


def _unpack_weights(g):
    flat = g.reshape(4, -1)
    out, off = {}, 0
    for nm, shp, ax, as_bits in _SHARDED:
        n = int(np.prod(shp)) * (2 if as_bits else 1)
        piece = flat[:, off:off + n]
        off += n
        if as_bits:
            piece = lax.bitcast_convert_type(piece.reshape((4,) + shp + (2,)), f32)
        else:
            piece = piece.reshape((4,) + shp)
        out[nm] = jnp.concatenate([piece[s] for s in range(4)], axis=ax)
    return out


def _pack_grads(grads):
    slots = []
    for s in range(4):
        parts = []
        for nm, shp, ax, _ in _SHARDED:
            n = shp[ax]
            parts.append(lax.slice_in_dim(grads[nm], s * n, (s + 1) * n, axis=ax).reshape(-1))
        flat = jnp.concatenate(parts)
        slots.append(jnp.pad(flat, (0, COMM_ROWS * COMM_W - flat.shape[0])))
    return jnp.stack(slots).reshape(4, COMM_ROWS, COMM_W)


def _unpack_grads(r):
    flat = r.reshape(-1)
    out, off = {}, 0
    for nm, shp, _, _ in _SHARDED:
        n = int(np.prod(shp))
        out[nm] = flat[off:off + n].reshape(shp)
        off += n
    return out


def _regroup_w_in(w):
    z = lambda n: jnp.zeros((w.shape[0], n), w.dtype)
    return jnp.concatenate([w[:, 0:3072], w[:, 3072:4096], w[:, 5392:6416], w[:, 6416:7440], w[:, 4112:5136],
                            w[:, 4096:4112], z(112), w[:, 5136:5264], w[:, 5264:5392], z(128)], axis=1)


def _ungroup_w_in(g):
    return jnp.concatenate([g[:, 0:3072], g[:, 3072:4096], g[:, 7168:7184], g[:, 6144:7168], g[:, 7296:7424], g[:, 7424:7552],
                            g[:, 4096:5120], g[:, 5120:6144]], axis=1)


def _pad_lanes(v, n=LANE):
    return jnp.pad(v, (0, n - v.shape[0])).reshape(1, n)


def _layer_consts(P):
    K = {}
    K["norm_mix"] = P["norm_mix"].reshape(1, D)
    K["norm_ffn"] = P["norm_ffn"].reshape(1, D)
    K["alog"] = _pad_lanes(P["dn_a_log"])
    K["dtb"] = _pad_lanes(P["dn_dt_bias"])
    K["dn_norm"] = P["dn_norm"].reshape(1, LANE)
    K["qn"] = jnp.tile(P["swa_q_norm"], 16).reshape(1, D)
    K["kn"] = jnp.tile(P["swa_k_norm"], 2).reshape(1, LANE)
    K["sinks"] = _pad_lanes(P["swa_sinks"])
    K["ffn_b"] = P["ffn_conv_b"].reshape(1, D_FF)
    return K


def _layer_fwd(x, mod, W, K, tabs, bd, hosted=None):
    sh1, sc1, gt1, sh2, sc2, gt2 = mod
    S = {"x": x}
    h1, h1t = _rowwise_fwd("normmod1_fwd", _normmod_fn, [(x, 0, D)], [K["norm_mix"], sc1, sh1], [D], [bf16], also_transposed=True)
    proj = _matmul("proj_fwd", h1, W["w_in"], "nn", f32)
    qn = _dnconv_fwd("dnconv_q_fwd", proj, CB_Q, W["dn_conv"], True)
    kn = _dnconv_fwd("dnconv_k_fwd", proj, CB_K, W["dn_conv"], True)
    vc = _dnconv_fwd("dnconv_v_fwd", proj, CB_V, W["dn_conv"], False)
    o, sall, xinv, hosted_out = _gdn_fwd("gdn_fwd", qn, kn, vc, proj, K["alog"], K["dtb"], hosted=hosted)
    (on,) = _rowwise_fwd("dngate_fwd", _dngate_fn, [(o, 0, LANE), (proj, 8 * WB_Z, LANE)], [K["dn_norm"]], [LANE], [bf16], nc=8,
                         tr=_pick(x.shape[0], (1024,)))
    ya = _matmul("dnout_fwd", on, W["w_dn_out"], "nn", f32)
    sq = _qkprep_fwd("qprep_fwd", proj, WB_SWQ, D, K["qn"], bd[0], tabs[0])
    sk = _qkprep_fwd("kprep_fwd", proj, CB_SWK, LANE, K["kn"], bd[1], tabs[1])
    attn = _attn_fwd("attn_fwd", sq, sk, proj, K["sinks"], K["attn_bias"])
    yb = _matmul("swaout_fwd", attn, W["w_swa_out"], "nn", f32)
    (merged,) = _rowwise_fwd("merge_fwd", _merge_fn, [(proj, WB_GA, D), (proj, WB_GB, D), (ya, 0, D), (yb, 0, D)], [], [D], [bf16])
    t1 = _matmul("wo_fwd", merged, W["w_o"], "nn", f32)
    (x1,) = _rowwise_fwd("resid1_fwd", _resid_fn, [(x, 0, D), (t1, 0, D)], [gt1], [D], [f32])
    h2, h2t = _rowwise_fwd("normmod2_fwd", _normmod_fn, [(x1, 0, D)], [K["norm_ffn"], sc2, sh2], [D], [bf16], also_transposed=True)
    up = _matmul("up_fwd", h2, W["w_up"], "nn", f32)
    mid = _ffnact_fwd("ffnact_fwd", up, W["ffn_conv"], K["ffn_b"])
    t2 = _matmul("down_fwd", mid, W["w_down"], "nn", f32)
    (x2,) = _rowwise_fwd("resid2_fwd", _resid_fn, [(x1, 0, D), (t2, 0, D)], [gt2], [D], [f32])
    S.update(h1t=h1t, h2t=h2t, proj=proj, qn=qn, kn=kn, vc=vc, o=o, sall=sall, xinv=xinv, on=on, ya=ya, sq=sq, sk=sk, attn=attn, yb=yb,
             merged=merged, t1=t1, x1=x1, h2=h2, up=up, mid=mid, t2=t2)
    return x2, S, hosted_out


def _layer_bwd(dx2, S, mod, W, K, tabs, bd, carry=None):
    sh1, sc1, gt1, sh2, sc2, gt2 = mod
    x, x1, proj, up = S["x"], S["x1"], S["proj"], S["up"]
    T = x.shape[0]
    gw, gs = {}, {}
    dt2, dgt2 = _rowwise_bwd("resid2_bwd", _resid_fn, [(x1, 0, D), (S["t2"], 0, D)], [gt2], [(dx2, 0, D)], [None, bf16])
    dmid = _matmul("down_bwd_x", dt2, W["w_down"], "nt", bf16)
    gw["w_down"] = _matmul("down_bwd_w", S["mid"], dt2, "tn", f32)
    dact, dlin, gw["ffn_conv"], dffn_b = _ffnact_bwd("ffnact_bwd", up, W["ffn_conv"], K["ffn_b"], dmid)
    dup = jnp.concatenate([dact, dlin], axis=1)
    dh2 = _matmul("up_bwd_x", dup, W["w_up"], "nt", f32)
    if carry is None:
        gw["w_up"] = _matmul("up_bwd_w", S["h2t"], dup, "nn", f32, out_slots=4)
        pair_sums = hosted = None
    else:
        gw["w_up"], b1 = _matmul("up_bwd_w", S["h2t"], dup, "nn", f32, out_slots=4, hosted=_hosted_swap(carry[0]))
        pair_sums = [_pair_add("rs_pair_%d" % i, g, b, carry[1]) for i, (g, b) in enumerate(zip(carry[0], b1))]
        hosted = _hosted_scatter(pair_sums)
    dx1, dnorm_ffn, dsc2, dsh2 = _rowwise_bwd("normmod2_bwd", _normmod_fn, [(x1, 0, D)], [K["norm_ffn"], sc2, sh2], [(dh2, 0, D)], [f32],
                                              add_to_first=(dx2, 0, D))
    dt1, dgt1 = _rowwise_bwd("resid1_bwd", _resid_fn, [(x, 0, D), (S["t1"], 0, D)], [gt1], [(dx1, 0, D)], [None, bf16])
    dmerged = _matmul("wo_bwd_x", dt1, W["w_o"], "nt", f32)
    gw["w_o"] = _matmul("wo_bwd_w", S["merged"], dt1, "tn", f32)
    dproj, dya, dyb = _rowwise_bwd("merge_bwd", _merge_fn, [(proj, WB_GA, D), (proj, WB_GB, D), (S["ya"], 0, D), (S["yb"], 0, D)], [],
                                   [(dmerged, 0, D)], [bf16, bf16, bf16, bf16], dest=(None, (0, 1), WB_GA // 2))
    don = _matmul("dnout_bwd_x", dya, W["w_dn_out"], "nt", f32)
    gw["w_dn_out"] = _matmul("dnout_bwd_w", S["on"], dya, "tn", f32)
    dproj, do, ddn_norm = _rowwise_bwd("dngate_bwd", _dngate_fn, [(S["o"], 0, LANE), (proj, 8 * WB_Z, LANE)], [K["dn_norm"]], [(don, 0, LANE)],
                                       [f32, bf16], nc=8, tr=_pick(T, (1024,)), dest=(dproj, (1,), 8 * WB_Z))
    (dqn, dkn, dvc, dab, dalog, ddtb), hosted_out = _gdn_bwd("gdn_bwd", S["qn"], S["kn"], S["vc"], proj, K["alog"], K["dtb"], S["sall"], S["xinv"], do,
                                                            hosted=hosted)
    dproj, dwq = _dnconv_bwd("dnconv_q_bwd", proj, CB_Q, W["dn_conv"], dqn, True, dproj)
    dproj, dwk = _dnconv_bwd("dnconv_k_bwd", proj, CB_K, W["dn_conv"], dkn, True, dproj)
    dproj, dwv = _dnconv_bwd("dnconv_v_bwd", proj, CB_V, W["dn_conv"], dvc, False, dproj)
    gw["dn_conv"] = jnp.concatenate([dwq, dwk, dwv], axis=1)
    dattn = _matmul("swaout_bwd_x", dyb, W["w_swa_out"], "nt", f32)
    gw["w_swa_out"] = _matmul("swaout_bwd_w", S["attn"], dyb, "tn", f32)
    dsq, dkp, dkc, dvp, dvc_, dsinks = _attn_bwd("attn_bwd", S["sq"], S["sk"], proj, K["sinks"], K["attn_bias"], dattn)
    dsk = _shift_add("attn_dk_join", dkc, dkp, f32)
    dswv = _shift_add("attn_dv_join", dvc_, dvp, bf16)
    dproj, dqn_w = _qkprep_bwd("qprep_bwd", proj, WB_SWQ, D, K["qn"], bd[0], tabs[0], dsq, dest_buf=dproj)
    dswk, dkn_w = _qkprep_bwd("kprep_bwd", proj, CB_SWK, LANE, K["kn"], bd[1], tabs[1], dsk)
    tail = jnp.concatenate([dab.astype(bf16), dswk, dswv, jnp.zeros((T, LANE), bf16)], axis=1)
    dproj = lax.dynamic_update_slice(dproj, tail, (0, CB_AB * LANE))
    dh1 = _matmul("proj_bwd_x", dproj, W["w_in"], "nt", f32)
    gw["w_in"] = _matmul("proj_bwd_w", S["h1t"], dproj, "nn", f32)
    dx, dnorm_mix, dsc1, dsh1 = _rowwise_bwd("normmod1_bwd", _normmod_fn, [(x, 0, D)], [K["norm_mix"], sc1, sh1], [(dh1, 0, D)], [f32],
                                             add_to_first=(dx1, 0, D))
    gs = {"norm_mix": dnorm_mix[0], "dn_a_log": dalog[0, :8], "dn_dt_bias": ddtb[0, :8], "dn_norm": ddn_norm[0],
          "swa_q_norm": dqn_w.reshape(16, 64).sum(0), "swa_k_norm": dkn_w.reshape(2, 64).sum(0), "swa_sinks": dsinks[0, :16],
          "norm_ffn": dnorm_ffn[0], "ffn_conv_b": dffn_b[0]}
    dmod = jnp.concatenate([dsh1, dsc1, dgt1, dsh2, dsc2, dgt2], axis=1)
    return dx, gw, gs, dmod, (pair_sums, hosted_out)


def _rope_tables(pos):
    T = pos.shape[0]
    half = 8
    inv = jnp.power(ROPE_THETA, -jnp.arange(half, dtype=f32) / half)
    ang = pos.astype(f32)[:, None] * inv
    cos, sin = jnp.cos(ang), jnp.sin(ang)
    z8, z48, o48 = jnp.zeros((T, 8), f32), jnp.zeros((T, 48), f32), jnp.ones((T, 48), f32)
    c64 = jnp.concatenate([cos, cos, o48], axis=1)
    s1 = jnp.concatenate([-sin, z8, z48], axis=1)
    s2 = jnp.concatenate([z8, sin, z48], axis=1)
    return tuple(jnp.tile(t, (1, 2)) for t in (c64, s1, s2))


_SMALL = (("norm_mix", D), ("dn_a_log", 8), ("dn_dt_bias", 8), ("dn_norm", 128), ("swa_q_norm", 64), ("swa_k_norm", 64),
          ("swa_sinks", 16), ("norm_ffn", D), ("ffn_conv_b", D_FF), ("b_ada", 6 * D))
_CONV = (("dn_conv", 4 * 3072), ("ffn_conv", 3 * D_FF))
_CONV_SHARD = (("dn_conv", 4 * 768), ("ffn_conv", 3 * 704))


def _pack_small(vals, spec):
    flat = jnp.concatenate([vals[nm].reshape(-1) for nm, _ in spec])
    rows = -(-flat.shape[0] // (8 * LANE)) * 8
    return jnp.pad(flat, (0, rows * LANE - flat.shape[0])).reshape(rows, LANE)


def _unpack_small(buf, spec):
    flat = buf.reshape(-1)
    out, off = {}, 0
    for nm, n in spec:
        out[nm] = flat[off:off + DEPTH * n].reshape(DEPTH, n)
        off += DEPTH * n
    return out


def kernel(x, c, positions, w_ada, b_ada, norm_mix, w_in, dn_conv, dn_a_log, dn_dt_bias, dn_norm, w_dn_out, swa_q_norm, swa_k_norm, swa_sinks, w_swa_out, w_o, norm_ffn, w_up, ffn_conv, ffn_conv_b, w_down, loss_target, m_w_ada, m_b_ada, m_norm_mix, m_w_in, m_dn_conv, m_dn_a_log, m_dn_dt_bias, m_dn_norm, m_w_dn_out, m_swa_q_norm, m_swa_k_norm, m_swa_sinks, m_w_swa_out, m_w_o, m_norm_ffn, m_w_up, m_ffn_conv, m_ffn_conv_b, m_w_down, v_w_ada, v_b_ada, v_norm_mix, v_w_in, v_dn_conv, v_dn_a_log, v_dn_dt_bias, v_dn_norm, v_w_dn_out, v_swa_q_norm, v_swa_k_norm, v_swa_sinks, v_w_swa_out, v_w_o, v_norm_ffn, v_w_up, v_ffn_conv, v_ffn_conv_b, v_w_down):
    weights = dict(w_ada=w_ada, b_ada=b_ada, norm_mix=norm_mix, w_in=w_in, dn_conv=dn_conv, dn_a_log=dn_a_log, dn_dt_bias=dn_dt_bias,
                   dn_norm=dn_norm, w_dn_out=w_dn_out, swa_q_norm=swa_q_norm, swa_k_norm=swa_k_norm, swa_sinks=swa_sinks,
                   w_swa_out=w_swa_out, w_o=w_o, norm_ffn=norm_ffn, w_up=w_up, ffn_conv=ffn_conv, ffn_conv_b=ffn_conv_b, w_down=w_down)
    mom_m = dict(w_ada=m_w_ada, b_ada=m_b_ada, norm_mix=m_norm_mix, w_in=m_w_in, dn_conv=m_dn_conv, dn_a_log=m_dn_a_log,
                 dn_dt_bias=m_dn_dt_bias, dn_norm=m_dn_norm, w_dn_out=m_w_dn_out, swa_q_norm=m_swa_q_norm, swa_k_norm=m_swa_k_norm,
                 swa_sinks=m_swa_sinks, w_swa_out=m_w_swa_out, w_o=m_w_o, norm_ffn=m_norm_ffn, w_up=m_w_up, ffn_conv=m_ffn_conv,
                 ffn_conv_b=m_ffn_conv_b, w_down=m_w_down)
    mom_v = dict(w_ada=v_w_ada, b_ada=v_b_ada, norm_mix=v_norm_mix, w_in=v_w_in, dn_conv=v_dn_conv, dn_a_log=v_dn_a_log,
                 dn_dt_bias=v_dn_dt_bias, dn_norm=v_dn_norm, w_dn_out=v_w_dn_out, swa_q_norm=v_swa_q_norm, swa_k_norm=v_swa_k_norm,
                 swa_sinks=v_swa_sinks, w_swa_out=v_w_swa_out, w_o=v_w_o, norm_ffn=v_norm_ffn, w_up=v_w_up, ffn_conv=v_ffn_conv,
                 ffn_conv_b=v_ffn_conv_b, w_down=v_w_down)
    order = ["w_ada", "b_ada", "norm_mix", "w_in", "dn_conv", "dn_a_log", "dn_dt_bias", "dn_norm", "w_dn_out", "swa_q_norm",
             "swa_k_norm", "swa_sinks", "w_swa_out", "w_o", "norm_ffn", "w_up", "ffn_conv", "ffn_conv_b", "w_down"]
    ax, ay, ac = lax.axis_index("x"), lax.axis_index("y"), lax.axis_index("c")
    chip = 2 * ax + ay
    dev = 4 * ax + 2 * ay + ac
    T = x.shape[1]
    xs = x[0]

    c_all = _allgather_all("gather_c", jnp.pad(c, ((0, 7), (0, 0)))).reshape(8, 8, D)[:, 0]
    c_act = _silu_rows("silu_c", jnp.pad(c_all, ((0, 8), (0, 0))))
    mod_sh = jnp.stack([
        _matmul("mod_fwd", c_act, w_ada[l].astype(bf16), "nn", f32,
                bias=lax.dynamic_slice(b_ada[l], (chip * 1536,), (1536,)).reshape(1, 1536)) for l in range(DEPTH)])
    mod_all = _allgather_all("gather_mod", mod_sh.reshape(DEPTH * 16 * 12, LANE)).reshape(8, DEPTH, 16, 1536)
    mod_me = jnp.concatenate([lax.dynamic_index_in_dim(mod_all[2 * s], dev, axis=1, keepdims=False) for s in range(4)], axis=1)

    tabs_q = _rope_tables(positions[0])
    tabs = (tabs_q, tabs_q)
    head = jnp.arange(LANE) // 64
    bd128 = (head[:, None] == head[None, :]).astype(f32) / 64.0
    bd = (bd128, bd128)
    attn_bias = _attn_bias()

    conv_all = _allgather_all("gather_conv", _pack_small({"dn_conv": dn_conv, "ffn_conv": ffn_conv}, _CONV_SHARD))
    conv_parts = [_unpack_small(conv_all[2 * s], _CONV_SHARD) for s in range(4)]
    dn_conv_full = jnp.concatenate([p["dn_conv"].reshape(DEPTH, 4, 768) for p in conv_parts], axis=2)
    ffn_conv_full = jnp.concatenate([p["ffn_conv"].reshape(DEPTH, 3, 704) for p in conv_parts], axis=2)

    saved, Ws, Ks, mods = [], [], [], []
    h = xs
    shards = [[weights[nm][l].astype(bf16) for nm in _BIG] for l in range(DEPTH)]
    gathered = _gather_chips("gather_w", shards[0])
    for l in range(DEPTH):
        gathered = [lax.dynamic_update_index_in_dim(g, s, chip, 0) for g, s in zip(gathered, shards[l])]
        W = _assemble_weights(gathered)
        W["dn_conv"], W["ffn_conv"] = dn_conv_full[l], ffn_conv_full[l]
        K = _layer_consts({nm: weights[nm][l] for nm in ("norm_mix", "norm_ffn", "dn_a_log", "dn_dt_bias", "dn_norm", "swa_q_norm",
                                                          "swa_k_norm", "swa_sinks", "ffn_conv_b")})
        K["attn_bias"] = attn_bias
        mod = tuple(mod_me[l, k * D:(k + 1) * D].reshape(1, D) for k in range(6))
        nxt = _hosted_gather(shards[l + 1]) if l + 1 < DEPTH else None
        h, S, arrived = _layer_fwd(h, mod, W, K, tabs, bd, hosted=nxt)
        if nxt is not None:
            gathered = _gather_forward("gather_w_pass", arrived)
        saved.append(S), Ws.append(W), Ks.append(K), mods.append(mod)

    loss_blk, dh = _loss("loss", h, loss_target[0])
    loss = lax.psum(loss_blk[0, 0], ("x", "y", "c"))

    grad_sh = [None] * DEPTH
    small = [None] * DEPTH
    dmods = [None] * DEPTH
    slots = None
    for l in reversed(range(DEPTH)):
        dh, gw, gs, dmod, (ps, b2) = _layer_bwd(dh, saved[l], mods[l], Ws[l], Ks[l], tabs, bd, carry=None if slots is None else (slots, ac))
        if slots is not None:
            grad_sh[l + 1] = dict(zip(_BIG, _rs_end(ps, b2, ac, chip)))
        slots = _grad_slots(gw)
        small[l], dmods[l] = dict(gs, dn_conv=gw["dn_conv"], ffn_conv=gw["ffn_conv"]), dmod[0]
    grad_sh[0] = dict(zip(_BIG, _reduce_scatter_multi(slots, ac, chip)))

    spec_g = _SMALL + _CONV
    vals = {nm: jnp.stack([small[l][nm] for l in range(DEPTH)]) for nm, _ in spec_g if nm != "b_ada"}
    vals["b_ada"] = jnp.stack(dmods)
    small_all = _allgather_all("gather_small", _pack_small(vals, spec_g))
    g_small = _unpack_small(_sum_leading("sum_small", small_all), spec_g)
    dmod_all = jnp.stack([_unpack_small(small_all[d], spec_g)["b_ada"] for d in range(8)])
    dmod_sh = lax.dynamic_slice(dmod_all, (0, 0, chip * 1536), (8, DEPTH, 1536))
    dmod_sh = jnp.pad(dmod_sh, ((0, 8), (0, 0), (0, 0))).astype(bf16)
    g_w_ada = jnp.stack([_matmul("mod_bwd_w", c_act, dmod_sh[:, l], "tn", f32) for l in range(DEPTH)])

    grads = {nm: g_small[nm] for nm, _ in _SMALL}
    grads["dn_conv"] = lax.dynamic_slice(g_small["dn_conv"].reshape(DEPTH, 4, 3072), (0, 0, chip * 768), (DEPTH, 4, 768))
    grads["ffn_conv"] = lax.dynamic_slice(g_small["ffn_conv"].reshape(DEPTH, 3, D_FF), (0, 0, chip * 704), (DEPTH, 3, 704))
    grads["w_ada"] = g_w_ada
    for nm in _BIG:
        grads[nm] = jnp.stack([grad_sh[l][nm] for l in range(DEPTH)])

    delta, new_m, new_v = {}, {}, {}
    for nm in ("w_ada", "dn_conv", "ffn_conv") + _BIG:
        delta[nm], new_m[nm], new_v[nm] = _adamw("adamw_" + nm, weights[nm], grads[nm], mom_m[nm], mom_v[nm])
    sm = [_pack_small({nm: t[nm] for nm, _ in _SMALL}, _SMALL) for t in (weights, grads, mom_m, mom_v)]
    for tgt, buf in zip((delta, new_m, new_v), _adamw("adamw_small", *sm)):
        tgt.update(_unpack_small(buf, _SMALL))

    return (loss, dh[None], *[grads[n] for n in order], *[delta[n] for n in order], *[new_m[n] for n in order], *[new_v[n] for n in order])
```

```python
import functools

import jax
import jax.numpy as jnp
import numpy as np
from jax import lax
from jax.experimental import pallas as pl
from jax.experimental.pallas import tpu as pltpu

f32 = jnp.float32
bf16 = jnp.bfloat16
SDS = jax.ShapeDtypeStruct
HI = lax.Precision.HIGHEST
MESH = pl.DeviceIdType.MESH

D = 1024
DEPTH = 4
EPS = 1e-6
DN_C = 64
SWA_B = 128
LANE = 128
ROPE_THETA = 500000.0
D_FF = 2816
IN_TOTAL = 7440
PROJ_W = 7680
CB_Q, CB_K, CB_V = 0, 8, 16
CB_AB, CB_SWK, CB_SWV = 56, 57, 58
WB_Z, WB_GA, WB_GB, WB_SWQ = 3, 4, 5, 6
TR = 256
COMM_W = 1024
COMM_ROWS = 4864
COMM_TR = 128
VMEM_BIG = 48 * 2 ** 20

ADAM_LR, ADAM_B1, ADAM_B2, ADAM_EPS, ADAM_WD, ADAM_STEP = 0.001, 0.9, 0.999, 1e-08, 0.01, 10


def _pcall(body, **kw):
    return pl.pallas_call(body, **kw)


def _cparams(vmem=None):
    return pltpu.CompilerParams(vmem_limit_bytes=vmem) if vmem else None


def _dot(a, b, ca, cb, precision=HI):
    return lax.dot_general(a, b, (((ca,), (cb,)), ((), ())), precision=precision, preferred_element_type=f32)


def _pick(n, cands):
    for c in cands:
        if n % c == 0:
            return c
    return n


def _tile(n, cap):
    if n <= cap:
        return n
    best = None
    for t in range(LANE, cap + 1, LANE):
        if n % t == 0:
            best = t
    assert best is not None, (n, cap)
    return best


def _matmul(name, a, b, mode, out_dtype, bias=None, out_slots=None, hosted=None):
    h_args, h_specs, h_shapes, h_sems = _hosted_parts(hosted)
    nh = len(h_args)
    if mode == "nn":
        (M, K), (_, N) = a.shape, b.shape
    elif mode == "nt":
        (M, K), (N, _) = a.shape, b.shape
    else:
        (K, M), (_, N) = a.shape, b.shape
    tm = _tile(M, 1536 if mode == "tn" else 1024)
    tn = N // out_slots if out_slots else _tile(N, 1536)
    tk = _tile(K, 512 if mode == "tn" else (1024 if K <= 1024 else 1536))
    nk = K // tk
    ca, cb = {"nn": (1, 0), "nt": (1, 1), "tn": (0, 0)}[mode]

    grid = (M // tm, N // tn, nk)

    def body(*refs):
        a_ref, b_ref = refs[:2]
        nb = 1 if bias is not None else 0
        bias_ref = refs[2] if nb else None
        h_ins, o_ref, h_outs = refs[2 + nb:2 + nb + nh], refs[2 + nb + nh], refs[3 + nb + nh:3 + nb + 2 * nh]
        rest = refs[3 + nb + 2 * nh:]
        if hosted is not None:
            step = (pl.program_id(0) * grid[1] + pl.program_id(1)) * grid[2] + pl.program_id(2)
            sems = rest[1:] if nk > 1 else rest

            @pl.when(step == 0)
            def _():
                hosted["start"](h_ins, h_outs, *sems)

        def finish(r):
            if bias is not None:
                r = r + bias_ref[...]
            o_ref[...] = r.astype(o_ref.dtype)

        part = _dot(a_ref[...].astype(bf16), b_ref[...].astype(bf16), ca, cb, precision=None)
        if nk == 1:
            finish(part)
        else:
            acc = rest[0]
            k = pl.program_id(2)

            @pl.when(k == 0)
            def _():
                acc[...] = part

            @pl.when(k > 0)
            def _():
                acc[...] += part

            @pl.when(k == nk - 1)
            def _():
                finish(acc[...])

        if hosted is not None:
            @pl.when(step == grid[0] * grid[1] * grid[2] - 1)
            def _():
                hosted["finish"](h_ins, h_outs, *sems)

    a_spec =pl.BlockSpec((tk, tm), lambda i, j, k: (k, i)) if mode == "tn" else pl.BlockSpec((tm, tk), lambda i, j, k: (i, k))
    b_spec = pl.BlockSpec((tn, tk), lambda i, j, k: (j, k)) if mode == "nt" else pl.BlockSpec((tk, tn), lambda i, j, k: (k, j))
    in_specs = [a_spec, b_spec]
    args = [a, b]
    if bias is not None:
        in_specs.append(pl.BlockSpec((1, tn), lambda i, j, k: (0, j)))
        args.append(bias)
    if out_slots:
        out_spec = pl.BlockSpec((None, tm, tn), lambda i, j, k: (j, i, 0))
        out_shape = SDS((out_slots, M, tn), out_dtype)
    else:
        out_spec = pl.BlockSpec((tm, tn), lambda i, j, k: (i, j))
        out_shape = SDS((M, N), out_dtype)
    if hosted is None:
        return _pcall(
            body, grid=grid, in_specs=in_specs, out_specs=out_spec, out_shape=out_shape,
            scratch_shapes=[pltpu.VMEM((tm, tn), f32)] if nk > 1 else [], compiler_params=_cparams(VMEM_BIG), name=name)(*args)
    outs = _pcall(
        body, grid=grid, in_specs=in_specs + h_specs, out_specs=[out_spec] + h_specs, out_shape=[out_shape] + h_shapes,
        scratch_shapes=([pltpu.VMEM((tm, tn), f32)] if nk > 1 else []) + h_sems, compiler_params=_cparams(VMEM_BIG), name=name)(*args, *h_args)
    return outs[0], list(outs[1:])


def _row_specs(rows, tr):
    return [pl.BlockSpec((tr, w), lambda i, j, off=off: (i, off + j)) for (_, off, w) in rows]


def _rowwise_fwd(name, fn, rows, vecs, out_widths, out_dtypes, nc=1, tr=TR, also_transposed=False):
    T = rows[0][0].shape[0]
    n_in = len(rows) + len(vecs)
    n_out = len(out_widths)

    def body(*refs):
        vals = [r[...].astype(f32) for r in refs[:n_in]]
        res = fn(*vals)
        for o_ref, r in zip(refs[n_in:n_in + n_out], res):
            o_ref[...] = r.astype(o_ref.dtype)
        if also_transposed:
            refs[n_in + n_out][...] = res[0].T.astype(refs[n_in + n_out].dtype)

    in_specs = _row_specs(rows, tr) + [pl.BlockSpec(v.shape, lambda i, j: (0, 0)) for v in vecs]
    out_specs = [pl.BlockSpec((tr, w), lambda i, j: (i, j)) for w in out_widths]
    out_shape = [SDS((T, w * nc), dt) for w, dt in zip(out_widths, out_dtypes)]
    if also_transposed:
        out_specs.append(pl.BlockSpec((out_widths[0], tr), lambda i, j: (j, i)))
        out_shape.append(SDS((out_widths[0] * nc, T), out_dtypes[0]))
    return _pcall(body, grid=(T // tr, nc), in_specs=in_specs, out_specs=out_specs, out_shape=out_shape, name=name)(
        *[r[0] for r in rows], *vecs)


def _into_buffer(dest_buf, n_inputs, out_index):
    if dest_buf is None:
        return [], [], {}
    return [dest_buf], [pl.BlockSpec(memory_space=pl.ANY)], {n_inputs: out_index}


def _rowwise_bwd(name, fn, rows, vecs, cts, drow_dtypes, nc=1, tr=TR, add_to_first=None, dest=None):
    T = rows[0][0].shape[0]
    n_r, n_v, n_c = len(rows), len(vecs), len(cts)
    n_add = 0 if add_to_first is None else 1
    keep = [k for k, dt in enumerate(drow_dtypes) if dt is not None]
    members = [] if dest is None else list(dest[1])
    plain = [pos for pos in range(len(keep)) if pos not in members]
    n_dest = 1 if dest is not None else 0
    n_in = n_r + n_v + n_c + n_add

    def body(*refs):
        vals = [r[...].astype(f32) for r in refs[:n_in]]
        outs = refs[n_in + (1 if dest is not None and dest[0] is not None else 0):]
        i, j = pl.program_id(0), pl.program_id(1)
        _, vjp = jax.vjp(fn, *vals[:n_r + n_v])
        grads = vjp(tuple(vals[n_r + n_v:n_r + n_v + n_c]))
        got = []
        for pos, k in enumerate(keep):
            g = grads[k]
            if n_add and pos == 0:
                g = g + vals[n_in - 1]
            got.append(g)
        if dest is not None:
            outs[0][...] = jnp.concatenate([got[pos] for pos in members], axis=1).astype(outs[0].dtype)
        for q, pos in enumerate(plain):
            outs[n_dest + q][...] = got[pos].astype(outs[n_dest + q].dtype)
        vec_outs = outs[n_dest + len(plain):]

        @pl.when((i == 0) & (j == 0))
        def _():
            for q in range(n_v):
                vec_outs[q][...] = jnp.zeros_like(vec_outs[q])

        for q in range(n_v):
            vec_outs[q][...] += grads[n_r + q]

    extra = [] if add_to_first is None else [add_to_first]
    in_specs = (_row_specs(rows, tr) + [pl.BlockSpec(v.shape, lambda i, j: (0, 0)) for v in vecs]
                + _row_specs(cts, tr) + _row_specs(extra, tr))
    args = [r[0] for r in rows] + list(vecs) + [c[0] for c in cts] + [e[0] for e in extra]
    out_specs, out_shape, aliases = [], [], {}
    if dest is not None:
        width = sum(rows[keep[pos]][2] for pos in members)
        col = dest[2]
        b_args, b_specs, aliases = _into_buffer(dest[0], n_in, 0)
        args, in_specs = args + b_args, in_specs + b_specs
        out_specs.append(pl.BlockSpec((tr, width), lambda i, j: (i, col + j)))
        out_shape.append(SDS((T, PROJ_W), bf16))
    out_specs += [pl.BlockSpec((tr, rows[keep[pos]][2]), lambda i, j: (i, j)) for pos in plain]
    out_shape += [SDS((T, rows[keep[pos]][2] * nc), drow_dtypes[keep[pos]]) for pos in plain]
    out_specs += [pl.BlockSpec(v.shape, lambda i, j: (0, 0)) for v in vecs]
    out_shape += [SDS(v.shape, f32) for v in vecs]
    return _pcall(body, grid=(T // tr, nc), in_specs=in_specs, out_specs=out_specs, out_shape=out_shape,
                  input_output_aliases=aliases, name=name)(*args)


def _normmod_fn(x, w, sc, sh):
    y = x * lax.rsqrt(jnp.mean(x * x, axis=-1, keepdims=True) + EPS)
    return ((y * w) * (1.0 + sc) + sh,)


def _resid_fn(x, t, gt):
    return (x + gt * t,)


def _merge_fn(ga, gb, ya, yb):
    return (jax.nn.sigmoid(ga) * ya + jax.nn.sigmoid(gb) * yb,)


def _dngate_fn(o, z, w):
    y = o * lax.rsqrt(jnp.mean(o * o, axis=-1, keepdims=True) + EPS)
    return ((y * w) * (z * jax.nn.sigmoid(z)),)


def _conv_taps(x, w_ref, taps, buf):
    w = lambda s: w_ref[taps - 1 - s:taps - s, :]
    row = lax.broadcasted_iota(jnp.int32, (8, x.shape[1]), 0)
    x8 = x[0:8]
    acc, acc8 = x * w(0), x8 * w(0)
    for s in range(1, taps):
        acc = acc + pltpu.roll(x, s, 0) * w(s)
        acc8 = acc8 + jnp.where(row >= s, pltpu.roll(x8, s, 0), 0.0) * w(s)
    buf[...] = acc
    buf[0:8, :] = acc8
    return buf[...]


def _conv_taps_bwd(x, dy, w_ref, dw_ref, taps, buf):
    T = x.shape[0]
    w = lambda s: w_ref[taps - 1 - s:taps - s, :]
    row = lax.broadcasted_iota(jnp.int32, (8, x.shape[1]), 0)
    dy_first, dy_last = dy[0:8], dy[T - 8:T]
    dx, dx_last = dy * w(0), dy_last * w(0)
    dw_ref[taps - 1:taps, :] = jnp.sum(dy * x, axis=0, keepdims=True)
    for s in range(1, taps):
        dx = dx + pltpu.roll(dy, T - s, 0) * w(s)
        dx_last = dx_last + jnp.where(row < 8 - s, pltpu.roll(dy_last, 8 - s, 0), 0.0) * w(s)
        xr = pltpu.roll(x, s, 0)
        wrapped = jnp.sum(jnp.where(row < s, dy_first * xr[0:8], 0.0), axis=0, keepdims=True)
        dw_ref[taps - 1 - s:taps - s, :] = jnp.sum(dy * xr, axis=0, keepdims=True) - wrapped
    buf[...] = dx
    buf[T - 8:T, :] = dx_last
    return buf[...]


def _dn_act(y, normalize):
    s = y * jax.nn.sigmoid(y)
    if normalize:
        s = s * lax.rsqrt(jnp.sum(s * s, axis=-1, keepdims=True) + EPS)
    return s


def _dnconv_fwd(name, proj, cb, w, normalize):
    T = proj.shape[0]

    def body(x_ref, w_ref, o_ref, buf):
        o_ref[...] = _dn_act(_conv_taps(x_ref[...], w_ref, 4, buf), normalize)

    return _pcall(
        body, grid=(8,), in_specs=[pl.BlockSpec((T, LANE), lambda j: (0, cb + j)), pl.BlockSpec((4, LANE), lambda j: (0, cb + j))],
        out_specs=pl.BlockSpec((T, LANE), lambda j: (0, j)), out_shape=SDS((T, 1024), f32), scratch_shapes=[pltpu.VMEM((T, LANE), f32)],
        compiler_params=_cparams(VMEM_BIG), name=name)(proj, w)


def _dnconv_bwd(name, proj, cb, w, dout, normalize, dest_buf):
    T = proj.shape[0]
    b_args, b_specs, aliases = _into_buffer(dest_buf, 3, 0)

    def body(x_ref, w_ref, do_ref, *rest):
        dx_ref, dw_ref, buf, buf2 = rest[len(b_args):]
        x = x_ref[...]
        y = _conv_taps(x, w_ref, 4, buf)
        _, vjp = jax.vjp(functools.partial(_dn_act, normalize=normalize), y)
        (dy,) = vjp(do_ref[...])
        dx_ref[...] = _conv_taps_bwd(x, dy, w_ref, dw_ref, 4, buf2).astype(dx_ref.dtype)

    return _pcall(
        body, grid=(8,),
        in_specs=[pl.BlockSpec((T, LANE), lambda j: (0, cb + j)), pl.BlockSpec((4, LANE), lambda j: (0, cb + j)),
                  pl.BlockSpec((T, LANE), lambda j: (0, j))] + b_specs,
        out_specs=[pl.BlockSpec((T, LANE), lambda j: (0, cb + j)), pl.BlockSpec((4, LANE), lambda j: (0, j))],
        out_shape=[SDS((T, PROJ_W), bf16), SDS((4, 1024), f32)], scratch_shapes=[pltpu.VMEM((T, LANE), f32)] * 2,
        input_output_aliases=aliases, compiler_params=_cparams(VMEM_BIG), name=name)(proj, w, dout, *b_args)


def _ffn_point(a, lin):
    return a * jax.nn.sigmoid(a) * lin


def _ffnact_fwd(name, up, w, b):
    T = up.shape[0]
    nblk = D_FF // LANE

    def body(a_ref, l_ref, w_ref, b_ref, o_ref, buf):
        a = _conv_taps(a_ref[...], w_ref, 3, buf) + b_ref[...]
        o_ref[...] = _ffn_point(a, l_ref[...]).astype(o_ref.dtype)

    return _pcall(
        body, grid=(nblk,),
        in_specs=[pl.BlockSpec((T, LANE), lambda j: (0, j)), pl.BlockSpec((T, LANE), lambda j: (0, nblk + j)),
                  pl.BlockSpec((3, LANE), lambda j: (0, j)), pl.BlockSpec((1, LANE), lambda j: (0, j))],
        out_specs=pl.BlockSpec((T, LANE), lambda j: (0, j)), out_shape=SDS((T, D_FF), bf16), scratch_shapes=[pltpu.VMEM((T, LANE), f32)],
        compiler_params=_cparams(VMEM_BIG), name=name)(up, up, w, b)


def _ffnact_bwd(name, up, w, b, dmid):
    T = up.shape[0]
    nblk = D_FF // LANE

    def body(a_ref, l_ref, w_ref, b_ref, dm_ref, da_ref, dl_ref, dw_ref, db_ref, buf, buf2):
        x = a_ref[...]
        a = _conv_taps(x, w_ref, 3, buf) + b_ref[...]
        _, vjp = jax.vjp(_ffn_point, a, l_ref[...])
        da, dl = vjp(dm_ref[...].astype(f32))
        dl_ref[...] = dl.astype(dl_ref.dtype)
        db_ref[...] = jnp.sum(da, axis=0, keepdims=True)
        da_ref[...] = _conv_taps_bwd(x, da, w_ref, dw_ref, 3, buf2).astype(da_ref.dtype)

    col = lambda r: pl.BlockSpec((r, LANE), lambda j: (0, j))
    return _pcall(
        body, grid=(nblk,),
        in_specs=[col(T), pl.BlockSpec((T, LANE), lambda j: (0, nblk + j)), col(3), col(1), col(T)],
        out_specs=[col(T), col(T), col(3), col(1)],
        out_shape=[SDS((T, D_FF), bf16), SDS((T, D_FF), bf16), SDS((3, D_FF), f32), SDS((1, D_FF), f32)],
        scratch_shapes=[pltpu.VMEM((T, LANE), f32)] * 2, compiler_params=_cparams(VMEM_BIG), name=name)(up, up, w, b, dmid)


def _bmm(a, b, ca, cb, precision=HI):
    return lax.dot_general(a, b, (((ca,), (cb,)), ((0,), (0,))), precision=precision, preferred_element_type=f32)


def _make_bdot(ca, cb):
    def raw(x, y, cx, cy):
        return _bmm(x.astype(bf16), y.astype(bf16), cx, cy, precision=None)

    @jax.custom_vjp
    def f(a, b):
        return raw(a, b, ca, cb)

    def fwd(a, b):
        return raw(a, b, ca, cb), (a, b)

    def bwd(res, dy):
        a, b = res
        if (ca, cb) == (2, 1):
            return raw(dy, b, 2, 2), raw(a, dy, 1, 1)
        if (ca, cb) == (2, 2):
            return raw(dy, b, 2, 1), raw(dy, a, 1, 1)
        return raw(b, dy, 2, 2), raw(a, dy, 2, 1)

    f.defvjp(fwd, bwd)
    return f


_bdot_nn, _bdot_nt, _bdot_tn = _make_bdot(2, 1), _make_bdot(2, 2), _make_bdot(1, 1)


def _pieces(a, n):
    out, r = [], a
    for _ in range(n):
        p = r.astype(bf16)
        out.append(p)
        r = r - p.astype(f32)
    return out


def _bmm_split(x, y, cx, cy, nx=2, ny=2, order=1):
    xs, ys = _pieces(x, nx), _pieces(y, ny)
    acc = None
    for i in reversed(range(nx)):
        for j in reversed(range(ny)):
            if i + j <= order:
                t = _bmm(xs[i], ys[j], cx, cy, precision=None)
                acc = t if acc is None else acc + t
    return acc


@jax.custom_vjp
def _solve_apply(X, r):
    return _bmm_split(X, r, 2, 1)


def _solve_apply_fwd(X, r):
    return _bmm_split(X, r, 2, 1), (X, r)


def _solve_apply_bwd(res, dy):
    X, r = res
    return _bmm_split(dy, r, 2, 2), _bmm_split(X, dy, 1, 1)


_solve_apply.defvjp(_solve_apply_fwd, _solve_apply_bwd)


def _lower_ones(H, C):
    ri = lax.broadcasted_iota(jnp.int32, (H, C, C), 1)
    ci = lax.broadcasted_iota(jnp.int32, (H, C, C), 2)
    return (ri >= ci).astype(f32)


def _cumsum_rows_raw(G):
    return _bmm_split(_lower_ones(G.shape[0], G.shape[1]), G, 2, 1, nx=1, ny=3, order=2)


@jax.custom_vjp
def _cumsum_rows(G):
    return _cumsum_rows_raw(G)


def _cumsum_rows_fwd(G):
    return _cumsum_rows_raw(G), None


def _cumsum_rows_bwd(_, dy):
    return (_bmm_split(_lower_ones(dy.shape[0], dy.shape[1]), dy, 1, 1, nx=1, ny=3, order=2),)


_cumsum_rows.defvjp(_cumsum_rows_fwd, _cumsum_rows_bwd)


def _tri_inverse_raw(L):
    H, C, _ = L.shape
    ri = lax.broadcasted_iota(jnp.int32, (C, C), 0)
    ci = lax.broadcasted_iota(jnp.int32, (C, C), 1)
    eye = jnp.broadcast_to((ri == ci).astype(f32)[None], (H, C, C))
    Dg = jnp.where(((ri >> 3) == (ci >> 3))[None], L, 0.0)
    D2 = _bmm_split(Dg, Dg, 2, 1)
    X = _bmm_split(_bmm_split(eye - Dg, eye + D2, 2, 1), eye + _bmm_split(D2, D2, 2, 1), 2, 1)
    for lg in range(3, C.bit_length() - 1):
        same = (ri >> (lg + 1)) == (ci >> (lg + 1))
        lower_left = same & (((ri >> lg) & 1) == 1) & (((ci >> lg) & 1) == 0)
        X = X - _bmm_split(_bmm_split(X, jnp.where(lower_left[None], L, 0.0), 2, 1), X, 2, 1)
    return X


@jax.custom_vjp
def _tri_inverse(L):
    return _tri_inverse_raw(L)


def _tri_inverse_fwd(L):
    X = _tri_inverse_raw(L)
    return X, X


def _tri_inverse_bwd(X, dX):
    return (-_bmm_split(_bmm_split(X, dX, 1, 1), X, 2, 2),)


_tri_inverse.defvjp(_tri_inverse_fwd, _tri_inverse_bwd)


@jax.custom_vjp
def _tri_inverse_known(L, X):
    return X


def _tri_inverse_known_fwd(L, X):
    return X, X


def _tri_inverse_known_bwd(X, dX):
    return _tri_inverse_bwd(X, dX)[0], jnp.zeros_like(X)


_tri_inverse_known.defvjp(_tri_inverse_known_fwd, _tri_inverse_known_bwd)


DN_NC = 2


def _gdn_chunk(q, k, v, ab, alog, dtb, S, X_known=None, keep_X=False):
    H, C, _ = q.shape
    NC, NH = ab.shape[0], H // ab.shape[0]
    lane = lax.broadcasted_iota(jnp.int32, (H, C, LANE), 2)
    head = lax.broadcasted_iota(jnp.int32, (H, C, LANE), 0) & (NH - 1)
    abb = jnp.concatenate([jnp.broadcast_to(ab[c][None], (NH, C, LANE)) for c in range(NC)], axis=0)
    a = jnp.sum(jnp.where(lane == head, abb, 0.0), axis=2, keepdims=True)
    b = jnp.sum(jnp.where(lane == head + 8, abb, 0.0), axis=2, keepdims=True)
    pick = lax.broadcasted_iota(jnp.int32, (H, 1, LANE), 2) == (lax.broadcasted_iota(jnp.int32, (H, 1, LANE), 0) & (NH - 1))
    al = jnp.sum(jnp.where(pick, alog[None], 0.0), axis=2, keepdims=True)
    db = jnp.sum(jnp.where(pick, dtb[None], 0.0), axis=2, keepdims=True)
    g = -jnp.exp(al) * jax.nn.softplus(a + db)
    beta = jax.nn.sigmoid(b)
    ri = lax.broadcasted_iota(jnp.int32, (C, C), 0)
    ci = lax.broadcasted_iota(jnp.int32, (C, C), 1)
    G = jnp.broadcast_to(g, (H, C, LANE))
    gc = _cumsum_rows(G)
    gi = _cumsum_rows(jnp.broadcast_to(g, (H, C, C)))
    decay = jnp.exp(jnp.where((ri >= ci)[None], gi - jnp.swapaxes(gi, 1, 2), -jnp.inf))
    qs = q * (LANE ** -0.5)
    kb = k * beta
    L = jnp.where((ri > ci)[None], _bdot_nt(kb, k) * decay, 0.0)
    X = _tri_inverse(L) if X_known is None else _tri_inverse_known(L, X_known)
    egc = jnp.exp(gc)
    u = _solve_apply(X, v * beta)
    w = _solve_apply(X, kb * egc)
    qk = _bdot_nt(qs, k) * decay
    g_last = jnp.sum(G, axis=1, keepdims=True)
    k_dec = k * jnp.exp(g_last - gc)
    q_dec = qs * egc
    e_last = jnp.exp(g_last)
    outs = []
    for c in range(NC):
        sl = slice(c * NH, (c + 1) * NH)
        v_new = u[sl] - _bdot_nn(w[sl], S)
        outs.append(_bdot_nn(q_dec[sl], S) + _bdot_nn(qk[sl], v_new))
        S = S * e_last[sl] + _bdot_tn(k_dec[sl], v_new)
    o = jnp.concatenate(outs, axis=0)
    return (o, S, X) if keep_X else (o, S)


def _heads(ref):
    return jnp.stack([ref[DN_C * c:DN_C * (c + 1), LANE * h:LANE * (h + 1)] for c in range(DN_NC) for h in range(8)], axis=0)


def _put_heads(ref, val):
    for c in range(DN_NC):
        for h in range(8):
            ref[DN_C * c:DN_C * (c + 1), LANE * h:LANE * (h + 1)] = val[8 * c + h]


def _chunk_rows(ref):
    return jnp.stack([ref[DN_C * c:DN_C * (c + 1), :] for c in range(DN_NC)], axis=0)


def _hosted_parts(hosted):
    if hosted is None:
        return [], [], [], []
    n = len(hosted["arrays"])
    return list(hosted["arrays"]), [HBM_SPEC] * n, list(hosted["out_shape"]), _dma_sems(hosted["n_sems"])


def _gdn_fwd(name, q, k, v, proj, alog, dtb, hosted=None):
    T = q.shape[0]
    R = DN_C * DN_NC
    N = T // R
    h_args, h_specs, h_shapes, h_sems = _hosted_parts(hosted)
    nh = len(h_args)

    def body(q_ref, k_ref, v_ref, ab_ref, al_ref, dt_ref, *rest):
        h_ins, (o_ref, sall_ref, xinv_ref), h_outs, s_scr, sems = rest[:nh], rest[nh:nh + 3], rest[nh + 3:2 * nh + 3], rest[2 * nh + 3], rest[2 * nh + 4:]
        step = pl.program_id(0)

        @pl.when(step == 0)
        def _():
            s_scr[...] = jnp.zeros_like(s_scr)
            if hosted is not None:
                hosted["start"](h_ins, h_outs, *sems)

        S = s_scr[...]
        sall_ref[...] = S
        o, S_new, X = _gdn_chunk(_heads(q_ref), _heads(k_ref), _heads(v_ref), _chunk_rows(ab_ref), al_ref[...], dt_ref[...], S, keep_X=True)
        _put_heads(o_ref, o)
        s_scr[...] = S_new
        xinv_ref[...] = X

        if hosted is not None:
            @pl.when(step == N - 1)
            def _():
                hosted["finish"](h_ins, h_outs, *sems)

    blk = pl.BlockSpec((R, 8 * LANE), lambda n: (n, 0))
    vec = pl.BlockSpec((1, LANE), lambda n: (0, 0))
    state = pl.BlockSpec((None, 8, LANE, LANE), lambda n: (n, 0, 0, 0))
    xinv = pl.BlockSpec((None, 8 * DN_NC, DN_C, DN_C), lambda n: (n, 0, 0, 0))
    outs = _pcall(
        body, grid=(N,), in_specs=[blk, blk, blk, pl.BlockSpec((R, LANE), lambda n: (n, CB_AB)), vec, vec] + h_specs,
        out_specs=[blk, state, xinv] + h_specs,
        out_shape=[SDS((T, 1024), f32), SDS((N, 8, LANE, LANE), f32), SDS((N, 8 * DN_NC, DN_C, DN_C), f32)] + h_shapes,
        scratch_shapes=[pltpu.VMEM((8, LANE, LANE), f32)] + h_sems, name=name)(q, k, v, proj, alog, dtb, *h_args)
    return outs[0], outs[1], outs[2], list(outs[3:])


def _gdn_bwd(name, q, k, v, proj, alog, dtb, sall, xinv, do, hosted=None):
    T = q.shape[0]
    R = DN_C * DN_NC
    N = T // R
    h_args, h_specs, h_shapes, h_sems = _hosted_parts(hosted)
    nh = len(h_args)

    def body(q_ref, k_ref, v_ref, ab_ref, al_ref, dt_ref, s_ref, x_ref, do_ref, *rest):
        h_ins, h_outs, ds_scr, sems = rest[:nh], rest[nh + 6:2 * nh + 6], rest[2 * nh + 6], rest[2 * nh + 7:]
        dq_ref, dk_ref, dv_ref, dab_ref, dal_ref, ddt_ref = rest[nh:nh + 6]
        step = pl.program_id(0)

        @pl.when(step == 0)
        def _():
            ds_scr[...] = jnp.zeros_like(ds_scr)
            dal_ref[...] = jnp.zeros_like(dal_ref)
            ddt_ref[...] = jnp.zeros_like(ddt_ref)
            if hosted is not None:
                hosted["start"](h_ins, h_outs, *sems)

        _, vjp = jax.vjp(functools.partial(_gdn_chunk, X_known=x_ref[...]), _heads(q_ref), _heads(k_ref), _heads(v_ref), _chunk_rows(ab_ref),
                         al_ref[...], dt_ref[...], s_ref[...])
        dq, dk, dv, dab, dal, ddt, dS = vjp((_heads(do_ref), ds_scr[...]))
        _put_heads(dq_ref, dq)
        _put_heads(dk_ref, dk)
        _put_heads(dv_ref, dv)
        ds_scr[...] = dS
        for c in range(DN_NC):
            dab_ref[DN_C * c:DN_C * (c + 1), :] = dab[c]
        dal_ref[...] += dal
        ddt_ref[...] += ddt

        if hosted is not None:
            @pl.when(step == N - 1)
            def _():
                hosted["finish"](h_ins, h_outs, *sems)

    blk = pl.BlockSpec((R, 8 * LANE), lambda n: (N - 1 - n, 0))
    vec = pl.BlockSpec((1, LANE), lambda n: (0, 0))
    state = pl.BlockSpec((None, 8, LANE, LANE), lambda n: (N - 1 - n, 0, 0, 0))
    outs = _pcall(
        body, grid=(N,),
        in_specs=[blk, blk, blk, pl.BlockSpec((R, LANE), lambda n: (N - 1 - n, CB_AB)), vec, vec, state,
                  pl.BlockSpec((None, 8 * DN_NC, DN_C, DN_C), lambda n: (N - 1 - n, 0, 0, 0)), blk] + h_specs,
        out_specs=[blk, blk, blk, pl.BlockSpec((R, LANE), lambda n: (N - 1 - n, 0)), vec, vec] + h_specs,
        out_shape=[SDS((T, 1024), f32)] * 3 + [SDS((T, LANE), f32), SDS((1, LANE), f32), SDS((1, LANE), f32)] + h_shapes,
        scratch_shapes=[pltpu.VMEM((8, LANE, LANE), f32)] + h_sems, name=name)(q, k, v, proj, alog, dtb, sall, xinv, do, *h_args)
    return tuple(outs[:6]), list(outs[6:])


def _segmean_raw(x2, bd):
    return jnp.concatenate([_dot(x2[:, LANE * j:LANE * (j + 1)], bd, 1, 0) for j in range(x2.shape[1] // LANE)], axis=1)


@jax.custom_vjp
def _segmean(x2, bd):
    return _segmean_raw(x2, bd)


def _segmean_fwd(x2, bd):
    return _segmean_raw(x2, bd), bd


def _segmean_bwd(bd, dy):
    return _segmean_raw(dy, bd), jnp.zeros_like(bd)


_segmean.defvjp(_segmean_fwd, _segmean_bwd)


def _qknorm_fn(x, w, bd):
    return x * lax.rsqrt(_segmean(x * x, bd) + EPS) * w


def _rope_apply(xn, c, s1, s2):
    W = xn.shape[1]
    return xn * c + pltpu.roll(xn, W - 8, 1) * s1 + pltpu.roll(xn, 8, 1) * s2


def _rope_apply_t(d, c, s1, s2):
    W = d.shape[1]
    return d * c + pltpu.roll(d * s1, 8, 1) + pltpu.roll(d * s2, W - 8, 1)


def _rope_tiles(refs, width):
    return [jnp.tile(r[...], (1, width // LANE)) for r in refs]


def _qkprep_fwd(name, proj, wb, width, w, bd, tabs):
    T = proj.shape[0]
    tr = _pick(T, (256, 128))

    def body(x_ref, w_ref, bd_ref, c_ref, s1_ref, s2_ref, o_ref):
        xn = _qknorm_fn(x_ref[...], w_ref[...], bd_ref[...])
        o_ref[...] = _rope_apply(xn, *_rope_tiles((c_ref, s1_ref, s2_ref), width))

    row0 = pl.BlockSpec((tr, width), lambda i: (i, 0))
    tab = pl.BlockSpec((tr, LANE), lambda i: (i, 0))
    full = lambda a: pl.BlockSpec(a.shape, lambda i: (0, 0))
    return _pcall(
        body, grid=(T // tr,), in_specs=[pl.BlockSpec((tr, width), lambda i: (i, wb)), full(w), full(bd), tab, tab, tab],
        out_specs=row0, out_shape=SDS((T, width), f32), name=name)(proj, w, bd, *tabs)


def _qkprep_bwd(name, proj, wb, width, w, bd, tabs, dout, dest_buf="none"):
    T = proj.shape[0]
    tr = _pick(T, (256, 128))
    into = not isinstance(dest_buf, str)
    b_args, b_specs, aliases = _into_buffer(dest_buf, 7, 0) if into else ([], [], {})

    def body(x_ref, w_ref, bd_ref, c_ref, s1_ref, s2_ref, do_ref, *rest):
        dx_ref, dw_ref = rest[len(b_args):]
        i = pl.program_id(0)
        dxn = _rope_apply_t(do_ref[...], *_rope_tiles((c_ref, s1_ref, s2_ref), width))
        bd = bd_ref[...]
        _, vjp = jax.vjp(lambda x, w_: _qknorm_fn(x, w_, bd), x_ref[...], w_ref[...])
        dx, dw = vjp(dxn)
        dx_ref[...] = dx.astype(dx_ref.dtype)

        @pl.when(i == 0)
        def _():
            dw_ref[...] = jnp.zeros_like(dw_ref)

        dw_ref[...] += dw

    row0 = pl.BlockSpec((tr, width), lambda i: (i, 0))
    tab = pl.BlockSpec((tr, LANE), lambda i: (i, 0))
    full = lambda a: pl.BlockSpec(a.shape, lambda i: (0, 0))
    dx_spec = pl.BlockSpec((tr, width), lambda i: (i, wb)) if into else row0
    dx_shape = SDS((T, PROJ_W), bf16) if into else SDS((T, width), bf16)
    return _pcall(
        body, grid=(T // tr,), in_specs=[pl.BlockSpec((tr, width), lambda i: (i, wb)), full(w), full(bd), tab, tab, tab, row0] + b_specs,
        out_specs=[dx_spec, full(w)], out_shape=[dx_shape, SDS(w.shape, f32)], input_output_aliases=aliases, name=name)(
            proj, w, bd, *tabs, dout, *b_args)


def _make_dot16(ca, cb):
    def raw(x, y, cx, cy):
        return _dot(x.astype(bf16), y.astype(bf16), cx, cy, precision=None)

    @jax.custom_vjp
    def f(a, b):
        return raw(a, b, ca, cb)

    def fwd(a, b):
        return raw(a, b, ca, cb), (a, b)

    def bwd(res, dy):
        a, b = res
        if (ca, cb) == (1, 0):
            return raw(dy, b, 1, 1), raw(a, dy, 0, 0)
        return raw(dy, b, 1, 0), raw(dy, a, 0, 0)

    f.defvjp(fwd, bwd)
    return f


_dot16_nn, _dot16_nt = _make_dot16(1, 0), _make_dot16(1, 1)


def _attn_bias():
    qi = jnp.arange(8 * SWA_B) % SWA_B
    kj = jnp.arange(2 * SWA_B)
    rel = qi[:, None] + SWA_B - kj[None, :]
    valid = (rel >= 0) & (rel < SWA_B)
    neg = jnp.float32(-jnp.inf)
    return jnp.stack([jnp.where(valid & (kj[None, :] >= SWA_B), 0.0, neg), jnp.where(valid, 0.0, neg)]).astype(f32)


def _attn_group(qg, kb, vb, sinks, bias, hk):
    R = qg.shape[0]
    s = _dot16_nt(qg, kb) * 0.125 + bias
    head = (lax.broadcasted_iota(jnp.int32, (R, LANE), 0) >> 7) + 8 * hk
    lane = lax.broadcasted_iota(jnp.int32, (R, LANE), 1)
    sink = jnp.sum(jnp.where(lane == head, jnp.broadcast_to(sinks, (R, LANE)), 0.0), axis=1, keepdims=True)
    m = lax.stop_gradient(jnp.maximum(jnp.max(s, axis=1, keepdims=True), sink))
    p = jnp.exp(s - m)
    denom = jnp.sum(p, axis=1, keepdims=True) + jnp.exp(sink - m)
    return _dot16_nn(p / denom, vb)


def _group_rows(ref, hk):
    return jnp.concatenate([ref[:, 64 * (8 * hk + g):64 * (8 * hk + g + 1)] for g in range(8)], axis=0)


def _put_group(ref, hk, val):
    for g in range(8):
        ref[:, 64 * (8 * hk + g):64 * (8 * hk + g + 1)] = val[SWA_B * g:SWA_B * (g + 1)].astype(ref.dtype)


def _attn_specs():
    qs = pl.BlockSpec((SWA_B, 1024), lambda i: (i, 0))
    cur = pl.BlockSpec((SWA_B, LANE), lambda i: (i, 0))
    prev = pl.BlockSpec((SWA_B, LANE), lambda i: (jnp.maximum(i - 1, 0), 0))
    vcur = pl.BlockSpec((SWA_B, LANE), lambda i: (i, CB_SWV))
    vprev = pl.BlockSpec((SWA_B, LANE), lambda i: (jnp.maximum(i - 1, 0), CB_SWV))
    vec = pl.BlockSpec((1, LANE), lambda i: (0, 0))
    bias = pl.BlockSpec((None, 8 * SWA_B, 2 * SWA_B), lambda i: (jnp.minimum(i, 1), 0, 0))
    return qs, cur, prev, vcur, vprev, vec, bias


def _hosted_edge(hosted, which, h_ins, h_outs, sems, at):
    if hosted is None:
        return

    @pl.when(at)
    def _():
        hosted[which](h_ins, h_outs, *sems)


def _attn_fwd(name, sq, sk, proj, sinks, bias, hosted=None):
    T = sq.shape[0]
    nb = T // SWA_B
    h_args, h_specs, h_shapes, h_sems = _hosted_parts(hosted)
    nh = len(h_args)

    def body(q_ref, kp_ref, kc_ref, vp_ref, vc_ref, sk_ref, b_ref, *rest):
        h_ins, o_ref, h_outs, sems = rest[:nh], rest[nh], rest[nh + 1:2 * nh + 1], rest[2 * nh + 1:]
        _hosted_edge(hosted, "start", h_ins, h_outs, sems, pl.program_id(0) == 0)
        sinks_v, bias_v = sk_ref[...], b_ref[...]
        for hk in range(2):
            ks = slice(64 * hk, 64 * hk + 64)
            kb = jnp.concatenate([kp_ref[:, ks], kc_ref[:, ks]], axis=0)
            vb = jnp.concatenate([vp_ref[:, ks], vc_ref[:, ks]], axis=0)
            _put_group(o_ref, hk, _attn_group(_group_rows(q_ref, hk), kb, vb, sinks_v, bias_v, hk))
        _hosted_edge(hosted, "finish", h_ins, h_outs, sems, pl.program_id(0) == nb - 1)

    qs, cur, prev, vcur, vprev, vec, bspec = _attn_specs()
    outs = _pcall(body, grid=(nb,), in_specs=[qs, prev, cur, vprev, vcur, vec, bspec] + h_specs, out_specs=[qs] + h_specs,
                  out_shape=[SDS((T, 1024), bf16)] + h_shapes, scratch_shapes=h_sems, name=name)(sq, sk, sk, proj, proj, sinks, bias, *h_args)
    return outs[0], list(outs[1:])


def _attn_bwd(name, sq, sk, proj, sinks, bias, do, hosted=None):
    T = sq.shape[0]
    nb = T // SWA_B
    h_args, h_specs, h_shapes, h_sems = _hosted_parts(hosted)
    nh = len(h_args)

    def body(q_ref, kp_ref, kc_ref, vp_ref, vc_ref, sk_ref, b_ref, do_ref, *rest):
        h_ins, h_outs, sems = rest[:nh], rest[nh + 6:2 * nh + 6], rest[2 * nh + 6:]
        dq_ref, dkp_ref, dkc_ref, dvp_ref, dvc_ref, dsk_ref = rest[nh:nh + 6]
        _hosted_edge(hosted, "start", h_ins, h_outs, sems, pl.program_id(0) == 0)

        @pl.when(pl.program_id(0) == 0)
        def _():
            dsk_ref[...] = jnp.zeros_like(dsk_ref)

        sinks_v, bias_v = sk_ref[...], b_ref[...]
        dsk = jnp.zeros((1, LANE), f32)
        for hk in range(2):
            ks = slice(64 * hk, 64 * hk + 64)
            kb = jnp.concatenate([kp_ref[:, ks], kc_ref[:, ks]], axis=0)
            vb = jnp.concatenate([vp_ref[:, ks], vc_ref[:, ks]], axis=0)
            _, vjp = jax.vjp(functools.partial(_attn_group, bias=bias_v, hk=hk), _group_rows(q_ref, hk), kb, vb, sinks_v)
            dq, dkb, dvb, ds_ = vjp(_group_rows(do_ref, hk))
            _put_group(dq_ref, hk, dq)
            dsk = dsk + ds_
            dkp_ref[:, ks] = dkb[:SWA_B]
            dkc_ref[:, ks] = dkb[SWA_B:]
            dvp_ref[:, ks] = dvb[:SWA_B]
            dvc_ref[:, ks] = dvb[SWA_B:]
        dsk_ref[...] += dsk
        _hosted_edge(hosted, "finish", h_ins, h_outs, sems, pl.program_id(0) == nb - 1)

    qs, cur, prev, vcur, vprev, vec, bspec = _attn_specs()
    outs = _pcall(
        body, grid=(nb,), in_specs=[qs, prev, cur, vprev, vcur, vec, bspec, qs] + h_specs, out_specs=[qs, cur, cur, cur, cur, vec] + h_specs,
        out_shape=[SDS((T, 1024), f32)] + [SDS((T, LANE), f32)] * 4 + [SDS((1, LANE), f32)] + h_shapes, scratch_shapes=h_sems,
        name=name)(sq, sk, sk, proj, proj, sinks, bias, do, *h_args)
    return tuple(outs[:6]), list(outs[6:])


def _shift_add(name, cur, prev, out_dtype):
    T = cur.shape[0]

    def body(c_ref, p_ref, o_ref):
        o_ref[0:T - SWA_B, :] = (c_ref[0:T - SWA_B, :] + p_ref[SWA_B:T, :]).astype(o_ref.dtype)
        o_ref[T - SWA_B:T, :] = c_ref[T - SWA_B:T, :].astype(o_ref.dtype)

    return _pcall(body, out_shape=SDS((T, LANE), out_dtype), name=name)(cur, prev)


def _loss(name, y, tgt):
    T = y.shape[0]

    def body(y_ref, t_ref, l_ref, dy_ref):
        @pl.when(pl.program_id(0) == 0)
        def _():
            l_ref[...] = jnp.zeros_like(l_ref)

        d = y_ref[...] - t_ref[...]
        l_ref[...] += jnp.sum(d * d) * (0.5 / D)
        dy_ref[...] = d * (1.0 / D)

    row = pl.BlockSpec((TR, D), lambda i: (i, 0))
    return _pcall(body, grid=(T // TR,), in_specs=[row, row], out_specs=[pl.BlockSpec((8, LANE), lambda i: (0, 0)), row],
                  out_shape=[SDS((8, LANE), f32), SDS((T, D), f32)], name=name)(y, tgt)


def _adamw(name, w, g, m, v):
    shape = w.shape
    C = shape[-1]
    R = int(np.prod(shape[:-1]))
    tr = _pick(R, (128, 64, 16, 8))
    bc1 = np.float32(1.0 - ADAM_B1 ** ADAM_STEP)
    bc2 = np.float32(1.0 - ADAM_B2 ** ADAM_STEP)

    def body(w_ref, g_ref, m_ref, v_ref, d_ref, mo_ref, vo_ref):
        g_ = g_ref[...]
        m_ = ADAM_B1 * m_ref[...] + (1.0 - ADAM_B1) * g_
        v_ = ADAM_B2 * v_ref[...] + (1.0 - ADAM_B2) * (g_ * g_)
        d_ref[...] = -ADAM_LR * ((m_ / bc1) / (jnp.sqrt(v_ / bc2) + ADAM_EPS) + ADAM_WD * w_ref[...])
        mo_ref[...] = m_
        vo_ref[...] = v_

    blk = pl.BlockSpec((tr, C), lambda i: (i, 0))
    outs = _pcall(body, grid=(R // tr,), in_specs=[blk] * 4, out_specs=[blk] * 3, out_shape=[SDS((R, C), f32)] * 3,
                  compiler_params=_cparams(VMEM_BIG), name=name)(*[t.reshape(R, C) for t in (w, g, m, v)])
    return [o.reshape(shape) for o in outs]


def _silu_rows(name, x):
    def body(x_ref, o_ref):
        t = x_ref[...]
        o_ref[...] = (t * jax.nn.sigmoid(t)).astype(o_ref.dtype)

    return _pcall(body, out_shape=SDS(x.shape, bf16), name=name)(x)


def _sum_leading(name, x):
    n = x.shape[0]

    def body(x_ref, o_ref):
        acc = x_ref[0]
        for k in range(1, n):
            acc = acc + x_ref[k]
        o_ref[...] = acc

    tr = x.shape[1] if x.size * 4 <= 8 * 2 ** 20 else _pick(x.shape[1], (COMM_TR, 8))
    return _pcall(body, grid=(x.shape[1] // tr,), in_specs=[pl.BlockSpec((n, tr, x.shape[2]), lambda i: (0, i, 0))],
                  out_specs=pl.BlockSpec((tr, x.shape[2]), lambda i: (i, 0)), out_shape=SDS(x.shape[1:], x.dtype), name=name)(x)


def _add_my_half(name, g4, b1, c):
    _, R, W = g4.shape
    nblk = (R // 2) // COMM_TR

    def body(c_ref, g_ref, b_ref, o_ref):
        o_ref[...] = g_ref[...] + b_ref[...]

    grid_spec = pltpu.PrefetchScalarGridSpec(
        num_scalar_prefetch=1, grid=(4, nblk),
        in_specs=[pl.BlockSpec((None, COMM_TR, W), lambda s, i, c_ref: (s, c_ref[0] * nblk + i, 0)),
                  pl.BlockSpec((None, COMM_TR, W), lambda s, i, c_ref: (s, i, 0))],
        out_specs=pl.BlockSpec((None, COMM_TR, W), lambda s, i, c_ref: (s, i, 0)))
    return _pcall(body, grid_spec=grid_spec, out_shape=SDS((4, R // 2, W), f32), name=name)(c.reshape(1), g4, b1)


HBM_SPEC = pl.BlockSpec(memory_space=pltpu.HBM)


def _position():
    x, y, c = lax.axis_index("x"), lax.axis_index("y"), lax.axis_index("c")
    return x, y, c, [(1 - x, y), (x, 1 - y), (1 - x, 1 - y)]


def _remote(src, dst, send_sems, recv_sems, k, to):
    return pltpu.make_async_remote_copy(src_ref=src, dst_ref=dst, send_sem=send_sems.at[k], recv_sem=recv_sems.at[k],
                                        device_id=to, device_id_type=MESH)


def _allgather_chips(name, buf):
    R, W = buf.shape
    Rh = R // 2

    def body(in_ref, out_ref, send_sems, recv_sems, local_sem):
        x, y, c, chips = _position()
        me = 2 * x + y
        sib = (x, y, 1 - c)
        half = pl.ds(pl.multiple_of(c * Rh, 32), Rh)
        ohalf = pl.ds(pl.multiple_of((1 - c) * Rh, 32), Rh)
        mine = pltpu.make_async_copy(in_ref, out_ref.at[me], local_sem)
        mine.start()
        first = [_remote(in_ref.at[half], out_ref.at[me, half], send_sems, recv_sems, j, (cx, cy, c)) for j, (cx, cy) in enumerate(chips)]
        for cp in first:
            cp.start()
        passed = []
        for j, (cx, cy) in enumerate(chips):
            rows = out_ref.at[2 * cx + cy, half]
            _remote(rows, rows, send_sems, recv_sems, j, (cx, cy, c)).wait_recv()
            cp = _remote(rows, rows, send_sems, recv_sems, 3 + j, sib)
            cp.start()
            passed.append(cp)
        for j, (cx, cy) in enumerate(chips):
            rows = out_ref.at[2 * cx + cy, ohalf]
            _remote(rows, rows, send_sems, recv_sems, 3 + j, sib).wait_recv()
        for cp in first + passed:
            cp.wait_send()
        mine.wait()

    return _pcall(body, in_specs=[HBM_SPEC], out_specs=HBM_SPEC, out_shape=SDS((4, R, W), buf.dtype),
                  scratch_shapes=[pltpu.SemaphoreType.DMA((6,)), pltpu.SemaphoreType.DMA((6,)), pltpu.SemaphoreType.DMA], name=name)(buf)


def _swap_halves(name, g4):
    _, R, W = g4.shape
    Rh = R // 2

    def body(in_ref, out_ref, send_sems, recv_sems):
        x, y, c, _ = _position()
        ohalf = pl.ds(pl.multiple_of((1 - c) * Rh, 32), Rh)
        cp = _remote(in_ref.at[:, ohalf, :], out_ref, send_sems, recv_sems, 0, (x, y, 1 - c))
        cp.start()
        cp.wait()

    return _pcall(body, in_specs=[HBM_SPEC], out_specs=HBM_SPEC, out_shape=SDS((4, Rh, W), g4.dtype),
                  scratch_shapes=[pltpu.SemaphoreType.DMA((1,)), pltpu.SemaphoreType.DMA((1,))], name=name)(g4)


def _scatter_chips(name, p4):
    _, Rh, W = p4.shape

    def body(in_ref, out_ref, send_sems, recv_sems, local_sem):
        x, y, c, chips = _position()
        me = 2 * x + y
        mine = pltpu.make_async_copy(in_ref.at[me], out_ref.at[me], local_sem)
        mine.start()
        sends = [_remote(in_ref.at[2 * cx + cy], out_ref.at[me], send_sems, recv_sems, j, (cx, cy, c)) for j, (cx, cy) in enumerate(chips)]
        for cp in sends:
            cp.start()
        for j, (cx, cy) in enumerate(chips):
            slot = out_ref.at[2 * cx + cy]
            _remote(slot, slot, send_sems, recv_sems, j, (cx, cy, c)).wait_recv()
        for cp in sends:
            cp.wait_send()
        mine.wait()

    return _pcall(body, in_specs=[HBM_SPEC], out_specs=HBM_SPEC, out_shape=SDS((4, Rh, W), p4.dtype),
                  scratch_shapes=[pltpu.SemaphoreType.DMA((3,)), pltpu.SemaphoreType.DMA((3,)), pltpu.SemaphoreType.DMA], name=name)(p4)


def _join_halves(name, r):
    Rh, W = r.shape

    def body(in_ref, out_ref, send_sems, recv_sems, local_sem):
        x, y, c, _ = _position()
        mine = pltpu.make_async_copy(in_ref, out_ref.at[c], local_sem)
        mine.start()
        cp = _remote(in_ref, out_ref.at[c], send_sems, recv_sems, 0, (x, y, 1 - c))
        cp.start()
        _remote(in_ref, out_ref.at[1 - c], send_sems, recv_sems, 0, (x, y, 1 - c)).wait_recv()
        cp.wait_send()
        mine.wait()

    return _pcall(body, in_specs=[HBM_SPEC], out_specs=HBM_SPEC, out_shape=SDS((2, Rh, W), r.dtype),
                  scratch_shapes=[pltpu.SemaphoreType.DMA((1,)), pltpu.SemaphoreType.DMA((1,)), pltpu.SemaphoreType.DMA], name=name)(r)


def _allgather_all(name, buf):
    r, W = buf.shape

    def body(in_ref, out_ref, send_sems, recv_sems, local_sem):
        x, y, c, _ = _position()
        me = 4 * x + 2 * y + c
        mine = pltpu.make_async_copy(in_ref, out_ref.at[me], local_sem)
        mine.start()
        peers = []
        for mk in range(1, 8):
            mx, my, mc = (mk >> 2) & 1, (mk >> 1) & 1, mk & 1
            px = 1 - x if mx else x
            py = 1 - y if my else y
            pc = 1 - c if mc else c
            peers.append((px, py, pc))
        sends = [_remote(in_ref, out_ref.at[me], send_sems, recv_sems, k, p) for k, p in enumerate(peers)]
        for cp in sends:
            cp.start()
        for k, (px, py, pc) in enumerate(peers):
            slot = out_ref.at[4 * px + 2 * py + pc]
            _remote(slot, slot, send_sems, recv_sems, k, (px, py, pc)).wait_recv()
        for cp in sends:
            cp.wait_send()
        mine.wait()

    return _pcall(body, in_specs=[HBM_SPEC], out_specs=HBM_SPEC, out_shape=SDS((8, r, W), buf.dtype),
                  scratch_shapes=[pltpu.SemaphoreType.DMA((7,)), pltpu.SemaphoreType.DMA((7,)), pltpu.SemaphoreType.DMA], name=name)(buf)


def _reduce_scatter(g4, c, tag):
    b1 = _swap_halves("rs_swap_" + tag, g4)
    p4 = _add_my_half("rs_pair_" + tag, g4, b1, c)
    b2 = _scatter_chips("rs_scatter_" + tag, p4)
    r = _sum_leading("rs_sum_" + tag, b2)
    full = _join_halves("rs_join_" + tag, r)
    return full.reshape(g4.shape[1], g4.shape[2])


def _dma_sems(n):
    return [pltpu.SemaphoreType.DMA((n,)), pltpu.SemaphoreType.DMA((n,))]


def _gather_chips(name, shards):
    n = len(shards)

    def body(*refs):
        ins, outs = refs[:n], refs[n:2 * n]
        send_sems, recv_sems = refs[2 * n:]
        x, y, c, chips = _position()
        me = 2 * x + y
        sib = (x, y, 1 - c)
        sends, halves = [], []
        for i in range(n):
            rh = ins[i].shape[0] // 2
            halves.append((pl.ds(pl.multiple_of(c * rh, 16), rh), pl.ds(pl.multiple_of((1 - c) * rh, 16), rh)))
        for i in range(n):
            for j, (cx, cy) in enumerate(chips):
                cp = _remote(ins[i].at[halves[i][0]], outs[i].at[me, halves[i][0]], send_sems, recv_sems, 6 * i + j, (cx, cy, c))
                cp.start()
                sends.append(cp)
        for j, (cx, cy) in enumerate(chips):
            for i in range(n):
                rows = outs[i].at[2 * cx + cy, halves[i][0]]
                _remote(rows, rows, send_sems, recv_sems, 6 * i + j, (cx, cy, c)).wait_recv()
                cp = _remote(rows, rows, send_sems, recv_sems, 6 * i + 3 + j, sib)
                cp.start()
                sends.append(cp)
        for j, (cx, cy) in enumerate(chips):
            for i in range(n):
                rows = outs[i].at[2 * cx + cy, halves[i][1]]
                _remote(rows, rows, send_sems, recv_sems, 6 * i + 3 + j, sib).wait_recv()
        for cp in sends:
            cp.wait_send()

    return _pcall(body, in_specs=[HBM_SPEC] * n, out_specs=[HBM_SPEC] * n, out_shape=[SDS((4,) + s.shape, s.dtype) for s in shards],
                  scratch_shapes=_dma_sems(6 * n), name=name)(*shards)


def _swap_halves_multi(name, slots):
    n = len(slots)

    def body(*refs):
        ins, outs = refs[:n], refs[n:2 * n]
        send_sems, recv_sems = refs[2 * n:]
        x, y, c, _ = _position()
        cps = []
        for i in range(n):
            rh = ins[i].shape[1] // 2
            ohalf = pl.ds(pl.multiple_of((1 - c) * rh, 8), rh)
            cp = _remote(ins[i].at[:, ohalf, :], outs[i], send_sems, recv_sems, i, (x, y, 1 - c))
            cp.start()
            cps.append(cp)
        for cp in cps:
            cp.wait()

    return _pcall(body, in_specs=[HBM_SPEC] * n, out_specs=[HBM_SPEC] * n,
                  out_shape=[SDS((4, s.shape[1] // 2, s.shape[2]), s.dtype) for s in slots], scratch_shapes=_dma_sems(n), name=name)(*slots)


def _pair_add(name, g4, b1, c):
    _, R, W = g4.shape
    tr = _pick(R // 2, (256, 128, 32, 16))
    nblk = (R // 2) // tr

    def body(c_ref, g_ref, b_ref, o_ref):
        o_ref[...] = (g_ref[...] + b_ref[...]).astype(o_ref.dtype)

    grid_spec = pltpu.PrefetchScalarGridSpec(
        num_scalar_prefetch=1, grid=(4, nblk),
        in_specs=[pl.BlockSpec((None, tr, W), lambda s, i, c_ref: (s, c_ref[0] * nblk + i, 0)),
                  pl.BlockSpec((None, tr, W), lambda s, i, c_ref: (s, i, 0))],
        out_specs=pl.BlockSpec((None, tr, W), lambda s, i, c_ref: (s, i, 0)))
    return _pcall(body, grid_spec=grid_spec, out_shape=SDS((4, R // 2, W), bf16), name=name)(c.reshape(1), g4, b1)


def _scatter_chips_multi(name, ps):
    n = len(ps)

    def body(*refs):
        ins, outs = refs[:n], refs[n:2 * n]
        send_sems, recv_sems = refs[2 * n:]
        x, y, c, chips = _position()
        me = 2 * x + y
        sends = []
        for i in range(n):
            for j, (cx, cy) in enumerate(chips):
                cp = _remote(ins[i].at[2 * cx + cy], outs[i].at[me], send_sems, recv_sems, 3 * i + j, (cx, cy, c))
                cp.start()
                sends.append(cp)
        for i in range(n):
            for j, (cx, cy) in enumerate(chips):
                slot = outs[i].at[2 * cx + cy]
                _remote(slot, slot, send_sems, recv_sems, 3 * i + j, (cx, cy, c)).wait_recv()
        for cp in sends:
            cp.wait_send()

    return _pcall(body, in_specs=[HBM_SPEC] * n, out_specs=[HBM_SPEC] * n, out_shape=[SDS(p.shape, p.dtype) for p in ps],
                  scratch_shapes=_dma_sems(3 * n), name=name)(*ps)


def _sum_chips(name, p4, b2, chip, c):
    _, Rh, W = p4.shape
    tr = _pick(Rh, (256, 128, 32, 16))
    nblk = Rh // tr

    def body(m_ref, c_ref, own_ref, r1_ref, r2_ref, r3_ref, o_ref):
        o_ref[...] = ((own_ref[...].astype(f32) + r1_ref[...].astype(f32)) + r2_ref[...].astype(f32)) + r3_ref[...].astype(f32)

    other = lambda k: pl.BlockSpec((None, tr, W), lambda i, m_ref, c_ref: (m_ref[0] ^ k, i, 0))
    grid_spec = pltpu.PrefetchScalarGridSpec(
        num_scalar_prefetch=2, grid=(nblk,),
        in_specs=[pl.BlockSpec((None, tr, W), lambda i, m_ref, c_ref: (m_ref[0], i, 0)), other(1), other(2), other(3)],
        out_specs=pl.BlockSpec((tr, W), lambda i, m_ref, c_ref: (c_ref[0] * nblk + i, 0)))
    return _pcall(body, grid_spec=grid_spec, out_shape=SDS((2 * Rh, W), f32), name=name)(chip.reshape(1), c.reshape(1), p4, b2, b2, b2)


def _join_halves(name, fulls):
    n = len(fulls)

    def body(*refs):
        outs = refs[n:2 * n]
        send_sems, recv_sems = refs[2 * n:]
        x, y, c, _ = _position()
        cps = []
        for i in range(n):
            rh = outs[i].shape[0] // 2
            mine = outs[i].at[pl.ds(pl.multiple_of(c * rh, 8), rh)]
            theirs = outs[i].at[pl.ds(pl.multiple_of((1 - c) * rh, 8), rh)]
            cp = _remote(mine, mine, send_sems, recv_sems, i, (x, y, 1 - c))
            cp.start()
            cps.append((cp, _remote(theirs, theirs, send_sems, recv_sems, i, (x, y, 1 - c))))
        for cp, back in cps:
            back.wait_recv()
            cp.wait_send()

    return _pcall(body, in_specs=[HBM_SPEC] * n, out_specs=[HBM_SPEC] * n, out_shape=[SDS(r.shape, r.dtype) for r in fulls],
                  input_output_aliases={i: i for i in range(n)}, scratch_shapes=_dma_sems(n), name=name)(*fulls)


def _hosted_gather(shards):
    n = len(shards)

    def half(ref_rows, c):
        rh = ref_rows // 2
        return pl.ds(pl.multiple_of(c * rh, 16), rh)

    def start(ins, outs, send_sems, recv_sems):
        x, y, c, chips = _position()
        me = 2 * x + y
        for i in range(n):
            rows = half(ins[i].shape[0], c)
            for j, (cx, cy) in enumerate(chips):
                _remote(ins[i].at[rows], outs[i].at[me, rows], send_sems, recv_sems, 3 * i + j, (cx, cy, c)).start()

    def finish(ins, outs, send_sems, recv_sems):
        x, y, c, chips = _position()
        me = 2 * x + y
        for i in range(n):
            rows = half(ins[i].shape[0], c)
            for j, (cx, cy) in enumerate(chips):
                _remote(ins[i].at[rows], outs[i].at[2 * cx + cy, rows], send_sems, recv_sems, 3 * i + j, (cx, cy, c)).wait_recv()
        for i in range(n):
            rows = half(ins[i].shape[0], c)
            for j, (cx, cy) in enumerate(chips):
                _remote(ins[i].at[rows], outs[i].at[me, rows], send_sems, recv_sems, 3 * i + j, (cx, cy, c)).wait_send()

    return {"arrays": shards, "out_shape": [SDS((4,) + s.shape, s.dtype) for s in shards], "n_sems": 3 * n, "start": start, "finish": finish}


def _gather_forward(name, gathered):
    n = len(gathered)

    def body(*refs):
        outs = refs[n:2 * n]
        send_sems, recv_sems = refs[2 * n:]
        x, y, c, chips = _position()
        sib = (x, y, 1 - c)
        sends = []
        for i in range(n):
            rh = outs[i].shape[1] // 2
            mine = pl.ds(pl.multiple_of(c * rh, 16), rh)
            for j, (cx, cy) in enumerate(chips):
                rows = outs[i].at[2 * cx + cy, mine]
                cp = _remote(rows, rows, send_sems, recv_sems, 3 * i + j, sib)
                cp.start()
                sends.append(cp)
        for i in range(n):
            rh = outs[i].shape[1] // 2
            theirs = pl.ds(pl.multiple_of((1 - c) * rh, 16), rh)
            for j, (cx, cy) in enumerate(chips):
                rows = outs[i].at[2 * cx + cy, theirs]
                _remote(rows, rows, send_sems, recv_sems, 3 * i + j, sib).wait_recv()
        for cp in sends:
            cp.wait_send()

    return _pcall(body, in_specs=[HBM_SPEC] * n, out_specs=[HBM_SPEC] * n, out_shape=[SDS(g.shape, g.dtype) for g in gathered],
                  input_output_aliases={i: i for i in range(n)}, scratch_shapes=_dma_sems(3 * n), name=name)(*gathered)


def _hosted_scatter(ps):
    n = len(ps)

    def start(ins, outs, send_sems, recv_sems):
        x, y, c, chips = _position()
        me = 2 * x + y
        for i in range(n):
            for j, (cx, cy) in enumerate(chips):
                _remote(ins[i].at[2 * cx + cy], outs[i].at[me], send_sems, recv_sems, 3 * i + j, (cx, cy, c)).start()

    def finish(ins, outs, send_sems, recv_sems):
        x, y, c, chips = _position()
        me = 2 * x + y
        for i in range(n):
            for j, (cx, cy) in enumerate(chips):
                slot = outs[i].at[2 * cx + cy]
                _remote(slot, slot, send_sems, recv_sems, 3 * i + j, (cx, cy, c)).wait_recv()
        for i in range(n):
            for j, (cx, cy) in enumerate(chips):
                _remote(ins[i].at[2 * cx + cy], outs[i].at[me], send_sems, recv_sems, 3 * i + j, (cx, cy, c)).wait_send()

    return {"arrays": ps, "out_shape": [SDS(p.shape, p.dtype) for p in ps], "n_sems": 3 * n, "start": start, "finish": finish}


def _hosted_swap(slots):
    n = len(slots)

    def copies(ins, outs, send_sems, recv_sems):
        x, y, c, _ = _position()
        cps = []
        for i in range(n):
            rh = ins[i].shape[1] // 2
            ohalf = pl.ds(pl.multiple_of((1 - c) * rh, 8), rh)
            cps.append(_remote(ins[i].at[:, ohalf, :], outs[i], send_sems, recv_sems, i, (x, y, 1 - c)))
        return cps

    def start(ins, outs, send_sems, recv_sems):
        for cp in copies(ins, outs, send_sems, recv_sems):
            cp.start()

    def finish(ins, outs, send_sems, recv_sems):
        for cp in copies(ins, outs, send_sems, recv_sems):
            cp.wait()

    return {"arrays": slots, "out_shape": [SDS((4, s.shape[1] // 2, s.shape[2]), s.dtype) for s in slots], "n_sems": n,
            "start": start, "finish": finish}


def _rs_begin(slots, c):
    b1 = _swap_halves_multi("rs_swap", slots)
    return [_pair_add("rs_pair_%d" % i, g, b, c) for i, (g, b) in enumerate(zip(slots, b1))]


def _rs_end(ps, b2, c, chip):
    return _join_halves("rs_join", [_sum_chips("rs_sum_%d" % i, p, b, chip, c) for i, (p, b) in enumerate(zip(ps, b2))])


def _reduce_scatter_multi(slots, c, chip):
    ps = _rs_begin(slots, c)
    return _rs_end(ps, _scatter_chips_multi("rs_scatter", ps), c, chip)


_BIG = ("w_in", "w_dn_out", "w_swa_out", "w_o", "w_up", "w_down")


_W_IN_PIECES = ((0, 3072, 0), (3072, 4096, 3072), (5392, 6416, 4096), (6416, 7440, 5120), (4112, 5136, 6144), (4096, 4112, 7168),
                (5136, 5264, 7296), (5264, 5392, 7424))
_W_IN_SHARD = IN_TOTAL // 4


def _w_in_from_slots(g):
    parts, at = [], 0
    for lo, hi, dst in _W_IN_PIECES:
        if dst > at:
            parts.append(jnp.zeros((g.shape[1], dst - at), g.dtype))
        for s in range(4):
            a, b = max(lo, s * _W_IN_SHARD), min(hi, (s + 1) * _W_IN_SHARD)
            if a < b:
                parts.append(g[s][:, a - s * _W_IN_SHARD:b - s * _W_IN_SHARD])
        at = dst + hi - lo
    parts.append(jnp.zeros((g.shape[1], PROJ_W - at), g.dtype))
    return jnp.concatenate(parts, axis=1)


def _w_in_to_slots(gw):
    slots = []
    for s in range(4):
        parts = []
        for lo, hi, dst in sorted(_W_IN_PIECES):
            a, b = max(lo, s * _W_IN_SHARD), min(hi, (s + 1) * _W_IN_SHARD)
            if a < b:
                parts.append(gw[:, dst + a - lo:dst + b - lo])
        slots.append(jnp.concatenate(parts, axis=1))
    return jnp.stack(slots)


def _assemble_mixer(gs):
    rows = lambda g: g.reshape(4 * g.shape[1], g.shape[2])
    return {"w_in": _w_in_from_slots(gs[0]), "w_dn_out": rows(gs[1]), "w_swa_out": rows(gs[2]), "w_o": rows(gs[3])}


def _assemble_ffn(gs):
    return {"w_up": jnp.concatenate([gs[0][s] for s in range(4)], axis=1), "w_down": gs[1].reshape(4 * gs[1].shape[1], gs[1].shape[2])}


def _grad_slots(gw):
    rows = lambda g: g.reshape(4, g.shape[0] // 4, g.shape[1])
    return [_w_in_to_slots(gw["w_in"]), rows(gw["w_dn_out"]), rows(gw["w_swa_out"]), rows(gw["w_o"]), gw["w_up"], rows(gw["w_down"])]


_SHARDED = (
    ("w_in", (D, 1860), 1, False),
    ("dn_conv", (4, 768), 1, True),
    ("w_dn_out", (256, D), 0, False),
    ("w_swa_out", (256, D), 0, False),
    ("w_o", (256, D), 0, False),
    ("w_up", (D, 1408), 1, False),
    ("ffn_conv", (3, 704), 1, True),
    ("w_down", (704, D), 0, False),
)


def _pack_weights(shards):
    parts = []
    for nm, shp, _, as_bits in _SHARDED:
        a = shards[nm]
        parts.append(lax.bitcast_convert_type(a, bf16).reshape(-1) if as_bits else a.astype(bf16).reshape(-1))
    flat = jnp.concatenate(parts)
    return jnp.pad(flat, (0, COMM_ROWS * COMM_W - flat.shape[0])).reshape(COMM_ROWS, COMM_W)


def _unpack_weights(g):
    flat = g.reshape(4, -1)
    out, off = {}, 0
    for nm, shp, ax, as_bits in _SHARDED:
        n = int(np.prod(shp)) * (2 if as_bits else 1)
        piece = flat[:, off:off + n]
        off += n
        if as_bits:
            piece = lax.bitcast_convert_type(piece.reshape((4,) + shp + (2,)), f32)
        else:
            piece = piece.reshape((4,) + shp)
        out[nm] = jnp.concatenate([piece[s] for s in range(4)], axis=ax)
    return out


def _pack_grads(grads):
    slots = []
    for s in range(4):
        parts = []
        for nm, shp, ax, _ in _SHARDED:
            n = shp[ax]
            parts.append(lax.slice_in_dim(grads[nm], s * n, (s + 1) * n, axis=ax).reshape(-1))
        flat = jnp.concatenate(parts)
        slots.append(jnp.pad(flat, (0, COMM_ROWS * COMM_W - flat.shape[0])))
    return jnp.stack(slots).reshape(4, COMM_ROWS, COMM_W)


def _unpack_grads(r):
    flat = r.reshape(-1)
    out, off = {}, 0
    for nm, shp, _, _ in _SHARDED:
        n = int(np.prod(shp))
        out[nm] = flat[off:off + n].reshape(shp)
        off += n
    return out


def _regroup_w_in(w):
    z = lambda n: jnp.zeros((w.shape[0], n), w.dtype)
    return jnp.concatenate([w[:, 0:3072], w[:, 3072:4096], w[:, 5392:6416], w[:, 6416:7440], w[:, 4112:5136],
                            w[:, 4096:4112], z(112), w[:, 5136:5264], w[:, 5264:5392], z(128)], axis=1)


def _ungroup_w_in(g):
    return jnp.concatenate([g[:, 0:3072], g[:, 3072:4096], g[:, 7168:7184], g[:, 6144:7168], g[:, 7296:7424], g[:, 7424:7552],
                            g[:, 4096:5120], g[:, 5120:6144]], axis=1)


def _pad_lanes(v, n=LANE):
    return jnp.pad(v, (0, n - v.shape[0])).reshape(1, n)


def _layer_consts(P):
    K = {}
    K["norm_mix"] = P["norm_mix"].reshape(1, D)
    K["norm_ffn"] = P["norm_ffn"].reshape(1, D)
    K["alog"] = _pad_lanes(P["dn_a_log"])
    K["dtb"] = _pad_lanes(P["dn_dt_bias"])
    K["dn_norm"] = P["dn_norm"].reshape(1, LANE)
    K["qn"] = jnp.tile(P["swa_q_norm"], 16).reshape(1, D)
    K["kn"] = jnp.tile(P["swa_k_norm"], 2).reshape(1, LANE)
    K["sinks"] = _pad_lanes(P["swa_sinks"])
    K["ffn_b"] = P["ffn_conv_b"].reshape(1, D_FF)
    return K


def _layer_fwd(x, mod, W, K, tabs, bd, hosted=None, late_ffn=None):
    sh1, sc1, gt1, sh2, sc2, gt2 = mod
    S = {"x": x}
    h1, h1t = _rowwise_fwd("normmod1_fwd", _normmod_fn, [(x, 0, D)], [K["norm_mix"], sc1, sh1], [D], [bf16], also_transposed=True)
    proj = _matmul("proj_fwd", h1, W["w_in"], "nn", f32)
    qn = _dnconv_fwd("dnconv_q_fwd", proj, CB_Q, W["dn_conv"], True)
    kn = _dnconv_fwd("dnconv_k_fwd", proj, CB_K, W["dn_conv"], True)
    vc = _dnconv_fwd("dnconv_v_fwd", proj, CB_V, W["dn_conv"], False)
    o, sall, xinv, hosted_out = _gdn_fwd("gdn_fwd", qn, kn, vc, proj, K["alog"], K["dtb"], hosted=hosted)
    (on,) = _rowwise_fwd("dngate_fwd", _dngate_fn, [(o, 0, LANE), (proj, 8 * WB_Z, LANE)], [K["dn_norm"]], [LANE], [bf16], nc=8,
                         tr=_pick(x.shape[0], (1024,)))
    ya = _matmul("dnout_fwd", on, W["w_dn_out"], "nn", f32)
    sq = _qkprep_fwd("qprep_fwd", proj, WB_SWQ, D, K["qn"], bd[0], tabs[0])
    sk = _qkprep_fwd("kprep_fwd", proj, CB_SWK, LANE, K["kn"], bd[1], tabs[1])
    attn, ffn_arrived = _attn_fwd("attn_fwd", sq, sk, proj, K["sinks"], K["attn_bias"], hosted=None if late_ffn is None else late_ffn[0])
    if late_ffn is not None:
        W.update(late_ffn[1](ffn_arrived))
    yb = _matmul("swaout_fwd", attn, W["w_swa_out"], "nn", f32)
    (merged,) = _rowwise_fwd("merge_fwd", _merge_fn, [(proj, WB_GA, D), (proj, WB_GB, D), (ya, 0, D), (yb, 0, D)], [], [D], [bf16])
    t1 = _matmul("wo_fwd", merged, W["w_o"], "nn", f32)
    (x1,) = _rowwise_fwd("resid1_fwd", _resid_fn, [(x, 0, D), (t1, 0, D)], [gt1], [D], [f32])
    h2, h2t = _rowwise_fwd("normmod2_fwd", _normmod_fn, [(x1, 0, D)], [K["norm_ffn"], sc2, sh2], [D], [bf16], also_transposed=True)
    up = _matmul("up_fwd", h2, W["w_up"], "nn", f32)
    mid = _ffnact_fwd("ffnact_fwd", up, W["ffn_conv"], K["ffn_b"])
    t2 = _matmul("down_fwd", mid, W["w_down"], "nn", f32)
    (x2,) = _rowwise_fwd("resid2_fwd", _resid_fn, [(x1, 0, D), (t2, 0, D)], [gt2], [D], [f32])
    S.update(h1t=h1t, h2t=h2t, proj=proj, qn=qn, kn=kn, vc=vc, o=o, sall=sall, xinv=xinv, on=on, ya=ya, sq=sq, sk=sk, attn=attn, yb=yb,
             merged=merged, t1=t1, x1=x1, h2=h2, up=up, mid=mid, t2=t2)
    return x2, S, hosted_out


def _layer_bwd(dx2, S, mod, W, K, tabs, bd, carry=None, early=None):
    sh1, sc1, gt1, sh2, sc2, gt2 = mod
    x, x1, proj, up = S["x"], S["x1"], S["proj"], S["up"]
    T = x.shape[0]
    gw, gs = {}, {}
    dt2, dgt2 = _rowwise_bwd("resid2_bwd", _resid_fn, [(x1, 0, D), (S["t2"], 0, D)], [gt2], [(dx2, 0, D)], [None, bf16])
    dmid = _matmul("down_bwd_x", dt2, W["w_down"], "nt", bf16)
    gw["w_down"] = _matmul("down_bwd_w", S["mid"], dt2, "tn", f32)
    dact, dlin, gw["ffn_conv"], dffn_b = _ffnact_bwd("ffnact_bwd", up, W["ffn_conv"], K["ffn_b"], dmid)
    dup = jnp.concatenate([dact, dlin], axis=1)
    dh2 = _matmul("up_bwd_x", dup, W["w_up"], "nt", f32)
    if carry is None:
        gw["w_up"] = _matmul("up_bwd_w", S["h2t"], dup, "nn", f32, out_slots=4)
        pair_sums = hosted = None
    else:
        gw["w_up"], b1 = _matmul("up_bwd_w", S["h2t"], dup, "nn", f32, out_slots=4, hosted=_hosted_swap(carry[0]))
        pair_sums = [_pair_add("rs_pair_%d" % i, g, b, carry[1]) for i, (g, b) in enumerate(zip(carry[0], b1))]
        hosted = _hosted_scatter(pair_sums)
    dx1, dnorm_ffn, dsc2, dsh2 = _rowwise_bwd("normmod2_bwd", _normmod_fn, [(x1, 0, D)], [K["norm_ffn"], sc2, sh2], [(dh2, 0, D)], [f32],
                                              add_to_first=(dx2, 0, D))
    early_out = None
    if early is not None:
        ffn_slots = [gw["w_up"], gw["w_down"].reshape(4, D_FF // 4, D)]
        ffn_ps = [_pair_add("rs_pair_ffn_%d" % i, g, b, early[0]) for i, (g, b) in enumerate(zip(ffn_slots, _swap_halves_multi("rs_swap_ffn", ffn_slots)))]
    dt1, dgt1 = _rowwise_bwd("resid1_bwd", _resid_fn, [(x, 0, D), (S["t1"], 0, D)], [gt1], [(dx1, 0, D)], [None, bf16])
    dmerged = _matmul("wo_bwd_x", dt1, W["w_o"], "nt", f32)
    gw["w_o"] = _matmul("wo_bwd_w", S["merged"], dt1, "tn", f32)
    dproj, dya, dyb = _rowwise_bwd("merge_bwd", _merge_fn, [(proj, WB_GA, D), (proj, WB_GB, D), (S["ya"], 0, D), (S["yb"], 0, D)], [],
                                   [(dmerged, 0, D)], [bf16, bf16, bf16, bf16], dest=(None, (0, 1), WB_GA // 2))
    don = _matmul("dnout_bwd_x", dya, W["w_dn_out"], "nt", f32)
    gw["w_dn_out"] = _matmul("dnout_bwd_w", S["on"], dya, "tn", f32)
    dproj, do, ddn_norm = _rowwise_bwd("dngate_bwd", _dngate_fn, [(S["o"], 0, LANE), (proj, 8 * WB_Z, LANE)], [K["dn_norm"]], [(don, 0, LANE)],
                                       [f32, bf16], nc=8, tr=_pick(T, (1024,)), dest=(dproj, (1,), 8 * WB_Z))
    (dqn, dkn, dvc, dab, dalog, ddtb), hosted_out = _gdn_bwd("gdn_bwd", S["qn"], S["kn"], S["vc"], proj, K["alog"], K["dtb"], S["sall"], S["xinv"], do,
                                                            hosted=hosted)
    dproj, dwq = _dnconv_bwd("dnconv_q_bwd", proj, CB_Q, W["dn_conv"], dqn, True, dproj)
    dproj, dwk = _dnconv_bwd("dnconv_k_bwd", proj, CB_K, W["dn_conv"], dkn, True, dproj)
    dproj, dwv = _dnconv_bwd("dnconv_v_bwd", proj, CB_V, W["dn_conv"], dvc, False, dproj)
    gw["dn_conv"] = jnp.concatenate([dwq, dwk, dwv], axis=1)
    dattn = _matmul("swaout_bwd_x", dyb, W["w_swa_out"], "nt", f32)
    gw["w_swa_out"] = _matmul("swaout_bwd_w", S["attn"], dyb, "tn", f32)
    (dsq, dkp, dkc, dvp, dvc_, dsinks), ffn_b2 = _attn_bwd("attn_bwd", S["sq"], S["sk"], proj, K["sinks"], K["attn_bias"], dattn,
                                                          hosted=None if early is None else _hosted_scatter(ffn_ps))
    if early is not None:
        early_out = _rs_end(ffn_ps, ffn_b2, early[0], early[1])
    dsk = _shift_add("attn_dk_join", dkc, dkp, f32)
    dswv = _shift_add("attn_dv_join", dvc_, dvp, bf16)
    dproj, dqn_w = _qkprep_bwd("qprep_bwd", proj, WB_SWQ, D, K["qn"], bd[0], tabs[0], dsq, dest_buf=dproj)
    dswk, dkn_w = _qkprep_bwd("kprep_bwd", proj, CB_SWK, LANE, K["kn"], bd[1], tabs[1], dsk)
    tail = jnp.concatenate([dab.astype(bf16), dswk, dswv, jnp.zeros((T, LANE), bf16)], axis=1)
    dproj = lax.dynamic_update_slice(dproj, tail, (0, CB_AB * LANE))
    dh1 = _matmul("proj_bwd_x", dproj, W["w_in"], "nt", f32)
    gw["w_in"] = _matmul("proj_bwd_w", S["h1t"], dproj, "nn", f32)
    dx, dnorm_mix, dsc1, dsh1 = _rowwise_bwd("normmod1_bwd", _normmod_fn, [(x, 0, D)], [K["norm_mix"], sc1, sh1], [(dh1, 0, D)], [f32],
                                             add_to_first=(dx1, 0, D))
    gs = {"norm_mix": dnorm_mix[0], "dn_a_log": dalog[0, :8], "dn_dt_bias": ddtb[0, :8], "dn_norm": ddn_norm[0],
          "swa_q_norm": dqn_w.reshape(16, 64).sum(0), "swa_k_norm": dkn_w.reshape(2, 64).sum(0), "swa_sinks": dsinks[0, :16],
          "norm_ffn": dnorm_ffn[0], "ffn_conv_b": dffn_b[0]}
    dmod = jnp.concatenate([dsh1, dsc1, dgt1, dsh2, dsc2, dgt2], axis=1)
    return dx, gw, gs, dmod, (pair_sums, hosted_out), early_out


def _rope_tables(pos):
    T = pos.shape[0]
    half = 8
    inv = jnp.power(ROPE_THETA, -jnp.arange(half, dtype=f32) / half)
    ang = pos.astype(f32)[:, None] * inv
    cos, sin = jnp.cos(ang), jnp.sin(ang)
    z8, z48, o48 = jnp.zeros((T, 8), f32), jnp.zeros((T, 48), f32), jnp.ones((T, 48), f32)
    c64 = jnp.concatenate([cos, cos, o48], axis=1)
    s1 = jnp.concatenate([-sin, z8, z48], axis=1)
    s2 = jnp.concatenate([z8, sin, z48], axis=1)
    return tuple(jnp.tile(t, (1, 2)) for t in (c64, s1, s2))


_SMALL = (("norm_mix", D), ("dn_a_log", 8), ("dn_dt_bias", 8), ("dn_norm", 128), ("swa_q_norm", 64), ("swa_k_norm", 64),
          ("swa_sinks", 16), ("norm_ffn", D), ("ffn_conv_b", D_FF), ("b_ada", 6 * D))
_CONV = (("dn_conv", 4 * 3072), ("ffn_conv", 3 * D_FF))
_CONV_SHARD = (("dn_conv", 4 * 768), ("ffn_conv", 3 * 704))


def _pack_small(vals, spec):
    flat = jnp.concatenate([vals[nm].reshape(-1) for nm, _ in spec])
    rows = -(-flat.shape[0] // (8 * LANE)) * 8
    return jnp.pad(flat, (0, rows * LANE - flat.shape[0])).reshape(rows, LANE)


def _unpack_small(buf, spec):
    flat = buf.reshape(-1)
    out, off = {}, 0
    for nm, n in spec:
        out[nm] = flat[off:off + DEPTH * n].reshape(DEPTH, n)
        off += DEPTH * n
    return out


def kernel(x, c, positions, w_ada, b_ada, norm_mix, w_in, dn_conv, dn_a_log, dn_dt_bias, dn_norm, w_dn_out, swa_q_norm, swa_k_norm, swa_sinks, w_swa_out, w_o, norm_ffn, w_up, ffn_conv, ffn_conv_b, w_down, loss_target, m_w_ada, m_b_ada, m_norm_mix, m_w_in, m_dn_conv, m_dn_a_log, m_dn_dt_bias, m_dn_norm, m_w_dn_out, m_swa_q_norm, m_swa_k_norm, m_swa_sinks, m_w_swa_out, m_w_o, m_norm_ffn, m_w_up, m_ffn_conv, m_ffn_conv_b, m_w_down, v_w_ada, v_b_ada, v_norm_mix, v_w_in, v_dn_conv, v_dn_a_log, v_dn_dt_bias, v_dn_norm, v_w_dn_out, v_swa_q_norm, v_swa_k_norm, v_swa_sinks, v_w_swa_out, v_w_o, v_norm_ffn, v_w_up, v_ffn_conv, v_ffn_conv_b, v_w_down):
    weights = dict(w_ada=w_ada, b_ada=b_ada, norm_mix=norm_mix, w_in=w_in, dn_conv=dn_conv, dn_a_log=dn_a_log, dn_dt_bias=dn_dt_bias,
                   dn_norm=dn_norm, w_dn_out=w_dn_out, swa_q_norm=swa_q_norm, swa_k_norm=swa_k_norm, swa_sinks=swa_sinks,
                   w_swa_out=w_swa_out, w_o=w_o, norm_ffn=norm_ffn, w_up=w_up, ffn_conv=ffn_conv, ffn_conv_b=ffn_conv_b, w_down=w_down)
    mom_m = dict(w_ada=m_w_ada, b_ada=m_b_ada, norm_mix=m_norm_mix, w_in=m_w_in, dn_conv=m_dn_conv, dn_a_log=m_dn_a_log,
                 dn_dt_bias=m_dn_dt_bias, dn_norm=m_dn_norm, w_dn_out=m_w_dn_out, swa_q_norm=m_swa_q_norm, swa_k_norm=m_swa_k_norm,
                 swa_sinks=m_swa_sinks, w_swa_out=m_w_swa_out, w_o=m_w_o, norm_ffn=m_norm_ffn, w_up=m_w_up, ffn_conv=m_ffn_conv,
                 ffn_conv_b=m_ffn_conv_b, w_down=m_w_down)
    mom_v = dict(w_ada=v_w_ada, b_ada=v_b_ada, norm_mix=v_norm_mix, w_in=v_w_in, dn_conv=v_dn_conv, dn_a_log=v_dn_a_log,
                 dn_dt_bias=v_dn_dt_bias, dn_norm=v_dn_norm, w_dn_out=v_w_dn_out, swa_q_norm=v_swa_q_norm, swa_k_norm=v_swa_k_norm,
                 swa_sinks=v_swa_sinks, w_swa_out=v_w_swa_out, w_o=v_w_o, norm_ffn=v_norm_ffn, w_up=v_w_up, ffn_conv=v_ffn_conv,
                 ffn_conv_b=v_ffn_conv_b, w_down=v_w_down)
    order = ["w_ada", "b_ada", "norm_mix", "w_in", "dn_conv", "dn_a_log", "dn_dt_bias", "dn_norm", "w_dn_out", "swa_q_norm",
             "swa_k_norm", "swa_sinks", "w_swa_out", "w_o", "norm_ffn", "w_up", "ffn_conv", "ffn_conv_b", "w_down"]
    ax, ay, ac = lax.axis_index("x"), lax.axis_index("y"), lax.axis_index("c")
    chip = 2 * ax + ay
    dev = 4 * ax + 2 * ay + ac
    T = x.shape[1]
    xs = x[0]

    c_all = _allgather_all("gather_c", jnp.pad(c, ((0, 7), (0, 0)))).reshape(8, 8, D)[:, 0]
    c_act = _silu_rows("silu_c", jnp.pad(c_all, ((0, 8), (0, 0))))
    mod_sh = jnp.stack([
        _matmul("mod_fwd", c_act, w_ada[l].astype(bf16), "nn", f32,
                bias=lax.dynamic_slice(b_ada[l], (chip * 1536,), (1536,)).reshape(1, 1536)) for l in range(DEPTH)])
    mod_all = _allgather_all("gather_mod", mod_sh.reshape(DEPTH * 16 * 12, LANE)).reshape(8, DEPTH, 16, 1536)
    mod_me = jnp.concatenate([lax.dynamic_index_in_dim(mod_all[2 * s], dev, axis=1, keepdims=False) for s in range(4)], axis=1)

    tabs_q = _rope_tables(positions[0])
    tabs = (tabs_q, tabs_q)
    head = jnp.arange(LANE) // 64
    bd128 = (head[:, None] == head[None, :]).astype(f32) / 64.0
    bd = (bd128, bd128)
    attn_bias = _attn_bias()

    conv_all = _allgather_all("gather_conv", _pack_small({"dn_conv": dn_conv, "ffn_conv": ffn_conv}, _CONV_SHARD))
    conv_parts = [_unpack_small(conv_all[2 * s], _CONV_SHARD) for s in range(4)]
    dn_conv_full = jnp.concatenate([p["dn_conv"].reshape(DEPTH, 4, 768) for p in conv_parts], axis=2)
    ffn_conv_full = jnp.concatenate([p["ffn_conv"].reshape(DEPTH, 3, 704) for p in conv_parts], axis=2)

    saved, Ws, Ks, mods = [], [], [], []
    h = xs
    shards = [[weights[nm][l].astype(bf16) for nm in _BIG] for l in range(DEPTH)]
    own = lambda gs, ss: [lax.dynamic_update_index_in_dim(g, s, chip, 0) for g, s in zip(gs, ss)]
    gathered = _gather_chips("gather_w", shards[0][:4])
    for l in range(DEPTH):
        gathered = own(gathered, shards[l])
        W = _assemble_mixer(gathered[:4])
        late_ffn = None
        if l == 0:
            late_ffn = (_hosted_gather(shards[0][4:]), lambda arrived: _assemble_ffn(own(_gather_forward("gather_ffn_pass", arrived), shards[0][4:])))
        else:
            W.update(_assemble_ffn(gathered[4:]))
        W["dn_conv"], W["ffn_conv"] = dn_conv_full[l], ffn_conv_full[l]
        K = _layer_consts({nm: weights[nm][l] for nm in ("norm_mix", "norm_ffn", "dn_a_log", "dn_dt_bias", "dn_norm", "swa_q_norm",
                                                          "swa_k_norm", "swa_sinks", "ffn_conv_b")})
        K["attn_bias"] = attn_bias
        mod = tuple(mod_me[l, k * D:(k + 1) * D].reshape(1, D) for k in range(6))
        nxt = _hosted_gather(shards[l + 1]) if l + 1 < DEPTH else None
        h, S, arrived = _layer_fwd(h, mod, W, K, tabs, bd, hosted=nxt, late_ffn=late_ffn)
        if nxt is not None:
            gathered = _gather_forward("gather_w_pass", arrived)
        saved.append(S), Ws.append(W), Ks.append(K), mods.append(mod)

    loss_blk, dh = _loss("loss", h, loss_target[0])
    loss = lax.psum(loss_blk[0, 0], ("x", "y", "c"))

    grad_sh = [None] * DEPTH
    small = [None] * DEPTH
    dmods = [None] * DEPTH
    slots = None
    for l in reversed(range(DEPTH)):
        dh, gw, gs, dmod, (ps, b2), ffn0 = _layer_bwd(dh, saved[l], mods[l], Ws[l], Ks[l], tabs, bd, carry=None if slots is None else (slots, ac),
                                                      early=(ac, chip) if l == 0 else None)
        if slots is not None:
            grad_sh[l + 1] = dict(zip(_BIG, _rs_end(ps, b2, ac, chip)))
        slots = _grad_slots(gw)
        small[l], dmods[l] = dict(gs, dn_conv=gw["dn_conv"], ffn_conv=gw["ffn_conv"]), dmod[0]
    grad_sh[0] = dict(zip(_BIG, _reduce_scatter_multi(slots[:4], ac, chip) + ffn0))

    spec_g = _SMALL + _CONV
    vals = {nm: jnp.stack([small[l][nm] for l in range(DEPTH)]) for nm, _ in spec_g if nm != "b_ada"}
    vals["b_ada"] = jnp.stack(dmods)
    small_all = _allgather_all("gather_small", _pack_small(vals, spec_g))
    g_small = _unpack_small(_sum_leading("sum_small", small_all), spec_g)
    dmod_all = jnp.stack([_unpack_small(small_all[d], spec_g)["b_ada"] for d in range(8)])
    dmod_sh = lax.dynamic_slice(dmod_all, (0, 0, chip * 1536), (8, DEPTH, 1536))
    dmod_sh = jnp.pad(dmod_sh, ((0, 8), (0, 0), (0, 0))).astype(bf16)
    g_w_ada = jnp.stack([_matmul("mod_bwd_w", c_act, dmod_sh[:, l], "tn", f32) for l in range(DEPTH)])

    grads = {nm: g_small[nm] for nm, _ in _SMALL}
    grads["dn_conv"] = lax.dynamic_slice(g_small["dn_conv"].reshape(DEPTH, 4, 3072), (0, 0, chip * 768), (DEPTH, 4, 768))
    grads["ffn_conv"] = lax.dynamic_slice(g_small["ffn_conv"].reshape(DEPTH, 3, D_FF), (0, 0, chip * 704), (DEPTH, 3, 704))
    grads["w_ada"] = g_w_ada
    for nm in _BIG:
        grads[nm] = jnp.stack([grad_sh[l][nm] for l in range(DEPTH)])

    delta, new_m, new_v = {}, {}, {}
    for nm in ("w_ada", "dn_conv", "ffn_conv") + _BIG:
        delta[nm], new_m[nm], new_v[nm] = _adamw("adamw_" + nm, weights[nm], grads[nm], mom_m[nm], mom_v[nm])
    sm = [_pack_small({nm: t[nm] for nm, _ in _SMALL}, _SMALL) for t in (weights, grads, mom_m, mom_v)]
    for tgt, buf in zip((delta, new_m, new_v), _adamw("adamw_small", *sm)):
        tgt.update(_unpack_small(buf, _SMALL))

    return (loss, dh[None], *[grads[n] for n in order], *[delta[n] for n in order], *[new_m[n] for n in order], *[new_v[n] for n in order])
```

```python
import functools

import jax
import jax.numpy as jnp
import numpy as np
from jax import lax
from jax.experimental import pallas as pl
from jax.experimental.pallas import tpu as pltpu

f32 = jnp.float32
bf16 = jnp.bfloat16
SDS = jax.ShapeDtypeStruct
HI = lax.Precision.HIGHEST
MESH = pl.DeviceIdType.MESH

D = 1024
DEPTH = 4
EPS = 1e-6
DN_C = 64
SWA_B = 128
LANE = 128
ROPE_THETA = 500000.0
D_FF = 2816
IN_TOTAL = 7440
PROJ_W = 7680
CB_Q, CB_K, CB_V = 0, 8, 16
CB_AB, CB_SWK, CB_SWV = 56, 57, 58
WB_Z, WB_GA, WB_GB, WB_SWQ = 3, 4, 5, 6
TR = 256
COMM_W = 1024
COMM_ROWS = 4864
COMM_TR = 128
VMEM_BIG = 48 * 2 ** 20

ADAM_LR, ADAM_B1, ADAM_B2, ADAM_EPS, ADAM_WD, ADAM_STEP = 0.001, 0.9, 0.999, 1e-08, 0.01, 10


def _pcall(body, **kw):
    return pl.pallas_call(body, **kw)


def _cparams(vmem=None):
    return pltpu.CompilerParams(vmem_limit_bytes=vmem) if vmem else None


def _dot(a, b, ca, cb, precision=HI):
    return lax.dot_general(a, b, (((ca,), (cb,)), ((), ())), precision=precision, preferred_element_type=f32)


def _pick(n, cands):
    for c in cands:
        if n % c == 0:
            return c
    return n


def _tile(n, cap):
    if n <= cap:
        return n
    best = None
    for t in range(LANE, cap + 1, LANE):
        if n % t == 0:
            best = t
    assert best is not None, (n, cap)
    return best


def _matmul(name, a, b, mode, out_dtype, bias=None, out_slots=None, hosted=None, resid=None):
    h_args, h_specs, h_shapes, h_sems = _hosted_parts(hosted)
    nh = len(h_args)
    a_parts = list(a) if isinstance(a, (tuple, list)) else [a]
    b_parts = list(b) if isinstance(b, (tuple, list)) else [b]
    na, nbp = len(a_parts), len(b_parts)
    assert (na == 1 or mode == "nt") and (nbp == 1 or mode == "nn")
    if mode == "nn":
        M, K = a_parts[0].shape
        N = b_parts[0].shape[1] * nbp
    elif mode == "nt":
        M, K = a_parts[0].shape[0], a_parts[0].shape[1] * na
        N = b_parts[0].shape[0]
    else:
        (K, M), N = a_parts[0].shape, b_parts[0].shape[1]
    tm = _tile(M, 1536 if mode == "tn" else 1024)
    tn = N // out_slots if out_slots else _tile(N, 1536)
    tk = _tile(K, 512 if mode == "tn" else (1024 if K <= 1024 else 1536))
    nk, nj = K // tk, N // tn
    ka, jb = nk // na, nj // nbp
    assert nk % na == 0 and nj % nbp == 0
    ca, cb = {"nn": (1, 0), "nt": (1, 1), "tn": (0, 0)}[mode]
    grid = (M // tm, nj, nk)
    n_bias, n_res = (1 if bias is not None else 0), (2 if resid is not None else 0)

    def body(*refs):
        a_refs, b_refs = refs[:na], refs[na:na + nbp]
        p = na + nbp
        bias_ref = refs[p] if n_bias else None
        res_refs = refs[p + n_bias:p + n_bias + n_res]
        p += n_bias + n_res
        h_ins, o_ref = refs[p:p + nh], refs[p + nh]
        xo_ref = refs[p + nh + 1] if n_res else None
        p += nh + 1 + (1 if n_res else 0)
        h_outs, rest = refs[p:p + nh], refs[p + nh:]
        acc, sems = (rest[0], rest[1:]) if nk > 1 else (None, rest)
        j, k = pl.program_id(1), pl.program_id(2)
        step = (pl.program_id(0) * grid[1] + j) * grid[2] + k
        _hosted_edge(hosted, "start", h_ins, h_outs, sems, step == 0)

        def finish(r):
            if bias is not None:
                r = r + bias_ref[...]
            o_ref[...] = r.astype(o_ref.dtype)
            if n_res:
                xo_ref[...] = res_refs[0][...] + res_refs[1][...] * r

        a_tile = a_refs[0][...] if na == 1 else jnp.where(k < ka, a_refs[0][...], a_refs[1][...])
        b_tile = b_refs[0][...] if nbp == 1 else jnp.where(j < jb, b_refs[0][...], b_refs[1][...])
        part = _dot(a_tile.astype(bf16), b_tile.astype(bf16), ca, cb, precision=None)
        if nk == 1:
            finish(part)
        else:
            @pl.when(k == 0)
            def _():
                acc[...] = part

            @pl.when(k > 0)
            def _():
                acc[...] += part

            @pl.when(k == nk - 1)
            def _():
                finish(acc[...])

        _hosted_edge(hosted, "finish", h_ins, h_outs, sems, step == grid[0] * grid[1] * grid[2] - 1)

    if mode == "tn":
        a_specs = [pl.BlockSpec((tk, tm), lambda i, j, k: (k, i))]
    else:
        a_specs = [pl.BlockSpec((tm, tk), lambda i, j, k, q=q: (i, jnp.clip(k - q * ka, 0, ka - 1))) for q in range(na)]
    if mode == "nt":
        b_specs = [pl.BlockSpec((tn, tk), lambda i, j, k: (j, k))]
    else:
        b_specs = [pl.BlockSpec((tk, tn), lambda i, j, k, q=q: (k, jnp.clip(j - q * jb, 0, jb - 1))) for q in range(nbp)]
    in_specs, args = a_specs + b_specs, a_parts + b_parts
    if bias is not None:
        in_specs.append(pl.BlockSpec((1, tn), lambda i, j, k: (0, j)))
        args.append(bias)
    if out_slots:
        out_specs, out_shape = [pl.BlockSpec((None, tm, tn), lambda i, j, k: (j, i, 0))], [SDS((out_slots, M, tn), out_dtype)]
    else:
        out_specs, out_shape = [pl.BlockSpec((tm, tn), lambda i, j, k: (i, j))], [SDS((M, N), out_dtype)]
    if resid is not None:
        in_specs += [pl.BlockSpec((tm, tn), lambda i, j, k: (i, j)), pl.BlockSpec((1, tn), lambda i, j, k: (0, j))]
        args += list(resid)
        out_specs.append(pl.BlockSpec((tm, tn), lambda i, j, k: (i, j)))
        out_shape.append(SDS((M, N), f32))
    outs = _pcall(
        body, grid=grid, in_specs=in_specs + h_specs, out_specs=out_specs + h_specs, out_shape=out_shape + h_shapes,
        scratch_shapes=([pltpu.VMEM((tm, tn), f32)] if nk > 1 else []) + h_sems, compiler_params=_cparams(VMEM_BIG), name=name)(*args, *h_args)
    n_main = len(out_shape)
    res = list(outs[:n_main]) + ([list(outs[n_main:])] if hosted is not None else [])
    return res[0] if len(res) == 1 else tuple(res)


def _row_specs(rows, tr):
    return [pl.BlockSpec((tr, w), lambda i, j, off=off: (i, off + j)) for (_, off, w) in rows]


def _rowwise_fwd(name, fn, rows, vecs, out_widths, out_dtypes, nc=1, tr=TR, also_transposed=False):
    T = rows[0][0].shape[0]
    n_in = len(rows) + len(vecs)
    n_out = len(out_widths)

    def body(*refs):
        vals = [r[...].astype(f32) for r in refs[:n_in]]
        res = fn(*vals)
        for o_ref, r in zip(refs[n_in:n_in + n_out], res):
            o_ref[...] = r.astype(o_ref.dtype)
        if also_transposed:
            refs[n_in + n_out][...] = res[0].T.astype(refs[n_in + n_out].dtype)

    in_specs = _row_specs(rows, tr) + [pl.BlockSpec(v.shape, lambda i, j: (0, 0)) for v in vecs]
    out_specs = [pl.BlockSpec((tr, w), lambda i, j: (i, j)) for w in out_widths]
    out_shape = [SDS((T, w * nc), dt) for w, dt in zip(out_widths, out_dtypes)]
    if also_transposed:
        out_specs.append(pl.BlockSpec((out_widths[0], tr), lambda i, j: (j, i)))
        out_shape.append(SDS((out_widths[0] * nc, T), out_dtypes[0]))
    return _pcall(body, grid=(T // tr, nc), in_specs=in_specs, out_specs=out_specs, out_shape=out_shape, name=name)(
        *[r[0] for r in rows], *vecs)


def _into_buffer(dest_buf, n_inputs, out_index):
    if dest_buf is None:
        return [], [], {}
    return [dest_buf], [pl.BlockSpec(memory_space=pl.ANY)], {n_inputs: out_index}


def _rowwise_bwd(name, fn, rows, vecs, cts, drow_dtypes, nc=1, tr=TR, add_to_first=None, dest=None):
    T = rows[0][0].shape[0]
    n_r, n_v, n_c = len(rows), len(vecs), len(cts)
    n_add = 0 if add_to_first is None else 1
    keep = [k for k, dt in enumerate(drow_dtypes) if dt is not None]
    members = [] if dest is None else list(dest[1])
    plain = [pos for pos in range(len(keep)) if pos not in members]
    n_dest = 1 if dest is not None else 0
    n_in = n_r + n_v + n_c + n_add

    def body(*refs):
        vals = [r[...].astype(f32) for r in refs[:n_in]]
        outs = refs[n_in + (1 if dest is not None and dest[0] is not None else 0):]
        i, j = pl.program_id(0), pl.program_id(1)
        _, vjp = jax.vjp(fn, *vals[:n_r + n_v])
        grads = vjp(tuple(vals[n_r + n_v:n_r + n_v + n_c]))
        got = []
        for pos, k in enumerate(keep):
            g = grads[k]
            if n_add and pos == 0:
                g = g + vals[n_in - 1]
            got.append(g)
        if dest is not None:
            outs[0][...] = jnp.concatenate([got[pos] for pos in members], axis=1).astype(outs[0].dtype)
        for q, pos in enumerate(plain):
            outs[n_dest + q][...] = got[pos].astype(outs[n_dest + q].dtype)
        vec_outs = outs[n_dest + len(plain):]

        @pl.when((i == 0) & (j == 0))
        def _():
            for q in range(n_v):
                vec_outs[q][...] = jnp.zeros_like(vec_outs[q])

        for q in range(n_v):
            vec_outs[q][...] += grads[n_r + q]

    extra = [] if add_to_first is None else [add_to_first]
    in_specs = (_row_specs(rows, tr) + [pl.BlockSpec(v.shape, lambda i, j: (0, 0)) for v in vecs]
                + _row_specs(cts, tr) + _row_specs(extra, tr))
    args = [r[0] for r in rows] + list(vecs) + [c[0] for c in cts] + [e[0] for e in extra]
    out_specs, out_shape, aliases = [], [], {}
    if dest is not None:
        width = sum(rows[keep[pos]][2] for pos in members)
        col = dest[2]
        b_args, b_specs, aliases = _into_buffer(dest[0], n_in, 0)
        args, in_specs = args + b_args, in_specs + b_specs
        out_specs.append(pl.BlockSpec((tr, width), lambda i, j: (i, col + j)))
        out_shape.append(SDS((T, PROJ_W), bf16))
    out_specs += [pl.BlockSpec((tr, rows[keep[pos]][2]), lambda i, j: (i, j)) for pos in plain]
    out_shape += [SDS((T, rows[keep[pos]][2] * nc), drow_dtypes[keep[pos]]) for pos in plain]
    out_specs += [pl.BlockSpec(v.shape, lambda i, j: (0, 0)) for v in vecs]
    out_shape += [SDS(v.shape, f32) for v in vecs]
    return _pcall(body, grid=(T // tr, nc), in_specs=in_specs, out_specs=out_specs, out_shape=out_shape,
                  input_output_aliases=aliases, name=name)(*args)


def _normmod_fn(x, w, sc, sh):
    y = x * lax.rsqrt(jnp.mean(x * x, axis=-1, keepdims=True) + EPS)
    return ((y * w) * (1.0 + sc) + sh,)


def _resid_fn(x, t, gt):
    return (x + gt * t,)


def _merge_fn(ga, gb, ya, yb):
    return (jax.nn.sigmoid(ga) * ya + jax.nn.sigmoid(gb) * yb,)


def _dngate_fn(o, z, w):
    y = o * lax.rsqrt(jnp.mean(o * o, axis=-1, keepdims=True) + EPS)
    return ((y * w) * (z * jax.nn.sigmoid(z)),)


def _conv_taps(x, w_ref, taps, buf):
    w = lambda s: w_ref[taps - 1 - s:taps - s, :]
    row = lax.broadcasted_iota(jnp.int32, (8, x.shape[1]), 0)
    x8 = x[0:8]
    acc, acc8 = x * w(0), x8 * w(0)
    for s in range(1, taps):
        acc = acc + pltpu.roll(x, s, 0) * w(s)
        acc8 = acc8 + jnp.where(row >= s, pltpu.roll(x8, s, 0), 0.0) * w(s)
    buf[...] = acc
    buf[0:8, :] = acc8
    return buf[...]


def _conv_taps_bwd(x, dy, w_ref, dw_ref, taps, buf):
    T = x.shape[0]
    w = lambda s: w_ref[taps - 1 - s:taps - s, :]
    row = lax.broadcasted_iota(jnp.int32, (8, x.shape[1]), 0)
    dy_first, dy_last = dy[0:8], dy[T - 8:T]
    dx, dx_last = dy * w(0), dy_last * w(0)
    dw_ref[taps - 1:taps, :] = jnp.sum(dy * x, axis=0, keepdims=True)
    for s in range(1, taps):
        dx = dx + pltpu.roll(dy, T - s, 0) * w(s)
        dx_last = dx_last + jnp.where(row < 8 - s, pltpu.roll(dy_last, 8 - s, 0), 0.0) * w(s)
        xr = pltpu.roll(x, s, 0)
        wrapped = jnp.sum(jnp.where(row < s, dy_first * xr[0:8], 0.0), axis=0, keepdims=True)
        dw_ref[taps - 1 - s:taps - s, :] = jnp.sum(dy * xr, axis=0, keepdims=True) - wrapped
    buf[...] = dx
    buf[T - 8:T, :] = dx_last
    return buf[...]


def _dn_act(y, normalize):
    s = y * jax.nn.sigmoid(y)
    if normalize:
        s = s * lax.rsqrt(jnp.sum(s * s, axis=-1, keepdims=True) + EPS)
    return s


def _dnconv_fwd(name, proj, cb, w, normalize):
    T = proj.shape[0]

    def body(x_ref, w_ref, o_ref, buf):
        o_ref[...] = _dn_act(_conv_taps(x_ref[...], w_ref, 4, buf), normalize)

    return _pcall(
        body, grid=(8,), in_specs=[pl.BlockSpec((T, LANE), lambda j: (0, cb + j)), pl.BlockSpec((4, LANE), lambda j: (0, cb + j))],
        out_specs=pl.BlockSpec((T, LANE), lambda j: (0, j)), out_shape=SDS((T, 1024), f32), scratch_shapes=[pltpu.VMEM((T, LANE), f32)],
        compiler_params=_cparams(VMEM_BIG), name=name)(proj, w)


def _dnconv_bwd(name, proj, cb, w, dout, normalize, dest_buf):
    T = proj.shape[0]
    b_args, b_specs, aliases = _into_buffer(dest_buf, 3, 0)

    def body(x_ref, w_ref, do_ref, *rest):
        dx_ref, dw_ref, buf, buf2 = rest[len(b_args):]
        x = x_ref[...]
        y = _conv_taps(x, w_ref, 4, buf)
        _, vjp = jax.vjp(functools.partial(_dn_act, normalize=normalize), y)
        (dy,) = vjp(do_ref[...])
        dx_ref[...] = _conv_taps_bwd(x, dy, w_ref, dw_ref, 4, buf2).astype(dx_ref.dtype)

    return _pcall(
        body, grid=(8,),
        in_specs=[pl.BlockSpec((T, LANE), lambda j: (0, cb + j)), pl.BlockSpec((4, LANE), lambda j: (0, cb + j)),
                  pl.BlockSpec((T, LANE), lambda j: (0, j))] + b_specs,
        out_specs=[pl.BlockSpec((T, LANE), lambda j: (0, cb + j)), pl.BlockSpec((4, LANE), lambda j: (0, j))],
        out_shape=[SDS((T, PROJ_W), bf16), SDS((4, 1024), f32)], scratch_shapes=[pltpu.VMEM((T, LANE), f32)] * 2,
        input_output_aliases=aliases, compiler_params=_cparams(VMEM_BIG), name=name)(proj, w, dout, *b_args)


def _ffn_point(a, lin):
    return a * jax.nn.sigmoid(a) * lin


def _ffnact_fwd(name, up, w, b):
    T = up.shape[0]
    nblk = D_FF // LANE

    def body(a_ref, l_ref, w_ref, b_ref, o_ref, buf):
        a = _conv_taps(a_ref[...], w_ref, 3, buf) + b_ref[...]
        o_ref[...] = _ffn_point(a, l_ref[...]).astype(o_ref.dtype)

    return _pcall(
        body, grid=(nblk,),
        in_specs=[pl.BlockSpec((T, LANE), lambda j: (0, j)), pl.BlockSpec((T, LANE), lambda j: (0, nblk + j)),
                  pl.BlockSpec((3, LANE), lambda j: (0, j)), pl.BlockSpec((1, LANE), lambda j: (0, j))],
        out_specs=pl.BlockSpec((T, LANE), lambda j: (0, j)), out_shape=SDS((T, D_FF), bf16), scratch_shapes=[pltpu.VMEM((T, LANE), f32)],
        compiler_params=_cparams(VMEM_BIG), name=name)(up, up, w, b)


def _ffnact_bwd(name, up, w, b, dmid):
    T = up.shape[0]
    nblk = D_FF // LANE

    def body(a_ref, l_ref, w_ref, b_ref, dm_ref, da_ref, dl_ref, dw_ref, db_ref, buf, buf2):
        x = a_ref[...]
        a = _conv_taps(x, w_ref, 3, buf) + b_ref[...]
        _, vjp = jax.vjp(_ffn_point, a, l_ref[...])
        da, dl = vjp(dm_ref[...].astype(f32))
        dl_ref[...] = dl.astype(dl_ref.dtype)
        db_ref[...] = jnp.sum(da, axis=0, keepdims=True)
        da_ref[...] = _conv_taps_bwd(x, da, w_ref, dw_ref, 3, buf2).astype(da_ref.dtype)

    col = lambda r: pl.BlockSpec((r, LANE), lambda j: (0, j))
    return _pcall(
        body, grid=(nblk,),
        in_specs=[col(T), pl.BlockSpec((T, LANE), lambda j: (0, nblk + j)), col(3), col(1), col(T)],
        out_specs=[col(T), col(T), col(3), col(1)],
        out_shape=[SDS((T, D_FF), bf16), SDS((T, D_FF), bf16), SDS((3, D_FF), f32), SDS((1, D_FF), f32)],
        scratch_shapes=[pltpu.VMEM((T, LANE), f32)] * 2, compiler_params=_cparams(VMEM_BIG), name=name)(up, up, w, b, dmid)


def _bmm(a, b, ca, cb, precision=HI):
    return lax.dot_general(a, b, (((ca,), (cb,)), ((0,), (0,))), precision=precision, preferred_element_type=f32)


def _make_bdot(ca, cb):
    def raw(x, y, cx, cy):
        return _bmm(x.astype(bf16), y.astype(bf16), cx, cy, precision=None)

    @jax.custom_vjp
    def f(a, b):
        return raw(a, b, ca, cb)

    def fwd(a, b):
        return raw(a, b, ca, cb), (a, b)

    def bwd(res, dy):
        a, b = res
        if (ca, cb) == (2, 1):
            return raw(dy, b, 2, 2), raw(a, dy, 1, 1)
        if (ca, cb) == (2, 2):
            return raw(dy, b, 2, 1), raw(dy, a, 1, 1)
        return raw(b, dy, 2, 2), raw(a, dy, 2, 1)

    f.defvjp(fwd, bwd)
    return f


_bdot_nn, _bdot_nt, _bdot_tn = _make_bdot(2, 1), _make_bdot(2, 2), _make_bdot(1, 1)


def _pieces(a, n):
    out, r = [], a
    for _ in range(n):
        p = r.astype(bf16)
        out.append(p)
        r = r - p.astype(f32)
    return out


def _bmm_split(x, y, cx, cy, nx=2, ny=2, order=1):
    xs, ys = _pieces(x, nx), _pieces(y, ny)
    acc = None
    for i in reversed(range(nx)):
        for j in reversed(range(ny)):
            if i + j <= order:
                t = _bmm(xs[i], ys[j], cx, cy, precision=None)
                acc = t if acc is None else acc + t
    return acc


@jax.custom_vjp
def _solve_apply(X, r):
    return _bmm_split(X, r, 2, 1)


def _solve_apply_fwd(X, r):
    return _bmm_split(X, r, 2, 1), (X, r)


def _solve_apply_bwd(res, dy):
    X, r = res
    return _bmm_split(dy, r, 2, 2), _bmm_split(X, dy, 1, 1)


_solve_apply.defvjp(_solve_apply_fwd, _solve_apply_bwd)


def _lower_ones(H, C):
    ri = lax.broadcasted_iota(jnp.int32, (H, C, C), 1)
    ci = lax.broadcasted_iota(jnp.int32, (H, C, C), 2)
    return (ri >= ci).astype(f32)


def _cumsum_rows_raw(G):
    return _bmm_split(_lower_ones(G.shape[0], G.shape[1]), G, 2, 1, nx=1, ny=3, order=2)


@jax.custom_vjp
def _cumsum_rows(G):
    return _cumsum_rows_raw(G)


def _cumsum_rows_fwd(G):
    return _cumsum_rows_raw(G), None


def _cumsum_rows_bwd(_, dy):
    return (_bmm_split(_lower_ones(dy.shape[0], dy.shape[1]), dy, 1, 1, nx=1, ny=3, order=2),)


_cumsum_rows.defvjp(_cumsum_rows_fwd, _cumsum_rows_bwd)


def _tri_inverse_raw(L):
    H, C, _ = L.shape
    ri = lax.broadcasted_iota(jnp.int32, (C, C), 0)
    ci = lax.broadcasted_iota(jnp.int32, (C, C), 1)
    eye = jnp.broadcast_to((ri == ci).astype(f32)[None], (H, C, C))
    Dg = jnp.where(((ri >> 3) == (ci >> 3))[None], L, 0.0)
    D2 = _bmm_split(Dg, Dg, 2, 1)
    X = _bmm_split(_bmm_split(eye - Dg, eye + D2, 2, 1), eye + _bmm_split(D2, D2, 2, 1), 2, 1)
    for lg in range(3, C.bit_length() - 1):
        same = (ri >> (lg + 1)) == (ci >> (lg + 1))
        lower_left = same & (((ri >> lg) & 1) == 1) & (((ci >> lg) & 1) == 0)
        X = X - _bmm_split(_bmm_split(X, jnp.where(lower_left[None], L, 0.0), 2, 1), X, 2, 1)
    return X


@jax.custom_vjp
def _tri_inverse(L):
    return _tri_inverse_raw(L)


def _tri_inverse_fwd(L):
    X = _tri_inverse_raw(L)
    return X, X


def _tri_inverse_bwd(X, dX):
    return (-_bmm_split(_bmm_split(X, dX, 1, 1), X, 2, 2),)


_tri_inverse.defvjp(_tri_inverse_fwd, _tri_inverse_bwd)


@jax.custom_vjp
def _tri_inverse_known(L, X):
    return X


def _tri_inverse_known_fwd(L, X):
    return X, X


def _tri_inverse_known_bwd(X, dX):
    return _tri_inverse_bwd(X, dX)[0], jnp.zeros_like(X)


_tri_inverse_known.defvjp(_tri_inverse_known_fwd, _tri_inverse_known_bwd)


DN_NC = 2


def _gdn_chunk(q, k, v, ab, alog, dtb, S, X_known=None, keep_X=False):
    H, C, _ = q.shape
    NC, NH = ab.shape[0], H // ab.shape[0]
    lane = lax.broadcasted_iota(jnp.int32, (H, C, LANE), 2)
    head = lax.broadcasted_iota(jnp.int32, (H, C, LANE), 0) & (NH - 1)
    abb = jnp.concatenate([jnp.broadcast_to(ab[c][None], (NH, C, LANE)) for c in range(NC)], axis=0)
    a = jnp.sum(jnp.where(lane == head, abb, 0.0), axis=2, keepdims=True)
    b = jnp.sum(jnp.where(lane == head + 8, abb, 0.0), axis=2, keepdims=True)
    pick = lax.broadcasted_iota(jnp.int32, (H, 1, LANE), 2) == (lax.broadcasted_iota(jnp.int32, (H, 1, LANE), 0) & (NH - 1))
    al = jnp.sum(jnp.where(pick, alog[None], 0.0), axis=2, keepdims=True)
    db = jnp.sum(jnp.where(pick, dtb[None], 0.0), axis=2, keepdims=True)
    g = -jnp.exp(al) * jax.nn.softplus(a + db)
    beta = jax.nn.sigmoid(b)
    ri = lax.broadcasted_iota(jnp.int32, (C, C), 0)
    ci = lax.broadcasted_iota(jnp.int32, (C, C), 1)
    G = jnp.broadcast_to(g, (H, C, LANE))
    gc = _cumsum_rows(G)
    gi = _cumsum_rows(jnp.broadcast_to(g, (H, C, C)))
    decay = jnp.exp(jnp.where((ri >= ci)[None], gi - jnp.swapaxes(gi, 1, 2), -jnp.inf))
    qs = q * (LANE ** -0.5)
    kb = k * beta
    L = jnp.where((ri > ci)[None], _bdot_nt(kb, k) * decay, 0.0)
    X = _tri_inverse(L) if X_known is None else _tri_inverse_known(L, X_known)
    egc = jnp.exp(gc)
    u = _solve_apply(X, v * beta)
    w = _solve_apply(X, kb * egc)
    qk = _bdot_nt(qs, k) * decay
    g_last = jnp.sum(G, axis=1, keepdims=True)
    k_dec = k * jnp.exp(g_last - gc)
    q_dec = qs * egc
    e_last = jnp.exp(g_last)
    outs = []
    for c in range(NC):
        sl = slice(c * NH, (c + 1) * NH)
        v_new = u[sl] - _bdot_nn(w[sl], S)
        outs.append(_bdot_nn(q_dec[sl], S) + _bdot_nn(qk[sl], v_new))
        S = S * e_last[sl] + _bdot_tn(k_dec[sl], v_new)
    o = jnp.concatenate(outs, axis=0)
    return (o, S, X) if keep_X else (o, S)


def _heads(ref):
    return jnp.stack([ref[DN_C * c:DN_C * (c + 1), LANE * h:LANE * (h + 1)] for c in range(DN_NC) for h in range(8)], axis=0)


def _put_heads(ref, val):
    for c in range(DN_NC):
        for h in range(8):
            ref[DN_C * c:DN_C * (c + 1), LANE * h:LANE * (h + 1)] = val[8 * c + h]


def _chunk_rows(ref):
    return jnp.stack([ref[DN_C * c:DN_C * (c + 1), :] for c in range(DN_NC)], axis=0)


def _hosted_parts(hosted):
    if hosted is None:
        return [], [], [], []
    n = len(hosted["arrays"])
    return list(hosted["arrays"]), [HBM_SPEC] * n, list(hosted["out_shape"]), _dma_sems(hosted["n_sems"])


def _gdn_fwd(name, q, k, v, proj, alog, dtb, hosted=None):
    T = q.shape[0]
    R = DN_C * DN_NC
    N = T // R
    h_args, h_specs, h_shapes, h_sems = _hosted_parts(hosted)
    nh = len(h_args)

    def body(q_ref, k_ref, v_ref, ab_ref, al_ref, dt_ref, *rest):
        h_ins, (o_ref, sall_ref, xinv_ref), h_outs, s_scr, sems = rest[:nh], rest[nh:nh + 3], rest[nh + 3:2 * nh + 3], rest[2 * nh + 3], rest[2 * nh + 4:]
        step = pl.program_id(0)

        @pl.when(step == 0)
        def _():
            s_scr[...] = jnp.zeros_like(s_scr)
            if hosted is not None:
                hosted["start"](h_ins, h_outs, *sems)

        S = s_scr[...]
        sall_ref[...] = S
        o, S_new, X = _gdn_chunk(_heads(q_ref), _heads(k_ref), _heads(v_ref), _chunk_rows(ab_ref), al_ref[...], dt_ref[...], S, keep_X=True)
        _put_heads(o_ref, o)
        s_scr[...] = S_new
        xinv_ref[...] = X

        if hosted is not None:
            @pl.when(step == N - 1)
            def _():
                hosted["finish"](h_ins, h_outs, *sems)

    blk = pl.BlockSpec((R, 8 * LANE), lambda n: (n, 0))
    vec = pl.BlockSpec((1, LANE), lambda n: (0, 0))
    state = pl.BlockSpec((None, 8, LANE, LANE), lambda n: (n, 0, 0, 0))
    xinv = pl.BlockSpec((None, 8 * DN_NC, DN_C, DN_C), lambda n: (n, 0, 0, 0))
    outs = _pcall(
        body, grid=(N,), in_specs=[blk, blk, blk, pl.BlockSpec((R, LANE), lambda n: (n, CB_AB)), vec, vec] + h_specs,
        out_specs=[blk, state, xinv] + h_specs,
        out_shape=[SDS((T, 1024), f32), SDS((N, 8, LANE, LANE), f32), SDS((N, 8 * DN_NC, DN_C, DN_C), f32)] + h_shapes,
        scratch_shapes=[pltpu.VMEM((8, LANE, LANE), f32)] + h_sems, name=name)(q, k, v, proj, alog, dtb, *h_args)
    return outs[0], outs[1], outs[2], list(outs[3:])


def _gdn_bwd(name, q, k, v, proj, alog, dtb, sall, xinv, do, hosted=None):
    T = q.shape[0]
    R = DN_C * DN_NC
    N = T // R
    h_args, h_specs, h_shapes, h_sems = _hosted_parts(hosted)
    nh = len(h_args)

    def body(q_ref, k_ref, v_ref, ab_ref, al_ref, dt_ref, s_ref, x_ref, do_ref, *rest):
        h_ins, h_outs, ds_scr, sems = rest[:nh], rest[nh + 6:2 * nh + 6], rest[2 * nh + 6], rest[2 * nh + 7:]
        dq_ref, dk_ref, dv_ref, dab_ref, dal_ref, ddt_ref = rest[nh:nh + 6]
        step = pl.program_id(0)

        @pl.when(step == 0)
        def _():
            ds_scr[...] = jnp.zeros_like(ds_scr)
            dal_ref[...] = jnp.zeros_like(dal_ref)
            ddt_ref[...] = jnp.zeros_like(ddt_ref)
            if hosted is not None:
                hosted["start"](h_ins, h_outs, *sems)

        _, vjp = jax.vjp(functools.partial(_gdn_chunk, X_known=x_ref[...]), _heads(q_ref), _heads(k_ref), _heads(v_ref), _chunk_rows(ab_ref),
                         al_ref[...], dt_ref[...], s_ref[...])
        dq, dk, dv, dab, dal, ddt, dS = vjp((_heads(do_ref), ds_scr[...]))
        _put_heads(dq_ref, dq)
        _put_heads(dk_ref, dk)
        _put_heads(dv_ref, dv)
        ds_scr[...] = dS
        for c in range(DN_NC):
            dab_ref[DN_C * c:DN_C * (c + 1), :] = dab[c]
        dal_ref[...] += dal
        ddt_ref[...] += ddt

        if hosted is not None:
            @pl.when(step == N - 1)
            def _():
                hosted["finish"](h_ins, h_outs, *sems)

    blk = pl.BlockSpec((R, 8 * LANE), lambda n: (N - 1 - n, 0))
    vec = pl.BlockSpec((1, LANE), lambda n: (0, 0))
    state = pl.BlockSpec((None, 8, LANE, LANE), lambda n: (N - 1 - n, 0, 0, 0))
    outs = _pcall(
        body, grid=(N,),
        in_specs=[blk, blk, blk, pl.BlockSpec((R, LANE), lambda n: (N - 1 - n, CB_AB)), vec, vec, state,
                  pl.BlockSpec((None, 8 * DN_NC, DN_C, DN_C), lambda n: (N - 1 - n, 0, 0, 0)), blk] + h_specs,
        out_specs=[blk, blk, blk, pl.BlockSpec((R, LANE), lambda n: (N - 1 - n, 0)), vec, vec] + h_specs,
        out_shape=[SDS((T, 1024), f32)] * 3 + [SDS((T, LANE), f32), SDS((1, LANE), f32), SDS((1, LANE), f32)] + h_shapes,
        scratch_shapes=[pltpu.VMEM((8, LANE, LANE), f32)] + h_sems, name=name)(q, k, v, proj, alog, dtb, sall, xinv, do, *h_args)
    return tuple(outs[:6]), list(outs[6:])


def _segmean_raw(x2, bd):
    return jnp.concatenate([_dot(x2[:, LANE * j:LANE * (j + 1)], bd, 1, 0) for j in range(x2.shape[1] // LANE)], axis=1)


@jax.custom_vjp
def _segmean(x2, bd):
    return _segmean_raw(x2, bd)


def _segmean_fwd(x2, bd):
    return _segmean_raw(x2, bd), bd


def _segmean_bwd(bd, dy):
    return _segmean_raw(dy, bd), jnp.zeros_like(bd)


_segmean.defvjp(_segmean_fwd, _segmean_bwd)


def _qknorm_fn(x, w, bd):
    return x * lax.rsqrt(_segmean(x * x, bd) + EPS) * w


def _rope_apply(xn, c, s1, s2):
    W = xn.shape[1]
    return xn * c + pltpu.roll(xn, W - 8, 1) * s1 + pltpu.roll(xn, 8, 1) * s2


def _rope_apply_t(d, c, s1, s2):
    W = d.shape[1]
    return d * c + pltpu.roll(d * s1, 8, 1) + pltpu.roll(d * s2, W - 8, 1)


def _rope_tiles(refs, width):
    return [jnp.tile(r[...], (1, width // LANE)) for r in refs]


def _qkprep_fwd(name, proj, wb, width, w, bd, tabs):
    T = proj.shape[0]
    tr = _pick(T, (256, 128))

    def body(x_ref, w_ref, bd_ref, c_ref, s1_ref, s2_ref, o_ref):
        xn = _qknorm_fn(x_ref[...], w_ref[...], bd_ref[...])
        o_ref[...] = _rope_apply(xn, *_rope_tiles((c_ref, s1_ref, s2_ref), width))

    row0 = pl.BlockSpec((tr, width), lambda i: (i, 0))
    tab = pl.BlockSpec((tr, LANE), lambda i: (i, 0))
    full = lambda a: pl.BlockSpec(a.shape, lambda i: (0, 0))
    return _pcall(
        body, grid=(T // tr,), in_specs=[pl.BlockSpec((tr, width), lambda i: (i, wb)), full(w), full(bd), tab, tab, tab],
        out_specs=row0, out_shape=SDS((T, width), f32), name=name)(proj, w, bd, *tabs)


def _qkprep_bwd(name, proj, wb, width, w, bd, tabs, dout, dest_buf="none"):
    T = proj.shape[0]
    tr = _pick(T, (256, 128))
    into = not isinstance(dest_buf, str)
    b_args, b_specs, aliases = _into_buffer(dest_buf, 7, 0) if into else ([], [], {})

    def body(x_ref, w_ref, bd_ref, c_ref, s1_ref, s2_ref, do_ref, *rest):
        dx_ref, dw_ref = rest[len(b_args):]
        i = pl.program_id(0)
        dxn = _rope_apply_t(do_ref[...], *_rope_tiles((c_ref, s1_ref, s2_ref), width))
        bd = bd_ref[...]
        _, vjp = jax.vjp(lambda x, w_: _qknorm_fn(x, w_, bd), x_ref[...], w_ref[...])
        dx, dw = vjp(dxn)
        dx_ref[...] = dx.astype(dx_ref.dtype)

        @pl.when(i == 0)
        def _():
            dw_ref[...] = jnp.zeros_like(dw_ref)

        dw_ref[...] += dw

    row0 = pl.BlockSpec((tr, width), lambda i: (i, 0))
    tab = pl.BlockSpec((tr, LANE), lambda i: (i, 0))
    full = lambda a: pl.BlockSpec(a.shape, lambda i: (0, 0))
    dx_spec = pl.BlockSpec((tr, width), lambda i: (i, wb)) if into else row0
    dx_shape = SDS((T, PROJ_W), bf16) if into else SDS((T, width), bf16)
    return _pcall(
        body, grid=(T // tr,), in_specs=[pl.BlockSpec((tr, width), lambda i: (i, wb)), full(w), full(bd), tab, tab, tab, row0] + b_specs,
        out_specs=[dx_spec, full(w)], out_shape=[dx_shape, SDS(w.shape, f32)], input_output_aliases=aliases, name=name)(
            proj, w, bd, *tabs, dout, *b_args)


def _make_dot16(ca, cb):
    def raw(x, y, cx, cy):
        return _dot(x.astype(bf16), y.astype(bf16), cx, cy, precision=None)

    @jax.custom_vjp
    def f(a, b):
        return raw(a, b, ca, cb)

    def fwd(a, b):
        return raw(a, b, ca, cb), (a, b)

    def bwd(res, dy):
        a, b = res
        if (ca, cb) == (1, 0):
            return raw(dy, b, 1, 1), raw(a, dy, 0, 0)
        return raw(dy, b, 1, 0), raw(dy, a, 0, 0)

    f.defvjp(fwd, bwd)
    return f


_dot16_nn, _dot16_nt = _make_dot16(1, 0), _make_dot16(1, 1)


def _attn_bias():
    qi = jnp.arange(8 * SWA_B) % SWA_B
    kj = jnp.arange(2 * SWA_B)
    rel = qi[:, None] + SWA_B - kj[None, :]
    valid = (rel >= 0) & (rel < SWA_B)
    neg = jnp.float32(-jnp.inf)
    return jnp.stack([jnp.where(valid & (kj[None, :] >= SWA_B), 0.0, neg), jnp.where(valid, 0.0, neg)]).astype(f32)


def _attn_group(qg, kb, vb, sinks, bias, hk):
    R = qg.shape[0]
    s = _dot16_nt(qg, kb) * 0.125 + bias
    head = (lax.broadcasted_iota(jnp.int32, (R, LANE), 0) >> 7) + 8 * hk
    lane = lax.broadcasted_iota(jnp.int32, (R, LANE), 1)
    sink = jnp.sum(jnp.where(lane == head, jnp.broadcast_to(sinks, (R, LANE)), 0.0), axis=1, keepdims=True)
    m = lax.stop_gradient(jnp.maximum(jnp.max(s, axis=1, keepdims=True), sink))
    p = jnp.exp(s - m)
    denom = jnp.sum(p, axis=1, keepdims=True) + jnp.exp(sink - m)
    return _dot16_nn(p / denom, vb)


def _group_rows(ref, hk):
    return jnp.concatenate([ref[:, 64 * (8 * hk + g):64 * (8 * hk + g + 1)] for g in range(8)], axis=0)


def _put_group(ref, hk, val):
    for g in range(8):
        ref[:, 64 * (8 * hk + g):64 * (8 * hk + g + 1)] = val[SWA_B * g:SWA_B * (g + 1)].astype(ref.dtype)


def _attn_specs():
    qs = pl.BlockSpec((SWA_B, 1024), lambda i: (i, 0))
    cur = pl.BlockSpec((SWA_B, LANE), lambda i: (i, 0))
    prev = pl.BlockSpec((SWA_B, LANE), lambda i: (jnp.maximum(i - 1, 0), 0))
    vcur = pl.BlockSpec((SWA_B, LANE), lambda i: (i, CB_SWV))
    vprev = pl.BlockSpec((SWA_B, LANE), lambda i: (jnp.maximum(i - 1, 0), CB_SWV))
    vec = pl.BlockSpec((1, LANE), lambda i: (0, 0))
    bias = pl.BlockSpec((None, 8 * SWA_B, 2 * SWA_B), lambda i: (jnp.minimum(i, 1), 0, 0))
    return qs, cur, prev, vcur, vprev, vec, bias


def _hosted_edge(hosted, which, h_ins, h_outs, sems, at):
    if hosted is None:
        return

    @pl.when(at)
    def _():
        hosted[which](h_ins, h_outs, *sems)


def _attn_fwd(name, sq, sk, proj, sinks, bias, hosted=None):
    T = sq.shape[0]
    nb = T // SWA_B
    h_args, h_specs, h_shapes, h_sems = _hosted_parts(hosted)
    nh = len(h_args)

    def body(q_ref, kp_ref, kc_ref, vp_ref, vc_ref, sk_ref, b_ref, *rest):
        h_ins, o_ref, h_outs, sems = rest[:nh], rest[nh], rest[nh + 1:2 * nh + 1], rest[2 * nh + 1:]
        _hosted_edge(hosted, "start", h_ins, h_outs, sems, pl.program_id(0) == 0)
        sinks_v, bias_v = sk_ref[...], b_ref[...]
        for hk in range(2):
            ks = slice(64 * hk, 64 * hk + 64)
            kb = jnp.concatenate([kp_ref[:, ks], kc_ref[:, ks]], axis=0)
            vb = jnp.concatenate([vp_ref[:, ks], vc_ref[:, ks]], axis=0)
            _put_group(o_ref, hk, _attn_group(_group_rows(q_ref, hk), kb, vb, sinks_v, bias_v, hk))
        _hosted_edge(hosted, "finish", h_ins, h_outs, sems, pl.program_id(0) == nb - 1)

    qs, cur, prev, vcur, vprev, vec, bspec = _attn_specs()
    outs = _pcall(body, grid=(nb,), in_specs=[qs, prev, cur, vprev, vcur, vec, bspec] + h_specs, out_specs=[qs] + h_specs,
                  out_shape=[SDS((T, 1024), bf16)] + h_shapes, scratch_shapes=h_sems, name=name)(sq, sk, sk, proj, proj, sinks, bias, *h_args)
    return outs[0], list(outs[1:])


def _attn_bwd(name, sq, sk, proj, sinks, bias, do, hosted=None):
    T = sq.shape[0]
    nb = T // SWA_B
    h_args, h_specs, h_shapes, h_sems = _hosted_parts(hosted)
    nh = len(h_args)

    def body(q_ref, kp_ref, kc_ref, vp_ref, vc_ref, sk_ref, b_ref, do_ref, *rest):
        h_ins, h_outs, sems = rest[:nh], rest[nh + 6:2 * nh + 6], rest[2 * nh + 6:]
        dq_ref, dkp_ref, dkc_ref, dvp_ref, dvc_ref, dsk_ref = rest[nh:nh + 6]
        _hosted_edge(hosted, "start", h_ins, h_outs, sems, pl.program_id(0) == 0)

        @pl.when(pl.program_id(0) == 0)
        def _():
            dsk_ref[...] = jnp.zeros_like(dsk_ref)

        sinks_v, bias_v = sk_ref[...], b_ref[...]
        dsk = jnp.zeros((1, LANE), f32)
        for hk in range(2):
            ks = slice(64 * hk, 64 * hk + 64)
            kb = jnp.concatenate([kp_ref[:, ks], kc_ref[:, ks]], axis=0)
            vb = jnp.concatenate([vp_ref[:, ks], vc_ref[:, ks]], axis=0)
            _, vjp = jax.vjp(functools.partial(_attn_group, bias=bias_v, hk=hk), _group_rows(q_ref, hk), kb, vb, sinks_v)
            dq, dkb, dvb, ds_ = vjp(_group_rows(do_ref, hk))
            _put_group(dq_ref, hk, dq)
            dsk = dsk + ds_
            dkp_ref[:, ks] = dkb[:SWA_B]
            dkc_ref[:, ks] = dkb[SWA_B:]
            dvp_ref[:, ks] = dvb[:SWA_B]
            dvc_ref[:, ks] = dvb[SWA_B:]
        dsk_ref[...] += dsk
        _hosted_edge(hosted, "finish", h_ins, h_outs, sems, pl.program_id(0) == nb - 1)

    qs, cur, prev, vcur, vprev, vec, bspec = _attn_specs()
    outs = _pcall(
        body, grid=(nb,), in_specs=[qs, prev, cur, vprev, vcur, vec, bspec, qs] + h_specs, out_specs=[qs, cur, cur, cur, cur, vec] + h_specs,
        out_shape=[SDS((T, 1024), f32)] + [SDS((T, LANE), f32)] * 4 + [SDS((1, LANE), f32)] + h_shapes, scratch_shapes=h_sems,
        name=name)(sq, sk, sk, proj, proj, sinks, bias, do, *h_args)
    return tuple(outs[:6]), list(outs[6:])


def _shift_add(name, cur, prev, out_dtype):
    T = cur.shape[0]

    def body(c_ref, p_ref, o_ref):
        o_ref[0:T - SWA_B, :] = (c_ref[0:T - SWA_B, :] + p_ref[SWA_B:T, :]).astype(o_ref.dtype)
        o_ref[T - SWA_B:T, :] = c_ref[T - SWA_B:T, :].astype(o_ref.dtype)

    return _pcall(body, out_shape=SDS((T, LANE), out_dtype), name=name)(cur, prev)


def _loss(name, y, tgt):
    T = y.shape[0]

    def body(y_ref, t_ref, l_ref, dy_ref):
        @pl.when(pl.program_id(0) == 0)
        def _():
            l_ref[...] = jnp.zeros_like(l_ref)

        d = y_ref[...] - t_ref[...]
        l_ref[...] += jnp.sum(d * d) * (0.5 / D)
        dy_ref[...] = d * (1.0 / D)

    row = pl.BlockSpec((TR, D), lambda i: (i, 0))
    return _pcall(body, grid=(T // TR,), in_specs=[row, row], out_specs=[pl.BlockSpec((8, LANE), lambda i: (0, 0)), row],
                  out_shape=[SDS((8, LANE), f32), SDS((T, D), f32)], name=name)(y, tgt)


def _adamw(name, w, g, m, v):
    shape = w.shape
    C = shape[-1]
    R = int(np.prod(shape[:-1]))
    tr = _pick(R, (128, 64, 16, 8))
    bc1 = np.float32(1.0 - ADAM_B1 ** ADAM_STEP)
    bc2 = np.float32(1.0 - ADAM_B2 ** ADAM_STEP)

    def body(w_ref, g_ref, m_ref, v_ref, d_ref, mo_ref, vo_ref):
        g_ = g_ref[...]
        m_ = ADAM_B1 * m_ref[...] + (1.0 - ADAM_B1) * g_
        v_ = ADAM_B2 * v_ref[...] + (1.0 - ADAM_B2) * (g_ * g_)
        d_ref[...] = -ADAM_LR * ((m_ / bc1) / (jnp.sqrt(v_ / bc2) + ADAM_EPS) + ADAM_WD * w_ref[...])
        mo_ref[...] = m_
        vo_ref[...] = v_

    blk = pl.BlockSpec((tr, C), lambda i: (i, 0))
    outs = _pcall(body, grid=(R // tr,), in_specs=[blk] * 4, out_specs=[blk] * 3, out_shape=[SDS((R, C), f32)] * 3,
                  compiler_params=_cparams(VMEM_BIG), name=name)(*[t.reshape(R, C) for t in (w, g, m, v)])
    return [o.reshape(shape) for o in outs]


def _silu_rows(name, x):
    def body(x_ref, o_ref):
        t = x_ref[...]
        o_ref[...] = (t * jax.nn.sigmoid(t)).astype(o_ref.dtype)

    return _pcall(body, out_shape=SDS(x.shape, bf16), name=name)(x)


def _sum_leading(name, x):
    n = x.shape[0]

    def body(x_ref, o_ref):
        acc = x_ref[0]
        for k in range(1, n):
            acc = acc + x_ref[k]
        o_ref[...] = acc

    tr = x.shape[1] if x.size * 4 <= 8 * 2 ** 20 else _pick(x.shape[1], (COMM_TR, 8))
    return _pcall(body, grid=(x.shape[1] // tr,), in_specs=[pl.BlockSpec((n, tr, x.shape[2]), lambda i: (0, i, 0))],
                  out_specs=pl.BlockSpec((tr, x.shape[2]), lambda i: (i, 0)), out_shape=SDS(x.shape[1:], x.dtype), name=name)(x)


def _add_my_half(name, g4, b1, c):
    _, R, W = g4.shape
    nblk = (R // 2) // COMM_TR

    def body(c_ref, g_ref, b_ref, o_ref):
        o_ref[...] = g_ref[...] + b_ref[...]

    grid_spec = pltpu.PrefetchScalarGridSpec(
        num_scalar_prefetch=1, grid=(4, nblk),
        in_specs=[pl.BlockSpec((None, COMM_TR, W), lambda s, i, c_ref: (s, c_ref[0] * nblk + i, 0)),
                  pl.BlockSpec((None, COMM_TR, W), lambda s, i, c_ref: (s, i, 0))],
        out_specs=pl.BlockSpec((None, COMM_TR, W), lambda s, i, c_ref: (s, i, 0)))
    return _pcall(body, grid_spec=grid_spec, out_shape=SDS((4, R // 2, W), f32), name=name)(c.reshape(1), g4, b1)


HBM_SPEC = pl.BlockSpec(memory_space=pltpu.HBM)


def _position():
    x, y, c = lax.axis_index("x"), lax.axis_index("y"), lax.axis_index("c")
    return x, y, c, [(1 - x, y), (x, 1 - y), (1 - x, 1 - y)]


def _remote(src, dst, send_sems, recv_sems, k, to):
    return pltpu.make_async_remote_copy(src_ref=src, dst_ref=dst, send_sem=send_sems.at[k], recv_sem=recv_sems.at[k],
                                        device_id=to, device_id_type=MESH)


def _allgather_chips(name, buf):
    R, W = buf.shape
    Rh = R // 2

    def body(in_ref, out_ref, send_sems, recv_sems, local_sem):
        x, y, c, chips = _position()
        me = 2 * x + y
        sib = (x, y, 1 - c)
        half = pl.ds(pl.multiple_of(c * Rh, 32), Rh)
        ohalf = pl.ds(pl.multiple_of((1 - c) * Rh, 32), Rh)
        mine = pltpu.make_async_copy(in_ref, out_ref.at[me], local_sem)
        mine.start()
        first = [_remote(in_ref.at[half], out_ref.at[me, half], send_sems, recv_sems, j, (cx, cy, c)) for j, (cx, cy) in enumerate(chips)]
        for cp in first:
            cp.start()
        passed = []
        for j, (cx, cy) in enumerate(chips):
            rows = out_ref.at[2 * cx + cy, half]
            _remote(rows, rows, send_sems, recv_sems, j, (cx, cy, c)).wait_recv()
            cp = _remote(rows, rows, send_sems, recv_sems, 3 + j, sib)
            cp.start()
            passed.append(cp)
        for j, (cx, cy) in enumerate(chips):
            rows = out_ref.at[2 * cx + cy, ohalf]
            _remote(rows, rows, send_sems, recv_sems, 3 + j, sib).wait_recv()
        for cp in first + passed:
            cp.wait_send()
        mine.wait()

    return _pcall(body, in_specs=[HBM_SPEC], out_specs=HBM_SPEC, out_shape=SDS((4, R, W), buf.dtype),
                  scratch_shapes=[pltpu.SemaphoreType.DMA((6,)), pltpu.SemaphoreType.DMA((6,)), pltpu.SemaphoreType.DMA], name=name)(buf)


def _swap_halves(name, g4):
    _, R, W = g4.shape
    Rh = R // 2

    def body(in_ref, out_ref, send_sems, recv_sems):
        x, y, c, _ = _position()
        ohalf = pl.ds(pl.multiple_of((1 - c) * Rh, 32), Rh)
        cp = _remote(in_ref.at[:, ohalf, :], out_ref, send_sems, recv_sems, 0, (x, y, 1 - c))
        cp.start()
        cp.wait()

    return _pcall(body, in_specs=[HBM_SPEC], out_specs=HBM_SPEC, out_shape=SDS((4, Rh, W), g4.dtype),
                  scratch_shapes=[pltpu.SemaphoreType.DMA((1,)), pltpu.SemaphoreType.DMA((1,))], name=name)(g4)


def _scatter_chips(name, p4):
    _, Rh, W = p4.shape

    def body(in_ref, out_ref, send_sems, recv_sems, local_sem):
        x, y, c, chips = _position()
        me = 2 * x + y
        mine = pltpu.make_async_copy(in_ref.at[me], out_ref.at[me], local_sem)
        mine.start()
        sends = [_remote(in_ref.at[2 * cx + cy], out_ref.at[me], send_sems, recv_sems, j, (cx, cy, c)) for j, (cx, cy) in enumerate(chips)]
        for cp in sends:
            cp.start()
        for j, (cx, cy) in enumerate(chips):
            slot = out_ref.at[2 * cx + cy]
            _remote(slot, slot, send_sems, recv_sems, j, (cx, cy, c)).wait_recv()
        for cp in sends:
            cp.wait_send()
        mine.wait()

    return _pcall(body, in_specs=[HBM_SPEC], out_specs=HBM_SPEC, out_shape=SDS((4, Rh, W), p4.dtype),
                  scratch_shapes=[pltpu.SemaphoreType.DMA((3,)), pltpu.SemaphoreType.DMA((3,)), pltpu.SemaphoreType.DMA], name=name)(p4)


def _join_halves(name, r):
    Rh, W = r.shape

    def body(in_ref, out_ref, send_sems, recv_sems, local_sem):
        x, y, c, _ = _position()
        mine = pltpu.make_async_copy(in_ref, out_ref.at[c], local_sem)
        mine.start()
        cp = _remote(in_ref, out_ref.at[c], send_sems, recv_sems, 0, (x, y, 1 - c))
        cp.start()
        _remote(in_ref, out_ref.at[1 - c], send_sems, recv_sems, 0, (x, y, 1 - c)).wait_recv()
        cp.wait_send()
        mine.wait()

    return _pcall(body, in_specs=[HBM_SPEC], out_specs=HBM_SPEC, out_shape=SDS((2, Rh, W), r.dtype),
                  scratch_shapes=[pltpu.SemaphoreType.DMA((1,)), pltpu.SemaphoreType.DMA((1,)), pltpu.SemaphoreType.DMA], name=name)(r)


def _allgather_all(name, buf):
    r, W = buf.shape

    def body(in_ref, out_ref, send_sems, recv_sems, local_sem):
        x, y, c, _ = _position()
        me = 4 * x + 2 * y + c
        mine = pltpu.make_async_copy(in_ref, out_ref.at[me], local_sem)
        mine.start()
        peers = []
        for mk in range(1, 8):
            mx, my, mc = (mk >> 2) & 1, (mk >> 1) & 1, mk & 1
            px = 1 - x if mx else x
            py = 1 - y if my else y
            pc = 1 - c if mc else c
            peers.append((px, py, pc))
        sends = [_remote(in_ref, out_ref.at[me], send_sems, recv_sems, k, p) for k, p in enumerate(peers)]
        for cp in sends:
            cp.start()
        for k, (px, py, pc) in enumerate(peers):
            slot = out_ref.at[4 * px + 2 * py + pc]
            _remote(slot, slot, send_sems, recv_sems, k, (px, py, pc)).wait_recv()
        for cp in sends:
            cp.wait_send()
        mine.wait()

    return _pcall(body, in_specs=[HBM_SPEC], out_specs=HBM_SPEC, out_shape=SDS((8, r, W), buf.dtype),
                  scratch_shapes=[pltpu.SemaphoreType.DMA((7,)), pltpu.SemaphoreType.DMA((7,)), pltpu.SemaphoreType.DMA], name=name)(buf)


def _reduce_scatter(g4, c, tag):
    b1 = _swap_halves("rs_swap_" + tag, g4)
    p4 = _add_my_half("rs_pair_" + tag, g4, b1, c)
    b2 = _scatter_chips("rs_scatter_" + tag, p4)
    r = _sum_leading("rs_sum_" + tag, b2)
    full = _join_halves("rs_join_" + tag, r)
    return full.reshape(g4.shape[1], g4.shape[2])


def _dma_sems(n):
    return [pltpu.SemaphoreType.DMA((n,)), pltpu.SemaphoreType.DMA((n,))]


def _gather_chips(name, shards):
    n = len(shards)

    def body(*refs):
        ins, outs = refs[:n], refs[n:2 * n]
        send_sems, recv_sems = refs[2 * n:]
        x, y, c, chips = _position()
        me = 2 * x + y
        sib = (x, y, 1 - c)
        sends, halves = [], []
        for i in range(n):
            rh = ins[i].shape[0] // 2
            halves.append((pl.ds(pl.multiple_of(c * rh, 16), rh), pl.ds(pl.multiple_of((1 - c) * rh, 16), rh)))
        for i in range(n):
            for j, (cx, cy) in enumerate(chips):
                cp = _remote(ins[i].at[halves[i][0]], outs[i].at[me, halves[i][0]], send_sems, recv_sems, 6 * i + j, (cx, cy, c))
                cp.start()
                sends.append(cp)
        for j, (cx, cy) in enumerate(chips):
            for i in range(n):
                rows = outs[i].at[2 * cx + cy, halves[i][0]]
                _remote(rows, rows, send_sems, recv_sems, 6 * i + j, (cx, cy, c)).wait_recv()
                cp = _remote(rows, rows, send_sems, recv_sems, 6 * i + 3 + j, sib)
                cp.start()
                sends.append(cp)
        for j, (cx, cy) in enumerate(chips):
            for i in range(n):
                rows = outs[i].at[2 * cx + cy, halves[i][1]]
                _remote(rows, rows, send_sems, recv_sems, 6 * i + 3 + j, sib).wait_recv()
        for cp in sends:
            cp.wait_send()

    return _pcall(body, in_specs=[HBM_SPEC] * n, out_specs=[HBM_SPEC] * n, out_shape=[SDS((4,) + s.shape, s.dtype) for s in shards],
                  scratch_shapes=_dma_sems(6 * n), name=name)(*shards)


def _swap_halves_multi(name, slots):
    n = len(slots)

    def body(*refs):
        ins, outs = refs[:n], refs[n:2 * n]
        send_sems, recv_sems = refs[2 * n:]
        x, y, c, _ = _position()
        cps = []
        for i in range(n):
            rh = ins[i].shape[1] // 2
            ohalf = pl.ds(pl.multiple_of((1 - c) * rh, 8), rh)
            cp = _remote(ins[i].at[:, ohalf, :], outs[i], send_sems, recv_sems, i, (x, y, 1 - c))
            cp.start()
            cps.append(cp)
        for cp in cps:
            cp.wait()

    return _pcall(body, in_specs=[HBM_SPEC] * n, out_specs=[HBM_SPEC] * n,
                  out_shape=[SDS((4, s.shape[1] // 2, s.shape[2]), s.dtype) for s in slots], scratch_shapes=_dma_sems(n), name=name)(*slots)


def _pair_add(name, g4, b1, c):
    _, R, W = g4.shape
    tr = _pick(R // 2, (256, 128, 32, 16))
    nblk = (R // 2) // tr

    def body(c_ref, g_ref, b_ref, o_ref):
        o_ref[...] = (g_ref[...] + b_ref[...]).astype(o_ref.dtype)

    grid_spec = pltpu.PrefetchScalarGridSpec(
        num_scalar_prefetch=1, grid=(4, nblk),
        in_specs=[pl.BlockSpec((None, tr, W), lambda s, i, c_ref: (s, c_ref[0] * nblk + i, 0)),
                  pl.BlockSpec((None, tr, W), lambda s, i, c_ref: (s, i, 0))],
        out_specs=pl.BlockSpec((None, tr, W), lambda s, i, c_ref: (s, i, 0)))
    return _pcall(body, grid_spec=grid_spec, out_shape=SDS((4, R // 2, W), bf16), name=name)(c.reshape(1), g4, b1)


def _scatter_chips_multi(name, ps):
    n = len(ps)

    def body(*refs):
        ins, outs = refs[:n], refs[n:2 * n]
        send_sems, recv_sems = refs[2 * n:]
        x, y, c, chips = _position()
        me = 2 * x + y
        sends = []
        for i in range(n):
            for j, (cx, cy) in enumerate(chips):
                cp = _remote(ins[i].at[2 * cx + cy], outs[i].at[me], send_sems, recv_sems, 3 * i + j, (cx, cy, c))
                cp.start()
                sends.append(cp)
        for i in range(n):
            for j, (cx, cy) in enumerate(chips):
                slot = outs[i].at[2 * cx + cy]
                _remote(slot, slot, send_sems, recv_sems, 3 * i + j, (cx, cy, c)).wait_recv()
        for cp in sends:
            cp.wait_send()

    return _pcall(body, in_specs=[HBM_SPEC] * n, out_specs=[HBM_SPEC] * n, out_shape=[SDS(p.shape, p.dtype) for p in ps],
                  scratch_shapes=_dma_sems(3 * n), name=name)(*ps)


def _sum_chips(name, p4, b2, chip, c):
    _, Rh, W = p4.shape
    tr = _pick(Rh, (256, 128, 32, 16))
    nblk = Rh // tr

    def body(m_ref, c_ref, own_ref, r1_ref, r2_ref, r3_ref, o_ref):
        o_ref[...] = ((own_ref[...].astype(f32) + r1_ref[...].astype(f32)) + r2_ref[...].astype(f32)) + r3_ref[...].astype(f32)

    other = lambda k: pl.BlockSpec((None, tr, W), lambda i, m_ref, c_ref: (m_ref[0] ^ k, i, 0))
    grid_spec = pltpu.PrefetchScalarGridSpec(
        num_scalar_prefetch=2, grid=(nblk,),
        in_specs=[pl.BlockSpec((None, tr, W), lambda i, m_ref, c_ref: (m_ref[0], i, 0)), other(1), other(2), other(3)],
        out_specs=pl.BlockSpec((tr, W), lambda i, m_ref, c_ref: (c_ref[0] * nblk + i, 0)))
    return _pcall(body, grid_spec=grid_spec, out_shape=SDS((2 * Rh, W), f32), name=name)(chip.reshape(1), c.reshape(1), p4, b2, b2, b2)


def _join_halves(name, fulls):
    n = len(fulls)

    def body(*refs):
        outs = refs[n:2 * n]
        send_sems, recv_sems = refs[2 * n:]
        x, y, c, _ = _position()
        cps = []
        for i in range(n):
            rh = outs[i].shape[0] // 2
            mine = outs[i].at[pl.ds(pl.multiple_of(c * rh, 8), rh)]
            theirs = outs[i].at[pl.ds(pl.multiple_of((1 - c) * rh, 8), rh)]
            cp = _remote(mine, mine, send_sems, recv_sems, i, (x, y, 1 - c))
            cp.start()
            cps.append((cp, _remote(theirs, theirs, send_sems, recv_sems, i, (x, y, 1 - c))))
        for cp, back in cps:
            back.wait_recv()
            cp.wait_send()

    return _pcall(body, in_specs=[HBM_SPEC] * n, out_specs=[HBM_SPEC] * n, out_shape=[SDS(r.shape, r.dtype) for r in fulls],
                  input_output_aliases={i: i for i in range(n)}, scratch_shapes=_dma_sems(n), name=name)(*fulls)


def _hosted_gather(shards):
    n = len(shards)

    def half(ref_rows, c):
        rh = ref_rows // 2
        return pl.ds(pl.multiple_of(c * rh, 16), rh)

    def start(ins, outs, send_sems, recv_sems):
        x, y, c, chips = _position()
        me = 2 * x + y
        for i in range(n):
            rows = half(ins[i].shape[0], c)
            for j, (cx, cy) in enumerate(chips):
                _remote(ins[i].at[rows], outs[i].at[me, rows], send_sems, recv_sems, 3 * i + j, (cx, cy, c)).start()

    def finish(ins, outs, send_sems, recv_sems):
        x, y, c, chips = _position()
        me = 2 * x + y
        for i in range(n):
            rows = half(ins[i].shape[0], c)
            for j, (cx, cy) in enumerate(chips):
                _remote(ins[i].at[rows], outs[i].at[2 * cx + cy, rows], send_sems, recv_sems, 3 * i + j, (cx, cy, c)).wait_recv()
        for i in range(n):
            rows = half(ins[i].shape[0], c)
            for j, (cx, cy) in enumerate(chips):
                _remote(ins[i].at[rows], outs[i].at[me, rows], send_sems, recv_sems, 3 * i + j, (cx, cy, c)).wait_send()

    return {"arrays": shards, "out_shape": [SDS((4,) + s.shape, s.dtype) for s in shards], "n_sems": 3 * n, "start": start, "finish": finish}


def _gather_forward(name, gathered):
    n = len(gathered)

    def body(*refs):
        outs = refs[n:2 * n]
        send_sems, recv_sems = refs[2 * n:]
        x, y, c, chips = _position()
        sib = (x, y, 1 - c)
        sends = []
        for i in range(n):
            rh = outs[i].shape[1] // 2
            mine = pl.ds(pl.multiple_of(c * rh, 16), rh)
            for j, (cx, cy) in enumerate(chips):
                rows = outs[i].at[2 * cx + cy, mine]
                cp = _remote(rows, rows, send_sems, recv_sems, 3 * i + j, sib)
                cp.start()
                sends.append(cp)
        for i in range(n):
            rh = outs[i].shape[1] // 2
            theirs = pl.ds(pl.multiple_of((1 - c) * rh, 16), rh)
            for j, (cx, cy) in enumerate(chips):
                rows = outs[i].at[2 * cx + cy, theirs]
                _remote(rows, rows, send_sems, recv_sems, 3 * i + j, sib).wait_recv()
        for cp in sends:
            cp.wait_send()

    return _pcall(body, in_specs=[HBM_SPEC] * n, out_specs=[HBM_SPEC] * n, out_shape=[SDS(g.shape, g.dtype) for g in gathered],
                  input_output_aliases={i: i for i in range(n)}, scratch_shapes=_dma_sems(3 * n), name=name)(*gathered)


def _hosted_scatter(ps):
    n = len(ps)

    def start(ins, outs, send_sems, recv_sems):
        x, y, c, chips = _position()
        me = 2 * x + y
        for i in range(n):
            for j, (cx, cy) in enumerate(chips):
                _remote(ins[i].at[2 * cx + cy], outs[i].at[me], send_sems, recv_sems, 3 * i + j, (cx, cy, c)).start()

    def finish(ins, outs, send_sems, recv_sems):
        x, y, c, chips = _position()
        me = 2 * x + y
        for i in range(n):
            for j, (cx, cy) in enumerate(chips):
                slot = outs[i].at[2 * cx + cy]
                _remote(slot, slot, send_sems, recv_sems, 3 * i + j, (cx, cy, c)).wait_recv()
        for i in range(n):
            for j, (cx, cy) in enumerate(chips):
                _remote(ins[i].at[2 * cx + cy], outs[i].at[me], send_sems, recv_sems, 3 * i + j, (cx, cy, c)).wait_send()

    return {"arrays": ps, "out_shape": [SDS(p.shape, p.dtype) for p in ps], "n_sems": 3 * n, "start": start, "finish": finish}


def _hosted_swap(slots):
    n = len(slots)

    def copies(ins, outs, send_sems, recv_sems):
        x, y, c, _ = _position()
        cps = []
        for i in range(n):
            rh = ins[i].shape[1] // 2
            ohalf = pl.ds(pl.multiple_of((1 - c) * rh, 8), rh)
            cps.append(_remote(ins[i].at[:, ohalf, :], outs[i], send_sems, recv_sems, i, (x, y, 1 - c)))
        return cps

    def start(ins, outs, send_sems, recv_sems):
        for cp in copies(ins, outs, send_sems, recv_sems):
            cp.start()

    def finish(ins, outs, send_sems, recv_sems):
        for cp in copies(ins, outs, send_sems, recv_sems):
            cp.wait()

    return {"arrays": slots, "out_shape": [SDS((4, s.shape[1] // 2, s.shape[2]), s.dtype) for s in slots], "n_sems": n,
            "start": start, "finish": finish}


def _rs_begin(slots, c):
    b1 = _swap_halves_multi("rs_swap", slots)
    return [_pair_add("rs_pair_%d" % i, g, b, c) for i, (g, b) in enumerate(zip(slots, b1))]


def _rs_end(ps, b2, c, chip):
    return _join_halves("rs_join", [_sum_chips("rs_sum_%d" % i, p, b, chip, c) for i, (p, b) in enumerate(zip(ps, b2))])


def _reduce_scatter_multi(slots, c, chip):
    ps = _rs_begin(slots, c)
    return _rs_end(ps, _scatter_chips_multi("rs_scatter", ps), c, chip)


_BIG = ("w_in", "w_dn_out", "w_swa_out", "w_o", "w_up", "w_down")


_W_IN_PIECES = ((0, 3072, 0), (3072, 4096, 3072), (5392, 6416, 4096), (6416, 7440, 5120), (4112, 5136, 6144), (4096, 4112, 7168),
                (5136, 5264, 7296), (5264, 5392, 7424))
_W_IN_SHARD = IN_TOTAL // 4


def _w_in_from_slots(g):
    parts, at = [], 0
    for lo, hi, dst in _W_IN_PIECES:
        if dst > at:
            parts.append(jnp.zeros((g.shape[1], dst - at), g.dtype))
        for s in range(4):
            a, b = max(lo, s * _W_IN_SHARD), min(hi, (s + 1) * _W_IN_SHARD)
            if a < b:
                parts.append(g[s][:, a - s * _W_IN_SHARD:b - s * _W_IN_SHARD])
        at = dst + hi - lo
    parts.append(jnp.zeros((g.shape[1], PROJ_W - at), g.dtype))
    return jnp.concatenate(parts, axis=1)


def _w_in_to_slots(gw):
    slots = []
    for s in range(4):
        parts = []
        for lo, hi, dst in sorted(_W_IN_PIECES):
            a, b = max(lo, s * _W_IN_SHARD), min(hi, (s + 1) * _W_IN_SHARD)
            if a < b:
                parts.append(gw[:, dst + a - lo:dst + b - lo])
        slots.append(jnp.concatenate(parts, axis=1))
    return jnp.stack(slots)


def _assemble_mixer(gs):
    rows = lambda g: g.reshape(4 * g.shape[1], g.shape[2])
    return {"w_in": _w_in_from_slots(gs[0]), "w_dn_out": rows(gs[1]), "w_swa_out": rows(gs[2]), "w_o": rows(gs[3])}


def _assemble_ffn(gs):
    return {"w_up": jnp.concatenate([gs[0][s] for s in range(4)], axis=1), "w_down": gs[1].reshape(4 * gs[1].shape[1], gs[1].shape[2])}


def _grad_slots(gw):
    rows = lambda g: g.reshape(4, g.shape[0] // 4, g.shape[1])
    return [_w_in_to_slots(gw["w_in"]), rows(gw["w_dn_out"]), rows(gw["w_swa_out"]), rows(gw["w_o"]), gw["w_up"], rows(gw["w_down"])]


_SHARDED = (
    ("w_in", (D, 1860), 1, False),
    ("dn_conv", (4, 768), 1, True),
    ("w_dn_out", (256, D), 0, False),
    ("w_swa_out", (256, D), 0, False),
    ("w_o", (256, D), 0, False),
    ("w_up", (D, 1408), 1, False),
    ("ffn_conv", (3, 704), 1, True),
    ("w_down", (704, D), 0, False),
)


def _pack_weights(shards):
    parts = []
    for nm, shp, _, as_bits in _SHARDED:
        a = shards[nm]
        parts.append(lax.bitcast_convert_type(a, bf16).reshape(-1) if as_bits else a.astype(bf16).reshape(-1))
    flat = jnp.concatenate(parts)
    return jnp.pad(flat, (0, COMM_ROWS * COMM_W - flat.shape[0])).reshape(COMM_ROWS, COMM_W)


def _unpack_weights(g):
    flat = g.reshape(4, -1)
    out, off = {}, 0
    for nm, shp, ax, as_bits in _SHARDED:
        n = int(np.prod(shp)) * (2 if as_bits else 1)
        piece = flat[:, off:off + n]
        off += n
        if as_bits:
            piece = lax.bitcast_convert_type(piece.reshape((4,) + shp + (2,)), f32)
        else:
            piece = piece.reshape((4,) + shp)
        out[nm] = jnp.concatenate([piece[s] for s in range(4)], axis=ax)
    return out


def _pack_grads(grads):
    slots = []
    for s in range(4):
        parts = []
        for nm, shp, ax, _ in _SHARDED:
            n = shp[ax]
            parts.append(lax.slice_in_dim(grads[nm], s * n, (s + 1) * n, axis=ax).reshape(-1))
        flat = jnp.concatenate(parts)
        slots.append(jnp.pad(flat, (0, COMM_ROWS * COMM_W - flat.shape[0])))
    return jnp.stack(slots).reshape(4, COMM_ROWS, COMM_W)


def _unpack_grads(r):
    flat = r.reshape(-1)
    out, off = {}, 0
    for nm, shp, _, _ in _SHARDED:
        n = int(np.prod(shp))
        out[nm] = flat[off:off + n].reshape(shp)
        off += n
    return out


def _regroup_w_in(w):
    z = lambda n: jnp.zeros((w.shape[0], n), w.dtype)
    return jnp.concatenate([w[:, 0:3072], w[:, 3072:4096], w[:, 5392:6416], w[:, 6416:7440], w[:, 4112:5136],
                            w[:, 4096:4112], z(112), w[:, 5136:5264], w[:, 5264:5392], z(128)], axis=1)


def _ungroup_w_in(g):
    return jnp.concatenate([g[:, 0:3072], g[:, 3072:4096], g[:, 7168:7184], g[:, 6144:7168], g[:, 7296:7424], g[:, 7424:7552],
                            g[:, 4096:5120], g[:, 5120:6144]], axis=1)


def _pad_lanes(v, n=LANE):
    return jnp.pad(v, (0, n - v.shape[0])).reshape(1, n)


def _layer_consts(P):
    K = {}
    K["norm_mix"] = P["norm_mix"].reshape(1, D)
    K["norm_ffn"] = P["norm_ffn"].reshape(1, D)
    K["alog"] = _pad_lanes(P["dn_a_log"])
    K["dtb"] = _pad_lanes(P["dn_dt_bias"])
    K["dn_norm"] = P["dn_norm"].reshape(1, LANE)
    K["qn"] = jnp.tile(P["swa_q_norm"], 16).reshape(1, D)
    K["kn"] = jnp.tile(P["swa_k_norm"], 2).reshape(1, LANE)
    K["sinks"] = _pad_lanes(P["swa_sinks"])
    K["ffn_b"] = P["ffn_conv_b"].reshape(1, D_FF)
    return K


def _layer_fwd(x, mod, W, K, tabs, bd, hosted=None, late_ffn=None):
    sh1, sc1, gt1, sh2, sc2, gt2 = mod
    S = {"x": x}
    h1, h1t = _rowwise_fwd("normmod1_fwd", _normmod_fn, [(x, 0, D)], [K["norm_mix"], sc1, sh1], [D], [bf16], also_transposed=True)
    proj = _matmul("proj_fwd", h1, W["w_in"], "nn", f32)
    qn = _dnconv_fwd("dnconv_q_fwd", proj, CB_Q, W["dn_conv"], True)
    kn = _dnconv_fwd("dnconv_k_fwd", proj, CB_K, W["dn_conv"], True)
    vc = _dnconv_fwd("dnconv_v_fwd", proj, CB_V, W["dn_conv"], False)
    o, sall, xinv, hosted_out = _gdn_fwd("gdn_fwd", qn, kn, vc, proj, K["alog"], K["dtb"], hosted=hosted)
    (on,) = _rowwise_fwd("dngate_fwd", _dngate_fn, [(o, 0, LANE), (proj, 8 * WB_Z, LANE)], [K["dn_norm"]], [LANE], [bf16], nc=8,
                         tr=_pick(x.shape[0], (1024,)))
    ya = _matmul("dnout_fwd", on, W["w_dn_out"], "nn", f32)
    sq = _qkprep_fwd("qprep_fwd", proj, WB_SWQ, D, K["qn"], bd[0], tabs[0])
    sk = _qkprep_fwd("kprep_fwd", proj, CB_SWK, LANE, K["kn"], bd[1], tabs[1])
    attn, ffn_arrived = _attn_fwd("attn_fwd", sq, sk, proj, K["sinks"], K["attn_bias"], hosted=None if late_ffn is None else late_ffn[0])
    if late_ffn is not None:
        W.update(late_ffn[1](ffn_arrived))
    yb = _matmul("swaout_fwd", attn, W["w_swa_out"], "nn", f32)
    (merged,) = _rowwise_fwd("merge_fwd", _merge_fn, [(proj, WB_GA, D), (proj, WB_GB, D), (ya, 0, D), (yb, 0, D)], [], [D], [bf16])
    t1, x1 = _matmul("wo_fwd", merged, W["w_o"], "nn", f32, resid=(x, gt1))
    h2, h2t = _rowwise_fwd("normmod2_fwd", _normmod_fn, [(x1, 0, D)], [K["norm_ffn"], sc2, sh2], [D], [bf16], also_transposed=True)
    up = _matmul("up_fwd", h2, W["w_up"], "nn", f32)
    mid = _ffnact_fwd("ffnact_fwd", up, W["ffn_conv"], K["ffn_b"])
    t2, x2 = _matmul("down_fwd", mid, W["w_down"], "nn", f32, resid=(x1, gt2))
    S.update(h1t=h1t, h2t=h2t, proj=proj, qn=qn, kn=kn, vc=vc, o=o, sall=sall, xinv=xinv, on=on, ya=ya, sq=sq, sk=sk, attn=attn, yb=yb,
             merged=merged, t1=t1, x1=x1, h2=h2, up=up, mid=mid, t2=t2)
    return x2, S, hosted_out


def _layer_bwd(dx2, S, mod, W, K, tabs, bd, carry=None, early=None):
    sh1, sc1, gt1, sh2, sc2, gt2 = mod
    x, x1, proj, up = S["x"], S["x1"], S["proj"], S["up"]
    T = x.shape[0]
    gw, gs = {}, {}
    dt2, dgt2 = _rowwise_bwd("resid2_bwd", _resid_fn, [(x1, 0, D), (S["t2"], 0, D)], [gt2], [(dx2, 0, D)], [None, bf16])
    dmid = _matmul("down_bwd_x", dt2, W["w_down"], "nt", bf16)
    gw["w_down"] = _matmul("down_bwd_w", S["mid"], dt2, "tn", f32)
    dact, dlin, gw["ffn_conv"], dffn_b = _ffnact_bwd("ffnact_bwd", up, W["ffn_conv"], K["ffn_b"], dmid)
    dup = (dact, dlin)
    dh2 = _matmul("up_bwd_x", dup, W["w_up"], "nt", f32)
    if carry is None:
        gw["w_up"] = _matmul("up_bwd_w", S["h2t"], dup, "nn", f32, out_slots=4)
        pair_sums = hosted = None
    else:
        gw["w_up"], b1 = _matmul("up_bwd_w", S["h2t"], dup, "nn", f32, out_slots=4, hosted=_hosted_swap(carry[0]))
        pair_sums = [_pair_add("rs_pair_%d" % i, g, b, carry[1]) for i, (g, b) in enumerate(zip(carry[0], b1))]
        hosted = _hosted_scatter(pair_sums)
    dx1, dnorm_ffn, dsc2, dsh2 = _rowwise_bwd("normmod2_bwd", _normmod_fn, [(x1, 0, D)], [K["norm_ffn"], sc2, sh2], [(dh2, 0, D)], [f32],
                                              add_to_first=(dx2, 0, D))
    early_out = None
    if early is not None:
        ffn_slots = [gw["w_up"], gw["w_down"].reshape(4, D_FF // 4, D)]
        ffn_ps = [_pair_add("rs_pair_ffn_%d" % i, g, b, early[0]) for i, (g, b) in enumerate(zip(ffn_slots, _swap_halves_multi("rs_swap_ffn", ffn_slots)))]
    dt1, dgt1 = _rowwise_bwd("resid1_bwd", _resid_fn, [(x, 0, D), (S["t1"], 0, D)], [gt1], [(dx1, 0, D)], [None, bf16])
    dmerged = _matmul("wo_bwd_x", dt1, W["w_o"], "nt", f32)
    gw["w_o"] = _matmul("wo_bwd_w", S["merged"], dt1, "tn", f32)
    dproj, dya, dyb = _rowwise_bwd("merge_bwd", _merge_fn, [(proj, WB_GA, D), (proj, WB_GB, D), (S["ya"], 0, D), (S["yb"], 0, D)], [],
                                   [(dmerged, 0, D)], [bf16, bf16, bf16, bf16], dest=(None, (0, 1), WB_GA // 2))
    don = _matmul("dnout_bwd_x", dya, W["w_dn_out"], "nt", f32)
    gw["w_dn_out"] = _matmul("dnout_bwd_w", S["on"], dya, "tn", f32)
    dproj, do, ddn_norm = _rowwise_bwd("dngate_bwd", _dngate_fn, [(S["o"], 0, LANE), (proj, 8 * WB_Z, LANE)], [K["dn_norm"]], [(don, 0, LANE)],
                                       [f32, bf16], nc=8, tr=_pick(T, (1024,)), dest=(dproj, (1,), 8 * WB_Z))
    (dqn, dkn, dvc, dab, dalog, ddtb), hosted_out = _gdn_bwd("gdn_bwd", S["qn"], S["kn"], S["vc"], proj, K["alog"], K["dtb"], S["sall"], S["xinv"], do,
                                                            hosted=hosted)
    dproj, dwq = _dnconv_bwd("dnconv_q_bwd", proj, CB_Q, W["dn_conv"], dqn, True, dproj)
    dproj, dwk = _dnconv_bwd("dnconv_k_bwd", proj, CB_K, W["dn_conv"], dkn, True, dproj)
    dproj, dwv = _dnconv_bwd("dnconv_v_bwd", proj, CB_V, W["dn_conv"], dvc, False, dproj)
    gw["dn_conv"] = jnp.concatenate([dwq, dwk, dwv], axis=1)
    dattn = _matmul("swaout_bwd_x", dyb, W["w_swa_out"], "nt", f32)
    gw["w_swa_out"] = _matmul("swaout_bwd_w", S["attn"], dyb, "tn", f32)
    (dsq, dkp, dkc, dvp, dvc_, dsinks), ffn_b2 = _attn_bwd("attn_bwd", S["sq"], S["sk"], proj, K["sinks"], K["attn_bias"], dattn,
                                                          hosted=None if early is None else _hosted_scatter(ffn_ps))
    if early is not None:
        early_out = _rs_end(ffn_ps, ffn_b2, early[0], early[1])
    dsk = _shift_add("attn_dk_join", dkc, dkp, f32)
    dswv = _shift_add("attn_dv_join", dvc_, dvp, bf16)
    dproj, dqn_w = _qkprep_bwd("qprep_bwd", proj, WB_SWQ, D, K["qn"], bd[0], tabs[0], dsq, dest_buf=dproj)
    dswk, dkn_w = _qkprep_bwd("kprep_bwd", proj, CB_SWK, LANE, K["kn"], bd[1], tabs[1], dsk)
    tail = jnp.concatenate([dab.astype(bf16), dswk, dswv, jnp.zeros((T, LANE), bf16)], axis=1)
    dproj = lax.dynamic_update_slice(dproj, tail, (0, CB_AB * LANE))
    dh1 = _matmul("proj_bwd_x", dproj, W["w_in"], "nt", f32)
    gw["w_in"] = _matmul("proj_bwd_w", S["h1t"], dproj, "nn", f32)
    dx, dnorm_mix, dsc1, dsh1 = _rowwise_bwd("normmod1_bwd", _normmod_fn, [(x, 0, D)], [K["norm_mix"], sc1, sh1], [(dh1, 0, D)], [f32],
                                             add_to_first=(dx1, 0, D))
    gs = {"norm_mix": dnorm_mix[0], "dn_a_log": dalog[0, :8], "dn_dt_bias": ddtb[0, :8], "dn_norm": ddn_norm[0],
          "swa_q_norm": dqn_w.reshape(16, 64).sum(0), "swa_k_norm": dkn_w.reshape(2, 64).sum(0), "swa_sinks": dsinks[0, :16],
          "norm_ffn": dnorm_ffn[0], "ffn_conv_b": dffn_b[0]}
    dmod = jnp.concatenate([dsh1, dsc1, dgt1, dsh2, dsc2, dgt2], axis=1)
    return dx, gw, gs, dmod, (pair_sums, hosted_out), early_out


def _rope_tables(pos):
    T = pos.shape[0]
    half = 8
    inv = jnp.power(ROPE_THETA, -jnp.arange(half, dtype=f32) / half)
    ang = pos.astype(f32)[:, None] * inv
    cos, sin = jnp.cos(ang), jnp.sin(ang)
    z8, z48, o48 = jnp.zeros((T, 8), f32), jnp.zeros((T, 48), f32), jnp.ones((T, 48), f32)
    c64 = jnp.concatenate([cos, cos, o48], axis=1)
    s1 = jnp.concatenate([-sin, z8, z48], axis=1)
    s2 = jnp.concatenate([z8, sin, z48], axis=1)
    return tuple(jnp.tile(t, (1, 2)) for t in (c64, s1, s2))


_SMALL = (("norm_mix", D), ("dn_a_log", 8), ("dn_dt_bias", 8), ("dn_norm", 128), ("swa_q_norm", 64), ("swa_k_norm", 64),
          ("swa_sinks", 16), ("norm_ffn", D), ("ffn_conv_b", D_FF), ("b_ada", 6 * D))
_CONV = (("dn_conv", 4 * 3072), ("ffn_conv", 3 * D_FF))
_CONV_SHARD = (("dn_conv", 4 * 768), ("ffn_conv", 3 * 704))


def _pack_small(vals, spec):
    flat = jnp.concatenate([vals[nm].reshape(-1) for nm, _ in spec])
    rows = -(-flat.shape[0] // (8 * LANE)) * 8
    return jnp.pad(flat, (0, rows * LANE - flat.shape[0])).reshape(rows, LANE)


def _unpack_small(buf, spec):
    flat = buf.reshape(-1)
    out, off = {}, 0
    for nm, n in spec:
        out[nm] = flat[off:off + DEPTH * n].reshape(DEPTH, n)
        off += DEPTH * n
    return out


def kernel(x, c, positions, w_ada, b_ada, norm_mix, w_in, dn_conv, dn_a_log, dn_dt_bias, dn_norm, w_dn_out, swa_q_norm, swa_k_norm, swa_sinks, w_swa_out, w_o, norm_ffn, w_up, ffn_conv, ffn_conv_b, w_down, loss_target, m_w_ada, m_b_ada, m_norm_mix, m_w_in, m_dn_conv, m_dn_a_log, m_dn_dt_bias, m_dn_norm, m_w_dn_out, m_swa_q_norm, m_swa_k_norm, m_swa_sinks, m_w_swa_out, m_w_o, m_norm_ffn, m_w_up, m_ffn_conv, m_ffn_conv_b, m_w_down, v_w_ada, v_b_ada, v_norm_mix, v_w_in, v_dn_conv, v_dn_a_log, v_dn_dt_bias, v_dn_norm, v_w_dn_out, v_swa_q_norm, v_swa_k_norm, v_swa_sinks, v_w_swa_out, v_w_o, v_norm_ffn, v_w_up, v_ffn_conv, v_ffn_conv_b, v_w_down):
    weights = dict(w_ada=w_ada, b_ada=b_ada, norm_mix=norm_mix, w_in=w_in, dn_conv=dn_conv, dn_a_log=dn_a_log, dn_dt_bias=dn_dt_bias,
                   dn_norm=dn_norm, w_dn_out=w_dn_out, swa_q_norm=swa_q_norm, swa_k_norm=swa_k_norm, swa_sinks=swa_sinks,
                   w_swa_out=w_swa_out, w_o=w_o, norm_ffn=norm_ffn, w_up=w_up, ffn_conv=ffn_conv, ffn_conv_b=ffn_conv_b, w_down=w_down)
    mom_m = dict(w_ada=m_w_ada, b_ada=m_b_ada, norm_mix=m_norm_mix, w_in=m_w_in, dn_conv=m_dn_conv, dn_a_log=m_dn_a_log,
                 dn_dt_bias=m_dn_dt_bias, dn_norm=m_dn_norm, w_dn_out=m_w_dn_out, swa_q_norm=m_swa_q_norm, swa_k_norm=m_swa_k_norm,
                 swa_sinks=m_swa_sinks, w_swa_out=m_w_swa_out, w_o=m_w_o, norm_ffn=m_norm_ffn, w_up=m_w_up, ffn_conv=m_ffn_conv,
                 ffn_conv_b=m_ffn_conv_b, w_down=m_w_down)
    mom_v = dict(w_ada=v_w_ada, b_ada=v_b_ada, norm_mix=v_norm_mix, w_in=v_w_in, dn_conv=v_dn_conv, dn_a_log=v_dn_a_log,
                 dn_dt_bias=v_dn_dt_bias, dn_norm=v_dn_norm, w_dn_out=v_w_dn_out, swa_q_norm=v_swa_q_norm, swa_k_norm=v_swa_k_norm,
                 swa_sinks=v_swa_sinks, w_swa_out=v_w_swa_out, w_o=v_w_o, norm_ffn=v_norm_ffn, w_up=v_w_up, ffn_conv=v_ffn_conv,
                 ffn_conv_b=v_ffn_conv_b, w_down=v_w_down)
    order = ["w_ada", "b_ada", "norm_mix", "w_in", "dn_conv", "dn_a_log", "dn_dt_bias", "dn_norm", "w_dn_out", "swa_q_norm",
             "swa_k_norm", "swa_sinks", "w_swa_out", "w_o", "norm_ffn", "w_up", "ffn_conv", "ffn_conv_b", "w_down"]
    ax, ay, ac = lax.axis_index("x"), lax.axis_index("y"), lax.axis_index("c")
    chip = 2 * ax + ay
    dev = 4 * ax + 2 * ay + ac
    T = x.shape[1]
    xs = x[0]

    c_all = _allgather_all("gather_c", jnp.pad(c, ((0, 7), (0, 0)))).reshape(8, 8, D)[:, 0]
    c_act = _silu_rows("silu_c", jnp.pad(c_all, ((0, 8), (0, 0))))
    mod_sh = jnp.stack([
        _matmul("mod_fwd", c_act, w_ada[l].astype(bf16), "nn", f32,
                bias=lax.dynamic_slice(b_ada[l], (chip * 1536,), (1536,)).reshape(1, 1536)) for l in range(DEPTH)])
    mod_all = _allgather_all("gather_mod", mod_sh.reshape(DEPTH * 16 * 12, LANE)).reshape(8, DEPTH, 16, 1536)
    mod_me = jnp.concatenate([lax.dynamic_index_in_dim(mod_all[2 * s], dev, axis=1, keepdims=False) for s in range(4)], axis=1)

    tabs_q = _rope_tables(positions[0])
    tabs = (tabs_q, tabs_q)
    head = jnp.arange(LANE) // 64
    bd128 = (head[:, None] == head[None, :]).astype(f32) / 64.0
    bd = (bd128, bd128)
    attn_bias = _attn_bias()

    conv_all = _allgather_all("gather_conv", _pack_small({"dn_conv": dn_conv, "ffn_conv": ffn_conv}, _CONV_SHARD))
    conv_parts = [_unpack_small(conv_all[2 * s], _CONV_SHARD) for s in range(4)]
    dn_conv_full = jnp.concatenate([p["dn_conv"].reshape(DEPTH, 4, 768) for p in conv_parts], axis=2)
    ffn_conv_full = jnp.concatenate([p["ffn_conv"].reshape(DEPTH, 3, 704) for p in conv_parts], axis=2)

    saved, Ws, Ks, mods = [], [], [], []
    h = xs
    shards = [[weights[nm][l].astype(bf16) for nm in _BIG] for l in range(DEPTH)]
    own = lambda gs, ss: [lax.dynamic_update_index_in_dim(g, s, chip, 0) for g, s in zip(gs, ss)]
    gathered = _gather_chips("gather_w", shards[0][:4])
    for l in range(DEPTH):
        gathered = own(gathered, shards[l])
        W = _assemble_mixer(gathered[:4])
        late_ffn = None
        if l == 0:
            late_ffn = (_hosted_gather(shards[0][4:]), lambda arrived: _assemble_ffn(own(_gather_forward("gather_ffn_pass", arrived), shards[0][4:])))
        else:
            W.update(_assemble_ffn(gathered[4:]))
        W["dn_conv"], W["ffn_conv"] = dn_conv_full[l], ffn_conv_full[l]
        K = _layer_consts({nm: weights[nm][l] for nm in ("norm_mix", "norm_ffn", "dn_a_log", "dn_dt_bias", "dn_norm", "swa_q_norm",
                                                          "swa_k_norm", "swa_sinks", "ffn_conv_b")})
        K["attn_bias"] = attn_bias
        mod = tuple(mod_me[l, k * D:(k + 1) * D].reshape(1, D) for k in range(6))
        nxt = _hosted_gather(shards[l + 1]) if l + 1 < DEPTH else None
        h, S, arrived = _layer_fwd(h, mod, W, K, tabs, bd, hosted=nxt, late_ffn=late_ffn)
        if nxt is not None:
            gathered = _gather_forward("gather_w_pass", arrived)
        saved.append(S), Ws.append(W), Ks.append(K), mods.append(mod)

    loss_blk, dh = _loss("loss", h, loss_target[0])
    loss = lax.psum(loss_blk[0, 0], ("x", "y", "c"))

    grad_sh = [None] * DEPTH
    small = [None] * DEPTH
    dmods = [None] * DEPTH
    slots = None
    for l in reversed(range(DEPTH)):
        dh, gw, gs, dmod, (ps, b2), ffn0 = _layer_bwd(dh, saved[l], mods[l], Ws[l], Ks[l], tabs, bd, carry=None if slots is None else (slots, ac),
                                                      early=(ac, chip) if l == 0 else None)
        if slots is not None:
            grad_sh[l + 1] = dict(zip(_BIG, _rs_end(ps, b2, ac, chip)))
        slots = _grad_slots(gw)
        small[l], dmods[l] = dict(gs, dn_conv=gw["dn_conv"], ffn_conv=gw["ffn_conv"]), dmod[0]
    grad_sh[0] = dict(zip(_BIG, _reduce_scatter_multi(slots[:4], ac, chip) + ffn0))

    spec_g = _SMALL + _CONV
    vals = {nm: jnp.stack([small[l][nm] for l in range(DEPTH)]) for nm, _ in spec_g if nm != "b_ada"}
    vals["b_ada"] = jnp.stack(dmods)
    small_all = _allgather_all("gather_small", _pack_small(vals, spec_g))
    g_small = _unpack_small(_sum_leading("sum_small", small_all), spec_g)
    dmod_all = jnp.stack([_unpack_small(small_all[d], spec_g)["b_ada"] for d in range(8)])
    dmod_sh = lax.dynamic_slice(dmod_all, (0, 0, chip * 1536), (8, DEPTH, 1536))
    dmod_sh = jnp.pad(dmod_sh, ((0, 8), (0, 0), (0, 0))).astype(bf16)
    g_w_ada = jnp.stack([_matmul("mod_bwd_w", c_act, dmod_sh[:, l], "tn", f32) for l in range(DEPTH)])

    grads = {nm: g_small[nm] for nm, _ in _SMALL}
    grads["dn_conv"] = lax.dynamic_slice(g_small["dn_conv"].reshape(DEPTH, 4, 3072), (0, 0, chip * 768), (DEPTH, 4, 768))
    grads["ffn_conv"] = lax.dynamic_slice(g_small["ffn_conv"].reshape(DEPTH, 3, D_FF), (0, 0, chip * 704), (DEPTH, 3, 704))
    grads["w_ada"] = g_w_ada
    for nm in _BIG:
        grads[nm] = jnp.stack([grad_sh[l][nm] for l in range(DEPTH)])

    delta, new_m, new_v = {}, {}, {}
    for nm in ("w_ada", "dn_conv", "ffn_conv") + _BIG:
        delta[nm], new_m[nm], new_v[nm] = _adamw("adamw_" + nm, weights[nm], grads[nm], mom_m[nm], mom_v[nm])
    sm = [_pack_small({nm: t[nm] for nm, _ in _SMALL}, _SMALL) for t in (weights, grads, mom_m, mom_v)]
    for tgt, buf in zip((delta, new_m, new_v), _adamw("adamw_small", *sm)):
        tgt.update(_unpack_small(buf, _SMALL))

    return (loss, dh[None], *[grads[n] for n in order], *[delta[n] for n in order], *[new_m[n] for n in order], *[new_v[n] for n in order])
```

```python
import functools

import jax
import jax.numpy as jnp
import numpy as np
from jax import lax
from jax.experimental import pallas as pl
from jax.experimental.pallas import tpu as pltpu

f32 = jnp.float32
bf16 = jnp.bfloat16
SDS = jax.ShapeDtypeStruct
HI = lax.Precision.HIGHEST
MESH = pl.DeviceIdType.MESH

D = 1024
DEPTH = 4
EPS = 1e-6
DN_C = 64
SWA_B = 128
LANE = 128
ROPE_THETA = 500000.0
D_FF = 2816
IN_TOTAL = 7440
PROJ_W = 7680
CB_Q, CB_K, CB_V = 0, 8, 16
CB_AB, CB_SWK, CB_SWV = 56, 57, 58
WB_Z, WB_GA, WB_GB, WB_SWQ = 3, 4, 5, 6
TR = 256
COMM_W = 1024
COMM_ROWS = 4864
COMM_TR = 128
VMEM_BIG = 48 * 2 ** 20

ADAM_LR, ADAM_B1, ADAM_B2, ADAM_EPS, ADAM_WD, ADAM_STEP = 0.001, 0.9, 0.999, 1e-08, 0.01, 10


def _pcall(body, **kw):
    return pl.pallas_call(body, **kw)


def _cparams(vmem=None):
    return pltpu.CompilerParams(vmem_limit_bytes=vmem) if vmem else None


def _dot(a, b, ca, cb, precision=HI):
    return lax.dot_general(a, b, (((ca,), (cb,)), ((), ())), precision=precision, preferred_element_type=f32)


def _pick(n, cands):
    for c in cands:
        if n % c == 0:
            return c
    return n


def _tile(n, cap):
    if n <= cap:
        return n
    best = None
    for t in range(LANE, cap + 1, LANE):
        if n % t == 0:
            best = t
    assert best is not None, (n, cap)
    return best


def _matmul(name, a, b, mode, out_dtype, bias=None, out_slots=None, hosted=None, resid=None, b_slots=False):
    h_args, h_specs, h_shapes, h_sems = _hosted_parts(hosted)
    nh = len(h_args)
    a_parts = list(a) if isinstance(a, (tuple, list)) else [a]
    b_parts = list(b) if isinstance(b, (tuple, list)) else [b]
    na, nbp = len(a_parts), len(b_parts)
    assert (na == 1 or mode == "nt") and (nbp == 1 or mode == "nn")
    if b_slots:
        S_, rows_, cols_ = b.shape
        M = a_parts[0].shape[0]
        K, N = (rows_, S_ * cols_) if mode == "nn" else (S_ * cols_, rows_)
    elif mode == "nn":
        M, K = a_parts[0].shape
        N = b_parts[0].shape[1] * nbp
    elif mode == "nt":
        M, K = a_parts[0].shape[0], a_parts[0].shape[1] * na
        N = b_parts[0].shape[0]
    else:
        (K, M), N = a_parts[0].shape, b_parts[0].shape[1]
    tm = _tile(M, 1536 if mode == "tn" else 1024)
    tn = N // out_slots if out_slots else _tile(N, 1536)
    tk = _tile(K, 512 if mode == "tn" else (1024 if K <= 1024 else 1536))
    nk, nj = K // tk, N // tn
    ka, jb = nk // na, nj // nbp
    assert nk % na == 0 and nj % nbp == 0
    ca, cb = {"nn": (1, 0), "nt": (1, 1), "tn": (0, 0)}[mode]
    grid = (M // tm, nj, nk)
    n_bias, n_res = (1 if bias is not None else 0), (2 if resid is not None else 0)

    def body(*refs):
        a_refs, b_refs = refs[:na], refs[na:na + nbp]
        p = na + nbp
        bias_ref = refs[p] if n_bias else None
        res_refs = refs[p + n_bias:p + n_bias + n_res]
        p += n_bias + n_res
        h_ins, o_ref = refs[p:p + nh], refs[p + nh]
        xo_ref = refs[p + nh + 1] if n_res else None
        p += nh + 1 + (1 if n_res else 0)
        h_outs, rest = refs[p:p + nh], refs[p + nh:]
        acc, sems = (rest[0], rest[1:]) if nk > 1 else (None, rest)
        j, k = pl.program_id(1), pl.program_id(2)
        step = (pl.program_id(0) * grid[1] + j) * grid[2] + k
        _hosted_edge(hosted, "start", h_ins, h_outs, sems, step == 0)

        def finish(r):
            if bias is not None:
                r = r + bias_ref[...]
            o_ref[...] = r.astype(o_ref.dtype)
            if n_res:
                xo_ref[...] = res_refs[0][...] + res_refs[1][...] * r

        a_tile = a_refs[0][...] if na == 1 else jnp.where(k < ka, a_refs[0][...], a_refs[1][...])
        b_tile = b_refs[0][...] if nbp == 1 else jnp.where(j < jb, b_refs[0][...], b_refs[1][...])
        part = _dot(a_tile.astype(bf16), b_tile.astype(bf16), ca, cb, precision=None)
        if nk == 1:
            finish(part)
        else:
            @pl.when(k == 0)
            def _():
                acc[...] = part

            @pl.when(k > 0)
            def _():
                acc[...] += part

            @pl.when(k == nk - 1)
            def _():
                finish(acc[...])

        _hosted_edge(hosted, "finish", h_ins, h_outs, sems, step == grid[0] * grid[1] * grid[2] - 1)

    if mode == "tn":
        a_specs = [pl.BlockSpec((tk, tm), lambda i, j, k: (k, i))]
    else:
        a_specs = [pl.BlockSpec((tm, tk), lambda i, j, k, q=q: (i, jnp.clip(k - q * ka, 0, ka - 1))) for q in range(na)]
    if b_slots:
        assert (tn if mode == "nn" else tk) == b.shape[2]
        b_specs = [pl.BlockSpec((None, tk, tn), lambda i, j, k: (j, k, 0)) if mode == "nn" else pl.BlockSpec((None, tn, tk), lambda i, j, k: (k, j, 0))]
    elif mode == "nt":
        b_specs = [pl.BlockSpec((tn, tk), lambda i, j, k: (j, k))]
    else:
        b_specs = [pl.BlockSpec((tk, tn), lambda i, j, k, q=q: (k, jnp.clip(j - q * jb, 0, jb - 1))) for q in range(nbp)]
    in_specs, args = a_specs + b_specs, a_parts + b_parts
    if bias is not None:
        in_specs.append(pl.BlockSpec((1, tn), lambda i, j, k: (0, j)))
        args.append(bias)
    if out_slots:
        out_specs, out_shape = [pl.BlockSpec((None, tm, tn), lambda i, j, k: (j, i, 0))], [SDS((out_slots, M, tn), out_dtype)]
    else:
        out_specs, out_shape = [pl.BlockSpec((tm, tn), lambda i, j, k: (i, j))], [SDS((M, N), out_dtype)]
    if resid is not None:
        in_specs += [pl.BlockSpec((tm, tn), lambda i, j, k: (i, j)), pl.BlockSpec((1, tn), lambda i, j, k: (0, j))]
        args += list(resid)
        out_specs.append(pl.BlockSpec((tm, tn), lambda i, j, k: (i, j)))
        out_shape.append(SDS((M, N), f32))
    outs = _pcall(
        body, grid=grid, in_specs=in_specs + h_specs, out_specs=out_specs + h_specs, out_shape=out_shape + h_shapes,
        scratch_shapes=([pltpu.VMEM((tm, tn), f32)] if nk > 1 else []) + h_sems, compiler_params=_cparams(VMEM_BIG), name=name)(*args, *h_args)
    n_main = len(out_shape)
    res = list(outs[:n_main]) + ([list(outs[n_main:])] if hosted is not None else [])
    return res[0] if len(res) == 1 else tuple(res)


def _row_specs(rows, tr):
    return [pl.BlockSpec((tr, w), lambda i, j, off=off: (i, off + j)) for (_, off, w) in rows]


def _rowwise_fwd(name, fn, rows, vecs, out_widths, out_dtypes, nc=1, tr=TR, also_transposed=False):
    T = rows[0][0].shape[0]
    n_in = len(rows) + len(vecs)
    n_out = len(out_widths)

    def body(*refs):
        vals = [r[...].astype(f32) for r in refs[:n_in]]
        res = fn(*vals)
        for o_ref, r in zip(refs[n_in:n_in + n_out], res):
            o_ref[...] = r.astype(o_ref.dtype)
        if also_transposed:
            refs[n_in + n_out][...] = res[0].T.astype(refs[n_in + n_out].dtype)

    in_specs = _row_specs(rows, tr) + [pl.BlockSpec(v.shape, lambda i, j: (0, 0)) for v in vecs]
    out_specs = [pl.BlockSpec((tr, w), lambda i, j: (i, j)) for w in out_widths]
    out_shape = [SDS((T, w * nc), dt) for w, dt in zip(out_widths, out_dtypes)]
    if also_transposed:
        out_specs.append(pl.BlockSpec((out_widths[0], tr), lambda i, j: (j, i)))
        out_shape.append(SDS((out_widths[0] * nc, T), out_dtypes[0]))
    return _pcall(body, grid=(T // tr, nc), in_specs=in_specs, out_specs=out_specs, out_shape=out_shape, name=name)(
        *[r[0] for r in rows], *vecs)


def _into_buffer(dest_buf, n_inputs, out_index):
    if dest_buf is None:
        return [], [], {}
    return [dest_buf], [pl.BlockSpec(memory_space=pl.ANY)], {n_inputs: out_index}


def _rowwise_bwd(name, fn, rows, vecs, cts, drow_dtypes, nc=1, tr=TR, add_to_first=None, dest=None):
    T = rows[0][0].shape[0]
    n_r, n_v, n_c = len(rows), len(vecs), len(cts)
    n_add = 0 if add_to_first is None else 1
    keep = [k for k, dt in enumerate(drow_dtypes) if dt is not None]
    members = [] if dest is None else list(dest[1])
    plain = [pos for pos in range(len(keep)) if pos not in members]
    n_dest = 1 if dest is not None else 0
    n_in = n_r + n_v + n_c + n_add

    def body(*refs):
        vals = [r[...].astype(f32) for r in refs[:n_in]]
        outs = refs[n_in + (1 if dest is not None and dest[0] is not None else 0):]
        i, j = pl.program_id(0), pl.program_id(1)
        _, vjp = jax.vjp(fn, *vals[:n_r + n_v])
        grads = vjp(tuple(vals[n_r + n_v:n_r + n_v + n_c]))
        got = []
        for pos, k in enumerate(keep):
            g = grads[k]
            if n_add and pos == 0:
                g = g + vals[n_in - 1]
            got.append(g)
        if dest is not None:
            outs[0][...] = jnp.concatenate([got[pos] for pos in members], axis=1).astype(outs[0].dtype)
        for q, pos in enumerate(plain):
            outs[n_dest + q][...] = got[pos].astype(outs[n_dest + q].dtype)
        vec_outs = outs[n_dest + len(plain):]

        @pl.when((i == 0) & (j == 0))
        def _():
            for q in range(n_v):
                vec_outs[q][...] = jnp.zeros_like(vec_outs[q])

        for q in range(n_v):
            vec_outs[q][...] += grads[n_r + q]

    extra = [] if add_to_first is None else [add_to_first]
    in_specs = (_row_specs(rows, tr) + [pl.BlockSpec(v.shape, lambda i, j: (0, 0)) for v in vecs]
                + _row_specs(cts, tr) + _row_specs(extra, tr))
    args = [r[0] for r in rows] + list(vecs) + [c[0] for c in cts] + [e[0] for e in extra]
    out_specs, out_shape, aliases = [], [], {}
    if dest is not None:
        width = sum(rows[keep[pos]][2] for pos in members)
        col = dest[2]
        b_args, b_specs, aliases = _into_buffer(dest[0], n_in, 0)
        args, in_specs = args + b_args, in_specs + b_specs
        out_specs.append(pl.BlockSpec((tr, width), lambda i, j: (i, col + j)))
        out_shape.append(SDS((T, PROJ_W), bf16))
    out_specs += [pl.BlockSpec((tr, rows[keep[pos]][2]), lambda i, j: (i, j)) for pos in plain]
    out_shape += [SDS((T, rows[keep[pos]][2] * nc), drow_dtypes[keep[pos]]) for pos in plain]
    out_specs += [pl.BlockSpec(v.shape, lambda i, j: (0, 0)) for v in vecs]
    out_shape += [SDS(v.shape, f32) for v in vecs]
    return _pcall(body, grid=(T // tr, nc), in_specs=in_specs, out_specs=out_specs, out_shape=out_shape,
                  input_output_aliases=aliases, name=name)(*args)


def _normmod_fn(x, w, sc, sh):
    y = x * lax.rsqrt(jnp.mean(x * x, axis=-1, keepdims=True) + EPS)
    return ((y * w) * (1.0 + sc) + sh,)


def _resid_fn(x, t, gt):
    return (x + gt * t,)


def _merge_fn(ga, gb, ya, yb):
    return (jax.nn.sigmoid(ga) * ya + jax.nn.sigmoid(gb) * yb,)


def _dngate_fn(o, z, w):
    y = o * lax.rsqrt(jnp.mean(o * o, axis=-1, keepdims=True) + EPS)
    return ((y * w) * (z * jax.nn.sigmoid(z)),)


def _conv_taps(x, w_ref, taps, buf):
    w = lambda s: w_ref[taps - 1 - s:taps - s, :]
    row = lax.broadcasted_iota(jnp.int32, (8, x.shape[1]), 0)
    x8 = x[0:8]
    acc, acc8 = x * w(0), x8 * w(0)
    for s in range(1, taps):
        acc = acc + pltpu.roll(x, s, 0) * w(s)
        acc8 = acc8 + jnp.where(row >= s, pltpu.roll(x8, s, 0), 0.0) * w(s)
    buf[...] = acc
    buf[0:8, :] = acc8
    return buf[...]


def _conv_taps_bwd(x, dy, w_ref, dw_ref, taps, buf):
    T = x.shape[0]
    w = lambda s: w_ref[taps - 1 - s:taps - s, :]
    row = lax.broadcasted_iota(jnp.int32, (8, x.shape[1]), 0)
    dy_first, dy_last = dy[0:8], dy[T - 8:T]
    dx, dx_last = dy * w(0), dy_last * w(0)
    dw_ref[taps - 1:taps, :] = jnp.sum(dy * x, axis=0, keepdims=True)
    for s in range(1, taps):
        dx = dx + pltpu.roll(dy, T - s, 0) * w(s)
        dx_last = dx_last + jnp.where(row < 8 - s, pltpu.roll(dy_last, 8 - s, 0), 0.0) * w(s)
        xr = pltpu.roll(x, s, 0)
        wrapped = jnp.sum(jnp.where(row < s, dy_first * xr[0:8], 0.0), axis=0, keepdims=True)
        dw_ref[taps - 1 - s:taps - s, :] = jnp.sum(dy * xr, axis=0, keepdims=True) - wrapped
    buf[...] = dx
    buf[T - 8:T, :] = dx_last
    return buf[...]


def _dn_act(y, normalize):
    s = y * jax.nn.sigmoid(y)
    if normalize:
        s = s * lax.rsqrt(jnp.sum(s * s, axis=-1, keepdims=True) + EPS)
    return s


def _dnconv_fwd(name, proj, cb, w, normalize):
    T = proj.shape[0]

    def body(x_ref, w_ref, o_ref, buf):
        o_ref[...] = _dn_act(_conv_taps(x_ref[...], w_ref, 4, buf), normalize)

    return _pcall(
        body, grid=(8,), in_specs=[pl.BlockSpec((T, LANE), lambda j: (0, cb + j)), pl.BlockSpec((4, LANE), lambda j: (0, cb + j))],
        out_specs=pl.BlockSpec((T, LANE), lambda j: (0, j)), out_shape=SDS((T, 1024), f32), scratch_shapes=[pltpu.VMEM((T, LANE), f32)],
        compiler_params=_cparams(VMEM_BIG), name=name)(proj, w)


def _dnconv_bwd(name, proj, cb, w, dout, normalize, dest_buf):
    T = proj.shape[0]
    b_args, b_specs, aliases = _into_buffer(dest_buf, 3, 0)

    def body(x_ref, w_ref, do_ref, *rest):
        dx_ref, dw_ref, buf, buf2 = rest[len(b_args):]
        x = x_ref[...]
        y = _conv_taps(x, w_ref, 4, buf)
        _, vjp = jax.vjp(functools.partial(_dn_act, normalize=normalize), y)
        (dy,) = vjp(do_ref[...])
        dx_ref[...] = _conv_taps_bwd(x, dy, w_ref, dw_ref, 4, buf2).astype(dx_ref.dtype)

    return _pcall(
        body, grid=(8,),
        in_specs=[pl.BlockSpec((T, LANE), lambda j: (0, cb + j)), pl.BlockSpec((4, LANE), lambda j: (0, cb + j)),
                  pl.BlockSpec((T, LANE), lambda j: (0, j))] + b_specs,
        out_specs=[pl.BlockSpec((T, LANE), lambda j: (0, cb + j)), pl.BlockSpec((4, LANE), lambda j: (0, j))],
        out_shape=[SDS((T, PROJ_W), bf16), SDS((4, 1024), f32)], scratch_shapes=[pltpu.VMEM((T, LANE), f32)] * 2,
        input_output_aliases=aliases, compiler_params=_cparams(VMEM_BIG), name=name)(proj, w, dout, *b_args)


def _ffn_point(a, lin):
    return a * jax.nn.sigmoid(a) * lin


def _ffnact_fwd(name, up, w, b):
    T = up.shape[0]
    nblk = D_FF // LANE

    def body(a_ref, l_ref, w_ref, b_ref, o_ref, buf):
        a = _conv_taps(a_ref[...], w_ref, 3, buf) + b_ref[...]
        o_ref[...] = _ffn_point(a, l_ref[...]).astype(o_ref.dtype)

    return _pcall(
        body, grid=(nblk,),
        in_specs=[pl.BlockSpec((T, LANE), lambda j: (0, j)), pl.BlockSpec((T, LANE), lambda j: (0, nblk + j)),
                  pl.BlockSpec((3, LANE), lambda j: (0, j)), pl.BlockSpec((1, LANE), lambda j: (0, j))],
        out_specs=pl.BlockSpec((T, LANE), lambda j: (0, j)), out_shape=SDS((T, D_FF), bf16), scratch_shapes=[pltpu.VMEM((T, LANE), f32)],
        compiler_params=_cparams(VMEM_BIG), name=name)(up, up, w, b)


def _ffnact_bwd(name, up, w, b, dmid):
    T = up.shape[0]
    nblk = D_FF // LANE

    def body(a_ref, l_ref, w_ref, b_ref, dm_ref, da_ref, dl_ref, dw_ref, db_ref, buf, buf2):
        x = a_ref[...]
        a = _conv_taps(x, w_ref, 3, buf) + b_ref[...]
        _, vjp = jax.vjp(_ffn_point, a, l_ref[...])
        da, dl = vjp(dm_ref[...].astype(f32))
        dl_ref[...] = dl.astype(dl_ref.dtype)
        db_ref[...] = jnp.sum(da, axis=0, keepdims=True)
        da_ref[...] = _conv_taps_bwd(x, da, w_ref, dw_ref, 3, buf2).astype(da_ref.dtype)

    col = lambda r: pl.BlockSpec((r, LANE), lambda j: (0, j))
    return _pcall(
        body, grid=(nblk,),
        in_specs=[col(T), pl.BlockSpec((T, LANE), lambda j: (0, nblk + j)), col(3), col(1), col(T)],
        out_specs=[col(T), col(T), col(3), col(1)],
        out_shape=[SDS((T, D_FF), bf16), SDS((T, D_FF), bf16), SDS((3, D_FF), f32), SDS((1, D_FF), f32)],
        scratch_shapes=[pltpu.VMEM((T, LANE), f32)] * 2, compiler_params=_cparams(VMEM_BIG), name=name)(up, up, w, b, dmid)


def _bmm(a, b, ca, cb, precision=HI):
    return lax.dot_general(a, b, (((ca,), (cb,)), ((0,), (0,))), precision=precision, preferred_element_type=f32)


def _make_bdot(ca, cb):
    def raw(x, y, cx, cy):
        return _bmm(x.astype(bf16), y.astype(bf16), cx, cy, precision=None)

    @jax.custom_vjp
    def f(a, b):
        return raw(a, b, ca, cb)

    def fwd(a, b):
        return raw(a, b, ca, cb), (a, b)

    def bwd(res, dy):
        a, b = res
        if (ca, cb) == (2, 1):
            return raw(dy, b, 2, 2), raw(a, dy, 1, 1)
        if (ca, cb) == (2, 2):
            return raw(dy, b, 2, 1), raw(dy, a, 1, 1)
        return raw(b, dy, 2, 2), raw(a, dy, 2, 1)

    f.defvjp(fwd, bwd)
    return f


_bdot_nn, _bdot_nt, _bdot_tn = _make_bdot(2, 1), _make_bdot(2, 2), _make_bdot(1, 1)


def _pieces(a, n):
    out, r = [], a
    for _ in range(n):
        p = r.astype(bf16)
        out.append(p)
        r = r - p.astype(f32)
    return out


def _bmm_split(x, y, cx, cy, nx=2, ny=2, order=1):
    xs, ys = _pieces(x, nx), _pieces(y, ny)
    acc = None
    for i in reversed(range(nx)):
        for j in reversed(range(ny)):
            if i + j <= order:
                t = _bmm(xs[i], ys[j], cx, cy, precision=None)
                acc = t if acc is None else acc + t
    return acc


@jax.custom_vjp
def _solve_apply(X, r):
    return _bmm_split(X, r, 2, 1)


def _solve_apply_fwd(X, r):
    return _bmm_split(X, r, 2, 1), (X, r)


def _solve_apply_bwd(res, dy):
    X, r = res
    return _bmm_split(dy, r, 2, 2), _bmm_split(X, dy, 1, 1)


_solve_apply.defvjp(_solve_apply_fwd, _solve_apply_bwd)


def _lower_ones(H, C):
    ri = lax.broadcasted_iota(jnp.int32, (H, C, C), 1)
    ci = lax.broadcasted_iota(jnp.int32, (H, C, C), 2)
    return (ri >= ci).astype(f32)


def _cumsum_rows_raw(G):
    return _bmm_split(_lower_ones(G.shape[0], G.shape[1]), G, 2, 1, nx=1, ny=3, order=2)


@jax.custom_vjp
def _cumsum_rows(G):
    return _cumsum_rows_raw(G)


def _cumsum_rows_fwd(G):
    return _cumsum_rows_raw(G), None


def _cumsum_rows_bwd(_, dy):
    return (_bmm_split(_lower_ones(dy.shape[0], dy.shape[1]), dy, 1, 1, nx=1, ny=3, order=2),)


_cumsum_rows.defvjp(_cumsum_rows_fwd, _cumsum_rows_bwd)


def _tri_inverse_raw(L):
    H, C, _ = L.shape
    ri = lax.broadcasted_iota(jnp.int32, (C, C), 0)
    ci = lax.broadcasted_iota(jnp.int32, (C, C), 1)
    eye = jnp.broadcast_to((ri == ci).astype(f32)[None], (H, C, C))
    Dg = jnp.where(((ri >> 3) == (ci >> 3))[None], L, 0.0)
    D2 = _bmm_split(Dg, Dg, 2, 1)
    X = _bmm_split(_bmm_split(eye - Dg, eye + D2, 2, 1), eye + _bmm_split(D2, D2, 2, 1), 2, 1)
    for lg in range(3, C.bit_length() - 1):
        same = (ri >> (lg + 1)) == (ci >> (lg + 1))
        lower_left = same & (((ri >> lg) & 1) == 1) & (((ci >> lg) & 1) == 0)
        X = X - _bmm_split(_bmm_split(X, jnp.where(lower_left[None], L, 0.0), 2, 1), X, 2, 1)
    return X


@jax.custom_vjp
def _tri_inverse(L):
    return _tri_inverse_raw(L)


def _tri_inverse_fwd(L):
    X = _tri_inverse_raw(L)
    return X, X


def _tri_inverse_bwd(X, dX):
    return (-_bmm_split(_bmm_split(X, dX, 1, 1), X, 2, 2),)


_tri_inverse.defvjp(_tri_inverse_fwd, _tri_inverse_bwd)


@jax.custom_vjp
def _tri_inverse_known(L, X):
    return X


def _tri_inverse_known_fwd(L, X):
    return X, X


def _tri_inverse_known_bwd(X, dX):
    return _tri_inverse_bwd(X, dX)[0], jnp.zeros_like(X)


_tri_inverse_known.defvjp(_tri_inverse_known_fwd, _tri_inverse_known_bwd)


DN_NC = 2


def _gdn_chunk(q, k, v, ab, alog, dtb, S, X_known=None, keep_X=False):
    H, C, _ = q.shape
    NC, NH = ab.shape[0], H // ab.shape[0]
    lane = lax.broadcasted_iota(jnp.int32, (H, C, LANE), 2)
    head = lax.broadcasted_iota(jnp.int32, (H, C, LANE), 0) & (NH - 1)
    abb = jnp.concatenate([jnp.broadcast_to(ab[c][None], (NH, C, LANE)) for c in range(NC)], axis=0)
    a = jnp.sum(jnp.where(lane == head, abb, 0.0), axis=2, keepdims=True)
    b = jnp.sum(jnp.where(lane == head + 8, abb, 0.0), axis=2, keepdims=True)
    pick = lax.broadcasted_iota(jnp.int32, (H, 1, LANE), 2) == (lax.broadcasted_iota(jnp.int32, (H, 1, LANE), 0) & (NH - 1))
    al = jnp.sum(jnp.where(pick, alog[None], 0.0), axis=2, keepdims=True)
    db = jnp.sum(jnp.where(pick, dtb[None], 0.0), axis=2, keepdims=True)
    g = -jnp.exp(al) * jax.nn.softplus(a + db)
    beta = jax.nn.sigmoid(b)
    ri = lax.broadcasted_iota(jnp.int32, (C, C), 0)
    ci = lax.broadcasted_iota(jnp.int32, (C, C), 1)
    G = jnp.broadcast_to(g, (H, C, LANE))
    gc = _cumsum_rows(G)
    gi = _cumsum_rows(jnp.broadcast_to(g, (H, C, C)))
    decay = jnp.exp(jnp.where((ri >= ci)[None], gi - jnp.swapaxes(gi, 1, 2), -jnp.inf))
    qs = q * (LANE ** -0.5)
    kb = k * beta
    L = jnp.where((ri > ci)[None], _bdot_nt(kb, k) * decay, 0.0)
    X = _tri_inverse(L) if X_known is None else _tri_inverse_known(L, X_known)
    egc = jnp.exp(gc)
    u = _solve_apply(X, v * beta)
    w = _solve_apply(X, kb * egc)
    qk = _bdot_nt(qs, k) * decay
    g_last = jnp.sum(G, axis=1, keepdims=True)
    k_dec = k * jnp.exp(g_last - gc)
    q_dec = qs * egc
    e_last = jnp.exp(g_last)
    outs = []
    for c in range(NC):
        sl = slice(c * NH, (c + 1) * NH)
        v_new = u[sl] - _bdot_nn(w[sl], S)
        outs.append(_bdot_nn(q_dec[sl], S) + _bdot_nn(qk[sl], v_new))
        S = S * e_last[sl] + _bdot_tn(k_dec[sl], v_new)
    o = jnp.concatenate(outs, axis=0)
    return (o, S, X) if keep_X else (o, S)


def _heads(ref):
    return jnp.stack([ref[DN_C * c:DN_C * (c + 1), LANE * h:LANE * (h + 1)] for c in range(DN_NC) for h in range(8)], axis=0)


def _put_heads(ref, val):
    for c in range(DN_NC):
        for h in range(8):
            ref[DN_C * c:DN_C * (c + 1), LANE * h:LANE * (h + 1)] = val[8 * c + h]


def _chunk_rows(ref):
    return jnp.stack([ref[DN_C * c:DN_C * (c + 1), :] for c in range(DN_NC)], axis=0)


def _hosted_parts(hosted):
    if hosted is None:
        return [], [], [], []
    n = len(hosted["arrays"])
    return list(hosted["arrays"]), [HBM_SPEC] * n, list(hosted["out_shape"]), _dma_sems(hosted["n_sems"])


def _gdn_fwd(name, q, k, v, proj, alog, dtb, hosted=None):
    T = q.shape[0]
    R = DN_C * DN_NC
    N = T // R
    h_args, h_specs, h_shapes, h_sems = _hosted_parts(hosted)
    nh = len(h_args)

    def body(q_ref, k_ref, v_ref, ab_ref, al_ref, dt_ref, *rest):
        h_ins, (o_ref, sall_ref, xinv_ref), h_outs, s_scr, sems = rest[:nh], rest[nh:nh + 3], rest[nh + 3:2 * nh + 3], rest[2 * nh + 3], rest[2 * nh + 4:]
        step = pl.program_id(0)

        @pl.when(step == 0)
        def _():
            s_scr[...] = jnp.zeros_like(s_scr)
            if hosted is not None:
                hosted["start"](h_ins, h_outs, *sems)

        S = s_scr[...]
        sall_ref[...] = S
        o, S_new, X = _gdn_chunk(_heads(q_ref), _heads(k_ref), _heads(v_ref), _chunk_rows(ab_ref), al_ref[...], dt_ref[...], S, keep_X=True)
        _put_heads(o_ref, o)
        s_scr[...] = S_new
        xinv_ref[...] = X

        if hosted is not None:
            @pl.when(step == N - 1)
            def _():
                hosted["finish"](h_ins, h_outs, *sems)

    blk = pl.BlockSpec((R, 8 * LANE), lambda n: (n, 0))
    vec = pl.BlockSpec((1, LANE), lambda n: (0, 0))
    state = pl.BlockSpec((None, 8, LANE, LANE), lambda n: (n, 0, 0, 0))
    xinv = pl.BlockSpec((None, 8 * DN_NC, DN_C, DN_C), lambda n: (n, 0, 0, 0))
    outs = _pcall(
        body, grid=(N,), in_specs=[blk, blk, blk, pl.BlockSpec((R, LANE), lambda n: (n, CB_AB)), vec, vec] + h_specs,
        out_specs=[blk, state, xinv] + h_specs,
        out_shape=[SDS((T, 1024), f32), SDS((N, 8, LANE, LANE), f32), SDS((N, 8 * DN_NC, DN_C, DN_C), f32)] + h_shapes,
        scratch_shapes=[pltpu.VMEM((8, LANE, LANE), f32)] + h_sems, name=name)(q, k, v, proj, alog, dtb, *h_args)
    return outs[0], outs[1], outs[2], list(outs[3:])


def _gdn_bwd(name, q, k, v, proj, alog, dtb, sall, xinv, do, hosted=None):
    T = q.shape[0]
    R = DN_C * DN_NC
    N = T // R
    h_args, h_specs, h_shapes, h_sems = _hosted_parts(hosted)
    nh = len(h_args)

    def body(q_ref, k_ref, v_ref, ab_ref, al_ref, dt_ref, s_ref, x_ref, do_ref, *rest):
        h_ins, h_outs, ds_scr, sems = rest[:nh], rest[nh + 6:2 * nh + 6], rest[2 * nh + 6], rest[2 * nh + 7:]
        dq_ref, dk_ref, dv_ref, dab_ref, dal_ref, ddt_ref = rest[nh:nh + 6]
        step = pl.program_id(0)

        @pl.when(step == 0)
        def _():
            ds_scr[...] = jnp.zeros_like(ds_scr)
            dal_ref[...] = jnp.zeros_like(dal_ref)
            ddt_ref[...] = jnp.zeros_like(ddt_ref)
            if hosted is not None:
                hosted["start"](h_ins, h_outs, *sems)

        _, vjp = jax.vjp(functools.partial(_gdn_chunk, X_known=x_ref[...]), _heads(q_ref), _heads(k_ref), _heads(v_ref), _chunk_rows(ab_ref),
                         al_ref[...], dt_ref[...], s_ref[...])
        dq, dk, dv, dab, dal, ddt, dS = vjp((_heads(do_ref), ds_scr[...]))
        _put_heads(dq_ref, dq)
        _put_heads(dk_ref, dk)
        _put_heads(dv_ref, dv)
        ds_scr[...] = dS
        for c in range(DN_NC):
            dab_ref[DN_C * c:DN_C * (c + 1), :] = dab[c]
        dal_ref[...] += dal
        ddt_ref[...] += ddt

        if hosted is not None:
            @pl.when(step == N - 1)
            def _():
                hosted["finish"](h_ins, h_outs, *sems)

    blk = pl.BlockSpec((R, 8 * LANE), lambda n: (N - 1 - n, 0))
    vec = pl.BlockSpec((1, LANE), lambda n: (0, 0))
    state = pl.BlockSpec((None, 8, LANE, LANE), lambda n: (N - 1 - n, 0, 0, 0))
    outs = _pcall(
        body, grid=(N,),
        in_specs=[blk, blk, blk, pl.BlockSpec((R, LANE), lambda n: (N - 1 - n, CB_AB)), vec, vec, state,
                  pl.BlockSpec((None, 8 * DN_NC, DN_C, DN_C), lambda n: (N - 1 - n, 0, 0, 0)), blk] + h_specs,
        out_specs=[blk, blk, blk, pl.BlockSpec((R, LANE), lambda n: (N - 1 - n, 0)), vec, vec] + h_specs,
        out_shape=[SDS((T, 1024), f32)] * 3 + [SDS((T, LANE), f32), SDS((1, LANE), f32), SDS((1, LANE), f32)] + h_shapes,
        scratch_shapes=[pltpu.VMEM((8, LANE, LANE), f32)] + h_sems, name=name)(q, k, v, proj, alog, dtb, sall, xinv, do, *h_args)
    return tuple(outs[:6]), list(outs[6:])


def _segmean_raw(x2, bd):
    return jnp.concatenate([_dot(x2[:, LANE * j:LANE * (j + 1)], bd, 1, 0) for j in range(x2.shape[1] // LANE)], axis=1)


@jax.custom_vjp
def _segmean(x2, bd):
    return _segmean_raw(x2, bd)


def _segmean_fwd(x2, bd):
    return _segmean_raw(x2, bd), bd


def _segmean_bwd(bd, dy):
    return _segmean_raw(dy, bd), jnp.zeros_like(bd)


_segmean.defvjp(_segmean_fwd, _segmean_bwd)


def _qknorm_fn(x, w, bd):
    return x * lax.rsqrt(_segmean(x * x, bd) + EPS) * w


def _rope_apply(xn, c, s1, s2):
    W = xn.shape[1]
    return xn * c + pltpu.roll(xn, W - 8, 1) * s1 + pltpu.roll(xn, 8, 1) * s2


def _rope_apply_t(d, c, s1, s2):
    W = d.shape[1]
    return d * c + pltpu.roll(d * s1, 8, 1) + pltpu.roll(d * s2, W - 8, 1)


def _rope_tiles(refs, width):
    return [jnp.tile(r[...], (1, width // LANE)) for r in refs]


def _qkprep_fwd(name, proj, wb, width, w, bd, tabs):
    T = proj.shape[0]
    tr = _pick(T, (256, 128))

    def body(x_ref, w_ref, bd_ref, c_ref, s1_ref, s2_ref, o_ref):
        xn = _qknorm_fn(x_ref[...], w_ref[...], bd_ref[...])
        o_ref[...] = _rope_apply(xn, *_rope_tiles((c_ref, s1_ref, s2_ref), width))

    row0 = pl.BlockSpec((tr, width), lambda i: (i, 0))
    tab = pl.BlockSpec((tr, LANE), lambda i: (i, 0))
    full = lambda a: pl.BlockSpec(a.shape, lambda i: (0, 0))
    return _pcall(
        body, grid=(T // tr,), in_specs=[pl.BlockSpec((tr, width), lambda i: (i, wb)), full(w), full(bd), tab, tab, tab],
        out_specs=row0, out_shape=SDS((T, width), f32), name=name)(proj, w, bd, *tabs)


def _qkprep_bwd(name, proj, wb, width, w, bd, tabs, dout, dest_buf="none"):
    T = proj.shape[0]
    tr = _pick(T, (256, 128))
    into = not isinstance(dest_buf, str)
    b_args, b_specs, aliases = _into_buffer(dest_buf, 7, 0) if into else ([], [], {})

    def body(x_ref, w_ref, bd_ref, c_ref, s1_ref, s2_ref, do_ref, *rest):
        dx_ref, dw_ref = rest[len(b_args):]
        i = pl.program_id(0)
        dxn = _rope_apply_t(do_ref[...], *_rope_tiles((c_ref, s1_ref, s2_ref), width))
        bd = bd_ref[...]
        _, vjp = jax.vjp(lambda x, w_: _qknorm_fn(x, w_, bd), x_ref[...], w_ref[...])
        dx, dw = vjp(dxn)
        dx_ref[...] = dx.astype(dx_ref.dtype)

        @pl.when(i == 0)
        def _():
            dw_ref[...] = jnp.zeros_like(dw_ref)

        dw_ref[...] += dw

    row0 = pl.BlockSpec((tr, width), lambda i: (i, 0))
    tab = pl.BlockSpec((tr, LANE), lambda i: (i, 0))
    full = lambda a: pl.BlockSpec(a.shape, lambda i: (0, 0))
    dx_spec = pl.BlockSpec((tr, width), lambda i: (i, wb)) if into else row0
    dx_shape = SDS((T, PROJ_W), bf16) if into else SDS((T, width), bf16)
    return _pcall(
        body, grid=(T // tr,), in_specs=[pl.BlockSpec((tr, width), lambda i: (i, wb)), full(w), full(bd), tab, tab, tab, row0] + b_specs,
        out_specs=[dx_spec, full(w)], out_shape=[dx_shape, SDS(w.shape, f32)], input_output_aliases=aliases, name=name)(
            proj, w, bd, *tabs, dout, *b_args)


def _make_dot16(ca, cb):
    def raw(x, y, cx, cy):
        return _dot(x.astype(bf16), y.astype(bf16), cx, cy, precision=None)

    @jax.custom_vjp
    def f(a, b):
        return raw(a, b, ca, cb)

    def fwd(a, b):
        return raw(a, b, ca, cb), (a, b)

    def bwd(res, dy):
        a, b = res
        if (ca, cb) == (1, 0):
            return raw(dy, b, 1, 1), raw(a, dy, 0, 0)
        return raw(dy, b, 1, 0), raw(dy, a, 0, 0)

    f.defvjp(fwd, bwd)
    return f


_dot16_nn, _dot16_nt = _make_dot16(1, 0), _make_dot16(1, 1)


def _attn_bias():
    qi = jnp.arange(8 * SWA_B) % SWA_B
    kj = jnp.arange(2 * SWA_B)
    rel = qi[:, None] + SWA_B - kj[None, :]
    valid = (rel >= 0) & (rel < SWA_B)
    neg = jnp.float32(-jnp.inf)
    return jnp.stack([jnp.where(valid & (kj[None, :] >= SWA_B), 0.0, neg), jnp.where(valid, 0.0, neg)]).astype(f32)


def _attn_group(qg, kb, vb, sinks, bias, hk):
    R = qg.shape[0]
    s = _dot16_nt(qg, kb) * 0.125 + bias
    head = (lax.broadcasted_iota(jnp.int32, (R, LANE), 0) >> 7) + 8 * hk
    lane = lax.broadcasted_iota(jnp.int32, (R, LANE), 1)
    sink = jnp.sum(jnp.where(lane == head, jnp.broadcast_to(sinks, (R, LANE)), 0.0), axis=1, keepdims=True)
    m = lax.stop_gradient(jnp.maximum(jnp.max(s, axis=1, keepdims=True), sink))
    p = jnp.exp(s - m)
    denom = jnp.sum(p, axis=1, keepdims=True) + jnp.exp(sink - m)
    return _dot16_nn(p / denom, vb)


def _group_rows(ref, hk):
    return jnp.concatenate([ref[:, 64 * (8 * hk + g):64 * (8 * hk + g + 1)] for g in range(8)], axis=0)


def _put_group(ref, hk, val):
    for g in range(8):
        ref[:, 64 * (8 * hk + g):64 * (8 * hk + g + 1)] = val[SWA_B * g:SWA_B * (g + 1)].astype(ref.dtype)


def _attn_specs():
    qs = pl.BlockSpec((SWA_B, 1024), lambda i: (i, 0))
    cur = pl.BlockSpec((SWA_B, LANE), lambda i: (i, 0))
    prev = pl.BlockSpec((SWA_B, LANE), lambda i: (jnp.maximum(i - 1, 0), 0))
    vcur = pl.BlockSpec((SWA_B, LANE), lambda i: (i, CB_SWV))
    vprev = pl.BlockSpec((SWA_B, LANE), lambda i: (jnp.maximum(i - 1, 0), CB_SWV))
    vec = pl.BlockSpec((1, LANE), lambda i: (0, 0))
    bias = pl.BlockSpec((None, 8 * SWA_B, 2 * SWA_B), lambda i: (jnp.minimum(i, 1), 0, 0))
    return qs, cur, prev, vcur, vprev, vec, bias


def _hosted_edge(hosted, which, h_ins, h_outs, sems, at):
    if hosted is None:
        return

    @pl.when(at)
    def _():
        hosted[which](h_ins, h_outs, *sems)


def _attn_fwd(name, sq, sk, proj, sinks, bias, hosted=None):
    T = sq.shape[0]
    nb = T // SWA_B
    h_args, h_specs, h_shapes, h_sems = _hosted_parts(hosted)
    nh = len(h_args)

    def body(q_ref, kp_ref, kc_ref, vp_ref, vc_ref, sk_ref, b_ref, *rest):
        h_ins, o_ref, h_outs, sems = rest[:nh], rest[nh], rest[nh + 1:2 * nh + 1], rest[2 * nh + 1:]
        _hosted_edge(hosted, "start", h_ins, h_outs, sems, pl.program_id(0) == 0)
        sinks_v, bias_v = sk_ref[...], b_ref[...]
        for hk in range(2):
            ks = slice(64 * hk, 64 * hk + 64)
            kb = jnp.concatenate([kp_ref[:, ks], kc_ref[:, ks]], axis=0)
            vb = jnp.concatenate([vp_ref[:, ks], vc_ref[:, ks]], axis=0)
            _put_group(o_ref, hk, _attn_group(_group_rows(q_ref, hk), kb, vb, sinks_v, bias_v, hk))
        _hosted_edge(hosted, "finish", h_ins, h_outs, sems, pl.program_id(0) == nb - 1)

    qs, cur, prev, vcur, vprev, vec, bspec = _attn_specs()
    outs = _pcall(body, grid=(nb,), in_specs=[qs, prev, cur, vprev, vcur, vec, bspec] + h_specs, out_specs=[qs] + h_specs,
                  out_shape=[SDS((T, 1024), bf16)] + h_shapes, scratch_shapes=h_sems, name=name)(sq, sk, sk, proj, proj, sinks, bias, *h_args)
    return outs[0], list(outs[1:])


def _attn_bwd(name, sq, sk, proj, sinks, bias, do, hosted=None):
    T = sq.shape[0]
    nb = T // SWA_B
    h_args, h_specs, h_shapes, h_sems = _hosted_parts(hosted)
    nh = len(h_args)

    def body(q_ref, kp_ref, kc_ref, vp_ref, vc_ref, sk_ref, b_ref, do_ref, *rest):
        h_ins, h_outs, sems = rest[:nh], rest[nh + 6:2 * nh + 6], rest[2 * nh + 6:]
        dq_ref, dkp_ref, dkc_ref, dvp_ref, dvc_ref, dsk_ref = rest[nh:nh + 6]
        _hosted_edge(hosted, "start", h_ins, h_outs, sems, pl.program_id(0) == 0)

        @pl.when(pl.program_id(0) == 0)
        def _():
            dsk_ref[...] = jnp.zeros_like(dsk_ref)

        sinks_v, bias_v = sk_ref[...], b_ref[...]
        dsk = jnp.zeros((1, LANE), f32)
        for hk in range(2):
            ks = slice(64 * hk, 64 * hk + 64)
            kb = jnp.concatenate([kp_ref[:, ks], kc_ref[:, ks]], axis=0)
            vb = jnp.concatenate([vp_ref[:, ks], vc_ref[:, ks]], axis=0)
            _, vjp = jax.vjp(functools.partial(_attn_group, bias=bias_v, hk=hk), _group_rows(q_ref, hk), kb, vb, sinks_v)
            dq, dkb, dvb, ds_ = vjp(_group_rows(do_ref, hk))
            _put_group(dq_ref, hk, dq)
            dsk = dsk + ds_
            dkp_ref[:, ks] = dkb[:SWA_B]
            dkc_ref[:, ks] = dkb[SWA_B:]
            dvp_ref[:, ks] = dvb[:SWA_B]
            dvc_ref[:, ks] = dvb[SWA_B:]
        dsk_ref[...] += dsk
        _hosted_edge(hosted, "finish", h_ins, h_outs, sems, pl.program_id(0) == nb - 1)

    qs, cur, prev, vcur, vprev, vec, bspec = _attn_specs()
    outs = _pcall(
        body, grid=(nb,), in_specs=[qs, prev, cur, vprev, vcur, vec, bspec, qs] + h_specs, out_specs=[qs, cur, cur, cur, cur, vec] + h_specs,
        out_shape=[SDS((T, 1024), f32)] + [SDS((T, LANE), f32)] * 4 + [SDS((1, LANE), f32)] + h_shapes, scratch_shapes=h_sems,
        name=name)(sq, sk, sk, proj, proj, sinks, bias, do, *h_args)
    return tuple(outs[:6]), list(outs[6:])


def _shift_add(name, cur, prev, out_dtype):
    T = cur.shape[0]

    def body(c_ref, p_ref, o_ref):
        o_ref[0:T - SWA_B, :] = (c_ref[0:T - SWA_B, :] + p_ref[SWA_B:T, :]).astype(o_ref.dtype)
        o_ref[T - SWA_B:T, :] = c_ref[T - SWA_B:T, :].astype(o_ref.dtype)

    return _pcall(body, out_shape=SDS((T, LANE), out_dtype), name=name)(cur, prev)


def _loss(name, y, tgt):
    T = y.shape[0]

    def body(y_ref, t_ref, l_ref, dy_ref):
        @pl.when(pl.program_id(0) == 0)
        def _():
            l_ref[...] = jnp.zeros_like(l_ref)

        d = y_ref[...] - t_ref[...]
        l_ref[...] += jnp.sum(d * d) * (0.5 / D)
        dy_ref[...] = d * (1.0 / D)

    row = pl.BlockSpec((TR, D), lambda i: (i, 0))
    return _pcall(body, grid=(T // TR,), in_specs=[row, row], out_specs=[pl.BlockSpec((8, LANE), lambda i: (0, 0)), row],
                  out_shape=[SDS((8, LANE), f32), SDS((T, D), f32)], name=name)(y, tgt)


def _adamw(name, w, g, m, v):
    shape = w.shape
    C = shape[-1]
    R = int(np.prod(shape[:-1]))
    tr = _pick(R, (128, 64, 16, 8))
    bc1 = np.float32(1.0 - ADAM_B1 ** ADAM_STEP)
    bc2 = np.float32(1.0 - ADAM_B2 ** ADAM_STEP)

    def body(w_ref, g_ref, m_ref, v_ref, d_ref, mo_ref, vo_ref):
        g_ = g_ref[...]
        m_ = ADAM_B1 * m_ref[...] + (1.0 - ADAM_B1) * g_
        v_ = ADAM_B2 * v_ref[...] + (1.0 - ADAM_B2) * (g_ * g_)
        d_ref[...] = -ADAM_LR * ((m_ / bc1) / (jnp.sqrt(v_ / bc2) + ADAM_EPS) + ADAM_WD * w_ref[...])
        mo_ref[...] = m_
        vo_ref[...] = v_

    blk = pl.BlockSpec((tr, C), lambda i: (i, 0))
    outs = _pcall(body, grid=(R // tr,), in_specs=[blk] * 4, out_specs=[blk] * 3, out_shape=[SDS((R, C), f32)] * 3,
                  compiler_params=_cparams(VMEM_BIG), name=name)(*[t.reshape(R, C) for t in (w, g, m, v)])
    return [o.reshape(shape) for o in outs]


def _silu_rows(name, x):
    def body(x_ref, o_ref):
        t = x_ref[...]
        o_ref[...] = (t * jax.nn.sigmoid(t)).astype(o_ref.dtype)

    return _pcall(body, out_shape=SDS(x.shape, bf16), name=name)(x)


def _sum_leading(name, x):
    n = x.shape[0]

    def body(x_ref, o_ref):
        acc = x_ref[0]
        for k in range(1, n):
            acc = acc + x_ref[k]
        o_ref[...] = acc

    tr = x.shape[1] if x.size * 4 <= 8 * 2 ** 20 else _pick(x.shape[1], (COMM_TR, 8))
    return _pcall(body, grid=(x.shape[1] // tr,), in_specs=[pl.BlockSpec((n, tr, x.shape[2]), lambda i: (0, i, 0))],
                  out_specs=pl.BlockSpec((tr, x.shape[2]), lambda i: (i, 0)), out_shape=SDS(x.shape[1:], x.dtype), name=name)(x)


def _add_my_half(name, g4, b1, c):
    _, R, W = g4.shape
    nblk = (R // 2) // COMM_TR

    def body(c_ref, g_ref, b_ref, o_ref):
        o_ref[...] = g_ref[...] + b_ref[...]

    grid_spec = pltpu.PrefetchScalarGridSpec(
        num_scalar_prefetch=1, grid=(4, nblk),
        in_specs=[pl.BlockSpec((None, COMM_TR, W), lambda s, i, c_ref: (s, c_ref[0] * nblk + i, 0)),
                  pl.BlockSpec((None, COMM_TR, W), lambda s, i, c_ref: (s, i, 0))],
        out_specs=pl.BlockSpec((None, COMM_TR, W), lambda s, i, c_ref: (s, i, 0)))
    return _pcall(body, grid_spec=grid_spec, out_shape=SDS((4, R // 2, W), f32), name=name)(c.reshape(1), g4, b1)


HBM_SPEC = pl.BlockSpec(memory_space=pltpu.HBM)


def _position():
    x, y, c = lax.axis_index("x"), lax.axis_index("y"), lax.axis_index("c")
    return x, y, c, [(1 - x, y), (x, 1 - y), (1 - x, 1 - y)]


def _remote(src, dst, send_sems, recv_sems, k, to):
    return pltpu.make_async_remote_copy(src_ref=src, dst_ref=dst, send_sem=send_sems.at[k], recv_sem=recv_sems.at[k],
                                        device_id=to, device_id_type=MESH)


def _allgather_chips(name, buf):
    R, W = buf.shape
    Rh = R // 2

    def body(in_ref, out_ref, send_sems, recv_sems, local_sem):
        x, y, c, chips = _position()
        me = 2 * x + y
        sib = (x, y, 1 - c)
        half = pl.ds(pl.multiple_of(c * Rh, 32), Rh)
        ohalf = pl.ds(pl.multiple_of((1 - c) * Rh, 32), Rh)
        mine = pltpu.make_async_copy(in_ref, out_ref.at[me], local_sem)
        mine.start()
        first = [_remote(in_ref.at[half], out_ref.at[me, half], send_sems, recv_sems, j, (cx, cy, c)) for j, (cx, cy) in enumerate(chips)]
        for cp in first:
            cp.start()
        passed = []
        for j, (cx, cy) in enumerate(chips):
            rows = out_ref.at[2 * cx + cy, half]
            _remote(rows, rows, send_sems, recv_sems, j, (cx, cy, c)).wait_recv()
            cp = _remote(rows, rows, send_sems, recv_sems, 3 + j, sib)
            cp.start()
            passed.append(cp)
        for j, (cx, cy) in enumerate(chips):
            rows = out_ref.at[2 * cx + cy, ohalf]
            _remote(rows, rows, send_sems, recv_sems, 3 + j, sib).wait_recv()
        for cp in first + passed:
            cp.wait_send()
        mine.wait()

    return _pcall(body, in_specs=[HBM_SPEC], out_specs=HBM_SPEC, out_shape=SDS((4, R, W), buf.dtype),
                  scratch_shapes=[pltpu.SemaphoreType.DMA((6,)), pltpu.SemaphoreType.DMA((6,)), pltpu.SemaphoreType.DMA], name=name)(buf)


def _swap_halves(name, g4):
    _, R, W = g4.shape
    Rh = R // 2

    def body(in_ref, out_ref, send_sems, recv_sems):
        x, y, c, _ = _position()
        ohalf = pl.ds(pl.multiple_of((1 - c) * Rh, 32), Rh)
        cp = _remote(in_ref.at[:, ohalf, :], out_ref, send_sems, recv_sems, 0, (x, y, 1 - c))
        cp.start()
        cp.wait()

    return _pcall(body, in_specs=[HBM_SPEC], out_specs=HBM_SPEC, out_shape=SDS((4, Rh, W), g4.dtype),
                  scratch_shapes=[pltpu.SemaphoreType.DMA((1,)), pltpu.SemaphoreType.DMA((1,))], name=name)(g4)


def _scatter_chips(name, p4):
    _, Rh, W = p4.shape

    def body(in_ref, out_ref, send_sems, recv_sems, local_sem):
        x, y, c, chips = _position()
        me = 2 * x + y
        mine = pltpu.make_async_copy(in_ref.at[me], out_ref.at[me], local_sem)
        mine.start()
        sends = [_remote(in_ref.at[2 * cx + cy], out_ref.at[me], send_sems, recv_sems, j, (cx, cy, c)) for j, (cx, cy) in enumerate(chips)]
        for cp in sends:
            cp.start()
        for j, (cx, cy) in enumerate(chips):
            slot = out_ref.at[2 * cx + cy]
            _remote(slot, slot, send_sems, recv_sems, j, (cx, cy, c)).wait_recv()
        for cp in sends:
            cp.wait_send()
        mine.wait()

    return _pcall(body, in_specs=[HBM_SPEC], out_specs=HBM_SPEC, out_shape=SDS((4, Rh, W), p4.dtype),
                  scratch_shapes=[pltpu.SemaphoreType.DMA((3,)), pltpu.SemaphoreType.DMA((3,)), pltpu.SemaphoreType.DMA], name=name)(p4)


def _join_halves(name, r):
    Rh, W = r.shape

    def body(in_ref, out_ref, send_sems, recv_sems, local_sem):
        x, y, c, _ = _position()
        mine = pltpu.make_async_copy(in_ref, out_ref.at[c], local_sem)
        mine.start()
        cp = _remote(in_ref, out_ref.at[c], send_sems, recv_sems, 0, (x, y, 1 - c))
        cp.start()
        _remote(in_ref, out_ref.at[1 - c], send_sems, recv_sems, 0, (x, y, 1 - c)).wait_recv()
        cp.wait_send()
        mine.wait()

    return _pcall(body, in_specs=[HBM_SPEC], out_specs=HBM_SPEC, out_shape=SDS((2, Rh, W), r.dtype),
                  scratch_shapes=[pltpu.SemaphoreType.DMA((1,)), pltpu.SemaphoreType.DMA((1,)), pltpu.SemaphoreType.DMA], name=name)(r)


def _allgather_all(name, buf):
    r, W = buf.shape

    def body(in_ref, out_ref, send_sems, recv_sems, local_sem):
        x, y, c, _ = _position()
        me = 4 * x + 2 * y + c
        mine = pltpu.make_async_copy(in_ref, out_ref.at[me], local_sem)
        mine.start()
        peers = []
        for mk in range(1, 8):
            mx, my, mc = (mk >> 2) & 1, (mk >> 1) & 1, mk & 1
            px = 1 - x if mx else x
            py = 1 - y if my else y
            pc = 1 - c if mc else c
            peers.append((px, py, pc))
        sends = [_remote(in_ref, out_ref.at[me], send_sems, recv_sems, k, p) for k, p in enumerate(peers)]
        for cp in sends:
            cp.start()
        for k, (px, py, pc) in enumerate(peers):
            slot = out_ref.at[4 * px + 2 * py + pc]
            _remote(slot, slot, send_sems, recv_sems, k, (px, py, pc)).wait_recv()
        for cp in sends:
            cp.wait_send()
        mine.wait()

    return _pcall(body, in_specs=[HBM_SPEC], out_specs=HBM_SPEC, out_shape=SDS((8, r, W), buf.dtype),
                  scratch_shapes=[pltpu.SemaphoreType.DMA((7,)), pltpu.SemaphoreType.DMA((7,)), pltpu.SemaphoreType.DMA], name=name)(buf)


def _reduce_scatter(g4, c, tag):
    b1 = _swap_halves("rs_swap_" + tag, g4)
    p4 = _add_my_half("rs_pair_" + tag, g4, b1, c)
    b2 = _scatter_chips("rs_scatter_" + tag, p4)
    r = _sum_leading("rs_sum_" + tag, b2)
    full = _join_halves("rs_join_" + tag, r)
    return full.reshape(g4.shape[1], g4.shape[2])


def _dma_sems(n):
    return [pltpu.SemaphoreType.DMA((n,)), pltpu.SemaphoreType.DMA((n,))]


def _gather_chips(name, shards):
    n = len(shards)

    def body(*refs):
        ins, outs = refs[:n], refs[n:2 * n]
        send_sems, recv_sems = refs[2 * n:]
        x, y, c, chips = _position()
        me = 2 * x + y
        sib = (x, y, 1 - c)
        sends, halves = [], []
        for i in range(n):
            rh = ins[i].shape[0] // 2
            halves.append((pl.ds(pl.multiple_of(c * rh, 16), rh), pl.ds(pl.multiple_of((1 - c) * rh, 16), rh)))
        for i in range(n):
            for j, (cx, cy) in enumerate(chips):
                cp = _remote(ins[i].at[halves[i][0]], outs[i].at[me, halves[i][0]], send_sems, recv_sems, 6 * i + j, (cx, cy, c))
                cp.start()
                sends.append(cp)
        for j, (cx, cy) in enumerate(chips):
            for i in range(n):
                rows = outs[i].at[2 * cx + cy, halves[i][0]]
                _remote(rows, rows, send_sems, recv_sems, 6 * i + j, (cx, cy, c)).wait_recv()
                cp = _remote(rows, rows, send_sems, recv_sems, 6 * i + 3 + j, sib)
                cp.start()
                sends.append(cp)
        for j, (cx, cy) in enumerate(chips):
            for i in range(n):
                rows = outs[i].at[2 * cx + cy, halves[i][1]]
                _remote(rows, rows, send_sems, recv_sems, 6 * i + 3 + j, sib).wait_recv()
        for cp in sends:
            cp.wait_send()

    return _pcall(body, in_specs=[HBM_SPEC] * n, out_specs=[HBM_SPEC] * n, out_shape=[SDS((4,) + s.shape, s.dtype) for s in shards],
                  scratch_shapes=_dma_sems(6 * n), name=name)(*shards)


def _swap_halves_multi(name, slots):
    n = len(slots)

    def body(*refs):
        ins, outs = refs[:n], refs[n:2 * n]
        send_sems, recv_sems = refs[2 * n:]
        x, y, c, _ = _position()
        cps = []
        for i in range(n):
            rh = ins[i].shape[1] // 2
            ohalf = pl.ds(pl.multiple_of((1 - c) * rh, 8), rh)
            cp = _remote(ins[i].at[:, ohalf, :], outs[i], send_sems, recv_sems, i, (x, y, 1 - c))
            cp.start()
            cps.append(cp)
        for cp in cps:
            cp.wait()

    return _pcall(body, in_specs=[HBM_SPEC] * n, out_specs=[HBM_SPEC] * n,
                  out_shape=[SDS((4, s.shape[1] // 2, s.shape[2]), s.dtype) for s in slots], scratch_shapes=_dma_sems(n), name=name)(*slots)


def _pair_add(name, g4, b1, c):
    _, R, W = g4.shape
    tr = _pick(R // 2, (256, 128, 32, 16))
    nblk = (R // 2) // tr

    def body(c_ref, g_ref, b_ref, o_ref):
        o_ref[...] = (g_ref[...] + b_ref[...]).astype(o_ref.dtype)

    grid_spec = pltpu.PrefetchScalarGridSpec(
        num_scalar_prefetch=1, grid=(4, nblk),
        in_specs=[pl.BlockSpec((None, tr, W), lambda s, i, c_ref: (s, c_ref[0] * nblk + i, 0)),
                  pl.BlockSpec((None, tr, W), lambda s, i, c_ref: (s, i, 0))],
        out_specs=pl.BlockSpec((None, tr, W), lambda s, i, c_ref: (s, i, 0)))
    return _pcall(body, grid_spec=grid_spec, out_shape=SDS((4, R // 2, W), bf16), name=name)(c.reshape(1), g4, b1)


def _scatter_chips_multi(name, ps):
    n = len(ps)

    def body(*refs):
        ins, outs = refs[:n], refs[n:2 * n]
        send_sems, recv_sems = refs[2 * n:]
        x, y, c, chips = _position()
        me = 2 * x + y
        sends = []
        for i in range(n):
            for j, (cx, cy) in enumerate(chips):
                cp = _remote(ins[i].at[2 * cx + cy], outs[i].at[me], send_sems, recv_sems, 3 * i + j, (cx, cy, c))
                cp.start()
                sends.append(cp)
        for i in range(n):
            for j, (cx, cy) in enumerate(chips):
                slot = outs[i].at[2 * cx + cy]
                _remote(slot, slot, send_sems, recv_sems, 3 * i + j, (cx, cy, c)).wait_recv()
        for cp in sends:
            cp.wait_send()

    return _pcall(body, in_specs=[HBM_SPEC] * n, out_specs=[HBM_SPEC] * n, out_shape=[SDS(p.shape, p.dtype) for p in ps],
                  scratch_shapes=_dma_sems(3 * n), name=name)(*ps)


def _sum_chips(name, p4, b2, chip, c):
    _, Rh, W = p4.shape
    tr = _pick(Rh, (256, 128, 32, 16))
    nblk = Rh // tr

    def body(m_ref, c_ref, own_ref, r1_ref, r2_ref, r3_ref, o_ref):
        o_ref[...] = ((own_ref[...].astype(f32) + r1_ref[...].astype(f32)) + r2_ref[...].astype(f32)) + r3_ref[...].astype(f32)

    other = lambda k: pl.BlockSpec((None, tr, W), lambda i, m_ref, c_ref: (m_ref[0] ^ k, i, 0))
    grid_spec = pltpu.PrefetchScalarGridSpec(
        num_scalar_prefetch=2, grid=(nblk,),
        in_specs=[pl.BlockSpec((None, tr, W), lambda i, m_ref, c_ref: (m_ref[0], i, 0)), other(1), other(2), other(3)],
        out_specs=pl.BlockSpec((tr, W), lambda i, m_ref, c_ref: (c_ref[0] * nblk + i, 0)))
    return _pcall(body, grid_spec=grid_spec, out_shape=SDS((2 * Rh, W), f32), name=name)(chip.reshape(1), c.reshape(1), p4, b2, b2, b2)


def _join_halves(name, fulls):
    n = len(fulls)

    def body(*refs):
        outs = refs[n:2 * n]
        send_sems, recv_sems = refs[2 * n:]
        x, y, c, _ = _position()
        cps = []
        for i in range(n):
            rh = outs[i].shape[0] // 2
            mine = outs[i].at[pl.ds(pl.multiple_of(c * rh, 8), rh)]
            theirs = outs[i].at[pl.ds(pl.multiple_of((1 - c) * rh, 8), rh)]
            cp = _remote(mine, mine, send_sems, recv_sems, i, (x, y, 1 - c))
            cp.start()
            cps.append((cp, _remote(theirs, theirs, send_sems, recv_sems, i, (x, y, 1 - c))))
        for cp, back in cps:
            back.wait_recv()
            cp.wait_send()

    return _pcall(body, in_specs=[HBM_SPEC] * n, out_specs=[HBM_SPEC] * n, out_shape=[SDS(r.shape, r.dtype) for r in fulls],
                  input_output_aliases={i: i for i in range(n)}, scratch_shapes=_dma_sems(n), name=name)(*fulls)


def _hosted_gather(shards):
    n = len(shards)

    def half(ref_rows, c):
        rh = ref_rows // 2
        return pl.ds(pl.multiple_of(c * rh, 16), rh)

    def start(ins, outs, send_sems, recv_sems):
        x, y, c, chips = _position()
        me = 2 * x + y
        for i in range(n):
            rows = half(ins[i].shape[0], c)
            for j, (cx, cy) in enumerate(chips):
                _remote(ins[i].at[rows], outs[i].at[me, rows], send_sems, recv_sems, 3 * i + j, (cx, cy, c)).start()

    def finish(ins, outs, send_sems, recv_sems):
        x, y, c, chips = _position()
        me = 2 * x + y
        for i in range(n):
            rows = half(ins[i].shape[0], c)
            for j, (cx, cy) in enumerate(chips):
                _remote(ins[i].at[rows], outs[i].at[2 * cx + cy, rows], send_sems, recv_sems, 3 * i + j, (cx, cy, c)).wait_recv()
        for i in range(n):
            rows = half(ins[i].shape[0], c)
            for j, (cx, cy) in enumerate(chips):
                _remote(ins[i].at[rows], outs[i].at[me, rows], send_sems, recv_sems, 3 * i + j, (cx, cy, c)).wait_send()

    return {"arrays": shards, "out_shape": [SDS((4,) + s.shape, s.dtype) for s in shards], "n_sems": 3 * n, "start": start, "finish": finish}


def _gather_forward(name, gathered):
    n = len(gathered)

    def body(*refs):
        outs = refs[n:2 * n]
        send_sems, recv_sems = refs[2 * n:]
        x, y, c, chips = _position()
        sib = (x, y, 1 - c)
        sends = []
        for i in range(n):
            rh = outs[i].shape[1] // 2
            mine = pl.ds(pl.multiple_of(c * rh, 16), rh)
            for j, (cx, cy) in enumerate(chips):
                rows = outs[i].at[2 * cx + cy, mine]
                cp = _remote(rows, rows, send_sems, recv_sems, 3 * i + j, sib)
                cp.start()
                sends.append(cp)
        for i in range(n):
            rh = outs[i].shape[1] // 2
            theirs = pl.ds(pl.multiple_of((1 - c) * rh, 16), rh)
            for j, (cx, cy) in enumerate(chips):
                rows = outs[i].at[2 * cx + cy, theirs]
                _remote(rows, rows, send_sems, recv_sems, 3 * i + j, sib).wait_recv()
        for cp in sends:
            cp.wait_send()

    return _pcall(body, in_specs=[HBM_SPEC] * n, out_specs=[HBM_SPEC] * n, out_shape=[SDS(g.shape, g.dtype) for g in gathered],
                  input_output_aliases={i: i for i in range(n)}, scratch_shapes=_dma_sems(3 * n), name=name)(*gathered)


def _hosted_scatter(ps):
    n = len(ps)

    def start(ins, outs, send_sems, recv_sems):
        x, y, c, chips = _position()
        me = 2 * x + y
        for i in range(n):
            for j, (cx, cy) in enumerate(chips):
                _remote(ins[i].at[2 * cx + cy], outs[i].at[me], send_sems, recv_sems, 3 * i + j, (cx, cy, c)).start()

    def finish(ins, outs, send_sems, recv_sems):
        x, y, c, chips = _position()
        me = 2 * x + y
        for i in range(n):
            for j, (cx, cy) in enumerate(chips):
                slot = outs[i].at[2 * cx + cy]
                _remote(slot, slot, send_sems, recv_sems, 3 * i + j, (cx, cy, c)).wait_recv()
        for i in range(n):
            for j, (cx, cy) in enumerate(chips):
                _remote(ins[i].at[2 * cx + cy], outs[i].at[me], send_sems, recv_sems, 3 * i + j, (cx, cy, c)).wait_send()

    return {"arrays": ps, "out_shape": [SDS(p.shape, p.dtype) for p in ps], "n_sems": 3 * n, "start": start, "finish": finish}


def _hosted_swap(slots):
    n = len(slots)

    def copies(ins, outs, send_sems, recv_sems):
        x, y, c, _ = _position()
        cps = []
        for i in range(n):
            rh = ins[i].shape[1] // 2
            ohalf = pl.ds(pl.multiple_of((1 - c) * rh, 8), rh)
            cps.append(_remote(ins[i].at[:, ohalf, :], outs[i], send_sems, recv_sems, i, (x, y, 1 - c)))
        return cps

    def start(ins, outs, send_sems, recv_sems):
        for cp in copies(ins, outs, send_sems, recv_sems):
            cp.start()

    def finish(ins, outs, send_sems, recv_sems):
        for cp in copies(ins, outs, send_sems, recv_sems):
            cp.wait()

    return {"arrays": slots, "out_shape": [SDS((4, s.shape[1] // 2, s.shape[2]), s.dtype) for s in slots], "n_sems": n,
            "start": start, "finish": finish}


def _rs_begin(slots, c):
    b1 = _swap_halves_multi("rs_swap", slots)
    return [_pair_add("rs_pair_%d" % i, g, b, c) for i, (g, b) in enumerate(zip(slots, b1))]


def _rs_end(ps, b2, c, chip):
    return _join_halves("rs_join", [_sum_chips("rs_sum_%d" % i, p, b, chip, c) for i, (p, b) in enumerate(zip(ps, b2))])


def _reduce_scatter_multi(slots, c, chip):
    ps = _rs_begin(slots, c)
    return _rs_end(ps, _scatter_chips_multi("rs_scatter", ps), c, chip)


_BIG = ("w_in", "w_dn_out", "w_swa_out", "w_o", "w_up", "w_down")


_W_IN_PIECES = ((0, 3072, 0), (3072, 4096, 3072), (5392, 6416, 4096), (6416, 7440, 5120), (4112, 5136, 6144), (4096, 4112, 7168),
                (5136, 5264, 7296), (5264, 5392, 7424))
_W_IN_SHARD = IN_TOTAL // 4


def _w_in_from_slots(g):
    parts, at = [], 0
    for lo, hi, dst in _W_IN_PIECES:
        if dst > at:
            parts.append(jnp.zeros((g.shape[1], dst - at), g.dtype))
        for s in range(4):
            a, b = max(lo, s * _W_IN_SHARD), min(hi, (s + 1) * _W_IN_SHARD)
            if a < b:
                parts.append(g[s][:, a - s * _W_IN_SHARD:b - s * _W_IN_SHARD])
        at = dst + hi - lo
    parts.append(jnp.zeros((g.shape[1], PROJ_W - at), g.dtype))
    return jnp.concatenate(parts, axis=1)


def _w_in_to_slots(gw):
    slots = []
    for s in range(4):
        parts = []
        for lo, hi, dst in sorted(_W_IN_PIECES):
            a, b = max(lo, s * _W_IN_SHARD), min(hi, (s + 1) * _W_IN_SHARD)
            if a < b:
                parts.append(gw[:, dst + a - lo:dst + b - lo])
        slots.append(jnp.concatenate(parts, axis=1))
    return jnp.stack(slots)


def _assemble_mixer(gs):
    rows = lambda g: g.reshape(4 * g.shape[1], g.shape[2])
    return {"w_in": _w_in_from_slots(gs[0]), "w_dn_out": rows(gs[1]), "w_swa_out": rows(gs[2]), "w_o": rows(gs[3])}


def _assemble_ffn(gs):
    return {"w_up": gs[0], "w_down": gs[1].reshape(4 * gs[1].shape[1], gs[1].shape[2])}


def _grad_slots(gw):
    rows = lambda g: g.reshape(4, g.shape[0] // 4, g.shape[1])
    return [_w_in_to_slots(gw["w_in"]), rows(gw["w_dn_out"]), rows(gw["w_swa_out"]), rows(gw["w_o"]), gw["w_up"], rows(gw["w_down"])]


_SHARDED = (
    ("w_in", (D, 1860), 1, False),
    ("dn_conv", (4, 768), 1, True),
    ("w_dn_out", (256, D), 0, False),
    ("w_swa_out", (256, D), 0, False),
    ("w_o", (256, D), 0, False),
    ("w_up", (D, 1408), 1, False),
    ("ffn_conv", (3, 704), 1, True),
    ("w_down", (704, D), 0, False),
)


def _pack_weights(shards):
    parts = []
    for nm, shp, _, as_bits in _SHARDED:
        a = shards[nm]
        parts.append(lax.bitcast_convert_type(a, bf16).reshape(-1) if as_bits else a.astype(bf16).reshape(-1))
    flat = jnp.concatenate(parts)
    return jnp.pad(flat, (0, COMM_ROWS * COMM_W - flat.shape[0])).reshape(COMM_ROWS, COMM_W)


def _unpack_weights(g):
    flat = g.reshape(4, -1)
    out, off = {}, 0
    for nm, shp, ax, as_bits in _SHARDED:
        n = int(np.prod(shp)) * (2 if as_bits else 1)
        piece = flat[:, off:off + n]
        off += n
        if as_bits:
            piece = lax.bitcast_convert_type(piece.reshape((4,) + shp + (2,)), f32)
        else:
            piece = piece.reshape((4,) + shp)
        out[nm] = jnp.concatenate([piece[s] for s in range(4)], axis=ax)
    return out


def _pack_grads(grads):
    slots = []
    for s in range(4):
        parts = []
        for nm, shp, ax, _ in _SHARDED:
            n = shp[ax]
            parts.append(lax.slice_in_dim(grads[nm], s * n, (s + 1) * n, axis=ax).reshape(-1))
        flat = jnp.concatenate(parts)
        slots.append(jnp.pad(flat, (0, COMM_ROWS * COMM_W - flat.shape[0])))
    return jnp.stack(slots).reshape(4, COMM_ROWS, COMM_W)


def _unpack_grads(r):
    flat = r.reshape(-1)
    out, off = {}, 0
    for nm, shp, _, _ in _SHARDED:
        n = int(np.prod(shp))
        out[nm] = flat[off:off + n].reshape(shp)
        off += n
    return out


def _regroup_w_in(w):
    z = lambda n: jnp.zeros((w.shape[0], n), w.dtype)
    return jnp.concatenate([w[:, 0:3072], w[:, 3072:4096], w[:, 5392:6416], w[:, 6416:7440], w[:, 4112:5136],
                            w[:, 4096:4112], z(112), w[:, 5136:5264], w[:, 5264:5392], z(128)], axis=1)


def _ungroup_w_in(g):
    return jnp.concatenate([g[:, 0:3072], g[:, 3072:4096], g[:, 7168:7184], g[:, 6144:7168], g[:, 7296:7424], g[:, 7424:7552],
                            g[:, 4096:5120], g[:, 5120:6144]], axis=1)


def _pad_lanes(v, n=LANE):
    return jnp.pad(v, (0, n - v.shape[0])).reshape(1, n)


def _layer_consts(P):
    K = {}
    K["norm_mix"] = P["norm_mix"].reshape(1, D)
    K["norm_ffn"] = P["norm_ffn"].reshape(1, D)
    K["alog"] = _pad_lanes(P["dn_a_log"])
    K["dtb"] = _pad_lanes(P["dn_dt_bias"])
    K["dn_norm"] = P["dn_norm"].reshape(1, LANE)
    K["qn"] = jnp.tile(P["swa_q_norm"], 16).reshape(1, D)
    K["kn"] = jnp.tile(P["swa_k_norm"], 2).reshape(1, LANE)
    K["sinks"] = _pad_lanes(P["swa_sinks"])
    K["ffn_b"] = P["ffn_conv_b"].reshape(1, D_FF)
    return K


def _layer_fwd(x, mod, W, K, tabs, bd, hosted=None, late_ffn=None):
    sh1, sc1, gt1, sh2, sc2, gt2 = mod
    S = {"x": x}
    h1, h1t = _rowwise_fwd("normmod1_fwd", _normmod_fn, [(x, 0, D)], [K["norm_mix"], sc1, sh1], [D], [bf16], also_transposed=True)
    proj = _matmul("proj_fwd", h1, W["w_in"], "nn", f32)
    qn = _dnconv_fwd("dnconv_q_fwd", proj, CB_Q, W["dn_conv"], True)
    kn = _dnconv_fwd("dnconv_k_fwd", proj, CB_K, W["dn_conv"], True)
    vc = _dnconv_fwd("dnconv_v_fwd", proj, CB_V, W["dn_conv"], False)
    o, sall, xinv, hosted_out = _gdn_fwd("gdn_fwd", qn, kn, vc, proj, K["alog"], K["dtb"], hosted=hosted)
    (on,) = _rowwise_fwd("dngate_fwd", _dngate_fn, [(o, 0, LANE), (proj, 8 * WB_Z, LANE)], [K["dn_norm"]], [LANE], [bf16], nc=8,
                         tr=_pick(x.shape[0], (1024,)))
    ya = _matmul("dnout_fwd", on, W["w_dn_out"], "nn", f32)
    sq = _qkprep_fwd("qprep_fwd", proj, WB_SWQ, D, K["qn"], bd[0], tabs[0])
    sk = _qkprep_fwd("kprep_fwd", proj, CB_SWK, LANE, K["kn"], bd[1], tabs[1])
    attn, ffn_arrived = _attn_fwd("attn_fwd", sq, sk, proj, K["sinks"], K["attn_bias"], hosted=None if late_ffn is None else late_ffn[0])
    if late_ffn is not None:
        W.update(late_ffn[1](ffn_arrived))
    yb = _matmul("swaout_fwd", attn, W["w_swa_out"], "nn", f32)
    (merged,) = _rowwise_fwd("merge_fwd", _merge_fn, [(proj, WB_GA, D), (proj, WB_GB, D), (ya, 0, D), (yb, 0, D)], [], [D], [bf16])
    t1, x1 = _matmul("wo_fwd", merged, W["w_o"], "nn", f32, resid=(x, gt1))
    h2, h2t = _rowwise_fwd("normmod2_fwd", _normmod_fn, [(x1, 0, D)], [K["norm_ffn"], sc2, sh2], [D], [bf16], also_transposed=True)
    up = _matmul("up_fwd", h2, W["w_up"], "nn", f32, b_slots=True)
    mid = _ffnact_fwd("ffnact_fwd", up, W["ffn_conv"], K["ffn_b"])
    t2, x2 = _matmul("down_fwd", mid, W["w_down"], "nn", f32, resid=(x1, gt2))
    S.update(h1t=h1t, h2t=h2t, proj=proj, qn=qn, kn=kn, vc=vc, o=o, sall=sall, xinv=xinv, on=on, ya=ya, sq=sq, sk=sk, attn=attn, yb=yb,
             merged=merged, t1=t1, x1=x1, h2=h2, up=up, mid=mid, t2=t2)
    return x2, S, hosted_out


def _layer_bwd(dx2, S, mod, W, K, tabs, bd, carry=None, early=None):
    sh1, sc1, gt1, sh2, sc2, gt2 = mod
    x, x1, proj, up = S["x"], S["x1"], S["proj"], S["up"]
    T = x.shape[0]
    gw, gs = {}, {}
    dt2, dgt2 = _rowwise_bwd("resid2_bwd", _resid_fn, [(x1, 0, D), (S["t2"], 0, D)], [gt2], [(dx2, 0, D)], [None, bf16])
    dmid = _matmul("down_bwd_x", dt2, W["w_down"], "nt", bf16)
    gw["w_down"] = _matmul("down_bwd_w", S["mid"], dt2, "tn", f32)
    dact, dlin, gw["ffn_conv"], dffn_b = _ffnact_bwd("ffnact_bwd", up, W["ffn_conv"], K["ffn_b"], dmid)
    dup = (dact, dlin)
    dh2 = _matmul("up_bwd_x", dup, W["w_up"], "nt", f32, b_slots=True)
    if carry is None:
        gw["w_up"] = _matmul("up_bwd_w", S["h2t"], dup, "nn", f32, out_slots=4)
        pair_sums = hosted = None
    else:
        gw["w_up"], b1 = _matmul("up_bwd_w", S["h2t"], dup, "nn", f32, out_slots=4, hosted=_hosted_swap(carry[0]))
        pair_sums = [_pair_add("rs_pair_%d" % i, g, b, carry[1]) for i, (g, b) in enumerate(zip(carry[0], b1))]
        hosted = _hosted_scatter(pair_sums)
    dx1, dnorm_ffn, dsc2, dsh2 = _rowwise_bwd("normmod2_bwd", _normmod_fn, [(x1, 0, D)], [K["norm_ffn"], sc2, sh2], [(dh2, 0, D)], [f32],
                                              add_to_first=(dx2, 0, D))
    early_out = None
    if early is not None:
        ffn_slots = [gw["w_up"], gw["w_down"].reshape(4, D_FF // 4, D)]
        ffn_ps = [_pair_add("rs_pair_ffn_%d" % i, g, b, early[0]) for i, (g, b) in enumerate(zip(ffn_slots, _swap_halves_multi("rs_swap_ffn", ffn_slots)))]
    dt1, dgt1 = _rowwise_bwd("resid1_bwd", _resid_fn, [(x, 0, D), (S["t1"], 0, D)], [gt1], [(dx1, 0, D)], [None, bf16])
    dmerged = _matmul("wo_bwd_x", dt1, W["w_o"], "nt", f32)
    gw["w_o"] = _matmul("wo_bwd_w", S["merged"], dt1, "tn", f32)
    dproj, dya, dyb = _rowwise_bwd("merge_bwd", _merge_fn, [(proj, WB_GA, D), (proj, WB_GB, D), (S["ya"], 0, D), (S["yb"], 0, D)], [],
                                   [(dmerged, 0, D)], [bf16, bf16, bf16, bf16], dest=(None, (0, 1), WB_GA // 2))
    don = _matmul("dnout_bwd_x", dya, W["w_dn_out"], "nt", f32)
    gw["w_dn_out"] = _matmul("dnout_bwd_w", S["on"], dya, "tn", f32)
    dproj, do, ddn_norm = _rowwise_bwd("dngate_bwd", _dngate_fn, [(S["o"], 0, LANE), (proj, 8 * WB_Z, LANE)], [K["dn_norm"]], [(don, 0, LANE)],
                                       [f32, bf16], nc=8, tr=_pick(T, (1024,)), dest=(dproj, (1,), 8 * WB_Z))
    (dqn, dkn, dvc, dab, dalog, ddtb), hosted_out = _gdn_bwd("gdn_bwd", S["qn"], S["kn"], S["vc"], proj, K["alog"], K["dtb"], S["sall"], S["xinv"], do,
                                                            hosted=hosted)
    dproj, dwq = _dnconv_bwd("dnconv_q_bwd", proj, CB_Q, W["dn_conv"], dqn, True, dproj)
    dproj, dwk = _dnconv_bwd("dnconv_k_bwd", proj, CB_K, W["dn_conv"], dkn, True, dproj)
    dproj, dwv = _dnconv_bwd("dnconv_v_bwd", proj, CB_V, W["dn_conv"], dvc, False, dproj)
    gw["dn_conv"] = jnp.concatenate([dwq, dwk, dwv], axis=1)
    dattn = _matmul("swaout_bwd_x", dyb, W["w_swa_out"], "nt", f32)
    gw["w_swa_out"] = _matmul("swaout_bwd_w", S["attn"], dyb, "tn", f32)
    (dsq, dkp, dkc, dvp, dvc_, dsinks), ffn_b2 = _attn_bwd("attn_bwd", S["sq"], S["sk"], proj, K["sinks"], K["attn_bias"], dattn,
                                                          hosted=None if early is None else _hosted_scatter(ffn_ps))
    if early is not None:
        early_out = _rs_end(ffn_ps, ffn_b2, early[0], early[1])
    dsk = _shift_add("attn_dk_join", dkc, dkp, f32)
    dswv = _shift_add("attn_dv_join", dvc_, dvp, bf16)
    dproj, dqn_w = _qkprep_bwd("qprep_bwd", proj, WB_SWQ, D, K["qn"], bd[0], tabs[0], dsq, dest_buf=dproj)
    dswk, dkn_w = _qkprep_bwd("kprep_bwd", proj, CB_SWK, LANE, K["kn"], bd[1], tabs[1], dsk)
    tail = jnp.concatenate([dab.astype(bf16), dswk, dswv, jnp.zeros((T, LANE), bf16)], axis=1)
    dproj = lax.dynamic_update_slice(dproj, tail, (0, CB_AB * LANE))
    dh1 = _matmul("proj_bwd_x", dproj, W["w_in"], "nt", f32)
    gw["w_in"] = _matmul("proj_bwd_w", S["h1t"], dproj, "nn", f32)
    dx, dnorm_mix, dsc1, dsh1 = _rowwise_bwd("normmod1_bwd", _normmod_fn, [(x, 0, D)], [K["norm_mix"], sc1, sh1], [(dh1, 0, D)], [f32],
                                             add_to_first=(dx1, 0, D))
    gs = {"norm_mix": dnorm_mix[0], "dn_a_log": dalog[0, :8], "dn_dt_bias": ddtb[0, :8], "dn_norm": ddn_norm[0],
          "swa_q_norm": dqn_w.reshape(16, 64).sum(0), "swa_k_norm": dkn_w.reshape(2, 64).sum(0), "swa_sinks": dsinks[0, :16],
          "norm_ffn": dnorm_ffn[0], "ffn_conv_b": dffn_b[0]}
    dmod = jnp.concatenate([dsh1, dsc1, dgt1, dsh2, dsc2, dgt2], axis=1)
    return dx, gw, gs, dmod, (pair_sums, hosted_out), early_out


def _rope_tables(pos):
    T = pos.shape[0]
    half = 8
    inv = jnp.power(ROPE_THETA, -jnp.arange(half, dtype=f32) / half)
    ang = pos.astype(f32)[:, None] * inv
    cos, sin = jnp.cos(ang), jnp.sin(ang)
    z8, z48, o48 = jnp.zeros((T, 8), f32), jnp.zeros((T, 48), f32), jnp.ones((T, 48), f32)
    c64 = jnp.concatenate([cos, cos, o48], axis=1)
    s1 = jnp.concatenate([-sin, z8, z48], axis=1)
    s2 = jnp.concatenate([z8, sin, z48], axis=1)
    return tuple(jnp.tile(t, (1, 2)) for t in (c64, s1, s2))


_SMALL = (("norm_mix", D), ("dn_a_log", 8), ("dn_dt_bias", 8), ("dn_norm", 128), ("swa_q_norm", 64), ("swa_k_norm", 64),
          ("swa_sinks", 16), ("norm_ffn", D), ("ffn_conv_b", D_FF), ("b_ada", 6 * D))
_CONV = (("dn_conv", 4 * 3072), ("ffn_conv", 3 * D_FF))
_CONV_SHARD = (("dn_conv", 4 * 768), ("ffn_conv", 3 * 704))


def _pack_small(vals, spec):
    flat = jnp.concatenate([vals[nm].reshape(-1) for nm, _ in spec])
    rows = -(-flat.shape[0] // (8 * LANE)) * 8
    return jnp.pad(flat, (0, rows * LANE - flat.shape[0])).reshape(rows, LANE)


def _unpack_small(buf, spec):
    flat = buf.reshape(-1)
    out, off = {}, 0
    for nm, n in spec:
        out[nm] = flat[off:off + DEPTH * n].reshape(DEPTH, n)
        off += DEPTH * n
    return out


def kernel(x, c, positions, w_ada, b_ada, norm_mix, w_in, dn_conv, dn_a_log, dn_dt_bias, dn_norm, w_dn_out, swa_q_norm, swa_k_norm, swa_sinks, w_swa_out, w_o, norm_ffn, w_up, ffn_conv, ffn_conv_b, w_down, loss_target, m_w_ada, m_b_ada, m_norm_mix, m_w_in, m_dn_conv, m_dn_a_log, m_dn_dt_bias, m_dn_norm, m_w_dn_out, m_swa_q_norm, m_swa_k_norm, m_swa_sinks, m_w_swa_out, m_w_o, m_norm_ffn, m_w_up, m_ffn_conv, m_ffn_conv_b, m_w_down, v_w_ada, v_b_ada, v_norm_mix, v_w_in, v_dn_conv, v_dn_a_log, v_dn_dt_bias, v_dn_norm, v_w_dn_out, v_swa_q_norm, v_swa_k_norm, v_swa_sinks, v_w_swa_out, v_w_o, v_norm_ffn, v_w_up, v_ffn_conv, v_ffn_conv_b, v_w_down):
    weights = dict(w_ada=w_ada, b_ada=b_ada, norm_mix=norm_mix, w_in=w_in, dn_conv=dn_conv, dn_a_log=dn_a_log, dn_dt_bias=dn_dt_bias,
                   dn_norm=dn_norm, w_dn_out=w_dn_out, swa_q_norm=swa_q_norm, swa_k_norm=swa_k_norm, swa_sinks=swa_sinks,
                   w_swa_out=w_swa_out, w_o=w_o, norm_ffn=norm_ffn, w_up=w_up, ffn_conv=ffn_conv, ffn_conv_b=ffn_conv_b, w_down=w_down)
    mom_m = dict(w_ada=m_w_ada, b_ada=m_b_ada, norm_mix=m_norm_mix, w_in=m_w_in, dn_conv=m_dn_conv, dn_a_log=m_dn_a_log,
                 dn_dt_bias=m_dn_dt_bias, dn_norm=m_dn_norm, w_dn_out=m_w_dn_out, swa_q_norm=m_swa_q_norm, swa_k_norm=m_swa_k_norm,
                 swa_sinks=m_swa_sinks, w_swa_out=m_w_swa_out, w_o=m_w_o, norm_ffn=m_norm_ffn, w_up=m_w_up, ffn_conv=m_ffn_conv,
                 ffn_conv_b=m_ffn_conv_b, w_down=m_w_down)
    mom_v = dict(w_ada=v_w_ada, b_ada=v_b_ada, norm_mix=v_norm_mix, w_in=v_w_in, dn_conv=v_dn_conv, dn_a_log=v_dn_a_log,
                 dn_dt_bias=v_dn_dt_bias, dn_norm=v_dn_norm, w_dn_out=v_w_dn_out, swa_q_norm=v_swa_q_norm, swa_k_norm=v_swa_k_norm,
                 swa_sinks=v_swa_sinks, w_swa_out=v_w_swa_out, w_o=v_w_o, norm_ffn=v_norm_ffn, w_up=v_w_up, ffn_conv=v_ffn_conv,
                 ffn_conv_b=v_ffn_conv_b, w_down=v_w_down)
    order = ["w_ada", "b_ada", "norm_mix", "w_in", "dn_conv", "dn_a_log", "dn_dt_bias", "dn_norm", "w_dn_out", "swa_q_norm",
             "swa_k_norm", "swa_sinks", "w_swa_out", "w_o", "norm_ffn", "w_up", "ffn_conv", "ffn_conv_b", "w_down"]
    ax, ay, ac = lax.axis_index("x"), lax.axis_index("y"), lax.axis_index("c")
    chip = 2 * ax + ay
    dev = 4 * ax + 2 * ay + ac
    T = x.shape[1]
    xs = x[0]

    c_all = _allgather_all("gather_c", jnp.pad(c, ((0, 7), (0, 0)))).reshape(8, 8, D)[:, 0]
    c_act = _silu_rows("silu_c", jnp.pad(c_all, ((0, 8), (0, 0))))
    mod_sh = jnp.stack([
        _matmul("mod_fwd", c_act, w_ada[l].astype(bf16), "nn", f32,
                bias=lax.dynamic_slice(b_ada[l], (chip * 1536,), (1536,)).reshape(1, 1536)) for l in range(DEPTH)])
    mod_all = _allgather_all("gather_mod", mod_sh.reshape(DEPTH * 16 * 12, LANE)).reshape(8, DEPTH, 16, 1536)
    mod_me = jnp.concatenate([lax.dynamic_index_in_dim(mod_all[2 * s], dev, axis=1, keepdims=False) for s in range(4)], axis=1)

    tabs_q = _rope_tables(positions[0])
    tabs = (tabs_q, tabs_q)
    head = jnp.arange(LANE) // 64
    bd128 = (head[:, None] == head[None, :]).astype(f32) / 64.0
    bd = (bd128, bd128)
    attn_bias = _attn_bias()

    conv_all = _allgather_all("gather_conv", _pack_small({"dn_conv": dn_conv, "ffn_conv": ffn_conv}, _CONV_SHARD))
    conv_parts = [_unpack_small(conv_all[2 * s], _CONV_SHARD) for s in range(4)]
    dn_conv_full = jnp.concatenate([p["dn_conv"].reshape(DEPTH, 4, 768) for p in conv_parts], axis=2)
    ffn_conv_full = jnp.concatenate([p["ffn_conv"].reshape(DEPTH, 3, 704) for p in conv_parts], axis=2)

    saved, Ws, Ks, mods = [], [], [], []
    h = xs
    shards = [[weights[nm][l].astype(bf16) for nm in _BIG] for l in range(DEPTH)]
    own = lambda gs, ss: [lax.dynamic_update_index_in_dim(g, s, chip, 0) for g, s in zip(gs, ss)]
    gathered = _gather_chips("gather_w", shards[0][:4])
    for l in range(DEPTH):
        gathered = own(gathered, shards[l])
        W = _assemble_mixer(gathered[:4])
        late_ffn = None
        if l == 0:
            late_ffn = (_hosted_gather(shards[0][4:]), lambda arrived: _assemble_ffn(own(_gather_forward("gather_ffn_pass", arrived), shards[0][4:])))
        else:
            W.update(_assemble_ffn(gathered[4:]))
        W["dn_conv"], W["ffn_conv"] = dn_conv_full[l], ffn_conv_full[l]
        K = _layer_consts({nm: weights[nm][l] for nm in ("norm_mix", "norm_ffn", "dn_a_log", "dn_dt_bias", "dn_norm", "swa_q_norm",
                                                          "swa_k_norm", "swa_sinks", "ffn_conv_b")})
        K["attn_bias"] = attn_bias
        mod = tuple(mod_me[l, k * D:(k + 1) * D].reshape(1, D) for k in range(6))
        nxt = _hosted_gather(shards[l + 1]) if l + 1 < DEPTH else None
        h, S, arrived = _layer_fwd(h, mod, W, K, tabs, bd, hosted=nxt, late_ffn=late_ffn)
        if nxt is not None:
            gathered = _gather_forward("gather_w_pass", arrived)
        saved.append(S), Ws.append(W), Ks.append(K), mods.append(mod)

    loss_blk, dh = _loss("loss", h, loss_target[0])
    loss = lax.psum(loss_blk[0, 0], ("x", "y", "c"))

    grad_sh = [None] * DEPTH
    small = [None] * DEPTH
    dmods = [None] * DEPTH
    slots = None
    for l in reversed(range(DEPTH)):
        dh, gw, gs, dmod, (ps, b2), ffn0 = _layer_bwd(dh, saved[l], mods[l], Ws[l], Ks[l], tabs, bd, carry=None if slots is None else (slots, ac),
                                                      early=(ac, chip) if l == 0 else None)
        if slots is not None:
            grad_sh[l + 1] = dict(zip(_BIG, _rs_end(ps, b2, ac, chip)))
        slots = _grad_slots(gw)
        small[l], dmods[l] = dict(gs, dn_conv=gw["dn_conv"], ffn_conv=gw["ffn_conv"]), dmod[0]
    grad_sh[0] = dict(zip(_BIG, _reduce_scatter_multi(slots[:4], ac, chip) + ffn0))

    spec_g = _SMALL + _CONV
    vals = {nm: jnp.stack([small[l][nm] for l in range(DEPTH)]) for nm, _ in spec_g if nm != "b_ada"}
    vals["b_ada"] = jnp.stack(dmods)
    small_all = _allgather_all("gather_small", _pack_small(vals, spec_g))
    g_small = _unpack_small(_sum_leading("sum_small", small_all), spec_g)
    dmod_all = jnp.stack([_unpack_small(small_all[d], spec_g)["b_ada"] for d in range(8)])
    dmod_sh = lax.dynamic_slice(dmod_all, (0, 0, chip * 1536), (8, DEPTH, 1536))
    dmod_sh = jnp.pad(dmod_sh, ((0, 8), (0, 0), (0, 0))).astype(bf16)
    g_w_ada = jnp.stack([_matmul("mod_bwd_w", c_act, dmod_sh[:, l], "tn", f32) for l in range(DEPTH)])

    grads = {nm: g_small[nm] for nm, _ in _SMALL}
    grads["dn_conv"] = lax.dynamic_slice(g_small["dn_conv"].reshape(DEPTH, 4, 3072), (0, 0, chip * 768), (DEPTH, 4, 768))
    grads["ffn_conv"] = lax.dynamic_slice(g_small["ffn_conv"].reshape(DEPTH, 3, D_FF), (0, 0, chip * 704), (DEPTH, 3, 704))
    grads["w_ada"] = g_w_ada
    for nm in _BIG:
        grads[nm] = jnp.stack([grad_sh[l][nm] for l in range(DEPTH)])

    delta, new_m, new_v = {}, {}, {}
    for nm in ("w_ada", "dn_conv", "ffn_conv") + _BIG:
        delta[nm], new_m[nm], new_v[nm] = _adamw("adamw_" + nm, weights[nm], grads[nm], mom_m[nm], mom_v[nm])
    sm = [_pack_small({nm: t[nm] for nm, _ in _SMALL}, _SMALL) for t in (weights, grads, mom_m, mom_v)]
    for tgt, buf in zip((delta, new_m, new_v), _adamw("adamw_small", *sm)):
        tgt.update(_unpack_small(buf, _SMALL))

    return (loss, dh[None], *[grads[n] for n in order], *[delta[n] for n in order], *[new_m[n] for n in order], *[new_v[n] for n in order])
```

```python
import functools

import jax
import jax.numpy as jnp
import numpy as np
from jax import lax
from jax.experimental import pallas as pl
from jax.experimental.pallas import tpu as pltpu

f32 = jnp.float32
bf16 = jnp.bfloat16
SDS = jax.ShapeDtypeStruct
HI = lax.Precision.HIGHEST
MESH = pl.DeviceIdType.MESH

D = 1024
DEPTH = 4
EPS = 1e-6
DN_C = 64
SWA_B = 128
LANE = 128
ROPE_THETA = 500000.0
D_FF = 2816
IN_TOTAL = 7440
PROJ_W = 7680
CB_Q, CB_K, CB_V = 0, 8, 16
CB_AB, CB_SWK, CB_SWV = 56, 57, 58
WB_Z, WB_GA, WB_GB, WB_SWQ = 3, 4, 5, 6
TR = 256
COMM_TR = 128
VMEM_BIG = 48 * 2 ** 20

ADAM_LR, ADAM_B1, ADAM_B2, ADAM_EPS, ADAM_WD, ADAM_STEP = 0.001, 0.9, 0.999, 1e-08, 0.01, 10


def _pcall(body, **kw):
    return pl.pallas_call(body, **kw)


def _cparams(vmem=None):
    return pltpu.CompilerParams(vmem_limit_bytes=vmem) if vmem else None


def _dot(a, b, ca, cb, precision=HI):
    return lax.dot_general(a, b, (((ca,), (cb,)), ((), ())), precision=precision, preferred_element_type=f32)


def _pick(n, cands):
    for c in cands:
        if n % c == 0:
            return c
    return n


def _tile(n, cap):
    if n <= cap:
        return n
    best = None
    for t in range(LANE, cap + 1, LANE):
        if n % t == 0:
            best = t
    assert best is not None, (n, cap)
    return best


def _matmul(name, a, b, mode, out_dtype, bias=None, out_slots=None, hosted=None, resid=None, b_slots=False):
    h_args, h_specs, h_shapes, h_sems = _hosted_parts(hosted)
    nh = len(h_args)
    a_parts = list(a) if isinstance(a, (tuple, list)) else [a]
    b_parts = list(b) if isinstance(b, (tuple, list)) else [b]
    na, nbp = len(a_parts), len(b_parts)
    assert (na == 1 or mode == "nt") and (nbp == 1 or mode == "nn")
    if b_slots:
        S_, rows_, cols_ = b.shape
        M = a_parts[0].shape[0]
        K, N = (rows_, S_ * cols_) if mode == "nn" else (S_ * cols_, rows_)
    elif mode == "nn":
        M, K = a_parts[0].shape
        N = b_parts[0].shape[1] * nbp
    elif mode == "nt":
        M, K = a_parts[0].shape[0], a_parts[0].shape[1] * na
        N = b_parts[0].shape[0]
    else:
        (K, M), N = a_parts[0].shape, b_parts[0].shape[1]
    tm = _tile(M, 1536 if mode == "tn" else 1024)
    tn = N // out_slots if out_slots else _tile(N, 1536)
    tk = _tile(K, 512 if mode == "tn" else (1024 if K <= 1024 else 1536))
    nk, nj = K // tk, N // tn
    ka, jb = nk // na, nj // nbp
    assert nk % na == 0 and nj % nbp == 0
    ca, cb = {"nn": (1, 0), "nt": (1, 1), "tn": (0, 0)}[mode]
    grid = (M // tm, nj, nk)
    n_bias, n_res = (1 if bias is not None else 0), (2 if resid is not None else 0)

    def body(*refs):
        a_refs, b_refs = refs[:na], refs[na:na + nbp]
        p = na + nbp
        bias_ref = refs[p] if n_bias else None
        res_refs = refs[p + n_bias:p + n_bias + n_res]
        p += n_bias + n_res
        h_ins, o_ref = refs[p:p + nh], refs[p + nh]
        xo_ref = refs[p + nh + 1] if n_res else None
        p += nh + 1 + (1 if n_res else 0)
        h_outs, rest = refs[p:p + nh], refs[p + nh:]
        acc, sems = (rest[0], rest[1:]) if nk > 1 else (None, rest)
        j, k = pl.program_id(1), pl.program_id(2)
        step = (pl.program_id(0) * grid[1] + j) * grid[2] + k
        _hosted_edge(hosted, "start", h_ins, h_outs, sems, step == 0)

        def finish(r):
            if bias is not None:
                r = r + bias_ref[...]
            o_ref[...] = r.astype(o_ref.dtype)
            if n_res:
                xo_ref[...] = res_refs[0][...] + res_refs[1][...] * r

        a_tile = a_refs[0][...] if na == 1 else jnp.where(k < ka, a_refs[0][...], a_refs[1][...])
        b_tile = b_refs[0][...] if nbp == 1 else jnp.where(j < jb, b_refs[0][...], b_refs[1][...])
        part = _dot(a_tile.astype(bf16), b_tile.astype(bf16), ca, cb, precision=None)
        if nk == 1:
            finish(part)
        else:
            @pl.when(k == 0)
            def _():
                acc[...] = part

            @pl.when(k > 0)
            def _():
                acc[...] += part

            @pl.when(k == nk - 1)
            def _():
                finish(acc[...])

        _hosted_edge(hosted, "finish", h_ins, h_outs, sems, step == grid[0] * grid[1] * grid[2] - 1)

    if mode == "tn":
        a_specs = [pl.BlockSpec((tk, tm), lambda i, j, k: (k, i))]
    else:
        a_specs = [pl.BlockSpec((tm, tk), lambda i, j, k, q=q: (i, jnp.clip(k - q * ka, 0, ka - 1))) for q in range(na)]
    if b_slots:
        assert (tn if mode == "nn" else tk) == b.shape[2]
        b_specs = [pl.BlockSpec((None, tk, tn), lambda i, j, k: (j, k, 0)) if mode == "nn" else pl.BlockSpec((None, tn, tk), lambda i, j, k: (k, j, 0))]
    elif mode == "nt":
        b_specs = [pl.BlockSpec((tn, tk), lambda i, j, k: (j, k))]
    else:
        b_specs = [pl.BlockSpec((tk, tn), lambda i, j, k, q=q: (k, jnp.clip(j - q * jb, 0, jb - 1))) for q in range(nbp)]
    in_specs, args = a_specs + b_specs, a_parts + b_parts
    if bias is not None:
        in_specs.append(pl.BlockSpec((1, tn), lambda i, j, k: (0, j)))
        args.append(bias)
    if out_slots:
        out_specs, out_shape = [pl.BlockSpec((None, tm, tn), lambda i, j, k: (j, i, 0))], [SDS((out_slots, M, tn), out_dtype)]
    else:
        out_specs, out_shape = [pl.BlockSpec((tm, tn), lambda i, j, k: (i, j))], [SDS((M, N), out_dtype)]
    if resid is not None:
        in_specs += [pl.BlockSpec((tm, tn), lambda i, j, k: (i, j)), pl.BlockSpec((1, tn), lambda i, j, k: (0, j))]
        args += list(resid)
        out_specs.append(pl.BlockSpec((tm, tn), lambda i, j, k: (i, j)))
        out_shape.append(SDS((M, N), f32))
    outs = _pcall(
        body, grid=grid, in_specs=in_specs + h_specs, out_specs=out_specs + h_specs, out_shape=out_shape + h_shapes,
        scratch_shapes=([pltpu.VMEM((tm, tn), f32)] if nk > 1 else []) + h_sems, compiler_params=_cparams(VMEM_BIG), name=name)(*args, *h_args)
    n_main = len(out_shape)
    res = list(outs[:n_main]) + ([list(outs[n_main:])] if hosted is not None else [])
    return res[0] if len(res) == 1 else tuple(res)


def _row_specs(rows, tr):
    return [pl.BlockSpec((tr, w), lambda i, j, off=off: (i, off + j)) for (_, off, w) in rows]


def _rowwise_fwd(name, fn, rows, vecs, out_widths, out_dtypes, nc=1, tr=TR, also_transposed=False):
    T = rows[0][0].shape[0]
    n_in = len(rows) + len(vecs)
    n_out = len(out_widths)

    def body(*refs):
        vals = [r[...].astype(f32) for r in refs[:n_in]]
        res = fn(*vals)
        for o_ref, r in zip(refs[n_in:n_in + n_out], res):
            o_ref[...] = r.astype(o_ref.dtype)
        if also_transposed:
            refs[n_in + n_out][...] = res[0].T.astype(refs[n_in + n_out].dtype)

    in_specs = _row_specs(rows, tr) + [pl.BlockSpec(v.shape, lambda i, j: (0, 0)) for v in vecs]
    out_specs = [pl.BlockSpec((tr, w), lambda i, j: (i, j)) for w in out_widths]
    out_shape = [SDS((T, w * nc), dt) for w, dt in zip(out_widths, out_dtypes)]
    if also_transposed:
        out_specs.append(pl.BlockSpec((out_widths[0], tr), lambda i, j: (j, i)))
        out_shape.append(SDS((out_widths[0] * nc, T), out_dtypes[0]))
    return _pcall(body, grid=(T // tr, nc), in_specs=in_specs, out_specs=out_specs, out_shape=out_shape, name=name)(
        *[r[0] for r in rows], *vecs)


def _into_buffer(dest_buf, n_inputs, out_index):
    if dest_buf is None:
        return [], [], {}
    return [dest_buf], [pl.BlockSpec(memory_space=pl.ANY)], {n_inputs: out_index}


def _rowwise_bwd(name, fn, rows, vecs, cts, drow_dtypes, nc=1, tr=TR, add_to_first=None, dest=None):
    T = rows[0][0].shape[0]
    n_r, n_v, n_c = len(rows), len(vecs), len(cts)
    n_add = 0 if add_to_first is None else 1
    keep = [k for k, dt in enumerate(drow_dtypes) if dt is not None]
    members = [] if dest is None else list(dest[1])
    plain = [pos for pos in range(len(keep)) if pos not in members]
    n_dest = 1 if dest is not None else 0
    n_in = n_r + n_v + n_c + n_add

    def body(*refs):
        vals = [r[...].astype(f32) for r in refs[:n_in]]
        outs = refs[n_in + (1 if dest is not None and dest[0] is not None else 0):]
        i, j = pl.program_id(0), pl.program_id(1)
        _, vjp = jax.vjp(fn, *vals[:n_r + n_v])
        grads = vjp(tuple(vals[n_r + n_v:n_r + n_v + n_c]))
        got = []
        for pos, k in enumerate(keep):
            g = grads[k]
            if n_add and pos == 0:
                g = g + vals[n_in - 1]
            got.append(g)
        if dest is not None:
            outs[0][...] = jnp.concatenate([got[pos] for pos in members], axis=1).astype(outs[0].dtype)
        for q, pos in enumerate(plain):
            outs[n_dest + q][...] = got[pos].astype(outs[n_dest + q].dtype)
        vec_outs = outs[n_dest + len(plain):]

        @pl.when((i == 0) & (j == 0))
        def _():
            for q in range(n_v):
                vec_outs[q][...] = jnp.zeros_like(vec_outs[q])

        for q in range(n_v):
            vec_outs[q][...] += grads[n_r + q]

    extra = [] if add_to_first is None else [add_to_first]
    in_specs = (_row_specs(rows, tr) + [pl.BlockSpec(v.shape, lambda i, j: (0, 0)) for v in vecs]
                + _row_specs(cts, tr) + _row_specs(extra, tr))
    args = [r[0] for r in rows] + list(vecs) + [c[0] for c in cts] + [e[0] for e in extra]
    out_specs, out_shape, aliases = [], [], {}
    if dest is not None:
        width = sum(rows[keep[pos]][2] for pos in members)
        col = dest[2]
        b_args, b_specs, aliases = _into_buffer(dest[0], n_in, 0)
        args, in_specs = args + b_args, in_specs + b_specs
        out_specs.append(pl.BlockSpec((tr, width), lambda i, j: (i, col + j)))
        out_shape.append(SDS((T, PROJ_W), bf16))
    out_specs += [pl.BlockSpec((tr, rows[keep[pos]][2]), lambda i, j: (i, j)) for pos in plain]
    out_shape += [SDS((T, rows[keep[pos]][2] * nc), drow_dtypes[keep[pos]]) for pos in plain]
    out_specs += [pl.BlockSpec(v.shape, lambda i, j: (0, 0)) for v in vecs]
    out_shape += [SDS(v.shape, f32) for v in vecs]
    return _pcall(body, grid=(T // tr, nc), in_specs=in_specs, out_specs=out_specs, out_shape=out_shape,
                  input_output_aliases=aliases, name=name)(*args)


def _normmod_fn(x, w, sc, sh):
    y = x * lax.rsqrt(jnp.mean(x * x, axis=-1, keepdims=True) + EPS)
    return ((y * w) * (1.0 + sc) + sh,)


def _resid_fn(x, t, gt):
    return (x + gt * t,)


def _merge_fn(ga, gb, ya, yb):
    return (jax.nn.sigmoid(ga) * ya + jax.nn.sigmoid(gb) * yb,)


def _dngate_fn(o, z, w):
    y = o * lax.rsqrt(jnp.mean(o * o, axis=-1, keepdims=True) + EPS)
    return ((y * w) * (z * jax.nn.sigmoid(z)),)


def _conv_taps(x, w_ref, taps, buf):
    w = lambda s: w_ref[taps - 1 - s:taps - s, :]
    row = lax.broadcasted_iota(jnp.int32, (8, x.shape[1]), 0)
    x8 = x[0:8]
    acc, acc8 = x * w(0), x8 * w(0)
    for s in range(1, taps):
        acc = acc + pltpu.roll(x, s, 0) * w(s)
        acc8 = acc8 + jnp.where(row >= s, pltpu.roll(x8, s, 0), 0.0) * w(s)
    buf[...] = acc
    buf[0:8, :] = acc8
    return buf[...]


def _conv_taps_bwd(x, dy, w_ref, dw_ref, taps, buf):
    T = x.shape[0]
    w = lambda s: w_ref[taps - 1 - s:taps - s, :]
    row = lax.broadcasted_iota(jnp.int32, (8, x.shape[1]), 0)
    dy_first, dy_last = dy[0:8], dy[T - 8:T]
    dx, dx_last = dy * w(0), dy_last * w(0)
    dw_ref[taps - 1:taps, :] = jnp.sum(dy * x, axis=0, keepdims=True)
    for s in range(1, taps):
        dx = dx + pltpu.roll(dy, T - s, 0) * w(s)
        dx_last = dx_last + jnp.where(row < 8 - s, pltpu.roll(dy_last, 8 - s, 0), 0.0) * w(s)
        xr = pltpu.roll(x, s, 0)
        wrapped = jnp.sum(jnp.where(row < s, dy_first * xr[0:8], 0.0), axis=0, keepdims=True)
        dw_ref[taps - 1 - s:taps - s, :] = jnp.sum(dy * xr, axis=0, keepdims=True) - wrapped
    buf[...] = dx
    buf[T - 8:T, :] = dx_last
    return buf[...]


def _dn_act(y, normalize):
    s = y * jax.nn.sigmoid(y)
    if normalize:
        s = s * lax.rsqrt(jnp.sum(s * s, axis=-1, keepdims=True) + EPS)
    return s


def _dnconv_fwd(name, proj, cb, w, normalize):
    T = proj.shape[0]

    def body(x_ref, w_ref, o_ref, buf):
        o_ref[...] = _dn_act(_conv_taps(x_ref[...], w_ref, 4, buf), normalize)

    return _pcall(
        body, grid=(8,), in_specs=[pl.BlockSpec((T, LANE), lambda j: (0, cb + j)), pl.BlockSpec((4, LANE), lambda j: (0, cb + j))],
        out_specs=pl.BlockSpec((T, LANE), lambda j: (0, j)), out_shape=SDS((T, 1024), f32), scratch_shapes=[pltpu.VMEM((T, LANE), f32)],
        compiler_params=_cparams(VMEM_BIG), name=name)(proj, w)


def _dnconv_bwd(name, proj, cb, w, dout, normalize, dest_buf):
    T = proj.shape[0]
    b_args, b_specs, aliases = _into_buffer(dest_buf, 3, 0)

    def body(x_ref, w_ref, do_ref, *rest):
        dx_ref, dw_ref, buf, buf2 = rest[len(b_args):]
        x = x_ref[...]
        y = _conv_taps(x, w_ref, 4, buf)
        _, vjp = jax.vjp(functools.partial(_dn_act, normalize=normalize), y)
        (dy,) = vjp(do_ref[...])
        dx_ref[...] = _conv_taps_bwd(x, dy, w_ref, dw_ref, 4, buf2).astype(dx_ref.dtype)

    return _pcall(
        body, grid=(8,),
        in_specs=[pl.BlockSpec((T, LANE), lambda j: (0, cb + j)), pl.BlockSpec((4, LANE), lambda j: (0, cb + j)),
                  pl.BlockSpec((T, LANE), lambda j: (0, j))] + b_specs,
        out_specs=[pl.BlockSpec((T, LANE), lambda j: (0, cb + j)), pl.BlockSpec((4, LANE), lambda j: (0, j))],
        out_shape=[SDS((T, PROJ_W), bf16), SDS((4, 1024), f32)], scratch_shapes=[pltpu.VMEM((T, LANE), f32)] * 2,
        input_output_aliases=aliases, compiler_params=_cparams(VMEM_BIG), name=name)(proj, w, dout, *b_args)


def _ffn_point(a, lin):
    return a * jax.nn.sigmoid(a) * lin


def _ffnact_fwd(name, up, w, b):
    T = up.shape[0]
    nblk = D_FF // LANE

    def body(a_ref, l_ref, w_ref, b_ref, o_ref, buf):
        a = _conv_taps(a_ref[...], w_ref, 3, buf) + b_ref[...]
        o_ref[...] = _ffn_point(a, l_ref[...]).astype(o_ref.dtype)

    return _pcall(
        body, grid=(nblk,),
        in_specs=[pl.BlockSpec((T, LANE), lambda j: (0, j)), pl.BlockSpec((T, LANE), lambda j: (0, nblk + j)),
                  pl.BlockSpec((3, LANE), lambda j: (0, j)), pl.BlockSpec((1, LANE), lambda j: (0, j))],
        out_specs=pl.BlockSpec((T, LANE), lambda j: (0, j)), out_shape=SDS((T, D_FF), bf16), scratch_shapes=[pltpu.VMEM((T, LANE), f32)],
        compiler_params=_cparams(VMEM_BIG), name=name)(up, up, w, b)


def _ffnact_bwd(name, up, w, b, dmid):
    T = up.shape[0]
    nblk = D_FF // LANE

    def body(a_ref, l_ref, w_ref, b_ref, dm_ref, da_ref, dl_ref, dw_ref, db_ref, buf, buf2):
        x = a_ref[...]
        a = _conv_taps(x, w_ref, 3, buf) + b_ref[...]
        _, vjp = jax.vjp(_ffn_point, a, l_ref[...])
        da, dl = vjp(dm_ref[...].astype(f32))
        dl_ref[...] = dl.astype(dl_ref.dtype)
        db_ref[...] = jnp.sum(da, axis=0, keepdims=True)
        da_ref[...] = _conv_taps_bwd(x, da, w_ref, dw_ref, 3, buf2).astype(da_ref.dtype)

    col = lambda r: pl.BlockSpec((r, LANE), lambda j: (0, j))
    return _pcall(
        body, grid=(nblk,),
        in_specs=[col(T), pl.BlockSpec((T, LANE), lambda j: (0, nblk + j)), col(3), col(1), col(T)],
        out_specs=[col(T), col(T), col(3), col(1)],
        out_shape=[SDS((T, D_FF), bf16), SDS((T, D_FF), bf16), SDS((3, D_FF), f32), SDS((1, D_FF), f32)],
        scratch_shapes=[pltpu.VMEM((T, LANE), f32)] * 2, compiler_params=_cparams(VMEM_BIG), name=name)(up, up, w, b, dmid)


def _bmm(a, b, ca, cb, precision=HI):
    return lax.dot_general(a, b, (((ca,), (cb,)), ((0,), (0,))), precision=precision, preferred_element_type=f32)


def _make_bdot(ca, cb):
    def raw(x, y, cx, cy):
        return _bmm(x.astype(bf16), y.astype(bf16), cx, cy, precision=None)

    @jax.custom_vjp
    def f(a, b):
        return raw(a, b, ca, cb)

    def fwd(a, b):
        return raw(a, b, ca, cb), (a, b)

    def bwd(res, dy):
        a, b = res
        if (ca, cb) == (2, 1):
            return raw(dy, b, 2, 2), raw(a, dy, 1, 1)
        if (ca, cb) == (2, 2):
            return raw(dy, b, 2, 1), raw(dy, a, 1, 1)
        return raw(b, dy, 2, 2), raw(a, dy, 2, 1)

    f.defvjp(fwd, bwd)
    return f


_bdot_nn, _bdot_nt, _bdot_tn = _make_bdot(2, 1), _make_bdot(2, 2), _make_bdot(1, 1)


def _pieces(a, n):
    out, r = [], a
    for _ in range(n):
        p = r.astype(bf16)
        out.append(p)
        r = r - p.astype(f32)
    return out


def _bmm_split(x, y, cx, cy, nx=2, ny=2, order=1):
    xs, ys = _pieces(x, nx), _pieces(y, ny)
    acc = None
    for i in reversed(range(nx)):
        for j in reversed(range(ny)):
            if i + j <= order:
                t = _bmm(xs[i], ys[j], cx, cy, precision=None)
                acc = t if acc is None else acc + t
    return acc


@jax.custom_vjp
def _solve_apply(X, r):
    return _bmm_split(X, r, 2, 1)


def _solve_apply_fwd(X, r):
    return _bmm_split(X, r, 2, 1), (X, r)


def _solve_apply_bwd(res, dy):
    X, r = res
    return _bmm_split(dy, r, 2, 2), _bmm_split(X, dy, 1, 1)


_solve_apply.defvjp(_solve_apply_fwd, _solve_apply_bwd)


def _lower_ones(H, C):
    ri = lax.broadcasted_iota(jnp.int32, (H, C, C), 1)
    ci = lax.broadcasted_iota(jnp.int32, (H, C, C), 2)
    return (ri >= ci).astype(f32)


def _cumsum_rows_raw(G):
    return _bmm_split(_lower_ones(G.shape[0], G.shape[1]), G, 2, 1, nx=1, ny=3, order=2)


@jax.custom_vjp
def _cumsum_rows(G):
    return _cumsum_rows_raw(G)


def _cumsum_rows_fwd(G):
    return _cumsum_rows_raw(G), None


def _cumsum_rows_bwd(_, dy):
    return (_bmm_split(_lower_ones(dy.shape[0], dy.shape[1]), dy, 1, 1, nx=1, ny=3, order=2),)


_cumsum_rows.defvjp(_cumsum_rows_fwd, _cumsum_rows_bwd)


def _tri_inverse_raw(L):
    H, C, _ = L.shape
    ri = lax.broadcasted_iota(jnp.int32, (C, C), 0)
    ci = lax.broadcasted_iota(jnp.int32, (C, C), 1)
    eye = jnp.broadcast_to((ri == ci).astype(f32)[None], (H, C, C))
    Dg = jnp.where(((ri >> 3) == (ci >> 3))[None], L, 0.0)
    D2 = _bmm_split(Dg, Dg, 2, 1)
    X = _bmm_split(_bmm_split(eye - Dg, eye + D2, 2, 1), eye + _bmm_split(D2, D2, 2, 1), 2, 1)
    for lg in range(3, C.bit_length() - 1):
        same = (ri >> (lg + 1)) == (ci >> (lg + 1))
        lower_left = same & (((ri >> lg) & 1) == 1) & (((ci >> lg) & 1) == 0)
        X = X - _bmm_split(_bmm_split(X, jnp.where(lower_left[None], L, 0.0), 2, 1), X, 2, 1)
    return X


@jax.custom_vjp
def _tri_inverse(L):
    return _tri_inverse_raw(L)


def _tri_inverse_fwd(L):
    X = _tri_inverse_raw(L)
    return X, X


def _tri_inverse_bwd(X, dX):
    return (-_bmm_split(_bmm_split(X, dX, 1, 1), X, 2, 2),)


_tri_inverse.defvjp(_tri_inverse_fwd, _tri_inverse_bwd)


@jax.custom_vjp
def _tri_inverse_known(L, X):
    return X


def _tri_inverse_known_fwd(L, X):
    return X, X


def _tri_inverse_known_bwd(X, dX):
    return _tri_inverse_bwd(X, dX)[0], jnp.zeros_like(X)


_tri_inverse_known.defvjp(_tri_inverse_known_fwd, _tri_inverse_known_bwd)


DN_NC = 2


def _gdn_chunk(q, k, v, ab, alog, dtb, S, X_known=None, keep_X=False):
    H, C, _ = q.shape
    NC, NH = ab.shape[0], H // ab.shape[0]
    lane = lax.broadcasted_iota(jnp.int32, (H, C, LANE), 2)
    head = lax.broadcasted_iota(jnp.int32, (H, C, LANE), 0) & (NH - 1)
    abb = jnp.concatenate([jnp.broadcast_to(ab[c][None], (NH, C, LANE)) for c in range(NC)], axis=0)
    a = jnp.sum(jnp.where(lane == head, abb, 0.0), axis=2, keepdims=True)
    b = jnp.sum(jnp.where(lane == head + 8, abb, 0.0), axis=2, keepdims=True)
    pick = lax.broadcasted_iota(jnp.int32, (H, 1, LANE), 2) == (lax.broadcasted_iota(jnp.int32, (H, 1, LANE), 0) & (NH - 1))
    al = jnp.sum(jnp.where(pick, alog[None], 0.0), axis=2, keepdims=True)
    db = jnp.sum(jnp.where(pick, dtb[None], 0.0), axis=2, keepdims=True)
    g = -jnp.exp(al) * jax.nn.softplus(a + db)
    beta = jax.nn.sigmoid(b)
    ri = lax.broadcasted_iota(jnp.int32, (C, C), 0)
    ci = lax.broadcasted_iota(jnp.int32, (C, C), 1)
    G = jnp.broadcast_to(g, (H, C, LANE))
    gc = _cumsum_rows(G)
    gi = _cumsum_rows(jnp.broadcast_to(g, (H, C, C)))
    decay = jnp.exp(jnp.where((ri >= ci)[None], gi - jnp.swapaxes(gi, 1, 2), -jnp.inf))
    qs = q * (LANE ** -0.5)
    kb = k * beta
    L = jnp.where((ri > ci)[None], _bdot_nt(kb, k) * decay, 0.0)
    X = _tri_inverse(L) if X_known is None else _tri_inverse_known(L, X_known)
    egc = jnp.exp(gc)
    u = _solve_apply(X, v * beta)
    w = _solve_apply(X, kb * egc)
    qk = _bdot_nt(qs, k) * decay
    g_last = jnp.sum(G, axis=1, keepdims=True)
    k_dec = k * jnp.exp(g_last - gc)
    q_dec = qs * egc
    e_last = jnp.exp(g_last)
    outs = []
    for c in range(NC):
        sl = slice(c * NH, (c + 1) * NH)
        v_new = u[sl] - _bdot_nn(w[sl], S)
        outs.append(_bdot_nn(q_dec[sl], S) + _bdot_nn(qk[sl], v_new))
        S = S * e_last[sl] + _bdot_tn(k_dec[sl], v_new)
    o = jnp.concatenate(outs, axis=0)
    return (o, S, X) if keep_X else (o, S)


def _heads(ref):
    return jnp.stack([ref[DN_C * c:DN_C * (c + 1), LANE * h:LANE * (h + 1)] for c in range(DN_NC) for h in range(8)], axis=0)


def _put_heads(ref, val):
    for c in range(DN_NC):
        for h in range(8):
            ref[DN_C * c:DN_C * (c + 1), LANE * h:LANE * (h + 1)] = val[8 * c + h]


def _chunk_rows(ref):
    return jnp.stack([ref[DN_C * c:DN_C * (c + 1), :] for c in range(DN_NC)], axis=0)


def _hosted_parts(hosted):
    if hosted is None:
        return [], [], [], []
    n = len(hosted["arrays"])
    return list(hosted["arrays"]), [HBM_SPEC] * n, list(hosted["out_shape"]), _dma_sems(hosted["n_sems"])


def _gdn_fwd(name, q, k, v, proj, alog, dtb, hosted=None):
    T = q.shape[0]
    R = DN_C * DN_NC
    N = T // R
    h_args, h_specs, h_shapes, h_sems = _hosted_parts(hosted)
    nh = len(h_args)

    def body(q_ref, k_ref, v_ref, ab_ref, al_ref, dt_ref, *rest):
        h_ins, (o_ref, sall_ref, xinv_ref), h_outs, s_scr, sems = rest[:nh], rest[nh:nh + 3], rest[nh + 3:2 * nh + 3], rest[2 * nh + 3], rest[2 * nh + 4:]
        step = pl.program_id(0)

        @pl.when(step == 0)
        def _():
            s_scr[...] = jnp.zeros_like(s_scr)
            if hosted is not None:
                hosted["start"](h_ins, h_outs, *sems)

        S = s_scr[...]
        sall_ref[...] = S
        o, S_new, X = _gdn_chunk(_heads(q_ref), _heads(k_ref), _heads(v_ref), _chunk_rows(ab_ref), al_ref[...], dt_ref[...], S, keep_X=True)
        _put_heads(o_ref, o)
        s_scr[...] = S_new
        xinv_ref[...] = X

        if hosted is not None:
            @pl.when(step == N - 1)
            def _():
                hosted["finish"](h_ins, h_outs, *sems)

    blk = pl.BlockSpec((R, 8 * LANE), lambda n: (n, 0))
    vec = pl.BlockSpec((1, LANE), lambda n: (0, 0))
    state = pl.BlockSpec((None, 8, LANE, LANE), lambda n: (n, 0, 0, 0))
    xinv = pl.BlockSpec((None, 8 * DN_NC, DN_C, DN_C), lambda n: (n, 0, 0, 0))
    outs = _pcall(
        body, grid=(N,), in_specs=[blk, blk, blk, pl.BlockSpec((R, LANE), lambda n: (n, CB_AB)), vec, vec] + h_specs,
        out_specs=[blk, state, xinv] + h_specs,
        out_shape=[SDS((T, 1024), f32), SDS((N, 8, LANE, LANE), f32), SDS((N, 8 * DN_NC, DN_C, DN_C), f32)] + h_shapes,
        scratch_shapes=[pltpu.VMEM((8, LANE, LANE), f32)] + h_sems, name=name)(q, k, v, proj, alog, dtb, *h_args)
    return outs[0], outs[1], outs[2], list(outs[3:])


def _gdn_bwd(name, q, k, v, proj, alog, dtb, sall, xinv, do, hosted=None):
    T = q.shape[0]
    R = DN_C * DN_NC
    N = T // R
    h_args, h_specs, h_shapes, h_sems = _hosted_parts(hosted)
    nh = len(h_args)

    def body(q_ref, k_ref, v_ref, ab_ref, al_ref, dt_ref, s_ref, x_ref, do_ref, *rest):
        h_ins, h_outs, ds_scr, sems = rest[:nh], rest[nh + 6:2 * nh + 6], rest[2 * nh + 6], rest[2 * nh + 7:]
        dq_ref, dk_ref, dv_ref, dab_ref, dal_ref, ddt_ref = rest[nh:nh + 6]
        step = pl.program_id(0)

        @pl.when(step == 0)
        def _():
            ds_scr[...] = jnp.zeros_like(ds_scr)
            dal_ref[...] = jnp.zeros_like(dal_ref)
            ddt_ref[...] = jnp.zeros_like(ddt_ref)
            if hosted is not None:
                hosted["start"](h_ins, h_outs, *sems)

        _, vjp = jax.vjp(functools.partial(_gdn_chunk, X_known=x_ref[...]), _heads(q_ref), _heads(k_ref), _heads(v_ref), _chunk_rows(ab_ref),
                         al_ref[...], dt_ref[...], s_ref[...])
        dq, dk, dv, dab, dal, ddt, dS = vjp((_heads(do_ref), ds_scr[...]))
        _put_heads(dq_ref, dq)
        _put_heads(dk_ref, dk)
        _put_heads(dv_ref, dv)
        ds_scr[...] = dS
        for c in range(DN_NC):
            dab_ref[DN_C * c:DN_C * (c + 1), :] = dab[c]
        dal_ref[...] += dal
        ddt_ref[...] += ddt

        if hosted is not None:
            @pl.when(step == N - 1)
            def _():
                hosted["finish"](h_ins, h_outs, *sems)

    blk = pl.BlockSpec((R, 8 * LANE), lambda n: (N - 1 - n, 0))
    vec = pl.BlockSpec((1, LANE), lambda n: (0, 0))
    state = pl.BlockSpec((None, 8, LANE, LANE), lambda n: (N - 1 - n, 0, 0, 0))
    outs = _pcall(
        body, grid=(N,),
        in_specs=[blk, blk, blk, pl.BlockSpec((R, LANE), lambda n: (N - 1 - n, CB_AB)), vec, vec, state,
                  pl.BlockSpec((None, 8 * DN_NC, DN_C, DN_C), lambda n: (N - 1 - n, 0, 0, 0)), blk] + h_specs,
        out_specs=[blk, blk, blk, pl.BlockSpec((R, LANE), lambda n: (N - 1 - n, 0)), vec, vec] + h_specs,
        out_shape=[SDS((T, 1024), f32)] * 3 + [SDS((T, LANE), f32), SDS((1, LANE), f32), SDS((1, LANE), f32)] + h_shapes,
        scratch_shapes=[pltpu.VMEM((8, LANE, LANE), f32)] + h_sems, name=name)(q, k, v, proj, alog, dtb, sall, xinv, do, *h_args)
    return tuple(outs[:6]), list(outs[6:])


def _segmean_raw(x2, bd):
    return jnp.concatenate([_dot(x2[:, LANE * j:LANE * (j + 1)], bd, 1, 0) for j in range(x2.shape[1] // LANE)], axis=1)


@jax.custom_vjp
def _segmean(x2, bd):
    return _segmean_raw(x2, bd)


def _segmean_fwd(x2, bd):
    return _segmean_raw(x2, bd), bd


def _segmean_bwd(bd, dy):
    return _segmean_raw(dy, bd), jnp.zeros_like(bd)


_segmean.defvjp(_segmean_fwd, _segmean_bwd)


def _qknorm_fn(x, w, bd):
    return x * lax.rsqrt(_segmean(x * x, bd) + EPS) * w


def _rope_apply(xn, c, s1, s2):
    W = xn.shape[1]
    return xn * c + pltpu.roll(xn, W - 8, 1) * s1 + pltpu.roll(xn, 8, 1) * s2


def _rope_apply_t(d, c, s1, s2):
    W = d.shape[1]
    return d * c + pltpu.roll(d * s1, 8, 1) + pltpu.roll(d * s2, W - 8, 1)


def _rope_tiles(refs, width):
    return [jnp.tile(r[...], (1, width // LANE)) for r in refs]


def _qkprep_fwd(name, proj, wb, width, w, bd, tabs):
    T = proj.shape[0]
    tr = _pick(T, (256, 128))

    def body(x_ref, w_ref, bd_ref, c_ref, s1_ref, s2_ref, o_ref):
        xn = _qknorm_fn(x_ref[...], w_ref[...], bd_ref[...])
        o_ref[...] = _rope_apply(xn, *_rope_tiles((c_ref, s1_ref, s2_ref), width))

    row0 = pl.BlockSpec((tr, width), lambda i: (i, 0))
    tab = pl.BlockSpec((tr, LANE), lambda i: (i, 0))
    full = lambda a: pl.BlockSpec(a.shape, lambda i: (0, 0))
    return _pcall(
        body, grid=(T // tr,), in_specs=[pl.BlockSpec((tr, width), lambda i: (i, wb)), full(w), full(bd), tab, tab, tab],
        out_specs=row0, out_shape=SDS((T, width), f32), name=name)(proj, w, bd, *tabs)


def _qkprep_bwd(name, proj, wb, width, w, bd, tabs, dout, dest_buf="none"):
    T = proj.shape[0]
    tr = _pick(T, (256, 128))
    into = not isinstance(dest_buf, str)
    b_args, b_specs, aliases = _into_buffer(dest_buf, 7, 0) if into else ([], [], {})

    def body(x_ref, w_ref, bd_ref, c_ref, s1_ref, s2_ref, do_ref, *rest):
        dx_ref, dw_ref = rest[len(b_args):]
        i = pl.program_id(0)
        dxn = _rope_apply_t(do_ref[...], *_rope_tiles((c_ref, s1_ref, s2_ref), width))
        bd = bd_ref[...]
        _, vjp = jax.vjp(lambda x, w_: _qknorm_fn(x, w_, bd), x_ref[...], w_ref[...])
        dx, dw = vjp(dxn)
        dx_ref[...] = dx.astype(dx_ref.dtype)

        @pl.when(i == 0)
        def _():
            dw_ref[...] = jnp.zeros_like(dw_ref)

        dw_ref[...] += dw

    row0 = pl.BlockSpec((tr, width), lambda i: (i, 0))
    tab = pl.BlockSpec((tr, LANE), lambda i: (i, 0))
    full = lambda a: pl.BlockSpec(a.shape, lambda i: (0, 0))
    dx_spec = pl.BlockSpec((tr, width), lambda i: (i, wb)) if into else row0
    dx_shape = SDS((T, PROJ_W), bf16) if into else SDS((T, width), bf16)
    return _pcall(
        body, grid=(T // tr,), in_specs=[pl.BlockSpec((tr, width), lambda i: (i, wb)), full(w), full(bd), tab, tab, tab, row0] + b_specs,
        out_specs=[dx_spec, full(w)], out_shape=[dx_shape, SDS(w.shape, f32)], input_output_aliases=aliases, name=name)(
            proj, w, bd, *tabs, dout, *b_args)


def _make_dot16(ca, cb):
    def raw(x, y, cx, cy):
        return _dot(x.astype(bf16), y.astype(bf16), cx, cy, precision=None)

    @jax.custom_vjp
    def f(a, b):
        return raw(a, b, ca, cb)

    def fwd(a, b):
        return raw(a, b, ca, cb), (a, b)

    def bwd(res, dy):
        a, b = res
        if (ca, cb) == (1, 0):
            return raw(dy, b, 1, 1), raw(a, dy, 0, 0)
        return raw(dy, b, 1, 0), raw(dy, a, 0, 0)

    f.defvjp(fwd, bwd)
    return f


_dot16_nn, _dot16_nt = _make_dot16(1, 0), _make_dot16(1, 1)


def _attn_bias():
    qi = jnp.arange(8 * SWA_B) % SWA_B
    kj = jnp.arange(2 * SWA_B)
    rel = qi[:, None] + SWA_B - kj[None, :]
    valid = (rel >= 0) & (rel < SWA_B)
    neg = jnp.float32(-jnp.inf)
    return jnp.stack([jnp.where(valid & (kj[None, :] >= SWA_B), 0.0, neg), jnp.where(valid, 0.0, neg)]).astype(f32)


def _attn_group(qg, kb, vb, sinks, bias, hk):
    R = qg.shape[0]
    s = _dot16_nt(qg, kb) * 0.125 + bias
    head = (lax.broadcasted_iota(jnp.int32, (R, LANE), 0) >> 7) + 8 * hk
    lane = lax.broadcasted_iota(jnp.int32, (R, LANE), 1)
    sink = jnp.sum(jnp.where(lane == head, jnp.broadcast_to(sinks, (R, LANE)), 0.0), axis=1, keepdims=True)
    m = lax.stop_gradient(jnp.maximum(jnp.max(s, axis=1, keepdims=True), sink))
    p = jnp.exp(s - m)
    denom = jnp.sum(p, axis=1, keepdims=True) + jnp.exp(sink - m)
    return _dot16_nn(p / denom, vb)


def _group_rows(ref, hk):
    return jnp.concatenate([ref[:, 64 * (8 * hk + g):64 * (8 * hk + g + 1)] for g in range(8)], axis=0)


def _put_group(ref, hk, val):
    for g in range(8):
        ref[:, 64 * (8 * hk + g):64 * (8 * hk + g + 1)] = val[SWA_B * g:SWA_B * (g + 1)].astype(ref.dtype)


def _attn_specs():
    qs = pl.BlockSpec((SWA_B, 1024), lambda i: (i, 0))
    cur = pl.BlockSpec((SWA_B, LANE), lambda i: (i, 0))
    prev = pl.BlockSpec((SWA_B, LANE), lambda i: (jnp.maximum(i - 1, 0), 0))
    vcur = pl.BlockSpec((SWA_B, LANE), lambda i: (i, CB_SWV))
    vprev = pl.BlockSpec((SWA_B, LANE), lambda i: (jnp.maximum(i - 1, 0), CB_SWV))
    vec = pl.BlockSpec((1, LANE), lambda i: (0, 0))
    bias = pl.BlockSpec((None, 8 * SWA_B, 2 * SWA_B), lambda i: (jnp.minimum(i, 1), 0, 0))
    return qs, cur, prev, vcur, vprev, vec, bias


def _hosted_edge(hosted, which, h_ins, h_outs, sems, at):
    if hosted is None:
        return

    @pl.when(at)
    def _():
        hosted[which](h_ins, h_outs, *sems)


def _attn_fwd(name, sq, sk, proj, sinks, bias, hosted=None):
    T = sq.shape[0]
    nb = T // SWA_B
    h_args, h_specs, h_shapes, h_sems = _hosted_parts(hosted)
    nh = len(h_args)

    def body(q_ref, kp_ref, kc_ref, vp_ref, vc_ref, sk_ref, b_ref, *rest):
        h_ins, o_ref, h_outs, sems = rest[:nh], rest[nh], rest[nh + 1:2 * nh + 1], rest[2 * nh + 1:]
        _hosted_edge(hosted, "start", h_ins, h_outs, sems, pl.program_id(0) == 0)
        sinks_v, bias_v = sk_ref[...], b_ref[...]
        for hk in range(2):
            ks = slice(64 * hk, 64 * hk + 64)
            kb = jnp.concatenate([kp_ref[:, ks], kc_ref[:, ks]], axis=0)
            vb = jnp.concatenate([vp_ref[:, ks], vc_ref[:, ks]], axis=0)
            _put_group(o_ref, hk, _attn_group(_group_rows(q_ref, hk), kb, vb, sinks_v, bias_v, hk))
        _hosted_edge(hosted, "finish", h_ins, h_outs, sems, pl.program_id(0) == nb - 1)

    qs, cur, prev, vcur, vprev, vec, bspec = _attn_specs()
    outs = _pcall(body, grid=(nb,), in_specs=[qs, prev, cur, vprev, vcur, vec, bspec] + h_specs, out_specs=[qs] + h_specs,
                  out_shape=[SDS((T, 1024), bf16)] + h_shapes, scratch_shapes=h_sems, name=name)(sq, sk, sk, proj, proj, sinks, bias, *h_args)
    return outs[0], list(outs[1:])


def _attn_bwd(name, sq, sk, proj, sinks, bias, do, hosted=None):
    T = sq.shape[0]
    nb = T // SWA_B
    h_args, h_specs, h_shapes, h_sems = _hosted_parts(hosted)
    nh = len(h_args)

    def body(q_ref, kp_ref, kc_ref, vp_ref, vc_ref, sk_ref, b_ref, do_ref, *rest):
        h_ins, h_outs, sems = rest[:nh], rest[nh + 6:2 * nh + 6], rest[2 * nh + 6:]
        dq_ref, dkp_ref, dkc_ref, dvp_ref, dvc_ref, dsk_ref = rest[nh:nh + 6]
        _hosted_edge(hosted, "start", h_ins, h_outs, sems, pl.program_id(0) == 0)

        @pl.when(pl.program_id(0) == 0)
        def _():
            dsk_ref[...] = jnp.zeros_like(dsk_ref)

        sinks_v, bias_v = sk_ref[...], b_ref[...]
        dsk = jnp.zeros((1, LANE), f32)
        for hk in range(2):
            ks = slice(64 * hk, 64 * hk + 64)
            kb = jnp.concatenate([kp_ref[:, ks], kc_ref[:, ks]], axis=0)
            vb = jnp.concatenate([vp_ref[:, ks], vc_ref[:, ks]], axis=0)
            _, vjp = jax.vjp(functools.partial(_attn_group, bias=bias_v, hk=hk), _group_rows(q_ref, hk), kb, vb, sinks_v)
            dq, dkb, dvb, ds_ = vjp(_group_rows(do_ref, hk))
            _put_group(dq_ref, hk, dq)
            dsk = dsk + ds_
            dkp_ref[:, ks] = dkb[:SWA_B]
            dkc_ref[:, ks] = dkb[SWA_B:]
            dvp_ref[:, ks] = dvb[:SWA_B]
            dvc_ref[:, ks] = dvb[SWA_B:]
        dsk_ref[...] += dsk
        _hosted_edge(hosted, "finish", h_ins, h_outs, sems, pl.program_id(0) == nb - 1)

    qs, cur, prev, vcur, vprev, vec, bspec = _attn_specs()
    outs = _pcall(
        body, grid=(nb,), in_specs=[qs, prev, cur, vprev, vcur, vec, bspec, qs] + h_specs, out_specs=[qs, cur, cur, cur, cur, vec] + h_specs,
        out_shape=[SDS((T, 1024), f32)] + [SDS((T, LANE), f32)] * 4 + [SDS((1, LANE), f32)] + h_shapes, scratch_shapes=h_sems,
        name=name)(sq, sk, sk, proj, proj, sinks, bias, do, *h_args)
    return tuple(outs[:6]), list(outs[6:])


def _shift_add(name, cur, prev, out_dtype):
    T = cur.shape[0]

    def body(c_ref, p_ref, o_ref):
        o_ref[0:T - SWA_B, :] = (c_ref[0:T - SWA_B, :] + p_ref[SWA_B:T, :]).astype(o_ref.dtype)
        o_ref[T - SWA_B:T, :] = c_ref[T - SWA_B:T, :].astype(o_ref.dtype)

    return _pcall(body, out_shape=SDS((T, LANE), out_dtype), name=name)(cur, prev)


def _loss(name, y, tgt):
    T = y.shape[0]

    def body(y_ref, t_ref, l_ref, dy_ref):
        @pl.when(pl.program_id(0) == 0)
        def _():
            l_ref[...] = jnp.zeros_like(l_ref)

        d = y_ref[...] - t_ref[...]
        l_ref[...] += jnp.sum(d * d) * (0.5 / D)
        dy_ref[...] = d * (1.0 / D)

    row = pl.BlockSpec((TR, D), lambda i: (i, 0))
    return _pcall(body, grid=(T // TR,), in_specs=[row, row], out_specs=[pl.BlockSpec((8, LANE), lambda i: (0, 0)), row],
                  out_shape=[SDS((8, LANE), f32), SDS((T, D), f32)], name=name)(y, tgt)


def _adamw(name, w, g, m, v):
    shape = w.shape
    C = shape[-1]
    R = int(np.prod(shape[:-1]))
    tr = _pick(R, (128, 64, 16, 8))
    bc1 = np.float32(1.0 - ADAM_B1 ** ADAM_STEP)
    bc2 = np.float32(1.0 - ADAM_B2 ** ADAM_STEP)

    def body(w_ref, g_ref, m_ref, v_ref, d_ref, mo_ref, vo_ref):
        g_ = g_ref[...]
        m_ = ADAM_B1 * m_ref[...] + (1.0 - ADAM_B1) * g_
        v_ = ADAM_B2 * v_ref[...] + (1.0 - ADAM_B2) * (g_ * g_)
        d_ref[...] = -ADAM_LR * ((m_ / bc1) / (jnp.sqrt(v_ / bc2) + ADAM_EPS) + ADAM_WD * w_ref[...])
        mo_ref[...] = m_
        vo_ref[...] = v_

    blk = pl.BlockSpec((tr, C), lambda i: (i, 0))
    outs = _pcall(body, grid=(R // tr,), in_specs=[blk] * 4, out_specs=[blk] * 3, out_shape=[SDS((R, C), f32)] * 3,
                  compiler_params=_cparams(VMEM_BIG), name=name)(*[t.reshape(R, C) for t in (w, g, m, v)])
    return [o.reshape(shape) for o in outs]


def _silu_rows(name, x):
    def body(x_ref, o_ref):
        t = x_ref[...]
        o_ref[...] = (t * jax.nn.sigmoid(t)).astype(o_ref.dtype)

    return _pcall(body, out_shape=SDS(x.shape, bf16), name=name)(x)


def _sum_leading(name, x):
    n = x.shape[0]

    def body(x_ref, o_ref):
        acc = x_ref[0]
        for k in range(1, n):
            acc = acc + x_ref[k]
        o_ref[...] = acc

    tr = x.shape[1] if x.size * 4 <= 8 * 2 ** 20 else _pick(x.shape[1], (COMM_TR, 8))
    return _pcall(body, grid=(x.shape[1] // tr,), in_specs=[pl.BlockSpec((n, tr, x.shape[2]), lambda i: (0, i, 0))],
                  out_specs=pl.BlockSpec((tr, x.shape[2]), lambda i: (i, 0)), out_shape=SDS(x.shape[1:], x.dtype), name=name)(x)


HBM_SPEC = pl.BlockSpec(memory_space=pltpu.HBM)


def _position():
    x, y, c = lax.axis_index("x"), lax.axis_index("y"), lax.axis_index("c")
    return x, y, c, [(1 - x, y), (x, 1 - y), (1 - x, 1 - y)]


def _remote(src, dst, send_sems, recv_sems, k, to):
    return pltpu.make_async_remote_copy(src_ref=src, dst_ref=dst, send_sem=send_sems.at[k], recv_sem=recv_sems.at[k],
                                        device_id=to, device_id_type=MESH)


def _allgather_all(name, buf):
    r, W = buf.shape

    def body(in_ref, out_ref, send_sems, recv_sems, local_sem):
        x, y, c, _ = _position()
        me = 4 * x + 2 * y + c
        mine = pltpu.make_async_copy(in_ref, out_ref.at[me], local_sem)
        mine.start()
        peers = []
        for mk in range(1, 8):
            mx, my, mc = (mk >> 2) & 1, (mk >> 1) & 1, mk & 1
            px = 1 - x if mx else x
            py = 1 - y if my else y
            pc = 1 - c if mc else c
            peers.append((px, py, pc))
        sends = [_remote(in_ref, out_ref.at[me], send_sems, recv_sems, k, p) for k, p in enumerate(peers)]
        for cp in sends:
            cp.start()
        for k, (px, py, pc) in enumerate(peers):
            slot = out_ref.at[4 * px + 2 * py + pc]
            _remote(slot, slot, send_sems, recv_sems, k, (px, py, pc)).wait_recv()
        for cp in sends:
            cp.wait_send()
        mine.wait()

    return _pcall(body, in_specs=[HBM_SPEC], out_specs=HBM_SPEC, out_shape=SDS((8, r, W), buf.dtype),
                  scratch_shapes=[pltpu.SemaphoreType.DMA((7,)), pltpu.SemaphoreType.DMA((7,)), pltpu.SemaphoreType.DMA], name=name)(buf)


def _dma_sems(n):
    return [pltpu.SemaphoreType.DMA((n,)), pltpu.SemaphoreType.DMA((n,))]


def _gather_chips(name, shards):
    n = len(shards)

    def body(*refs):
        ins, outs = refs[:n], refs[n:2 * n]
        send_sems, recv_sems = refs[2 * n:]
        x, y, c, chips = _position()
        me = 2 * x + y
        sib = (x, y, 1 - c)
        sends, halves = [], []
        for i in range(n):
            rh = ins[i].shape[0] // 2
            halves.append((pl.ds(pl.multiple_of(c * rh, 16), rh), pl.ds(pl.multiple_of((1 - c) * rh, 16), rh)))
        for i in range(n):
            for j, (cx, cy) in enumerate(chips):
                cp = _remote(ins[i].at[halves[i][0]], outs[i].at[me, halves[i][0]], send_sems, recv_sems, 6 * i + j, (cx, cy, c))
                cp.start()
                sends.append(cp)
        for j, (cx, cy) in enumerate(chips):
            for i in range(n):
                rows = outs[i].at[2 * cx + cy, halves[i][0]]
                _remote(rows, rows, send_sems, recv_sems, 6 * i + j, (cx, cy, c)).wait_recv()
                cp = _remote(rows, rows, send_sems, recv_sems, 6 * i + 3 + j, sib)
                cp.start()
                sends.append(cp)
        for j, (cx, cy) in enumerate(chips):
            for i in range(n):
                rows = outs[i].at[2 * cx + cy, halves[i][1]]
                _remote(rows, rows, send_sems, recv_sems, 6 * i + 3 + j, sib).wait_recv()
        for cp in sends:
            cp.wait_send()

    return _pcall(body, in_specs=[HBM_SPEC] * n, out_specs=[HBM_SPEC] * n, out_shape=[SDS((4,) + s.shape, s.dtype) for s in shards],
                  scratch_shapes=_dma_sems(6 * n), name=name)(*shards)


def _swap_halves_multi(name, slots):
    n = len(slots)

    def body(*refs):
        ins, outs = refs[:n], refs[n:2 * n]
        send_sems, recv_sems = refs[2 * n:]
        x, y, c, _ = _position()
        cps = []
        for i in range(n):
            rh = ins[i].shape[1] // 2
            ohalf = pl.ds(pl.multiple_of((1 - c) * rh, 8), rh)
            cp = _remote(ins[i].at[:, ohalf, :], outs[i], send_sems, recv_sems, i, (x, y, 1 - c))
            cp.start()
            cps.append(cp)
        for cp in cps:
            cp.wait()

    return _pcall(body, in_specs=[HBM_SPEC] * n, out_specs=[HBM_SPEC] * n,
                  out_shape=[SDS((4, s.shape[1] // 2, s.shape[2]), s.dtype) for s in slots], scratch_shapes=_dma_sems(n), name=name)(*slots)


def _pair_add(name, g4, b1, c):
    _, R, W = g4.shape
    tr = _pick(R // 2, (256, 128, 32, 16))
    nblk = (R // 2) // tr

    def body(c_ref, g_ref, b_ref, o_ref):
        o_ref[...] = (g_ref[...] + b_ref[...]).astype(o_ref.dtype)

    grid_spec = pltpu.PrefetchScalarGridSpec(
        num_scalar_prefetch=1, grid=(4, nblk),
        in_specs=[pl.BlockSpec((None, tr, W), lambda s, i, c_ref: (s, c_ref[0] * nblk + i, 0)),
                  pl.BlockSpec((None, tr, W), lambda s, i, c_ref: (s, i, 0))],
        out_specs=pl.BlockSpec((None, tr, W), lambda s, i, c_ref: (s, i, 0)))
    return _pcall(body, grid_spec=grid_spec, out_shape=SDS((4, R // 2, W), bf16), name=name)(c.reshape(1), g4, b1)


def _scatter_chips_multi(name, ps):
    n = len(ps)

    def body(*refs):
        ins, outs = refs[:n], refs[n:2 * n]
        send_sems, recv_sems = refs[2 * n:]
        x, y, c, chips = _position()
        me = 2 * x + y
        sends = []
        for i in range(n):
            for j, (cx, cy) in enumerate(chips):
                cp = _remote(ins[i].at[2 * cx + cy], outs[i].at[me], send_sems, recv_sems, 3 * i + j, (cx, cy, c))
                cp.start()
                sends.append(cp)
        for i in range(n):
            for j, (cx, cy) in enumerate(chips):
                slot = outs[i].at[2 * cx + cy]
                _remote(slot, slot, send_sems, recv_sems, 3 * i + j, (cx, cy, c)).wait_recv()
        for cp in sends:
            cp.wait_send()

    return _pcall(body, in_specs=[HBM_SPEC] * n, out_specs=[HBM_SPEC] * n, out_shape=[SDS(p.shape, p.dtype) for p in ps],
                  scratch_shapes=_dma_sems(3 * n), name=name)(*ps)


def _sum_chips(name, p4, b2, chip, c):
    _, Rh, W = p4.shape
    tr = _pick(Rh, (256, 128, 32, 16))
    nblk = Rh // tr

    def body(m_ref, c_ref, own_ref, r1_ref, r2_ref, r3_ref, o_ref):
        o_ref[...] = ((own_ref[...].astype(f32) + r1_ref[...].astype(f32)) + r2_ref[...].astype(f32)) + r3_ref[...].astype(f32)

    other = lambda k: pl.BlockSpec((None, tr, W), lambda i, m_ref, c_ref: (m_ref[0] ^ k, i, 0))
    grid_spec = pltpu.PrefetchScalarGridSpec(
        num_scalar_prefetch=2, grid=(nblk,),
        in_specs=[pl.BlockSpec((None, tr, W), lambda i, m_ref, c_ref: (m_ref[0], i, 0)), other(1), other(2), other(3)],
        out_specs=pl.BlockSpec((tr, W), lambda i, m_ref, c_ref: (c_ref[0] * nblk + i, 0)))
    return _pcall(body, grid_spec=grid_spec, out_shape=SDS((2 * Rh, W), f32), name=name)(chip.reshape(1), c.reshape(1), p4, b2, b2, b2)


def _join_halves(name, fulls):
    n = len(fulls)

    def body(*refs):
        outs = refs[n:2 * n]
        send_sems, recv_sems = refs[2 * n:]
        x, y, c, _ = _position()
        cps = []
        for i in range(n):
            rh = outs[i].shape[0] // 2
            mine = outs[i].at[pl.ds(pl.multiple_of(c * rh, 8), rh)]
            theirs = outs[i].at[pl.ds(pl.multiple_of((1 - c) * rh, 8), rh)]
            cp = _remote(mine, mine, send_sems, recv_sems, i, (x, y, 1 - c))
            cp.start()
            cps.append((cp, _remote(theirs, theirs, send_sems, recv_sems, i, (x, y, 1 - c))))
        for cp, back in cps:
            back.wait_recv()
            cp.wait_send()

    return _pcall(body, in_specs=[HBM_SPEC] * n, out_specs=[HBM_SPEC] * n, out_shape=[SDS(r.shape, r.dtype) for r in fulls],
                  input_output_aliases={i: i for i in range(n)}, scratch_shapes=_dma_sems(n), name=name)(*fulls)


def _hosted_gather(shards):
    n = len(shards)

    def half(ref_rows, c):
        rh = ref_rows // 2
        return pl.ds(pl.multiple_of(c * rh, 16), rh)

    def start(ins, outs, send_sems, recv_sems):
        x, y, c, chips = _position()
        me = 2 * x + y
        for i in range(n):
            rows = half(ins[i].shape[0], c)
            for j, (cx, cy) in enumerate(chips):
                _remote(ins[i].at[rows], outs[i].at[me, rows], send_sems, recv_sems, 3 * i + j, (cx, cy, c)).start()

    def finish(ins, outs, send_sems, recv_sems):
        x, y, c, chips = _position()
        me = 2 * x + y
        for i in range(n):
            rows = half(ins[i].shape[0], c)
            for j, (cx, cy) in enumerate(chips):
                _remote(ins[i].at[rows], outs[i].at[2 * cx + cy, rows], send_sems, recv_sems, 3 * i + j, (cx, cy, c)).wait_recv()
        for i in range(n):
            rows = half(ins[i].shape[0], c)
            for j, (cx, cy) in enumerate(chips):
                _remote(ins[i].at[rows], outs[i].at[me, rows], send_sems, recv_sems, 3 * i + j, (cx, cy, c)).wait_send()

    return {"arrays": shards, "out_shape": [SDS((4,) + s.shape, s.dtype) for s in shards], "n_sems": 3 * n, "start": start, "finish": finish}


def _gather_forward(name, gathered):
    n = len(gathered)

    def body(*refs):
        outs = refs[n:2 * n]
        send_sems, recv_sems = refs[2 * n:]
        x, y, c, chips = _position()
        sib = (x, y, 1 - c)
        sends = []
        for i in range(n):
            rh = outs[i].shape[1] // 2
            mine = pl.ds(pl.multiple_of(c * rh, 16), rh)
            for j, (cx, cy) in enumerate(chips):
                rows = outs[i].at[2 * cx + cy, mine]
                cp = _remote(rows, rows, send_sems, recv_sems, 3 * i + j, sib)
                cp.start()
                sends.append(cp)
        for i in range(n):
            rh = outs[i].shape[1] // 2
            theirs = pl.ds(pl.multiple_of((1 - c) * rh, 16), rh)
            for j, (cx, cy) in enumerate(chips):
                rows = outs[i].at[2 * cx + cy, theirs]
                _remote(rows, rows, send_sems, recv_sems, 3 * i + j, sib).wait_recv()
        for cp in sends:
            cp.wait_send()

    return _pcall(body, in_specs=[HBM_SPEC] * n, out_specs=[HBM_SPEC] * n, out_shape=[SDS(g.shape, g.dtype) for g in gathered],
                  input_output_aliases={i: i for i in range(n)}, scratch_shapes=_dma_sems(3 * n), name=name)(*gathered)


def _hosted_scatter(ps):
    n = len(ps)

    def start(ins, outs, send_sems, recv_sems):
        x, y, c, chips = _position()
        me = 2 * x + y
        for i in range(n):
            for j, (cx, cy) in enumerate(chips):
                _remote(ins[i].at[2 * cx + cy], outs[i].at[me], send_sems, recv_sems, 3 * i + j, (cx, cy, c)).start()

    def finish(ins, outs, send_sems, recv_sems):
        x, y, c, chips = _position()
        me = 2 * x + y
        for i in range(n):
            for j, (cx, cy) in enumerate(chips):
                slot = outs[i].at[2 * cx + cy]
                _remote(slot, slot, send_sems, recv_sems, 3 * i + j, (cx, cy, c)).wait_recv()
        for i in range(n):
            for j, (cx, cy) in enumerate(chips):
                _remote(ins[i].at[2 * cx + cy], outs[i].at[me], send_sems, recv_sems, 3 * i + j, (cx, cy, c)).wait_send()

    return {"arrays": ps, "out_shape": [SDS(p.shape, p.dtype) for p in ps], "n_sems": 3 * n, "start": start, "finish": finish}


def _hosted_swap(slots):
    n = len(slots)

    def copies(ins, outs, send_sems, recv_sems):
        x, y, c, _ = _position()
        cps = []
        for i in range(n):
            rh = ins[i].shape[1] // 2
            ohalf = pl.ds(pl.multiple_of((1 - c) * rh, 8), rh)
            cps.append(_remote(ins[i].at[:, ohalf, :], outs[i], send_sems, recv_sems, i, (x, y, 1 - c)))
        return cps

    def start(ins, outs, send_sems, recv_sems):
        for cp in copies(ins, outs, send_sems, recv_sems):
            cp.start()

    def finish(ins, outs, send_sems, recv_sems):
        for cp in copies(ins, outs, send_sems, recv_sems):
            cp.wait()

    return {"arrays": slots, "out_shape": [SDS((4, s.shape[1] // 2, s.shape[2]), s.dtype) for s in slots], "n_sems": n,
            "start": start, "finish": finish}


def _rs_begin(slots, c):
    b1 = _swap_halves_multi("rs_swap", slots)
    return [_pair_add("rs_pair_%d" % i, g, b, c) for i, (g, b) in enumerate(zip(slots, b1))]


def _rs_end(ps, b2, c, chip):
    return _join_halves("rs_join", [_sum_chips("rs_sum_%d" % i, p, b, chip, c) for i, (p, b) in enumerate(zip(ps, b2))])


def _reduce_scatter_multi(slots, c, chip):
    ps = _rs_begin(slots, c)
    return _rs_end(ps, _scatter_chips_multi("rs_scatter", ps), c, chip)


_BIG = ("w_in", "w_dn_out", "w_swa_out", "w_o", "w_up", "w_down")


_W_IN_PIECES = ((0, 3072, 0), (3072, 4096, 3072), (5392, 6416, 4096), (6416, 7440, 5120), (4112, 5136, 6144), (4096, 4112, 7168),
                (5136, 5264, 7296), (5264, 5392, 7424))
_W_IN_SHARD = IN_TOTAL // 4


def _w_in_from_slots(g):
    parts, at = [], 0
    for lo, hi, dst in _W_IN_PIECES:
        if dst > at:
            parts.append(jnp.zeros((g.shape[1], dst - at), g.dtype))
        for s in range(4):
            a, b = max(lo, s * _W_IN_SHARD), min(hi, (s + 1) * _W_IN_SHARD)
            if a < b:
                parts.append(g[s][:, a - s * _W_IN_SHARD:b - s * _W_IN_SHARD])
        at = dst + hi - lo
    parts.append(jnp.zeros((g.shape[1], PROJ_W - at), g.dtype))
    return jnp.concatenate(parts, axis=1)


def _w_in_to_slots(gw):
    slots = []
    for s in range(4):
        parts = []
        for lo, hi, dst in sorted(_W_IN_PIECES):
            a, b = max(lo, s * _W_IN_SHARD), min(hi, (s + 1) * _W_IN_SHARD)
            if a < b:
                parts.append(gw[:, dst + a - lo:dst + b - lo])
        slots.append(jnp.concatenate(parts, axis=1))
    return jnp.stack(slots)


def _assemble_mixer(gs):
    rows = lambda g: g.reshape(4 * g.shape[1], g.shape[2])
    return {"w_in": _w_in_from_slots(gs[0]), "w_dn_out": rows(gs[1]), "w_swa_out": rows(gs[2]), "w_o": rows(gs[3])}


def _grad_slots(gw):
    rows = lambda g: g.reshape(4, g.shape[0] // 4, g.shape[1])
    return [_w_in_to_slots(gw["w_in"]), rows(gw["w_dn_out"]), rows(gw["w_swa_out"]), rows(gw["w_o"]), gw["w_up"], rows(gw["w_down"])]


def _pad_lanes(v, n=LANE):
    return jnp.pad(v, (0, n - v.shape[0])).reshape(1, n)


def _layer_consts(P):
    K = {}
    K["norm_mix"] = P["norm_mix"].reshape(1, D)
    K["norm_ffn"] = P["norm_ffn"].reshape(1, D)
    K["alog"] = _pad_lanes(P["dn_a_log"])
    K["dtb"] = _pad_lanes(P["dn_dt_bias"])
    K["dn_norm"] = P["dn_norm"].reshape(1, LANE)
    K["qn"] = jnp.tile(P["swa_q_norm"], 16).reshape(1, D)
    K["kn"] = jnp.tile(P["swa_k_norm"], 2).reshape(1, LANE)
    K["sinks"] = _pad_lanes(P["swa_sinks"])
    K["ffn_b"] = P["ffn_conv_b"].reshape(1, D_FF)
    return K


def _layer_fwd(x, mod, W, K, tabs, bd, hosted=None, late=None):
    late = late or {}
    sh1, sc1, gt1, sh2, sc2, gt2 = mod
    S = {"x": x}
    h1, h1t = _rowwise_fwd("normmod1_fwd", _normmod_fn, [(x, 0, D)], [K["norm_mix"], sc1, sh1], [D], [bf16], also_transposed=True)
    if "proj" in late:
        proj, arrived = _matmul("proj_fwd", h1, W["w_in"], "nn", f32, hosted=late["proj"][0])
        W.update(late["proj"][1](arrived))
    else:
        proj = _matmul("proj_fwd", h1, W["w_in"], "nn", f32)
    qn = _dnconv_fwd("dnconv_q_fwd", proj, CB_Q, W["dn_conv"], True)
    kn = _dnconv_fwd("dnconv_k_fwd", proj, CB_K, W["dn_conv"], True)
    vc = _dnconv_fwd("dnconv_v_fwd", proj, CB_V, W["dn_conv"], False)
    o, sall, xinv, hosted_out = _gdn_fwd("gdn_fwd", qn, kn, vc, proj, K["alog"], K["dtb"], hosted=hosted)
    (on,) = _rowwise_fwd("dngate_fwd", _dngate_fn, [(o, 0, LANE), (proj, 8 * WB_Z, LANE)], [K["dn_norm"]], [LANE], [bf16], nc=8,
                         tr=_pick(x.shape[0], (1024,)))
    ya = _matmul("dnout_fwd", on, W["w_dn_out"], "nn", f32)
    sq = _qkprep_fwd("qprep_fwd", proj, WB_SWQ, D, K["qn"], bd[0], tabs[0])
    sk = _qkprep_fwd("kprep_fwd", proj, CB_SWK, LANE, K["kn"], bd[1], tabs[1])
    attn, arrived = _attn_fwd("attn_fwd", sq, sk, proj, K["sinks"], K["attn_bias"], hosted=late["attn"][0] if "attn" in late else None)
    if "attn" in late:
        W.update(late["attn"][1](arrived))
    yb = _matmul("swaout_fwd", attn, W["w_swa_out"], "nn", f32)
    (merged,) = _rowwise_fwd("merge_fwd", _merge_fn, [(proj, WB_GA, D), (proj, WB_GB, D), (ya, 0, D), (yb, 0, D)], [], [D], [bf16])
    t1, x1 = _matmul("wo_fwd", merged, W["w_o"], "nn", f32, resid=(x, gt1))
    h2, h2t = _rowwise_fwd("normmod2_fwd", _normmod_fn, [(x1, 0, D)], [K["norm_ffn"], sc2, sh2], [D], [bf16], also_transposed=True)
    up = _matmul("up_fwd", h2, W["w_up"], "nn", f32, b_slots=True)
    mid = _ffnact_fwd("ffnact_fwd", up, W["ffn_conv"], K["ffn_b"])
    t2, x2 = _matmul("down_fwd", mid, W["w_down"], "nn", f32, resid=(x1, gt2))
    S.update(h1t=h1t, h2t=h2t, proj=proj, qn=qn, kn=kn, vc=vc, o=o, sall=sall, xinv=xinv, on=on, ya=ya, sq=sq, sk=sk, attn=attn, yb=yb,
             merged=merged, t1=t1, x1=x1, h2=h2, up=up, mid=mid, t2=t2)
    return x2, S, hosted_out


def _layer_bwd(dx2, S, mod, W, K, tabs, bd, carry=None, early=None):
    sh1, sc1, gt1, sh2, sc2, gt2 = mod
    x, x1, proj, up = S["x"], S["x1"], S["proj"], S["up"]
    T = x.shape[0]
    gw, gs = {}, {}
    dt2, dgt2 = _rowwise_bwd("resid2_bwd", _resid_fn, [(x1, 0, D), (S["t2"], 0, D)], [gt2], [(dx2, 0, D)], [None, bf16])
    dmid = _matmul("down_bwd_x", dt2, W["w_down"], "nt", bf16)
    gw["w_down"] = _matmul("down_bwd_w", S["mid"], dt2, "tn", f32)
    dact, dlin, gw["ffn_conv"], dffn_b = _ffnact_bwd("ffnact_bwd", up, W["ffn_conv"], K["ffn_b"], dmid)
    dup = (dact, dlin)
    dh2 = _matmul("up_bwd_x", dup, W["w_up"], "nt", f32, b_slots=True)
    if carry is None:
        gw["w_up"] = _matmul("up_bwd_w", S["h2t"], dup, "nn", f32, out_slots=4)
        pair_sums = hosted = None
    else:
        gw["w_up"], b1 = _matmul("up_bwd_w", S["h2t"], dup, "nn", f32, out_slots=4, hosted=_hosted_swap(carry[0]))
        pair_sums = [_pair_add("rs_pair_%d" % i, g, b, carry[1]) for i, (g, b) in enumerate(zip(carry[0], b1))]
        hosted = _hosted_scatter(pair_sums)
    dx1, dnorm_ffn, dsc2, dsh2 = _rowwise_bwd("normmod2_bwd", _normmod_fn, [(x1, 0, D)], [K["norm_ffn"], sc2, sh2], [(dh2, 0, D)], [f32],
                                              add_to_first=(dx2, 0, D))
    early_out = None
    if early is not None:
        ffn_slots = [gw["w_up"], gw["w_down"].reshape(4, D_FF // 4, D)]
        ffn_ps = [_pair_add("rs_pair_ffn_%d" % i, g, b, early[0]) for i, (g, b) in enumerate(zip(ffn_slots, _swap_halves_multi("rs_swap_ffn", ffn_slots)))]
    dt1, dgt1 = _rowwise_bwd("resid1_bwd", _resid_fn, [(x, 0, D), (S["t1"], 0, D)], [gt1], [(dx1, 0, D)], [None, bf16])
    dmerged = _matmul("wo_bwd_x", dt1, W["w_o"], "nt", f32)
    gw["w_o"] = _matmul("wo_bwd_w", S["merged"], dt1, "tn", f32)
    dproj, dya, dyb = _rowwise_bwd("merge_bwd", _merge_fn, [(proj, WB_GA, D), (proj, WB_GB, D), (S["ya"], 0, D), (S["yb"], 0, D)], [],
                                   [(dmerged, 0, D)], [bf16, bf16, bf16, bf16], dest=(None, (0, 1), WB_GA // 2))
    don = _matmul("dnout_bwd_x", dya, W["w_dn_out"], "nt", f32)
    gw["w_dn_out"] = _matmul("dnout_bwd_w", S["on"], dya, "tn", f32)
    dproj, do, ddn_norm = _rowwise_bwd("dngate_bwd", _dngate_fn, [(S["o"], 0, LANE), (proj, 8 * WB_Z, LANE)], [K["dn_norm"]], [(don, 0, LANE)],
                                       [f32, bf16], nc=8, tr=_pick(T, (1024,)), dest=(dproj, (1,), 8 * WB_Z))
    (dqn, dkn, dvc, dab, dalog, ddtb), hosted_out = _gdn_bwd("gdn_bwd", S["qn"], S["kn"], S["vc"], proj, K["alog"], K["dtb"], S["sall"], S["xinv"], do,
                                                            hosted=hosted)
    dproj, dwq = _dnconv_bwd("dnconv_q_bwd", proj, CB_Q, W["dn_conv"], dqn, True, dproj)
    dproj, dwk = _dnconv_bwd("dnconv_k_bwd", proj, CB_K, W["dn_conv"], dkn, True, dproj)
    dproj, dwv = _dnconv_bwd("dnconv_v_bwd", proj, CB_V, W["dn_conv"], dvc, False, dproj)
    gw["dn_conv"] = jnp.concatenate([dwq, dwk, dwv], axis=1)
    dattn = _matmul("swaout_bwd_x", dyb, W["w_swa_out"], "nt", f32)
    gw["w_swa_out"] = _matmul("swaout_bwd_w", S["attn"], dyb, "tn", f32)
    (dsq, dkp, dkc, dvp, dvc_, dsinks), ffn_b2 = _attn_bwd("attn_bwd", S["sq"], S["sk"], proj, K["sinks"], K["attn_bias"], dattn,
                                                          hosted=None if early is None else _hosted_scatter(ffn_ps))
    if early is not None:
        early_out = _rs_end(ffn_ps, ffn_b2, early[0], early[1])
    dsk = _shift_add("attn_dk_join", dkc, dkp, f32)
    dswv = _shift_add("attn_dv_join", dvc_, dvp, bf16)
    dproj, dqn_w = _qkprep_bwd("qprep_bwd", proj, WB_SWQ, D, K["qn"], bd[0], tabs[0], dsq, dest_buf=dproj)
    dswk, dkn_w = _qkprep_bwd("kprep_bwd", proj, CB_SWK, LANE, K["kn"], bd[1], tabs[1], dsk)
    tail = jnp.concatenate([dab.astype(bf16), dswk, dswv, jnp.zeros((T, LANE), bf16)], axis=1)
    dproj = lax.dynamic_update_slice(dproj, tail, (0, CB_AB * LANE))
    dh1 = _matmul("proj_bwd_x", dproj, W["w_in"], "nt", f32)
    gw["w_in"] = _matmul("proj_bwd_w", S["h1t"], dproj, "nn", f32)
    dx, dnorm_mix, dsc1, dsh1 = _rowwise_bwd("normmod1_bwd", _normmod_fn, [(x, 0, D)], [K["norm_mix"], sc1, sh1], [(dh1, 0, D)], [f32],
                                             add_to_first=(dx1, 0, D))
    gs = {"norm_mix": dnorm_mix[0], "dn_a_log": dalog[0, :8], "dn_dt_bias": ddtb[0, :8], "dn_norm": ddn_norm[0],
          "swa_q_norm": dqn_w.reshape(16, 64).sum(0), "swa_k_norm": dkn_w.reshape(2, 64).sum(0), "swa_sinks": dsinks[0, :16],
          "norm_ffn": dnorm_ffn[0], "ffn_conv_b": dffn_b[0]}
    dmod = jnp.concatenate([dsh1, dsc1, dgt1, dsh2, dsc2, dgt2], axis=1)
    return dx, gw, gs, dmod, (pair_sums, hosted_out), early_out


def _rope_tables(pos):
    T = pos.shape[0]
    half = 8
    inv = jnp.power(ROPE_THETA, -jnp.arange(half, dtype=f32) / half)
    ang = pos.astype(f32)[:, None] * inv
    cos, sin = jnp.cos(ang), jnp.sin(ang)
    z8, z48, o48 = jnp.zeros((T, 8), f32), jnp.zeros((T, 48), f32), jnp.ones((T, 48), f32)
    c64 = jnp.concatenate([cos, cos, o48], axis=1)
    s1 = jnp.concatenate([-sin, z8, z48], axis=1)
    s2 = jnp.concatenate([z8, sin, z48], axis=1)
    return tuple(jnp.tile(t, (1, 2)) for t in (c64, s1, s2))


_SMALL = (("norm_mix", D), ("dn_a_log", 8), ("dn_dt_bias", 8), ("dn_norm", 128), ("swa_q_norm", 64), ("swa_k_norm", 64),
          ("swa_sinks", 16), ("norm_ffn", D), ("ffn_conv_b", D_FF), ("b_ada", 6 * D))
_CONV = (("dn_conv", 4 * 3072), ("ffn_conv", 3 * D_FF))
_CONV_SHARD = (("dn_conv", 4 * 768), ("ffn_conv", 3 * 704))


def _pack_small(vals, spec):
    flat = jnp.concatenate([vals[nm].reshape(-1) for nm, _ in spec])
    rows = -(-flat.shape[0] // (8 * LANE)) * 8
    return jnp.pad(flat, (0, rows * LANE - flat.shape[0])).reshape(rows, LANE)


def _unpack_small(buf, spec):
    flat = buf.reshape(-1)
    out, off = {}, 0
    for nm, n in spec:
        out[nm] = flat[off:off + DEPTH * n].reshape(DEPTH, n)
        off += DEPTH * n
    return out


def kernel(x, c, positions, w_ada, b_ada, norm_mix, w_in, dn_conv, dn_a_log, dn_dt_bias, dn_norm, w_dn_out, swa_q_norm, swa_k_norm, swa_sinks, w_swa_out, w_o, norm_ffn, w_up, ffn_conv, ffn_conv_b, w_down, loss_target, m_w_ada, m_b_ada, m_norm_mix, m_w_in, m_dn_conv, m_dn_a_log, m_dn_dt_bias, m_dn_norm, m_w_dn_out, m_swa_q_norm, m_swa_k_norm, m_swa_sinks, m_w_swa_out, m_w_o, m_norm_ffn, m_w_up, m_ffn_conv, m_ffn_conv_b, m_w_down, v_w_ada, v_b_ada, v_norm_mix, v_w_in, v_dn_conv, v_dn_a_log, v_dn_dt_bias, v_dn_norm, v_w_dn_out, v_swa_q_norm, v_swa_k_norm, v_swa_sinks, v_w_swa_out, v_w_o, v_norm_ffn, v_w_up, v_ffn_conv, v_ffn_conv_b, v_w_down):
    weights = dict(w_ada=w_ada, b_ada=b_ada, norm_mix=norm_mix, w_in=w_in, dn_conv=dn_conv, dn_a_log=dn_a_log, dn_dt_bias=dn_dt_bias,
                   dn_norm=dn_norm, w_dn_out=w_dn_out, swa_q_norm=swa_q_norm, swa_k_norm=swa_k_norm, swa_sinks=swa_sinks,
                   w_swa_out=w_swa_out, w_o=w_o, norm_ffn=norm_ffn, w_up=w_up, ffn_conv=ffn_conv, ffn_conv_b=ffn_conv_b, w_down=w_down)
    mom_m = dict(w_ada=m_w_ada, b_ada=m_b_ada, norm_mix=m_norm_mix, w_in=m_w_in, dn_conv=m_dn_conv, dn_a_log=m_dn_a_log,
                 dn_dt_bias=m_dn_dt_bias, dn_norm=m_dn_norm, w_dn_out=m_w_dn_out, swa_q_norm=m_swa_q_norm, swa_k_norm=m_swa_k_norm,
                 swa_sinks=m_swa_sinks, w_swa_out=m_w_swa_out, w_o=m_w_o, norm_ffn=m_norm_ffn, w_up=m_w_up, ffn_conv=m_ffn_conv,
                 ffn_conv_b=m_ffn_conv_b, w_down=m_w_down)
    mom_v = dict(w_ada=v_w_ada, b_ada=v_b_ada, norm_mix=v_norm_mix, w_in=v_w_in, dn_conv=v_dn_conv, dn_a_log=v_dn_a_log,
                 dn_dt_bias=v_dn_dt_bias, dn_norm=v_dn_norm, w_dn_out=v_w_dn_out, swa_q_norm=v_swa_q_norm, swa_k_norm=v_swa_k_norm,
                 swa_sinks=v_swa_sinks, w_swa_out=v_w_swa_out, w_o=v_w_o, norm_ffn=v_norm_ffn, w_up=v_w_up, ffn_conv=v_ffn_conv,
                 ffn_conv_b=v_ffn_conv_b, w_down=v_w_down)
    order = ["w_ada", "b_ada", "norm_mix", "w_in", "dn_conv", "dn_a_log", "dn_dt_bias", "dn_norm", "w_dn_out", "swa_q_norm",
             "swa_k_norm", "swa_sinks", "w_swa_out", "w_o", "norm_ffn", "w_up", "ffn_conv", "ffn_conv_b", "w_down"]
    ax, ay, ac = lax.axis_index("x"), lax.axis_index("y"), lax.axis_index("c")
    chip = 2 * ax + ay
    dev = 4 * ax + 2 * ay + ac
    T = x.shape[1]
    xs = x[0]

    c_all = _allgather_all("gather_c", jnp.pad(c, ((0, 7), (0, 0)))).reshape(8, 8, D)[:, 0]
    c_act = _silu_rows("silu_c", jnp.pad(c_all, ((0, 8), (0, 0))))
    mod_sh = jnp.stack([
        _matmul("mod_fwd", c_act, w_ada[l].astype(bf16), "nn", f32,
                bias=lax.dynamic_slice(b_ada[l], (chip * 1536,), (1536,)).reshape(1, 1536)) for l in range(DEPTH)])
    mod_all = _allgather_all("gather_mod", mod_sh.reshape(DEPTH * 16 * 12, LANE)).reshape(8, DEPTH, 16, 1536)
    mod_me = jnp.concatenate([lax.dynamic_index_in_dim(mod_all[2 * s], dev, axis=1, keepdims=False) for s in range(4)], axis=1)

    tabs_q = _rope_tables(positions[0])
    tabs = (tabs_q, tabs_q)
    head = jnp.arange(LANE) // 64
    bd128 = (head[:, None] == head[None, :]).astype(f32) / 64.0
    bd = (bd128, bd128)
    attn_bias = _attn_bias()

    conv_all = _allgather_all("gather_conv", _pack_small({"dn_conv": dn_conv, "ffn_conv": ffn_conv}, _CONV_SHARD))
    conv_parts = [_unpack_small(conv_all[2 * s], _CONV_SHARD) for s in range(4)]
    dn_conv_full = jnp.concatenate([p["dn_conv"].reshape(DEPTH, 4, 768) for p in conv_parts], axis=2)
    ffn_conv_full = jnp.concatenate([p["ffn_conv"].reshape(DEPTH, 3, 704) for p in conv_parts], axis=2)

    saved, Ws, Ks, mods = [], [], [], []
    h = xs
    shards = [[weights[nm][l].astype(bf16) for nm in _BIG] for l in range(DEPTH)]
    rest = (0, 1, 2, 3, 5)
    rows = lambda g: g.reshape(4 * g.shape[1], g.shape[2])

    def arrive(name, arrived, l, idx):
        got = _gather_forward(name, arrived)
        return [lax.dynamic_update_index_in_dim(g, shards[l][i], chip, 0) for g, i in zip(got, idx)]

    def up_late(l):
        return (_hosted_gather([shards[l][4]]), lambda arrived: {"w_up": arrive("gather_up_pass", arrived, l, (4,))[0]})

    got = [lax.dynamic_update_index_in_dim(g, shards[0][i], chip, 0) for i, g in enumerate(_gather_chips("gather_w", shards[0][:4]))]
    for l in range(DEPTH):
        W = _assemble_mixer(got[:4])
        late = {"proj": up_late(l)}
        if l == 0:
            late["attn"] = (_hosted_gather([shards[0][5]]), lambda arrived: {"w_down": rows(arrive("gather_down_pass", arrived, 0, (5,))[0])})
        else:
            W["w_down"] = rows(got[4])
        W["dn_conv"], W["ffn_conv"] = dn_conv_full[l], ffn_conv_full[l]
        K = _layer_consts({nm: weights[nm][l] for nm in ("norm_mix", "norm_ffn", "dn_a_log", "dn_dt_bias", "dn_norm", "swa_q_norm",
                                                          "swa_k_norm", "swa_sinks", "ffn_conv_b")})
        K["attn_bias"] = attn_bias
        mod = tuple(mod_me[l, k * D:(k + 1) * D].reshape(1, D) for k in range(6))
        nxt = _hosted_gather([shards[l + 1][i] for i in rest]) if l + 1 < DEPTH else None
        h, S, arrived = _layer_fwd(h, mod, W, K, tabs, bd, hosted=nxt, late=late)
        if nxt is not None:
            got = arrive("gather_w_pass", arrived, l + 1, rest)
        saved.append(S), Ws.append(W), Ks.append(K), mods.append(mod)

    loss_blk, dh = _loss("loss", h, loss_target[0])
    loss = lax.psum(loss_blk[0, 0], ("x", "y", "c"))

    grad_sh = [None] * DEPTH
    small = [None] * DEPTH
    dmods = [None] * DEPTH
    slots = None
    for l in reversed(range(DEPTH)):
        dh, gw, gs, dmod, (ps, b2), ffn0 = _layer_bwd(dh, saved[l], mods[l], Ws[l], Ks[l], tabs, bd, carry=None if slots is None else (slots, ac),
                                                      early=(ac, chip) if l == 0 else None)
        if slots is not None:
            grad_sh[l + 1] = dict(zip(_BIG, _rs_end(ps, b2, ac, chip)))
        slots = _grad_slots(gw)
        small[l], dmods[l] = dict(gs, dn_conv=gw["dn_conv"], ffn_conv=gw["ffn_conv"]), dmod[0]
    grad_sh[0] = dict(zip(_BIG, _reduce_scatter_multi(slots[:4], ac, chip) + ffn0))

    spec_g = _SMALL + _CONV
    vals = {nm: jnp.stack([small[l][nm] for l in range(DEPTH)]) for nm, _ in spec_g if nm != "b_ada"}
    vals["b_ada"] = jnp.stack(dmods)
    small_all = _allgather_all("gather_small", _pack_small(vals, spec_g))
    g_small = _unpack_small(_sum_leading("sum_small", small_all), spec_g)
    dmod_all = jnp.stack([_unpack_small(small_all[d], spec_g)["b_ada"] for d in range(8)])
    dmod_sh = lax.dynamic_slice(dmod_all, (0, 0, chip * 1536), (8, DEPTH, 1536))
    dmod_sh = jnp.pad(dmod_sh, ((0, 8), (0, 0), (0, 0))).astype(bf16)
    g_w_ada = jnp.stack([_matmul("mod_bwd_w", c_act, dmod_sh[:, l], "tn", f32) for l in range(DEPTH)])

    grads = {nm: g_small[nm] for nm, _ in _SMALL}
    grads["dn_conv"] = lax.dynamic_slice(g_small["dn_conv"].reshape(DEPTH, 4, 3072), (0, 0, chip * 768), (DEPTH, 4, 768))
    grads["ffn_conv"] = lax.dynamic_slice(g_small["ffn_conv"].reshape(DEPTH, 3, D_FF), (0, 0, chip * 704), (DEPTH, 3, 704))
    grads["w_ada"] = g_w_ada
    for nm in _BIG:
        grads[nm] = jnp.stack([grad_sh[l][nm] for l in range(DEPTH)])

    delta, new_m, new_v = {}, {}, {}
    for nm in ("w_ada", "dn_conv", "ffn_conv") + _BIG:
        delta[nm], new_m[nm], new_v[nm] = _adamw("adamw_" + nm, weights[nm], grads[nm], mom_m[nm], mom_v[nm])
    sm = [_pack_small({nm: t[nm] for nm, _ in _SMALL}, _SMALL) for t in (weights, grads, mom_m, mom_v)]
    for tgt, buf in zip((delta, new_m, new_v), _adamw("adamw_small", *sm)):
        tgt.update(_unpack_small(buf, _SMALL))

    return (loss, dh[None], *[grads[n] for n in order], *[delta[n] for n in order], *[new_m[n] for n in order], *[new_v[n] for n in order])
```

```python
import functools

import jax
import jax.numpy as jnp
import numpy as np
from jax import lax
from jax.experimental import pallas as pl
from jax.experimental.pallas import tpu as pltpu

f32 = jnp.float32
bf16 = jnp.bfloat16
SDS = jax.ShapeDtypeStruct
HI = lax.Precision.HIGHEST
MESH = pl.DeviceIdType.MESH

D = 1024
DEPTH = 4
EPS = 1e-6
DN_C = 64
SWA_B = 128
LANE = 128
ROPE_THETA = 500000.0
D_FF = 2816
IN_TOTAL = 7440
PROJ_W = 7680
CB_Q, CB_K, CB_V = 0, 8, 16
CB_AB, CB_SWK, CB_SWV = 56, 57, 58
WB_Z, WB_GA, WB_GB, WB_SWQ = 3, 4, 5, 6
TR = 256
COMM_TR = 128
VMEM_BIG = 48 * 2 ** 20

ADAM_LR, ADAM_B1, ADAM_B2, ADAM_EPS, ADAM_WD, ADAM_STEP = 0.001, 0.9, 0.999, 1e-08, 0.01, 10


def _pcall(body, **kw):
    return pl.pallas_call(body, **kw)


def _cparams(vmem=None):
    return pltpu.CompilerParams(vmem_limit_bytes=vmem) if vmem else None


def _dot(a, b, ca, cb, precision=HI):
    return lax.dot_general(a, b, (((ca,), (cb,)), ((), ())), precision=precision, preferred_element_type=f32)


def _pick(n, cands):
    for c in cands:
        if n % c == 0:
            return c
    return n


def _tile(n, cap):
    if n <= cap:
        return n
    best = None
    for t in range(LANE, cap + 1, LANE):
        if n % t == 0:
            best = t
    assert best is not None, (n, cap)
    return best


def _matmul(name, a, b, mode, out_dtype, bias=None, out_slots=None, hosted=None, resid=None, b_slots=False):
    h_args, h_specs, h_shapes, h_sems = _hosted_parts(hosted)
    nh = len(h_args)
    a_parts = list(a) if isinstance(a, (tuple, list)) else [a]
    b_parts = list(b) if isinstance(b, (tuple, list)) else [b]
    na, nbp = len(a_parts), len(b_parts)
    assert (na == 1 or mode == "nt") and (nbp == 1 or mode == "nn")
    if b_slots:
        S_, rows_, cols_ = b.shape
        M = a_parts[0].shape[0]
        K, N = (rows_, S_ * cols_) if mode == "nn" else (S_ * cols_, rows_)
    elif mode == "nn":
        M, K = a_parts[0].shape
        N = b_parts[0].shape[1] * nbp
    elif mode == "nt":
        M, K = a_parts[0].shape[0], a_parts[0].shape[1] * na
        N = b_parts[0].shape[0]
    else:
        (K, M), N = a_parts[0].shape, b_parts[0].shape[1]
    tm = _tile(M, 1536 if mode == "tn" else 1024)
    tn = N // out_slots if out_slots else _tile(N, 1536)
    tk = _tile(K, 512 if mode == "tn" else (1024 if K <= 1024 else 1536))
    nk, nj = K // tk, N // tn
    ka, jb = nk // na, nj // nbp
    assert nk % na == 0 and nj % nbp == 0
    ca, cb = {"nn": (1, 0), "nt": (1, 1), "tn": (0, 0)}[mode]
    grid = (M // tm, nj, nk)
    n_bias, n_res = (1 if bias is not None else 0), (2 if resid is not None else 0)

    def body(*refs):
        a_refs, b_refs = refs[:na], refs[na:na + nbp]
        p = na + nbp
        bias_ref = refs[p] if n_bias else None
        res_refs = refs[p + n_bias:p + n_bias + n_res]
        p += n_bias + n_res
        h_ins, o_ref = refs[p:p + nh], refs[p + nh]
        xo_ref = refs[p + nh + 1] if n_res else None
        p += nh + 1 + (1 if n_res else 0)
        h_outs, rest = refs[p:p + nh], refs[p + nh:]
        acc, sems = (rest[0], rest[1:]) if nk > 1 else (None, rest)
        j, k = pl.program_id(1), pl.program_id(2)
        step = (pl.program_id(0) * grid[1] + j) * grid[2] + k
        _hosted_edge(hosted, "start", h_ins, h_outs, sems, step == 0)

        def finish(r):
            if bias is not None:
                r = r + bias_ref[...]
            o_ref[...] = r.astype(o_ref.dtype)
            if n_res:
                xo_ref[...] = res_refs[0][...] + res_refs[1][...] * r

        a_tile = a_refs[0][...] if na == 1 else jnp.where(k < ka, a_refs[0][...], a_refs[1][...])
        b_tile = b_refs[0][...] if nbp == 1 else jnp.where(j < jb, b_refs[0][...], b_refs[1][...])
        part = _dot(a_tile.astype(bf16), b_tile.astype(bf16), ca, cb, precision=None)
        if nk == 1:
            finish(part)
        else:
            @pl.when(k == 0)
            def _():
                acc[...] = part

            @pl.when(k > 0)
            def _():
                acc[...] += part

            @pl.when(k == nk - 1)
            def _():
                finish(acc[...])

        _hosted_edge(hosted, "finish", h_ins, h_outs, sems, step == grid[0] * grid[1] * grid[2] - 1)

    if mode == "tn":
        a_specs = [pl.BlockSpec((tk, tm), lambda i, j, k: (k, i))]
    else:
        a_specs = [pl.BlockSpec((tm, tk), lambda i, j, k, q=q: (i, jnp.clip(k - q * ka, 0, ka - 1))) for q in range(na)]
    if b_slots:
        assert (tn if mode == "nn" else tk) == b.shape[2]
        b_specs = [pl.BlockSpec((None, tk, tn), lambda i, j, k: (j, k, 0)) if mode == "nn" else pl.BlockSpec((None, tn, tk), lambda i, j, k: (k, j, 0))]
    elif mode == "nt":
        b_specs = [pl.BlockSpec((tn, tk), lambda i, j, k: (j, k))]
    else:
        b_specs = [pl.BlockSpec((tk, tn), lambda i, j, k, q=q: (k, jnp.clip(j - q * jb, 0, jb - 1))) for q in range(nbp)]
    in_specs, args = a_specs + b_specs, a_parts + b_parts
    if bias is not None:
        in_specs.append(pl.BlockSpec((1, tn), lambda i, j, k: (0, j)))
        args.append(bias)
    if out_slots:
        out_specs, out_shape = [pl.BlockSpec((None, tm, tn), lambda i, j, k: (j, i, 0))], [SDS((out_slots, M, tn), out_dtype)]
    else:
        out_specs, out_shape = [pl.BlockSpec((tm, tn), lambda i, j, k: (i, j))], [SDS((M, N), out_dtype)]
    if resid is not None:
        in_specs += [pl.BlockSpec((tm, tn), lambda i, j, k: (i, j)), pl.BlockSpec((1, tn), lambda i, j, k: (0, j))]
        args += list(resid)
        out_specs.append(pl.BlockSpec((tm, tn), lambda i, j, k: (i, j)))
        out_shape.append(SDS((M, N), f32))
    outs = _pcall(
        body, grid=grid, in_specs=in_specs + h_specs, out_specs=out_specs + h_specs, out_shape=out_shape + h_shapes,
        scratch_shapes=([pltpu.VMEM((tm, tn), f32)] if nk > 1 else []) + h_sems, compiler_params=_cparams(VMEM_BIG), name=name)(*args, *h_args)
    n_main = len(out_shape)
    res = list(outs[:n_main]) + ([list(outs[n_main:])] if hosted is not None else [])
    return res[0] if len(res) == 1 else tuple(res)


def _row_specs(rows, tr):
    return [pl.BlockSpec((tr, w), lambda i, j, off=off: (i, off + j)) for (_, off, w) in rows]


def _rowwise_fwd(name, fn, rows, vecs, out_widths, out_dtypes, nc=1, tr=TR, also_transposed=False):
    T = rows[0][0].shape[0]
    n_in = len(rows) + len(vecs)
    n_out = len(out_widths)

    def body(*refs):
        vals = [r[...].astype(f32) for r in refs[:n_in]]
        res = fn(*vals)
        for o_ref, r in zip(refs[n_in:n_in + n_out], res):
            o_ref[...] = r.astype(o_ref.dtype)
        if also_transposed:
            refs[n_in + n_out][...] = res[0].T.astype(refs[n_in + n_out].dtype)

    in_specs = _row_specs(rows, tr) + [pl.BlockSpec(v.shape, lambda i, j: (0, 0)) for v in vecs]
    out_specs = [pl.BlockSpec((tr, w), lambda i, j: (i, j)) for w in out_widths]
    out_shape = [SDS((T, w * nc), dt) for w, dt in zip(out_widths, out_dtypes)]
    if also_transposed:
        out_specs.append(pl.BlockSpec((out_widths[0], tr), lambda i, j: (j, i)))
        out_shape.append(SDS((out_widths[0] * nc, T), out_dtypes[0]))
    return _pcall(body, grid=(T // tr, nc), in_specs=in_specs, out_specs=out_specs, out_shape=out_shape, name=name)(
        *[r[0] for r in rows], *vecs)


def _into_buffer(dest_buf, n_inputs, out_index):
    if dest_buf is None:
        return [], [], {}
    return [dest_buf], [pl.BlockSpec(memory_space=pl.ANY)], {n_inputs: out_index}


def _rowwise_bwd(name, fn, rows, vecs, cts, drow_dtypes, nc=1, tr=TR, add_to_first=None, dest=None):
    T = rows[0][0].shape[0]
    n_r, n_v, n_c = len(rows), len(vecs), len(cts)
    n_add = 0 if add_to_first is None else 1
    keep = [k for k, dt in enumerate(drow_dtypes) if dt is not None]
    members = [] if dest is None else list(dest[1])
    plain = [pos for pos in range(len(keep)) if pos not in members]
    n_dest = 1 if dest is not None else 0
    n_in = n_r + n_v + n_c + n_add

    def body(*refs):
        vals = [r[...].astype(f32) for r in refs[:n_in]]
        outs = refs[n_in + (1 if dest is not None and dest[0] is not None else 0):]
        i, j = pl.program_id(0), pl.program_id(1)
        _, vjp = jax.vjp(fn, *vals[:n_r + n_v])
        grads = vjp(tuple(vals[n_r + n_v:n_r + n_v + n_c]))
        got = []
        for pos, k in enumerate(keep):
            g = grads[k]
            if n_add and pos == 0:
                g = g + vals[n_in - 1]
            got.append(g)
        if dest is not None:
            outs[0][...] = jnp.concatenate([got[pos] for pos in members], axis=1).astype(outs[0].dtype)
        for q, pos in enumerate(plain):
            outs[n_dest + q][...] = got[pos].astype(outs[n_dest + q].dtype)
        vec_outs = outs[n_dest + len(plain):]

        @pl.when((i == 0) & (j == 0))
        def _():
            for q in range(n_v):
                vec_outs[q][...] = jnp.zeros_like(vec_outs[q])

        for q in range(n_v):
            vec_outs[q][...] += grads[n_r + q]

    extra = [] if add_to_first is None else [add_to_first]
    in_specs = (_row_specs(rows, tr) + [pl.BlockSpec(v.shape, lambda i, j: (0, 0)) for v in vecs]
                + _row_specs(cts, tr) + _row_specs(extra, tr))
    args = [r[0] for r in rows] + list(vecs) + [c[0] for c in cts] + [e[0] for e in extra]
    out_specs, out_shape, aliases = [], [], {}
    if dest is not None:
        width = sum(rows[keep[pos]][2] for pos in members)
        col = dest[2]
        b_args, b_specs, aliases = _into_buffer(dest[0], n_in, 0)
        args, in_specs = args + b_args, in_specs + b_specs
        out_specs.append(pl.BlockSpec((tr, width), lambda i, j: (i, col + j)))
        out_shape.append(SDS((T, PROJ_W), bf16))
    out_specs += [pl.BlockSpec((tr, rows[keep[pos]][2]), lambda i, j: (i, j)) for pos in plain]
    out_shape += [SDS((T, rows[keep[pos]][2] * nc), drow_dtypes[keep[pos]]) for pos in plain]
    out_specs += [pl.BlockSpec(v.shape, lambda i, j: (0, 0)) for v in vecs]
    out_shape += [SDS(v.shape, f32) for v in vecs]
    return _pcall(body, grid=(T // tr, nc), in_specs=in_specs, out_specs=out_specs, out_shape=out_shape,
                  input_output_aliases=aliases, name=name)(*args)


def _normmod_fn(x, w, sc, sh):
    y = x * lax.rsqrt(jnp.mean(x * x, axis=-1, keepdims=True) + EPS)
    return ((y * w) * (1.0 + sc) + sh,)


def _resid_fn(x, t, gt):
    return (x + gt * t,)


def _merge_fn(ga, gb, ya, yb):
    return (jax.nn.sigmoid(ga) * ya + jax.nn.sigmoid(gb) * yb,)


def _headmean_raw(x2):
    rows = x2.shape[0]
    return jnp.concatenate([jnp.broadcast_to(jnp.mean(x2[:, LANE * h:LANE * (h + 1)], axis=1, keepdims=True), (rows, LANE))
                            for h in range(x2.shape[1] // LANE)], axis=1)


@jax.custom_vjp
def _headmean(x2):
    return _headmean_raw(x2)


_headmean.defvjp(lambda x2: (_headmean_raw(x2), None), lambda _, dy: (_headmean_raw(dy),))


def _dngate_fn(o, z, w):
    y = o * lax.rsqrt(_headmean(o * o) + EPS)
    return ((y * w) * (z * jax.nn.sigmoid(z)),)


def _conv_taps(x, w_ref, taps, buf):
    w = lambda s: w_ref[taps - 1 - s:taps - s, :]
    row = lax.broadcasted_iota(jnp.int32, (8, x.shape[1]), 0)
    x8 = x[0:8]
    acc, acc8 = x * w(0), x8 * w(0)
    for s in range(1, taps):
        acc = acc + pltpu.roll(x, s, 0) * w(s)
        acc8 = acc8 + jnp.where(row >= s, pltpu.roll(x8, s, 0), 0.0) * w(s)
    buf[...] = acc
    buf[0:8, :] = acc8
    return buf[...]


def _conv_taps_bwd(x, dy, w_ref, dw_ref, taps, buf):
    T = x.shape[0]
    w = lambda s: w_ref[taps - 1 - s:taps - s, :]
    row = lax.broadcasted_iota(jnp.int32, (8, x.shape[1]), 0)
    dy_first, dy_last = dy[0:8], dy[T - 8:T]
    dx, dx_last = dy * w(0), dy_last * w(0)
    dw_ref[taps - 1:taps, :] = jnp.sum(dy * x, axis=0, keepdims=True)
    for s in range(1, taps):
        dx = dx + pltpu.roll(dy, T - s, 0) * w(s)
        dx_last = dx_last + jnp.where(row < 8 - s, pltpu.roll(dy_last, 8 - s, 0), 0.0) * w(s)
        xr = pltpu.roll(x, s, 0)
        wrapped = jnp.sum(jnp.where(row < s, dy_first * xr[0:8], 0.0), axis=0, keepdims=True)
        dw_ref[taps - 1 - s:taps - s, :] = jnp.sum(dy * xr, axis=0, keepdims=True) - wrapped
    buf[...] = dx
    buf[T - 8:T, :] = dx_last
    return buf[...]


def _dn_act(y, normalize):
    s = y * jax.nn.sigmoid(y)
    if normalize:
        s = s * lax.rsqrt(jnp.sum(s * s, axis=-1, keepdims=True) + EPS)
    return s


def _dnconv_fwd(name, proj, cb, w, normalize):
    T = proj.shape[0]

    def body(x_ref, w_ref, o_ref, buf):
        o_ref[...] = _dn_act(_conv_taps(x_ref[...], w_ref, 4, buf), normalize)

    return _pcall(
        body, grid=(8,), in_specs=[pl.BlockSpec((T, LANE), lambda j: (0, cb + j)), pl.BlockSpec((4, LANE), lambda j: (0, cb + j))],
        out_specs=pl.BlockSpec((T, LANE), lambda j: (0, j)), out_shape=SDS((T, 1024), f32), scratch_shapes=[pltpu.VMEM((T, LANE), f32)],
        compiler_params=_cparams(VMEM_BIG), name=name)(proj, w)


def _dnconv_bwd(name, proj, cb, w, dout, normalize, dest_buf):
    T = proj.shape[0]
    b_args, b_specs, aliases = _into_buffer(dest_buf, 3, 0)

    def body(x_ref, w_ref, do_ref, *rest):
        dx_ref, dw_ref, buf, buf2 = rest[len(b_args):]
        x = x_ref[...]
        y = _conv_taps(x, w_ref, 4, buf)
        _, vjp = jax.vjp(functools.partial(_dn_act, normalize=normalize), y)
        (dy,) = vjp(do_ref[...])
        dx_ref[...] = _conv_taps_bwd(x, dy, w_ref, dw_ref, 4, buf2).astype(dx_ref.dtype)

    return _pcall(
        body, grid=(8,),
        in_specs=[pl.BlockSpec((T, LANE), lambda j: (0, cb + j)), pl.BlockSpec((4, LANE), lambda j: (0, cb + j)),
                  pl.BlockSpec((T, LANE), lambda j: (0, j))] + b_specs,
        out_specs=[pl.BlockSpec((T, LANE), lambda j: (0, cb + j)), pl.BlockSpec((4, LANE), lambda j: (0, j))],
        out_shape=[SDS((T, PROJ_W), bf16), SDS((4, 1024), f32)], scratch_shapes=[pltpu.VMEM((T, LANE), f32)] * 2,
        input_output_aliases=aliases, compiler_params=_cparams(VMEM_BIG), name=name)(proj, w, dout, *b_args)


def _ffn_point(a, lin):
    return a * jax.nn.sigmoid(a) * lin


def _ffnact_fwd(name, up, w, b):
    T = up.shape[0]
    nblk = D_FF // LANE

    def body(a_ref, l_ref, w_ref, b_ref, o_ref, buf):
        a = _conv_taps(a_ref[...], w_ref, 3, buf) + b_ref[...]
        o_ref[...] = _ffn_point(a, l_ref[...]).astype(o_ref.dtype)

    return _pcall(
        body, grid=(nblk,),
        in_specs=[pl.BlockSpec((T, LANE), lambda j: (0, j)), pl.BlockSpec((T, LANE), lambda j: (0, nblk + j)),
                  pl.BlockSpec((3, LANE), lambda j: (0, j)), pl.BlockSpec((1, LANE), lambda j: (0, j))],
        out_specs=pl.BlockSpec((T, LANE), lambda j: (0, j)), out_shape=SDS((T, D_FF), bf16), scratch_shapes=[pltpu.VMEM((T, LANE), f32)],
        compiler_params=_cparams(VMEM_BIG), name=name)(up, up, w, b)


def _ffnact_bwd(name, up, w, b, dmid):
    T = up.shape[0]
    nblk = D_FF // LANE

    def body(a_ref, l_ref, w_ref, b_ref, dm_ref, da_ref, dl_ref, dw_ref, db_ref, buf, buf2):
        x = a_ref[...]
        a = _conv_taps(x, w_ref, 3, buf) + b_ref[...]
        _, vjp = jax.vjp(_ffn_point, a, l_ref[...])
        da, dl = vjp(dm_ref[...].astype(f32))
        dl_ref[...] = dl.astype(dl_ref.dtype)
        db_ref[...] = jnp.sum(da, axis=0, keepdims=True)
        da_ref[...] = _conv_taps_bwd(x, da, w_ref, dw_ref, 3, buf2).astype(da_ref.dtype)

    col = lambda r: pl.BlockSpec((r, LANE), lambda j: (0, j))
    return _pcall(
        body, grid=(nblk,),
        in_specs=[col(T), pl.BlockSpec((T, LANE), lambda j: (0, nblk + j)), col(3), col(1), col(T)],
        out_specs=[col(T), col(T), col(3), col(1)],
        out_shape=[SDS((T, D_FF), bf16), SDS((T, D_FF), bf16), SDS((3, D_FF), f32), SDS((1, D_FF), f32)],
        scratch_shapes=[pltpu.VMEM((T, LANE), f32)] * 2, compiler_params=_cparams(VMEM_BIG), name=name)(up, up, w, b, dmid)


def _bmm(a, b, ca, cb, precision=HI):
    return lax.dot_general(a, b, (((ca,), (cb,)), ((0,), (0,))), precision=precision, preferred_element_type=f32)


def _make_bdot(ca, cb):
    def raw(x, y, cx, cy):
        return _bmm(x.astype(bf16), y.astype(bf16), cx, cy, precision=None)

    @jax.custom_vjp
    def f(a, b):
        return raw(a, b, ca, cb)

    def fwd(a, b):
        return raw(a, b, ca, cb), (a, b)

    def bwd(res, dy):
        a, b = res
        if (ca, cb) == (2, 1):
            return raw(dy, b, 2, 2), raw(a, dy, 1, 1)
        if (ca, cb) == (2, 2):
            return raw(dy, b, 2, 1), raw(dy, a, 1, 1)
        return raw(b, dy, 2, 2), raw(a, dy, 2, 1)

    f.defvjp(fwd, bwd)
    return f


_bdot_nn, _bdot_nt, _bdot_tn = _make_bdot(2, 1), _make_bdot(2, 2), _make_bdot(1, 1)


def _pieces(a, n):
    out, r = [], a
    for _ in range(n):
        p = r.astype(bf16)
        out.append(p)
        r = r - p.astype(f32)
    return out


def _bmm_split(x, y, cx, cy, nx=2, ny=2, order=1):
    xs, ys = _pieces(x, nx), _pieces(y, ny)
    acc = None
    for i in reversed(range(nx)):
        for j in reversed(range(ny)):
            if i + j <= order:
                t = _bmm(xs[i], ys[j], cx, cy, precision=None)
                acc = t if acc is None else acc + t
    return acc


@jax.custom_vjp
def _solve_apply(X, r):
    return _bmm_split(X, r, 2, 1)


def _solve_apply_fwd(X, r):
    return _bmm_split(X, r, 2, 1), (X, r)


def _solve_apply_bwd(res, dy):
    X, r = res
    return _bmm_split(dy, r, 2, 2), _bmm_split(X, dy, 1, 1)


_solve_apply.defvjp(_solve_apply_fwd, _solve_apply_bwd)


def _lower_ones(H, C):
    ri = lax.broadcasted_iota(jnp.int32, (H, C, C), 1)
    ci = lax.broadcasted_iota(jnp.int32, (H, C, C), 2)
    return (ri >= ci).astype(f32)


def _cumsum_rows_raw(G):
    return _bmm_split(_lower_ones(G.shape[0], G.shape[1]), G, 2, 1, nx=1, ny=3, order=2)


@jax.custom_vjp
def _cumsum_rows(G):
    return _cumsum_rows_raw(G)


def _cumsum_rows_fwd(G):
    return _cumsum_rows_raw(G), None


def _cumsum_rows_bwd(_, dy):
    return (_bmm_split(_lower_ones(dy.shape[0], dy.shape[1]), dy, 1, 1, nx=1, ny=3, order=2),)


_cumsum_rows.defvjp(_cumsum_rows_fwd, _cumsum_rows_bwd)


def _tri_inverse_raw(L):
    H, C, _ = L.shape
    ri = lax.broadcasted_iota(jnp.int32, (C, C), 0)
    ci = lax.broadcasted_iota(jnp.int32, (C, C), 1)
    eye = jnp.broadcast_to((ri == ci).astype(f32)[None], (H, C, C))
    Dg = jnp.where(((ri >> 3) == (ci >> 3))[None], L, 0.0)
    D2 = _bmm_split(Dg, Dg, 2, 1)
    X = _bmm_split(_bmm_split(eye - Dg, eye + D2, 2, 1), eye + _bmm_split(D2, D2, 2, 1), 2, 1)
    for lg in range(3, C.bit_length() - 1):
        same = (ri >> (lg + 1)) == (ci >> (lg + 1))
        lower_left = same & (((ri >> lg) & 1) == 1) & (((ci >> lg) & 1) == 0)
        X = X - _bmm_split(_bmm_split(X, jnp.where(lower_left[None], L, 0.0), 2, 1), X, 2, 1)
    return X


@jax.custom_vjp
def _tri_inverse(L):
    return _tri_inverse_raw(L)


def _tri_inverse_fwd(L):
    X = _tri_inverse_raw(L)
    return X, X


def _tri_inverse_bwd(X, dX):
    return (-_bmm_split(_bmm_split(X, dX, 1, 1), X, 2, 2),)


_tri_inverse.defvjp(_tri_inverse_fwd, _tri_inverse_bwd)


@jax.custom_vjp
def _tri_inverse_known(L, X):
    return X


def _tri_inverse_known_fwd(L, X):
    return X, X


def _tri_inverse_known_bwd(X, dX):
    return _tri_inverse_bwd(X, dX)[0], jnp.zeros_like(X)


_tri_inverse_known.defvjp(_tri_inverse_known_fwd, _tri_inverse_known_bwd)


DN_NC = 2


def _gdn_chunk(q, k, v, ab, alog, dtb, S, X_known=None, keep_X=False):
    H, C, _ = q.shape
    NC, NH = ab.shape[0], H // ab.shape[0]
    lane = lax.broadcasted_iota(jnp.int32, (H, C, LANE), 2)
    head = lax.broadcasted_iota(jnp.int32, (H, C, LANE), 0) & (NH - 1)
    abb = jnp.concatenate([jnp.broadcast_to(ab[c][None], (NH, C, LANE)) for c in range(NC)], axis=0)
    a = jnp.sum(jnp.where(lane == head, abb, 0.0), axis=2, keepdims=True)
    b = jnp.sum(jnp.where(lane == head + 8, abb, 0.0), axis=2, keepdims=True)
    pick = lax.broadcasted_iota(jnp.int32, (H, 1, LANE), 2) == (lax.broadcasted_iota(jnp.int32, (H, 1, LANE), 0) & (NH - 1))
    al = jnp.sum(jnp.where(pick, alog[None], 0.0), axis=2, keepdims=True)
    db = jnp.sum(jnp.where(pick, dtb[None], 0.0), axis=2, keepdims=True)
    g = -jnp.exp(al) * jax.nn.softplus(a + db)
    beta = jax.nn.sigmoid(b)
    ri = lax.broadcasted_iota(jnp.int32, (C, C), 0)
    ci = lax.broadcasted_iota(jnp.int32, (C, C), 1)
    G = jnp.broadcast_to(g, (H, C, LANE))
    gc = _cumsum_rows(G)
    gi = _cumsum_rows(jnp.broadcast_to(g, (H, C, C)))
    decay = jnp.exp(jnp.where((ri >= ci)[None], gi - jnp.swapaxes(gi, 1, 2), -jnp.inf))
    qs = q * (LANE ** -0.5)
    kb = k * beta
    L = jnp.where((ri > ci)[None], _bdot_nt(kb, k) * decay, 0.0)
    X = _tri_inverse(L) if X_known is None else _tri_inverse_known(L, X_known)
    egc = jnp.exp(gc)
    u = _solve_apply(X, v * beta)
    w = _solve_apply(X, kb * egc)
    qk = _bdot_nt(qs, k) * decay
    g_last = jnp.sum(G, axis=1, keepdims=True)
    k_dec = k * jnp.exp(g_last - gc)
    q_dec = qs * egc
    e_last = jnp.exp(g_last)
    outs = []
    for c in range(NC):
        sl = slice(c * NH, (c + 1) * NH)
        v_new = u[sl] - _bdot_nn(w[sl], S)
        outs.append(_bdot_nn(q_dec[sl], S) + _bdot_nn(qk[sl], v_new))
        S = S * e_last[sl] + _bdot_tn(k_dec[sl], v_new)
    o = jnp.concatenate(outs, axis=0)
    return (o, S, X) if keep_X else (o, S)


def _heads(ref):
    return jnp.stack([ref[DN_C * c:DN_C * (c + 1), LANE * h:LANE * (h + 1)] for c in range(DN_NC) for h in range(8)], axis=0)


def _put_heads(ref, val):
    for c in range(DN_NC):
        for h in range(8):
            ref[DN_C * c:DN_C * (c + 1), LANE * h:LANE * (h + 1)] = val[8 * c + h]


def _chunk_rows(ref):
    return jnp.stack([ref[DN_C * c:DN_C * (c + 1), :] for c in range(DN_NC)], axis=0)


def _hosted_parts(hosted):
    if hosted is None:
        return [], [], [], []
    n = len(hosted["arrays"])
    return list(hosted["arrays"]), [HBM_SPEC] * n, list(hosted["out_shape"]), _dma_sems(hosted["n_sems"])


def _gdn_fwd(name, q, k, v, proj, alog, dtb, hosted=None):
    T = q.shape[0]
    R = DN_C * DN_NC
    N = T // R
    h_args, h_specs, h_shapes, h_sems = _hosted_parts(hosted)
    nh = len(h_args)

    def body(q_ref, k_ref, v_ref, ab_ref, al_ref, dt_ref, *rest):
        h_ins, (o_ref, sall_ref, xinv_ref), h_outs, s_scr, sems = rest[:nh], rest[nh:nh + 3], rest[nh + 3:2 * nh + 3], rest[2 * nh + 3], rest[2 * nh + 4:]
        step = pl.program_id(0)

        @pl.when(step == 0)
        def _():
            s_scr[...] = jnp.zeros_like(s_scr)
            if hosted is not None:
                hosted["start"](h_ins, h_outs, *sems)

        S = s_scr[...]
        sall_ref[...] = S
        o, S_new, X = _gdn_chunk(_heads(q_ref), _heads(k_ref), _heads(v_ref), _chunk_rows(ab_ref), al_ref[...], dt_ref[...], S, keep_X=True)
        _put_heads(o_ref, o)
        s_scr[...] = S_new
        xinv_ref[...] = X

        if hosted is not None:
            @pl.when(step == N - 1)
            def _():
                hosted["finish"](h_ins, h_outs, *sems)

    blk = pl.BlockSpec((R, 8 * LANE), lambda n: (n, 0))
    vec = pl.BlockSpec((1, LANE), lambda n: (0, 0))
    state = pl.BlockSpec((None, 8, LANE, LANE), lambda n: (n, 0, 0, 0))
    xinv = pl.BlockSpec((None, 8 * DN_NC, DN_C, DN_C), lambda n: (n, 0, 0, 0))
    outs = _pcall(
        body, grid=(N,), in_specs=[blk, blk, blk, pl.BlockSpec((R, LANE), lambda n: (n, CB_AB)), vec, vec] + h_specs,
        out_specs=[blk, state, xinv] + h_specs,
        out_shape=[SDS((T, 1024), f32), SDS((N, 8, LANE, LANE), f32), SDS((N, 8 * DN_NC, DN_C, DN_C), f32)] + h_shapes,
        scratch_shapes=[pltpu.VMEM((8, LANE, LANE), f32)] + h_sems, name=name)(q, k, v, proj, alog, dtb, *h_args)
    return outs[0], outs[1], outs[2], list(outs[3:])


def _gdn_bwd(name, q, k, v, proj, alog, dtb, sall, xinv, do, hosted=None):
    T = q.shape[0]
    R = DN_C * DN_NC
    N = T // R
    h_args, h_specs, h_shapes, h_sems = _hosted_parts(hosted)
    nh = len(h_args)

    def body(q_ref, k_ref, v_ref, ab_ref, al_ref, dt_ref, s_ref, x_ref, do_ref, *rest):
        h_ins, h_outs, ds_scr, sems = rest[:nh], rest[nh + 6:2 * nh + 6], rest[2 * nh + 6], rest[2 * nh + 7:]
        dq_ref, dk_ref, dv_ref, dab_ref, dal_ref, ddt_ref = rest[nh:nh + 6]
        step = pl.program_id(0)

        @pl.when(step == 0)
        def _():
            ds_scr[...] = jnp.zeros_like(ds_scr)
            dal_ref[...] = jnp.zeros_like(dal_ref)
            ddt_ref[...] = jnp.zeros_like(ddt_ref)
            if hosted is not None:
                hosted["start"](h_ins, h_outs, *sems)

        _, vjp = jax.vjp(functools.partial(_gdn_chunk, X_known=x_ref[...]), _heads(q_ref), _heads(k_ref), _heads(v_ref), _chunk_rows(ab_ref),
                         al_ref[...], dt_ref[...], s_ref[...])
        dq, dk, dv, dab, dal, ddt, dS = vjp((_heads(do_ref), ds_scr[...]))
        _put_heads(dq_ref, dq)
        _put_heads(dk_ref, dk)
        _put_heads(dv_ref, dv)
        ds_scr[...] = dS
        for c in range(DN_NC):
            dab_ref[DN_C * c:DN_C * (c + 1), :] = dab[c]
        dal_ref[...] += dal
        ddt_ref[...] += ddt

        if hosted is not None:
            @pl.when(step == N - 1)
            def _():
                hosted["finish"](h_ins, h_outs, *sems)

    blk = pl.BlockSpec((R, 8 * LANE), lambda n: (N - 1 - n, 0))
    vec = pl.BlockSpec((1, LANE), lambda n: (0, 0))
    state = pl.BlockSpec((None, 8, LANE, LANE), lambda n: (N - 1 - n, 0, 0, 0))
    outs = _pcall(
        body, grid=(N,),
        in_specs=[blk, blk, blk, pl.BlockSpec((R, LANE), lambda n: (N - 1 - n, CB_AB)), vec, vec, state,
                  pl.BlockSpec((None, 8 * DN_NC, DN_C, DN_C), lambda n: (N - 1 - n, 0, 0, 0)), blk] + h_specs,
        out_specs=[blk, blk, blk, pl.BlockSpec((R, LANE), lambda n: (N - 1 - n, 0)), vec, vec] + h_specs,
        out_shape=[SDS((T, 1024), f32)] * 3 + [SDS((T, LANE), f32), SDS((1, LANE), f32), SDS((1, LANE), f32)] + h_shapes,
        scratch_shapes=[pltpu.VMEM((8, LANE, LANE), f32)] + h_sems, name=name)(q, k, v, proj, alog, dtb, sall, xinv, do, *h_args)
    return tuple(outs[:6]), list(outs[6:])


def _segmean_raw(x2, bd):
    return jnp.concatenate([_dot(x2[:, LANE * j:LANE * (j + 1)], bd, 1, 0) for j in range(x2.shape[1] // LANE)], axis=1)


@jax.custom_vjp
def _segmean(x2, bd):
    return _segmean_raw(x2, bd)


def _segmean_fwd(x2, bd):
    return _segmean_raw(x2, bd), bd


def _segmean_bwd(bd, dy):
    return _segmean_raw(dy, bd), jnp.zeros_like(bd)


_segmean.defvjp(_segmean_fwd, _segmean_bwd)


def _qknorm_fn(x, w, bd):
    return x * lax.rsqrt(_segmean(x * x, bd) + EPS) * w


def _rope_apply(xn, c, s1, s2):
    W = xn.shape[1]
    return xn * c + pltpu.roll(xn, W - 8, 1) * s1 + pltpu.roll(xn, 8, 1) * s2


def _rope_apply_t(d, c, s1, s2):
    W = d.shape[1]
    return d * c + pltpu.roll(d * s1, 8, 1) + pltpu.roll(d * s2, W - 8, 1)


def _rope_tiles(refs, width):
    return [jnp.tile(r[...], (1, width // LANE)) for r in refs]


def _qkprep_fwd(name, proj, wb, width, w, bd, tabs):
    T = proj.shape[0]
    tr = _pick(T, (256, 128))

    def body(x_ref, w_ref, bd_ref, c_ref, s1_ref, s2_ref, o_ref):
        xn = _qknorm_fn(x_ref[...], w_ref[...], bd_ref[...])
        o_ref[...] = _rope_apply(xn, *_rope_tiles((c_ref, s1_ref, s2_ref), width))

    row0 = pl.BlockSpec((tr, width), lambda i: (i, 0))
    tab = pl.BlockSpec((tr, LANE), lambda i: (i, 0))
    full = lambda a: pl.BlockSpec(a.shape, lambda i: (0, 0))
    return _pcall(
        body, grid=(T // tr,), in_specs=[pl.BlockSpec((tr, width), lambda i: (i, wb)), full(w), full(bd), tab, tab, tab],
        out_specs=row0, out_shape=SDS((T, width), f32), name=name)(proj, w, bd, *tabs)


def _qkprep_bwd(name, proj, wb, width, w, bd, tabs, dout, dest_buf="none"):
    T = proj.shape[0]
    tr = _pick(T, (256, 128))
    into = not isinstance(dest_buf, str)
    b_args, b_specs, aliases = _into_buffer(dest_buf, 7, 0) if into else ([], [], {})

    def body(x_ref, w_ref, bd_ref, c_ref, s1_ref, s2_ref, do_ref, *rest):
        dx_ref, dw_ref = rest[len(b_args):]
        i = pl.program_id(0)
        dxn = _rope_apply_t(do_ref[...], *_rope_tiles((c_ref, s1_ref, s2_ref), width))
        bd = bd_ref[...]
        _, vjp = jax.vjp(lambda x, w_: _qknorm_fn(x, w_, bd), x_ref[...], w_ref[...])
        dx, dw = vjp(dxn)
        dx_ref[...] = dx.astype(dx_ref.dtype)

        @pl.when(i == 0)
        def _():
            dw_ref[...] = jnp.zeros_like(dw_ref)

        dw_ref[...] += dw

    row0 = pl.BlockSpec((tr, width), lambda i: (i, 0))
    tab = pl.BlockSpec((tr, LANE), lambda i: (i, 0))
    full = lambda a: pl.BlockSpec(a.shape, lambda i: (0, 0))
    dx_spec = pl.BlockSpec((tr, width), lambda i: (i, wb)) if into else row0
    dx_shape = SDS((T, PROJ_W), bf16) if into else SDS((T, width), bf16)
    return _pcall(
        body, grid=(T // tr,), in_specs=[pl.BlockSpec((tr, width), lambda i: (i, wb)), full(w), full(bd), tab, tab, tab, row0] + b_specs,
        out_specs=[dx_spec, full(w)], out_shape=[dx_shape, SDS(w.shape, f32)], input_output_aliases=aliases, name=name)(
            proj, w, bd, *tabs, dout, *b_args)


def _make_dot16(ca, cb):
    def raw(x, y, cx, cy):
        return _dot(x.astype(bf16), y.astype(bf16), cx, cy, precision=None)

    @jax.custom_vjp
    def f(a, b):
        return raw(a, b, ca, cb)

    def fwd(a, b):
        return raw(a, b, ca, cb), (a, b)

    def bwd(res, dy):
        a, b = res
        if (ca, cb) == (1, 0):
            return raw(dy, b, 1, 1), raw(a, dy, 0, 0)
        return raw(dy, b, 1, 0), raw(dy, a, 0, 0)

    f.defvjp(fwd, bwd)
    return f


_dot16_nn, _dot16_nt = _make_dot16(1, 0), _make_dot16(1, 1)


def _attn_bias():
    qi = jnp.arange(8 * SWA_B) % SWA_B
    kj = jnp.arange(2 * SWA_B)
    rel = qi[:, None] + SWA_B - kj[None, :]
    valid = (rel >= 0) & (rel < SWA_B)
    neg = jnp.float32(-jnp.inf)
    return jnp.stack([jnp.where(valid & (kj[None, :] >= SWA_B), 0.0, neg), jnp.where(valid, 0.0, neg)]).astype(f32)


def _attn_group(qg, kb, vb, sinks, bias, hk):
    R = qg.shape[0]
    s = _dot16_nt(qg, kb) * 0.125 + bias
    head = (lax.broadcasted_iota(jnp.int32, (R, LANE), 0) >> 7) + 8 * hk
    lane = lax.broadcasted_iota(jnp.int32, (R, LANE), 1)
    sink = jnp.sum(jnp.where(lane == head, jnp.broadcast_to(sinks, (R, LANE)), 0.0), axis=1, keepdims=True)
    m = lax.stop_gradient(jnp.maximum(jnp.max(s, axis=1, keepdims=True), sink))
    p = jnp.exp(s - m)
    denom = jnp.sum(p, axis=1, keepdims=True) + jnp.exp(sink - m)
    return _dot16_nn(p / denom, vb)


def _group_rows(ref, hk):
    return jnp.concatenate([ref[:, 64 * (8 * hk + g):64 * (8 * hk + g + 1)] for g in range(8)], axis=0)


def _put_group(ref, hk, val):
    for g in range(8):
        ref[:, 64 * (8 * hk + g):64 * (8 * hk + g + 1)] = val[SWA_B * g:SWA_B * (g + 1)].astype(ref.dtype)


def _attn_specs():
    qs = pl.BlockSpec((SWA_B, 1024), lambda i: (i, 0))
    cur = pl.BlockSpec((SWA_B, LANE), lambda i: (i, 0))
    prev = pl.BlockSpec((SWA_B, LANE), lambda i: (jnp.maximum(i - 1, 0), 0))
    vcur = pl.BlockSpec((SWA_B, LANE), lambda i: (i, CB_SWV))
    vprev = pl.BlockSpec((SWA_B, LANE), lambda i: (jnp.maximum(i - 1, 0), CB_SWV))
    vec = pl.BlockSpec((1, LANE), lambda i: (0, 0))
    bias = pl.BlockSpec((None, 8 * SWA_B, 2 * SWA_B), lambda i: (jnp.minimum(i, 1), 0, 0))
    return qs, cur, prev, vcur, vprev, vec, bias


def _hosted_edge(hosted, which, h_ins, h_outs, sems, at):
    if hosted is None:
        return

    @pl.when(at)
    def _():
        hosted[which](h_ins, h_outs, *sems)


def _attn_fwd(name, sq, sk, proj, sinks, bias, hosted=None):
    T = sq.shape[0]
    nb = T // SWA_B
    h_args, h_specs, h_shapes, h_sems = _hosted_parts(hosted)
    nh = len(h_args)

    def body(q_ref, kp_ref, kc_ref, vp_ref, vc_ref, sk_ref, b_ref, *rest):
        h_ins, o_ref, h_outs, sems = rest[:nh], rest[nh], rest[nh + 1:2 * nh + 1], rest[2 * nh + 1:]
        _hosted_edge(hosted, "start", h_ins, h_outs, sems, pl.program_id(0) == 0)
        sinks_v, bias_v = sk_ref[...], b_ref[...]
        for hk in range(2):
            ks = slice(64 * hk, 64 * hk + 64)
            kb = jnp.concatenate([kp_ref[:, ks], kc_ref[:, ks]], axis=0)
            vb = jnp.concatenate([vp_ref[:, ks], vc_ref[:, ks]], axis=0)
            _put_group(o_ref, hk, _attn_group(_group_rows(q_ref, hk), kb, vb, sinks_v, bias_v, hk))
        _hosted_edge(hosted, "finish", h_ins, h_outs, sems, pl.program_id(0) == nb - 1)

    qs, cur, prev, vcur, vprev, vec, bspec = _attn_specs()
    outs = _pcall(body, grid=(nb,), in_specs=[qs, prev, cur, vprev, vcur, vec, bspec] + h_specs, out_specs=[qs] + h_specs,
                  out_shape=[SDS((T, 1024), bf16)] + h_shapes, scratch_shapes=h_sems, name=name)(sq, sk, sk, proj, proj, sinks, bias, *h_args)
    return outs[0], list(outs[1:])


def _attn_bwd(name, sq, sk, proj, sinks, bias, do, hosted=None):
    T = sq.shape[0]
    nb = T // SWA_B
    h_args, h_specs, h_shapes, h_sems = _hosted_parts(hosted)
    nh = len(h_args)

    def body(q_ref, kp_ref, kc_ref, vp_ref, vc_ref, sk_ref, b_ref, do_ref, *rest):
        h_ins, h_outs, sems = rest[:nh], rest[nh + 6:2 * nh + 6], rest[2 * nh + 6:]
        dq_ref, dkp_ref, dkc_ref, dvp_ref, dvc_ref, dsk_ref = rest[nh:nh + 6]
        _hosted_edge(hosted, "start", h_ins, h_outs, sems, pl.program_id(0) == 0)

        @pl.when(pl.program_id(0) == 0)
        def _():
            dsk_ref[...] = jnp.zeros_like(dsk_ref)

        sinks_v, bias_v = sk_ref[...], b_ref[...]
        dsk = jnp.zeros((1, LANE), f32)
        for hk in range(2):
            ks = slice(64 * hk, 64 * hk + 64)
            kb = jnp.concatenate([kp_ref[:, ks], kc_ref[:, ks]], axis=0)
            vb = jnp.concatenate([vp_ref[:, ks], vc_ref[:, ks]], axis=0)
            _, vjp = jax.vjp(functools.partial(_attn_group, bias=bias_v, hk=hk), _group_rows(q_ref, hk), kb, vb, sinks_v)
            dq, dkb, dvb, ds_ = vjp(_group_rows(do_ref, hk))
            _put_group(dq_ref, hk, dq)
            dsk = dsk + ds_
            dkp_ref[:, ks] = dkb[:SWA_B]
            dkc_ref[:, ks] = dkb[SWA_B:]
            dvp_ref[:, ks] = dvb[:SWA_B]
            dvc_ref[:, ks] = dvb[SWA_B:]
        dsk_ref[...] += dsk
        _hosted_edge(hosted, "finish", h_ins, h_outs, sems, pl.program_id(0) == nb - 1)

    qs, cur, prev, vcur, vprev, vec, bspec = _attn_specs()
    outs = _pcall(
        body, grid=(nb,), in_specs=[qs, prev, cur, vprev, vcur, vec, bspec, qs] + h_specs, out_specs=[qs, cur, cur, cur, cur, vec] + h_specs,
        out_shape=[SDS((T, 1024), f32)] + [SDS((T, LANE), f32)] * 4 + [SDS((1, LANE), f32)] + h_shapes, scratch_shapes=h_sems,
        name=name)(sq, sk, sk, proj, proj, sinks, bias, do, *h_args)
    return tuple(outs[:6]), list(outs[6:])


def _shift_add(name, cur, prev, out_dtype):
    T = cur.shape[0]

    def body(c_ref, p_ref, o_ref):
        o_ref[0:T - SWA_B, :] = (c_ref[0:T - SWA_B, :] + p_ref[SWA_B:T, :]).astype(o_ref.dtype)
        o_ref[T - SWA_B:T, :] = c_ref[T - SWA_B:T, :].astype(o_ref.dtype)

    return _pcall(body, out_shape=SDS((T, LANE), out_dtype), name=name)(cur, prev)


def _loss(name, y, tgt):
    T = y.shape[0]

    def body(y_ref, t_ref, l_ref, dy_ref):
        @pl.when(pl.program_id(0) == 0)
        def _():
            l_ref[...] = jnp.zeros_like(l_ref)

        d = y_ref[...] - t_ref[...]
        l_ref[...] += jnp.sum(d * d) * (0.5 / D)
        dy_ref[...] = d * (1.0 / D)

    row = pl.BlockSpec((TR, D), lambda i: (i, 0))
    return _pcall(body, grid=(T // TR,), in_specs=[row, row], out_specs=[pl.BlockSpec((8, LANE), lambda i: (0, 0)), row],
                  out_shape=[SDS((8, LANE), f32), SDS((T, D), f32)], name=name)(y, tgt)


def _adamw(name, w, g, m, v):
    shape = w.shape
    C = shape[-1]
    R = int(np.prod(shape[:-1]))
    tr = _pick(R, (128, 64, 16, 8))
    bc1 = np.float32(1.0 - ADAM_B1 ** ADAM_STEP)
    bc2 = np.float32(1.0 - ADAM_B2 ** ADAM_STEP)

    def body(w_ref, g_ref, m_ref, v_ref, d_ref, mo_ref, vo_ref):
        g_ = g_ref[...]
        m_ = ADAM_B1 * m_ref[...] + (1.0 - ADAM_B1) * g_
        v_ = ADAM_B2 * v_ref[...] + (1.0 - ADAM_B2) * (g_ * g_)
        d_ref[...] = -ADAM_LR * ((m_ / bc1) / (jnp.sqrt(v_ / bc2) + ADAM_EPS) + ADAM_WD * w_ref[...])
        mo_ref[...] = m_
        vo_ref[...] = v_

    blk = pl.BlockSpec((tr, C), lambda i: (i, 0))
    outs = _pcall(body, grid=(R // tr,), in_specs=[blk] * 4, out_specs=[blk] * 3, out_shape=[SDS((R, C), f32)] * 3,
                  compiler_params=_cparams(VMEM_BIG), name=name)(*[t.reshape(R, C) for t in (w, g, m, v)])
    return [o.reshape(shape) for o in outs]


def _silu_rows(name, x):
    def body(x_ref, o_ref):
        t = x_ref[...]
        o_ref[...] = (t * jax.nn.sigmoid(t)).astype(o_ref.dtype)

    return _pcall(body, out_shape=SDS(x.shape, bf16), name=name)(x)


def _sum_leading(name, x):
    n = x.shape[0]

    def body(x_ref, o_ref):
        acc = x_ref[0]
        for k in range(1, n):
            acc = acc + x_ref[k]
        o_ref[...] = acc

    tr = x.shape[1] if x.size * 4 <= 8 * 2 ** 20 else _pick(x.shape[1], (COMM_TR, 8))
    return _pcall(body, grid=(x.shape[1] // tr,), in_specs=[pl.BlockSpec((n, tr, x.shape[2]), lambda i: (0, i, 0))],
                  out_specs=pl.BlockSpec((tr, x.shape[2]), lambda i: (i, 0)), out_shape=SDS(x.shape[1:], x.dtype), name=name)(x)


HBM_SPEC = pl.BlockSpec(memory_space=pltpu.HBM)


def _position():
    x, y, c = lax.axis_index("x"), lax.axis_index("y"), lax.axis_index("c")
    return x, y, c, [(1 - x, y), (x, 1 - y), (1 - x, 1 - y)]


def _remote(src, dst, send_sems, recv_sems, k, to):
    return pltpu.make_async_remote_copy(src_ref=src, dst_ref=dst, send_sem=send_sems.at[k], recv_sem=recv_sems.at[k],
                                        device_id=to, device_id_type=MESH)


def _allgather_all(name, buf):
    r, W = buf.shape

    def body(in_ref, out_ref, send_sems, recv_sems, local_sem):
        x, y, c, _ = _position()
        me = 4 * x + 2 * y + c
        mine = pltpu.make_async_copy(in_ref, out_ref.at[me], local_sem)
        mine.start()
        peers = []
        for mk in range(1, 8):
            mx, my, mc = (mk >> 2) & 1, (mk >> 1) & 1, mk & 1
            px = 1 - x if mx else x
            py = 1 - y if my else y
            pc = 1 - c if mc else c
            peers.append((px, py, pc))
        sends = [_remote(in_ref, out_ref.at[me], send_sems, recv_sems, k, p) for k, p in enumerate(peers)]
        for cp in sends:
            cp.start()
        for k, (px, py, pc) in enumerate(peers):
            slot = out_ref.at[4 * px + 2 * py + pc]
            _remote(slot, slot, send_sems, recv_sems, k, (px, py, pc)).wait_recv()
        for cp in sends:
            cp.wait_send()
        mine.wait()

    return _pcall(body, in_specs=[HBM_SPEC], out_specs=HBM_SPEC, out_shape=SDS((8, r, W), buf.dtype),
                  scratch_shapes=[pltpu.SemaphoreType.DMA((7,)), pltpu.SemaphoreType.DMA((7,)), pltpu.SemaphoreType.DMA], name=name)(buf)


def _dma_sems(n):
    return [pltpu.SemaphoreType.DMA((n,)), pltpu.SemaphoreType.DMA((n,))]


def _gather_chips(name, shards):
    n = len(shards)

    def body(*refs):
        ins, outs = refs[:n], refs[n:2 * n]
        send_sems, recv_sems = refs[2 * n:]
        x, y, c, chips = _position()
        me = 2 * x + y
        sib = (x, y, 1 - c)
        sends, halves = [], []
        for i in range(n):
            rh = ins[i].shape[0] // 2
            halves.append((pl.ds(pl.multiple_of(c * rh, 16), rh), pl.ds(pl.multiple_of((1 - c) * rh, 16), rh)))
        for i in range(n):
            for j, (cx, cy) in enumerate(chips):
                cp = _remote(ins[i].at[halves[i][0]], outs[i].at[me, halves[i][0]], send_sems, recv_sems, 6 * i + j, (cx, cy, c))
                cp.start()
                sends.append(cp)
        for j, (cx, cy) in enumerate(chips):
            for i in range(n):
                rows = outs[i].at[2 * cx + cy, halves[i][0]]
                _remote(rows, rows, send_sems, recv_sems, 6 * i + j, (cx, cy, c)).wait_recv()
                cp = _remote(rows, rows, send_sems, recv_sems, 6 * i + 3 + j, sib)
                cp.start()
                sends.append(cp)
        for j, (cx, cy) in enumerate(chips):
            for i in range(n):
                rows = outs[i].at[2 * cx + cy, halves[i][1]]
                _remote(rows, rows, send_sems, recv_sems, 6 * i + 3 + j, sib).wait_recv()
        for cp in sends:
            cp.wait_send()

    return _pcall(body, in_specs=[HBM_SPEC] * n, out_specs=[HBM_SPEC] * n, out_shape=[SDS((4,) + s.shape, s.dtype) for s in shards],
                  scratch_shapes=_dma_sems(6 * n), name=name)(*shards)


def _swap_halves_multi(name, slots):
    n = len(slots)

    def body(*refs):
        ins, outs = refs[:n], refs[n:2 * n]
        send_sems, recv_sems = refs[2 * n:]
        x, y, c, _ = _position()
        cps = []
        for i in range(n):
            rh = ins[i].shape[1] // 2
            ohalf = pl.ds(pl.multiple_of((1 - c) * rh, 8), rh)
            cp = _remote(ins[i].at[:, ohalf, :], outs[i], send_sems, recv_sems, i, (x, y, 1 - c))
            cp.start()
            cps.append(cp)
        for cp in cps:
            cp.wait()

    return _pcall(body, in_specs=[HBM_SPEC] * n, out_specs=[HBM_SPEC] * n,
                  out_shape=[SDS((4, s.shape[1] // 2, s.shape[2]), s.dtype) for s in slots], scratch_shapes=_dma_sems(n), name=name)(*slots)


def _pair_add(name, g4, b1, c):
    _, R, W = g4.shape
    tr = _pick(R // 2, (256, 128, 32, 16))
    nblk = (R // 2) // tr

    def body(c_ref, g_ref, b_ref, o_ref):
        o_ref[...] = (g_ref[...] + b_ref[...]).astype(o_ref.dtype)

    grid_spec = pltpu.PrefetchScalarGridSpec(
        num_scalar_prefetch=1, grid=(4, nblk),
        in_specs=[pl.BlockSpec((None, tr, W), lambda s, i, c_ref: (s, c_ref[0] * nblk + i, 0)),
                  pl.BlockSpec((None, tr, W), lambda s, i, c_ref: (s, i, 0))],
        out_specs=pl.BlockSpec((None, tr, W), lambda s, i, c_ref: (s, i, 0)))
    return _pcall(body, grid_spec=grid_spec, out_shape=SDS((4, R // 2, W), bf16), name=name)(c.reshape(1), g4, b1)


def _scatter_chips_multi(name, ps):
    n = len(ps)

    def body(*refs):
        ins, outs = refs[:n], refs[n:2 * n]
        send_sems, recv_sems = refs[2 * n:]
        x, y, c, chips = _position()
        me = 2 * x + y
        sends = []
        for i in range(n):
            for j, (cx, cy) in enumerate(chips):
                cp = _remote(ins[i].at[2 * cx + cy], outs[i].at[me], send_sems, recv_sems, 3 * i + j, (cx, cy, c))
                cp.start()
                sends.append(cp)
        for i in range(n):
            for j, (cx, cy) in enumerate(chips):
                slot = outs[i].at[2 * cx + cy]
                _remote(slot, slot, send_sems, recv_sems, 3 * i + j, (cx, cy, c)).wait_recv()
        for cp in sends:
            cp.wait_send()

    return _pcall(body, in_specs=[HBM_SPEC] * n, out_specs=[HBM_SPEC] * n, out_shape=[SDS(p.shape, p.dtype) for p in ps],
                  scratch_shapes=_dma_sems(3 * n), name=name)(*ps)


def _sum_chips(name, p4, b2, chip, c):
    _, Rh, W = p4.shape
    tr = _pick(Rh, (256, 128, 32, 16))
    nblk = Rh // tr

    def body(m_ref, c_ref, own_ref, r1_ref, r2_ref, r3_ref, o_ref):
        o_ref[...] = ((own_ref[...].astype(f32) + r1_ref[...].astype(f32)) + r2_ref[...].astype(f32)) + r3_ref[...].astype(f32)

    other = lambda k: pl.BlockSpec((None, tr, W), lambda i, m_ref, c_ref: (m_ref[0] ^ k, i, 0))
    grid_spec = pltpu.PrefetchScalarGridSpec(
        num_scalar_prefetch=2, grid=(nblk,),
        in_specs=[pl.BlockSpec((None, tr, W), lambda i, m_ref, c_ref: (m_ref[0], i, 0)), other(1), other(2), other(3)],
        out_specs=pl.BlockSpec((tr, W), lambda i, m_ref, c_ref: (c_ref[0] * nblk + i, 0)))
    return _pcall(body, grid_spec=grid_spec, out_shape=SDS((2 * Rh, W), f32), name=name)(chip.reshape(1), c.reshape(1), p4, b2, b2, b2)


def _join_halves(name, fulls):
    n = len(fulls)

    def body(*refs):
        outs = refs[n:2 * n]
        send_sems, recv_sems = refs[2 * n:]
        x, y, c, _ = _position()
        cps = []
        for i in range(n):
            rh = outs[i].shape[0] // 2
            mine = outs[i].at[pl.ds(pl.multiple_of(c * rh, 8), rh)]
            theirs = outs[i].at[pl.ds(pl.multiple_of((1 - c) * rh, 8), rh)]
            cp = _remote(mine, mine, send_sems, recv_sems, i, (x, y, 1 - c))
            cp.start()
            cps.append((cp, _remote(theirs, theirs, send_sems, recv_sems, i, (x, y, 1 - c))))
        for cp, back in cps:
            back.wait_recv()
            cp.wait_send()

    return _pcall(body, in_specs=[HBM_SPEC] * n, out_specs=[HBM_SPEC] * n, out_shape=[SDS(r.shape, r.dtype) for r in fulls],
                  input_output_aliases={i: i for i in range(n)}, scratch_shapes=_dma_sems(n), name=name)(*fulls)


def _hosted_gather(shards):
    n = len(shards)

    def half(ref_rows, c):
        rh = ref_rows // 2
        return pl.ds(pl.multiple_of(c * rh, 16), rh)

    def start(ins, outs, send_sems, recv_sems):
        x, y, c, chips = _position()
        me = 2 * x + y
        for i in range(n):
            rows = half(ins[i].shape[0], c)
            for j, (cx, cy) in enumerate(chips):
                _remote(ins[i].at[rows], outs[i].at[me, rows], send_sems, recv_sems, 3 * i + j, (cx, cy, c)).start()

    def finish(ins, outs, send_sems, recv_sems):
        x, y, c, chips = _position()
        me = 2 * x + y
        for i in range(n):
            rows = half(ins[i].shape[0], c)
            for j, (cx, cy) in enumerate(chips):
                _remote(ins[i].at[rows], outs[i].at[2 * cx + cy, rows], send_sems, recv_sems, 3 * i + j, (cx, cy, c)).wait_recv()
        for i in range(n):
            rows = half(ins[i].shape[0], c)
            for j, (cx, cy) in enumerate(chips):
                _remote(ins[i].at[rows], outs[i].at[me, rows], send_sems, recv_sems, 3 * i + j, (cx, cy, c)).wait_send()

    return {"arrays": shards, "out_shape": [SDS((4,) + s.shape, s.dtype) for s in shards], "n_sems": 3 * n, "start": start, "finish": finish}


def _gather_forward(name, gathered):
    n = len(gathered)

    def body(*refs):
        outs = refs[n:2 * n]
        send_sems, recv_sems = refs[2 * n:]
        x, y, c, chips = _position()
        sib = (x, y, 1 - c)
        sends = []
        for i in range(n):
            rh = outs[i].shape[1] // 2
            mine = pl.ds(pl.multiple_of(c * rh, 16), rh)
            for j, (cx, cy) in enumerate(chips):
                rows = outs[i].at[2 * cx + cy, mine]
                cp = _remote(rows, rows, send_sems, recv_sems, 3 * i + j, sib)
                cp.start()
                sends.append(cp)
        for i in range(n):
            rh = outs[i].shape[1] // 2
            theirs = pl.ds(pl.multiple_of((1 - c) * rh, 16), rh)
            for j, (cx, cy) in enumerate(chips):
                rows = outs[i].at[2 * cx + cy, theirs]
                _remote(rows, rows, send_sems, recv_sems, 3 * i + j, sib).wait_recv()
        for cp in sends:
            cp.wait_send()

    return _pcall(body, in_specs=[HBM_SPEC] * n, out_specs=[HBM_SPEC] * n, out_shape=[SDS(g.shape, g.dtype) for g in gathered],
                  input_output_aliases={i: i for i in range(n)}, scratch_shapes=_dma_sems(3 * n), name=name)(*gathered)


def _hosted_scatter(ps):
    n = len(ps)

    def start(ins, outs, send_sems, recv_sems):
        x, y, c, chips = _position()
        me = 2 * x + y
        for i in range(n):
            for j, (cx, cy) in enumerate(chips):
                _remote(ins[i].at[2 * cx + cy], outs[i].at[me], send_sems, recv_sems, 3 * i + j, (cx, cy, c)).start()

    def finish(ins, outs, send_sems, recv_sems):
        x, y, c, chips = _position()
        me = 2 * x + y
        for i in range(n):
            for j, (cx, cy) in enumerate(chips):
                slot = outs[i].at[2 * cx + cy]
                _remote(slot, slot, send_sems, recv_sems, 3 * i + j, (cx, cy, c)).wait_recv()
        for i in range(n):
            for j, (cx, cy) in enumerate(chips):
                _remote(ins[i].at[2 * cx + cy], outs[i].at[me], send_sems, recv_sems, 3 * i + j, (cx, cy, c)).wait_send()

    return {"arrays": ps, "out_shape": [SDS(p.shape, p.dtype) for p in ps], "n_sems": 3 * n, "start": start, "finish": finish}


def _hosted_swap(slots):
    n = len(slots)

    def copies(ins, outs, send_sems, recv_sems):
        x, y, c, _ = _position()
        cps = []
        for i in range(n):
            rh = ins[i].shape[1] // 2
            ohalf = pl.ds(pl.multiple_of((1 - c) * rh, 8), rh)
            cps.append(_remote(ins[i].at[:, ohalf, :], outs[i], send_sems, recv_sems, i, (x, y, 1 - c)))
        return cps

    def start(ins, outs, send_sems, recv_sems):
        for cp in copies(ins, outs, send_sems, recv_sems):
            cp.start()

    def finish(ins, outs, send_sems, recv_sems):
        for cp in copies(ins, outs, send_sems, recv_sems):
            cp.wait()

    return {"arrays": slots, "out_shape": [SDS((4, s.shape[1] // 2, s.shape[2]), s.dtype) for s in slots], "n_sems": n,
            "start": start, "finish": finish}


def _rs_begin(slots, c):
    b1 = _swap_halves_multi("rs_swap", slots)
    return [_pair_add("rs_pair_%d" % i, g, b, c) for i, (g, b) in enumerate(zip(slots, b1))]


def _rs_end(ps, b2, c, chip):
    return _join_halves("rs_join", [_sum_chips("rs_sum_%d" % i, p, b, chip, c) for i, (p, b) in enumerate(zip(ps, b2))])


def _reduce_scatter_multi(slots, c, chip):
    ps = _rs_begin(slots, c)
    return _rs_end(ps, _scatter_chips_multi("rs_scatter", ps), c, chip)


_BIG = ("w_in", "w_dn_out", "w_swa_out", "w_o", "w_up", "w_down")


_W_IN_PIECES = ((0, 3072, 0), (3072, 4096, 3072), (5392, 6416, 4096), (6416, 7440, 5120), (4112, 5136, 6144), (4096, 4112, 7168),
                (5136, 5264, 7296), (5264, 5392, 7424))
_W_IN_SHARD = IN_TOTAL // 4


def _w_in_from_slots(g):
    parts, at = [], 0
    for lo, hi, dst in _W_IN_PIECES:
        if dst > at:
            parts.append(jnp.zeros((g.shape[1], dst - at), g.dtype))
        for s in range(4):
            a, b = max(lo, s * _W_IN_SHARD), min(hi, (s + 1) * _W_IN_SHARD)
            if a < b:
                parts.append(g[s][:, a - s * _W_IN_SHARD:b - s * _W_IN_SHARD])
        at = dst + hi - lo
    parts.append(jnp.zeros((g.shape[1], PROJ_W - at), g.dtype))
    return jnp.concatenate(parts, axis=1)


def _w_in_to_slots(gw):
    slots = []
    for s in range(4):
        parts = []
        for lo, hi, dst in sorted(_W_IN_PIECES):
            a, b = max(lo, s * _W_IN_SHARD), min(hi, (s + 1) * _W_IN_SHARD)
            if a < b:
                parts.append(gw[:, dst + a - lo:dst + b - lo])
        slots.append(jnp.concatenate(parts, axis=1))
    return jnp.stack(slots)


def _assemble_mixer(gs):
    rows = lambda g: g.reshape(4 * g.shape[1], g.shape[2])
    return {"w_in": _w_in_from_slots(gs[0]), "w_dn_out": rows(gs[1]), "w_swa_out": rows(gs[2]), "w_o": rows(gs[3])}


def _grad_slots(gw):
    rows = lambda g: g.reshape(4, g.shape[0] // 4, g.shape[1])
    return [_w_in_to_slots(gw["w_in"]), rows(gw["w_dn_out"]), rows(gw["w_swa_out"]), rows(gw["w_o"]), gw["w_up"], rows(gw["w_down"])]


def _pad_lanes(v, n=LANE):
    return jnp.pad(v, (0, n - v.shape[0])).reshape(1, n)


def _layer_consts(P):
    K = {}
    K["norm_mix"] = P["norm_mix"].reshape(1, D)
    K["norm_ffn"] = P["norm_ffn"].reshape(1, D)
    K["alog"] = _pad_lanes(P["dn_a_log"])
    K["dtb"] = _pad_lanes(P["dn_dt_bias"])
    K["dn_norm"] = jnp.tile(P["dn_norm"], 8).reshape(1, D)
    K["qn"] = jnp.tile(P["swa_q_norm"], 16).reshape(1, D)
    K["kn"] = jnp.tile(P["swa_k_norm"], 2).reshape(1, LANE)
    K["sinks"] = _pad_lanes(P["swa_sinks"])
    K["ffn_b"] = P["ffn_conv_b"].reshape(1, D_FF)
    return K


def _layer_fwd(x, mod, W, K, tabs, bd, hosted=None, late=None):
    late = late or {}
    sh1, sc1, gt1, sh2, sc2, gt2 = mod
    S = {"x": x}
    h1, h1t = _rowwise_fwd("normmod1_fwd", _normmod_fn, [(x, 0, D)], [K["norm_mix"], sc1, sh1], [D], [bf16], also_transposed=True)
    if "proj" in late:
        proj, arrived = _matmul("proj_fwd", h1, W["w_in"], "nn", f32, hosted=late["proj"][0])
        W.update(late["proj"][1](arrived))
    else:
        proj = _matmul("proj_fwd", h1, W["w_in"], "nn", f32)
    qn = _dnconv_fwd("dnconv_q_fwd", proj, CB_Q, W["dn_conv"], True)
    kn = _dnconv_fwd("dnconv_k_fwd", proj, CB_K, W["dn_conv"], True)
    vc = _dnconv_fwd("dnconv_v_fwd", proj, CB_V, W["dn_conv"], False)
    o, sall, xinv, hosted_out = _gdn_fwd("gdn_fwd", qn, kn, vc, proj, K["alog"], K["dtb"], hosted=hosted)
    (on,) = _rowwise_fwd("dngate_fwd", _dngate_fn, [(o, 0, D), (proj, WB_Z, D)], [K["dn_norm"]], [D], [bf16])
    ya = _matmul("dnout_fwd", on, W["w_dn_out"], "nn", f32)
    sq = _qkprep_fwd("qprep_fwd", proj, WB_SWQ, D, K["qn"], bd[0], tabs[0])
    sk = _qkprep_fwd("kprep_fwd", proj, CB_SWK, LANE, K["kn"], bd[1], tabs[1])
    attn, arrived = _attn_fwd("attn_fwd", sq, sk, proj, K["sinks"], K["attn_bias"], hosted=late["attn"][0] if "attn" in late else None)
    if "attn" in late:
        W.update(late["attn"][1](arrived))
    yb = _matmul("swaout_fwd", attn, W["w_swa_out"], "nn", f32)
    (merged,) = _rowwise_fwd("merge_fwd", _merge_fn, [(proj, WB_GA, D), (proj, WB_GB, D), (ya, 0, D), (yb, 0, D)], [], [D], [bf16])
    t1, x1 = _matmul("wo_fwd", merged, W["w_o"], "nn", f32, resid=(x, gt1))
    h2, h2t = _rowwise_fwd("normmod2_fwd", _normmod_fn, [(x1, 0, D)], [K["norm_ffn"], sc2, sh2], [D], [bf16], also_transposed=True)
    up = _matmul("up_fwd", h2, W["w_up"], "nn", f32, b_slots=True)
    mid = _ffnact_fwd("ffnact_fwd", up, W["ffn_conv"], K["ffn_b"])
    t2, x2 = _matmul("down_fwd", mid, W["w_down"], "nn", f32, resid=(x1, gt2))
    S.update(h1t=h1t, h2t=h2t, proj=proj, qn=qn, kn=kn, vc=vc, o=o, sall=sall, xinv=xinv, on=on, ya=ya, sq=sq, sk=sk, attn=attn, yb=yb,
             merged=merged, t1=t1, x1=x1, h2=h2, up=up, mid=mid, t2=t2)
    return x2, S, hosted_out


def _layer_bwd(dx2, S, mod, W, K, tabs, bd, carry=None, early=None):
    sh1, sc1, gt1, sh2, sc2, gt2 = mod
    x, x1, proj, up = S["x"], S["x1"], S["proj"], S["up"]
    T = x.shape[0]
    gw, gs = {}, {}
    dt2, dgt2 = _rowwise_bwd("resid2_bwd", _resid_fn, [(x1, 0, D), (S["t2"], 0, D)], [gt2], [(dx2, 0, D)], [None, bf16])
    dmid = _matmul("down_bwd_x", dt2, W["w_down"], "nt", bf16)
    gw["w_down"] = _matmul("down_bwd_w", S["mid"], dt2, "tn", f32)
    dact, dlin, gw["ffn_conv"], dffn_b = _ffnact_bwd("ffnact_bwd", up, W["ffn_conv"], K["ffn_b"], dmid)
    dup = (dact, dlin)
    dh2 = _matmul("up_bwd_x", dup, W["w_up"], "nt", f32, b_slots=True)
    if carry is None:
        gw["w_up"] = _matmul("up_bwd_w", S["h2t"], dup, "nn", f32, out_slots=4)
        pair_sums = hosted = None
    else:
        gw["w_up"], b1 = _matmul("up_bwd_w", S["h2t"], dup, "nn", f32, out_slots=4, hosted=_hosted_swap(carry[0]))
        pair_sums = [_pair_add("rs_pair_%d" % i, g, b, carry[1]) for i, (g, b) in enumerate(zip(carry[0], b1))]
        hosted = _hosted_scatter(pair_sums)
    dx1, dnorm_ffn, dsc2, dsh2 = _rowwise_bwd("normmod2_bwd", _normmod_fn, [(x1, 0, D)], [K["norm_ffn"], sc2, sh2], [(dh2, 0, D)], [f32],
                                              add_to_first=(dx2, 0, D))
    early_out = None
    if early is not None:
        ffn_slots = [gw["w_up"], gw["w_down"].reshape(4, D_FF // 4, D)]
        ffn_ps = [_pair_add("rs_pair_ffn_%d" % i, g, b, early[0]) for i, (g, b) in enumerate(zip(ffn_slots, _swap_halves_multi("rs_swap_ffn", ffn_slots)))]
    dt1, dgt1 = _rowwise_bwd("resid1_bwd", _resid_fn, [(x, 0, D), (S["t1"], 0, D)], [gt1], [(dx1, 0, D)], [None, bf16])
    dmerged = _matmul("wo_bwd_x", dt1, W["w_o"], "nt", f32)
    gw["w_o"] = _matmul("wo_bwd_w", S["merged"], dt1, "tn", f32)
    dproj, dya, dyb = _rowwise_bwd("merge_bwd", _merge_fn, [(proj, WB_GA, D), (proj, WB_GB, D), (S["ya"], 0, D), (S["yb"], 0, D)], [],
                                   [(dmerged, 0, D)], [bf16, bf16, bf16, bf16], dest=(None, (0, 1), WB_GA // 2))
    don = _matmul("dnout_bwd_x", dya, W["w_dn_out"], "nt", f32)
    gw["w_dn_out"] = _matmul("dnout_bwd_w", S["on"], dya, "tn", f32)
    dproj, do, ddn_norm = _rowwise_bwd("dngate_bwd", _dngate_fn, [(S["o"], 0, D), (proj, WB_Z, D)], [K["dn_norm"]], [(don, 0, D)],
                                       [f32, bf16], dest=(dproj, (1,), WB_Z))
    (dqn, dkn, dvc, dab, dalog, ddtb), hosted_out = _gdn_bwd("gdn_bwd", S["qn"], S["kn"], S["vc"], proj, K["alog"], K["dtb"], S["sall"], S["xinv"], do,
                                                            hosted=hosted)
    dproj, dwq = _dnconv_bwd("dnconv_q_bwd", proj, CB_Q, W["dn_conv"], dqn, True, dproj)
    dproj, dwk = _dnconv_bwd("dnconv_k_bwd", proj, CB_K, W["dn_conv"], dkn, True, dproj)
    dproj, dwv = _dnconv_bwd("dnconv_v_bwd", proj, CB_V, W["dn_conv"], dvc, False, dproj)
    gw["dn_conv"] = jnp.concatenate([dwq, dwk, dwv], axis=1)
    dattn = _matmul("swaout_bwd_x", dyb, W["w_swa_out"], "nt", f32)
    gw["w_swa_out"] = _matmul("swaout_bwd_w", S["attn"], dyb, "tn", f32)
    (dsq, dkp, dkc, dvp, dvc_, dsinks), ffn_b2 = _attn_bwd("attn_bwd", S["sq"], S["sk"], proj, K["sinks"], K["attn_bias"], dattn,
                                                          hosted=None if early is None else _hosted_scatter(ffn_ps))
    if early is not None:
        early_out = _rs_end(ffn_ps, ffn_b2, early[0], early[1])
    dsk = _shift_add("attn_dk_join", dkc, dkp, f32)
    dswv = _shift_add("attn_dv_join", dvc_, dvp, bf16)
    dproj, dqn_w = _qkprep_bwd("qprep_bwd", proj, WB_SWQ, D, K["qn"], bd[0], tabs[0], dsq, dest_buf=dproj)
    dswk, dkn_w = _qkprep_bwd("kprep_bwd", proj, CB_SWK, LANE, K["kn"], bd[1], tabs[1], dsk)
    tail = jnp.concatenate([dab.astype(bf16), dswk, dswv, jnp.zeros((T, LANE), bf16)], axis=1)
    dproj = lax.dynamic_update_slice(dproj, tail, (0, CB_AB * LANE))
    dh1 = _matmul("proj_bwd_x", dproj, W["w_in"], "nt", f32)
    gw["w_in"] = _matmul("proj_bwd_w", S["h1t"], dproj, "nn", f32)
    dx, dnorm_mix, dsc1, dsh1 = _rowwise_bwd("normmod1_bwd", _normmod_fn, [(x, 0, D)], [K["norm_mix"], sc1, sh1], [(dh1, 0, D)], [f32],
                                             add_to_first=(dx1, 0, D))
    gs = {"norm_mix": dnorm_mix[0], "dn_a_log": dalog[0, :8], "dn_dt_bias": ddtb[0, :8], "dn_norm": ddn_norm.reshape(8, LANE).sum(0),
          "swa_q_norm": dqn_w.reshape(16, 64).sum(0), "swa_k_norm": dkn_w.reshape(2, 64).sum(0), "swa_sinks": dsinks[0, :16],
          "norm_ffn": dnorm_ffn[0], "ffn_conv_b": dffn_b[0]}
    dmod = jnp.concatenate([dsh1, dsc1, dgt1, dsh2, dsc2, dgt2], axis=1)
    return dx, gw, gs, dmod, (pair_sums, hosted_out), early_out


def _rope_tables(pos):
    T = pos.shape[0]
    half = 8
    inv = jnp.power(ROPE_THETA, -jnp.arange(half, dtype=f32) / half)
    ang = pos.astype(f32)[:, None] * inv
    cos, sin = jnp.cos(ang), jnp.sin(ang)
    z8, z48, o48 = jnp.zeros((T, 8), f32), jnp.zeros((T, 48), f32), jnp.ones((T, 48), f32)
    c64 = jnp.concatenate([cos, cos, o48], axis=1)
    s1 = jnp.concatenate([-sin, z8, z48], axis=1)
    s2 = jnp.concatenate([z8, sin, z48], axis=1)
    return tuple(jnp.tile(t, (1, 2)) for t in (c64, s1, s2))


_SMALL = (("norm_mix", D), ("dn_a_log", 8), ("dn_dt_bias", 8), ("dn_norm", 128), ("swa_q_norm", 64), ("swa_k_norm", 64),
          ("swa_sinks", 16), ("norm_ffn", D), ("ffn_conv_b", D_FF), ("b_ada", 6 * D))
_CONV = (("dn_conv", 4 * 3072), ("ffn_conv", 3 * D_FF))
_CONV_SHARD = (("dn_conv", 4 * 768), ("ffn_conv", 3 * 704))


def _pack_small(vals, spec):
    flat = jnp.concatenate([vals[nm].reshape(-1) for nm, _ in spec])
    rows = -(-flat.shape[0] // (8 * LANE)) * 8
    return jnp.pad(flat, (0, rows * LANE - flat.shape[0])).reshape(rows, LANE)


def _unpack_small(buf, spec):
    flat = buf.reshape(-1)
    out, off = {}, 0
    for nm, n in spec:
        out[nm] = flat[off:off + DEPTH * n].reshape(DEPTH, n)
        off += DEPTH * n
    return out


def kernel(x, c, positions, w_ada, b_ada, norm_mix, w_in, dn_conv, dn_a_log, dn_dt_bias, dn_norm, w_dn_out, swa_q_norm, swa_k_norm, swa_sinks, w_swa_out, w_o, norm_ffn, w_up, ffn_conv, ffn_conv_b, w_down, loss_target, m_w_ada, m_b_ada, m_norm_mix, m_w_in, m_dn_conv, m_dn_a_log, m_dn_dt_bias, m_dn_norm, m_w_dn_out, m_swa_q_norm, m_swa_k_norm, m_swa_sinks, m_w_swa_out, m_w_o, m_norm_ffn, m_w_up, m_ffn_conv, m_ffn_conv_b, m_w_down, v_w_ada, v_b_ada, v_norm_mix, v_w_in, v_dn_conv, v_dn_a_log, v_dn_dt_bias, v_dn_norm, v_w_dn_out, v_swa_q_norm, v_swa_k_norm, v_swa_sinks, v_w_swa_out, v_w_o, v_norm_ffn, v_w_up, v_ffn_conv, v_ffn_conv_b, v_w_down):
    weights = dict(w_ada=w_ada, b_ada=b_ada, norm_mix=norm_mix, w_in=w_in, dn_conv=dn_conv, dn_a_log=dn_a_log, dn_dt_bias=dn_dt_bias,
                   dn_norm=dn_norm, w_dn_out=w_dn_out, swa_q_norm=swa_q_norm, swa_k_norm=swa_k_norm, swa_sinks=swa_sinks,
                   w_swa_out=w_swa_out, w_o=w_o, norm_ffn=norm_ffn, w_up=w_up, ffn_conv=ffn_conv, ffn_conv_b=ffn_conv_b, w_down=w_down)
    mom_m = dict(w_ada=m_w_ada, b_ada=m_b_ada, norm_mix=m_norm_mix, w_in=m_w_in, dn_conv=m_dn_conv, dn_a_log=m_dn_a_log,
                 dn_dt_bias=m_dn_dt_bias, dn_norm=m_dn_norm, w_dn_out=m_w_dn_out, swa_q_norm=m_swa_q_norm, swa_k_norm=m_swa_k_norm,
                 swa_sinks=m_swa_sinks, w_swa_out=m_w_swa_out, w_o=m_w_o, norm_ffn=m_norm_ffn, w_up=m_w_up, ffn_conv=m_ffn_conv,
                 ffn_conv_b=m_ffn_conv_b, w_down=m_w_down)
    mom_v = dict(w_ada=v_w_ada, b_ada=v_b_ada, norm_mix=v_norm_mix, w_in=v_w_in, dn_conv=v_dn_conv, dn_a_log=v_dn_a_log,
                 dn_dt_bias=v_dn_dt_bias, dn_norm=v_dn_norm, w_dn_out=v_w_dn_out, swa_q_norm=v_swa_q_norm, swa_k_norm=v_swa_k_norm,
                 swa_sinks=v_swa_sinks, w_swa_out=v_w_swa_out, w_o=v_w_o, norm_ffn=v_norm_ffn, w_up=v_w_up, ffn_conv=v_ffn_conv,
                 ffn_conv_b=v_ffn_conv_b, w_down=v_w_down)
    order = ["w_ada", "b_ada", "norm_mix", "w_in", "dn_conv", "dn_a_log", "dn_dt_bias", "dn_norm", "w_dn_out", "swa_q_norm",
             "swa_k_norm", "swa_sinks", "w_swa_out", "w_o", "norm_ffn", "w_up", "ffn_conv", "ffn_conv_b", "w_down"]
    ax, ay, ac = lax.axis_index("x"), lax.axis_index("y"), lax.axis_index("c")
    chip = 2 * ax + ay
    dev = 4 * ax + 2 * ay + ac
    T = x.shape[1]
    xs = x[0]

    c_all = _allgather_all("gather_c", jnp.pad(c, ((0, 7), (0, 0)))).reshape(8, 8, D)[:, 0]
    c_act = _silu_rows("silu_c", jnp.pad(c_all, ((0, 8), (0, 0))))
    mod_sh = jnp.stack([
        _matmul("mod_fwd", c_act, w_ada[l].astype(bf16), "nn", f32,
                bias=lax.dynamic_slice(b_ada[l], (chip * 1536,), (1536,)).reshape(1, 1536)) for l in range(DEPTH)])
    mod_all = _allgather_all("gather_mod", mod_sh.reshape(DEPTH * 16 * 12, LANE)).reshape(8, DEPTH, 16, 1536)
    mod_me = jnp.concatenate([lax.dynamic_index_in_dim(mod_all[2 * s], dev, axis=1, keepdims=False) for s in range(4)], axis=1)

    tabs_q = _rope_tables(positions[0])
    tabs = (tabs_q, tabs_q)
    head = jnp.arange(LANE) // 64
    bd128 = (head[:, None] == head[None, :]).astype(f32) / 64.0
    bd = (bd128, bd128)
    attn_bias = _attn_bias()

    conv_all = _allgather_all("gather_conv", _pack_small({"dn_conv": dn_conv, "ffn_conv": ffn_conv}, _CONV_SHARD))
    conv_parts = [_unpack_small(conv_all[2 * s], _CONV_SHARD) for s in range(4)]
    dn_conv_full = jnp.concatenate([p["dn_conv"].reshape(DEPTH, 4, 768) for p in conv_parts], axis=2)
    ffn_conv_full = jnp.concatenate([p["ffn_conv"].reshape(DEPTH, 3, 704) for p in conv_parts], axis=2)

    saved, Ws, Ks, mods = [], [], [], []
    h = xs
    shards = [[weights[nm][l].astype(bf16) for nm in _BIG] for l in range(DEPTH)]
    rest = (0, 1, 2, 3, 5)
    rows = lambda g: g.reshape(4 * g.shape[1], g.shape[2])

    def arrive(name, arrived, l, idx):
        got = _gather_forward(name, arrived)
        return [lax.dynamic_update_index_in_dim(g, shards[l][i], chip, 0) for g, i in zip(got, idx)]

    def up_late(l):
        return (_hosted_gather([shards[l][4]]), lambda arrived: {"w_up": arrive("gather_up_pass", arrived, l, (4,))[0]})

    got = [lax.dynamic_update_index_in_dim(g, shards[0][i], chip, 0) for i, g in enumerate(_gather_chips("gather_w", shards[0][:4]))]
    for l in range(DEPTH):
        W = _assemble_mixer(got[:4])
        late = {"proj": up_late(l)}
        if l == 0:
            late["attn"] = (_hosted_gather([shards[0][5]]), lambda arrived: {"w_down": rows(arrive("gather_down_pass", arrived, 0, (5,))[0])})
        else:
            W["w_down"] = rows(got[4])
        W["dn_conv"], W["ffn_conv"] = dn_conv_full[l], ffn_conv_full[l]
        K = _layer_consts({nm: weights[nm][l] for nm in ("norm_mix", "norm_ffn", "dn_a_log", "dn_dt_bias", "dn_norm", "swa_q_norm",
                                                          "swa_k_norm", "swa_sinks", "ffn_conv_b")})
        K["attn_bias"] = attn_bias
        mod = tuple(mod_me[l, k * D:(k + 1) * D].reshape(1, D) for k in range(6))
        nxt = _hosted_gather([shards[l + 1][i] for i in rest]) if l + 1 < DEPTH else None
        h, S, arrived = _layer_fwd(h, mod, W, K, tabs, bd, hosted=nxt, late=late)
        if nxt is not None:
            got = arrive("gather_w_pass", arrived, l + 1, rest)
        saved.append(S), Ws.append(W), Ks.append(K), mods.append(mod)

    loss_blk, dh = _loss("loss", h, loss_target[0])
    loss = lax.psum(loss_blk[0, 0], ("x", "y", "c"))

    grad_sh = [None] * DEPTH
    small = [None] * DEPTH
    dmods = [None] * DEPTH
    slots = None
    for l in reversed(range(DEPTH)):
        dh, gw, gs, dmod, (ps, b2), ffn0 = _layer_bwd(dh, saved[l], mods[l], Ws[l], Ks[l], tabs, bd, carry=None if slots is None else (slots, ac),
                                                      early=(ac, chip) if l == 0 else None)
        if slots is not None:
            grad_sh[l + 1] = dict(zip(_BIG, _rs_end(ps, b2, ac, chip)))
        slots = _grad_slots(gw)
        small[l], dmods[l] = dict(gs, dn_conv=gw["dn_conv"], ffn_conv=gw["ffn_conv"]), dmod[0]
    grad_sh[0] = dict(zip(_BIG, _reduce_scatter_multi(slots[:4], ac, chip) + ffn0))

    spec_g = _SMALL + _CONV
    vals = {nm: jnp.stack([small[l][nm] for l in range(DEPTH)]) for nm, _ in spec_g if nm != "b_ada"}
    vals["b_ada"] = jnp.stack(dmods)
    small_all = _allgather_all("gather_small", _pack_small(vals, spec_g))
    g_small = _unpack_small(_sum_leading("sum_small", small_all), spec_g)
    dmod_all = jnp.stack([_unpack_small(small_all[d], spec_g)["b_ada"] for d in range(8)])
    dmod_sh = lax.dynamic_slice(dmod_all, (0, 0, chip * 1536), (8, DEPTH, 1536))
    dmod_sh = jnp.pad(dmod_sh, ((0, 8), (0, 0), (0, 0))).astype(bf16)
    g_w_ada = jnp.stack([_matmul("mod_bwd_w", c_act, dmod_sh[:, l], "tn", f32) for l in range(DEPTH)])

    grads = {nm: g_small[nm] for nm, _ in _SMALL}
    grads["dn_conv"] = lax.dynamic_slice(g_small["dn_conv"].reshape(DEPTH, 4, 3072), (0, 0, chip * 768), (DEPTH, 4, 768))
    grads["ffn_conv"] = lax.dynamic_slice(g_small["ffn_conv"].reshape(DEPTH, 3, D_FF), (0, 0, chip * 704), (DEPTH, 3, 704))
    grads["w_ada"] = g_w_ada
    for nm in _BIG:
        grads[nm] = jnp.stack([grad_sh[l][nm] for l in range(DEPTH)])

    delta, new_m, new_v = {}, {}, {}
    for nm in ("w_ada", "dn_conv", "ffn_conv") + _BIG:
        delta[nm], new_m[nm], new_v[nm] = _adamw("adamw_" + nm, weights[nm], grads[nm], mom_m[nm], mom_v[nm])
    sm = [_pack_small({nm: t[nm] for nm, _ in _SMALL}, _SMALL) for t in (weights, grads, mom_m, mom_v)]
    for tgt, buf in zip((delta, new_m, new_v), _adamw("adamw_small", *sm)):
        tgt.update(_unpack_small(buf, _SMALL))

    return (loss, dh[None], *[grads[n] for n in order], *[delta[n] for n in order], *[new_m[n] for n in order], *[new_v[n] for n in order])
```

```python
import functools

import jax
import jax.numpy as jnp
import numpy as np
from jax import lax
from jax.experimental import pallas as pl
from jax.experimental.pallas import tpu as pltpu

f32 = jnp.float32
bf16 = jnp.bfloat16
SDS = jax.ShapeDtypeStruct
HI = lax.Precision.HIGHEST
MESH = pl.DeviceIdType.MESH

D = 1024
DEPTH = 4
EPS = 1e-6
DN_C = 64
SWA_B = 128
LANE = 128
ROPE_THETA = 500000.0
D_FF = 2816
IN_TOTAL = 7440
PROJ_W = 7680
CB_Q, CB_K, CB_V = 0, 8, 16
CB_AB, CB_SWK, CB_SWV = 56, 57, 58
WB_Z, WB_GA, WB_GB, WB_SWQ = 3, 4, 5, 6
TR = 256
COMM_TR = 128
VMEM_BIG = 48 * 2 ** 20

ADAM_LR, ADAM_B1, ADAM_B2, ADAM_EPS, ADAM_WD, ADAM_STEP = 0.001, 0.9, 0.999, 1e-08, 0.01, 10


def _pcall(body, **kw):
    return pl.pallas_call(body, **kw)


def _cparams(vmem=None):
    return pltpu.CompilerParams(vmem_limit_bytes=vmem) if vmem else None


def _dot(a, b, ca, cb, precision=HI):
    return lax.dot_general(a, b, (((ca,), (cb,)), ((), ())), precision=precision, preferred_element_type=f32)


def _pick(n, cands):
    for c in cands:
        if n % c == 0:
            return c
    return n


def _tile(n, cap):
    if n <= cap:
        return n
    best = None
    for t in range(LANE, cap + 1, LANE):
        if n % t == 0:
            best = t
    assert best is not None, (n, cap)
    return best


def _matmul(name, a, b, mode, out_dtype, bias=None, out_slots=None, hosted=None, resid=None, b_slots=False):
    h_args, h_specs, h_shapes, h_sems = _hosted_parts(hosted)
    nh = len(h_args)
    a_parts = list(a) if isinstance(a, (tuple, list)) else [a]
    b_parts = list(b) if isinstance(b, (tuple, list)) else [b]
    na, nbp = len(a_parts), len(b_parts)
    assert (na == 1 or mode == "nt") and (nbp == 1 or mode == "nn")
    if b_slots:
        S_, rows_, cols_ = b.shape
        M = a_parts[0].shape[0]
        K, N = (rows_, S_ * cols_) if mode == "nn" else (S_ * cols_, rows_)
    elif mode == "nn":
        M, K = a_parts[0].shape
        N = b_parts[0].shape[1] * nbp
    elif mode == "nt":
        M, K = a_parts[0].shape[0], a_parts[0].shape[1] * na
        N = b_parts[0].shape[0]
    else:
        (K, M), N = a_parts[0].shape, b_parts[0].shape[1]
    tm = _tile(M, 1536 if mode == "tn" else 1024)
    tn = N // out_slots if out_slots else _tile(N, 1536)
    tk = _tile(K, 512 if mode == "tn" else (1024 if K <= 1024 else 1536))
    nk, nj = K // tk, N // tn
    ka, jb = nk // na, nj // nbp
    assert nk % na == 0 and nj % nbp == 0
    ca, cb = {"nn": (1, 0), "nt": (1, 1), "tn": (0, 0)}[mode]
    grid = (M // tm, nj, nk)
    n_bias, n_res = (1 if bias is not None else 0), (2 if resid is not None else 0)

    def body(*refs):
        a_refs, b_refs = refs[:na], refs[na:na + nbp]
        p = na + nbp
        bias_ref = refs[p] if n_bias else None
        res_refs = refs[p + n_bias:p + n_bias + n_res]
        p += n_bias + n_res
        h_ins, o_ref = refs[p:p + nh], refs[p + nh]
        xo_ref = refs[p + nh + 1] if n_res else None
        p += nh + 1 + (1 if n_res else 0)
        h_outs, rest = refs[p:p + nh], refs[p + nh:]
        acc, sems = (rest[0], rest[1:]) if nk > 1 else (None, rest)
        j, k = pl.program_id(1), pl.program_id(2)
        step = (pl.program_id(0) * grid[1] + j) * grid[2] + k
        _hosted_edge(hosted, "start", h_ins, h_outs, sems, step == 0)

        def finish(r):
            if bias is not None:
                r = r + bias_ref[...]
            o_ref[...] = r.astype(o_ref.dtype)
            if n_res:
                xo_ref[...] = res_refs[0][...] + res_refs[1][...] * r

        a_tile = a_refs[0][...] if na == 1 else jnp.where(k < ka, a_refs[0][...], a_refs[1][...])
        b_tile = b_refs[0][...] if nbp == 1 else jnp.where(j < jb, b_refs[0][...], b_refs[1][...])
        part = _dot(a_tile.astype(bf16), b_tile.astype(bf16), ca, cb, precision=None)
        if nk == 1:
            finish(part)
        else:
            @pl.when(k == 0)
            def _():
                acc[...] = part

            @pl.when(k > 0)
            def _():
                acc[...] += part

            @pl.when(k == nk - 1)
            def _():
                finish(acc[...])

        _hosted_edge(hosted, "finish", h_ins, h_outs, sems, step == grid[0] * grid[1] * grid[2] - 1)

    if mode == "tn":
        a_specs = [pl.BlockSpec((tk, tm), lambda i, j, k: (k, i))]
    else:
        a_specs = [pl.BlockSpec((tm, tk), lambda i, j, k, q=q: (i, jnp.clip(k - q * ka, 0, ka - 1))) for q in range(na)]
    if b_slots:
        assert (tn if mode == "nn" else tk) == b.shape[2]
        b_specs = [pl.BlockSpec((None, tk, tn), lambda i, j, k: (j, k, 0)) if mode == "nn" else pl.BlockSpec((None, tn, tk), lambda i, j, k: (k, j, 0))]
    elif mode == "nt":
        b_specs = [pl.BlockSpec((tn, tk), lambda i, j, k: (j, k))]
    else:
        b_specs = [pl.BlockSpec((tk, tn), lambda i, j, k, q=q: (k, jnp.clip(j - q * jb, 0, jb - 1))) for q in range(nbp)]
    in_specs, args = a_specs + b_specs, a_parts + b_parts
    if bias is not None:
        in_specs.append(pl.BlockSpec((1, tn), lambda i, j, k: (0, j)))
        args.append(bias)
    if out_slots:
        out_specs, out_shape = [pl.BlockSpec((None, tm, tn), lambda i, j, k: (j, i, 0))], [SDS((out_slots, M, tn), out_dtype)]
    else:
        out_specs, out_shape = [pl.BlockSpec((tm, tn), lambda i, j, k: (i, j))], [SDS((M, N), out_dtype)]
    if resid is not None:
        in_specs += [pl.BlockSpec((tm, tn), lambda i, j, k: (i, j)), pl.BlockSpec((1, tn), lambda i, j, k: (0, j))]
        args += list(resid)
        out_specs.append(pl.BlockSpec((tm, tn), lambda i, j, k: (i, j)))
        out_shape.append(SDS((M, N), f32))
    outs = _pcall(
        body, grid=grid, in_specs=in_specs + h_specs, out_specs=out_specs + h_specs, out_shape=out_shape + h_shapes,
        scratch_shapes=([pltpu.VMEM((tm, tn), f32)] if nk > 1 else []) + h_sems, compiler_params=_cparams(VMEM_BIG), name=name)(*args, *h_args)
    n_main = len(out_shape)
    res = list(outs[:n_main]) + ([list(outs[n_main:])] if hosted is not None else [])
    return res[0] if len(res) == 1 else tuple(res)


def _row_specs(rows, tr):
    return [pl.BlockSpec((tr, w), lambda i, j, off=off: (i, off + j)) for (_, off, w) in rows]


def _rowwise_fwd(name, fn, rows, vecs, out_widths, out_dtypes, nc=1, tr=TR, also_transposed=False):
    T = rows[0][0].shape[0]
    n_in = len(rows) + len(vecs)
    n_out = len(out_widths)

    def body(*refs):
        vals = [r[...].astype(f32) for r in refs[:n_in]]
        res = fn(*vals)
        for o_ref, r in zip(refs[n_in:n_in + n_out], res):
            o_ref[...] = r.astype(o_ref.dtype)
        if also_transposed:
            refs[n_in + n_out][...] = res[0].T.astype(refs[n_in + n_out].dtype)

    in_specs = _row_specs(rows, tr) + [pl.BlockSpec(v.shape, lambda i, j: (0, 0)) for v in vecs]
    out_specs = [pl.BlockSpec((tr, w), lambda i, j: (i, j)) for w in out_widths]
    out_shape = [SDS((T, w * nc), dt) for w, dt in zip(out_widths, out_dtypes)]
    if also_transposed:
        out_specs.append(pl.BlockSpec((out_widths[0], tr), lambda i, j: (j, i)))
        out_shape.append(SDS((out_widths[0] * nc, T), out_dtypes[0]))
    return _pcall(body, grid=(T // tr, nc), in_specs=in_specs, out_specs=out_specs, out_shape=out_shape, name=name)(
        *[r[0] for r in rows], *vecs)


def _into_buffer(dest_buf, n_inputs, out_index):
    if dest_buf is None:
        return [], [], {}
    return [dest_buf], [pl.BlockSpec(memory_space=pl.ANY)], {n_inputs: out_index}


def _rowwise_bwd(name, fn, rows, vecs, cts, drow_dtypes, nc=1, tr=TR, add_to_first=None, dest=None):
    T = rows[0][0].shape[0]
    n_r, n_v, n_c = len(rows), len(vecs), len(cts)
    n_add = 0 if add_to_first is None else 1
    keep = [k for k, dt in enumerate(drow_dtypes) if dt is not None]
    members = [] if dest is None else list(dest[1])
    plain = [pos for pos in range(len(keep)) if pos not in members]
    n_dest = 1 if dest is not None else 0
    n_in = n_r + n_v + n_c + n_add

    def body(*refs):
        vals = [r[...].astype(f32) for r in refs[:n_in]]
        outs = refs[n_in + (1 if dest is not None and dest[0] is not None else 0):]
        i, j = pl.program_id(0), pl.program_id(1)
        _, vjp = jax.vjp(fn, *vals[:n_r + n_v])
        grads = vjp(tuple(vals[n_r + n_v:n_r + n_v + n_c]))
        got = []
        for pos, k in enumerate(keep):
            g = grads[k]
            if n_add and pos == 0:
                g = g + vals[n_in - 1]
            got.append(g)
        if dest is not None:
            outs[0][...] = jnp.concatenate([got[pos] for pos in members], axis=1).astype(outs[0].dtype)
        for q, pos in enumerate(plain):
            outs[n_dest + q][...] = got[pos].astype(outs[n_dest + q].dtype)
        vec_outs = outs[n_dest + len(plain):]

        @pl.when((i == 0) & (j == 0))
        def _():
            for q in range(n_v):
                vec_outs[q][...] = jnp.zeros_like(vec_outs[q])

        for q in range(n_v):
            vec_outs[q][...] += grads[n_r + q]

    extra = [] if add_to_first is None else [add_to_first]
    in_specs = (_row_specs(rows, tr) + [pl.BlockSpec(v.shape, lambda i, j: (0, 0)) for v in vecs]
                + _row_specs(cts, tr) + _row_specs(extra, tr))
    args = [r[0] for r in rows] + list(vecs) + [c[0] for c in cts] + [e[0] for e in extra]
    out_specs, out_shape, aliases = [], [], {}
    if dest is not None:
        width = sum(rows[keep[pos]][2] for pos in members)
        col = dest[2]
        b_args, b_specs, aliases = _into_buffer(dest[0], n_in, 0)
        args, in_specs = args + b_args, in_specs + b_specs
        out_specs.append(pl.BlockSpec((tr, width), lambda i, j: (i, col + j)))
        out_shape.append(SDS((T, PROJ_W), bf16))
    out_specs += [pl.BlockSpec((tr, rows[keep[pos]][2]), lambda i, j: (i, j)) for pos in plain]
    out_shape += [SDS((T, rows[keep[pos]][2] * nc), drow_dtypes[keep[pos]]) for pos in plain]
    out_specs += [pl.BlockSpec(v.shape, lambda i, j: (0, 0)) for v in vecs]
    out_shape += [SDS(v.shape, f32) for v in vecs]
    return _pcall(body, grid=(T // tr, nc), in_specs=in_specs, out_specs=out_specs, out_shape=out_shape,
                  input_output_aliases=aliases, name=name)(*args)


def _normmod_fn(x, w, sc, sh):
    y = x * lax.rsqrt(jnp.mean(x * x, axis=-1, keepdims=True) + EPS)
    return ((y * w) * (1.0 + sc) + sh,)


def _resid_fn(x, t, gt):
    return (x + gt * t,)


def _merge_fn(ga, gb, ya, yb):
    return (jax.nn.sigmoid(ga) * ya + jax.nn.sigmoid(gb) * yb,)


def _headmean_raw(x2):
    rows = x2.shape[0]
    return jnp.concatenate([jnp.broadcast_to(jnp.mean(x2[:, LANE * h:LANE * (h + 1)], axis=1, keepdims=True), (rows, LANE))
                            for h in range(x2.shape[1] // LANE)], axis=1)


@jax.custom_vjp
def _headmean(x2):
    return _headmean_raw(x2)


_headmean.defvjp(lambda x2: (_headmean_raw(x2), None), lambda _, dy: (_headmean_raw(dy),))


def _dngate_fn(o, z, w):
    y = o * lax.rsqrt(_headmean(o * o) + EPS)
    return ((y * w) * (z * jax.nn.sigmoid(z)),)


def _conv_taps(x, w_ref, taps, buf):
    w = lambda s: w_ref[taps - 1 - s:taps - s, :]
    row = lax.broadcasted_iota(jnp.int32, (8, x.shape[1]), 0)
    x8 = x[0:8]
    acc, acc8 = x * w(0), x8 * w(0)
    for s in range(1, taps):
        acc = acc + pltpu.roll(x, s, 0) * w(s)
        acc8 = acc8 + jnp.where(row >= s, pltpu.roll(x8, s, 0), 0.0) * w(s)
    buf[...] = acc
    buf[0:8, :] = acc8
    return buf[...]


def _conv_taps_bwd(x, dy, w_ref, dw_ref, taps, buf):
    T = x.shape[0]
    w = lambda s: w_ref[taps - 1 - s:taps - s, :]
    row = lax.broadcasted_iota(jnp.int32, (8, x.shape[1]), 0)
    dy_first, dy_last = dy[0:8], dy[T - 8:T]
    dx, dx_last = dy * w(0), dy_last * w(0)
    dw_ref[taps - 1:taps, :] = jnp.sum(dy * x, axis=0, keepdims=True)
    for s in range(1, taps):
        dx = dx + pltpu.roll(dy, T - s, 0) * w(s)
        dx_last = dx_last + jnp.where(row < 8 - s, pltpu.roll(dy_last, 8 - s, 0), 0.0) * w(s)
        xr = pltpu.roll(x, s, 0)
        wrapped = jnp.sum(jnp.where(row < s, dy_first * xr[0:8], 0.0), axis=0, keepdims=True)
        dw_ref[taps - 1 - s:taps - s, :] = jnp.sum(dy * xr, axis=0, keepdims=True) - wrapped
    buf[...] = dx
    buf[T - 8:T, :] = dx_last
    return buf[...]


def _dn_act(y, normalize):
    s = y * jax.nn.sigmoid(y)
    if normalize:
        s = s * lax.rsqrt(jnp.sum(s * s, axis=-1, keepdims=True) + EPS)
    return s


def _dnconv_fwd(name, proj, cb, w, normalize):
    T = proj.shape[0]

    def body(x_ref, w_ref, o_ref, buf):
        o_ref[...] = _dn_act(_conv_taps(x_ref[...], w_ref, 4, buf), normalize)

    return _pcall(
        body, grid=(8,), in_specs=[pl.BlockSpec((T, LANE), lambda j: (0, cb + j)), pl.BlockSpec((4, LANE), lambda j: (0, cb + j))],
        out_specs=pl.BlockSpec((T, LANE), lambda j: (0, j)), out_shape=SDS((T, 1024), f32), scratch_shapes=[pltpu.VMEM((T, LANE), f32)],
        compiler_params=_cparams(VMEM_BIG), name=name)(proj, w)


def _dnconv_bwd(name, proj, cb, w, dout, normalize, dest_buf):
    T = proj.shape[0]
    b_args, b_specs, aliases = _into_buffer(dest_buf, 3, 0)

    def body(x_ref, w_ref, do_ref, *rest):
        dx_ref, dw_ref, buf, buf2 = rest[len(b_args):]
        x = x_ref[...]
        y = _conv_taps(x, w_ref, 4, buf)
        _, vjp = jax.vjp(functools.partial(_dn_act, normalize=normalize), y)
        (dy,) = vjp(do_ref[...])
        dx_ref[...] = _conv_taps_bwd(x, dy, w_ref, dw_ref, 4, buf2).astype(dx_ref.dtype)

    return _pcall(
        body, grid=(8,),
        in_specs=[pl.BlockSpec((T, LANE), lambda j: (0, cb + j)), pl.BlockSpec((4, LANE), lambda j: (0, cb + j)),
                  pl.BlockSpec((T, LANE), lambda j: (0, j))] + b_specs,
        out_specs=[pl.BlockSpec((T, LANE), lambda j: (0, cb + j)), pl.BlockSpec((4, LANE), lambda j: (0, j))],
        out_shape=[SDS((T, PROJ_W), bf16), SDS((4, 1024), f32)], scratch_shapes=[pltpu.VMEM((T, LANE), f32)] * 2,
        input_output_aliases=aliases, compiler_params=_cparams(VMEM_BIG), name=name)(proj, w, dout, *b_args)


def _ffn_point(a, lin):
    return a * jax.nn.sigmoid(a) * lin


def _ffnact_fwd(name, up, w, b):
    T = up.shape[0]
    nblk = D_FF // LANE

    def body(a_ref, l_ref, w_ref, b_ref, o_ref, buf):
        a = _conv_taps(a_ref[...], w_ref, 3, buf) + b_ref[...]
        o_ref[...] = _ffn_point(a, l_ref[...]).astype(o_ref.dtype)

    return _pcall(
        body, grid=(nblk,),
        in_specs=[pl.BlockSpec((T, LANE), lambda j: (0, j)), pl.BlockSpec((T, LANE), lambda j: (0, nblk + j)),
                  pl.BlockSpec((3, LANE), lambda j: (0, j)), pl.BlockSpec((1, LANE), lambda j: (0, j))],
        out_specs=pl.BlockSpec((T, LANE), lambda j: (0, j)), out_shape=SDS((T, D_FF), bf16), scratch_shapes=[pltpu.VMEM((T, LANE), f32)],
        compiler_params=_cparams(VMEM_BIG), name=name)(up, up, w, b)


def _ffnact_bwd(name, up, w, b, dmid):
    T = up.shape[0]
    nblk = D_FF // LANE

    def body(a_ref, l_ref, w_ref, b_ref, dm_ref, da_ref, dl_ref, dw_ref, db_ref, buf, buf2):
        x = a_ref[...]
        a = _conv_taps(x, w_ref, 3, buf) + b_ref[...]
        _, vjp = jax.vjp(_ffn_point, a, l_ref[...])
        da, dl = vjp(dm_ref[...].astype(f32))
        dl_ref[...] = dl.astype(dl_ref.dtype)
        db_ref[...] = jnp.sum(da, axis=0, keepdims=True)
        da_ref[...] = _conv_taps_bwd(x, da, w_ref, dw_ref, 3, buf2).astype(da_ref.dtype)

    col = lambda r: pl.BlockSpec((r, LANE), lambda j: (0, j))
    return _pcall(
        body, grid=(nblk,),
        in_specs=[col(T), pl.BlockSpec((T, LANE), lambda j: (0, nblk + j)), col(3), col(1), col(T)],
        out_specs=[col(T), col(T), col(3), col(1)],
        out_shape=[SDS((T, D_FF), bf16), SDS((T, D_FF), bf16), SDS((3, D_FF), f32), SDS((1, D_FF), f32)],
        scratch_shapes=[pltpu.VMEM((T, LANE), f32)] * 2, compiler_params=_cparams(VMEM_BIG), name=name)(up, up, w, b, dmid)


def _bmm(a, b, ca, cb, precision=HI):
    return lax.dot_general(a, b, (((ca,), (cb,)), ((0,), (0,))), precision=precision, preferred_element_type=f32)


def _make_bdot(ca, cb):
    def raw(x, y, cx, cy):
        return _bmm(x.astype(bf16), y.astype(bf16), cx, cy, precision=None)

    @jax.custom_vjp
    def f(a, b):
        return raw(a, b, ca, cb)

    def fwd(a, b):
        return raw(a, b, ca, cb), (a, b)

    def bwd(res, dy):
        a, b = res
        if (ca, cb) == (2, 1):
            return raw(dy, b, 2, 2), raw(a, dy, 1, 1)
        if (ca, cb) == (2, 2):
            return raw(dy, b, 2, 1), raw(dy, a, 1, 1)
        return raw(b, dy, 2, 2), raw(a, dy, 2, 1)

    f.defvjp(fwd, bwd)
    return f


_bdot_nn, _bdot_nt, _bdot_tn = _make_bdot(2, 1), _make_bdot(2, 2), _make_bdot(1, 1)


def _pieces(a, n):
    out, r = [], a
    for _ in range(n):
        p = r.astype(bf16)
        out.append(p)
        r = r - p.astype(f32)
    return out


def _bmm_split(x, y, cx, cy, nx=2, ny=2, order=1):
    xs, ys = _pieces(x, nx), _pieces(y, ny)
    acc = None
    for i in reversed(range(nx)):
        for j in reversed(range(ny)):
            if i + j <= order:
                t = _bmm(xs[i], ys[j], cx, cy, precision=None)
                acc = t if acc is None else acc + t
    return acc


@jax.custom_vjp
def _solve_apply(X, r):
    return _bmm_split(X, r, 2, 1)


def _solve_apply_fwd(X, r):
    return _bmm_split(X, r, 2, 1), (X, r)


def _solve_apply_bwd(res, dy):
    X, r = res
    return _bmm_split(dy, r, 2, 2), _bmm_split(X, dy, 1, 1)


_solve_apply.defvjp(_solve_apply_fwd, _solve_apply_bwd)


def _lower_ones(H, C):
    ri = lax.broadcasted_iota(jnp.int32, (H, C, C), 1)
    ci = lax.broadcasted_iota(jnp.int32, (H, C, C), 2)
    return (ri >= ci).astype(f32)


def _cumsum_rows_raw(G):
    return _bmm_split(_lower_ones(G.shape[0], G.shape[1]), G, 2, 1, nx=1, ny=3, order=2)


@jax.custom_vjp
def _cumsum_rows(G):
    return _cumsum_rows_raw(G)


def _cumsum_rows_fwd(G):
    return _cumsum_rows_raw(G), None


def _cumsum_rows_bwd(_, dy):
    return (_bmm_split(_lower_ones(dy.shape[0], dy.shape[1]), dy, 1, 1, nx=1, ny=3, order=2),)


_cumsum_rows.defvjp(_cumsum_rows_fwd, _cumsum_rows_bwd)


def _tri_inverse_raw(L):
    H, C, _ = L.shape
    ri = lax.broadcasted_iota(jnp.int32, (C, C), 0)
    ci = lax.broadcasted_iota(jnp.int32, (C, C), 1)
    eye = jnp.broadcast_to((ri == ci).astype(f32)[None], (H, C, C))
    Dg = jnp.where(((ri >> 3) == (ci >> 3))[None], L, 0.0)
    D2 = _bmm_split(Dg, Dg, 2, 1)
    X = _bmm_split(_bmm_split(eye - Dg, eye + D2, 2, 1), eye + _bmm_split(D2, D2, 2, 1), 2, 1)
    for lg in range(3, C.bit_length() - 1):
        same = (ri >> (lg + 1)) == (ci >> (lg + 1))
        lower_left = same & (((ri >> lg) & 1) == 1) & (((ci >> lg) & 1) == 0)
        X = X - _bmm_split(_bmm_split(X, jnp.where(lower_left[None], L, 0.0), 2, 1), X, 2, 1)
    return X


@jax.custom_vjp
def _tri_inverse(L):
    return _tri_inverse_raw(L)


def _tri_inverse_fwd(L):
    X = _tri_inverse_raw(L)
    return X, X


def _tri_inverse_bwd(X, dX):
    return (-_bmm_split(_bmm_split(X, dX, 1, 1), X, 2, 2),)


_tri_inverse.defvjp(_tri_inverse_fwd, _tri_inverse_bwd)


@jax.custom_vjp
def _tri_inverse_known(L, X):
    return X


def _tri_inverse_known_fwd(L, X):
    return X, X


def _tri_inverse_known_bwd(X, dX):
    return _tri_inverse_bwd(X, dX)[0], jnp.zeros_like(X)


_tri_inverse_known.defvjp(_tri_inverse_known_fwd, _tri_inverse_known_bwd)


DN_NC = 4


def _gdn_chunk(q, k, v, ab, alog, dtb, S, X_known=None, keep_X=False):
    H, C, _ = q.shape
    NC, NH = ab.shape[0], H // ab.shape[0]
    lane = lax.broadcasted_iota(jnp.int32, (H, C, LANE), 2)
    head = lax.broadcasted_iota(jnp.int32, (H, C, LANE), 0) & (NH - 1)
    abb = jnp.concatenate([jnp.broadcast_to(ab[c][None], (NH, C, LANE)) for c in range(NC)], axis=0)
    a = jnp.sum(jnp.where(lane == head, abb, 0.0), axis=2, keepdims=True)
    b = jnp.sum(jnp.where(lane == head + 8, abb, 0.0), axis=2, keepdims=True)
    pick = lax.broadcasted_iota(jnp.int32, (H, 1, LANE), 2) == (lax.broadcasted_iota(jnp.int32, (H, 1, LANE), 0) & (NH - 1))
    al = jnp.sum(jnp.where(pick, alog[None], 0.0), axis=2, keepdims=True)
    db = jnp.sum(jnp.where(pick, dtb[None], 0.0), axis=2, keepdims=True)
    g = -jnp.exp(al) * jax.nn.softplus(a + db)
    beta = jax.nn.sigmoid(b)
    ri = lax.broadcasted_iota(jnp.int32, (C, C), 0)
    ci = lax.broadcasted_iota(jnp.int32, (C, C), 1)
    G = jnp.broadcast_to(g, (H, C, LANE))
    gc = _cumsum_rows(G)
    gi = _cumsum_rows(jnp.broadcast_to(g, (H, C, C)))
    decay = jnp.exp(jnp.where((ri >= ci)[None], gi - jnp.swapaxes(gi, 1, 2), -jnp.inf))
    qs = q * (LANE ** -0.5)
    kb = k * beta
    L = jnp.where((ri > ci)[None], _bdot_nt(kb, k) * decay, 0.0)
    X = _tri_inverse(L) if X_known is None else _tri_inverse_known(L, X_known)
    egc = jnp.exp(gc)
    u = _solve_apply(X, v * beta)
    w = _solve_apply(X, kb * egc)
    qk = _bdot_nt(qs, k) * decay
    g_last = jnp.sum(G, axis=1, keepdims=True)
    k_dec = k * jnp.exp(g_last - gc)
    q_dec = qs * egc
    e_last = jnp.exp(g_last)
    outs = []
    for c in range(NC):
        sl = slice(c * NH, (c + 1) * NH)
        v_new = u[sl] - _bdot_nn(w[sl], S)
        outs.append(_bdot_nn(q_dec[sl], S) + _bdot_nn(qk[sl], v_new))
        S = S * e_last[sl] + _bdot_tn(k_dec[sl], v_new)
    o = jnp.concatenate(outs, axis=0)
    return (o, S, X) if keep_X else (o, S)


def _heads(ref):
    return jnp.stack([ref[DN_C * c:DN_C * (c + 1), LANE * h:LANE * (h + 1)] for c in range(DN_NC) for h in range(8)], axis=0)


def _put_heads(ref, val):
    for c in range(DN_NC):
        for h in range(8):
            ref[DN_C * c:DN_C * (c + 1), LANE * h:LANE * (h + 1)] = val[8 * c + h]


def _chunk_rows(ref):
    return jnp.stack([ref[DN_C * c:DN_C * (c + 1), :] for c in range(DN_NC)], axis=0)


def _hosted_parts(hosted):
    if hosted is None:
        return [], [], [], []
    n = len(hosted["arrays"])
    return list(hosted["arrays"]), [HBM_SPEC] * n, list(hosted["out_shape"]), _dma_sems(hosted["n_sems"])


def _gdn_fwd(name, q, k, v, proj, alog, dtb, hosted=None):
    T = q.shape[0]
    R = DN_C * DN_NC
    N = T // R
    h_args, h_specs, h_shapes, h_sems = _hosted_parts(hosted)
    nh = len(h_args)

    def body(q_ref, k_ref, v_ref, ab_ref, al_ref, dt_ref, *rest):
        h_ins, (o_ref, sall_ref, xinv_ref), h_outs, s_scr, sems = rest[:nh], rest[nh:nh + 3], rest[nh + 3:2 * nh + 3], rest[2 * nh + 3], rest[2 * nh + 4:]
        step = pl.program_id(0)

        @pl.when(step == 0)
        def _():
            s_scr[...] = jnp.zeros_like(s_scr)
            if hosted is not None:
                hosted["start"](h_ins, h_outs, *sems)

        S = s_scr[...]
        sall_ref[...] = S
        o, S_new, X = _gdn_chunk(_heads(q_ref), _heads(k_ref), _heads(v_ref), _chunk_rows(ab_ref), al_ref[...], dt_ref[...], S, keep_X=True)
        _put_heads(o_ref, o)
        s_scr[...] = S_new
        xinv_ref[...] = X

        if hosted is not None:
            @pl.when(step == N - 1)
            def _():
                hosted["finish"](h_ins, h_outs, *sems)

    blk = pl.BlockSpec((R, 8 * LANE), lambda n: (n, 0))
    vec = pl.BlockSpec((1, LANE), lambda n: (0, 0))
    state = pl.BlockSpec((None, 8, LANE, LANE), lambda n: (n, 0, 0, 0))
    xinv = pl.BlockSpec((None, 8 * DN_NC, DN_C, DN_C), lambda n: (n, 0, 0, 0))
    outs = _pcall(
        body, grid=(N,), in_specs=[blk, blk, blk, pl.BlockSpec((R, LANE), lambda n: (n, CB_AB)), vec, vec] + h_specs,
        out_specs=[blk, state, xinv] + h_specs,
        out_shape=[SDS((T, 1024), f32), SDS((N, 8, LANE, LANE), f32), SDS((N, 8 * DN_NC, DN_C, DN_C), f32)] + h_shapes,
        scratch_shapes=[pltpu.VMEM((8, LANE, LANE), f32)] + h_sems, compiler_params=_cparams(VMEM_BIG), name=name)(q, k, v, proj, alog, dtb, *h_args)
    return outs[0], outs[1], outs[2], list(outs[3:])


def _gdn_bwd(name, q, k, v, proj, alog, dtb, sall, xinv, do, hosted=None):
    T = q.shape[0]
    R = DN_C * DN_NC
    N = T // R
    h_args, h_specs, h_shapes, h_sems = _hosted_parts(hosted)
    nh = len(h_args)

    def body(q_ref, k_ref, v_ref, ab_ref, al_ref, dt_ref, s_ref, x_ref, do_ref, *rest):
        h_ins, h_outs, ds_scr, sems = rest[:nh], rest[nh + 6:2 * nh + 6], rest[2 * nh + 6], rest[2 * nh + 7:]
        dq_ref, dk_ref, dv_ref, dab_ref, dal_ref, ddt_ref = rest[nh:nh + 6]
        step = pl.program_id(0)

        @pl.when(step == 0)
        def _():
            ds_scr[...] = jnp.zeros_like(ds_scr)
            dal_ref[...] = jnp.zeros_like(dal_ref)
            ddt_ref[...] = jnp.zeros_like(ddt_ref)
            if hosted is not None:
                hosted["start"](h_ins, h_outs, *sems)

        _, vjp = jax.vjp(functools.partial(_gdn_chunk, X_known=x_ref[...]), _heads(q_ref), _heads(k_ref), _heads(v_ref), _chunk_rows(ab_ref),
                         al_ref[...], dt_ref[...], s_ref[...])
        dq, dk, dv, dab, dal, ddt, dS = vjp((_heads(do_ref), ds_scr[...]))
        _put_heads(dq_ref, dq)
        _put_heads(dk_ref, dk)
        _put_heads(dv_ref, dv)
        ds_scr[...] = dS
        for c in range(DN_NC):
            dab_ref[DN_C * c:DN_C * (c + 1), :] = dab[c]
        dal_ref[...] += dal
        ddt_ref[...] += ddt

        if hosted is not None:
            @pl.when(step == N - 1)
            def _():
                hosted["finish"](h_ins, h_outs, *sems)

    blk = pl.BlockSpec((R, 8 * LANE), lambda n: (N - 1 - n, 0))
    vec = pl.BlockSpec((1, LANE), lambda n: (0, 0))
    state = pl.BlockSpec((None, 8, LANE, LANE), lambda n: (N - 1 - n, 0, 0, 0))
    outs = _pcall(
        body, grid=(N,),
        in_specs=[blk, blk, blk, pl.BlockSpec((R, LANE), lambda n: (N - 1 - n, CB_AB)), vec, vec, state,
                  pl.BlockSpec((None, 8 * DN_NC, DN_C, DN_C), lambda n: (N - 1 - n, 0, 0, 0)), blk] + h_specs,
        out_specs=[blk, blk, blk, pl.BlockSpec((R, LANE), lambda n: (N - 1 - n, 0)), vec, vec] + h_specs,
        out_shape=[SDS((T, 1024), f32)] * 3 + [SDS((T, LANE), f32), SDS((1, LANE), f32), SDS((1, LANE), f32)] + h_shapes,
        scratch_shapes=[pltpu.VMEM((8, LANE, LANE), f32)] + h_sems, compiler_params=_cparams(VMEM_BIG),
        name=name)(q, k, v, proj, alog, dtb, sall, xinv, do, *h_args)
    return tuple(outs[:6]), list(outs[6:])


def _segmean_raw(x2, bd):
    return jnp.concatenate([_dot(x2[:, LANE * j:LANE * (j + 1)], bd, 1, 0) for j in range(x2.shape[1] // LANE)], axis=1)


@jax.custom_vjp
def _segmean(x2, bd):
    return _segmean_raw(x2, bd)


def _segmean_fwd(x2, bd):
    return _segmean_raw(x2, bd), bd


def _segmean_bwd(bd, dy):
    return _segmean_raw(dy, bd), jnp.zeros_like(bd)


_segmean.defvjp(_segmean_fwd, _segmean_bwd)


def _qknorm_fn(x, w, bd):
    return x * lax.rsqrt(_segmean(x * x, bd) + EPS) * w


def _rope_apply(xn, c, s1, s2):
    W = xn.shape[1]
    return xn * c + pltpu.roll(xn, W - 8, 1) * s1 + pltpu.roll(xn, 8, 1) * s2


def _rope_apply_t(d, c, s1, s2):
    W = d.shape[1]
    return d * c + pltpu.roll(d * s1, 8, 1) + pltpu.roll(d * s2, W - 8, 1)


def _rope_tiles(refs, width):
    return [jnp.tile(r[...], (1, width // LANE)) for r in refs]


def _qkprep_fwd(name, proj, wb, width, w, bd, tabs):
    T = proj.shape[0]
    tr = _pick(T, (256, 128))

    def body(x_ref, w_ref, bd_ref, c_ref, s1_ref, s2_ref, o_ref):
        xn = _qknorm_fn(x_ref[...], w_ref[...], bd_ref[...])
        o_ref[...] = _rope_apply(xn, *_rope_tiles((c_ref, s1_ref, s2_ref), width))

    row0 = pl.BlockSpec((tr, width), lambda i: (i, 0))
    tab = pl.BlockSpec((tr, LANE), lambda i: (i, 0))
    full = lambda a: pl.BlockSpec(a.shape, lambda i: (0, 0))
    return _pcall(
        body, grid=(T // tr,), in_specs=[pl.BlockSpec((tr, width), lambda i: (i, wb)), full(w), full(bd), tab, tab, tab],
        out_specs=row0, out_shape=SDS((T, width), f32), name=name)(proj, w, bd, *tabs)


def _qkprep_bwd(name, proj, wb, width, w, bd, tabs, dout, dest_buf="none"):
    T = proj.shape[0]
    tr = _pick(T, (256, 128))
    into = not isinstance(dest_buf, str)
    b_args, b_specs, aliases = _into_buffer(dest_buf, 7, 0) if into else ([], [], {})

    def body(x_ref, w_ref, bd_ref, c_ref, s1_ref, s2_ref, do_ref, *rest):
        dx_ref, dw_ref = rest[len(b_args):]
        i = pl.program_id(0)
        dxn = _rope_apply_t(do_ref[...], *_rope_tiles((c_ref, s1_ref, s2_ref), width))
        bd = bd_ref[...]
        _, vjp = jax.vjp(lambda x, w_: _qknorm_fn(x, w_, bd), x_ref[...], w_ref[...])
        dx, dw = vjp(dxn)
        dx_ref[...] = dx.astype(dx_ref.dtype)

        @pl.when(i == 0)
        def _():
            dw_ref[...] = jnp.zeros_like(dw_ref)

        dw_ref[...] += dw

    row0 = pl.BlockSpec((tr, width), lambda i: (i, 0))
    tab = pl.BlockSpec((tr, LANE), lambda i: (i, 0))
    full = lambda a: pl.BlockSpec(a.shape, lambda i: (0, 0))
    dx_spec = pl.BlockSpec((tr, width), lambda i: (i, wb)) if into else row0
    dx_shape = SDS((T, PROJ_W), bf16) if into else SDS((T, width), bf16)
    return _pcall(
        body, grid=(T // tr,), in_specs=[pl.BlockSpec((tr, width), lambda i: (i, wb)), full(w), full(bd), tab, tab, tab, row0] + b_specs,
        out_specs=[dx_spec, full(w)], out_shape=[dx_shape, SDS(w.shape, f32)], input_output_aliases=aliases, name=name)(
            proj, w, bd, *tabs, dout, *b_args)


def _make_dot16(ca, cb):
    def raw(x, y, cx, cy):
        return _dot(x.astype(bf16), y.astype(bf16), cx, cy, precision=None)

    @jax.custom_vjp
    def f(a, b):
        return raw(a, b, ca, cb)

    def fwd(a, b):
        return raw(a, b, ca, cb), (a, b)

    def bwd(res, dy):
        a, b = res
        if (ca, cb) == (1, 0):
            return raw(dy, b, 1, 1), raw(a, dy, 0, 0)
        return raw(dy, b, 1, 0), raw(dy, a, 0, 0)

    f.defvjp(fwd, bwd)
    return f


_dot16_nn, _dot16_nt = _make_dot16(1, 0), _make_dot16(1, 1)


def _attn_bias():
    qi = jnp.arange(8 * SWA_B) % SWA_B
    kj = jnp.arange(2 * SWA_B)
    rel = qi[:, None] + SWA_B - kj[None, :]
    valid = (rel >= 0) & (rel < SWA_B)
    neg = jnp.float32(-jnp.inf)
    return jnp.stack([jnp.where(valid & (kj[None, :] >= SWA_B), 0.0, neg), jnp.where(valid, 0.0, neg)]).astype(f32)


def _attn_group(qg, kb, vb, sinks, bias, hk):
    R = qg.shape[0]
    s = _dot16_nt(qg, kb) * 0.125 + bias
    head = (lax.broadcasted_iota(jnp.int32, (R, LANE), 0) >> 7) + 8 * hk
    lane = lax.broadcasted_iota(jnp.int32, (R, LANE), 1)
    sink = jnp.sum(jnp.where(lane == head, jnp.broadcast_to(sinks, (R, LANE)), 0.0), axis=1, keepdims=True)
    m = lax.stop_gradient(jnp.maximum(jnp.max(s, axis=1, keepdims=True), sink))
    p = jnp.exp(s - m)
    denom = jnp.sum(p, axis=1, keepdims=True) + jnp.exp(sink - m)
    return _dot16_nn(p / denom, vb)


def _group_rows(ref, hk):
    return jnp.concatenate([ref[:, 64 * (8 * hk + g):64 * (8 * hk + g + 1)] for g in range(8)], axis=0)


def _put_group(ref, hk, val):
    for g in range(8):
        ref[:, 64 * (8 * hk + g):64 * (8 * hk + g + 1)] = val[SWA_B * g:SWA_B * (g + 1)].astype(ref.dtype)


def _attn_specs():
    qs = pl.BlockSpec((SWA_B, 1024), lambda i: (i, 0))
    cur = pl.BlockSpec((SWA_B, LANE), lambda i: (i, 0))
    prev = pl.BlockSpec((SWA_B, LANE), lambda i: (jnp.maximum(i - 1, 0), 0))
    vcur = pl.BlockSpec((SWA_B, LANE), lambda i: (i, CB_SWV))
    vprev = pl.BlockSpec((SWA_B, LANE), lambda i: (jnp.maximum(i - 1, 0), CB_SWV))
    vec = pl.BlockSpec((1, LANE), lambda i: (0, 0))
    bias = pl.BlockSpec((None, 8 * SWA_B, 2 * SWA_B), lambda i: (jnp.minimum(i, 1), 0, 0))
    return qs, cur, prev, vcur, vprev, vec, bias


def _hosted_edge(hosted, which, h_ins, h_outs, sems, at):
    if hosted is None:
        return

    @pl.when(at)
    def _():
        hosted[which](h_ins, h_outs, *sems)


def _attn_fwd(name, sq, sk, proj, sinks, bias, hosted=None):
    T = sq.shape[0]
    nb = T // SWA_B
    h_args, h_specs, h_shapes, h_sems = _hosted_parts(hosted)
    nh = len(h_args)

    def body(q_ref, kp_ref, kc_ref, vp_ref, vc_ref, sk_ref, b_ref, *rest):
        h_ins, o_ref, h_outs, sems = rest[:nh], rest[nh], rest[nh + 1:2 * nh + 1], rest[2 * nh + 1:]
        _hosted_edge(hosted, "start", h_ins, h_outs, sems, pl.program_id(0) == 0)
        sinks_v, bias_v = sk_ref[...], b_ref[...]
        for hk in range(2):
            ks = slice(64 * hk, 64 * hk + 64)
            kb = jnp.concatenate([kp_ref[:, ks], kc_ref[:, ks]], axis=0)
            vb = jnp.concatenate([vp_ref[:, ks], vc_ref[:, ks]], axis=0)
            _put_group(o_ref, hk, _attn_group(_group_rows(q_ref, hk), kb, vb, sinks_v, bias_v, hk))
        _hosted_edge(hosted, "finish", h_ins, h_outs, sems, pl.program_id(0) == nb - 1)

    qs, cur, prev, vcur, vprev, vec, bspec = _attn_specs()
    outs = _pcall(body, grid=(nb,), in_specs=[qs, prev, cur, vprev, vcur, vec, bspec] + h_specs, out_specs=[qs] + h_specs,
                  out_shape=[SDS((T, 1024), bf16)] + h_shapes, scratch_shapes=h_sems, name=name)(sq, sk, sk, proj, proj, sinks, bias, *h_args)
    return outs[0], list(outs[1:])


def _attn_bwd(name, sq, sk, proj, sinks, bias, do, hosted=None):
    T = sq.shape[0]
    nb = T // SWA_B
    h_args, h_specs, h_shapes, h_sems = _hosted_parts(hosted)
    nh = len(h_args)

    def body(q_ref, kp_ref, kc_ref, vp_ref, vc_ref, sk_ref, b_ref, do_ref, *rest):
        h_ins, h_outs, sems = rest[:nh], rest[nh + 6:2 * nh + 6], rest[2 * nh + 6:]
        dq_ref, dkp_ref, dkc_ref, dvp_ref, dvc_ref, dsk_ref = rest[nh:nh + 6]
        _hosted_edge(hosted, "start", h_ins, h_outs, sems, pl.program_id(0) == 0)

        @pl.when(pl.program_id(0) == 0)
        def _():
            dsk_ref[...] = jnp.zeros_like(dsk_ref)

        sinks_v, bias_v = sk_ref[...], b_ref[...]
        dsk = jnp.zeros((1, LANE), f32)
        for hk in range(2):
            ks = slice(64 * hk, 64 * hk + 64)
            kb = jnp.concatenate([kp_ref[:, ks], kc_ref[:, ks]], axis=0)
            vb = jnp.concatenate([vp_ref[:, ks], vc_ref[:, ks]], axis=0)
            _, vjp = jax.vjp(functools.partial(_attn_group, bias=bias_v, hk=hk), _group_rows(q_ref, hk), kb, vb, sinks_v)
            dq, dkb, dvb, ds_ = vjp(_group_rows(do_ref, hk))
            _put_group(dq_ref, hk, dq)
            dsk = dsk + ds_
            dkp_ref[:, ks] = dkb[:SWA_B]
            dkc_ref[:, ks] = dkb[SWA_B:]
            dvp_ref[:, ks] = dvb[:SWA_B]
            dvc_ref[:, ks] = dvb[SWA_B:]
        dsk_ref[...] += dsk
        _hosted_edge(hosted, "finish", h_ins, h_outs, sems, pl.program_id(0) == nb - 1)

    qs, cur, prev, vcur, vprev, vec, bspec = _attn_specs()
    outs = _pcall(
        body, grid=(nb,), in_specs=[qs, prev, cur, vprev, vcur, vec, bspec, qs] + h_specs, out_specs=[qs, cur, cur, cur, cur, vec] + h_specs,
        out_shape=[SDS((T, 1024), f32)] + [SDS((T, LANE), f32)] * 4 + [SDS((1, LANE), f32)] + h_shapes, scratch_shapes=h_sems,
        name=name)(sq, sk, sk, proj, proj, sinks, bias, do, *h_args)
    return tuple(outs[:6]), list(outs[6:])


def _shift_add(name, cur, prev, out_dtype):
    T = cur.shape[0]

    def body(c_ref, p_ref, o_ref):
        o_ref[0:T - SWA_B, :] = (c_ref[0:T - SWA_B, :] + p_ref[SWA_B:T, :]).astype(o_ref.dtype)
        o_ref[T - SWA_B:T, :] = c_ref[T - SWA_B:T, :].astype(o_ref.dtype)

    return _pcall(body, out_shape=SDS((T, LANE), out_dtype), name=name)(cur, prev)


def _loss(name, y, tgt):
    T = y.shape[0]

    def body(y_ref, t_ref, l_ref, dy_ref):
        @pl.when(pl.program_id(0) == 0)
        def _():
            l_ref[...] = jnp.zeros_like(l_ref)

        d = y_ref[...] - t_ref[...]
        l_ref[...] += jnp.sum(d * d) * (0.5 / D)
        dy_ref[...] = d * (1.0 / D)

    row = pl.BlockSpec((TR, D), lambda i: (i, 0))
    return _pcall(body, grid=(T // TR,), in_specs=[row, row], out_specs=[pl.BlockSpec((8, LANE), lambda i: (0, 0)), row],
                  out_shape=[SDS((8, LANE), f32), SDS((T, D), f32)], name=name)(y, tgt)


def _adamw(name, w, g, m, v):
    shape = w.shape
    C = shape[-1]
    R = int(np.prod(shape[:-1]))
    tr = _pick(R, (128, 64, 16, 8))
    bc1 = np.float32(1.0 - ADAM_B1 ** ADAM_STEP)
    bc2 = np.float32(1.0 - ADAM_B2 ** ADAM_STEP)

    def body(w_ref, g_ref, m_ref, v_ref, d_ref, mo_ref, vo_ref):
        g_ = g_ref[...]
        m_ = ADAM_B1 * m_ref[...] + (1.0 - ADAM_B1) * g_
        v_ = ADAM_B2 * v_ref[...] + (1.0 - ADAM_B2) * (g_ * g_)
        d_ref[...] = -ADAM_LR * ((m_ / bc1) / (jnp.sqrt(v_ / bc2) + ADAM_EPS) + ADAM_WD * w_ref[...])
        mo_ref[...] = m_
        vo_ref[...] = v_

    blk = pl.BlockSpec((tr, C), lambda i: (i, 0))
    outs = _pcall(body, grid=(R // tr,), in_specs=[blk] * 4, out_specs=[blk] * 3, out_shape=[SDS((R, C), f32)] * 3,
                  compiler_params=_cparams(VMEM_BIG), name=name)(*[t.reshape(R, C) for t in (w, g, m, v)])
    return [o.reshape(shape) for o in outs]


def _silu_rows(name, x):
    def body(x_ref, o_ref):
        t = x_ref[...]
        o_ref[...] = (t * jax.nn.sigmoid(t)).astype(o_ref.dtype)

    return _pcall(body, out_shape=SDS(x.shape, bf16), name=name)(x)


def _sum_leading(name, x):
    n = x.shape[0]

    def body(x_ref, o_ref):
        acc = x_ref[0]
        for k in range(1, n):
            acc = acc + x_ref[k]
        o_ref[...] = acc

    tr = x.shape[1] if x.size * 4 <= 8 * 2 ** 20 else _pick(x.shape[1], (COMM_TR, 8))
    return _pcall(body, grid=(x.shape[1] // tr,), in_specs=[pl.BlockSpec((n, tr, x.shape[2]), lambda i: (0, i, 0))],
                  out_specs=pl.BlockSpec((tr, x.shape[2]), lambda i: (i, 0)), out_shape=SDS(x.shape[1:], x.dtype), name=name)(x)


HBM_SPEC = pl.BlockSpec(memory_space=pltpu.HBM)


def _position():
    x, y, c = lax.axis_index("x"), lax.axis_index("y"), lax.axis_index("c")
    return x, y, c, [(1 - x, y), (x, 1 - y), (1 - x, 1 - y)]


def _remote(src, dst, send_sems, recv_sems, k, to):
    return pltpu.make_async_remote_copy(src_ref=src, dst_ref=dst, send_sem=send_sems.at[k], recv_sem=recv_sems.at[k],
                                        device_id=to, device_id_type=MESH)


def _allgather_all(name, buf):
    r, W = buf.shape

    def body(in_ref, out_ref, send_sems, recv_sems, local_sem):
        x, y, c, _ = _position()
        me = 4 * x + 2 * y + c
        mine = pltpu.make_async_copy(in_ref, out_ref.at[me], local_sem)
        mine.start()
        peers = []
        for mk in range(1, 8):
            mx, my, mc = (mk >> 2) & 1, (mk >> 1) & 1, mk & 1
            px = 1 - x if mx else x
            py = 1 - y if my else y
            pc = 1 - c if mc else c
            peers.append((px, py, pc))
        sends = [_remote(in_ref, out_ref.at[me], send_sems, recv_sems, k, p) for k, p in enumerate(peers)]
        for cp in sends:
            cp.start()
        for k, (px, py, pc) in enumerate(peers):
            slot = out_ref.at[4 * px + 2 * py + pc]
            _remote(slot, slot, send_sems, recv_sems, k, (px, py, pc)).wait_recv()
        for cp in sends:
            cp.wait_send()
        mine.wait()

    return _pcall(body, in_specs=[HBM_SPEC], out_specs=HBM_SPEC, out_shape=SDS((8, r, W), buf.dtype),
                  scratch_shapes=[pltpu.SemaphoreType.DMA((7,)), pltpu.SemaphoreType.DMA((7,)), pltpu.SemaphoreType.DMA], name=name)(buf)


def _dma_sems(n):
    return [pltpu.SemaphoreType.DMA((n,)), pltpu.SemaphoreType.DMA((n,))]


def _gather_chips(name, shards):
    n = len(shards)

    def body(*refs):
        ins, outs = refs[:n], refs[n:2 * n]
        send_sems, recv_sems = refs[2 * n:]
        x, y, c, chips = _position()
        me = 2 * x + y
        sib = (x, y, 1 - c)
        sends, halves = [], []
        for i in range(n):
            rh = ins[i].shape[0] // 2
            halves.append((pl.ds(pl.multiple_of(c * rh, 16), rh), pl.ds(pl.multiple_of((1 - c) * rh, 16), rh)))
        for i in range(n):
            for j, (cx, cy) in enumerate(chips):
                cp = _remote(ins[i].at[halves[i][0]], outs[i].at[me, halves[i][0]], send_sems, recv_sems, 6 * i + j, (cx, cy, c))
                cp.start()
                sends.append(cp)
        for j, (cx, cy) in enumerate(chips):
            for i in range(n):
                rows = outs[i].at[2 * cx + cy, halves[i][0]]
                _remote(rows, rows, send_sems, recv_sems, 6 * i + j, (cx, cy, c)).wait_recv()
                cp = _remote(rows, rows, send_sems, recv_sems, 6 * i + 3 + j, sib)
                cp.start()
                sends.append(cp)
        for j, (cx, cy) in enumerate(chips):
            for i in range(n):
                rows = outs[i].at[2 * cx + cy, halves[i][1]]
                _remote(rows, rows, send_sems, recv_sems, 6 * i + 3 + j, sib).wait_recv()
        for cp in sends:
            cp.wait_send()

    return _pcall(body, in_specs=[HBM_SPEC] * n, out_specs=[HBM_SPEC] * n, out_shape=[SDS((4,) + s.shape, s.dtype) for s in shards],
                  scratch_shapes=_dma_sems(6 * n), name=name)(*shards)


def _swap_halves_multi(name, slots):
    n = len(slots)

    def body(*refs):
        ins, outs = refs[:n], refs[n:2 * n]
        send_sems, recv_sems = refs[2 * n:]
        x, y, c, _ = _position()
        cps = []
        for i in range(n):
            rh = ins[i].shape[1] // 2
            ohalf = pl.ds(pl.multiple_of((1 - c) * rh, 8), rh)
            cp = _remote(ins[i].at[:, ohalf, :], outs[i], send_sems, recv_sems, i, (x, y, 1 - c))
            cp.start()
            cps.append(cp)
        for cp in cps:
            cp.wait()

    return _pcall(body, in_specs=[HBM_SPEC] * n, out_specs=[HBM_SPEC] * n,
                  out_shape=[SDS((4, s.shape[1] // 2, s.shape[2]), s.dtype) for s in slots], scratch_shapes=_dma_sems(n), name=name)(*slots)


def _pair_add(name, g4, b1, c):
    _, R, W = g4.shape
    tr = _pick(R // 2, (256, 128, 32, 16))
    nblk = (R // 2) // tr

    def body(c_ref, g_ref, b_ref, o_ref):
        o_ref[...] = (g_ref[...] + b_ref[...]).astype(o_ref.dtype)

    grid_spec = pltpu.PrefetchScalarGridSpec(
        num_scalar_prefetch=1, grid=(4, nblk),
        in_specs=[pl.BlockSpec((None, tr, W), lambda s, i, c_ref: (s, c_ref[0] * nblk + i, 0)),
                  pl.BlockSpec((None, tr, W), lambda s, i, c_ref: (s, i, 0))],
        out_specs=pl.BlockSpec((None, tr, W), lambda s, i, c_ref: (s, i, 0)))
    return _pcall(body, grid_spec=grid_spec, out_shape=SDS((4, R // 2, W), bf16), name=name)(c.reshape(1), g4, b1)


def _scatter_chips_multi(name, ps):
    n = len(ps)

    def body(*refs):
        ins, outs = refs[:n], refs[n:2 * n]
        send_sems, recv_sems = refs[2 * n:]
        x, y, c, chips = _position()
        me = 2 * x + y
        sends = []
        for i in range(n):
            for j, (cx, cy) in enumerate(chips):
                cp = _remote(ins[i].at[2 * cx + cy], outs[i].at[me], send_sems, recv_sems, 3 * i + j, (cx, cy, c))
                cp.start()
                sends.append(cp)
        for i in range(n):
            for j, (cx, cy) in enumerate(chips):
                slot = outs[i].at[2 * cx + cy]
                _remote(slot, slot, send_sems, recv_sems, 3 * i + j, (cx, cy, c)).wait_recv()
        for cp in sends:
            cp.wait_send()

    return _pcall(body, in_specs=[HBM_SPEC] * n, out_specs=[HBM_SPEC] * n, out_shape=[SDS(p.shape, p.dtype) for p in ps],
                  scratch_shapes=_dma_sems(3 * n), name=name)(*ps)


def _sum_chips(name, p4, b2, chip, c):
    _, Rh, W = p4.shape
    tr = _pick(Rh, (256, 128, 32, 16))
    nblk = Rh // tr

    def body(m_ref, c_ref, own_ref, r1_ref, r2_ref, r3_ref, o_ref):
        o_ref[...] = ((own_ref[...].astype(f32) + r1_ref[...].astype(f32)) + r2_ref[...].astype(f32)) + r3_ref[...].astype(f32)

    other = lambda k: pl.BlockSpec((None, tr, W), lambda i, m_ref, c_ref: (m_ref[0] ^ k, i, 0))
    grid_spec = pltpu.PrefetchScalarGridSpec(
        num_scalar_prefetch=2, grid=(nblk,),
        in_specs=[pl.BlockSpec((None, tr, W), lambda i, m_ref, c_ref: (m_ref[0], i, 0)), other(1), other(2), other(3)],
        out_specs=pl.BlockSpec((tr, W), lambda i, m_ref, c_ref: (c_ref[0] * nblk + i, 0)))
    return _pcall(body, grid_spec=grid_spec, out_shape=SDS((2 * Rh, W), f32), name=name)(chip.reshape(1), c.reshape(1), p4, b2, b2, b2)


def _join_halves(name, fulls):
    n = len(fulls)

    def body(*refs):
        outs = refs[n:2 * n]
        send_sems, recv_sems = refs[2 * n:]
        x, y, c, _ = _position()
        cps = []
        for i in range(n):
            rh = outs[i].shape[0] // 2
            mine = outs[i].at[pl.ds(pl.multiple_of(c * rh, 8), rh)]
            theirs = outs[i].at[pl.ds(pl.multiple_of((1 - c) * rh, 8), rh)]
            cp = _remote(mine, mine, send_sems, recv_sems, i, (x, y, 1 - c))
            cp.start()
            cps.append((cp, _remote(theirs, theirs, send_sems, recv_sems, i, (x, y, 1 - c))))
        for cp, back in cps:
            back.wait_recv()
            cp.wait_send()

    return _pcall(body, in_specs=[HBM_SPEC] * n, out_specs=[HBM_SPEC] * n, out_shape=[SDS(r.shape, r.dtype) for r in fulls],
                  input_output_aliases={i: i for i in range(n)}, scratch_shapes=_dma_sems(n), name=name)(*fulls)


def _hosted_gather(shards):
    n = len(shards)

    def half(ref_rows, c):
        rh = ref_rows // 2
        return pl.ds(pl.multiple_of(c * rh, 16), rh)

    def start(ins, outs, send_sems, recv_sems):
        x, y, c, chips = _position()
        me = 2 * x + y
        for i in range(n):
            rows = half(ins[i].shape[0], c)
            for j, (cx, cy) in enumerate(chips):
                _remote(ins[i].at[rows], outs[i].at[me, rows], send_sems, recv_sems, 3 * i + j, (cx, cy, c)).start()

    def finish(ins, outs, send_sems, recv_sems):
        x, y, c, chips = _position()
        me = 2 * x + y
        for i in range(n):
            rows = half(ins[i].shape[0], c)
            for j, (cx, cy) in enumerate(chips):
                _remote(ins[i].at[rows], outs[i].at[2 * cx + cy, rows], send_sems, recv_sems, 3 * i + j, (cx, cy, c)).wait_recv()
        for i in range(n):
            rows = half(ins[i].shape[0], c)
            for j, (cx, cy) in enumerate(chips):
                _remote(ins[i].at[rows], outs[i].at[me, rows], send_sems, recv_sems, 3 * i + j, (cx, cy, c)).wait_send()

    return {"arrays": shards, "out_shape": [SDS((4,) + s.shape, s.dtype) for s in shards], "n_sems": 3 * n, "start": start, "finish": finish}


def _gather_forward(name, gathered):
    n = len(gathered)

    def body(*refs):
        outs = refs[n:2 * n]
        send_sems, recv_sems = refs[2 * n:]
        x, y, c, chips = _position()
        sib = (x, y, 1 - c)
        sends = []
        for i in range(n):
            rh = outs[i].shape[1] // 2
            mine = pl.ds(pl.multiple_of(c * rh, 16), rh)
            for j, (cx, cy) in enumerate(chips):
                rows = outs[i].at[2 * cx + cy, mine]
                cp = _remote(rows, rows, send_sems, recv_sems, 3 * i + j, sib)
                cp.start()
                sends.append(cp)
        for i in range(n):
            rh = outs[i].shape[1] // 2
            theirs = pl.ds(pl.multiple_of((1 - c) * rh, 16), rh)
            for j, (cx, cy) in enumerate(chips):
                rows = outs[i].at[2 * cx + cy, theirs]
                _remote(rows, rows, send_sems, recv_sems, 3 * i + j, sib).wait_recv()
        for cp in sends:
            cp.wait_send()

    return _pcall(body, in_specs=[HBM_SPEC] * n, out_specs=[HBM_SPEC] * n, out_shape=[SDS(g.shape, g.dtype) for g in gathered],
                  input_output_aliases={i: i for i in range(n)}, scratch_shapes=_dma_sems(3 * n), name=name)(*gathered)


def _hosted_scatter(ps):
    n = len(ps)

    def start(ins, outs, send_sems, recv_sems):
        x, y, c, chips = _position()
        me = 2 * x + y
        for i in range(n):
            for j, (cx, cy) in enumerate(chips):
                _remote(ins[i].at[2 * cx + cy], outs[i].at[me], send_sems, recv_sems, 3 * i + j, (cx, cy, c)).start()

    def finish(ins, outs, send_sems, recv_sems):
        x, y, c, chips = _position()
        me = 2 * x + y
        for i in range(n):
            for j, (cx, cy) in enumerate(chips):
                slot = outs[i].at[2 * cx + cy]
                _remote(slot, slot, send_sems, recv_sems, 3 * i + j, (cx, cy, c)).wait_recv()
        for i in range(n):
            for j, (cx, cy) in enumerate(chips):
                _remote(ins[i].at[2 * cx + cy], outs[i].at[me], send_sems, recv_sems, 3 * i + j, (cx, cy, c)).wait_send()

    return {"arrays": ps, "out_shape": [SDS(p.shape, p.dtype) for p in ps], "n_sems": 3 * n, "start": start, "finish": finish}


def _hosted_swap(slots):
    n = len(slots)

    def copies(ins, outs, send_sems, recv_sems):
        x, y, c, _ = _position()
        cps = []
        for i in range(n):
            rh = ins[i].shape[1] // 2
            ohalf = pl.ds(pl.multiple_of((1 - c) * rh, 8), rh)
            cps.append(_remote(ins[i].at[:, ohalf, :], outs[i], send_sems, recv_sems, i, (x, y, 1 - c)))
        return cps

    def start(ins, outs, send_sems, recv_sems):
        for cp in copies(ins, outs, send_sems, recv_sems):
            cp.start()

    def finish(ins, outs, send_sems, recv_sems):
        for cp in copies(ins, outs, send_sems, recv_sems):
            cp.wait()

    return {"arrays": slots, "out_shape": [SDS((4, s.shape[1] // 2, s.shape[2]), s.dtype) for s in slots], "n_sems": n,
            "start": start, "finish": finish}


def _rs_begin(slots, c):
    b1 = _swap_halves_multi("rs_swap", slots)
    return [_pair_add("rs_pair_%d" % i, g, b, c) for i, (g, b) in enumerate(zip(slots, b1))]


def _rs_end(ps, b2, c, chip):
    return _join_halves("rs_join", [_sum_chips("rs_sum_%d" % i, p, b, chip, c) for i, (p, b) in enumerate(zip(ps, b2))])


def _reduce_scatter_multi(slots, c, chip):
    ps = _rs_begin(slots, c)
    return _rs_end(ps, _scatter_chips_multi("rs_scatter", ps), c, chip)


_BIG = ("w_in", "w_dn_out", "w_swa_out", "w_o", "w_up", "w_down")


_W_IN_PIECES = ((0, 3072, 0), (3072, 4096, 3072), (5392, 6416, 4096), (6416, 7440, 5120), (4112, 5136, 6144), (4096, 4112, 7168),
                (5136, 5264, 7296), (5264, 5392, 7424))
_W_IN_SHARD = IN_TOTAL // 4


def _w_in_from_slots(g):
    parts, at = [], 0
    for lo, hi, dst in _W_IN_PIECES:
        if dst > at:
            parts.append(jnp.zeros((g.shape[1], dst - at), g.dtype))
        for s in range(4):
            a, b = max(lo, s * _W_IN_SHARD), min(hi, (s + 1) * _W_IN_SHARD)
            if a < b:
                parts.append(g[s][:, a - s * _W_IN_SHARD:b - s * _W_IN_SHARD])
        at = dst + hi - lo
    parts.append(jnp.zeros((g.shape[1], PROJ_W - at), g.dtype))
    return jnp.concatenate(parts, axis=1)


def _w_in_to_slots(gw):
    slots = []
    for s in range(4):
        parts = []
        for lo, hi, dst in sorted(_W_IN_PIECES):
            a, b = max(lo, s * _W_IN_SHARD), min(hi, (s + 1) * _W_IN_SHARD)
            if a < b:
                parts.append(gw[:, dst + a - lo:dst + b - lo])
        slots.append(jnp.concatenate(parts, axis=1))
    return jnp.stack(slots)


def _assemble_mixer(gs):
    rows = lambda g: g.reshape(4 * g.shape[1], g.shape[2])
    return {"w_in": _w_in_from_slots(gs[0]), "w_dn_out": rows(gs[1]), "w_swa_out": rows(gs[2]), "w_o": rows(gs[3])}


def _grad_slots(gw):
    rows = lambda g: g.reshape(4, g.shape[0] // 4, g.shape[1])
    return [_w_in_to_slots(gw["w_in"]), rows(gw["w_dn_out"]), rows(gw["w_swa_out"]), rows(gw["w_o"]), gw["w_up"], rows(gw["w_down"])]


def _pad_lanes(v, n=LANE):
    return jnp.pad(v, (0, n - v.shape[0])).reshape(1, n)


def _layer_consts(P):
    K = {}
    K["norm_mix"] = P["norm_mix"].reshape(1, D)
    K["norm_ffn"] = P["norm_ffn"].reshape(1, D)
    K["alog"] = _pad_lanes(P["dn_a_log"])
    K["dtb"] = _pad_lanes(P["dn_dt_bias"])
    K["dn_norm"] = jnp.tile(P["dn_norm"], 8).reshape(1, D)
    K["qn"] = jnp.tile(P["swa_q_norm"], 16).reshape(1, D)
    K["kn"] = jnp.tile(P["swa_k_norm"], 2).reshape(1, LANE)
    K["sinks"] = _pad_lanes(P["swa_sinks"])
    K["ffn_b"] = P["ffn_conv_b"].reshape(1, D_FF)
    return K


def _layer_fwd(x, mod, W, K, tabs, bd, hosted=None, late=None):
    late = late or {}
    sh1, sc1, gt1, sh2, sc2, gt2 = mod
    S = {"x": x}
    h1, h1t = _rowwise_fwd("normmod1_fwd", _normmod_fn, [(x, 0, D)], [K["norm_mix"], sc1, sh1], [D], [bf16], also_transposed=True)
    if "proj" in late:
        proj, arrived = _matmul("proj_fwd", h1, W["w_in"], "nn", f32, hosted=late["proj"][0])
        W.update(late["proj"][1](arrived))
    else:
        proj = _matmul("proj_fwd", h1, W["w_in"], "nn", f32)
    qn = _dnconv_fwd("dnconv_q_fwd", proj, CB_Q, W["dn_conv"], True)
    kn = _dnconv_fwd("dnconv_k_fwd", proj, CB_K, W["dn_conv"], True)
    vc = _dnconv_fwd("dnconv_v_fwd", proj, CB_V, W["dn_conv"], False)
    o, sall, xinv, hosted_out = _gdn_fwd("gdn_fwd", qn, kn, vc, proj, K["alog"], K["dtb"], hosted=hosted)
    (on,) = _rowwise_fwd("dngate_fwd", _dngate_fn, [(o, 0, D), (proj, WB_Z, D)], [K["dn_norm"]], [D], [bf16])
    ya = _matmul("dnout_fwd", on, W["w_dn_out"], "nn", f32)
    sq = _qkprep_fwd("qprep_fwd", proj, WB_SWQ, D, K["qn"], bd[0], tabs[0])
    sk = _qkprep_fwd("kprep_fwd", proj, CB_SWK, LANE, K["kn"], bd[1], tabs[1])
    attn, arrived = _attn_fwd("attn_fwd", sq, sk, proj, K["sinks"], K["attn_bias"], hosted=late["attn"][0] if "attn" in late else None)
    if "attn" in late:
        W.update(late["attn"][1](arrived))
    yb = _matmul("swaout_fwd", attn, W["w_swa_out"], "nn", f32)
    (merged,) = _rowwise_fwd("merge_fwd", _merge_fn, [(proj, WB_GA, D), (proj, WB_GB, D), (ya, 0, D), (yb, 0, D)], [], [D], [bf16])
    t1, x1 = _matmul("wo_fwd", merged, W["w_o"], "nn", f32, resid=(x, gt1))
    h2, h2t = _rowwise_fwd("normmod2_fwd", _normmod_fn, [(x1, 0, D)], [K["norm_ffn"], sc2, sh2], [D], [bf16], also_transposed=True)
    up = _matmul("up_fwd", h2, W["w_up"], "nn", f32, b_slots=True)
    mid = _ffnact_fwd("ffnact_fwd", up, W["ffn_conv"], K["ffn_b"])
    t2, x2 = _matmul("down_fwd", mid, W["w_down"], "nn", f32, resid=(x1, gt2))
    S.update(h1t=h1t, h2t=h2t, proj=proj, qn=qn, kn=kn, vc=vc, o=o, sall=sall, xinv=xinv, on=on, ya=ya, sq=sq, sk=sk, attn=attn, yb=yb,
             merged=merged, t1=t1, x1=x1, h2=h2, up=up, mid=mid, t2=t2)
    return x2, S, hosted_out


def _layer_bwd(dx2, S, mod, W, K, tabs, bd, carry=None, early=None):
    sh1, sc1, gt1, sh2, sc2, gt2 = mod
    x, x1, proj, up = S["x"], S["x1"], S["proj"], S["up"]
    T = x.shape[0]
    gw, gs = {}, {}
    dt2, dgt2 = _rowwise_bwd("resid2_bwd", _resid_fn, [(x1, 0, D), (S["t2"], 0, D)], [gt2], [(dx2, 0, D)], [None, bf16])
    dmid = _matmul("down_bwd_x", dt2, W["w_down"], "nt", bf16)
    gw["w_down"] = _matmul("down_bwd_w", S["mid"], dt2, "tn", f32)
    dact, dlin, gw["ffn_conv"], dffn_b = _ffnact_bwd("ffnact_bwd", up, W["ffn_conv"], K["ffn_b"], dmid)
    dup = (dact, dlin)
    dh2 = _matmul("up_bwd_x", dup, W["w_up"], "nt", f32, b_slots=True)
    if carry is None:
        gw["w_up"] = _matmul("up_bwd_w", S["h2t"], dup, "nn", f32, out_slots=4)
        pair_sums = hosted = None
    else:
        gw["w_up"], b1 = _matmul("up_bwd_w", S["h2t"], dup, "nn", f32, out_slots=4, hosted=_hosted_swap(carry[0]))
        pair_sums = [_pair_add("rs_pair_%d" % i, g, b, carry[1]) for i, (g, b) in enumerate(zip(carry[0], b1))]
        hosted = _hosted_scatter(pair_sums)
    dx1, dnorm_ffn, dsc2, dsh2 = _rowwise_bwd("normmod2_bwd", _normmod_fn, [(x1, 0, D)], [K["norm_ffn"], sc2, sh2], [(dh2, 0, D)], [f32],
                                              add_to_first=(dx2, 0, D))
    early_out = None
    if early is not None:
        ffn_slots = [gw["w_up"], gw["w_down"].reshape(4, D_FF // 4, D)]
        ffn_ps = [_pair_add("rs_pair_ffn_%d" % i, g, b, early[0]) for i, (g, b) in enumerate(zip(ffn_slots, _swap_halves_multi("rs_swap_ffn", ffn_slots)))]
    dt1, dgt1 = _rowwise_bwd("resid1_bwd", _resid_fn, [(x, 0, D), (S["t1"], 0, D)], [gt1], [(dx1, 0, D)], [None, bf16])
    dmerged = _matmul("wo_bwd_x", dt1, W["w_o"], "nt", f32)
    gw["w_o"] = _matmul("wo_bwd_w", S["merged"], dt1, "tn", f32)
    dproj, dya, dyb = _rowwise_bwd("merge_bwd", _merge_fn, [(proj, WB_GA, D), (proj, WB_GB, D), (S["ya"], 0, D), (S["yb"], 0, D)], [],
                                   [(dmerged, 0, D)], [bf16, bf16, bf16, bf16], dest=(None, (0, 1), WB_GA // 2))
    don = _matmul("dnout_bwd_x", dya, W["w_dn_out"], "nt", f32)
    gw["w_dn_out"] = _matmul("dnout_bwd_w", S["on"], dya, "tn", f32)
    dproj, do, ddn_norm = _rowwise_bwd("dngate_bwd", _dngate_fn, [(S["o"], 0, D), (proj, WB_Z, D)], [K["dn_norm"]], [(don, 0, D)],
                                       [f32, bf16], dest=(dproj, (1,), WB_Z))
    (dqn, dkn, dvc, dab, dalog, ddtb), hosted_out = _gdn_bwd("gdn_bwd", S["qn"], S["kn"], S["vc"], proj, K["alog"], K["dtb"], S["sall"], S["xinv"], do,
                                                            hosted=hosted)
    dproj, dwq = _dnconv_bwd("dnconv_q_bwd", proj, CB_Q, W["dn_conv"], dqn, True, dproj)
    dproj, dwk = _dnconv_bwd("dnconv_k_bwd", proj, CB_K, W["dn_conv"], dkn, True, dproj)
    dproj, dwv = _dnconv_bwd("dnconv_v_bwd", proj, CB_V, W["dn_conv"], dvc, False, dproj)
    gw["dn_conv"] = jnp.concatenate([dwq, dwk, dwv], axis=1)
    dattn = _matmul("swaout_bwd_x", dyb, W["w_swa_out"], "nt", f32)
    gw["w_swa_out"] = _matmul("swaout_bwd_w", S["attn"], dyb, "tn", f32)
    (dsq, dkp, dkc, dvp, dvc_, dsinks), ffn_b2 = _attn_bwd("attn_bwd", S["sq"], S["sk"], proj, K["sinks"], K["attn_bias"], dattn,
                                                          hosted=None if early is None else _hosted_scatter(ffn_ps))
    if early is not None:
        early_out = _rs_end(ffn_ps, ffn_b2, early[0], early[1])
    dsk = _shift_add("attn_dk_join", dkc, dkp, f32)
    dswv = _shift_add("attn_dv_join", dvc_, dvp, bf16)
    dproj, dqn_w = _qkprep_bwd("qprep_bwd", proj, WB_SWQ, D, K["qn"], bd[0], tabs[0], dsq, dest_buf=dproj)
    dswk, dkn_w = _qkprep_bwd("kprep_bwd", proj, CB_SWK, LANE, K["kn"], bd[1], tabs[1], dsk)
    tail = jnp.concatenate([dab.astype(bf16), dswk, dswv, jnp.zeros((T, LANE), bf16)], axis=1)
    dproj = lax.dynamic_update_slice(dproj, tail, (0, CB_AB * LANE))
    dh1 = _matmul("proj_bwd_x", dproj, W["w_in"], "nt", f32)
    gw["w_in"] = _matmul("proj_bwd_w", S["h1t"], dproj, "nn", f32)
    dx, dnorm_mix, dsc1, dsh1 = _rowwise_bwd("normmod1_bwd", _normmod_fn, [(x, 0, D)], [K["norm_mix"], sc1, sh1], [(dh1, 0, D)], [f32],
                                             add_to_first=(dx1, 0, D))
    gs = {"norm_mix": dnorm_mix[0], "dn_a_log": dalog[0, :8], "dn_dt_bias": ddtb[0, :8], "dn_norm": ddn_norm.reshape(8, LANE).sum(0),
          "swa_q_norm": dqn_w.reshape(16, 64).sum(0), "swa_k_norm": dkn_w.reshape(2, 64).sum(0), "swa_sinks": dsinks[0, :16],
          "norm_ffn": dnorm_ffn[0], "ffn_conv_b": dffn_b[0]}
    dmod = jnp.concatenate([dsh1, dsc1, dgt1, dsh2, dsc2, dgt2], axis=1)
    return dx, gw, gs, dmod, (pair_sums, hosted_out), early_out


def _rope_tables(pos):
    T = pos.shape[0]
    half = 8
    inv = jnp.power(ROPE_THETA, -jnp.arange(half, dtype=f32) / half)
    ang = pos.astype(f32)[:, None] * inv
    cos, sin = jnp.cos(ang), jnp.sin(ang)
    z8, z48, o48 = jnp.zeros((T, 8), f32), jnp.zeros((T, 48), f32), jnp.ones((T, 48), f32)
    c64 = jnp.concatenate([cos, cos, o48], axis=1)
    s1 = jnp.concatenate([-sin, z8, z48], axis=1)
    s2 = jnp.concatenate([z8, sin, z48], axis=1)
    return tuple(jnp.tile(t, (1, 2)) for t in (c64, s1, s2))


_SMALL = (("norm_mix", D), ("dn_a_log", 8), ("dn_dt_bias", 8), ("dn_norm", 128), ("swa_q_norm", 64), ("swa_k_norm", 64),
          ("swa_sinks", 16), ("norm_ffn", D), ("ffn_conv_b", D_FF), ("b_ada", 6 * D))
_CONV = (("dn_conv", 4 * 3072), ("ffn_conv", 3 * D_FF))
_CONV_SHARD = (("dn_conv", 4 * 768), ("ffn_conv", 3 * 704))


def _pack_small(vals, spec):
    flat = jnp.concatenate([vals[nm].reshape(-1) for nm, _ in spec])
    rows = -(-flat.shape[0] // (8 * LANE)) * 8
    return jnp.pad(flat, (0, rows * LANE - flat.shape[0])).reshape(rows, LANE)


def _unpack_small(buf, spec):
    flat = buf.reshape(-1)
    out, off = {}, 0
    for nm, n in spec:
        out[nm] = flat[off:off + DEPTH * n].reshape(DEPTH, n)
        off += DEPTH * n
    return out


def kernel(x, c, positions, w_ada, b_ada, norm_mix, w_in, dn_conv, dn_a_log, dn_dt_bias, dn_norm, w_dn_out, swa_q_norm, swa_k_norm, swa_sinks, w_swa_out, w_o, norm_ffn, w_up, ffn_conv, ffn_conv_b, w_down, loss_target, m_w_ada, m_b_ada, m_norm_mix, m_w_in, m_dn_conv, m_dn_a_log, m_dn_dt_bias, m_dn_norm, m_w_dn_out, m_swa_q_norm, m_swa_k_norm, m_swa_sinks, m_w_swa_out, m_w_o, m_norm_ffn, m_w_up, m_ffn_conv, m_ffn_conv_b, m_w_down, v_w_ada, v_b_ada, v_norm_mix, v_w_in, v_dn_conv, v_dn_a_log, v_dn_dt_bias, v_dn_norm, v_w_dn_out, v_swa_q_norm, v_swa_k_norm, v_swa_sinks, v_w_swa_out, v_w_o, v_norm_ffn, v_w_up, v_ffn_conv, v_ffn_conv_b, v_w_down):
    weights = dict(w_ada=w_ada, b_ada=b_ada, norm_mix=norm_mix, w_in=w_in, dn_conv=dn_conv, dn_a_log=dn_a_log, dn_dt_bias=dn_dt_bias,
                   dn_norm=dn_norm, w_dn_out=w_dn_out, swa_q_norm=swa_q_norm, swa_k_norm=swa_k_norm, swa_sinks=swa_sinks,
                   w_swa_out=w_swa_out, w_o=w_o, norm_ffn=norm_ffn, w_up=w_up, ffn_conv=ffn_conv, ffn_conv_b=ffn_conv_b, w_down=w_down)
    mom_m = dict(w_ada=m_w_ada, b_ada=m_b_ada, norm_mix=m_norm_mix, w_in=m_w_in, dn_conv=m_dn_conv, dn_a_log=m_dn_a_log,
                 dn_dt_bias=m_dn_dt_bias, dn_norm=m_dn_norm, w_dn_out=m_w_dn_out, swa_q_norm=m_swa_q_norm, swa_k_norm=m_swa_k_norm,
                 swa_sinks=m_swa_sinks, w_swa_out=m_w_swa_out, w_o=m_w_o, norm_ffn=m_norm_ffn, w_up=m_w_up, ffn_conv=m_ffn_conv,
                 ffn_conv_b=m_ffn_conv_b, w_down=m_w_down)
    mom_v = dict(w_ada=v_w_ada, b_ada=v_b_ada, norm_mix=v_norm_mix, w_in=v_w_in, dn_conv=v_dn_conv, dn_a_log=v_dn_a_log,
                 dn_dt_bias=v_dn_dt_bias, dn_norm=v_dn_norm, w_dn_out=v_w_dn_out, swa_q_norm=v_swa_q_norm, swa_k_norm=v_swa_k_norm,
                 swa_sinks=v_swa_sinks, w_swa_out=v_w_swa_out, w_o=v_w_o, norm_ffn=v_norm_ffn, w_up=v_w_up, ffn_conv=v_ffn_conv,
                 ffn_conv_b=v_ffn_conv_b, w_down=v_w_down)
    order = ["w_ada", "b_ada", "norm_mix", "w_in", "dn_conv", "dn_a_log", "dn_dt_bias", "dn_norm", "w_dn_out", "swa_q_norm",
             "swa_k_norm", "swa_sinks", "w_swa_out", "w_o", "norm_ffn", "w_up", "ffn_conv", "ffn_conv_b", "w_down"]
    ax, ay, ac = lax.axis_index("x"), lax.axis_index("y"), lax.axis_index("c")
    chip = 2 * ax + ay
    dev = 4 * ax + 2 * ay + ac
    T = x.shape[1]
    xs = x[0]

    c_all = _allgather_all("gather_c", jnp.pad(c, ((0, 7), (0, 0)))).reshape(8, 8, D)[:, 0]
    c_act = _silu_rows("silu_c", jnp.pad(c_all, ((0, 8), (0, 0))))
    mod_sh = jnp.stack([
        _matmul("mod_fwd", c_act, w_ada[l].astype(bf16), "nn", f32,
                bias=lax.dynamic_slice(b_ada[l], (chip * 1536,), (1536,)).reshape(1, 1536)) for l in range(DEPTH)])
    mod_all = _allgather_all("gather_mod", mod_sh.reshape(DEPTH * 16 * 12, LANE)).reshape(8, DEPTH, 16, 1536)
    mod_me = jnp.concatenate([lax.dynamic_index_in_dim(mod_all[2 * s], dev, axis=1, keepdims=False) for s in range(4)], axis=1)

    tabs_q = _rope_tables(positions[0])
    tabs = (tabs_q, tabs_q)
    head = jnp.arange(LANE) // 64
    bd128 = (head[:, None] == head[None, :]).astype(f32) / 64.0
    bd = (bd128, bd128)
    attn_bias = _attn_bias()

    conv_all = _allgather_all("gather_conv", _pack_small({"dn_conv": dn_conv, "ffn_conv": ffn_conv}, _CONV_SHARD))
    conv_parts = [_unpack_small(conv_all[2 * s], _CONV_SHARD) for s in range(4)]
    dn_conv_full = jnp.concatenate([p["dn_conv"].reshape(DEPTH, 4, 768) for p in conv_parts], axis=2)
    ffn_conv_full = jnp.concatenate([p["ffn_conv"].reshape(DEPTH, 3, 704) for p in conv_parts], axis=2)

    saved, Ws, Ks, mods = [], [], [], []
    h = xs
    shards = [[weights[nm][l].astype(bf16) for nm in _BIG] for l in range(DEPTH)]
    rest = (0, 1, 2, 3, 5)
    rows = lambda g: g.reshape(4 * g.shape[1], g.shape[2])

    def arrive(name, arrived, l, idx):
        got = _gather_forward(name, arrived)
        return [lax.dynamic_update_index_in_dim(g, shards[l][i], chip, 0) for g, i in zip(got, idx)]

    def up_late(l):
        return (_hosted_gather([shards[l][4]]), lambda arrived: {"w_up": arrive("gather_up_pass", arrived, l, (4,))[0]})

    got = [lax.dynamic_update_index_in_dim(g, shards[0][i], chip, 0) for i, g in enumerate(_gather_chips("gather_w", shards[0][:4]))]
    for l in range(DEPTH):
        W = _assemble_mixer(got[:4])
        late = {"proj": up_late(l)}
        if l == 0:
            late["attn"] = (_hosted_gather([shards[0][5]]), lambda arrived: {"w_down": rows(arrive("gather_down_pass", arrived, 0, (5,))[0])})
        else:
            W["w_down"] = rows(got[4])
        W["dn_conv"], W["ffn_conv"] = dn_conv_full[l], ffn_conv_full[l]
        K = _layer_consts({nm: weights[nm][l] for nm in ("norm_mix", "norm_ffn", "dn_a_log", "dn_dt_bias", "dn_norm", "swa_q_norm",
                                                          "swa_k_norm", "swa_sinks", "ffn_conv_b")})
        K["attn_bias"] = attn_bias
        mod = tuple(mod_me[l, k * D:(k + 1) * D].reshape(1, D) for k in range(6))
        nxt = _hosted_gather([shards[l + 1][i] for i in rest]) if l + 1 < DEPTH else None
        h, S, arrived = _layer_fwd(h, mod, W, K, tabs, bd, hosted=nxt, late=late)
        if nxt is not None:
            got = arrive("gather_w_pass", arrived, l + 1, rest)
        saved.append(S), Ws.append(W), Ks.append(K), mods.append(mod)

    loss_blk, dh = _loss("loss", h, loss_target[0])
    loss = lax.psum(loss_blk[0, 0], ("x", "y", "c"))

    grad_sh = [None] * DEPTH
    small = [None] * DEPTH
    dmods = [None] * DEPTH
    slots = None
    for l in reversed(range(DEPTH)):
        dh, gw, gs, dmod, (ps, b2), ffn0 = _layer_bwd(dh, saved[l], mods[l], Ws[l], Ks[l], tabs, bd, carry=None if slots is None else (slots, ac),
                                                      early=(ac, chip) if l == 0 else None)
        if slots is not None:
            grad_sh[l + 1] = dict(zip(_BIG, _rs_end(ps, b2, ac, chip)))
        slots = _grad_slots(gw)
        small[l], dmods[l] = dict(gs, dn_conv=gw["dn_conv"], ffn_conv=gw["ffn_conv"]), dmod[0]
    grad_sh[0] = dict(zip(_BIG, _reduce_scatter_multi(slots[:4], ac, chip) + ffn0))

    spec_g = _SMALL + _CONV
    vals = {nm: jnp.stack([small[l][nm] for l in range(DEPTH)]) for nm, _ in spec_g if nm != "b_ada"}
    vals["b_ada"] = jnp.stack(dmods)
    small_all = _allgather_all("gather_small", _pack_small(vals, spec_g))
    g_small = _unpack_small(_sum_leading("sum_small", small_all), spec_g)
    dmod_all = jnp.stack([_unpack_small(small_all[d], spec_g)["b_ada"] for d in range(8)])
    dmod_sh = lax.dynamic_slice(dmod_all, (0, 0, chip * 1536), (8, DEPTH, 1536))
    dmod_sh = jnp.pad(dmod_sh, ((0, 8), (0, 0), (0, 0))).astype(bf16)
    g_w_ada = jnp.stack([_matmul("mod_bwd_w", c_act, dmod_sh[:, l], "tn", f32) for l in range(DEPTH)])

    grads = {nm: g_small[nm] for nm, _ in _SMALL}
    grads["dn_conv"] = lax.dynamic_slice(g_small["dn_conv"].reshape(DEPTH, 4, 3072), (0, 0, chip * 768), (DEPTH, 4, 768))
    grads["ffn_conv"] = lax.dynamic_slice(g_small["ffn_conv"].reshape(DEPTH, 3, D_FF), (0, 0, chip * 704), (DEPTH, 3, 704))
    grads["w_ada"] = g_w_ada
    for nm in _BIG:
        grads[nm] = jnp.stack([grad_sh[l][nm] for l in range(DEPTH)])

    delta, new_m, new_v = {}, {}, {}
    for nm in ("w_ada", "dn_conv", "ffn_conv") + _BIG:
        delta[nm], new_m[nm], new_v[nm] = _adamw("adamw_" + nm, weights[nm], grads[nm], mom_m[nm], mom_v[nm])
    sm = [_pack_small({nm: t[nm] for nm, _ in _SMALL}, _SMALL) for t in (weights, grads, mom_m, mom_v)]
    for tgt, buf in zip((delta, new_m, new_v), _adamw("adamw_small", *sm)):
        tgt.update(_unpack_small(buf, _SMALL))

    return (loss, dh[None], *[grads[n] for n in order], *[delta[n] for n in order], *[new_m[n] for n in order], *[new_v[n] for n in order])
```

```python
import functools

import jax
import jax.numpy as jnp
import numpy as np
from jax import lax
from jax.experimental import pallas as pl
from jax.experimental.pallas import tpu as pltpu

f32 = jnp.float32
bf16 = jnp.bfloat16
SDS = jax.ShapeDtypeStruct
HI = lax.Precision.HIGHEST
MESH = pl.DeviceIdType.MESH

D = 1024
DEPTH = 4
EPS = 1e-6
DN_C = 64
SWA_B = 128
LANE = 128
ROPE_THETA = 500000.0
D_FF = 2816
IN_TOTAL = 7440
PROJ_W = 7680
CB_Q, CB_K, CB_V = 0, 8, 16
CB_AB, CB_SWK, CB_SWV = 56, 57, 58
WB_Z, WB_GA, WB_GB, WB_SWQ = 3, 4, 5, 6
TR = 256
COMM_TR = 128
VMEM_BIG = 48 * 2 ** 20

ADAM_LR, ADAM_B1, ADAM_B2, ADAM_EPS, ADAM_WD, ADAM_STEP = 0.001, 0.9, 0.999, 1e-08, 0.01, 10


def _pcall(body, **kw):
    return pl.pallas_call(body, **kw)


def _cparams(vmem=None):
    return pltpu.CompilerParams(vmem_limit_bytes=vmem) if vmem else None


def _dot(a, b, ca, cb, precision=HI):
    return lax.dot_general(a, b, (((ca,), (cb,)), ((), ())), precision=precision, preferred_element_type=f32)


def _pick(n, cands):
    for c in cands:
        if n % c == 0:
            return c
    return n


def _tile(n, cap):
    if n <= cap:
        return n
    best = None
    for t in range(LANE, cap + 1, LANE):
        if n % t == 0:
            best = t
    assert best is not None, (n, cap)
    return best


def _matmul(name, a, b, mode, out_dtype, bias=None, out_slots=None, hosted=None, resid=None, b_slots=False):
    h_args, h_specs, h_shapes, h_sems = _hosted_parts(hosted)
    nh = len(h_args)
    a_parts = list(a) if isinstance(a, (tuple, list)) else [a]
    b_parts = list(b) if isinstance(b, (tuple, list)) else [b]
    na, nbp = len(a_parts), len(b_parts)
    assert (na == 1 or mode == "nt") and (nbp == 1 or mode == "nn")
    if b_slots:
        S_, rows_, cols_ = b.shape
        M = a_parts[0].shape[0]
        K, N = (rows_, S_ * cols_) if mode == "nn" else (S_ * cols_, rows_)
    elif mode == "nn":
        M, K = a_parts[0].shape
        N = b_parts[0].shape[1] * nbp
    elif mode == "nt":
        M, K = a_parts[0].shape[0], a_parts[0].shape[1] * na
        N = b_parts[0].shape[0]
    else:
        (K, M), N = a_parts[0].shape, b_parts[0].shape[1]
    tm = _tile(M, 1536 if mode == "tn" else 1024)
    tn = N // out_slots if out_slots else _tile(N, 1536)
    tk = _tile(K, 512 if mode == "tn" else (1024 if K <= 1024 else 1536))
    nk, nj = K // tk, N // tn
    ka, jb = nk // na, nj // nbp
    assert nk % na == 0 and nj % nbp == 0
    ca, cb = {"nn": (1, 0), "nt": (1, 1), "tn": (0, 0)}[mode]
    grid = (M // tm, nj, nk)
    n_bias, n_res = (1 if bias is not None else 0), (2 if resid is not None else 0)

    def body(*refs):
        a_refs, b_refs = refs[:na], refs[na:na + nbp]
        p = na + nbp
        bias_ref = refs[p] if n_bias else None
        res_refs = refs[p + n_bias:p + n_bias + n_res]
        p += n_bias + n_res
        h_ins, o_ref = refs[p:p + nh], refs[p + nh]
        xo_ref = refs[p + nh + 1] if n_res else None
        p += nh + 1 + (1 if n_res else 0)
        h_outs, rest = refs[p:p + nh], refs[p + nh:]
        acc, sems = (rest[0], rest[1:]) if nk > 1 else (None, rest)
        j, k = pl.program_id(1), pl.program_id(2)
        step = (pl.program_id(0) * grid[1] + j) * grid[2] + k
        _hosted_edge(hosted, "start", h_ins, h_outs, sems, step == 0)

        def finish(r):
            if bias is not None:
                r = r + bias_ref[...]
            o_ref[...] = r.astype(o_ref.dtype)
            if n_res:
                xo_ref[...] = res_refs[0][...] + res_refs[1][...] * r

        a_tile = a_refs[0][...] if na == 1 else jnp.where(k < ka, a_refs[0][...], a_refs[1][...])
        b_tile = b_refs[0][...] if nbp == 1 else jnp.where(j < jb, b_refs[0][...], b_refs[1][...])
        part = _dot(a_tile.astype(bf16), b_tile.astype(bf16), ca, cb, precision=None)
        if nk == 1:
            finish(part)
        else:
            @pl.when(k == 0)
            def _():
                acc[...] = part

            @pl.when(k > 0)
            def _():
                acc[...] += part

            @pl.when(k == nk - 1)
            def _():
                finish(acc[...])

        _hosted_edge(hosted, "finish", h_ins, h_outs, sems, step == grid[0] * grid[1] * grid[2] - 1)

    if mode == "tn":
        a_specs = [pl.BlockSpec((tk, tm), lambda i, j, k: (k, i))]
    else:
        a_specs = [pl.BlockSpec((tm, tk), lambda i, j, k, q=q: (i, jnp.clip(k - q * ka, 0, ka - 1))) for q in range(na)]
    if b_slots:
        assert (tn if mode == "nn" else tk) == b.shape[2]
        b_specs = [pl.BlockSpec((None, tk, tn), lambda i, j, k: (j, k, 0)) if mode == "nn" else pl.BlockSpec((None, tn, tk), lambda i, j, k: (k, j, 0))]
    elif mode == "nt":
        b_specs = [pl.BlockSpec((tn, tk), lambda i, j, k: (j, k))]
    else:
        b_specs = [pl.BlockSpec((tk, tn), lambda i, j, k, q=q: (k, jnp.clip(j - q * jb, 0, jb - 1))) for q in range(nbp)]
    in_specs, args = a_specs + b_specs, a_parts + b_parts
    if bias is not None:
        in_specs.append(pl.BlockSpec((1, tn), lambda i, j, k: (0, j)))
        args.append(bias)
    if out_slots:
        out_specs, out_shape = [pl.BlockSpec((None, tm, tn), lambda i, j, k: (j, i, 0))], [SDS((out_slots, M, tn), out_dtype)]
    else:
        out_specs, out_shape = [pl.BlockSpec((tm, tn), lambda i, j, k: (i, j))], [SDS((M, N), out_dtype)]
    if resid is not None:
        in_specs += [pl.BlockSpec((tm, tn), lambda i, j, k: (i, j)), pl.BlockSpec((1, tn), lambda i, j, k: (0, j))]
        args += list(resid)
        out_specs.append(pl.BlockSpec((tm, tn), lambda i, j, k: (i, j)))
        out_shape.append(SDS((M, N), f32))
    outs = _pcall(
        body, grid=grid, in_specs=in_specs + h_specs, out_specs=out_specs + h_specs, out_shape=out_shape + h_shapes,
        scratch_shapes=([pltpu.VMEM((tm, tn), f32)] if nk > 1 else []) + h_sems, compiler_params=_cparams(VMEM_BIG), name=name)(*args, *h_args)
    n_main = len(out_shape)
    res = list(outs[:n_main]) + ([list(outs[n_main:])] if hosted is not None else [])
    return res[0] if len(res) == 1 else tuple(res)


def _row_specs(rows, tr):
    return [pl.BlockSpec((tr, w), lambda i, j, off=off: (i, off + j)) for (_, off, w) in rows]


def _rowwise_fwd(name, fn, rows, vecs, out_widths, out_dtypes, nc=1, tr=TR, also_transposed=False):
    T = rows[0][0].shape[0]
    n_in = len(rows) + len(vecs)
    n_out = len(out_widths)

    def body(*refs):
        vals = [r[...].astype(f32) for r in refs[:n_in]]
        res = fn(*vals)
        for o_ref, r in zip(refs[n_in:n_in + n_out], res):
            o_ref[...] = r.astype(o_ref.dtype)
        if also_transposed:
            refs[n_in + n_out][...] = res[0].T.astype(refs[n_in + n_out].dtype)

    in_specs = _row_specs(rows, tr) + [pl.BlockSpec(v.shape, lambda i, j: (0, 0)) for v in vecs]
    out_specs = [pl.BlockSpec((tr, w), lambda i, j: (i, j)) for w in out_widths]
    out_shape = [SDS((T, w * nc), dt) for w, dt in zip(out_widths, out_dtypes)]
    if also_transposed:
        out_specs.append(pl.BlockSpec((out_widths[0], tr), lambda i, j: (j, i)))
        out_shape.append(SDS((out_widths[0] * nc, T), out_dtypes[0]))
    return _pcall(body, grid=(T // tr, nc), in_specs=in_specs, out_specs=out_specs, out_shape=out_shape, name=name)(
        *[r[0] for r in rows], *vecs)


def _into_buffer(dest_buf, n_inputs, out_index):
    if dest_buf is None:
        return [], [], {}
    return [dest_buf], [pl.BlockSpec(memory_space=pl.ANY)], {n_inputs: out_index}


def _rowwise_bwd(name, fn, rows, vecs, cts, drow_dtypes, nc=1, tr=TR, add_to_first=None, dest=None):
    T = rows[0][0].shape[0]
    n_r, n_v, n_c = len(rows), len(vecs), len(cts)
    n_add = 0 if add_to_first is None else 1
    keep = [k for k, dt in enumerate(drow_dtypes) if dt is not None]
    members = [] if dest is None else list(dest[1])
    plain = [pos for pos in range(len(keep)) if pos not in members]
    n_dest = 1 if dest is not None else 0
    n_in = n_r + n_v + n_c + n_add

    def body(*refs):
        vals = [r[...].astype(f32) for r in refs[:n_in]]
        outs = refs[n_in + (1 if dest is not None and dest[0] is not None else 0):]
        i, j = pl.program_id(0), pl.program_id(1)
        _, vjp = jax.vjp(fn, *vals[:n_r + n_v])
        grads = vjp(tuple(vals[n_r + n_v:n_r + n_v + n_c]))
        got = []
        for pos, k in enumerate(keep):
            g = grads[k]
            if n_add and pos == 0:
                g = g + vals[n_in - 1]
            got.append(g)
        if dest is not None:
            outs[0][...] = jnp.concatenate([got[pos] for pos in members], axis=1).astype(outs[0].dtype)
        for q, pos in enumerate(plain):
            outs[n_dest + q][...] = got[pos].astype(outs[n_dest + q].dtype)
        vec_outs = outs[n_dest + len(plain):]

        @pl.when((i == 0) & (j == 0))
        def _():
            for q in range(n_v):
                vec_outs[q][...] = jnp.zeros_like(vec_outs[q])

        for q in range(n_v):
            vec_outs[q][...] += grads[n_r + q]

    extra = [] if add_to_first is None else [add_to_first]
    in_specs = (_row_specs(rows, tr) + [pl.BlockSpec(v.shape, lambda i, j: (0, 0)) for v in vecs]
                + _row_specs(cts, tr) + _row_specs(extra, tr))
    args = [r[0] for r in rows] + list(vecs) + [c[0] for c in cts] + [e[0] for e in extra]
    out_specs, out_shape, aliases = [], [], {}
    if dest is not None:
        width = sum(rows[keep[pos]][2] for pos in members)
        col = dest[2]
        b_args, b_specs, aliases = _into_buffer(dest[0], n_in, 0)
        args, in_specs = args + b_args, in_specs + b_specs
        out_specs.append(pl.BlockSpec((tr, width), lambda i, j: (i, col + j)))
        out_shape.append(SDS((T, PROJ_W), bf16))
    out_specs += [pl.BlockSpec((tr, rows[keep[pos]][2]), lambda i, j: (i, j)) for pos in plain]
    out_shape += [SDS((T, rows[keep[pos]][2] * nc), drow_dtypes[keep[pos]]) for pos in plain]
    out_specs += [pl.BlockSpec(v.shape, lambda i, j: (0, 0)) for v in vecs]
    out_shape += [SDS(v.shape, f32) for v in vecs]
    return _pcall(body, grid=(T // tr, nc), in_specs=in_specs, out_specs=out_specs, out_shape=out_shape,
                  input_output_aliases=aliases, name=name)(*args)


def _normmod_fn(x, w, sc, sh):
    y = x * lax.rsqrt(jnp.mean(x * x, axis=-1, keepdims=True) + EPS)
    return ((y * w) * (1.0 + sc) + sh,)


def _resid_fn(x, t, gt):
    return (x + gt * t,)


def _merge_fn(ga, gb, ya, yb):
    return (jax.nn.sigmoid(ga) * ya + jax.nn.sigmoid(gb) * yb,)


def _headmean_raw(x2):
    rows = x2.shape[0]
    return jnp.concatenate([jnp.broadcast_to(jnp.mean(x2[:, LANE * h:LANE * (h + 1)], axis=1, keepdims=True), (rows, LANE))
                            for h in range(x2.shape[1] // LANE)], axis=1)


@jax.custom_vjp
def _headmean(x2):
    return _headmean_raw(x2)


_headmean.defvjp(lambda x2: (_headmean_raw(x2), None), lambda _, dy: (_headmean_raw(dy),))


def _dngate_fn(o, z, w):
    y = o * lax.rsqrt(_headmean(o * o) + EPS)
    return ((y * w) * (z * jax.nn.sigmoid(z)),)


def _conv_taps(x, w_ref, taps, buf):
    w = lambda s: w_ref[taps - 1 - s:taps - s, :]
    row = lax.broadcasted_iota(jnp.int32, (8, x.shape[1]), 0)
    x8 = x[0:8]
    acc, acc8 = x * w(0), x8 * w(0)
    for s in range(1, taps):
        acc = acc + pltpu.roll(x, s, 0) * w(s)
        acc8 = acc8 + jnp.where(row >= s, pltpu.roll(x8, s, 0), 0.0) * w(s)
    buf[...] = acc
    buf[0:8, :] = acc8
    return buf[...]


def _conv_taps_bwd(x, dy, w_ref, dw_ref, taps, buf):
    T = x.shape[0]
    w = lambda s: w_ref[taps - 1 - s:taps - s, :]
    row = lax.broadcasted_iota(jnp.int32, (8, x.shape[1]), 0)
    dy_first, dy_last = dy[0:8], dy[T - 8:T]
    dx, dx_last = dy * w(0), dy_last * w(0)
    dw_ref[taps - 1:taps, :] = jnp.sum(dy * x, axis=0, keepdims=True)
    for s in range(1, taps):
        dx = dx + pltpu.roll(dy, T - s, 0) * w(s)
        dx_last = dx_last + jnp.where(row < 8 - s, pltpu.roll(dy_last, 8 - s, 0), 0.0) * w(s)
        xr = pltpu.roll(x, s, 0)
        wrapped = jnp.sum(jnp.where(row < s, dy_first * xr[0:8], 0.0), axis=0, keepdims=True)
        dw_ref[taps - 1 - s:taps - s, :] = jnp.sum(dy * xr, axis=0, keepdims=True) - wrapped
    buf[...] = dx
    buf[T - 8:T, :] = dx_last
    return buf[...]


def _dn_act(y, normalize):
    s = y * jax.nn.sigmoid(y)
    if normalize:
        s = s * lax.rsqrt(jnp.sum(s * s, axis=-1, keepdims=True) + EPS)
    return s


def _dnconv_fwd(name, proj, cb, w, normalize):
    T = proj.shape[0]

    def body(x_ref, w_ref, o_ref, buf):
        o_ref[...] = _dn_act(_conv_taps(x_ref[...], w_ref, 4, buf), normalize)

    return _pcall(
        body, grid=(8,), in_specs=[pl.BlockSpec((T, LANE), lambda j: (0, cb + j)), pl.BlockSpec((4, LANE), lambda j: (0, cb + j))],
        out_specs=pl.BlockSpec((T, LANE), lambda j: (0, j)), out_shape=SDS((T, 1024), f32), scratch_shapes=[pltpu.VMEM((T, LANE), f32)],
        compiler_params=_cparams(VMEM_BIG), name=name)(proj, w)


def _dnconv_bwd(name, proj, cb, w, dout, normalize, dest_buf):
    T = proj.shape[0]
    b_args, b_specs, aliases = _into_buffer(dest_buf, 3, 0)

    def body(x_ref, w_ref, do_ref, *rest):
        dx_ref, dw_ref, buf, buf2 = rest[len(b_args):]
        x = x_ref[...]
        y = _conv_taps(x, w_ref, 4, buf)
        _, vjp = jax.vjp(functools.partial(_dn_act, normalize=normalize), y)
        (dy,) = vjp(do_ref[...])
        dx_ref[...] = _conv_taps_bwd(x, dy, w_ref, dw_ref, 4, buf2).astype(dx_ref.dtype)

    return _pcall(
        body, grid=(8,),
        in_specs=[pl.BlockSpec((T, LANE), lambda j: (0, cb + j)), pl.BlockSpec((4, LANE), lambda j: (0, cb + j)),
                  pl.BlockSpec((T, LANE), lambda j: (0, j))] + b_specs,
        out_specs=[pl.BlockSpec((T, LANE), lambda j: (0, cb + j)), pl.BlockSpec((4, LANE), lambda j: (0, j))],
        out_shape=[SDS((T, PROJ_W), bf16), SDS((4, 1024), f32)], scratch_shapes=[pltpu.VMEM((T, LANE), f32)] * 2,
        input_output_aliases=aliases, compiler_params=_cparams(VMEM_BIG), name=name)(proj, w, dout, *b_args)


def _ffn_point(a, lin):
    return a * jax.nn.sigmoid(a) * lin


def _ffnact_fwd(name, up, w, b):
    T = up.shape[0]
    nblk = D_FF // LANE

    def body(a_ref, l_ref, w_ref, b_ref, o_ref, buf):
        a = _conv_taps(a_ref[...], w_ref, 3, buf) + b_ref[...]
        o_ref[...] = _ffn_point(a, l_ref[...]).astype(o_ref.dtype)

    return _pcall(
        body, grid=(nblk,),
        in_specs=[pl.BlockSpec((T, LANE), lambda j: (0, j)), pl.BlockSpec((T, LANE), lambda j: (0, nblk + j)),
                  pl.BlockSpec((3, LANE), lambda j: (0, j)), pl.BlockSpec((1, LANE), lambda j: (0, j))],
        out_specs=pl.BlockSpec((T, LANE), lambda j: (0, j)), out_shape=SDS((T, D_FF), bf16), scratch_shapes=[pltpu.VMEM((T, LANE), f32)],
        compiler_params=_cparams(VMEM_BIG), name=name)(up, up, w, b)


def _ffnact_bwd(name, up, w, b, dmid):
    T = up.shape[0]
    nblk = D_FF // LANE

    def body(a_ref, l_ref, w_ref, b_ref, dm_ref, da_ref, dl_ref, dw_ref, db_ref, buf, buf2):
        x = a_ref[...]
        a = _conv_taps(x, w_ref, 3, buf) + b_ref[...]
        _, vjp = jax.vjp(_ffn_point, a, l_ref[...])
        da, dl = vjp(dm_ref[...].astype(f32))
        dl_ref[...] = dl.astype(dl_ref.dtype)
        db_ref[...] = jnp.sum(da, axis=0, keepdims=True)
        da_ref[...] = _conv_taps_bwd(x, da, w_ref, dw_ref, 3, buf2).astype(da_ref.dtype)

    col = lambda r: pl.BlockSpec((r, LANE), lambda j: (0, j))
    return _pcall(
        body, grid=(nblk,),
        in_specs=[col(T), pl.BlockSpec((T, LANE), lambda j: (0, nblk + j)), col(3), col(1), col(T)],
        out_specs=[col(T), col(T), col(3), col(1)],
        out_shape=[SDS((T, D_FF), bf16), SDS((T, D_FF), bf16), SDS((3, D_FF), f32), SDS((1, D_FF), f32)],
        scratch_shapes=[pltpu.VMEM((T, LANE), f32)] * 2, compiler_params=_cparams(VMEM_BIG), name=name)(up, up, w, b, dmid)


def _bmm(a, b, ca, cb, precision=HI):
    return lax.dot_general(a, b, (((ca,), (cb,)), ((0,), (0,))), precision=precision, preferred_element_type=f32)


def _make_bdot(ca, cb):
    def raw(x, y, cx, cy):
        return _bmm(x.astype(bf16), y.astype(bf16), cx, cy, precision=None)

    @jax.custom_vjp
    def f(a, b):
        return raw(a, b, ca, cb)

    def fwd(a, b):
        return raw(a, b, ca, cb), (a, b)

    def bwd(res, dy):
        a, b = res
        if (ca, cb) == (2, 1):
            return raw(dy, b, 2, 2), raw(a, dy, 1, 1)
        if (ca, cb) == (2, 2):
            return raw(dy, b, 2, 1), raw(dy, a, 1, 1)
        return raw(b, dy, 2, 2), raw(a, dy, 2, 1)

    f.defvjp(fwd, bwd)
    return f


_bdot_nn, _bdot_nt, _bdot_tn = _make_bdot(2, 1), _make_bdot(2, 2), _make_bdot(1, 1)


def _pieces(a, n):
    out, r = [], a
    for _ in range(n):
        p = r.astype(bf16)
        out.append(p)
        r = r - p.astype(f32)
    return out


def _bmm_split(x, y, cx, cy, nx=2, ny=2, order=1):
    xs, ys = _pieces(x, nx), _pieces(y, ny)
    acc = None
    for i in reversed(range(nx)):
        for j in reversed(range(ny)):
            if i + j <= order:
                t = _bmm(xs[i], ys[j], cx, cy, precision=None)
                acc = t if acc is None else acc + t
    return acc


@jax.custom_vjp
def _solve_apply(X, r):
    return _bmm_split(X, r, 2, 1)


def _solve_apply_fwd(X, r):
    return _bmm_split(X, r, 2, 1), (X, r)


def _solve_apply_bwd(res, dy):
    X, r = res
    return _bmm_split(dy, r, 2, 2), _bmm_split(X, dy, 1, 1)


_solve_apply.defvjp(_solve_apply_fwd, _solve_apply_bwd)


def _lower_ones(H, C):
    ri = lax.broadcasted_iota(jnp.int32, (H, C, C), 1)
    ci = lax.broadcasted_iota(jnp.int32, (H, C, C), 2)
    return (ri >= ci).astype(f32)


def _cumsum_rows_raw(G):
    return _bmm_split(_lower_ones(G.shape[0], G.shape[1]), G, 2, 1, nx=1, ny=3, order=2)


@jax.custom_vjp
def _cumsum_rows(G):
    return _cumsum_rows_raw(G)


def _cumsum_rows_fwd(G):
    return _cumsum_rows_raw(G), None


def _cumsum_rows_bwd(_, dy):
    return (_bmm_split(_lower_ones(dy.shape[0], dy.shape[1]), dy, 1, 1, nx=1, ny=3, order=2),)


_cumsum_rows.defvjp(_cumsum_rows_fwd, _cumsum_rows_bwd)


def _tri_inverse_raw(L):
    H, C, _ = L.shape
    ri = lax.broadcasted_iota(jnp.int32, (C, C), 0)
    ci = lax.broadcasted_iota(jnp.int32, (C, C), 1)
    eye = jnp.broadcast_to((ri == ci).astype(f32)[None], (H, C, C))
    Dg = jnp.where(((ri >> 3) == (ci >> 3))[None], L, 0.0)
    D2 = _bmm_split(Dg, Dg, 2, 1)
    X = _bmm_split(_bmm_split(eye - Dg, eye + D2, 2, 1), eye + _bmm_split(D2, D2, 2, 1), 2, 1)
    for lg in range(3, C.bit_length() - 1):
        same = (ri >> (lg + 1)) == (ci >> (lg + 1))
        lower_left = same & (((ri >> lg) & 1) == 1) & (((ci >> lg) & 1) == 0)
        X = X - _bmm_split(_bmm_split(X, jnp.where(lower_left[None], L, 0.0), 2, 1), X, 2, 1)
    return X


@jax.custom_vjp
def _tri_inverse(L):
    return _tri_inverse_raw(L)


def _tri_inverse_fwd(L):
    X = _tri_inverse_raw(L)
    return X, X


def _tri_inverse_bwd(X, dX):
    return (-_bmm_split(_bmm_split(X, dX, 1, 1), X, 2, 2),)


_tri_inverse.defvjp(_tri_inverse_fwd, _tri_inverse_bwd)


@jax.custom_vjp
def _tri_inverse_known(L, X):
    return X


def _tri_inverse_known_fwd(L, X):
    return X, X


def _tri_inverse_known_bwd(X, dX):
    return _tri_inverse_bwd(X, dX)[0], jnp.zeros_like(X)


_tri_inverse_known.defvjp(_tri_inverse_known_fwd, _tri_inverse_known_bwd)


DN_NC = 4


def _gdn_chunk(q, k, v, ab, alog, dtb, S, X_known=None, keep_X=False):
    H, C, _ = q.shape
    NC, NH = ab.shape[0], H // ab.shape[0]
    lane = lax.broadcasted_iota(jnp.int32, (H, C, LANE), 2)
    head = lax.broadcasted_iota(jnp.int32, (H, C, LANE), 0) & (NH - 1)
    abb = jnp.concatenate([jnp.broadcast_to(ab[c][None], (NH, C, LANE)) for c in range(NC)], axis=0)
    a = jnp.sum(jnp.where(lane == head, abb, 0.0), axis=2, keepdims=True)
    b = jnp.sum(jnp.where(lane == head + 8, abb, 0.0), axis=2, keepdims=True)
    pick = lax.broadcasted_iota(jnp.int32, (H, 1, LANE), 2) == (lax.broadcasted_iota(jnp.int32, (H, 1, LANE), 0) & (NH - 1))
    al = jnp.sum(jnp.where(pick, alog[None], 0.0), axis=2, keepdims=True)
    db = jnp.sum(jnp.where(pick, dtb[None], 0.0), axis=2, keepdims=True)
    g = -jnp.exp(al) * jax.nn.softplus(a + db)
    beta = jax.nn.sigmoid(b)
    ri = lax.broadcasted_iota(jnp.int32, (C, C), 0)
    ci = lax.broadcasted_iota(jnp.int32, (C, C), 1)
    G = jnp.broadcast_to(g, (H, C, LANE))
    gc = _cumsum_rows(G)
    gi = _cumsum_rows(jnp.broadcast_to(g, (H, C, C)))
    decay = jnp.exp(jnp.where((ri >= ci)[None], gi - jnp.swapaxes(gi, 1, 2), -jnp.inf))
    qs = q * (LANE ** -0.5)
    kb = k * beta
    L = jnp.where((ri > ci)[None], _bdot_nt(kb, k) * decay, 0.0)
    X = _tri_inverse(L) if X_known is None else _tri_inverse_known(L, X_known)
    egc = jnp.exp(gc)
    u = _solve_apply(X, v * beta)
    w = _solve_apply(X, kb * egc)
    qk = _bdot_nt(qs, k) * decay
    g_last = jnp.sum(G, axis=1, keepdims=True)
    k_dec = k * jnp.exp(g_last - gc)
    q_dec = qs * egc
    e_last = jnp.exp(g_last)
    outs = []
    for c in range(NC):
        sl = slice(c * NH, (c + 1) * NH)
        v_new = u[sl] - _bdot_nn(w[sl], S)
        outs.append(_bdot_nn(q_dec[sl], S) + _bdot_nn(qk[sl], v_new))
        S = S * e_last[sl] + _bdot_tn(k_dec[sl], v_new)
    o = jnp.concatenate(outs, axis=0)
    return (o, S, X) if keep_X else (o, S)


def _heads(ref):
    return jnp.stack([ref[DN_C * c:DN_C * (c + 1), LANE * h:LANE * (h + 1)] for c in range(DN_NC) for h in range(8)], axis=0)


def _put_heads(ref, val):
    for c in range(DN_NC):
        for h in range(8):
            ref[DN_C * c:DN_C * (c + 1), LANE * h:LANE * (h + 1)] = val[8 * c + h]


def _chunk_rows(ref):
    return jnp.stack([ref[DN_C * c:DN_C * (c + 1), :] for c in range(DN_NC)], axis=0)


def _hosted_parts(hosted):
    if hosted is None:
        return [], [], [], []
    n = len(hosted["arrays"])
    return list(hosted["arrays"]), [HBM_SPEC] * n, list(hosted["out_shape"]), _dma_sems(hosted["n_sems"])


def _gdn_fwd(name, q, k, v, proj, alog, dtb, hosted=None):
    T = q.shape[0]
    R = DN_C * DN_NC
    N = T // R
    h_args, h_specs, h_shapes, h_sems = _hosted_parts(hosted)
    nh = len(h_args)

    def body(q_ref, k_ref, v_ref, ab_ref, al_ref, dt_ref, *rest):
        h_ins, (o_ref, sall_ref, xinv_ref), h_outs, s_scr, sems = rest[:nh], rest[nh:nh + 3], rest[nh + 3:2 * nh + 3], rest[2 * nh + 3], rest[2 * nh + 4:]
        step = pl.program_id(0)

        @pl.when(step == 0)
        def _():
            s_scr[...] = jnp.zeros_like(s_scr)
            if hosted is not None:
                hosted["start"](h_ins, h_outs, *sems)

        S = s_scr[...]
        sall_ref[...] = S
        o, S_new, X = _gdn_chunk(_heads(q_ref), _heads(k_ref), _heads(v_ref), _chunk_rows(ab_ref), al_ref[...], dt_ref[...], S, keep_X=True)
        _put_heads(o_ref, o)
        s_scr[...] = S_new
        xinv_ref[...] = X

        if hosted is not None:
            @pl.when(step == N - 1)
            def _():
                hosted["finish"](h_ins, h_outs, *sems)

    blk = pl.BlockSpec((R, 8 * LANE), lambda n: (n, 0))
    vec = pl.BlockSpec((1, LANE), lambda n: (0, 0))
    state = pl.BlockSpec((None, 8, LANE, LANE), lambda n: (n, 0, 0, 0))
    xinv = pl.BlockSpec((None, 8 * DN_NC, DN_C, DN_C), lambda n: (n, 0, 0, 0))
    outs = _pcall(
        body, grid=(N,), in_specs=[blk, blk, blk, pl.BlockSpec((R, LANE), lambda n: (n, CB_AB)), vec, vec] + h_specs,
        out_specs=[blk, state, xinv] + h_specs,
        out_shape=[SDS((T, 1024), f32), SDS((N, 8, LANE, LANE), f32), SDS((N, 8 * DN_NC, DN_C, DN_C), f32)] + h_shapes,
        scratch_shapes=[pltpu.VMEM((8, LANE, LANE), f32)] + h_sems, compiler_params=_cparams(VMEM_BIG), name=name)(q, k, v, proj, alog, dtb, *h_args)
    return outs[0], outs[1], outs[2], list(outs[3:])


def _gdn_bwd(name, q, k, v, proj, alog, dtb, sall, xinv, do, hosted=None):
    T = q.shape[0]
    R = DN_C * DN_NC
    N = T // R
    h_args, h_specs, h_shapes, h_sems = _hosted_parts(hosted)
    nh = len(h_args)

    def body(q_ref, k_ref, v_ref, ab_ref, al_ref, dt_ref, s_ref, x_ref, do_ref, *rest):
        h_ins, h_outs, ds_scr, sems = rest[:nh], rest[nh + 6:2 * nh + 6], rest[2 * nh + 6], rest[2 * nh + 7:]
        dq_ref, dk_ref, dv_ref, dab_ref, dal_ref, ddt_ref = rest[nh:nh + 6]
        step = pl.program_id(0)

        @pl.when(step == 0)
        def _():
            ds_scr[...] = jnp.zeros_like(ds_scr)
            dal_ref[...] = jnp.zeros_like(dal_ref)
            ddt_ref[...] = jnp.zeros_like(ddt_ref)
            if hosted is not None:
                hosted["start"](h_ins, h_outs, *sems)

        _, vjp = jax.vjp(functools.partial(_gdn_chunk, X_known=x_ref[...]), _heads(q_ref), _heads(k_ref), _heads(v_ref), _chunk_rows(ab_ref),
                         al_ref[...], dt_ref[...], s_ref[...])
        dq, dk, dv, dab, dal, ddt, dS = vjp((_heads(do_ref), ds_scr[...]))
        _put_heads(dq_ref, dq)
        _put_heads(dk_ref, dk)
        _put_heads(dv_ref, dv)
        ds_scr[...] = dS
        for c in range(DN_NC):
            dab_ref[DN_C * c:DN_C * (c + 1), :] = dab[c]
        dal_ref[...] += dal
        ddt_ref[...] += ddt

        if hosted is not None:
            @pl.when(step == N - 1)
            def _():
                hosted["finish"](h_ins, h_outs, *sems)

    blk = pl.BlockSpec((R, 8 * LANE), lambda n: (N - 1 - n, 0))
    vec = pl.BlockSpec((1, LANE), lambda n: (0, 0))
    state = pl.BlockSpec((None, 8, LANE, LANE), lambda n: (N - 1 - n, 0, 0, 0))
    outs = _pcall(
        body, grid=(N,),
        in_specs=[blk, blk, blk, pl.BlockSpec((R, LANE), lambda n: (N - 1 - n, CB_AB)), vec, vec, state,
                  pl.BlockSpec((None, 8 * DN_NC, DN_C, DN_C), lambda n: (N - 1 - n, 0, 0, 0)), blk] + h_specs,
        out_specs=[blk, blk, blk, pl.BlockSpec((R, LANE), lambda n: (N - 1 - n, 0)), vec, vec] + h_specs,
        out_shape=[SDS((T, 1024), f32)] * 3 + [SDS((T, LANE), f32), SDS((1, LANE), f32), SDS((1, LANE), f32)] + h_shapes,
        scratch_shapes=[pltpu.VMEM((8, LANE, LANE), f32)] + h_sems, compiler_params=_cparams(VMEM_BIG),
        name=name)(q, k, v, proj, alog, dtb, sall, xinv, do, *h_args)
    return tuple(outs[:6]), list(outs[6:])


def _segmean_raw(x2, bd):
    return jnp.concatenate([_dot(x2[:, LANE * j:LANE * (j + 1)], bd, 1, 0) for j in range(x2.shape[1] // LANE)], axis=1)


@jax.custom_vjp
def _segmean(x2, bd):
    return _segmean_raw(x2, bd)


def _segmean_fwd(x2, bd):
    return _segmean_raw(x2, bd), bd


def _segmean_bwd(bd, dy):
    return _segmean_raw(dy, bd), jnp.zeros_like(bd)


_segmean.defvjp(_segmean_fwd, _segmean_bwd)


def _qknorm_fn(x, w, bd):
    return x * lax.rsqrt(_segmean(x * x, bd) + EPS) * w


def _rope_apply(xn, c, s1, s2):
    W = xn.shape[1]
    return xn * c + pltpu.roll(xn, W - 8, 1) * s1 + pltpu.roll(xn, 8, 1) * s2


def _rope_apply_t(d, c, s1, s2):
    W = d.shape[1]
    return d * c + pltpu.roll(d * s1, 8, 1) + pltpu.roll(d * s2, W - 8, 1)


def _rope_tiles(refs, width):
    return [jnp.tile(r[...], (1, width // LANE)) for r in refs]


def _qkprep_fwd(name, proj, wb, width, w, bd, tabs):
    T = proj.shape[0]
    tr = _pick(T, (256, 128))

    def body(x_ref, w_ref, bd_ref, c_ref, s1_ref, s2_ref, o_ref):
        xn = _qknorm_fn(x_ref[...], w_ref[...], bd_ref[...])
        o_ref[...] = _rope_apply(xn, *_rope_tiles((c_ref, s1_ref, s2_ref), width))

    row0 = pl.BlockSpec((tr, width), lambda i: (i, 0))
    tab = pl.BlockSpec((tr, LANE), lambda i: (i, 0))
    full = lambda a: pl.BlockSpec(a.shape, lambda i: (0, 0))
    return _pcall(
        body, grid=(T // tr,), in_specs=[pl.BlockSpec((tr, width), lambda i: (i, wb)), full(w), full(bd), tab, tab, tab],
        out_specs=row0, out_shape=SDS((T, width), f32), name=name)(proj, w, bd, *tabs)


def _qkprep_bwd(name, proj, wb, width, w, bd, tabs, dout, dest_buf="none"):
    T = proj.shape[0]
    tr = _pick(T, (256, 128))
    into = not isinstance(dest_buf, str)
    b_args, b_specs, aliases = _into_buffer(dest_buf, 7, 0) if into else ([], [], {})

    def body(x_ref, w_ref, bd_ref, c_ref, s1_ref, s2_ref, do_ref, *rest):
        dx_ref, dw_ref = rest[len(b_args):]
        i = pl.program_id(0)
        dxn = _rope_apply_t(do_ref[...], *_rope_tiles((c_ref, s1_ref, s2_ref), width))
        bd = bd_ref[...]
        _, vjp = jax.vjp(lambda x, w_: _qknorm_fn(x, w_, bd), x_ref[...], w_ref[...])
        dx, dw = vjp(dxn)
        dx_ref[...] = dx.astype(dx_ref.dtype)

        @pl.when(i == 0)
        def _():
            dw_ref[...] = jnp.zeros_like(dw_ref)

        dw_ref[...] += dw

    row0 = pl.BlockSpec((tr, width), lambda i: (i, 0))
    tab = pl.BlockSpec((tr, LANE), lambda i: (i, 0))
    full = lambda a: pl.BlockSpec(a.shape, lambda i: (0, 0))
    dx_spec = pl.BlockSpec((tr, width), lambda i: (i, wb)) if into else row0
    dx_shape = SDS((T, PROJ_W), bf16) if into else SDS((T, width), bf16)
    return _pcall(
        body, grid=(T // tr,), in_specs=[pl.BlockSpec((tr, width), lambda i: (i, wb)), full(w), full(bd), tab, tab, tab, row0] + b_specs,
        out_specs=[dx_spec, full(w)], out_shape=[dx_shape, SDS(w.shape, f32)], input_output_aliases=aliases, name=name)(
            proj, w, bd, *tabs, dout, *b_args)


def _make_dot16(ca, cb):
    def raw(x, y, cx, cy):
        return _dot(x.astype(bf16), y.astype(bf16), cx, cy, precision=None)

    @jax.custom_vjp
    def f(a, b):
        return raw(a, b, ca, cb)

    def fwd(a, b):
        return raw(a, b, ca, cb), (a, b)

    def bwd(res, dy):
        a, b = res
        if (ca, cb) == (1, 0):
            return raw(dy, b, 1, 1), raw(a, dy, 0, 0)
        return raw(dy, b, 1, 0), raw(dy, a, 0, 0)

    f.defvjp(fwd, bwd)
    return f


_dot16_nn, _dot16_nt = _make_dot16(1, 0), _make_dot16(1, 1)


def _attn_bias():
    qi = jnp.arange(8 * SWA_B) % SWA_B
    kj = jnp.arange(2 * SWA_B)
    rel = qi[:, None] + SWA_B - kj[None, :]
    valid = (rel >= 0) & (rel < SWA_B)
    neg = jnp.float32(-jnp.inf)
    return jnp.stack([jnp.where(valid & (kj[None, :] >= SWA_B), 0.0, neg), jnp.where(valid, 0.0, neg)]).astype(f32)


def _attn_group(qg, kb, vb, sinks, bias, hk):
    R = qg.shape[0]
    s = _dot16_nt(qg, kb) * 0.125 + bias
    head = (lax.broadcasted_iota(jnp.int32, (R, LANE), 0) >> 7) + 8 * hk
    lane = lax.broadcasted_iota(jnp.int32, (R, LANE), 1)
    sink = jnp.sum(jnp.where(lane == head, jnp.broadcast_to(sinks, (R, LANE)), 0.0), axis=1, keepdims=True)
    m = lax.stop_gradient(jnp.maximum(jnp.max(s, axis=1, keepdims=True), sink))
    p = jnp.exp(s - m)
    denom = jnp.sum(p, axis=1, keepdims=True) + jnp.exp(sink - m)
    return _dot16_nn(p / denom, vb)


def _group_rows(ref, hk):
    return jnp.concatenate([ref[:, 64 * (8 * hk + g):64 * (8 * hk + g + 1)] for g in range(8)], axis=0)


def _put_group(ref, hk, val):
    for g in range(8):
        ref[:, 64 * (8 * hk + g):64 * (8 * hk + g + 1)] = val[SWA_B * g:SWA_B * (g + 1)].astype(ref.dtype)


def _attn_specs():
    qs = pl.BlockSpec((SWA_B, 1024), lambda i: (i, 0))
    cur = pl.BlockSpec((SWA_B, LANE), lambda i: (i, 0))
    prev = pl.BlockSpec((SWA_B, LANE), lambda i: (jnp.maximum(i - 1, 0), 0))
    vcur = pl.BlockSpec((SWA_B, LANE), lambda i: (i, CB_SWV))
    vprev = pl.BlockSpec((SWA_B, LANE), lambda i: (jnp.maximum(i - 1, 0), CB_SWV))
    vec = pl.BlockSpec((1, LANE), lambda i: (0, 0))
    bias = pl.BlockSpec((None, 8 * SWA_B, 2 * SWA_B), lambda i: (jnp.minimum(i, 1), 0, 0))
    return qs, cur, prev, vcur, vprev, vec, bias


def _hosted_edge(hosted, which, h_ins, h_outs, sems, at):
    if hosted is None:
        return

    @pl.when(at)
    def _():
        hosted[which](h_ins, h_outs, *sems)


def _attn_fwd(name, sq, sk, proj, sinks, bias, hosted=None):
    T = sq.shape[0]
    nb = T // SWA_B
    h_args, h_specs, h_shapes, h_sems = _hosted_parts(hosted)
    nh = len(h_args)

    def body(q_ref, kp_ref, kc_ref, vp_ref, vc_ref, sk_ref, b_ref, *rest):
        h_ins, o_ref, h_outs, sems = rest[:nh], rest[nh], rest[nh + 1:2 * nh + 1], rest[2 * nh + 1:]
        _hosted_edge(hosted, "start", h_ins, h_outs, sems, pl.program_id(0) == 0)
        sinks_v, bias_v = sk_ref[...], b_ref[...]
        for hk in range(2):
            ks = slice(64 * hk, 64 * hk + 64)
            kb = jnp.concatenate([kp_ref[:, ks], kc_ref[:, ks]], axis=0)
            vb = jnp.concatenate([vp_ref[:, ks], vc_ref[:, ks]], axis=0)
            _put_group(o_ref, hk, _attn_group(_group_rows(q_ref, hk), kb, vb, sinks_v, bias_v, hk))
        _hosted_edge(hosted, "finish", h_ins, h_outs, sems, pl.program_id(0) == nb - 1)

    qs, cur, prev, vcur, vprev, vec, bspec = _attn_specs()
    outs = _pcall(body, grid=(nb,), in_specs=[qs, prev, cur, vprev, vcur, vec, bspec] + h_specs, out_specs=[qs] + h_specs,
                  out_shape=[SDS((T, 1024), bf16)] + h_shapes, scratch_shapes=h_sems, name=name)(sq, sk, sk, proj, proj, sinks, bias, *h_args)
    return outs[0], list(outs[1:])


def _attn_bwd(name, sq, sk, proj, sinks, bias, do, hosted=None):
    T = sq.shape[0]
    nb = T // SWA_B
    h_args, h_specs, h_shapes, h_sems = _hosted_parts(hosted)
    nh = len(h_args)

    def body(q_ref, kp_ref, kc_ref, vp_ref, vc_ref, sk_ref, b_ref, do_ref, *rest):
        h_ins, h_outs, sems = rest[:nh], rest[nh + 6:2 * nh + 6], rest[2 * nh + 6:]
        dq_ref, dkp_ref, dkc_ref, dvp_ref, dvc_ref, dsk_ref = rest[nh:nh + 6]
        _hosted_edge(hosted, "start", h_ins, h_outs, sems, pl.program_id(0) == 0)

        @pl.when(pl.program_id(0) == 0)
        def _():
            dsk_ref[...] = jnp.zeros_like(dsk_ref)

        sinks_v, bias_v = sk_ref[...], b_ref[...]
        dsk = jnp.zeros((1, LANE), f32)
        for hk in range(2):
            ks = slice(64 * hk, 64 * hk + 64)
            kb = jnp.concatenate([kp_ref[:, ks], kc_ref[:, ks]], axis=0)
            vb = jnp.concatenate([vp_ref[:, ks], vc_ref[:, ks]], axis=0)
            _, vjp = jax.vjp(functools.partial(_attn_group, bias=bias_v, hk=hk), _group_rows(q_ref, hk), kb, vb, sinks_v)
            dq, dkb, dvb, ds_ = vjp(_group_rows(do_ref, hk))
            _put_group(dq_ref, hk, dq)
            dsk = dsk + ds_
            dkp_ref[:, ks] = dkb[:SWA_B]
            dkc_ref[:, ks] = dkb[SWA_B:]
            dvp_ref[:, ks] = dvb[:SWA_B]
            dvc_ref[:, ks] = dvb[SWA_B:]
        dsk_ref[...] += dsk
        _hosted_edge(hosted, "finish", h_ins, h_outs, sems, pl.program_id(0) == nb - 1)

    qs, cur, prev, vcur, vprev, vec, bspec = _attn_specs()
    outs = _pcall(
        body, grid=(nb,), in_specs=[qs, prev, cur, vprev, vcur, vec, bspec, qs] + h_specs, out_specs=[qs, cur, cur, cur, cur, vec] + h_specs,
        out_shape=[SDS((T, 1024), f32)] + [SDS((T, LANE), f32)] * 4 + [SDS((1, LANE), f32)] + h_shapes, scratch_shapes=h_sems,
        name=name)(sq, sk, sk, proj, proj, sinks, bias, do, *h_args)
    return tuple(outs[:6]), list(outs[6:])


def _shift_add(name, cur, prev, out_dtype):
    T = cur.shape[0]

    def body(c_ref, p_ref, o_ref):
        o_ref[0:T - SWA_B, :] = (c_ref[0:T - SWA_B, :] + p_ref[SWA_B:T, :]).astype(o_ref.dtype)
        o_ref[T - SWA_B:T, :] = c_ref[T - SWA_B:T, :].astype(o_ref.dtype)

    return _pcall(body, out_shape=SDS((T, LANE), out_dtype), name=name)(cur, prev)


def _loss(name, y, tgt):
    T = y.shape[0]

    def body(y_ref, t_ref, l_ref, dy_ref):
        @pl.when(pl.program_id(0) == 0)
        def _():
            l_ref[...] = jnp.zeros_like(l_ref)

        d = y_ref[...] - t_ref[...]
        l_ref[...] += jnp.sum(d * d) * (0.5 / D)
        dy_ref[...] = d * (1.0 / D)

    row = pl.BlockSpec((TR, D), lambda i: (i, 0))
    return _pcall(body, grid=(T // TR,), in_specs=[row, row], out_specs=[pl.BlockSpec((8, LANE), lambda i: (0, 0)), row],
                  out_shape=[SDS((8, LANE), f32), SDS((T, D), f32)], name=name)(y, tgt)


def _adamw(name, w, g, m, v):
    shape = w.shape
    R, C = shape[-2], shape[-1]
    tr = _pick(R, (128, 64, 16, 8))
    bc1 = np.float32(1.0 - ADAM_B1 ** ADAM_STEP)
    bc2 = np.float32(1.0 - ADAM_B2 ** ADAM_STEP)

    def body(w_ref, g_ref, m_ref, v_ref, d_ref, mo_ref, vo_ref):
        g_ = g_ref[...]
        m_ = ADAM_B1 * m_ref[...] + (1.0 - ADAM_B1) * g_
        v_ = ADAM_B2 * v_ref[...] + (1.0 - ADAM_B2) * (g_ * g_)
        d_ref[...] = -ADAM_LR * ((m_ / bc1) / (jnp.sqrt(v_ / bc2) + ADAM_EPS) + ADAM_WD * w_ref[...])
        mo_ref[...] = m_
        vo_ref[...] = v_

    if len(shape) == 3:
        grid, blk = (shape[0], R // tr), pl.BlockSpec((None, tr, C), lambda l, i: (l, i, 0))
    else:
        grid, blk = (R // tr,), pl.BlockSpec((tr, C), lambda i: (i, 0))
    return list(_pcall(body, grid=grid, in_specs=[blk] * 4, out_specs=[blk] * 3, out_shape=[SDS(shape, f32)] * 3,
                       compiler_params=_cparams(VMEM_BIG), name=name)(w, g, m, v))


def _silu_rows(name, x):
    def body(x_ref, o_ref):
        t = x_ref[...]
        o_ref[...] = (t * jax.nn.sigmoid(t)).astype(o_ref.dtype)

    return _pcall(body, out_shape=SDS(x.shape, bf16), name=name)(x)


def _sum_leading(name, x):
    n = x.shape[0]

    def body(x_ref, o_ref):
        acc = x_ref[0]
        for k in range(1, n):
            acc = acc + x_ref[k]
        o_ref[...] = acc

    tr = x.shape[1] if x.size * 4 <= 8 * 2 ** 20 else _pick(x.shape[1], (COMM_TR, 8))
    return _pcall(body, grid=(x.shape[1] // tr,), in_specs=[pl.BlockSpec((n, tr, x.shape[2]), lambda i: (0, i, 0))],
                  out_specs=pl.BlockSpec((tr, x.shape[2]), lambda i: (i, 0)), out_shape=SDS(x.shape[1:], x.dtype), name=name)(x)


HBM_SPEC = pl.BlockSpec(memory_space=pltpu.HBM)


def _position():
    x, y, c = lax.axis_index("x"), lax.axis_index("y"), lax.axis_index("c")
    return x, y, c, [(1 - x, y), (x, 1 - y), (1 - x, 1 - y)]


def _remote(src, dst, send_sems, recv_sems, k, to):
    return pltpu.make_async_remote_copy(src_ref=src, dst_ref=dst, send_sem=send_sems.at[k], recv_sem=recv_sems.at[k],
                                        device_id=to, device_id_type=MESH)


def _allgather_all(name, buf):
    r, W = buf.shape

    def body(in_ref, out_ref, send_sems, recv_sems, local_sem):
        x, y, c, _ = _position()
        me = 4 * x + 2 * y + c
        mine = pltpu.make_async_copy(in_ref, out_ref.at[me], local_sem)
        mine.start()
        peers = []
        for mk in range(1, 8):
            mx, my, mc = (mk >> 2) & 1, (mk >> 1) & 1, mk & 1
            px = 1 - x if mx else x
            py = 1 - y if my else y
            pc = 1 - c if mc else c
            peers.append((px, py, pc))
        sends = [_remote(in_ref, out_ref.at[me], send_sems, recv_sems, k, p) for k, p in enumerate(peers)]
        for cp in sends:
            cp.start()
        for k, (px, py, pc) in enumerate(peers):
            slot = out_ref.at[4 * px + 2 * py + pc]
            _remote(slot, slot, send_sems, recv_sems, k, (px, py, pc)).wait_recv()
        for cp in sends:
            cp.wait_send()
        mine.wait()

    return _pcall(body, in_specs=[HBM_SPEC], out_specs=HBM_SPEC, out_shape=SDS((8, r, W), buf.dtype),
                  scratch_shapes=[pltpu.SemaphoreType.DMA((7,)), pltpu.SemaphoreType.DMA((7,)), pltpu.SemaphoreType.DMA], name=name)(buf)


def _dma_sems(n):
    return [pltpu.SemaphoreType.DMA((n,)), pltpu.SemaphoreType.DMA((n,))]


def _gather_chips(name, shards):
    n = len(shards)

    def body(*refs):
        ins, outs = refs[:n], refs[n:2 * n]
        send_sems, recv_sems = refs[2 * n:]
        x, y, c, chips = _position()
        me = 2 * x + y
        sib = (x, y, 1 - c)
        sends, halves = [], []
        for i in range(n):
            rh = ins[i].shape[0] // 2
            halves.append((pl.ds(pl.multiple_of(c * rh, 16), rh), pl.ds(pl.multiple_of((1 - c) * rh, 16), rh)))
        for i in range(n):
            for j, (cx, cy) in enumerate(chips):
                cp = _remote(ins[i].at[halves[i][0]], outs[i].at[me, halves[i][0]], send_sems, recv_sems, 6 * i + j, (cx, cy, c))
                cp.start()
                sends.append(cp)
        for j, (cx, cy) in enumerate(chips):
            for i in range(n):
                rows = outs[i].at[2 * cx + cy, halves[i][0]]
                _remote(rows, rows, send_sems, recv_sems, 6 * i + j, (cx, cy, c)).wait_recv()
                cp = _remote(rows, rows, send_sems, recv_sems, 6 * i + 3 + j, sib)
                cp.start()
                sends.append(cp)
        for j, (cx, cy) in enumerate(chips):
            for i in range(n):
                rows = outs[i].at[2 * cx + cy, halves[i][1]]
                _remote(rows, rows, send_sems, recv_sems, 6 * i + 3 + j, sib).wait_recv()
        for cp in sends:
            cp.wait_send()

    return _pcall(body, in_specs=[HBM_SPEC] * n, out_specs=[HBM_SPEC] * n, out_shape=[SDS((4,) + s.shape, s.dtype) for s in shards],
                  scratch_shapes=_dma_sems(6 * n), name=name)(*shards)


def _swap_halves_multi(name, slots):
    n = len(slots)

    def body(*refs):
        ins, outs = refs[:n], refs[n:2 * n]
        send_sems, recv_sems = refs[2 * n:]
        x, y, c, _ = _position()
        cps = []
        for i in range(n):
            rh = ins[i].shape[1] // 2
            ohalf = pl.ds(pl.multiple_of((1 - c) * rh, 8), rh)
            cp = _remote(ins[i].at[:, ohalf, :], outs[i], send_sems, recv_sems, i, (x, y, 1 - c))
            cp.start()
            cps.append(cp)
        for cp in cps:
            cp.wait()

    return _pcall(body, in_specs=[HBM_SPEC] * n, out_specs=[HBM_SPEC] * n,
                  out_shape=[SDS((4, s.shape[1] // 2, s.shape[2]), s.dtype) for s in slots], scratch_shapes=_dma_sems(n), name=name)(*slots)


def _pair_add(name, g4, b1, c):
    _, R, W = g4.shape
    tr = _pick(R // 2, (256, 128, 32, 16))
    nblk = (R // 2) // tr

    def body(c_ref, g_ref, b_ref, o_ref):
        o_ref[...] = (g_ref[...] + b_ref[...]).astype(o_ref.dtype)

    grid_spec = pltpu.PrefetchScalarGridSpec(
        num_scalar_prefetch=1, grid=(4, nblk),
        in_specs=[pl.BlockSpec((None, tr, W), lambda s, i, c_ref: (s, c_ref[0] * nblk + i, 0)),
                  pl.BlockSpec((None, tr, W), lambda s, i, c_ref: (s, i, 0))],
        out_specs=pl.BlockSpec((None, tr, W), lambda s, i, c_ref: (s, i, 0)))
    return _pcall(body, grid_spec=grid_spec, out_shape=SDS((4, R // 2, W), bf16), name=name)(c.reshape(1), g4, b1)


def _scatter_chips_multi(name, ps):
    n = len(ps)

    def body(*refs):
        ins, outs = refs[:n], refs[n:2 * n]
        send_sems, recv_sems = refs[2 * n:]
        x, y, c, chips = _position()
        me = 2 * x + y
        sends = []
        for i in range(n):
            for j, (cx, cy) in enumerate(chips):
                cp = _remote(ins[i].at[2 * cx + cy], outs[i].at[me], send_sems, recv_sems, 3 * i + j, (cx, cy, c))
                cp.start()
                sends.append(cp)
        for i in range(n):
            for j, (cx, cy) in enumerate(chips):
                slot = outs[i].at[2 * cx + cy]
                _remote(slot, slot, send_sems, recv_sems, 3 * i + j, (cx, cy, c)).wait_recv()
        for cp in sends:
            cp.wait_send()

    return _pcall(body, in_specs=[HBM_SPEC] * n, out_specs=[HBM_SPEC] * n, out_shape=[SDS(p.shape, p.dtype) for p in ps],
                  scratch_shapes=_dma_sems(3 * n), name=name)(*ps)


def _sum_chips(name, p4, b2, chip, c):
    _, Rh, W = p4.shape
    tr = _pick(Rh, (256, 128, 32, 16))
    nblk = Rh // tr

    def body(m_ref, c_ref, own_ref, r1_ref, r2_ref, r3_ref, o_ref):
        o_ref[...] = ((own_ref[...].astype(f32) + r1_ref[...].astype(f32)) + r2_ref[...].astype(f32)) + r3_ref[...].astype(f32)

    other = lambda k: pl.BlockSpec((None, tr, W), lambda i, m_ref, c_ref: (m_ref[0] ^ k, i, 0))
    grid_spec = pltpu.PrefetchScalarGridSpec(
        num_scalar_prefetch=2, grid=(nblk,),
        in_specs=[pl.BlockSpec((None, tr, W), lambda i, m_ref, c_ref: (m_ref[0], i, 0)), other(1), other(2), other(3)],
        out_specs=pl.BlockSpec((tr, W), lambda i, m_ref, c_ref: (c_ref[0] * nblk + i, 0)))
    return _pcall(body, grid_spec=grid_spec, out_shape=SDS((2 * Rh, W), f32), name=name)(chip.reshape(1), c.reshape(1), p4, b2, b2, b2)


def _join_halves(name, fulls):
    n = len(fulls)

    def body(*refs):
        outs = refs[n:2 * n]
        send_sems, recv_sems = refs[2 * n:]
        x, y, c, _ = _position()
        cps = []
        for i in range(n):
            rh = outs[i].shape[0] // 2
            mine = outs[i].at[pl.ds(pl.multiple_of(c * rh, 8), rh)]
            theirs = outs[i].at[pl.ds(pl.multiple_of((1 - c) * rh, 8), rh)]
            cp = _remote(mine, mine, send_sems, recv_sems, i, (x, y, 1 - c))
            cp.start()
            cps.append((cp, _remote(theirs, theirs, send_sems, recv_sems, i, (x, y, 1 - c))))
        for cp, back in cps:
            back.wait_recv()
            cp.wait_send()

    return _pcall(body, in_specs=[HBM_SPEC] * n, out_specs=[HBM_SPEC] * n, out_shape=[SDS(r.shape, r.dtype) for r in fulls],
                  input_output_aliases={i: i for i in range(n)}, scratch_shapes=_dma_sems(n), name=name)(*fulls)


def _hosted_gather(shards):
    n = len(shards)

    def half(ref_rows, c):
        rh = ref_rows // 2
        return pl.ds(pl.multiple_of(c * rh, 16), rh)

    def start(ins, outs, send_sems, recv_sems):
        x, y, c, chips = _position()
        me = 2 * x + y
        for i in range(n):
            rows = half(ins[i].shape[0], c)
            for j, (cx, cy) in enumerate(chips):
                _remote(ins[i].at[rows], outs[i].at[me, rows], send_sems, recv_sems, 3 * i + j, (cx, cy, c)).start()

    def finish(ins, outs, send_sems, recv_sems):
        x, y, c, chips = _position()
        me = 2 * x + y
        for i in range(n):
            rows = half(ins[i].shape[0], c)
            for j, (cx, cy) in enumerate(chips):
                _remote(ins[i].at[rows], outs[i].at[2 * cx + cy, rows], send_sems, recv_sems, 3 * i + j, (cx, cy, c)).wait_recv()
        for i in range(n):
            rows = half(ins[i].shape[0], c)
            for j, (cx, cy) in enumerate(chips):
                _remote(ins[i].at[rows], outs[i].at[me, rows], send_sems, recv_sems, 3 * i + j, (cx, cy, c)).wait_send()

    return {"arrays": shards, "out_shape": [SDS((4,) + s.shape, s.dtype) for s in shards], "n_sems": 3 * n, "start": start, "finish": finish}


def _gather_forward(name, gathered):
    n = len(gathered)

    def body(*refs):
        outs = refs[n:2 * n]
        send_sems, recv_sems = refs[2 * n:]
        x, y, c, chips = _position()
        sib = (x, y, 1 - c)
        sends = []
        for i in range(n):
            rh = outs[i].shape[1] // 2
            mine = pl.ds(pl.multiple_of(c * rh, 16), rh)
            for j, (cx, cy) in enumerate(chips):
                rows = outs[i].at[2 * cx + cy, mine]
                cp = _remote(rows, rows, send_sems, recv_sems, 3 * i + j, sib)
                cp.start()
                sends.append(cp)
        for i in range(n):
            rh = outs[i].shape[1] // 2
            theirs = pl.ds(pl.multiple_of((1 - c) * rh, 16), rh)
            for j, (cx, cy) in enumerate(chips):
                rows = outs[i].at[2 * cx + cy, theirs]
                _remote(rows, rows, send_sems, recv_sems, 3 * i + j, sib).wait_recv()
        for cp in sends:
            cp.wait_send()

    return _pcall(body, in_specs=[HBM_SPEC] * n, out_specs=[HBM_SPEC] * n, out_shape=[SDS(g.shape, g.dtype) for g in gathered],
                  input_output_aliases={i: i for i in range(n)}, scratch_shapes=_dma_sems(3 * n), name=name)(*gathered)


def _hosted_scatter(ps):
    n = len(ps)

    def start(ins, outs, send_sems, recv_sems):
        x, y, c, chips = _position()
        me = 2 * x + y
        for i in range(n):
            for j, (cx, cy) in enumerate(chips):
                _remote(ins[i].at[2 * cx + cy], outs[i].at[me], send_sems, recv_sems, 3 * i + j, (cx, cy, c)).start()

    def finish(ins, outs, send_sems, recv_sems):
        x, y, c, chips = _position()
        me = 2 * x + y
        for i in range(n):
            for j, (cx, cy) in enumerate(chips):
                slot = outs[i].at[2 * cx + cy]
                _remote(slot, slot, send_sems, recv_sems, 3 * i + j, (cx, cy, c)).wait_recv()
        for i in range(n):
            for j, (cx, cy) in enumerate(chips):
                _remote(ins[i].at[2 * cx + cy], outs[i].at[me], send_sems, recv_sems, 3 * i + j, (cx, cy, c)).wait_send()

    return {"arrays": ps, "out_shape": [SDS(p.shape, p.dtype) for p in ps], "n_sems": 3 * n, "start": start, "finish": finish}


def _hosted_swap(slots):
    n = len(slots)

    def copies(ins, outs, send_sems, recv_sems):
        x, y, c, _ = _position()
        cps = []
        for i in range(n):
            rh = ins[i].shape[1] // 2
            ohalf = pl.ds(pl.multiple_of((1 - c) * rh, 8), rh)
            cps.append(_remote(ins[i].at[:, ohalf, :], outs[i], send_sems, recv_sems, i, (x, y, 1 - c)))
        return cps

    def start(ins, outs, send_sems, recv_sems):
        for cp in copies(ins, outs, send_sems, recv_sems):
            cp.start()

    def finish(ins, outs, send_sems, recv_sems):
        for cp in copies(ins, outs, send_sems, recv_sems):
            cp.wait()

    return {"arrays": slots, "out_shape": [SDS((4, s.shape[1] // 2, s.shape[2]), s.dtype) for s in slots], "n_sems": n,
            "start": start, "finish": finish}


def _rs_begin(slots, c):
    b1 = _swap_halves_multi("rs_swap", slots)
    return [_pair_add("rs_pair_%d" % i, g, b, c) for i, (g, b) in enumerate(zip(slots, b1))]


def _rs_end(ps, b2, c, chip):
    return _join_halves("rs_join", [_sum_chips("rs_sum_%d" % i, p, b, chip, c) for i, (p, b) in enumerate(zip(ps, b2))])


def _reduce_scatter_multi(slots, c, chip):
    ps = _rs_begin(slots, c)
    return _rs_end(ps, _scatter_chips_multi("rs_scatter", ps), c, chip)


_BIG = ("w_in", "w_dn_out", "w_swa_out", "w_o", "w_up", "w_down")


_W_IN_PIECES = ((0, 3072, 0), (3072, 4096, 3072), (5392, 6416, 4096), (6416, 7440, 5120), (4112, 5136, 6144), (4096, 4112, 7168),
                (5136, 5264, 7296), (5264, 5392, 7424))
_W_IN_SHARD = IN_TOTAL // 4


def _w_in_from_slots(g):
    parts, at = [], 0
    for lo, hi, dst in _W_IN_PIECES:
        if dst > at:
            parts.append(jnp.zeros((g.shape[1], dst - at), g.dtype))
        for s in range(4):
            a, b = max(lo, s * _W_IN_SHARD), min(hi, (s + 1) * _W_IN_SHARD)
            if a < b:
                parts.append(g[s][:, a - s * _W_IN_SHARD:b - s * _W_IN_SHARD])
        at = dst + hi - lo
    parts.append(jnp.zeros((g.shape[1], PROJ_W - at), g.dtype))
    return jnp.concatenate(parts, axis=1)


def _w_in_to_slots(gw):
    slots = []
    for s in range(4):
        parts = []
        for lo, hi, dst in sorted(_W_IN_PIECES):
            a, b = max(lo, s * _W_IN_SHARD), min(hi, (s + 1) * _W_IN_SHARD)
            if a < b:
                parts.append(gw[:, dst + a - lo:dst + b - lo])
        slots.append(jnp.concatenate(parts, axis=1))
    return jnp.stack(slots)


def _assemble_mixer(gs):
    rows = lambda g: g.reshape(4 * g.shape[1], g.shape[2])
    return {"w_in": _w_in_from_slots(gs[0]), "w_dn_out": rows(gs[1]), "w_swa_out": rows(gs[2]), "w_o": rows(gs[3])}


def _grad_slots(gw):
    rows = lambda g: g.reshape(4, g.shape[0] // 4, g.shape[1])
    return [_w_in_to_slots(gw["w_in"]), rows(gw["w_dn_out"]), rows(gw["w_swa_out"]), rows(gw["w_o"]), gw["w_up"], rows(gw["w_down"])]


def _pad_lanes(v, n=LANE):
    return jnp.pad(v, (0, n - v.shape[0])).reshape(1, n)


def _layer_consts(P):
    K = {}
    K["norm_mix"] = P["norm_mix"].reshape(1, D)
    K["norm_ffn"] = P["norm_ffn"].reshape(1, D)
    K["alog"] = _pad_lanes(P["dn_a_log"])
    K["dtb"] = _pad_lanes(P["dn_dt_bias"])
    K["dn_norm"] = jnp.tile(P["dn_norm"], 8).reshape(1, D)
    K["qn"] = jnp.tile(P["swa_q_norm"], 16).reshape(1, D)
    K["kn"] = jnp.tile(P["swa_k_norm"], 2).reshape(1, LANE)
    K["sinks"] = _pad_lanes(P["swa_sinks"])
    K["ffn_b"] = P["ffn_conv_b"].reshape(1, D_FF)
    return K


def _layer_fwd(x, mod, W, K, tabs, bd, hosted=None, late=None):
    late = late or {}
    sh1, sc1, gt1, sh2, sc2, gt2 = mod
    S = {"x": x}
    h1, h1t = _rowwise_fwd("normmod1_fwd", _normmod_fn, [(x, 0, D)], [K["norm_mix"], sc1, sh1], [D], [bf16], also_transposed=True)
    if "proj" in late:
        proj, arrived = _matmul("proj_fwd", h1, W["w_in"], "nn", f32, hosted=late["proj"][0])
        W.update(late["proj"][1](arrived))
    else:
        proj = _matmul("proj_fwd", h1, W["w_in"], "nn", f32)
    qn = _dnconv_fwd("dnconv_q_fwd", proj, CB_Q, W["dn_conv"], True)
    kn = _dnconv_fwd("dnconv_k_fwd", proj, CB_K, W["dn_conv"], True)
    vc = _dnconv_fwd("dnconv_v_fwd", proj, CB_V, W["dn_conv"], False)
    o, sall, xinv, hosted_out = _gdn_fwd("gdn_fwd", qn, kn, vc, proj, K["alog"], K["dtb"], hosted=hosted)
    (on,) = _rowwise_fwd("dngate_fwd", _dngate_fn, [(o, 0, D), (proj, WB_Z, D)], [K["dn_norm"]], [D], [bf16])
    ya = _matmul("dnout_fwd", on, W["w_dn_out"], "nn", f32)
    sq = _qkprep_fwd("qprep_fwd", proj, WB_SWQ, D, K["qn"], bd[0], tabs[0])
    sk = _qkprep_fwd("kprep_fwd", proj, CB_SWK, LANE, K["kn"], bd[1], tabs[1])
    attn, arrived = _attn_fwd("attn_fwd", sq, sk, proj, K["sinks"], K["attn_bias"], hosted=late["attn"][0] if "attn" in late else None)
    if "attn" in late:
        W.update(late["attn"][1](arrived))
    yb = _matmul("swaout_fwd", attn, W["w_swa_out"], "nn", f32)
    (merged,) = _rowwise_fwd("merge_fwd", _merge_fn, [(proj, WB_GA, D), (proj, WB_GB, D), (ya, 0, D), (yb, 0, D)], [], [D], [bf16])
    t1, x1 = _matmul("wo_fwd", merged, W["w_o"], "nn", f32, resid=(x, gt1))
    h2, h2t = _rowwise_fwd("normmod2_fwd", _normmod_fn, [(x1, 0, D)], [K["norm_ffn"], sc2, sh2], [D], [bf16], also_transposed=True)
    up = _matmul("up_fwd", h2, W["w_up"], "nn", f32, b_slots=True)
    mid = _ffnact_fwd("ffnact_fwd", up, W["ffn_conv"], K["ffn_b"])
    t2, x2 = _matmul("down_fwd", mid, W["w_down"], "nn", f32, resid=(x1, gt2))
    S.update(h1t=h1t, h2t=h2t, proj=proj, qn=qn, kn=kn, vc=vc, o=o, sall=sall, xinv=xinv, on=on, ya=ya, sq=sq, sk=sk, attn=attn, yb=yb,
             merged=merged, t1=t1, x1=x1, h2=h2, up=up, mid=mid, t2=t2)
    return x2, S, hosted_out


def _layer_bwd(dx2, S, mod, W, K, tabs, bd, carry=None, early=None):
    sh1, sc1, gt1, sh2, sc2, gt2 = mod
    x, x1, proj, up = S["x"], S["x1"], S["proj"], S["up"]
    T = x.shape[0]
    gw, gs = {}, {}
    dt2, dgt2 = _rowwise_bwd("resid2_bwd", _resid_fn, [(x1, 0, D), (S["t2"], 0, D)], [gt2], [(dx2, 0, D)], [None, bf16])
    dmid = _matmul("down_bwd_x", dt2, W["w_down"], "nt", bf16)
    gw["w_down"] = _matmul("down_bwd_w", S["mid"], dt2, "tn", f32)
    dact, dlin, gw["ffn_conv"], dffn_b = _ffnact_bwd("ffnact_bwd", up, W["ffn_conv"], K["ffn_b"], dmid)
    dup = (dact, dlin)
    dh2 = _matmul("up_bwd_x", dup, W["w_up"], "nt", f32, b_slots=True)
    if carry is None:
        gw["w_up"] = _matmul("up_bwd_w", S["h2t"], dup, "nn", f32, out_slots=4)
        pair_sums = hosted = None
    else:
        gw["w_up"], b1 = _matmul("up_bwd_w", S["h2t"], dup, "nn", f32, out_slots=4, hosted=_hosted_swap(carry[0]))
        pair_sums = [_pair_add("rs_pair_%d" % i, g, b, carry[1]) for i, (g, b) in enumerate(zip(carry[0], b1))]
        hosted = _hosted_scatter(pair_sums)
    dx1, dnorm_ffn, dsc2, dsh2 = _rowwise_bwd("normmod2_bwd", _normmod_fn, [(x1, 0, D)], [K["norm_ffn"], sc2, sh2], [(dh2, 0, D)], [f32],
                                              add_to_first=(dx2, 0, D))
    early_out = None
    if early is not None:
        ffn_slots = [gw["w_up"], gw["w_down"].reshape(4, D_FF // 4, D)]
        ffn_ps = [_pair_add("rs_pair_ffn_%d" % i, g, b, early[0]) for i, (g, b) in enumerate(zip(ffn_slots, _swap_halves_multi("rs_swap_ffn", ffn_slots)))]
    dt1, dgt1 = _rowwise_bwd("resid1_bwd", _resid_fn, [(x, 0, D), (S["t1"], 0, D)], [gt1], [(dx1, 0, D)], [None, bf16])
    dmerged = _matmul("wo_bwd_x", dt1, W["w_o"], "nt", f32)
    gw["w_o"] = _matmul("wo_bwd_w", S["merged"], dt1, "tn", f32)
    dproj, dya, dyb = _rowwise_bwd("merge_bwd", _merge_fn, [(proj, WB_GA, D), (proj, WB_GB, D), (S["ya"], 0, D), (S["yb"], 0, D)], [],
                                   [(dmerged, 0, D)], [bf16, bf16, bf16, bf16], dest=(None, (0, 1), WB_GA // 2))
    don = _matmul("dnout_bwd_x", dya, W["w_dn_out"], "nt", f32)
    gw["w_dn_out"] = _matmul("dnout_bwd_w", S["on"], dya, "tn", f32)
    dproj, do, ddn_norm = _rowwise_bwd("dngate_bwd", _dngate_fn, [(S["o"], 0, D), (proj, WB_Z, D)], [K["dn_norm"]], [(don, 0, D)],
                                       [f32, bf16], dest=(dproj, (1,), WB_Z))
    (dqn, dkn, dvc, dab, dalog, ddtb), hosted_out = _gdn_bwd("gdn_bwd", S["qn"], S["kn"], S["vc"], proj, K["alog"], K["dtb"], S["sall"], S["xinv"], do,
                                                            hosted=hosted)
    dproj, dwq = _dnconv_bwd("dnconv_q_bwd", proj, CB_Q, W["dn_conv"], dqn, True, dproj)
    dproj, dwk = _dnconv_bwd("dnconv_k_bwd", proj, CB_K, W["dn_conv"], dkn, True, dproj)
    dproj, dwv = _dnconv_bwd("dnconv_v_bwd", proj, CB_V, W["dn_conv"], dvc, False, dproj)
    gw["dn_conv"] = jnp.concatenate([dwq, dwk, dwv], axis=1)
    dattn = _matmul("swaout_bwd_x", dyb, W["w_swa_out"], "nt", f32)
    gw["w_swa_out"] = _matmul("swaout_bwd_w", S["attn"], dyb, "tn", f32)
    (dsq, dkp, dkc, dvp, dvc_, dsinks), ffn_b2 = _attn_bwd("attn_bwd", S["sq"], S["sk"], proj, K["sinks"], K["attn_bias"], dattn,
                                                          hosted=None if early is None else _hosted_scatter(ffn_ps))
    if early is not None:
        early_out = _rs_end(ffn_ps, ffn_b2, early[0], early[1])
    dsk = _shift_add("attn_dk_join", dkc, dkp, f32)
    dswv = _shift_add("attn_dv_join", dvc_, dvp, bf16)
    dproj, dqn_w = _qkprep_bwd("qprep_bwd", proj, WB_SWQ, D, K["qn"], bd[0], tabs[0], dsq, dest_buf=dproj)
    dswk, dkn_w = _qkprep_bwd("kprep_bwd", proj, CB_SWK, LANE, K["kn"], bd[1], tabs[1], dsk)
    tail = jnp.concatenate([dab.astype(bf16), dswk, dswv, jnp.zeros((T, LANE), bf16)], axis=1)
    dproj = lax.dynamic_update_slice(dproj, tail, (0, CB_AB * LANE))
    dh1 = _matmul("proj_bwd_x", dproj, W["w_in"], "nt", f32)
    gw["w_in"] = _matmul("proj_bwd_w", S["h1t"], dproj, "nn", f32)
    dx, dnorm_mix, dsc1, dsh1 = _rowwise_bwd("normmod1_bwd", _normmod_fn, [(x, 0, D)], [K["norm_mix"], sc1, sh1], [(dh1, 0, D)], [f32],
                                             add_to_first=(dx1, 0, D))
    gs = {"norm_mix": dnorm_mix[0], "dn_a_log": dalog[0, :8], "dn_dt_bias": ddtb[0, :8], "dn_norm": ddn_norm.reshape(8, LANE).sum(0),
          "swa_q_norm": dqn_w.reshape(16, 64).sum(0), "swa_k_norm": dkn_w.reshape(2, 64).sum(0), "swa_sinks": dsinks[0, :16],
          "norm_ffn": dnorm_ffn[0], "ffn_conv_b": dffn_b[0]}
    dmod = jnp.concatenate([dsh1, dsc1, dgt1, dsh2, dsc2, dgt2], axis=1)
    return dx, gw, gs, dmod, (pair_sums, hosted_out), early_out


def _rope_tables(pos):
    T = pos.shape[0]
    half = 8
    inv = jnp.power(ROPE_THETA, -jnp.arange(half, dtype=f32) / half)
    ang = pos.astype(f32)[:, None] * inv
    cos, sin = jnp.cos(ang), jnp.sin(ang)
    z8, z48, o48 = jnp.zeros((T, 8), f32), jnp.zeros((T, 48), f32), jnp.ones((T, 48), f32)
    c64 = jnp.concatenate([cos, cos, o48], axis=1)
    s1 = jnp.concatenate([-sin, z8, z48], axis=1)
    s2 = jnp.concatenate([z8, sin, z48], axis=1)
    return tuple(jnp.tile(t, (1, 2)) for t in (c64, s1, s2))


_SMALL = (("norm_mix", D), ("dn_a_log", 8), ("dn_dt_bias", 8), ("dn_norm", 128), ("swa_q_norm", 64), ("swa_k_norm", 64),
          ("swa_sinks", 16), ("norm_ffn", D), ("ffn_conv_b", D_FF), ("b_ada", 6 * D))
_CONV = (("dn_conv", 4 * 3072), ("ffn_conv", 3 * D_FF))
_CONV_SHARD = (("dn_conv", 4 * 768), ("ffn_conv", 3 * 704))


def _pack_small(vals, spec):
    flat = jnp.concatenate([vals[nm].reshape(-1) for nm, _ in spec])
    rows = -(-flat.shape[0] // (8 * LANE)) * 8
    return jnp.pad(flat, (0, rows * LANE - flat.shape[0])).reshape(rows, LANE)


def _unpack_small(buf, spec):
    flat = buf.reshape(-1)
    out, off = {}, 0
    for nm, n in spec:
        out[nm] = flat[off:off + DEPTH * n].reshape(DEPTH, n)
        off += DEPTH * n
    return out


def kernel(x, c, positions, w_ada, b_ada, norm_mix, w_in, dn_conv, dn_a_log, dn_dt_bias, dn_norm, w_dn_out, swa_q_norm, swa_k_norm, swa_sinks, w_swa_out, w_o, norm_ffn, w_up, ffn_conv, ffn_conv_b, w_down, loss_target, m_w_ada, m_b_ada, m_norm_mix, m_w_in, m_dn_conv, m_dn_a_log, m_dn_dt_bias, m_dn_norm, m_w_dn_out, m_swa_q_norm, m_swa_k_norm, m_swa_sinks, m_w_swa_out, m_w_o, m_norm_ffn, m_w_up, m_ffn_conv, m_ffn_conv_b, m_w_down, v_w_ada, v_b_ada, v_norm_mix, v_w_in, v_dn_conv, v_dn_a_log, v_dn_dt_bias, v_dn_norm, v_w_dn_out, v_swa_q_norm, v_swa_k_norm, v_swa_sinks, v_w_swa_out, v_w_o, v_norm_ffn, v_w_up, v_ffn_conv, v_ffn_conv_b, v_w_down):
    weights = dict(w_ada=w_ada, b_ada=b_ada, norm_mix=norm_mix, w_in=w_in, dn_conv=dn_conv, dn_a_log=dn_a_log, dn_dt_bias=dn_dt_bias,
                   dn_norm=dn_norm, w_dn_out=w_dn_out, swa_q_norm=swa_q_norm, swa_k_norm=swa_k_norm, swa_sinks=swa_sinks,
                   w_swa_out=w_swa_out, w_o=w_o, norm_ffn=norm_ffn, w_up=w_up, ffn_conv=ffn_conv, ffn_conv_b=ffn_conv_b, w_down=w_down)
    mom_m = dict(w_ada=m_w_ada, b_ada=m_b_ada, norm_mix=m_norm_mix, w_in=m_w_in, dn_conv=m_dn_conv, dn_a_log=m_dn_a_log,
                 dn_dt_bias=m_dn_dt_bias, dn_norm=m_dn_norm, w_dn_out=m_w_dn_out, swa_q_norm=m_swa_q_norm, swa_k_norm=m_swa_k_norm,
                 swa_sinks=m_swa_sinks, w_swa_out=m_w_swa_out, w_o=m_w_o, norm_ffn=m_norm_ffn, w_up=m_w_up, ffn_conv=m_ffn_conv,
                 ffn_conv_b=m_ffn_conv_b, w_down=m_w_down)
    mom_v = dict(w_ada=v_w_ada, b_ada=v_b_ada, norm_mix=v_norm_mix, w_in=v_w_in, dn_conv=v_dn_conv, dn_a_log=v_dn_a_log,
                 dn_dt_bias=v_dn_dt_bias, dn_norm=v_dn_norm, w_dn_out=v_w_dn_out, swa_q_norm=v_swa_q_norm, swa_k_norm=v_swa_k_norm,
                 swa_sinks=v_swa_sinks, w_swa_out=v_w_swa_out, w_o=v_w_o, norm_ffn=v_norm_ffn, w_up=v_w_up, ffn_conv=v_ffn_conv,
                 ffn_conv_b=v_ffn_conv_b, w_down=v_w_down)
    order = ["w_ada", "b_ada", "norm_mix", "w_in", "dn_conv", "dn_a_log", "dn_dt_bias", "dn_norm", "w_dn_out", "swa_q_norm",
             "swa_k_norm", "swa_sinks", "w_swa_out", "w_o", "norm_ffn", "w_up", "ffn_conv", "ffn_conv_b", "w_down"]
    ax, ay, ac = lax.axis_index("x"), lax.axis_index("y"), lax.axis_index("c")
    chip = 2 * ax + ay
    dev = 4 * ax + 2 * ay + ac
    T = x.shape[1]
    xs = x[0]

    c_all = _allgather_all("gather_c", jnp.pad(c, ((0, 7), (0, 0)))).reshape(8, 8, D)[:, 0]
    c_act = _silu_rows("silu_c", jnp.pad(c_all, ((0, 8), (0, 0))))
    mod_sh = jnp.stack([
        _matmul("mod_fwd", c_act, w_ada[l].astype(bf16), "nn", f32,
                bias=lax.dynamic_slice(b_ada[l], (chip * 1536,), (1536,)).reshape(1, 1536)) for l in range(DEPTH)])
    mod_all = _allgather_all("gather_mod", mod_sh.reshape(DEPTH * 16 * 12, LANE)).reshape(8, DEPTH, 16, 1536)
    mod_me = jnp.concatenate([lax.dynamic_index_in_dim(mod_all[2 * s], dev, axis=1, keepdims=False) for s in range(4)], axis=1)

    tabs_q = _rope_tables(positions[0])
    tabs = (tabs_q, tabs_q)
    head = jnp.arange(LANE) // 64
    bd128 = (head[:, None] == head[None, :]).astype(f32) / 64.0
    bd = (bd128, bd128)
    attn_bias = _attn_bias()

    conv_all = _allgather_all("gather_conv", _pack_small({"dn_conv": dn_conv, "ffn_conv": ffn_conv}, _CONV_SHARD))
    conv_parts = [_unpack_small(conv_all[2 * s], _CONV_SHARD) for s in range(4)]
    dn_conv_full = jnp.concatenate([p["dn_conv"].reshape(DEPTH, 4, 768) for p in conv_parts], axis=2)
    ffn_conv_full = jnp.concatenate([p["ffn_conv"].reshape(DEPTH, 3, 704) for p in conv_parts], axis=2)

    saved, Ws, Ks, mods = [], [], [], []
    h = xs
    shards = [[weights[nm][l].astype(bf16) for nm in _BIG] for l in range(DEPTH)]
    rest = (0, 1, 2, 3, 5)
    rows = lambda g: g.reshape(4 * g.shape[1], g.shape[2])

    def arrive(name, arrived, l, idx):
        got = _gather_forward(name, arrived)
        return [lax.dynamic_update_index_in_dim(g, shards[l][i], chip, 0) for g, i in zip(got, idx)]

    def up_late(l):
        return (_hosted_gather([shards[l][4]]), lambda arrived: {"w_up": arrive("gather_up_pass", arrived, l, (4,))[0]})

    got = [lax.dynamic_update_index_in_dim(g, shards[0][i], chip, 0) for i, g in enumerate(_gather_chips("gather_w", shards[0][:4]))]
    for l in range(DEPTH):
        W = _assemble_mixer(got[:4])
        late = {"proj": up_late(l)}
        if l == 0:
            late["attn"] = (_hosted_gather([shards[0][5]]), lambda arrived: {"w_down": rows(arrive("gather_down_pass", arrived, 0, (5,))[0])})
        else:
            W["w_down"] = rows(got[4])
        W["dn_conv"], W["ffn_conv"] = dn_conv_full[l], ffn_conv_full[l]
        K = _layer_consts({nm: weights[nm][l] for nm in ("norm_mix", "norm_ffn", "dn_a_log", "dn_dt_bias", "dn_norm", "swa_q_norm",
                                                          "swa_k_norm", "swa_sinks", "ffn_conv_b")})
        K["attn_bias"] = attn_bias
        mod = tuple(mod_me[l, k * D:(k + 1) * D].reshape(1, D) for k in range(6))
        nxt = _hosted_gather([shards[l + 1][i] for i in rest]) if l + 1 < DEPTH else None
        h, S, arrived = _layer_fwd(h, mod, W, K, tabs, bd, hosted=nxt, late=late)
        if nxt is not None:
            got = arrive("gather_w_pass", arrived, l + 1, rest)
        saved.append(S), Ws.append(W), Ks.append(K), mods.append(mod)

    loss_blk, dh = _loss("loss", h, loss_target[0])
    loss = lax.psum(loss_blk[0, 0], ("x", "y", "c"))

    grad_sh = [None] * DEPTH
    small = [None] * DEPTH
    dmods = [None] * DEPTH
    slots = None
    for l in reversed(range(DEPTH)):
        dh, gw, gs, dmod, (ps, b2), ffn0 = _layer_bwd(dh, saved[l], mods[l], Ws[l], Ks[l], tabs, bd, carry=None if slots is None else (slots, ac),
                                                      early=(ac, chip) if l == 0 else None)
        if slots is not None:
            grad_sh[l + 1] = dict(zip(_BIG, _rs_end(ps, b2, ac, chip)))
        slots = _grad_slots(gw)
        small[l], dmods[l] = dict(gs, dn_conv=gw["dn_conv"], ffn_conv=gw["ffn_conv"]), dmod[0]
    grad_sh[0] = dict(zip(_BIG, _reduce_scatter_multi(slots[:4], ac, chip) + ffn0))

    spec_g = _SMALL + _CONV
    vals = {nm: jnp.stack([small[l][nm] for l in range(DEPTH)]) for nm, _ in spec_g if nm != "b_ada"}
    vals["b_ada"] = jnp.stack(dmods)
    small_all = _allgather_all("gather_small", _pack_small(vals, spec_g))
    g_small = _unpack_small(_sum_leading("sum_small", small_all), spec_g)
    dmod_all = jnp.stack([_unpack_small(small_all[d], spec_g)["b_ada"] for d in range(8)])
    dmod_sh = lax.dynamic_slice(dmod_all, (0, 0, chip * 1536), (8, DEPTH, 1536))
    dmod_sh = jnp.pad(dmod_sh, ((0, 8), (0, 0), (0, 0))).astype(bf16)
    g_w_ada = jnp.stack([_matmul("mod_bwd_w", c_act, dmod_sh[:, l], "tn", f32) for l in range(DEPTH)])

    grads = {nm: g_small[nm] for nm, _ in _SMALL}
    grads["dn_conv"] = lax.dynamic_slice(g_small["dn_conv"].reshape(DEPTH, 4, 3072), (0, 0, chip * 768), (DEPTH, 4, 768))
    grads["ffn_conv"] = lax.dynamic_slice(g_small["ffn_conv"].reshape(DEPTH, 3, D_FF), (0, 0, chip * 704), (DEPTH, 3, 704))
    grads["w_ada"] = g_w_ada
    for nm in _BIG:
        grads[nm] = jnp.stack([grad_sh[l][nm] for l in range(DEPTH)])

    delta, new_m, new_v = {}, {}, {}
    for nm in ("w_ada", "dn_conv", "ffn_conv") + _BIG:
        delta[nm], new_m[nm], new_v[nm] = _adamw("adamw_" + nm, weights[nm], grads[nm], mom_m[nm], mom_v[nm])
    sm = [_pack_small({nm: t[nm] for nm, _ in _SMALL}, _SMALL) for t in (weights, grads, mom_m, mom_v)]
    for tgt, buf in zip((delta, new_m, new_v), _adamw("adamw_small", *sm)):
        tgt.update(_unpack_small(buf, _SMALL))

    return (loss, dh[None], *[grads[n] for n in order], *[delta[n] for n in order], *[new_m[n] for n in order], *[new_v[n] for n in order])
```

```python
import functools

import jax
import jax.numpy as jnp
import numpy as np
from jax import lax
from jax.experimental import pallas as pl
from jax.experimental.pallas import tpu as pltpu

f32 = jnp.float32
bf16 = jnp.bfloat16
SDS = jax.ShapeDtypeStruct
HI = lax.Precision.HIGHEST
MESH = pl.DeviceIdType.MESH

D = 1024
DEPTH = 4
EPS = 1e-6
DN_C = 64
SWA_B = 128
LANE = 128
ROPE_THETA = 500000.0
D_FF = 2816
IN_TOTAL = 7440
PROJ_W = 7680
CB_Q, CB_K, CB_V = 0, 8, 16
CB_AB, CB_SWK, CB_SWV = 56, 57, 58
WB_Z, WB_GA, WB_GB, WB_SWQ = 3, 4, 5, 6
TR = 256
COMM_TR = 128
VMEM_BIG = 48 * 2 ** 20

ADAM_LR, ADAM_B1, ADAM_B2, ADAM_EPS, ADAM_WD, ADAM_STEP = 0.001, 0.9, 0.999, 1e-08, 0.01, 10


def _pcall(body, **kw):
    return pl.pallas_call(body, **kw)


def _cparams(vmem=None):
    return pltpu.CompilerParams(vmem_limit_bytes=vmem) if vmem else None


def _dot(a, b, ca, cb, precision=HI):
    return lax.dot_general(a, b, (((ca,), (cb,)), ((), ())), precision=precision, preferred_element_type=f32)


def _pick(n, cands):
    for c in cands:
        if n % c == 0:
            return c
    return n


def _tile(n, cap):
    if n <= cap:
        return n
    best = None
    for t in range(LANE, cap + 1, LANE):
        if n % t == 0:
            best = t
    assert best is not None, (n, cap)
    return best


def _matmul(name, a, b, mode, out_dtype, bias=None, out_slots=None, hosted=None, resid=None, b_slots=False):
    h_args, h_specs, h_shapes, h_sems = _hosted_parts(hosted)
    nh = len(h_args)
    a_parts = list(a) if isinstance(a, (tuple, list)) else [a]
    b_parts = list(b) if isinstance(b, (tuple, list)) else [b]
    na, nbp = len(a_parts), len(b_parts)
    assert (na == 1 or mode == "nt") and (nbp == 1 or mode == "nn")
    if b_slots:
        S_, rows_, cols_ = b.shape
        M = a_parts[0].shape[0]
        K, N = (rows_, S_ * cols_) if mode == "nn" else (S_ * cols_, rows_)
    elif mode == "nn":
        M, K = a_parts[0].shape
        N = b_parts[0].shape[1] * nbp
    elif mode == "nt":
        M, K = a_parts[0].shape[0], a_parts[0].shape[1] * na
        N = b_parts[0].shape[0]
    else:
        (K, M), N = a_parts[0].shape, b_parts[0].shape[1]
    tm = _tile(M, 1536 if mode == "tn" else 1024)
    tn = N // out_slots if out_slots else _tile(N, 1536)
    tk = _tile(K, 512 if mode == "tn" else (1024 if K <= 1024 else 1536))
    nk, nj = K // tk, N // tn
    ka, jb = nk // na, nj // nbp
    assert nk % na == 0 and nj % nbp == 0
    ca, cb = {"nn": (1, 0), "nt": (1, 1), "tn": (0, 0)}[mode]
    grid = (M // tm, nj, nk)
    n_bias, n_res = (1 if bias is not None else 0), (2 if resid is not None else 0)

    def body(*refs):
        a_refs, b_refs = refs[:na], refs[na:na + nbp]
        p = na + nbp
        bias_ref = refs[p] if n_bias else None
        res_refs = refs[p + n_bias:p + n_bias + n_res]
        p += n_bias + n_res
        h_ins, o_ref = refs[p:p + nh], refs[p + nh]
        xo_ref = refs[p + nh + 1] if n_res else None
        p += nh + 1 + (1 if n_res else 0)
        h_outs, rest = refs[p:p + nh], refs[p + nh:]
        acc, sems = (rest[0], rest[1:]) if nk > 1 else (None, rest)
        j, k = pl.program_id(1), pl.program_id(2)
        step = (pl.program_id(0) * grid[1] + j) * grid[2] + k
        _hosted_edge(hosted, "start", h_ins, h_outs, sems, step == 0)

        def finish(r):
            if bias is not None:
                r = r + bias_ref[...]
            o_ref[...] = r.astype(o_ref.dtype)
            if n_res:
                xo_ref[...] = res_refs[0][...] + res_refs[1][...] * r

        a_tile = a_refs[0][...] if na == 1 else jnp.where(k < ka, a_refs[0][...], a_refs[1][...])
        b_tile = b_refs[0][...] if nbp == 1 else jnp.where(j < jb, b_refs[0][...], b_refs[1][...])
        part = _dot(a_tile.astype(bf16), b_tile.astype(bf16), ca, cb, precision=None)
        if nk == 1:
            finish(part)
        else:
            @pl.when(k == 0)
            def _():
                acc[...] = part

            @pl.when(k > 0)
            def _():
                acc[...] += part

            @pl.when(k == nk - 1)
            def _():
                finish(acc[...])

        _hosted_edge(hosted, "finish", h_ins, h_outs, sems, step == grid[0] * grid[1] * grid[2] - 1)

    if mode == "tn":
        a_specs = [pl.BlockSpec((tk, tm), lambda i, j, k: (k, i))]
    else:
        a_specs = [pl.BlockSpec((tm, tk), lambda i, j, k, q=q: (i, jnp.clip(k - q * ka, 0, ka - 1))) for q in range(na)]
    if b_slots:
        assert (tn if mode == "nn" else tk) == b.shape[2]
        b_specs = [pl.BlockSpec((None, tk, tn), lambda i, j, k: (j, k, 0)) if mode == "nn" else pl.BlockSpec((None, tn, tk), lambda i, j, k: (k, j, 0))]
    elif mode == "nt":
        b_specs = [pl.BlockSpec((tn, tk), lambda i, j, k: (j, k))]
    else:
        b_specs = [pl.BlockSpec((tk, tn), lambda i, j, k, q=q: (k, jnp.clip(j - q * jb, 0, jb - 1))) for q in range(nbp)]
    in_specs, args = a_specs + b_specs, a_parts + b_parts
    if bias is not None:
        in_specs.append(pl.BlockSpec((1, tn), lambda i, j, k: (0, j)))
        args.append(bias)
    if out_slots:
        out_specs, out_shape = [pl.BlockSpec((None, tm, tn), lambda i, j, k: (j, i, 0))], [SDS((out_slots, M, tn), out_dtype)]
    else:
        out_specs, out_shape = [pl.BlockSpec((tm, tn), lambda i, j, k: (i, j))], [SDS((M, N), out_dtype)]
    if resid is not None:
        in_specs += [pl.BlockSpec((tm, tn), lambda i, j, k: (i, j)), pl.BlockSpec((1, tn), lambda i, j, k: (0, j))]
        args += list(resid)
        out_specs.append(pl.BlockSpec((tm, tn), lambda i, j, k: (i, j)))
        out_shape.append(SDS((M, N), f32))
    outs = _pcall(
        body, grid=grid, in_specs=in_specs + h_specs, out_specs=out_specs + h_specs, out_shape=out_shape + h_shapes,
        scratch_shapes=([pltpu.VMEM((tm, tn), f32)] if nk > 1 else []) + h_sems, compiler_params=_cparams(VMEM_BIG), name=name)(*args, *h_args)
    n_main = len(out_shape)
    res = list(outs[:n_main]) + ([list(outs[n_main:])] if hosted is not None else [])
    return res[0] if len(res) == 1 else tuple(res)


def _row_specs(rows, tr):
    return [pl.BlockSpec((tr, w), lambda i, j, off=off: (i, off + j)) for (_, off, w) in rows]


def _rowwise_fwd(name, fn, rows, vecs, out_widths, out_dtypes, nc=1, tr=TR, also_transposed=False):
    T = rows[0][0].shape[0]
    n_in = len(rows) + len(vecs)
    n_out = len(out_widths)

    def body(*refs):
        vals = [r[...].astype(f32) for r in refs[:n_in]]
        res = fn(*vals)
        for o_ref, r in zip(refs[n_in:n_in + n_out], res):
            o_ref[...] = r.astype(o_ref.dtype)
        if also_transposed:
            refs[n_in + n_out][...] = res[0].T.astype(refs[n_in + n_out].dtype)

    in_specs = _row_specs(rows, tr) + [pl.BlockSpec(v.shape, lambda i, j: (0, 0)) for v in vecs]
    out_specs = [pl.BlockSpec((tr, w), lambda i, j: (i, j)) for w in out_widths]
    out_shape = [SDS((T, w * nc), dt) for w, dt in zip(out_widths, out_dtypes)]
    if also_transposed:
        out_specs.append(pl.BlockSpec((out_widths[0], tr), lambda i, j: (j, i)))
        out_shape.append(SDS((out_widths[0] * nc, T), out_dtypes[0]))
    return _pcall(body, grid=(T // tr, nc), in_specs=in_specs, out_specs=out_specs, out_shape=out_shape, name=name)(
        *[r[0] for r in rows], *vecs)


def _into_buffer(dest_buf, n_inputs, out_index):
    if dest_buf is None:
        return [], [], {}
    return [dest_buf], [pl.BlockSpec(memory_space=pl.ANY)], {n_inputs: out_index}


def _rowwise_bwd(name, fn, rows, vecs, cts, drow_dtypes, nc=1, tr=TR, add_to_first=None, dest=None):
    T = rows[0][0].shape[0]
    n_r, n_v, n_c = len(rows), len(vecs), len(cts)
    n_add = 0 if add_to_first is None else 1
    keep = [k for k, dt in enumerate(drow_dtypes) if dt is not None]
    members = [] if dest is None else list(dest[1])
    plain = [pos for pos in range(len(keep)) if pos not in members]
    n_dest = 1 if dest is not None else 0
    n_in = n_r + n_v + n_c + n_add

    def body(*refs):
        vals = [r[...].astype(f32) for r in refs[:n_in]]
        outs = refs[n_in + (1 if dest is not None and dest[0] is not None else 0):]
        i, j = pl.program_id(0), pl.program_id(1)
        _, vjp = jax.vjp(fn, *vals[:n_r + n_v])
        grads = vjp(tuple(vals[n_r + n_v:n_r + n_v + n_c]))
        got = []
        for pos, k in enumerate(keep):
            g = grads[k]
            if n_add and pos == 0:
                g = g + vals[n_in - 1]
            got.append(g)
        if dest is not None:
            outs[0][...] = jnp.concatenate([got[pos] for pos in members], axis=1).astype(outs[0].dtype)
        for q, pos in enumerate(plain):
            outs[n_dest + q][...] = got[pos].astype(outs[n_dest + q].dtype)
        vec_outs = outs[n_dest + len(plain):]

        @pl.when((i == 0) & (j == 0))
        def _():
            for q in range(n_v):
                vec_outs[q][...] = jnp.zeros_like(vec_outs[q])

        for q in range(n_v):
            vec_outs[q][...] += grads[n_r + q]

    extra = [] if add_to_first is None else [add_to_first]
    in_specs = (_row_specs(rows, tr) + [pl.BlockSpec(v.shape, lambda i, j: (0, 0)) for v in vecs]
                + _row_specs(cts, tr) + _row_specs(extra, tr))
    args = [r[0] for r in rows] + list(vecs) + [c[0] for c in cts] + [e[0] for e in extra]
    out_specs, out_shape, aliases = [], [], {}
    if dest is not None:
        width = sum(rows[keep[pos]][2] for pos in members)
        col = dest[2]
        b_args, b_specs, aliases = _into_buffer(dest[0], n_in, 0)
        args, in_specs = args + b_args, in_specs + b_specs
        out_specs.append(pl.BlockSpec((tr, width), lambda i, j: (i, col + j)))
        out_shape.append(SDS((T, PROJ_W), bf16))
    out_specs += [pl.BlockSpec((tr, rows[keep[pos]][2]), lambda i, j: (i, j)) for pos in plain]
    out_shape += [SDS((T, rows[keep[pos]][2] * nc), drow_dtypes[keep[pos]]) for pos in plain]
    out_specs += [pl.BlockSpec(v.shape, lambda i, j: (0, 0)) for v in vecs]
    out_shape += [SDS(v.shape, f32) for v in vecs]
    return _pcall(body, grid=(T // tr, nc), in_specs=in_specs, out_specs=out_specs, out_shape=out_shape,
                  input_output_aliases=aliases, name=name)(*args)


def _normmod_fn(x, w, sc, sh):
    y = x * lax.rsqrt(jnp.mean(x * x, axis=-1, keepdims=True) + EPS)
    return ((y * w) * (1.0 + sc) + sh,)


def _resid_fn(x, t, gt):
    return (x + gt * t,)


def _merge_fn(ga, gb, ya, yb):
    return (jax.nn.sigmoid(ga) * ya + jax.nn.sigmoid(gb) * yb,)


def _headmean_raw(x2):
    rows = x2.shape[0]
    return jnp.concatenate([jnp.broadcast_to(jnp.mean(x2[:, LANE * h:LANE * (h + 1)], axis=1, keepdims=True), (rows, LANE))
                            for h in range(x2.shape[1] // LANE)], axis=1)


@jax.custom_vjp
def _headmean(x2):
    return _headmean_raw(x2)


_headmean.defvjp(lambda x2: (_headmean_raw(x2), None), lambda _, dy: (_headmean_raw(dy),))


def _dngate_fn(o, z, w):
    y = o * lax.rsqrt(_headmean(o * o) + EPS)
    return ((y * w) * (z * jax.nn.sigmoid(z)),)


def _conv_taps(x, w_ref, taps, buf):
    w = lambda s: w_ref[taps - 1 - s:taps - s, :]
    row = lax.broadcasted_iota(jnp.int32, (8, x.shape[1]), 0)
    x8 = x[0:8]
    acc, acc8 = x * w(0), x8 * w(0)
    for s in range(1, taps):
        acc = acc + pltpu.roll(x, s, 0) * w(s)
        acc8 = acc8 + jnp.where(row >= s, pltpu.roll(x8, s, 0), 0.0) * w(s)
    buf[...] = acc
    buf[0:8, :] = acc8
    return buf[...]


def _conv_taps_bwd(x, dy, w_ref, dw_ref, taps, buf):
    T = x.shape[0]
    w = lambda s: w_ref[taps - 1 - s:taps - s, :]
    row = lax.broadcasted_iota(jnp.int32, (8, x.shape[1]), 0)
    dy_first, dy_last = dy[0:8], dy[T - 8:T]
    dx, dx_last = dy * w(0), dy_last * w(0)
    dw_ref[taps - 1:taps, :] = jnp.sum(dy * x, axis=0, keepdims=True)
    for s in range(1, taps):
        dx = dx + pltpu.roll(dy, T - s, 0) * w(s)
        dx_last = dx_last + jnp.where(row < 8 - s, pltpu.roll(dy_last, 8 - s, 0), 0.0) * w(s)
        xr = pltpu.roll(x, s, 0)
        wrapped = jnp.sum(jnp.where(row < s, dy_first * xr[0:8], 0.0), axis=0, keepdims=True)
        dw_ref[taps - 1 - s:taps - s, :] = jnp.sum(dy * xr, axis=0, keepdims=True) - wrapped
    buf[...] = dx
    buf[T - 8:T, :] = dx_last
    return buf[...]


def _dn_act_parts(y, normalize):
    sg = jax.nn.sigmoid(y)
    a = y * sg
    n = lax.rsqrt(jnp.sum(a * a, axis=-1, keepdims=True) + EPS) if normalize else None
    return sg, a, n


@functools.partial(jax.custom_vjp, nondiff_argnums=(1,))
def _dn_act(y, normalize):
    _, a, n = _dn_act_parts(y, normalize)
    return a * n if normalize else a


def _dn_act_fwd(y, normalize):
    return _dn_act(y, normalize), y


def _dn_act_bwd(normalize, y, do):
    sg, a, n = _dn_act_parts(y, normalize)
    if normalize:
        out = a * n
        do = n * (do - out * jnp.sum(do * out, axis=-1, keepdims=True))
    return (do * (sg + a * (1.0 - sg)),)


_dn_act.defvjp(_dn_act_fwd, _dn_act_bwd)


def _dnconv_fwd(name, proj, cb, w, normalize):
    T = proj.shape[0]

    def body(x_ref, w_ref, o_ref, buf):
        o_ref[...] = _dn_act(_conv_taps(x_ref[...], w_ref, 4, buf), normalize)

    return _pcall(
        body, grid=(8,), in_specs=[pl.BlockSpec((T, LANE), lambda j: (0, cb + j)), pl.BlockSpec((4, LANE), lambda j: (0, cb + j))],
        out_specs=pl.BlockSpec((T, LANE), lambda j: (0, j)), out_shape=SDS((T, 1024), f32), scratch_shapes=[pltpu.VMEM((T, LANE), f32)],
        compiler_params=_cparams(VMEM_BIG), name=name)(proj, w)


def _dnconv_bwd(name, proj, cb, w, dout, normalize, dest_buf):
    T = proj.shape[0]
    b_args, b_specs, aliases = _into_buffer(dest_buf, 3, 0)

    def body(x_ref, w_ref, do_ref, *rest):
        dx_ref, dw_ref, buf, buf2 = rest[len(b_args):]
        x = x_ref[...]
        y = _conv_taps(x, w_ref, 4, buf)
        _, vjp = jax.vjp(functools.partial(_dn_act, normalize=normalize), y)
        (dy,) = vjp(do_ref[...])
        dx_ref[...] = _conv_taps_bwd(x, dy, w_ref, dw_ref, 4, buf2).astype(dx_ref.dtype)

    return _pcall(
        body, grid=(8,),
        in_specs=[pl.BlockSpec((T, LANE), lambda j: (0, cb + j)), pl.BlockSpec((4, LANE), lambda j: (0, cb + j)),
                  pl.BlockSpec((T, LANE), lambda j: (0, j))] + b_specs,
        out_specs=[pl.BlockSpec((T, LANE), lambda j: (0, cb + j)), pl.BlockSpec((4, LANE), lambda j: (0, j))],
        out_shape=[SDS((T, PROJ_W), bf16), SDS((4, 1024), f32)], scratch_shapes=[pltpu.VMEM((T, LANE), f32)] * 2,
        input_output_aliases=aliases, compiler_params=_cparams(VMEM_BIG), name=name)(proj, w, dout, *b_args)


def _ffn_point(a, lin):
    return a * jax.nn.sigmoid(a) * lin


def _ffnact_fwd(name, up, w, b):
    T = up.shape[0]
    nblk = D_FF // LANE

    def body(a_ref, l_ref, w_ref, b_ref, o_ref, buf):
        a = _conv_taps(a_ref[...], w_ref, 3, buf) + b_ref[...]
        o_ref[...] = _ffn_point(a, l_ref[...]).astype(o_ref.dtype)

    return _pcall(
        body, grid=(nblk,),
        in_specs=[pl.BlockSpec((T, LANE), lambda j: (0, j)), pl.BlockSpec((T, LANE), lambda j: (0, nblk + j)),
                  pl.BlockSpec((3, LANE), lambda j: (0, j)), pl.BlockSpec((1, LANE), lambda j: (0, j))],
        out_specs=pl.BlockSpec((T, LANE), lambda j: (0, j)), out_shape=SDS((T, D_FF), bf16), scratch_shapes=[pltpu.VMEM((T, LANE), f32)],
        compiler_params=_cparams(VMEM_BIG), name=name)(up, up, w, b)


def _ffnact_bwd(name, up, w, b, dmid):
    T = up.shape[0]
    nblk = D_FF // LANE

    def body(a_ref, l_ref, w_ref, b_ref, dm_ref, da_ref, dl_ref, dw_ref, db_ref, buf, buf2):
        x = a_ref[...]
        a = _conv_taps(x, w_ref, 3, buf) + b_ref[...]
        _, vjp = jax.vjp(_ffn_point, a, l_ref[...])
        da, dl = vjp(dm_ref[...].astype(f32))
        dl_ref[...] = dl.astype(dl_ref.dtype)
        db_ref[...] = jnp.sum(da, axis=0, keepdims=True)
        da_ref[...] = _conv_taps_bwd(x, da, w_ref, dw_ref, 3, buf2).astype(da_ref.dtype)

    col = lambda r: pl.BlockSpec((r, LANE), lambda j: (0, j))
    return _pcall(
        body, grid=(nblk,),
        in_specs=[col(T), pl.BlockSpec((T, LANE), lambda j: (0, nblk + j)), col(3), col(1), col(T)],
        out_specs=[col(T), col(T), col(3), col(1)],
        out_shape=[SDS((T, D_FF), bf16), SDS((T, D_FF), bf16), SDS((3, D_FF), f32), SDS((1, D_FF), f32)],
        scratch_shapes=[pltpu.VMEM((T, LANE), f32)] * 2, compiler_params=_cparams(VMEM_BIG), name=name)(up, up, w, b, dmid)


def _bmm(a, b, ca, cb, precision=HI):
    return lax.dot_general(a, b, (((ca,), (cb,)), ((0,), (0,))), precision=precision, preferred_element_type=f32)


def _make_bdot(ca, cb):
    def raw(x, y, cx, cy):
        return _bmm(x.astype(bf16), y.astype(bf16), cx, cy, precision=None)

    @jax.custom_vjp
    def f(a, b):
        return raw(a, b, ca, cb)

    def fwd(a, b):
        return raw(a, b, ca, cb), (a, b)

    def bwd(res, dy):
        a, b = res
        if (ca, cb) == (2, 1):
            return raw(dy, b, 2, 2), raw(a, dy, 1, 1)
        if (ca, cb) == (2, 2):
            return raw(dy, b, 2, 1), raw(dy, a, 1, 1)
        return raw(b, dy, 2, 2), raw(a, dy, 2, 1)

    f.defvjp(fwd, bwd)
    return f


_bdot_nn, _bdot_nt, _bdot_tn = _make_bdot(2, 1), _make_bdot(2, 2), _make_bdot(1, 1)


def _pieces(a, n):
    out, r = [], a
    for _ in range(n):
        p = r.astype(bf16)
        out.append(p)
        r = r - p.astype(f32)
    return out


def _bmm_split(x, y, cx, cy, nx=2, ny=2, order=1):
    xs, ys = _pieces(x, nx), _pieces(y, ny)
    acc = None
    for i in reversed(range(nx)):
        for j in reversed(range(ny)):
            if i + j <= order:
                t = _bmm(xs[i], ys[j], cx, cy, precision=None)
                acc = t if acc is None else acc + t
    return acc


@jax.custom_vjp
def _solve_apply(X, r):
    return _bmm_split(X, r, 2, 1)


def _solve_apply_fwd(X, r):
    return _bmm_split(X, r, 2, 1), (X, r)


def _solve_apply_bwd(res, dy):
    X, r = res
    return _bmm_split(dy, r, 2, 2), _bmm_split(X, dy, 1, 1)


_solve_apply.defvjp(_solve_apply_fwd, _solve_apply_bwd)


def _lower_ones(H, C):
    ri = lax.broadcasted_iota(jnp.int32, (H, C, C), 1)
    ci = lax.broadcasted_iota(jnp.int32, (H, C, C), 2)
    return (ri >= ci).astype(f32)


def _cumsum_rows_raw(G):
    return _bmm_split(_lower_ones(G.shape[0], G.shape[1]), G, 2, 1, nx=1, ny=3, order=2)


@jax.custom_vjp
def _cumsum_rows(G):
    return _cumsum_rows_raw(G)


def _cumsum_rows_fwd(G):
    return _cumsum_rows_raw(G), None


def _cumsum_rows_bwd(_, dy):
    return (_bmm_split(_lower_ones(dy.shape[0], dy.shape[1]), dy, 1, 1, nx=1, ny=3, order=2),)


_cumsum_rows.defvjp(_cumsum_rows_fwd, _cumsum_rows_bwd)


def _tri_inverse_raw(L):
    H, C, _ = L.shape
    ri = lax.broadcasted_iota(jnp.int32, (C, C), 0)
    ci = lax.broadcasted_iota(jnp.int32, (C, C), 1)
    eye = jnp.broadcast_to((ri == ci).astype(f32)[None], (H, C, C))
    Dg = jnp.where(((ri >> 3) == (ci >> 3))[None], L, 0.0)
    D2 = _bmm_split(Dg, Dg, 2, 1)
    X = _bmm_split(_bmm_split(eye - Dg, eye + D2, 2, 1), eye + _bmm_split(D2, D2, 2, 1), 2, 1)
    for lg in range(3, C.bit_length() - 1):
        same = (ri >> (lg + 1)) == (ci >> (lg + 1))
        lower_left = same & (((ri >> lg) & 1) == 1) & (((ci >> lg) & 1) == 0)
        X = X - _bmm_split(_bmm_split(X, jnp.where(lower_left[None], L, 0.0), 2, 1), X, 2, 1)
    return X


@jax.custom_vjp
def _tri_inverse(L):
    return _tri_inverse_raw(L)


def _tri_inverse_fwd(L):
    X = _tri_inverse_raw(L)
    return X, X


def _tri_inverse_bwd(X, dX):
    return (-_bmm_split(_bmm_split(X, dX, 1, 1), X, 2, 2),)


_tri_inverse.defvjp(_tri_inverse_fwd, _tri_inverse_bwd)


@jax.custom_vjp
def _tri_inverse_known(L, X):
    return X


def _tri_inverse_known_fwd(L, X):
    return X, X


def _tri_inverse_known_bwd(X, dX):
    return _tri_inverse_bwd(X, dX)[0], jnp.zeros_like(X)


_tri_inverse_known.defvjp(_tri_inverse_known_fwd, _tri_inverse_known_bwd)


DN_NC = 4


def _gdn_chunk(q, k, v, ab, alog, dtb, S, X_known=None, keep_X=False):
    H, C, _ = q.shape
    NC, NH = ab.shape[0], H // ab.shape[0]
    lane = lax.broadcasted_iota(jnp.int32, (H, C, LANE), 2)
    head = lax.broadcasted_iota(jnp.int32, (H, C, LANE), 0) & (NH - 1)
    abb = jnp.concatenate([jnp.broadcast_to(ab[c][None], (NH, C, LANE)) for c in range(NC)], axis=0)
    a = jnp.sum(jnp.where(lane == head, abb, 0.0), axis=2, keepdims=True)
    b = jnp.sum(jnp.where(lane == head + 8, abb, 0.0), axis=2, keepdims=True)
    pick = lax.broadcasted_iota(jnp.int32, (H, 1, LANE), 2) == (lax.broadcasted_iota(jnp.int32, (H, 1, LANE), 0) & (NH - 1))
    al = jnp.sum(jnp.where(pick, alog[None], 0.0), axis=2, keepdims=True)
    db = jnp.sum(jnp.where(pick, dtb[None], 0.0), axis=2, keepdims=True)
    g = -jnp.exp(al) * jax.nn.softplus(a + db)
    beta = jax.nn.sigmoid(b)
    ri = lax.broadcasted_iota(jnp.int32, (C, C), 0)
    ci = lax.broadcasted_iota(jnp.int32, (C, C), 1)
    G = jnp.broadcast_to(g, (H, C, LANE))
    gc = _cumsum_rows(G)
    gi = _cumsum_rows(jnp.broadcast_to(g, (H, C, C)))
    decay = jnp.exp(jnp.where((ri >= ci)[None], gi - jnp.swapaxes(gi, 1, 2), -jnp.inf))
    qs = q * (LANE ** -0.5)
    kb = k * beta
    L = jnp.where((ri > ci)[None], _bdot_nt(kb, k) * decay, 0.0)
    X = _tri_inverse(L) if X_known is None else _tri_inverse_known(L, X_known)
    egc = jnp.exp(gc)
    u = _solve_apply(X, v * beta)
    w = _solve_apply(X, kb * egc)
    qk = _bdot_nt(qs, k) * decay
    g_last = jnp.sum(G, axis=1, keepdims=True)
    k_dec = k * jnp.exp(g_last - gc)
    q_dec = qs * egc
    e_last = jnp.exp(g_last)
    outs = []
    for c in range(NC):
        sl = slice(c * NH, (c + 1) * NH)
        v_new = u[sl] - _bdot_nn(w[sl], S)
        outs.append(_bdot_nn(q_dec[sl], S) + _bdot_nn(qk[sl], v_new))
        S = S * e_last[sl] + _bdot_tn(k_dec[sl], v_new)
    o = jnp.concatenate(outs, axis=0)
    return (o, S, X) if keep_X else (o, S)


def _heads(ref):
    return jnp.stack([ref[DN_C * c:DN_C * (c + 1), LANE * h:LANE * (h + 1)] for c in range(DN_NC) for h in range(8)], axis=0)


def _put_heads(ref, val):
    for c in range(DN_NC):
        for h in range(8):
            ref[DN_C * c:DN_C * (c + 1), LANE * h:LANE * (h + 1)] = val[8 * c + h]


def _chunk_rows(ref):
    return jnp.stack([ref[DN_C * c:DN_C * (c + 1), :] for c in range(DN_NC)], axis=0)


def _hosted_parts(hosted):
    if hosted is None:
        return [], [], [], []
    n = len(hosted["arrays"])
    return list(hosted["arrays"]), [HBM_SPEC] * n, list(hosted["out_shape"]), _dma_sems(hosted["n_sems"])


def _gdn_fwd(name, q, k, v, proj, alog, dtb, hosted=None):
    T = q.shape[0]
    R = DN_C * DN_NC
    N = T // R
    h_args, h_specs, h_shapes, h_sems = _hosted_parts(hosted)
    nh = len(h_args)

    def body(q_ref, k_ref, v_ref, ab_ref, al_ref, dt_ref, *rest):
        h_ins, (o_ref, sall_ref, xinv_ref), h_outs, s_scr, sems = rest[:nh], rest[nh:nh + 3], rest[nh + 3:2 * nh + 3], rest[2 * nh + 3], rest[2 * nh + 4:]
        step = pl.program_id(0)

        @pl.when(step == 0)
        def _():
            s_scr[...] = jnp.zeros_like(s_scr)
            if hosted is not None:
                hosted["start"](h_ins, h_outs, *sems)

        S = s_scr[...]
        sall_ref[...] = S
        o, S_new, X = _gdn_chunk(_heads(q_ref), _heads(k_ref), _heads(v_ref), _chunk_rows(ab_ref), al_ref[...], dt_ref[...], S, keep_X=True)
        _put_heads(o_ref, o)
        s_scr[...] = S_new
        xinv_ref[...] = X

        if hosted is not None:
            @pl.when(step == N - 1)
            def _():
                hosted["finish"](h_ins, h_outs, *sems)

    blk = pl.BlockSpec((R, 8 * LANE), lambda n: (n, 0))
    vec = pl.BlockSpec((1, LANE), lambda n: (0, 0))
    state = pl.BlockSpec((None, 8, LANE, LANE), lambda n: (n, 0, 0, 0))
    xinv = pl.BlockSpec((None, 8 * DN_NC, DN_C, DN_C), lambda n: (n, 0, 0, 0))
    outs = _pcall(
        body, grid=(N,), in_specs=[blk, blk, blk, pl.BlockSpec((R, LANE), lambda n: (n, CB_AB)), vec, vec] + h_specs,
        out_specs=[blk, state, xinv] + h_specs,
        out_shape=[SDS((T, 1024), f32), SDS((N, 8, LANE, LANE), f32), SDS((N, 8 * DN_NC, DN_C, DN_C), f32)] + h_shapes,
        scratch_shapes=[pltpu.VMEM((8, LANE, LANE), f32)] + h_sems, compiler_params=_cparams(VMEM_BIG), name=name)(q, k, v, proj, alog, dtb, *h_args)
    return outs[0], outs[1], outs[2], list(outs[3:])


def _gdn_bwd(name, q, k, v, proj, alog, dtb, sall, xinv, do, hosted=None):
    T = q.shape[0]
    R = DN_C * DN_NC
    N = T // R
    h_args, h_specs, h_shapes, h_sems = _hosted_parts(hosted)
    nh = len(h_args)

    def body(q_ref, k_ref, v_ref, ab_ref, al_ref, dt_ref, s_ref, x_ref, do_ref, *rest):
        h_ins, h_outs, ds_scr, sems = rest[:nh], rest[nh + 6:2 * nh + 6], rest[2 * nh + 6], rest[2 * nh + 7:]
        dq_ref, dk_ref, dv_ref, dab_ref, dal_ref, ddt_ref = rest[nh:nh + 6]
        step = pl.program_id(0)

        @pl.when(step == 0)
        def _():
            ds_scr[...] = jnp.zeros_like(ds_scr)
            dal_ref[...] = jnp.zeros_like(dal_ref)
            ddt_ref[...] = jnp.zeros_like(ddt_ref)
            if hosted is not None:
                hosted["start"](h_ins, h_outs, *sems)

        _, vjp = jax.vjp(functools.partial(_gdn_chunk, X_known=x_ref[...]), _heads(q_ref), _heads(k_ref), _heads(v_ref), _chunk_rows(ab_ref),
                         al_ref[...], dt_ref[...], s_ref[...])
        dq, dk, dv, dab, dal, ddt, dS = vjp((_heads(do_ref), ds_scr[...]))
        _put_heads(dq_ref, dq)
        _put_heads(dk_ref, dk)
        _put_heads(dv_ref, dv)
        ds_scr[...] = dS
        for c in range(DN_NC):
            dab_ref[DN_C * c:DN_C * (c + 1), :] = dab[c]
        dal_ref[...] += dal
        ddt_ref[...] += ddt

        if hosted is not None:
            @pl.when(step == N - 1)
            def _():
                hosted["finish"](h_ins, h_outs, *sems)

    blk = pl.BlockSpec((R, 8 * LANE), lambda n: (N - 1 - n, 0))
    vec = pl.BlockSpec((1, LANE), lambda n: (0, 0))
    state = pl.BlockSpec((None, 8, LANE, LANE), lambda n: (N - 1 - n, 0, 0, 0))
    outs = _pcall(
        body, grid=(N,),
        in_specs=[blk, blk, blk, pl.BlockSpec((R, LANE), lambda n: (N - 1 - n, CB_AB)), vec, vec, state,
                  pl.BlockSpec((None, 8 * DN_NC, DN_C, DN_C), lambda n: (N - 1 - n, 0, 0, 0)), blk] + h_specs,
        out_specs=[blk, blk, blk, pl.BlockSpec((R, LANE), lambda n: (N - 1 - n, 0)), vec, vec] + h_specs,
        out_shape=[SDS((T, 1024), f32)] * 3 + [SDS((T, LANE), f32), SDS((1, LANE), f32), SDS((1, LANE), f32)] + h_shapes,
        scratch_shapes=[pltpu.VMEM((8, LANE, LANE), f32)] + h_sems, compiler_params=_cparams(VMEM_BIG),
        name=name)(q, k, v, proj, alog, dtb, sall, xinv, do, *h_args)
    return tuple(outs[:6]), list(outs[6:])


def _segmean_raw(x2, bd):
    return jnp.concatenate([_dot(x2[:, LANE * j:LANE * (j + 1)], bd, 1, 0) for j in range(x2.shape[1] // LANE)], axis=1)


@jax.custom_vjp
def _segmean(x2, bd):
    return _segmean_raw(x2, bd)


def _segmean_fwd(x2, bd):
    return _segmean_raw(x2, bd), bd


def _segmean_bwd(bd, dy):
    return _segmean_raw(dy, bd), jnp.zeros_like(bd)


_segmean.defvjp(_segmean_fwd, _segmean_bwd)


def _qknorm_fn(x, w, bd):
    return x * lax.rsqrt(_segmean(x * x, bd) + EPS) * w


def _rope_apply(xn, c, s1, s2):
    W = xn.shape[1]
    return xn * c + pltpu.roll(xn, W - 8, 1) * s1 + pltpu.roll(xn, 8, 1) * s2


def _rope_apply_t(d, c, s1, s2):
    W = d.shape[1]
    return d * c + pltpu.roll(d * s1, 8, 1) + pltpu.roll(d * s2, W - 8, 1)


def _rope_tiles(refs, width):
    return [jnp.tile(r[...], (1, width // LANE)) for r in refs]


def _qkprep_fwd(name, proj, wb, width, w, bd, tabs):
    T = proj.shape[0]
    tr = _pick(T, (256, 128))

    def body(x_ref, w_ref, bd_ref, c_ref, s1_ref, s2_ref, o_ref):
        xn = _qknorm_fn(x_ref[...], w_ref[...], bd_ref[...])
        o_ref[...] = _rope_apply(xn, *_rope_tiles((c_ref, s1_ref, s2_ref), width))

    row0 = pl.BlockSpec((tr, width), lambda i: (i, 0))
    tab = pl.BlockSpec((tr, LANE), lambda i: (i, 0))
    full = lambda a: pl.BlockSpec(a.shape, lambda i: (0, 0))
    return _pcall(
        body, grid=(T // tr,), in_specs=[pl.BlockSpec((tr, width), lambda i: (i, wb)), full(w), full(bd), tab, tab, tab],
        out_specs=row0, out_shape=SDS((T, width), f32), name=name)(proj, w, bd, *tabs)


def _qkprep_bwd(name, proj, wb, width, w, bd, tabs, dout, dest_buf="none"):
    T = proj.shape[0]
    tr = _pick(T, (256, 128))
    into = not isinstance(dest_buf, str)
    b_args, b_specs, aliases = _into_buffer(dest_buf, 7, 0) if into else ([], [], {})

    def body(x_ref, w_ref, bd_ref, c_ref, s1_ref, s2_ref, do_ref, *rest):
        dx_ref, dw_ref = rest[len(b_args):]
        i = pl.program_id(0)
        dxn = _rope_apply_t(do_ref[...], *_rope_tiles((c_ref, s1_ref, s2_ref), width))
        bd = bd_ref[...]
        _, vjp = jax.vjp(lambda x, w_: _qknorm_fn(x, w_, bd), x_ref[...], w_ref[...])
        dx, dw = vjp(dxn)
        dx_ref[...] = dx.astype(dx_ref.dtype)

        @pl.when(i == 0)
        def _():
            dw_ref[...] = jnp.zeros_like(dw_ref)

        dw_ref[...] += dw

    row0 = pl.BlockSpec((tr, width), lambda i: (i, 0))
    tab = pl.BlockSpec((tr, LANE), lambda i: (i, 0))
    full = lambda a: pl.BlockSpec(a.shape, lambda i: (0, 0))
    dx_spec = pl.BlockSpec((tr, width), lambda i: (i, wb)) if into else row0
    dx_shape = SDS((T, PROJ_W), bf16) if into else SDS((T, width), bf16)
    return _pcall(
        body, grid=(T // tr,), in_specs=[pl.BlockSpec((tr, width), lambda i: (i, wb)), full(w), full(bd), tab, tab, tab, row0] + b_specs,
        out_specs=[dx_spec, full(w)], out_shape=[dx_shape, SDS(w.shape, f32)], input_output_aliases=aliases, name=name)(
            proj, w, bd, *tabs, dout, *b_args)


def _make_dot16(ca, cb):
    def raw(x, y, cx, cy):
        return _dot(x.astype(bf16), y.astype(bf16), cx, cy, precision=None)

    @jax.custom_vjp
    def f(a, b):
        return raw(a, b, ca, cb)

    def fwd(a, b):
        return raw(a, b, ca, cb), (a, b)

    def bwd(res, dy):
        a, b = res
        if (ca, cb) == (1, 0):
            return raw(dy, b, 1, 1), raw(a, dy, 0, 0)
        return raw(dy, b, 1, 0), raw(dy, a, 0, 0)

    f.defvjp(fwd, bwd)
    return f


_dot16_nn, _dot16_nt = _make_dot16(1, 0), _make_dot16(1, 1)


def _attn_bias():
    qi = jnp.arange(8 * SWA_B) % SWA_B
    kj = jnp.arange(2 * SWA_B)
    rel = qi[:, None] + SWA_B - kj[None, :]
    valid = (rel >= 0) & (rel < SWA_B)
    neg = jnp.float32(-jnp.inf)
    return jnp.stack([jnp.where(valid & (kj[None, :] >= SWA_B), 0.0, neg), jnp.where(valid, 0.0, neg)]).astype(f32)


def _attn_group(qg, kb, vb, sinks, bias, hk):
    R = qg.shape[0]
    s = _dot16_nt(qg, kb) * 0.125 + bias
    head = (lax.broadcasted_iota(jnp.int32, (R, LANE), 0) >> 7) + 8 * hk
    lane = lax.broadcasted_iota(jnp.int32, (R, LANE), 1)
    sink = jnp.sum(jnp.where(lane == head, jnp.broadcast_to(sinks, (R, LANE)), 0.0), axis=1, keepdims=True)
    m = lax.stop_gradient(jnp.maximum(jnp.max(s, axis=1, keepdims=True), sink))
    p = jnp.exp(s - m)
    denom = jnp.sum(p, axis=1, keepdims=True) + jnp.exp(sink - m)
    return _dot16_nn(p / denom, vb)


def _group_rows(ref, hk):
    return jnp.concatenate([ref[:, 64 * (8 * hk + g):64 * (8 * hk + g + 1)] for g in range(8)], axis=0)


def _put_group(ref, hk, val):
    for g in range(8):
        ref[:, 64 * (8 * hk + g):64 * (8 * hk + g + 1)] = val[SWA_B * g:SWA_B * (g + 1)].astype(ref.dtype)


def _attn_specs():
    qs = pl.BlockSpec((SWA_B, 1024), lambda i: (i, 0))
    cur = pl.BlockSpec((SWA_B, LANE), lambda i: (i, 0))
    prev = pl.BlockSpec((SWA_B, LANE), lambda i: (jnp.maximum(i - 1, 0), 0))
    vcur = pl.BlockSpec((SWA_B, LANE), lambda i: (i, CB_SWV))
    vprev = pl.BlockSpec((SWA_B, LANE), lambda i: (jnp.maximum(i - 1, 0), CB_SWV))
    vec = pl.BlockSpec((1, LANE), lambda i: (0, 0))
    bias = pl.BlockSpec((None, 8 * SWA_B, 2 * SWA_B), lambda i: (jnp.minimum(i, 1), 0, 0))
    return qs, cur, prev, vcur, vprev, vec, bias


def _hosted_edge(hosted, which, h_ins, h_outs, sems, at):
    if hosted is None:
        return

    @pl.when(at)
    def _():
        hosted[which](h_ins, h_outs, *sems)


def _attn_fwd(name, sq, sk, proj, sinks, bias, hosted=None):
    T = sq.shape[0]
    nb = T // SWA_B
    h_args, h_specs, h_shapes, h_sems = _hosted_parts(hosted)
    nh = len(h_args)

    def body(q_ref, kp_ref, kc_ref, vp_ref, vc_ref, sk_ref, b_ref, *rest):
        h_ins, o_ref, h_outs, sems = rest[:nh], rest[nh], rest[nh + 1:2 * nh + 1], rest[2 * nh + 1:]
        _hosted_edge(hosted, "start", h_ins, h_outs, sems, pl.program_id(0) == 0)
        sinks_v, bias_v = sk_ref[...], b_ref[...]
        for hk in range(2):
            ks = slice(64 * hk, 64 * hk + 64)
            kb = jnp.concatenate([kp_ref[:, ks], kc_ref[:, ks]], axis=0)
            vb = jnp.concatenate([vp_ref[:, ks], vc_ref[:, ks]], axis=0)
            _put_group(o_ref, hk, _attn_group(_group_rows(q_ref, hk), kb, vb, sinks_v, bias_v, hk))
        _hosted_edge(hosted, "finish", h_ins, h_outs, sems, pl.program_id(0) == nb - 1)

    qs, cur, prev, vcur, vprev, vec, bspec = _attn_specs()
    outs = _pcall(body, grid=(nb,), in_specs=[qs, prev, cur, vprev, vcur, vec, bspec] + h_specs, out_specs=[qs] + h_specs,
                  out_shape=[SDS((T, 1024), bf16)] + h_shapes, scratch_shapes=h_sems, name=name)(sq, sk, sk, proj, proj, sinks, bias, *h_args)
    return outs[0], list(outs[1:])


def _attn_bwd(name, sq, sk, proj, sinks, bias, do, hosted=None):
    T = sq.shape[0]
    nb = T // SWA_B
    h_args, h_specs, h_shapes, h_sems = _hosted_parts(hosted)
    nh = len(h_args)

    def body(q_ref, kp_ref, kc_ref, vp_ref, vc_ref, sk_ref, b_ref, do_ref, *rest):
        h_ins, h_outs, sems = rest[:nh], rest[nh + 6:2 * nh + 6], rest[2 * nh + 6:]
        dq_ref, dkp_ref, dkc_ref, dvp_ref, dvc_ref, dsk_ref = rest[nh:nh + 6]
        _hosted_edge(hosted, "start", h_ins, h_outs, sems, pl.program_id(0) == 0)

        @pl.when(pl.program_id(0) == 0)
        def _():
            dsk_ref[...] = jnp.zeros_like(dsk_ref)

        sinks_v, bias_v = sk_ref[...], b_ref[...]
        dsk = jnp.zeros((1, LANE), f32)
        for hk in range(2):
            ks = slice(64 * hk, 64 * hk + 64)
            kb = jnp.concatenate([kp_ref[:, ks], kc_ref[:, ks]], axis=0)
            vb = jnp.concatenate([vp_ref[:, ks], vc_ref[:, ks]], axis=0)
            _, vjp = jax.vjp(functools.partial(_attn_group, bias=bias_v, hk=hk), _group_rows(q_ref, hk), kb, vb, sinks_v)
            dq, dkb, dvb, ds_ = vjp(_group_rows(do_ref, hk))
            _put_group(dq_ref, hk, dq)
            dsk = dsk + ds_
            dkp_ref[:, ks] = dkb[:SWA_B]
            dkc_ref[:, ks] = dkb[SWA_B:]
            dvp_ref[:, ks] = dvb[:SWA_B]
            dvc_ref[:, ks] = dvb[SWA_B:]
        dsk_ref[...] += dsk
        _hosted_edge(hosted, "finish", h_ins, h_outs, sems, pl.program_id(0) == nb - 1)

    qs, cur, prev, vcur, vprev, vec, bspec = _attn_specs()
    outs = _pcall(
        body, grid=(nb,), in_specs=[qs, prev, cur, vprev, vcur, vec, bspec, qs] + h_specs, out_specs=[qs, cur, cur, cur, cur, vec] + h_specs,
        out_shape=[SDS((T, 1024), f32)] + [SDS((T, LANE), f32)] * 4 + [SDS((1, LANE), f32)] + h_shapes, scratch_shapes=h_sems,
        name=name)(sq, sk, sk, proj, proj, sinks, bias, do, *h_args)
    return tuple(outs[:6]), list(outs[6:])


def _shift_add(name, cur, prev, out_dtype):
    T = cur.shape[0]

    def body(c_ref, p_ref, o_ref):
        o_ref[0:T - SWA_B, :] = (c_ref[0:T - SWA_B, :] + p_ref[SWA_B:T, :]).astype(o_ref.dtype)
        o_ref[T - SWA_B:T, :] = c_ref[T - SWA_B:T, :].astype(o_ref.dtype)

    return _pcall(body, out_shape=SDS((T, LANE), out_dtype), name=name)(cur, prev)


def _loss(name, y, tgt):
    T = y.shape[0]

    def body(y_ref, t_ref, l_ref, dy_ref):
        @pl.when(pl.program_id(0) == 0)
        def _():
            l_ref[...] = jnp.zeros_like(l_ref)

        d = y_ref[...] - t_ref[...]
        l_ref[...] += jnp.sum(d * d) * (0.5 / D)
        dy_ref[...] = d * (1.0 / D)

    row = pl.BlockSpec((TR, D), lambda i: (i, 0))
    return _pcall(body, grid=(T // TR,), in_specs=[row, row], out_specs=[pl.BlockSpec((8, LANE), lambda i: (0, 0)), row],
                  out_shape=[SDS((8, LANE), f32), SDS((T, D), f32)], name=name)(y, tgt)


def _adamw(name, w, g, m, v):
    shape = w.shape
    C = shape[-1]
    R = int(np.prod(shape[:-1]))
    tr = _pick(R, (128, 64, 16, 8))
    bc1 = np.float32(1.0 - ADAM_B1 ** ADAM_STEP)
    bc2 = np.float32(1.0 - ADAM_B2 ** ADAM_STEP)

    def body(w_ref, g_ref, m_ref, v_ref, d_ref, mo_ref, vo_ref):
        g_ = g_ref[...]
        m_ = ADAM_B1 * m_ref[...] + (1.0 - ADAM_B1) * g_
        v_ = ADAM_B2 * v_ref[...] + (1.0 - ADAM_B2) * (g_ * g_)
        d_ref[...] = -ADAM_LR * ((m_ / bc1) / (jnp.sqrt(v_ / bc2) + ADAM_EPS) + ADAM_WD * w_ref[...])
        mo_ref[...] = m_
        vo_ref[...] = v_

    blk = pl.BlockSpec((tr, C), lambda i: (i, 0))
    outs = _pcall(body, grid=(R // tr,), in_specs=[blk] * 4, out_specs=[blk] * 3, out_shape=[SDS((R, C), f32)] * 3,
                  compiler_params=_cparams(VMEM_BIG), name=name)(*[t.reshape(R, C) for t in (w, g, m, v)])
    return [o.reshape(shape) for o in outs]


def _silu_rows(name, x):
    def body(x_ref, o_ref):
        t = x_ref[...]
        o_ref[...] = (t * jax.nn.sigmoid(t)).astype(o_ref.dtype)

    return _pcall(body, out_shape=SDS(x.shape, bf16), name=name)(x)


def _sum_leading(name, x):
    n = x.shape[0]

    def body(x_ref, o_ref):
        acc = x_ref[0]
        for k in range(1, n):
            acc = acc + x_ref[k]
        o_ref[...] = acc

    tr = x.shape[1] if x.size * 4 <= 8 * 2 ** 20 else _pick(x.shape[1], (COMM_TR, 8))
    return _pcall(body, grid=(x.shape[1] // tr,), in_specs=[pl.BlockSpec((n, tr, x.shape[2]), lambda i: (0, i, 0))],
                  out_specs=pl.BlockSpec((tr, x.shape[2]), lambda i: (i, 0)), out_shape=SDS(x.shape[1:], x.dtype), name=name)(x)


HBM_SPEC = pl.BlockSpec(memory_space=pltpu.HBM)


def _position():
    x, y, c = lax.axis_index("x"), lax.axis_index("y"), lax.axis_index("c")
    return x, y, c, [(1 - x, y), (x, 1 - y), (1 - x, 1 - y)]


def _remote(src, dst, send_sems, recv_sems, k, to):
    return pltpu.make_async_remote_copy(src_ref=src, dst_ref=dst, send_sem=send_sems.at[k], recv_sem=recv_sems.at[k],
                                        device_id=to, device_id_type=MESH)


def _allgather_all(name, buf):
    r, W = buf.shape

    def body(in_ref, out_ref, send_sems, recv_sems, local_sem):
        x, y, c, _ = _position()
        me = 4 * x + 2 * y + c
        mine = pltpu.make_async_copy(in_ref, out_ref.at[me], local_sem)
        mine.start()
        peers = []
        for mk in range(1, 8):
            mx, my, mc = (mk >> 2) & 1, (mk >> 1) & 1, mk & 1
            px = 1 - x if mx else x
            py = 1 - y if my else y
            pc = 1 - c if mc else c
            peers.append((px, py, pc))
        sends = [_remote(in_ref, out_ref.at[me], send_sems, recv_sems, k, p) for k, p in enumerate(peers)]
        for cp in sends:
            cp.start()
        for k, (px, py, pc) in enumerate(peers):
            slot = out_ref.at[4 * px + 2 * py + pc]
            _remote(slot, slot, send_sems, recv_sems, k, (px, py, pc)).wait_recv()
        for cp in sends:
            cp.wait_send()
        mine.wait()

    return _pcall(body, in_specs=[HBM_SPEC], out_specs=HBM_SPEC, out_shape=SDS((8, r, W), buf.dtype),
                  scratch_shapes=[pltpu.SemaphoreType.DMA((7,)), pltpu.SemaphoreType.DMA((7,)), pltpu.SemaphoreType.DMA], name=name)(buf)


def _dma_sems(n):
    return [pltpu.SemaphoreType.DMA((n,)), pltpu.SemaphoreType.DMA((n,))]


def _gather_chips(name, shards):
    n = len(shards)

    def body(*refs):
        ins, outs = refs[:n], refs[n:2 * n]
        send_sems, recv_sems = refs[2 * n:]
        x, y, c, chips = _position()
        me = 2 * x + y
        sib = (x, y, 1 - c)
        sends, halves = [], []
        for i in range(n):
            rh = ins[i].shape[0] // 2
            halves.append((pl.ds(pl.multiple_of(c * rh, 16), rh), pl.ds(pl.multiple_of((1 - c) * rh, 16), rh)))
        for i in range(n):
            for j, (cx, cy) in enumerate(chips):
                cp = _remote(ins[i].at[halves[i][0]], outs[i].at[me, halves[i][0]], send_sems, recv_sems, 6 * i + j, (cx, cy, c))
                cp.start()
                sends.append(cp)
        for j, (cx, cy) in enumerate(chips):
            for i in range(n):
                rows = outs[i].at[2 * cx + cy, halves[i][0]]
                _remote(rows, rows, send_sems, recv_sems, 6 * i + j, (cx, cy, c)).wait_recv()
                cp = _remote(rows, rows, send_sems, recv_sems, 6 * i + 3 + j, sib)
                cp.start()
                sends.append(cp)
        for j, (cx, cy) in enumerate(chips):
            for i in range(n):
                rows = outs[i].at[2 * cx + cy, halves[i][1]]
                _remote(rows, rows, send_sems, recv_sems, 6 * i + 3 + j, sib).wait_recv()
        for cp in sends:
            cp.wait_send()

    return _pcall(body, in_specs=[HBM_SPEC] * n, out_specs=[HBM_SPEC] * n, out_shape=[SDS((4,) + s.shape, s.dtype) for s in shards],
                  scratch_shapes=_dma_sems(6 * n), name=name)(*shards)


def _swap_halves_multi(name, slots):
    n = len(slots)

    def body(*refs):
        ins, outs = refs[:n], refs[n:2 * n]
        send_sems, recv_sems = refs[2 * n:]
        x, y, c, _ = _position()
        cps = []
        for i in range(n):
            rh = ins[i].shape[1] // 2
            ohalf = pl.ds(pl.multiple_of((1 - c) * rh, 8), rh)
            cp = _remote(ins[i].at[:, ohalf, :], outs[i], send_sems, recv_sems, i, (x, y, 1 - c))
            cp.start()
            cps.append(cp)
        for cp in cps:
            cp.wait()

    return _pcall(body, in_specs=[HBM_SPEC] * n, out_specs=[HBM_SPEC] * n,
                  out_shape=[SDS((4, s.shape[1] // 2, s.shape[2]), s.dtype) for s in slots], scratch_shapes=_dma_sems(n), name=name)(*slots)


def _pair_add(name, g4, b1, c):
    _, R, W = g4.shape
    tr = _pick(R // 2, (256, 128, 32, 16))
    nblk = (R // 2) // tr

    def body(c_ref, g_ref, b_ref, o_ref):
        o_ref[...] = (g_ref[...] + b_ref[...]).astype(o_ref.dtype)

    grid_spec = pltpu.PrefetchScalarGridSpec(
        num_scalar_prefetch=1, grid=(4, nblk),
        in_specs=[pl.BlockSpec((None, tr, W), lambda s, i, c_ref: (s, c_ref[0] * nblk + i, 0)),
                  pl.BlockSpec((None, tr, W), lambda s, i, c_ref: (s, i, 0))],
        out_specs=pl.BlockSpec((None, tr, W), lambda s, i, c_ref: (s, i, 0)))
    return _pcall(body, grid_spec=grid_spec, out_shape=SDS((4, R // 2, W), bf16), name=name)(c.reshape(1), g4, b1)


def _scatter_chips_multi(name, ps):
    n = len(ps)

    def body(*refs):
        ins, outs = refs[:n], refs[n:2 * n]
        send_sems, recv_sems = refs[2 * n:]
        x, y, c, chips = _position()
        me = 2 * x + y
        sends = []
        for i in range(n):
            for j, (cx, cy) in enumerate(chips):
                cp = _remote(ins[i].at[2 * cx + cy], outs[i].at[me], send_sems, recv_sems, 3 * i + j, (cx, cy, c))
                cp.start()
                sends.append(cp)
        for i in range(n):
            for j, (cx, cy) in enumerate(chips):
                slot = outs[i].at[2 * cx + cy]
                _remote(slot, slot, send_sems, recv_sems, 3 * i + j, (cx, cy, c)).wait_recv()
        for cp in sends:
            cp.wait_send()

    return _pcall(body, in_specs=[HBM_SPEC] * n, out_specs=[HBM_SPEC] * n, out_shape=[SDS(p.shape, p.dtype) for p in ps],
                  scratch_shapes=_dma_sems(3 * n), name=name)(*ps)


def _sum_chips(name, p4, b2, chip, c):
    _, Rh, W = p4.shape
    tr = _pick(Rh, (256, 128, 32, 16))
    nblk = Rh // tr

    def body(m_ref, c_ref, own_ref, r1_ref, r2_ref, r3_ref, o_ref):
        o_ref[...] = ((own_ref[...].astype(f32) + r1_ref[...].astype(f32)) + r2_ref[...].astype(f32)) + r3_ref[...].astype(f32)

    other = lambda k: pl.BlockSpec((None, tr, W), lambda i, m_ref, c_ref: (m_ref[0] ^ k, i, 0))
    grid_spec = pltpu.PrefetchScalarGridSpec(
        num_scalar_prefetch=2, grid=(nblk,),
        in_specs=[pl.BlockSpec((None, tr, W), lambda i, m_ref, c_ref: (m_ref[0], i, 0)), other(1), other(2), other(3)],
        out_specs=pl.BlockSpec((tr, W), lambda i, m_ref, c_ref: (c_ref[0] * nblk + i, 0)))
    return _pcall(body, grid_spec=grid_spec, out_shape=SDS((2 * Rh, W), f32), name=name)(chip.reshape(1), c.reshape(1), p4, b2, b2, b2)


def _join_halves(name, fulls):
    n = len(fulls)

    def body(*refs):
        outs = refs[n:2 * n]
        send_sems, recv_sems = refs[2 * n:]
        x, y, c, _ = _position()
        cps = []
        for i in range(n):
            rh = outs[i].shape[0] // 2
            mine = outs[i].at[pl.ds(pl.multiple_of(c * rh, 8), rh)]
            theirs = outs[i].at[pl.ds(pl.multiple_of((1 - c) * rh, 8), rh)]
            cp = _remote(mine, mine, send_sems, recv_sems, i, (x, y, 1 - c))
            cp.start()
            cps.append((cp, _remote(theirs, theirs, send_sems, recv_sems, i, (x, y, 1 - c))))
        for cp, back in cps:
            back.wait_recv()
            cp.wait_send()

    return _pcall(body, in_specs=[HBM_SPEC] * n, out_specs=[HBM_SPEC] * n, out_shape=[SDS(r.shape, r.dtype) for r in fulls],
                  input_output_aliases={i: i for i in range(n)}, scratch_shapes=_dma_sems(n), name=name)(*fulls)


def _hosted_gather(shards):
    n = len(shards)

    def half(ref_rows, c):
        rh = ref_rows // 2
        return pl.ds(pl.multiple_of(c * rh, 16), rh)

    def start(ins, outs, send_sems, recv_sems):
        x, y, c, chips = _position()
        me = 2 * x + y
        for i in range(n):
            rows = half(ins[i].shape[0], c)
            for j, (cx, cy) in enumerate(chips):
                _remote(ins[i].at[rows], outs[i].at[me, rows], send_sems, recv_sems, 3 * i + j, (cx, cy, c)).start()

    def finish(ins, outs, send_sems, recv_sems):
        x, y, c, chips = _position()
        me = 2 * x + y
        for i in range(n):
            rows = half(ins[i].shape[0], c)
            for j, (cx, cy) in enumerate(chips):
                _remote(ins[i].at[rows], outs[i].at[2 * cx + cy, rows], send_sems, recv_sems, 3 * i + j, (cx, cy, c)).wait_recv()
        for i in range(n):
            rows = half(ins[i].shape[0], c)
            for j, (cx, cy) in enumerate(chips):
                _remote(ins[i].at[rows], outs[i].at[me, rows], send_sems, recv_sems, 3 * i + j, (cx, cy, c)).wait_send()

    return {"arrays": shards, "out_shape": [SDS((4,) + s.shape, s.dtype) for s in shards], "n_sems": 3 * n, "start": start, "finish": finish}


def _gather_forward(name, gathered):
    n = len(gathered)

    def body(*refs):
        outs = refs[n:2 * n]
        send_sems, recv_sems = refs[2 * n:]
        x, y, c, chips = _position()
        sib = (x, y, 1 - c)
        sends = []
        for i in range(n):
            rh = outs[i].shape[1] // 2
            mine = pl.ds(pl.multiple_of(c * rh, 16), rh)
            for j, (cx, cy) in enumerate(chips):
                rows = outs[i].at[2 * cx + cy, mine]
                cp = _remote(rows, rows, send_sems, recv_sems, 3 * i + j, sib)
                cp.start()
                sends.append(cp)
        for i in range(n):
            rh = outs[i].shape[1] // 2
            theirs = pl.ds(pl.multiple_of((1 - c) * rh, 16), rh)
            for j, (cx, cy) in enumerate(chips):
                rows = outs[i].at[2 * cx + cy, theirs]
                _remote(rows, rows, send_sems, recv_sems, 3 * i + j, sib).wait_recv()
        for cp in sends:
            cp.wait_send()

    return _pcall(body, in_specs=[HBM_SPEC] * n, out_specs=[HBM_SPEC] * n, out_shape=[SDS(g.shape, g.dtype) for g in gathered],
                  input_output_aliases={i: i for i in range(n)}, scratch_shapes=_dma_sems(3 * n), name=name)(*gathered)


def _hosted_scatter(ps):
    n = len(ps)

    def start(ins, outs, send_sems, recv_sems):
        x, y, c, chips = _position()
        me = 2 * x + y
        for i in range(n):
            for j, (cx, cy) in enumerate(chips):
                _remote(ins[i].at[2 * cx + cy], outs[i].at[me], send_sems, recv_sems, 3 * i + j, (cx, cy, c)).start()

    def finish(ins, outs, send_sems, recv_sems):
        x, y, c, chips = _position()
        me = 2 * x + y
        for i in range(n):
            for j, (cx, cy) in enumerate(chips):
                slot = outs[i].at[2 * cx + cy]
                _remote(slot, slot, send_sems, recv_sems, 3 * i + j, (cx, cy, c)).wait_recv()
        for i in range(n):
            for j, (cx, cy) in enumerate(chips):
                _remote(ins[i].at[2 * cx + cy], outs[i].at[me], send_sems, recv_sems, 3 * i + j, (cx, cy, c)).wait_send()

    return {"arrays": ps, "out_shape": [SDS(p.shape, p.dtype) for p in ps], "n_sems": 3 * n, "start": start, "finish": finish}


def _hosted_swap(slots):
    n = len(slots)

    def copies(ins, outs, send_sems, recv_sems):
        x, y, c, _ = _position()
        cps = []
        for i in range(n):
            rh = ins[i].shape[1] // 2
            ohalf = pl.ds(pl.multiple_of((1 - c) * rh, 8), rh)
            cps.append(_remote(ins[i].at[:, ohalf, :], outs[i], send_sems, recv_sems, i, (x, y, 1 - c)))
        return cps

    def start(ins, outs, send_sems, recv_sems):
        for cp in copies(ins, outs, send_sems, recv_sems):
            cp.start()

    def finish(ins, outs, send_sems, recv_sems):
        for cp in copies(ins, outs, send_sems, recv_sems):
            cp.wait()

    return {"arrays": slots, "out_shape": [SDS((4, s.shape[1] // 2, s.shape[2]), s.dtype) for s in slots], "n_sems": n,
            "start": start, "finish": finish}


def _rs_begin(slots, c):
    b1 = _swap_halves_multi("rs_swap", slots)
    return [_pair_add("rs_pair_%d" % i, g, b, c) for i, (g, b) in enumerate(zip(slots, b1))]


def _rs_end(ps, b2, c, chip):
    return _join_halves("rs_join", [_sum_chips("rs_sum_%d" % i, p, b, chip, c) for i, (p, b) in enumerate(zip(ps, b2))])


def _reduce_scatter_multi(slots, c, chip):
    ps = _rs_begin(slots, c)
    return _rs_end(ps, _scatter_chips_multi("rs_scatter", ps), c, chip)


_BIG = ("w_in", "w_dn_out", "w_swa_out", "w_o", "w_up", "w_down")


_W_IN_PIECES = ((0, 3072, 0), (3072, 4096, 3072), (5392, 6416, 4096), (6416, 7440, 5120), (4112, 5136, 6144), (4096, 4112, 7168),
                (5136, 5264, 7296), (5264, 5392, 7424))
_W_IN_SHARD = IN_TOTAL // 4


def _w_in_from_slots(g):
    parts, at = [], 0
    for lo, hi, dst in _W_IN_PIECES:
        if dst > at:
            parts.append(jnp.zeros((g.shape[1], dst - at), g.dtype))
        for s in range(4):
            a, b = max(lo, s * _W_IN_SHARD), min(hi, (s + 1) * _W_IN_SHARD)
            if a < b:
                parts.append(g[s][:, a - s * _W_IN_SHARD:b - s * _W_IN_SHARD])
        at = dst + hi - lo
    parts.append(jnp.zeros((g.shape[1], PROJ_W - at), g.dtype))
    return jnp.concatenate(parts, axis=1)


def _w_in_to_slots(gw):
    slots = []
    for s in range(4):
        parts = []
        for lo, hi, dst in sorted(_W_IN_PIECES):
            a, b = max(lo, s * _W_IN_SHARD), min(hi, (s + 1) * _W_IN_SHARD)
            if a < b:
                parts.append(gw[:, dst + a - lo:dst + b - lo])
        slots.append(jnp.concatenate(parts, axis=1))
    return jnp.stack(slots)


def _assemble_mixer(gs):
    rows = lambda g: g.reshape(4 * g.shape[1], g.shape[2])
    return {"w_in": _w_in_from_slots(gs[0]), "w_dn_out": rows(gs[1]), "w_swa_out": rows(gs[2]), "w_o": rows(gs[3])}


def _grad_slots(gw):
    rows = lambda g: g.reshape(4, g.shape[0] // 4, g.shape[1])
    return [_w_in_to_slots(gw["w_in"]), rows(gw["w_dn_out"]), rows(gw["w_swa_out"]), rows(gw["w_o"]), gw["w_up"], rows(gw["w_down"])]


def _pad_lanes(v, n=LANE):
    return jnp.pad(v, (0, n - v.shape[0])).reshape(1, n)


def _layer_consts(P):
    K = {}
    K["norm_mix"] = P["norm_mix"].reshape(1, D)
    K["norm_ffn"] = P["norm_ffn"].reshape(1, D)
    K["alog"] = _pad_lanes(P["dn_a_log"])
    K["dtb"] = _pad_lanes(P["dn_dt_bias"])
    K["dn_norm"] = jnp.tile(P["dn_norm"], 8).reshape(1, D)
    K["qn"] = jnp.tile(P["swa_q_norm"], 16).reshape(1, D)
    K["kn"] = jnp.tile(P["swa_k_norm"], 2).reshape(1, LANE)
    K["sinks"] = _pad_lanes(P["swa_sinks"])
    K["ffn_b"] = P["ffn_conv_b"].reshape(1, D_FF)
    return K


def _layer_fwd(x, mod, W, K, tabs, bd, hosted=None, late=None):
    late = late or {}
    sh1, sc1, gt1, sh2, sc2, gt2 = mod
    S = {"x": x}
    h1, h1t = _rowwise_fwd("normmod1_fwd", _normmod_fn, [(x, 0, D)], [K["norm_mix"], sc1, sh1], [D], [bf16], also_transposed=True)
    if "proj" in late:
        proj, arrived = _matmul("proj_fwd", h1, W["w_in"], "nn", f32, hosted=late["proj"][0])
        W.update(late["proj"][1](arrived))
    else:
        proj = _matmul("proj_fwd", h1, W["w_in"], "nn", f32)
    qn = _dnconv_fwd("dnconv_q_fwd", proj, CB_Q, W["dn_conv"], True)
    kn = _dnconv_fwd("dnconv_k_fwd", proj, CB_K, W["dn_conv"], True)
    vc = _dnconv_fwd("dnconv_v_fwd", proj, CB_V, W["dn_conv"], False)
    o, sall, xinv, hosted_out = _gdn_fwd("gdn_fwd", qn, kn, vc, proj, K["alog"], K["dtb"], hosted=hosted)
    (on,) = _rowwise_fwd("dngate_fwd", _dngate_fn, [(o, 0, D), (proj, WB_Z, D)], [K["dn_norm"]], [D], [bf16])
    ya = _matmul("dnout_fwd", on, W["w_dn_out"], "nn", f32)
    sq = _qkprep_fwd("qprep_fwd", proj, WB_SWQ, D, K["qn"], bd[0], tabs[0])
    sk = _qkprep_fwd("kprep_fwd", proj, CB_SWK, LANE, K["kn"], bd[1], tabs[1])
    attn, arrived = _attn_fwd("attn_fwd", sq, sk, proj, K["sinks"], K["attn_bias"], hosted=late["attn"][0] if "attn" in late else None)
    if "attn" in late:
        W.update(late["attn"][1](arrived))
    yb = _matmul("swaout_fwd", attn, W["w_swa_out"], "nn", f32)
    (merged,) = _rowwise_fwd("merge_fwd", _merge_fn, [(proj, WB_GA, D), (proj, WB_GB, D), (ya, 0, D), (yb, 0, D)], [], [D], [bf16])
    t1, x1 = _matmul("wo_fwd", merged, W["w_o"], "nn", f32, resid=(x, gt1))
    h2, h2t = _rowwise_fwd("normmod2_fwd", _normmod_fn, [(x1, 0, D)], [K["norm_ffn"], sc2, sh2], [D], [bf16], also_transposed=True)
    up = _matmul("up_fwd", h2, W["w_up"], "nn", f32, b_slots=True)
    mid = _ffnact_fwd("ffnact_fwd", up, W["ffn_conv"], K["ffn_b"])
    t2, x2 = _matmul("down_fwd", mid, W["w_down"], "nn", f32, resid=(x1, gt2))
    S.update(h1t=h1t, h2t=h2t, proj=proj, qn=qn, kn=kn, vc=vc, o=o, sall=sall, xinv=xinv, on=on, ya=ya, sq=sq, sk=sk, attn=attn, yb=yb,
             merged=merged, t1=t1, x1=x1, h2=h2, up=up, mid=mid, t2=t2)
    return x2, S, hosted_out


def _layer_bwd(dx2, S, mod, W, K, tabs, bd, carry=None, early=None):
    sh1, sc1, gt1, sh2, sc2, gt2 = mod
    x, x1, proj, up = S["x"], S["x1"], S["proj"], S["up"]
    T = x.shape[0]
    gw, gs = {}, {}
    dt2, dgt2 = _rowwise_bwd("resid2_bwd", _resid_fn, [(x1, 0, D), (S["t2"], 0, D)], [gt2], [(dx2, 0, D)], [None, bf16])
    dmid = _matmul("down_bwd_x", dt2, W["w_down"], "nt", bf16)
    gw["w_down"] = _matmul("down_bwd_w", S["mid"], dt2, "tn", f32)
    dact, dlin, gw["ffn_conv"], dffn_b = _ffnact_bwd("ffnact_bwd", up, W["ffn_conv"], K["ffn_b"], dmid)
    dup = (dact, dlin)
    dh2 = _matmul("up_bwd_x", dup, W["w_up"], "nt", f32, b_slots=True)
    if carry is None:
        gw["w_up"] = _matmul("up_bwd_w", S["h2t"], dup, "nn", f32, out_slots=4)
        pair_sums = hosted = None
    else:
        gw["w_up"], b1 = _matmul("up_bwd_w", S["h2t"], dup, "nn", f32, out_slots=4, hosted=_hosted_swap(carry[0]))
        pair_sums = [_pair_add("rs_pair_%d" % i, g, b, carry[1]) for i, (g, b) in enumerate(zip(carry[0], b1))]
        hosted = _hosted_scatter(pair_sums)
    dx1, dnorm_ffn, dsc2, dsh2 = _rowwise_bwd("normmod2_bwd", _normmod_fn, [(x1, 0, D)], [K["norm_ffn"], sc2, sh2], [(dh2, 0, D)], [f32],
                                              add_to_first=(dx2, 0, D))
    early_out = None
    if early is not None:
        ffn_slots = [gw["w_up"], gw["w_down"].reshape(4, D_FF // 4, D)]
        ffn_ps = [_pair_add("rs_pair_ffn_%d" % i, g, b, early[0]) for i, (g, b) in enumerate(zip(ffn_slots, _swap_halves_multi("rs_swap_ffn", ffn_slots)))]
    dt1, dgt1 = _rowwise_bwd("resid1_bwd", _resid_fn, [(x, 0, D), (S["t1"], 0, D)], [gt1], [(dx1, 0, D)], [None, bf16])
    dmerged = _matmul("wo_bwd_x", dt1, W["w_o"], "nt", f32)
    gw["w_o"] = _matmul("wo_bwd_w", S["merged"], dt1, "tn", f32)
    dproj, dya, dyb = _rowwise_bwd("merge_bwd", _merge_fn, [(proj, WB_GA, D), (proj, WB_GB, D), (S["ya"], 0, D), (S["yb"], 0, D)], [],
                                   [(dmerged, 0, D)], [bf16, bf16, bf16, bf16], dest=(None, (0, 1), WB_GA // 2))
    don = _matmul("dnout_bwd_x", dya, W["w_dn_out"], "nt", f32)
    gw["w_dn_out"] = _matmul("dnout_bwd_w", S["on"], dya, "tn", f32)
    dproj, do, ddn_norm = _rowwise_bwd("dngate_bwd", _dngate_fn, [(S["o"], 0, D), (proj, WB_Z, D)], [K["dn_norm"]], [(don, 0, D)],
                                       [f32, bf16], dest=(dproj, (1,), WB_Z))
    (dqn, dkn, dvc, dab, dalog, ddtb), hosted_out = _gdn_bwd("gdn_bwd", S["qn"], S["kn"], S["vc"], proj, K["alog"], K["dtb"], S["sall"], S["xinv"], do,
                                                            hosted=hosted)
    dproj, dwq = _dnconv_bwd("dnconv_q_bwd", proj, CB_Q, W["dn_conv"], dqn, True, dproj)
    dproj, dwk = _dnconv_bwd("dnconv_k_bwd", proj, CB_K, W["dn_conv"], dkn, True, dproj)
    dproj, dwv = _dnconv_bwd("dnconv_v_bwd", proj, CB_V, W["dn_conv"], dvc, False, dproj)
    gw["dn_conv"] = jnp.concatenate([dwq, dwk, dwv], axis=1)
    dattn = _matmul("swaout_bwd_x", dyb, W["w_swa_out"], "nt", f32)
    gw["w_swa_out"] = _matmul("swaout_bwd_w", S["attn"], dyb, "tn", f32)
    (dsq, dkp, dkc, dvp, dvc_, dsinks), ffn_b2 = _attn_bwd("attn_bwd", S["sq"], S["sk"], proj, K["sinks"], K["attn_bias"], dattn,
                                                          hosted=None if early is None else _hosted_scatter(ffn_ps))
    if early is not None:
        early_out = _rs_end(ffn_ps, ffn_b2, early[0], early[1])
    dsk = _shift_add("attn_dk_join", dkc, dkp, f32)
    dswv = _shift_add("attn_dv_join", dvc_, dvp, bf16)
    dproj, dqn_w = _qkprep_bwd("qprep_bwd", proj, WB_SWQ, D, K["qn"], bd[0], tabs[0], dsq, dest_buf=dproj)
    dswk, dkn_w = _qkprep_bwd("kprep_bwd", proj, CB_SWK, LANE, K["kn"], bd[1], tabs[1], dsk)
    tail = jnp.concatenate([dab.astype(bf16), dswk, dswv, jnp.zeros((T, LANE), bf16)], axis=1)
    dproj = lax.dynamic_update_slice(dproj, tail, (0, CB_AB * LANE))
    dh1 = _matmul("proj_bwd_x", dproj, W["w_in"], "nt", f32)
    gw["w_in"] = _matmul("proj_bwd_w", S["h1t"], dproj, "nn", f32)
    dx, dnorm_mix, dsc1, dsh1 = _rowwise_bwd("normmod1_bwd", _normmod_fn, [(x, 0, D)], [K["norm_mix"], sc1, sh1], [(dh1, 0, D)], [f32],
                                             add_to_first=(dx1, 0, D))
    gs = {"norm_mix": dnorm_mix[0], "dn_a_log": dalog[0, :8], "dn_dt_bias": ddtb[0, :8], "dn_norm": ddn_norm.reshape(8, LANE).sum(0),
          "swa_q_norm": dqn_w.reshape(16, 64).sum(0), "swa_k_norm": dkn_w.reshape(2, 64).sum(0), "swa_sinks": dsinks[0, :16],
          "norm_ffn": dnorm_ffn[0], "ffn_conv_b": dffn_b[0]}
    dmod = jnp.concatenate([dsh1, dsc1, dgt1, dsh2, dsc2, dgt2], axis=1)
    return dx, gw, gs, dmod, (pair_sums, hosted_out), early_out


def _rope_tables(pos):
    T = pos.shape[0]
    half = 8
    inv = jnp.power(ROPE_THETA, -jnp.arange(half, dtype=f32) / half)
    ang = pos.astype(f32)[:, None] * inv
    cos, sin = jnp.cos(ang), jnp.sin(ang)
    z8, z48, o48 = jnp.zeros((T, 8), f32), jnp.zeros((T, 48), f32), jnp.ones((T, 48), f32)
    c64 = jnp.concatenate([cos, cos, o48], axis=1)
    s1 = jnp.concatenate([-sin, z8, z48], axis=1)
    s2 = jnp.concatenate([z8, sin, z48], axis=1)
    return tuple(jnp.tile(t, (1, 2)) for t in (c64, s1, s2))


_SMALL = (("norm_mix", D), ("dn_a_log", 8), ("dn_dt_bias", 8), ("dn_norm", 128), ("swa_q_norm", 64), ("swa_k_norm", 64),
          ("swa_sinks", 16), ("norm_ffn", D), ("ffn_conv_b", D_FF), ("b_ada", 6 * D))
_CONV = (("dn_conv", 4 * 3072), ("ffn_conv", 3 * D_FF))
_CONV_SHARD = (("dn_conv", 4 * 768), ("ffn_conv", 3 * 704))


def _pack_small(vals, spec):
    flat = jnp.concatenate([vals[nm].reshape(-1) for nm, _ in spec])
    rows = -(-flat.shape[0] // (8 * LANE)) * 8
    return jnp.pad(flat, (0, rows * LANE - flat.shape[0])).reshape(rows, LANE)


def _unpack_small(buf, spec):
    flat = buf.reshape(-1)
    out, off = {}, 0
    for nm, n in spec:
        out[nm] = flat[off:off + DEPTH * n].reshape(DEPTH, n)
        off += DEPTH * n
    return out


def kernel(x, c, positions, w_ada, b_ada, norm_mix, w_in, dn_conv, dn_a_log, dn_dt_bias, dn_norm, w_dn_out, swa_q_norm, swa_k_norm, swa_sinks, w_swa_out, w_o, norm_ffn, w_up, ffn_conv, ffn_conv_b, w_down, loss_target, m_w_ada, m_b_ada, m_norm_mix, m_w_in, m_dn_conv, m_dn_a_log, m_dn_dt_bias, m_dn_norm, m_w_dn_out, m_swa_q_norm, m_swa_k_norm, m_swa_sinks, m_w_swa_out, m_w_o, m_norm_ffn, m_w_up, m_ffn_conv, m_ffn_conv_b, m_w_down, v_w_ada, v_b_ada, v_norm_mix, v_w_in, v_dn_conv, v_dn_a_log, v_dn_dt_bias, v_dn_norm, v_w_dn_out, v_swa_q_norm, v_swa_k_norm, v_swa_sinks, v_w_swa_out, v_w_o, v_norm_ffn, v_w_up, v_ffn_conv, v_ffn_conv_b, v_w_down):
    weights = dict(w_ada=w_ada, b_ada=b_ada, norm_mix=norm_mix, w_in=w_in, dn_conv=dn_conv, dn_a_log=dn_a_log, dn_dt_bias=dn_dt_bias,
                   dn_norm=dn_norm, w_dn_out=w_dn_out, swa_q_norm=swa_q_norm, swa_k_norm=swa_k_norm, swa_sinks=swa_sinks,
                   w_swa_out=w_swa_out, w_o=w_o, norm_ffn=norm_ffn, w_up=w_up, ffn_conv=ffn_conv, ffn_conv_b=ffn_conv_b, w_down=w_down)
    mom_m = dict(w_ada=m_w_ada, b_ada=m_b_ada, norm_mix=m_norm_mix, w_in=m_w_in, dn_conv=m_dn_conv, dn_a_log=m_dn_a_log,
                 dn_dt_bias=m_dn_dt_bias, dn_norm=m_dn_norm, w_dn_out=m_w_dn_out, swa_q_norm=m_swa_q_norm, swa_k_norm=m_swa_k_norm,
                 swa_sinks=m_swa_sinks, w_swa_out=m_w_swa_out, w_o=m_w_o, norm_ffn=m_norm_ffn, w_up=m_w_up, ffn_conv=m_ffn_conv,
                 ffn_conv_b=m_ffn_conv_b, w_down=m_w_down)
    mom_v = dict(w_ada=v_w_ada, b_ada=v_b_ada, norm_mix=v_norm_mix, w_in=v_w_in, dn_conv=v_dn_conv, dn_a_log=v_dn_a_log,
                 dn_dt_bias=v_dn_dt_bias, dn_norm=v_dn_norm, w_dn_out=v_w_dn_out, swa_q_norm=v_swa_q_norm, swa_k_norm=v_swa_k_norm,
                 swa_sinks=v_swa_sinks, w_swa_out=v_w_swa_out, w_o=v_w_o, norm_ffn=v_norm_ffn, w_up=v_w_up, ffn_conv=v_ffn_conv,
                 ffn_conv_b=v_ffn_conv_b, w_down=v_w_down)
    order = ["w_ada", "b_ada", "norm_mix", "w_in", "dn_conv", "dn_a_log", "dn_dt_bias", "dn_norm", "w_dn_out", "swa_q_norm",
             "swa_k_norm", "swa_sinks", "w_swa_out", "w_o", "norm_ffn", "w_up", "ffn_conv", "ffn_conv_b", "w_down"]
    ax, ay, ac = lax.axis_index("x"), lax.axis_index("y"), lax.axis_index("c")
    chip = 2 * ax + ay
    dev = 4 * ax + 2 * ay + ac
    T = x.shape[1]
    xs = x[0]

    c_all = _allgather_all("gather_c", jnp.pad(c, ((0, 7), (0, 0)))).reshape(8, 8, D)[:, 0]
    c_act = _silu_rows("silu_c", jnp.pad(c_all, ((0, 8), (0, 0))))
    mod_sh = jnp.stack([
        _matmul("mod_fwd", c_act, w_ada[l].astype(bf16), "nn", f32,
                bias=lax.dynamic_slice(b_ada[l], (chip * 1536,), (1536,)).reshape(1, 1536)) for l in range(DEPTH)])
    mod_all = _allgather_all("gather_mod", mod_sh.reshape(DEPTH * 16 * 12, LANE)).reshape(8, DEPTH, 16, 1536)
    mod_me = jnp.concatenate([lax.dynamic_index_in_dim(mod_all[2 * s], dev, axis=1, keepdims=False) for s in range(4)], axis=1)

    tabs_q = _rope_tables(positions[0])
    tabs = (tabs_q, tabs_q)
    head = jnp.arange(LANE) // 64
    bd128 = (head[:, None] == head[None, :]).astype(f32) / 64.0
    bd = (bd128, bd128)
    attn_bias = _attn_bias()

    conv_all = _allgather_all("gather_conv", _pack_small({"dn_conv": dn_conv, "ffn_conv": ffn_conv}, _CONV_SHARD))
    conv_parts = [_unpack_small(conv_all[2 * s], _CONV_SHARD) for s in range(4)]
    dn_conv_full = jnp.concatenate([p["dn_conv"].reshape(DEPTH, 4, 768) for p in conv_parts], axis=2)
    ffn_conv_full = jnp.concatenate([p["ffn_conv"].reshape(DEPTH, 3, 704) for p in conv_parts], axis=2)

    saved, Ws, Ks, mods = [], [], [], []
    h = xs
    shards = [[weights[nm][l].astype(bf16) for nm in _BIG] for l in range(DEPTH)]
    rest = (0, 1, 2, 3, 5)
    rows = lambda g: g.reshape(4 * g.shape[1], g.shape[2])

    def arrive(name, arrived, l, idx):
        got = _gather_forward(name, arrived)
        return [lax.dynamic_update_index_in_dim(g, shards[l][i], chip, 0) for g, i in zip(got, idx)]

    def up_late(l):
        return (_hosted_gather([shards[l][4]]), lambda arrived: {"w_up": arrive("gather_up_pass", arrived, l, (4,))[0]})

    got = [lax.dynamic_update_index_in_dim(g, shards[0][i], chip, 0) for i, g in enumerate(_gather_chips("gather_w", shards[0][:4]))]
    for l in range(DEPTH):
        W = _assemble_mixer(got[:4])
        late = {"proj": up_late(l)}
        if l == 0:
            late["attn"] = (_hosted_gather([shards[0][5]]), lambda arrived: {"w_down": rows(arrive("gather_down_pass", arrived, 0, (5,))[0])})
        else:
            W["w_down"] = rows(got[4])
        W["dn_conv"], W["ffn_conv"] = dn_conv_full[l], ffn_conv_full[l]
        K = _layer_consts({nm: weights[nm][l] for nm in ("norm_mix", "norm_ffn", "dn_a_log", "dn_dt_bias", "dn_norm", "swa_q_norm",
                                                          "swa_k_norm", "swa_sinks", "ffn_conv_b")})
        K["attn_bias"] = attn_bias
        mod = tuple(mod_me[l, k * D:(k + 1) * D].reshape(1, D) for k in range(6))
        nxt = _hosted_gather([shards[l + 1][i] for i in rest]) if l + 1 < DEPTH else None
        h, S, arrived = _layer_fwd(h, mod, W, K, tabs, bd, hosted=nxt, late=late)
        if nxt is not None:
            got = arrive("gather_w_pass", arrived, l + 1, rest)
        saved.append(S), Ws.append(W), Ks.append(K), mods.append(mod)

    loss_blk, dh = _loss("loss", h, loss_target[0])
    loss = lax.psum(loss_blk[0, 0], ("x", "y", "c"))

    grad_sh = [None] * DEPTH
    small = [None] * DEPTH
    dmods = [None] * DEPTH
    slots = None
    for l in reversed(range(DEPTH)):
        dh, gw, gs, dmod, (ps, b2), ffn0 = _layer_bwd(dh, saved[l], mods[l], Ws[l], Ks[l], tabs, bd, carry=None if slots is None else (slots, ac),
                                                      early=(ac, chip) if l == 0 else None)
        if slots is not None:
            grad_sh[l + 1] = dict(zip(_BIG, _rs_end(ps, b2, ac, chip)))
        slots = _grad_slots(gw)
        small[l], dmods[l] = dict(gs, dn_conv=gw["dn_conv"], ffn_conv=gw["ffn_conv"]), dmod[0]
    grad_sh[0] = dict(zip(_BIG, _reduce_scatter_multi(slots[:4], ac, chip) + ffn0))

    spec_g = _SMALL + _CONV
    vals = {nm: jnp.stack([small[l][nm] for l in range(DEPTH)]) for nm, _ in spec_g if nm != "b_ada"}
    vals["b_ada"] = jnp.stack(dmods)
    small_all = _allgather_all("gather_small", _pack_small(vals, spec_g))
    g_small = _unpack_small(_sum_leading("sum_small", small_all), spec_g)
    dmod_all = jnp.stack([_unpack_small(small_all[d], spec_g)["b_ada"] for d in range(8)])
    dmod_sh = lax.dynamic_slice(dmod_all, (0, 0, chip * 1536), (8, DEPTH, 1536))
    dmod_sh = jnp.pad(dmod_sh, ((0, 8), (0, 0), (0, 0))).astype(bf16)
    g_w_ada = jnp.stack([_matmul("mod_bwd_w", c_act, dmod_sh[:, l], "tn", f32) for l in range(DEPTH)])

    grads = {nm: g_small[nm] for nm, _ in _SMALL}
    grads["dn_conv"] = lax.dynamic_slice(g_small["dn_conv"].reshape(DEPTH, 4, 3072), (0, 0, chip * 768), (DEPTH, 4, 768))
    grads["ffn_conv"] = lax.dynamic_slice(g_small["ffn_conv"].reshape(DEPTH, 3, D_FF), (0, 0, chip * 704), (DEPTH, 3, 704))
    grads["w_ada"] = g_w_ada
    for nm in _BIG:
        grads[nm] = jnp.stack([grad_sh[l][nm] for l in range(DEPTH)])

    delta, new_m, new_v = {}, {}, {}
    for nm in ("w_ada", "dn_conv", "ffn_conv") + _BIG:
        delta[nm], new_m[nm], new_v[nm] = _adamw("adamw_" + nm, weights[nm], grads[nm], mom_m[nm], mom_v[nm])
    sm = [_pack_small({nm: t[nm] for nm, _ in _SMALL}, _SMALL) for t in (weights, grads, mom_m, mom_v)]
    for tgt, buf in zip((delta, new_m, new_v), _adamw("adamw_small", *sm)):
        tgt.update(_unpack_small(buf, _SMALL))

    return (loss, dh[None], *[grads[n] for n in order], *[delta[n] for n in order], *[new_m[n] for n in order], *[new_v[n] for n in order])
```
